```python
import jax, jax.numpy as jnp
from jax import lax
import numpy as np

D_MODEL = 1024
BATCH = 16
SEQ = 2048
DEPTH = 2

N_MIXERS = 2
N_A_LAYERS = (DEPTH + 1) // 2
N_B_LAYERS = DEPTH // 2
CONV_KERNEL = 31
POOL_WINDOWS = (2, 4, 8, 16)
N_POOL_GROUPS = len(POOL_WINDOWS)
POOL_GROUP_DIM = D_MODEL // N_POOL_GROUPS
D_FF = ((8 * D_MODEL // 3 + 127) // 128) * 128
FFN_CONV_KERNEL = 3
RMS_EPS = 1e-6
LN_EPS = 1e-5

kernel_name = "hybrid_conformerconv_msnpool_convffn"


def rmsnorm(x, g):
    xf = x.astype(jnp.float32)
    y = xf * lax.rsqrt(jnp.mean(xf * xf, axis=-1, keepdims=True) + RMS_EPS)
    return (y * g.astype(jnp.float32)).astype(x.dtype)


def layernorm(x, g, b):
    xf = x.astype(jnp.float32)
    mu = jnp.mean(xf, axis=-1, keepdims=True)
    var = jnp.mean(jnp.square(xf - mu), axis=-1, keepdims=True)
    y = (xf - mu) * lax.rsqrt(var + LN_EPS)
    return (y * g.astype(jnp.float32) + b.astype(jnp.float32)).astype(x.dtype)


def causal_dwconv(x, w, b):
    k, c = w.shape
    y = lax.conv_general_dilated(
        x, w[:, None, :].astype(x.dtype),
        window_strides=(1,), padding=((k - 1, 0),),
        dimension_numbers=("NWC", "WIO", "NWC"),
        feature_group_count=c)
    return y + b


def conformer_conv_module(h, w_pw1, b_pw1, w_dw, b_dw, ln_g, ln_b, w_pw2, b_pw2):
    d = h.shape[-1]
    a = jnp.einsum("bsd,de->bse", h, w_pw1) + b_pw1
    u = a[..., :d] * jax.nn.sigmoid(a[..., d:])
    u = causal_dwconv(u, w_dw, b_dw)
    u = jax.nn.silu(layernorm(u, ln_g, ln_b))
    return jnp.einsum("bsd,de->bse", u, w_pw2) + b_pw2


def multiscale_pool_mixer(h, w_grp, b_grp, scale):
    bsz, s, d = h.shape
    hf = h.astype(jnp.float32)
    cs = jnp.concatenate(
        [jnp.zeros((bsz, 1, d), jnp.float32), lax.cumsum(hf, axis=1)], axis=1)
    t = jnp.arange(s)
    outs = []
    for g, w in enumerate(POOL_WINDOWS):
        sl = slice(g * POOL_GROUP_DIM, (g + 1) * POOL_GROUP_DIM)
        c = cs[..., sl]
        lagged = jnp.pad(c[:, : s + 1 - w], ((0, 0), (w - 1, 0), (0, 0)))
        win_sum = c[:, 1:] - lagged
        cnt = jnp.minimum(t + 1, w).astype(jnp.float32)[None, :, None]
        outs.append(win_sum / cnt - hf[..., sl])
    pooled = jnp.stack(outs, axis=2).astype(h.dtype)
    mixed = jnp.einsum("bsgc,gce->bsge", pooled, w_grp).reshape(bsz, s, d)
    return scale * (mixed + b_grp)


def conv_ffn(h, w_up, w_dw, b_dw, w_down):
    up = jnp.einsum("bsd,df->bsf", h, w_up)
    act, gate = up[..., :D_FF], up[..., D_FF:]
    act = causal_dwconv(act, w_dw, b_dw)
    return jnp.einsum("bsf,fd->bsd", jax.nn.silu(act) * gate, w_down)


def _fwd_setup_inputs(seed: int = 0) -> dict:
    key = jax.random.key(seed)
    ks = jax.random.split(key, 24)
    f32 = jnp.float32
    D, G, Cg = D_MODEL, N_POOL_GROUPS, POOL_GROUP_DIM
    nrm = lambda k, shape, sc: jax.random.normal(k, shape, f32) * sc
    gain = lambda k, shape: 1.0 + 0.05 * jax.random.normal(k, shape, f32)
    return {
        "x": jax.random.normal(ks[0], (BATCH, SEQ, D), f32),
        "norm_mix": gain(ks[1], (DEPTH, D)),
        "norm_ffn": gain(ks[2], (DEPTH, D)),
        "conv_w_pw1": nrm(ks[3], (N_A_LAYERS, D, 2 * D), D ** -0.5),
        "conv_b_pw1": nrm(ks[4], (N_A_LAYERS, 2 * D), 0.02),
        "conv_w_dw": nrm(ks[5], (N_A_LAYERS, CONV_KERNEL, D), CONV_KERNEL ** -0.5),
        "conv_b_dw": nrm(ks[6], (N_A_LAYERS, D), 0.02),
        "conv_ln_g": gain(ks[7], (N_A_LAYERS, D)),
        "conv_ln_b": nrm(ks[8], (N_A_LAYERS, D), 0.02),
        "conv_w_pw2": nrm(ks[9], (N_A_LAYERS, D, D), D ** -0.5),
        "conv_b_pw2": nrm(ks[10], (N_A_LAYERS, D), 0.02),
        "pool_w": nrm(ks[11], (N_B_LAYERS, G, Cg, Cg), Cg ** -0.5),
        "pool_b": nrm(ks[12], (N_B_LAYERS, D), 0.02),
        "pool_scale": gain(ks[13], (N_B_LAYERS, D)),
        "ffn_w_up": nrm(ks[14], (DEPTH, D, 2 * D_FF), D ** -0.5),
        "ffn_w_dw": nrm(ks[15], (DEPTH, FFN_CONV_KERNEL, D_FF), FFN_CONV_KERNEL ** -0.5),
        "ffn_b_dw": nrm(ks[16], (DEPTH, D_FF), 0.02),
        "ffn_w_down": nrm(ks[17], (DEPTH, D_FF, D), D_FF ** -0.5),
        "final_norm": gain(ks[18], (D,)),
    }


def _fwd_reference(x, norm_mix, norm_ffn, conv_w_pw1, conv_b_pw1, conv_w_dw, conv_b_dw,
              conv_ln_g, conv_ln_b, conv_w_pw2, conv_b_pw2, pool_w, pool_b, pool_scale,
              ffn_w_up, ffn_w_dw, ffn_b_dw, ffn_w_down, final_norm):
    ia, ib = 0, 0
    for i in range(DEPTH):
        h = rmsnorm(x, norm_mix[i])
        if i % N_MIXERS == 0:
            x = x + conformer_conv_module(
                h, conv_w_pw1[ia], conv_b_pw1[ia], conv_w_dw[ia], conv_b_dw[ia],
                conv_ln_g[ia], conv_ln_b[ia], conv_w_pw2[ia], conv_b_pw2[ia])
            ia += 1
        else:
            x = x + multiscale_pool_mixer(h, pool_w[ib], pool_b[ib], pool_scale[ib])
            ib += 1
        h = rmsnorm(x, norm_ffn[i])
        x = x + conv_ffn(h, ffn_w_up[i], ffn_w_dw[i], ffn_b_dw[i], ffn_w_down[i])
    return rmsnorm(x, final_norm)


import jax as _jax
import jax.numpy as _jnp

TWIN_FORMAT = 'train_step'
FWD_PARAMS = ['x', 'norm_mix', 'norm_ffn', 'conv_w_pw1', 'conv_b_pw1', 'conv_w_dw', 'conv_b_dw', 'conv_ln_g', 'conv_ln_b', 'conv_w_pw2', 'conv_b_pw2', 'pool_w', 'pool_b', 'pool_scale', 'ffn_w_up', 'ffn_w_dw', 'ffn_b_dw', 'ffn_w_down', 'final_norm']
TWIN_WEIGHTS = ['norm_mix', 'norm_ffn', 'conv_w_pw1', 'conv_b_pw1', 'conv_w_dw', 'conv_b_dw', 'conv_ln_g', 'conv_ln_b', 'conv_w_pw2', 'conv_b_pw2', 'pool_w', 'pool_b', 'pool_scale', 'ffn_w_up', 'ffn_w_dw', 'ffn_b_dw', 'ffn_w_down', 'final_norm']
TWIN_DIFF_INPUT = 'x'
TWIN_INPUTS = ['x', 'norm_mix', 'norm_ffn', 'conv_w_pw1', 'conv_b_pw1', 'conv_w_dw', 'conv_b_dw', 'conv_ln_g', 'conv_ln_b', 'conv_w_pw2', 'conv_b_pw2', 'pool_w', 'pool_b', 'pool_scale', 'ffn_w_up', 'ffn_w_dw', 'ffn_b_dw', 'ffn_w_down', 'final_norm', 'loss_target', 'm_norm_mix', 'm_norm_ffn', 'm_conv_w_pw1', 'm_conv_b_pw1', 'm_conv_w_dw', 'm_conv_b_dw', 'm_conv_ln_g', 'm_conv_ln_b', 'm_conv_w_pw2', 'm_conv_b_pw2', 'm_pool_w', 'm_pool_b', 'm_pool_scale', 'm_ffn_w_up', 'm_ffn_w_dw', 'm_ffn_b_dw', 'm_ffn_w_down', 'm_final_norm', 'v_norm_mix', 'v_norm_ffn', 'v_conv_w_pw1', 'v_conv_b_pw1', 'v_conv_w_dw', 'v_conv_b_dw', 'v_conv_ln_g', 'v_conv_ln_b', 'v_conv_w_pw2', 'v_conv_b_pw2', 'v_pool_w', 'v_pool_b', 'v_pool_scale', 'v_ffn_w_up', 'v_ffn_w_dw', 'v_ffn_b_dw', 'v_ffn_w_down', 'v_final_norm']
TWIN_OUTPUTS = ['loss', 'grad_x', 'grad_norm_mix', 'grad_norm_ffn', 'grad_conv_w_pw1', 'grad_conv_b_pw1', 'grad_conv_w_dw', 'grad_conv_b_dw', 'grad_conv_ln_g', 'grad_conv_ln_b', 'grad_conv_w_pw2', 'grad_conv_b_pw2', 'grad_pool_w', 'grad_pool_b', 'grad_pool_scale', 'grad_ffn_w_up', 'grad_ffn_w_dw', 'grad_ffn_b_dw', 'grad_ffn_w_down', 'grad_final_norm', 'delta_norm_mix', 'delta_norm_ffn', 'delta_conv_w_pw1', 'delta_conv_b_pw1', 'delta_conv_w_dw', 'delta_conv_b_dw', 'delta_conv_ln_g', 'delta_conv_ln_b', 'delta_conv_w_pw2', 'delta_conv_b_pw2', 'delta_pool_w', 'delta_pool_b', 'delta_pool_scale', 'delta_ffn_w_up', 'delta_ffn_w_dw', 'delta_ffn_b_dw', 'delta_ffn_w_down', 'delta_final_norm', 'new_m_norm_mix', 'new_m_norm_ffn', 'new_m_conv_w_pw1', 'new_m_conv_b_pw1', 'new_m_conv_w_dw', 'new_m_conv_b_dw', 'new_m_conv_ln_g', 'new_m_conv_ln_b', 'new_m_conv_w_pw2', 'new_m_conv_b_pw2', 'new_m_pool_w', 'new_m_pool_b', 'new_m_pool_scale', 'new_m_ffn_w_up', 'new_m_ffn_w_dw', 'new_m_ffn_b_dw', 'new_m_ffn_w_down', 'new_m_final_norm', 'new_v_norm_mix', 'new_v_norm_ffn', 'new_v_conv_w_pw1', 'new_v_conv_b_pw1', 'new_v_conv_w_dw', 'new_v_conv_b_dw', 'new_v_conv_ln_g', 'new_v_conv_ln_b', 'new_v_conv_w_pw2', 'new_v_conv_b_pw2', 'new_v_pool_w', 'new_v_pool_b', 'new_v_pool_scale', 'new_v_ffn_w_up', 'new_v_ffn_w_dw', 'new_v_ffn_b_dw', 'new_v_ffn_w_down', 'new_v_final_norm']
TWIN_LEAF_KINDS = {'loss': 'loss', 'grad_x': 'grad_x', 'grad_norm_mix': 'grad_w', 'grad_norm_ffn': 'grad_w', 'grad_conv_w_pw1': 'grad_w', 'grad_conv_b_pw1': 'grad_w', 'grad_conv_w_dw': 'grad_w', 'grad_conv_b_dw': 'grad_w', 'grad_conv_ln_g': 'grad_w', 'grad_conv_ln_b': 'grad_w', 'grad_conv_w_pw2': 'grad_w', 'grad_conv_b_pw2': 'grad_w', 'grad_pool_w': 'grad_w', 'grad_pool_b': 'grad_w', 'grad_pool_scale': 'grad_w', 'grad_ffn_w_up': 'grad_w', 'grad_ffn_w_dw': 'grad_w', 'grad_ffn_b_dw': 'grad_w', 'grad_ffn_w_down': 'grad_w', 'grad_final_norm': 'grad_w', 'delta_norm_mix': 'delta_w', 'delta_norm_ffn': 'delta_w', 'delta_conv_w_pw1': 'delta_w', 'delta_conv_b_pw1': 'delta_w', 'delta_conv_w_dw': 'delta_w', 'delta_conv_b_dw': 'delta_w', 'delta_conv_ln_g': 'delta_w', 'delta_conv_ln_b': 'delta_w', 'delta_conv_w_pw2': 'delta_w', 'delta_conv_b_pw2': 'delta_w', 'delta_pool_w': 'delta_w', 'delta_pool_b': 'delta_w', 'delta_pool_scale': 'delta_w', 'delta_ffn_w_up': 'delta_w', 'delta_ffn_w_dw': 'delta_w', 'delta_ffn_b_dw': 'delta_w', 'delta_ffn_w_down': 'delta_w', 'delta_final_norm': 'delta_w', 'new_m_norm_mix': 'new_m', 'new_m_norm_ffn': 'new_m', 'new_m_conv_w_pw1': 'new_m', 'new_m_conv_b_pw1': 'new_m', 'new_m_conv_w_dw': 'new_m', 'new_m_conv_b_dw': 'new_m', 'new_m_conv_ln_g': 'new_m', 'new_m_conv_ln_b': 'new_m', 'new_m_conv_w_pw2': 'new_m', 'new_m_conv_b_pw2': 'new_m', 'new_m_pool_w': 'new_m', 'new_m_pool_b': 'new_m', 'new_m_pool_scale': 'new_m', 'new_m_ffn_w_up': 'new_m', 'new_m_ffn_w_dw': 'new_m', 'new_m_ffn_b_dw': 'new_m', 'new_m_ffn_w_down': 'new_m', 'new_m_final_norm': 'new_m', 'new_v_norm_mix': 'new_v', 'new_v_norm_ffn': 'new_v', 'new_v_conv_w_pw1': 'new_v', 'new_v_conv_b_pw1': 'new_v', 'new_v_conv_w_dw': 'new_v', 'new_v_conv_b_dw': 'new_v', 'new_v_conv_ln_g': 'new_v', 'new_v_conv_ln_b': 'new_v', 'new_v_conv_w_pw2': 'new_v', 'new_v_conv_b_pw2': 'new_v', 'new_v_pool_w': 'new_v', 'new_v_pool_b': 'new_v', 'new_v_pool_scale': 'new_v', 'new_v_ffn_w_up': 'new_v', 'new_v_ffn_w_dw': 'new_v', 'new_v_ffn_b_dw': 'new_v', 'new_v_ffn_w_down': 'new_v', 'new_v_final_norm': 'new_v'}


def _forward(args):
    return _fwd_reference(*[args[k] for k in FWD_PARAMS])


def _output_shape():
    out = _jax.eval_shape(lambda: _forward(_fwd_setup_inputs(0)))
    return out.shape, out.dtype

N_MICROBATCH = 1
ADAM_LR = 0.001
ADAM_B1 = 0.9
ADAM_B2 = 0.999
ADAM_EPS = 1e-08
ADAM_WD = 0.01
ADAM_STEP = 10
PER_EXAMPLE_BATCH_AXIS = {'x': 0, 'loss_target': 0}
SHARED_INPUTS = []
_WEIGHT_DTYPES = {'norm_mix': _jnp.float32, 'norm_ffn': _jnp.float32, 'conv_w_pw1': _jnp.float32, 'conv_b_pw1': _jnp.float32, 'conv_w_dw': _jnp.float32, 'conv_b_dw': _jnp.float32, 'conv_ln_g': _jnp.float32, 'conv_ln_b': _jnp.float32, 'conv_w_pw2': _jnp.float32, 'conv_b_pw2': _jnp.float32, 'pool_w': _jnp.float32, 'pool_b': _jnp.float32, 'pool_scale': _jnp.float32, 'ffn_w_up': _jnp.float32, 'ffn_w_dw': _jnp.float32, 'ffn_b_dw': _jnp.float32, 'ffn_w_down': _jnp.float32, 'final_norm': _jnp.float32}
MOMENT_SCALE = {'norm_mix': 1.269409e-01, 'norm_ffn': 1.142444e-01, 'conv_w_pw1': 8.530176e-02, 'conv_b_pw1': 8.802595e-02, 'conv_w_dw': 1.135699e-01, 'conv_b_dw': 2.218652e-01, 'conv_ln_g': 1.361602e-01, 'conv_ln_b': 1.208345e-01, 'conv_w_pw2': 1.117901e-01, 'conv_b_pw2': 2.239065e-01, 'pool_w': 1.143792e-01, 'pool_b': 1.962149e-01, 'pool_scale': 7.297389e-01, 'ffn_w_up': 4.813202e-02, 'ffn_w_dw': 5.022843e-02, 'ffn_b_dw': 4.628222e-02, 'ffn_w_down': 7.918851e-02, 'final_norm': 3.213877e+01}


def _to_microbatches(a, axis):
    t = _jnp.moveaxis(a, axis, 0)
    t = t.reshape((N_MICROBATCH, t.shape[0] // N_MICROBATCH) + t.shape[1:])
    return _jnp.moveaxis(t, 1, axis + 1)


def setup_inputs(seed: int = 0) -> dict:
    inp = _fwd_setup_inputs(seed)
    key = _jax.random.fold_in(_jax.random.key(seed), 7919)
    shape, _ = _output_shape()
    out = dict(inp)
    out["loss_target"] = _jax.random.normal(_jax.random.fold_in(key, 0), shape, _jnp.float32)
    for i, name in enumerate(TWIN_WEIGHTS):
        w = inp[name].astype(_jnp.float32)
        if MOMENT_SCALE is None:
            s = _jnp.sqrt(_jnp.mean(_jnp.square(w)) + 1e-30)
        else:
            s = MOMENT_SCALE[name]
        km, kv = _jax.random.split(_jax.random.fold_in(key, i + 1))
        out[name] = w
        out["m_" + name] = s * _jax.random.normal(km, w.shape, _jnp.float32)
        out["v_" + name] = (s * s) * _jax.random.uniform(kv, w.shape, _jnp.float32, 0.5, 1.5)
    if N_MICROBATCH > 1:
        for name, axis in PER_EXAMPLE_BATCH_AXIS.items():
            out[name] = _to_microbatches(out[name], axis)
    return {'x': out['x'], 'norm_mix': out['norm_mix'], 'norm_ffn': out['norm_ffn'], 'conv_w_pw1': out['conv_w_pw1'], 'conv_b_pw1': out['conv_b_pw1'], 'conv_w_dw': out['conv_w_dw'], 'conv_b_dw': out['conv_b_dw'], 'conv_ln_g': out['conv_ln_g'], 'conv_ln_b': out['conv_ln_b'], 'conv_w_pw2': out['conv_w_pw2'], 'conv_b_pw2': out['conv_b_pw2'], 'pool_w': out['pool_w'], 'pool_b': out['pool_b'], 'pool_scale': out['pool_scale'], 'ffn_w_up': out['ffn_w_up'], 'ffn_w_dw': out['ffn_w_dw'], 'ffn_b_dw': out['ffn_b_dw'], 'ffn_w_down': out['ffn_w_down'], 'final_norm': out['final_norm'], 'loss_target': out['loss_target'], 'm_norm_mix': out['m_norm_mix'], 'm_norm_ffn': out['m_norm_ffn'], 'm_conv_w_pw1': out['m_conv_w_pw1'], 'm_conv_b_pw1': out['m_conv_b_pw1'], 'm_conv_w_dw': out['m_conv_w_dw'], 'm_conv_b_dw': out['m_conv_b_dw'], 'm_conv_ln_g': out['m_conv_ln_g'], 'm_conv_ln_b': out['m_conv_ln_b'], 'm_conv_w_pw2': out['m_conv_w_pw2'], 'm_conv_b_pw2': out['m_conv_b_pw2'], 'm_pool_w': out['m_pool_w'], 'm_pool_b': out['m_pool_b'], 'm_pool_scale': out['m_pool_scale'], 'm_ffn_w_up': out['m_ffn_w_up'], 'm_ffn_w_dw': out['m_ffn_w_dw'], 'm_ffn_b_dw': out['m_ffn_b_dw'], 'm_ffn_w_down': out['m_ffn_w_down'], 'm_final_norm': out['m_final_norm'], 'v_norm_mix': out['v_norm_mix'], 'v_norm_ffn': out['v_norm_ffn'], 'v_conv_w_pw1': out['v_conv_w_pw1'], 'v_conv_b_pw1': out['v_conv_b_pw1'], 'v_conv_w_dw': out['v_conv_w_dw'], 'v_conv_b_dw': out['v_conv_b_dw'], 'v_conv_ln_g': out['v_conv_ln_g'], 'v_conv_ln_b': out['v_conv_ln_b'], 'v_conv_w_pw2': out['v_conv_w_pw2'], 'v_conv_b_pw2': out['v_conv_b_pw2'], 'v_pool_w': out['v_pool_w'], 'v_pool_b': out['v_pool_b'], 'v_pool_scale': out['v_pool_scale'], 'v_ffn_w_up': out['v_ffn_w_up'], 'v_ffn_w_dw': out['v_ffn_w_dw'], 'v_ffn_b_dw': out['v_ffn_b_dw'], 'v_ffn_w_down': out['v_ffn_w_down'], 'v_final_norm': out['v_final_norm']}


def _loss(weights, diff, rest, loss_target):
    with _jax.named_scope("forward"):
        args = {**rest, TWIN_DIFF_INPUT: diff, **{k: w.astype(_WEIGHT_DTYPES[k]) for k, w in weights.items()}}
        y = _forward(args)
    with _jax.named_scope("loss_head"):
        err = _jnp.square(y.astype(_jnp.float32) - loss_target)
        return 0.5 * _jnp.sum(_jnp.mean(err, axis=-1)) if err.ndim else 0.5 * err


def _adamw(w, g, m, v):
    m = ADAM_B1 * m + (1.0 - ADAM_B1) * g
    v = ADAM_B2 * v + (1.0 - ADAM_B2) * _jnp.square(g)
    m_hat = m / (1.0 - ADAM_B1 ** ADAM_STEP)
    v_hat = v / (1.0 - ADAM_B2 ** ADAM_STEP)
    delta = -ADAM_LR * (m_hat / (_jnp.sqrt(v_hat) + ADAM_EPS) + ADAM_WD * w)
    return delta, m, v


def reference(x, norm_mix, norm_ffn, conv_w_pw1, conv_b_pw1, conv_w_dw, conv_b_dw, conv_ln_g, conv_ln_b, conv_w_pw2, conv_b_pw2, pool_w, pool_b, pool_scale, ffn_w_up, ffn_w_dw, ffn_b_dw, ffn_w_down, final_norm, loss_target, m_norm_mix, m_norm_ffn, m_conv_w_pw1, m_conv_b_pw1, m_conv_w_dw, m_conv_b_dw, m_conv_ln_g, m_conv_ln_b, m_conv_w_pw2, m_conv_b_pw2, m_pool_w, m_pool_b, m_pool_scale, m_ffn_w_up, m_ffn_w_dw, m_ffn_b_dw, m_ffn_w_down, m_final_norm, v_norm_mix, v_norm_ffn, v_conv_w_pw1, v_conv_b_pw1, v_conv_w_dw, v_conv_b_dw, v_conv_ln_g, v_conv_ln_b, v_conv_w_pw2, v_conv_b_pw2, v_pool_w, v_pool_b, v_pool_scale, v_ffn_w_up, v_ffn_w_dw, v_ffn_b_dw, v_ffn_w_down, v_final_norm):
    given = dict(x=x, norm_mix=norm_mix, norm_ffn=norm_ffn, conv_w_pw1=conv_w_pw1, conv_b_pw1=conv_b_pw1, conv_w_dw=conv_w_dw, conv_b_dw=conv_b_dw, conv_ln_g=conv_ln_g, conv_ln_b=conv_ln_b, conv_w_pw2=conv_w_pw2, conv_b_pw2=conv_b_pw2, pool_w=pool_w, pool_b=pool_b, pool_scale=pool_scale, ffn_w_up=ffn_w_up, ffn_w_dw=ffn_w_dw, ffn_b_dw=ffn_b_dw, ffn_w_down=ffn_w_down, final_norm=final_norm, loss_target=loss_target, m_norm_mix=m_norm_mix, m_norm_ffn=m_norm_ffn, m_conv_w_pw1=m_conv_w_pw1, m_conv_b_pw1=m_conv_b_pw1, m_conv_w_dw=m_conv_w_dw, m_conv_b_dw=m_conv_b_dw, m_conv_ln_g=m_conv_ln_g, m_conv_ln_b=m_conv_ln_b, m_conv_w_pw2=m_conv_w_pw2, m_conv_b_pw2=m_conv_b_pw2, m_pool_w=m_pool_w, m_pool_b=m_pool_b, m_pool_scale=m_pool_scale, m_ffn_w_up=m_ffn_w_up, m_ffn_w_dw=m_ffn_w_dw, m_ffn_b_dw=m_ffn_b_dw, m_ffn_w_down=m_ffn_w_down, m_final_norm=m_final_norm, v_norm_mix=v_norm_mix, v_norm_ffn=v_norm_ffn, v_conv_w_pw1=v_conv_w_pw1, v_conv_b_pw1=v_conv_b_pw1, v_conv_w_dw=v_conv_w_dw, v_conv_b_dw=v_conv_b_dw, v_conv_ln_g=v_conv_ln_g, v_conv_ln_b=v_conv_ln_b, v_conv_w_pw2=v_conv_w_pw2, v_conv_b_pw2=v_conv_b_pw2, v_pool_w=v_pool_w, v_pool_b=v_pool_b, v_pool_scale=v_pool_scale, v_ffn_w_up=v_ffn_w_up, v_ffn_w_dw=v_ffn_w_dw, v_ffn_b_dw=v_ffn_b_dw, v_ffn_w_down=v_ffn_w_down, v_final_norm=v_final_norm)
    weights = {n: given[n] for n in TWIN_WEIGHTS}
    shared = {n: given[n] for n in SHARED_INPUTS}
    per_example = {n: given[n] for n in ['x']}
    grad_fn = _jax.value_and_grad(_loss, argnums=(0, 1))

    def one_microbatch(ex, loss_target):
        ex = dict(ex)
        diff = ex.pop(TWIN_DIFF_INPUT)
        return grad_fn(weights, diff, {**shared, **ex}, loss_target)

    if N_MICROBATCH == 1:
        loss, (grad_w, grad_x) = one_microbatch(per_example, given["loss_target"])
    else:
        def body(carry, xs):
            loss_sum, grad_sum = carry
            l_k, (gw_k, gx_k) = one_microbatch(xs[0], xs[1])
            with _jax.named_scope("update"):
                return (loss_sum + l_k, _jax.tree.map(_jnp.add, grad_sum, gw_k)), gx_k

        init = (_jnp.zeros((), _jnp.float32), _jax.tree.map(_jnp.zeros_like, weights))
        (loss, grad_w), grad_x = _jax.lax.scan(body, init, (per_example, given["loss_target"]))
    with _jax.named_scope("update"):
        delta_w, new_m, new_v = {}, {}, {}
        for n in TWIN_WEIGHTS:
            delta_w[n], new_m[n], new_v[n] = _adamw(weights[n], grad_w[n], given["m_" + n], given["v_" + n])
    return (loss, grad_x, *[grad_w[n] for n in TWIN_WEIGHTS], *[delta_w[n] for n in TWIN_WEIGHTS],
            *[new_m[n] for n in TWIN_WEIGHTS], *[new_v[n] for n in TWIN_WEIGHTS])
```

```python
import functools

import jax
import jax.numpy as jnp
from jax import lax
from jax.experimental import pallas as pl
from jax.experimental.pallas import tpu as pltpu

F32 = jnp.float32
BF16 = jnp.bfloat16
MESH = pl.DeviceIdType.MESH
HBM = pl.BlockSpec(memory_space=pltpu.HBM)

N_DEV = 8
RMS_EPS = 1e-6
LN_EPS = 1e-5
POOL_WINDOWS = (2, 4, 8, 16)
N_GROUPS = len(POOL_WINDOWS)
ADAM_LR = 0.001
ADAM_B1 = 0.9
ADAM_B2 = 0.999
ADAM_EPS = 1e-08
ADAM_WD = 0.01
ADAM_STEP = 10

LANE = 128
HALO = 32
HALO16 = 16
VMEM_LIMIT = 56 * 1024 * 1024


def _params(*sem):
    return pltpu.CompilerParams(dimension_semantics=sem if sem else None, vmem_limit_bytes=VMEM_LIMIT)


def _tile(n, pref):
    for t in range(min(pref, n), 15, -1):
        if n % t == 0 and t % 16 == 0:
            return t
    return n


def _sigmoid(z):
    return 1.0 / (1.0 + jnp.exp(-z))


def _me():
    return lax.axis_index("x"), lax.axis_index("y"), lax.axis_index("c")


def _flip(pos, m):
    x, y, c = pos
    return ((1 - x) if m & 4 else x, (1 - y) if m & 2 else y, (1 - c) if m & 1 else c)


def _lin(pos):
    return 4 * pos[0] + 2 * pos[1] + pos[2]


def _exchange(name, arrs, scatter):
    n = len(arrs)

    def body(*refs):
        ins, outs = refs[:n], refs[n:2 * n]
        send_sems, recv_sems, loc_sems = refs[2 * n:]
        me = _me()
        my = _lin(me)
        local = []
        for a in range(n):
            src = ins[a].at[my] if scatter else ins[a]
            cp = pltpu.make_async_copy(src, outs[a].at[my], loc_sems.at[a])
            cp.start()
            local.append(cp)
        remote = []
        for m in range(1, N_DEV):
            peer = _flip(me, m)
            pid = _lin(peer)
            for a in range(n):
                src = ins[a].at[pid] if scatter else ins[a]
                cp = pltpu.make_async_remote_copy(
                    src_ref=src, dst_ref=outs[a].at[my], send_sem=send_sems.at[a, m - 1],
                    recv_sem=recv_sems.at[a, m - 1], device_id=peer, device_id_type=MESH)
                cp.start()
                arrival = pltpu.make_async_remote_copy(
                    src_ref=src, dst_ref=outs[a].at[pid], send_sem=send_sems.at[a, m - 1],
                    recv_sem=recv_sems.at[a, m - 1], device_id=peer, device_id_type=MESH)
                remote.append((cp, arrival))
        for cp in local:
            cp.wait()
        for cp, arrival in remote:
            cp.wait_send()
            arrival.wait_recv()

    blocks = [a.shape[1:] if scatter else a.shape for a in arrs]
    return pl.pallas_call(
        body, name=name,
        out_shape=[jax.ShapeDtypeStruct((N_DEV,) + tuple(b), a.dtype) for a, b in zip(arrs, blocks)],
        in_specs=[HBM] * n, out_specs=[HBM] * n,
        scratch_shapes=[pltpu.SemaphoreType.DMA((n, N_DEV - 1)), pltpu.SemaphoreType.DMA((n, N_DEV - 1)),
                        pltpu.SemaphoreType.DMA((n,))],
    )(*arrs)


def _mm(name, a, b, *, grid, a_spec, b_spec, out_spec, out_shape, dims, acc_shape, extras=(), extra_specs=(),
        epilogue=None):
    nk = grid[2]
    ne = len(extras)

    def body(a_ref, b_ref, *rest):
        ex, o_ref, acc_ref = rest[:ne], rest[ne], rest[ne + 1]
        k = pl.program_id(2)
        part = lax.dot_general(a_ref[...].astype(BF16), b_ref[...].astype(BF16), (dims, ((), ())),
                               preferred_element_type=F32)

        @pl.when(k == 0)
        def _():
            acc_ref[...] = part

        @pl.when(k > 0)
        def _():
            acc_ref[...] += part

        @pl.when(k == nk - 1)
        def _():
            r = acc_ref[...]
            if epilogue is not None:
                r = epilogue(r, *[e[...] for e in ex])
            o_ref[...] = r.astype(o_ref.dtype)

    return pl.pallas_call(
        body, name=name, grid=grid, in_specs=[a_spec, b_spec, *extra_specs], out_specs=out_spec,
        out_shape=out_shape, scratch_shapes=[pltpu.VMEM(acc_shape, F32)],
        compiler_params=_params("parallel", "parallel", "arbitrary"),
    )(a, b, *extras)


NN = ((1,), (0,))
NT = ((1,), (1,))
TN = ((0,), (0,))


def _rms_fwd(name, x, gain):
    t, d = x.shape
    tr = _tile(t, 512)

    def body(x_ref, g_ref, h_ref):
        xv = x_ref[...]
        rstd = lax.rsqrt(jnp.mean(xv * xv, axis=-1, keepdims=True) + RMS_EPS)
        h_ref[...] = (xv * rstd * g_ref[...]).astype(BF16)

    return pl.pallas_call(
        body, name=name, grid=(t // tr,),
        in_specs=[pl.BlockSpec((tr, d), lambda i: (i, 0)), pl.BlockSpec((1, d), lambda i: (0, 0))],
        out_specs=pl.BlockSpec((tr, d), lambda i: (i, 0)),
        out_shape=jax.ShapeDtypeStruct((t, d), BF16), compiler_params=_params("parallel"),
    )(x, gain)


def _rms_bwd(name, dh, x, gain, dres):
    t, d = x.shape
    tr = _tile(t, 512)

    def body(dh_ref, x_ref, g_ref, dres_ref, dx_ref, dg_ref, cs_ref):
        i = pl.program_id(0)
        xv = x_ref[...]
        rstd = lax.rsqrt(jnp.mean(xv * xv, axis=-1, keepdims=True) + RMS_EPS)
        xhat = xv * rstd
        dhv = dh_ref[...]
        dxhat = dhv * g_ref[...]
        dx = dres_ref[...] + rstd * (dxhat - xhat * jnp.mean(dxhat * xhat, axis=-1, keepdims=True))
        dx_ref[...] = dx

        @pl.when(i == 0)
        def _():
            dg_ref[...] = jnp.zeros_like(dg_ref)
            cs_ref[...] = jnp.zeros_like(cs_ref)

        dg_ref[...] += jnp.sum(dhv * xhat, axis=0, keepdims=True)
        cs_ref[...] += jnp.sum(dx, axis=0, keepdims=True)

    row = pl.BlockSpec((tr, d), lambda i: (i, 0))
    vec = pl.BlockSpec((1, d), lambda i: (0, 0))
    return pl.pallas_call(
        body, name=name, grid=(t // tr,), in_specs=[row, row, vec, row], out_specs=[row, vec, vec],
        out_shape=[jax.ShapeDtypeStruct((t, d), F32), jax.ShapeDtypeStruct((1, d), F32),
                   jax.ShapeDtypeStruct((1, d), F32)],
        compiler_params=_params("arbitrary"),
    )(dh, x, gain, dres)


def _final(name, x, tgt, gain):
    t, d = x.shape
    tr = _tile(t, 512)

    def body(x_ref, t_ref, g_ref, dx_ref, loss_ref, dg_ref, cs_ref):
        i = pl.program_id(0)
        xv = x_ref[...]
        g = g_ref[...]
        rstd = lax.rsqrt(jnp.mean(xv * xv, axis=-1, keepdims=True) + RMS_EPS)
        xhat = xv * rstd
        err = xhat * g - t_ref[...]
        dy = err / d
        dxhat = dy * g
        dx = rstd * (dxhat - xhat * jnp.mean(dxhat * xhat, axis=-1, keepdims=True))
        dx_ref[...] = dx

        @pl.when(i == 0)
        def _():
            loss_ref[...] = jnp.zeros_like(loss_ref)
            dg_ref[...] = jnp.zeros_like(dg_ref)
            cs_ref[...] = jnp.zeros_like(cs_ref)

        loss_ref[...] += 0.5 * jnp.sum(jnp.mean(err * err, axis=-1, keepdims=True), axis=0, keepdims=True)
        dg_ref[...] += jnp.sum(dy * xhat, axis=0, keepdims=True)
        cs_ref[...] += jnp.sum(dx, axis=0, keepdims=True)

    row = pl.BlockSpec((tr, d), lambda i: (i, 0))
    vec = pl.BlockSpec((1, d), lambda i: (0, 0))
    one = pl.BlockSpec((1, 1), lambda i: (0, 0))
    return pl.pallas_call(
        body, name=name, grid=(t // tr,), in_specs=[row, row, vec], out_specs=[row, one, vec, vec],
        out_shape=[jax.ShapeDtypeStruct((t, d), F32), jax.ShapeDtypeStruct((1, 1), F32),
                   jax.ShapeDtypeStruct((1, d), F32), jax.ShapeDtypeStruct((1, d), F32)],
        compiler_params=_params("arbitrary"),
    )(x, tgt, gain)


def _ln_fwd(name, v, g, b):
    t, d = v.shape
    tr = _tile(t, 512)

    def body(v_ref, g_ref, b_ref, s_ref):
        vv = v_ref[...]
        mu = jnp.mean(vv, axis=-1, keepdims=True)
        cen = vv - mu
        rstd = lax.rsqrt(jnp.mean(cen * cen, axis=-1, keepdims=True) + LN_EPS)
        z = cen * rstd * g_ref[...] + b_ref[...]
        s_ref[...] = (z * _sigmoid(z)).astype(BF16)

    row = pl.BlockSpec((tr, d), lambda i: (i, 0))
    vec = pl.BlockSpec((1, d), lambda i: (0, 0))
    return pl.pallas_call(
        body, name=name, grid=(t // tr,), in_specs=[row, vec, vec], out_specs=row,
        out_shape=jax.ShapeDtypeStruct((t, d), BF16), compiler_params=_params("parallel"),
    )(v, g, b)


def _ln_bwd(name, ds, v, g, b):
    t, d = v.shape
    tr = _tile(t, 512)

    def body(ds_ref, v_ref, g_ref, b_ref, dv_ref, dg_ref, db_ref, cs_ref):
        i = pl.program_id(0)
        vv = v_ref[...]
        g = g_ref[...]
        mu = jnp.mean(vv, axis=-1, keepdims=True)
        cen = vv - mu
        rstd = lax.rsqrt(jnp.mean(cen * cen, axis=-1, keepdims=True) + LN_EPS)
        y = cen * rstd
        z = y * g + b_ref[...]
        sig = _sigmoid(z)
        dz = ds_ref[...] * sig * (1.0 + z * (1.0 - sig))
        dy = dz * g
        dv = rstd * (dy - jnp.mean(dy, axis=-1, keepdims=True) - y * jnp.mean(dy * y, axis=-1, keepdims=True))
        dv_ref[...] = dv

        @pl.when(i == 0)
        def _():
            dg_ref[...] = jnp.zeros_like(dg_ref)
            db_ref[...] = jnp.zeros_like(db_ref)
            cs_ref[...] = jnp.zeros_like(cs_ref)

        dg_ref[...] += jnp.sum(dz * y, axis=0, keepdims=True)
        db_ref[...] += jnp.sum(dz, axis=0, keepdims=True)
        cs_ref[...] += jnp.sum(dv, axis=0, keepdims=True)

    row = pl.BlockSpec((tr, d), lambda i: (i, 0))
    vec = pl.BlockSpec((1, d), lambda i: (0, 0))
    return pl.pallas_call(
        body, name=name, grid=(t // tr,), in_specs=[row, row, vec, vec], out_specs=[row, vec, vec, vec],
        out_shape=[jax.ShapeDtypeStruct((t, d), F32)] + [jax.ShapeDtypeStruct((1, d), F32)] * 3,
        compiler_params=_params("arbitrary"),
    )(ds, v, g, b)


def _conv_tiles(t, seq):
    ts = _tile(seq, 512)
    return ts, seq // ts, _tile(ts, 64)


def _conv_fwd(name, a, w, b, seq):
    _, t, d = a.shape
    k_taps = w.shape[0]
    ts, tps, rc = _conv_tiles(t, seq)
    hb = ts // HALO

    def body(cur_ref, prev_ref, w_ref, b_ref, v_ref, upad):
        i = pl.program_id(1)
        first = (i % tps) == 0
        pv = prev_ref[0].astype(F32)
        pg = prev_ref[1].astype(F32)
        upad[0:HALO, :] = jnp.where(first, 0.0, pv * _sigmoid(pg))
        upad[HALO:HALO + ts, :] = cur_ref[0].astype(F32) * _sigmoid(cur_ref[1].astype(F32))
        wv = w_ref[...]
        bias = jnp.broadcast_to(b_ref[...], (rc, LANE))
        for r0 in range(0, ts, rc):
            acc = bias
            for k in range(k_taps):
                acc = acc + wv[k:k + 1, :] * upad[pl.ds(HALO - (k_taps - 1) + k + r0, rc), :]
            v_ref[pl.ds(r0, rc), :] = acc

    return pl.pallas_call(
        body, name=name, grid=(d // LANE, t // ts),
        in_specs=[pl.BlockSpec((2, ts, LANE), lambda c, i: (0, i, c)),
                  pl.BlockSpec((2, HALO, LANE), lambda c, i: (0, jnp.maximum(i * hb - 1, 0), c)),
                  pl.BlockSpec((k_taps, LANE), lambda c, i: (0, c)),
                  pl.BlockSpec((1, LANE), lambda c, i: (0, c))],
        out_specs=pl.BlockSpec((ts, LANE), lambda c, i: (i, c)),
        out_shape=jax.ShapeDtypeStruct((t, d), F32),
        scratch_shapes=[pltpu.VMEM((HALO + ts, LANE), F32)],
        compiler_params=_params("parallel", "parallel"),
    )(a, a, w, b)


def _conv_bwd(name, a, dv, w, seq):
    _, t, d = a.shape
    k_taps = w.shape[0]
    ts, tps, rc = _conv_tiles(t, seq)
    hb = ts // HALO
    nhb = t // HALO

    def body(cur_ref, prev_ref, dv_ref, ndv_ref, w_ref, da_ref, dw_ref, dbp_ref, upad, dvpad, dwrows):
        i = pl.program_id(1)
        first = (i % tps) == 0
        last = (i % tps) == tps - 1
        pv = prev_ref[0].astype(F32)
        pg = prev_ref[1].astype(F32)
        upad[0:HALO, :] = jnp.where(first, 0.0, pv * _sigmoid(pg))
        upad[HALO:HALO + ts, :] = cur_ref[0].astype(F32) * _sigmoid(cur_ref[1].astype(F32))
        dvpad[0:ts, :] = dv_ref[...]
        dvpad[ts:ts + HALO, :] = jnp.where(last, 0.0, ndv_ref[...])
        wv = w_ref[...]

        @pl.when(i == 0)
        def _():
            dw_ref[...] = jnp.zeros_like(dw_ref)
            dbp_ref[...] = jnp.zeros_like(dbp_ref)

        sv = jnp.zeros((1, LANE), F32)
        sg = jnp.zeros((1, LANE), F32)
        for r0 in range(0, ts, rc):
            du = jnp.zeros((rc, LANE), F32)
            for k in range(k_taps):
                du = du + wv[k:k + 1, :] * dvpad[pl.ds(r0 + (k_taps - 1) - k, rc), :]
            av = cur_ref[0, pl.ds(r0, rc), :].astype(F32)
            sig = _sigmoid(cur_ref[1, pl.ds(r0, rc), :].astype(F32))
            dval = du * sig
            dgate = du * av * sig * (1.0 - sig)
            da_ref[0, pl.ds(r0, rc), :] = dval.astype(BF16)
            da_ref[1, pl.ds(r0, rc), :] = dgate.astype(BF16)
            sv = sv + jnp.sum(dval, axis=0, keepdims=True)
            sg = sg + jnp.sum(dgate, axis=0, keepdims=True)
        dbp_ref[0] += sv
        dbp_ref[1] += sg

        for k in range(k_taps):
            acc = jnp.zeros((rc, LANE), F32)
            for r0 in range(0, ts, rc):
                acc = acc + dvpad[pl.ds(r0, rc), :] * upad[pl.ds(HALO - (k_taps - 1) + k + r0, rc), :]
            dwrows[k:k + 1, :] = jnp.sum(acc, axis=0, keepdims=True)
        dw_ref[...] += dwrows[0:k_taps, :]

    return pl.pallas_call(
        body, name=name, grid=(d // LANE, t // ts),
        in_specs=[pl.BlockSpec((2, ts, LANE), lambda c, i: (0, i, c)),
                  pl.BlockSpec((2, HALO, LANE), lambda c, i: (0, jnp.maximum(i * hb - 1, 0), c)),
                  pl.BlockSpec((ts, LANE), lambda c, i: (i, c)),
                  pl.BlockSpec((HALO, LANE), lambda c, i: (jnp.minimum((i + 1) * hb, nhb - 1), c)),
                  pl.BlockSpec((k_taps, LANE), lambda c, i: (0, c))],
        out_specs=[pl.BlockSpec((2, ts, LANE), lambda c, i: (0, i, c)),
                   pl.BlockSpec((k_taps, LANE), lambda c, i: (0, c)),
                   pl.BlockSpec((2, 1, LANE), lambda c, i: (0, 0, c))],
        out_shape=[jax.ShapeDtypeStruct((2, t, d), BF16), jax.ShapeDtypeStruct((k_taps, d), F32),
                   jax.ShapeDtypeStruct((2, 1, d), F32)],
        scratch_shapes=[pltpu.VMEM((HALO + ts, LANE), F32), pltpu.VMEM((ts + HALO, LANE), F32),
                        pltpu.VMEM((HALO, LANE), F32)],
        compiler_params=_params("parallel", "arbitrary"),
    )(a, a, dv, dv, w)


def _pool_fwd(name, x, gain, seq):
    t, d = x.shape
    ts = _tile(seq, 256)
    tps = seq // ts
    hb = ts // HALO
    cg = d // N_GROUPS

    def body(cur_ref, prev_ref, g_ref, o_ref, hpad):
        i = pl.program_id(0)
        first = (i % tps) == 0
        g = g_ref[...]

        def norm(xv):
            return xv * lax.rsqrt(jnp.mean(xv * xv, axis=-1, keepdims=True) + RMS_EPS) * g

        hpad[0:HALO, :] = jnp.where(first, 0.0, norm(prev_ref[...]))
        hpad[HALO:HALO + ts, :] = norm(cur_ref[...])
        pos = (i % tps) * ts + lax.broadcasted_iota(jnp.int32, (ts, 1), 0)
        for gi, win in enumerate(POOL_WINDOWS):
            sl = slice(gi * cg, (gi + 1) * cg)
            own = hpad[HALO:HALO + ts, sl]
            acc = own
            for j in range(1, win):
                acc = acc + hpad[HALO - j:HALO - j + ts, sl]
            cnt = jnp.minimum(pos + 1, win).astype(F32)
            o_ref[:, sl] = (acc / cnt - own).astype(BF16)

    return pl.pallas_call(
        body, name=name, grid=(t // ts,),
        in_specs=[pl.BlockSpec((ts, d), lambda i: (i, 0)),
                  pl.BlockSpec((HALO, d), lambda i: (jnp.maximum(i * hb - 1, 0), 0)),
                  pl.BlockSpec((1, d), lambda i: (0, 0))],
        out_specs=pl.BlockSpec((ts, d), lambda i: (i, 0)),
        out_shape=jax.ShapeDtypeStruct((t, d), BF16),
        scratch_shapes=[pltpu.VMEM((HALO + ts, d), F32)],
        compiler_params=_params("parallel"),
    )(x, x, gain)


def _pool_bwd_mm(name, pooled, wp, dp, scale, bias):
    t, d = pooled.shape
    cg = d // N_GROUPS
    tm = _tile(t, 512)

    def body(p_ref, w_ref, dp_ref, s_ref, b_ref, dpo_ref, dmx_ref, ds_ref, db_ref):
        i = pl.program_id(1)
        wv = w_ref[...]
        mixed = jnp.dot(p_ref[...], wv, preferred_element_type=F32)
        dpv = dp_ref[...]
        dmx = dpv * s_ref[...]
        dmx16 = dmx.astype(BF16)
        dmx_ref[...] = dmx16
        dpo_ref[...] = lax.dot_general(dmx16, wv, (NT, ((), ())), preferred_element_type=F32)

        @pl.when(i == 0)
        def _():
            ds_ref[...] = jnp.zeros_like(ds_ref)
            db_ref[...] = jnp.zeros_like(db_ref)

        ds_ref[...] += jnp.sum(dpv * (mixed + b_ref[...]), axis=0, keepdims=True)
        db_ref[...] += jnp.sum(dmx, axis=0, keepdims=True)

    blk = pl.BlockSpec((tm, cg), lambda g, i: (i, g))
    vec = pl.BlockSpec((1, cg), lambda g, i: (0, g))
    return pl.pallas_call(
        body, name=name, grid=(N_GROUPS, t // tm),
        in_specs=[blk, pl.BlockSpec((None, cg, cg), lambda g, i: (g, 0, 0)), blk, vec, vec],
        out_specs=[blk, blk, vec, vec],
        out_shape=[jax.ShapeDtypeStruct((t, d), F32), jax.ShapeDtypeStruct((t, d), BF16),
                   jax.ShapeDtypeStruct((1, d), F32), jax.ShapeDtypeStruct((1, d), F32)],
        compiler_params=_params("parallel", "arbitrary"),
    )(pooled, wp, dp, scale, bias)


def _pool_bwd(name, dpooled, x, gain, dres, seq):
    t, d = x.shape
    ts = _tile(seq, 256)
    tps = seq // ts
    hb = ts // HALO
    nhb = t // HALO
    cg = d // N_GROUPS

    def body(dpo_ref, ndpo_ref, x_ref, g_ref, dres_ref, dx_ref, dg_ref, qpad, dh):
        i = pl.program_id(0)
        last = (i % tps) == tps - 1
        pos = (i % tps) * ts + lax.broadcasted_iota(jnp.int32, (ts, 1), 0)
        for gi, win in enumerate(POOL_WINDOWS):
            sl = slice(gi * cg, (gi + 1) * cg)
            cur = dpo_ref[:, sl]
            qpad[0:ts, sl] = cur / jnp.minimum(pos + 1, win).astype(F32)
            qpad[ts:ts + HALO, sl] = jnp.where(last, 0.0, ndpo_ref[:, sl] / float(win))
            acc = -cur
            for j in range(win):
                acc = acc + qpad[j:j + ts, sl]
            dh[:, sl] = acc
        xv = x_ref[...]
        rstd = lax.rsqrt(jnp.mean(xv * xv, axis=-1, keepdims=True) + RMS_EPS)
        xhat = xv * rstd
        dhv = dh[...]
        dxhat = dhv * g_ref[...]
        dx_ref[...] = dres_ref[...] + rstd * (dxhat - xhat * jnp.mean(dxhat * xhat, axis=-1, keepdims=True))

        @pl.when(i == 0)
        def _():
            dg_ref[...] = jnp.zeros_like(dg_ref)

        dg_ref[...] += jnp.sum(dhv * xhat, axis=0, keepdims=True)

    row = pl.BlockSpec((ts, d), lambda i: (i, 0))
    vec = pl.BlockSpec((1, d), lambda i: (0, 0))
    return pl.pallas_call(
        body, name=name, grid=(t // ts,),
        in_specs=[row, pl.BlockSpec((HALO, d), lambda i: (jnp.minimum((i + 1) * hb, nhb - 1), 0)), row, vec, row],
        out_specs=[row, vec],
        out_shape=[jax.ShapeDtypeStruct((t, d), F32), jax.ShapeDtypeStruct((1, d), F32)],
        scratch_shapes=[pltpu.VMEM((ts + HALO, d), F32), pltpu.VMEM((ts, d), F32)],
        compiler_params=_params("arbitrary"),
    )(dpooled, dpooled, x, gain, dres)


def _ffn_fwd(name, up, w, b, seq):
    _, nb, t, f = up.shape
    k_taps = w.shape[1]
    ts = _tile(seq, 256)
    tps = seq // ts
    hb = ts // HALO16

    def body(cur_ref, prev_ref, w_ref, b_ref, g_ref, apad):
        i = pl.program_id(1)
        first = (i % tps) == 0
        apad[0:HALO16, :] = jnp.where(first, 0.0, prev_ref[...].astype(F32))
        apad[HALO16:HALO16 + ts, :] = cur_ref[0].astype(F32)
        wv = w_ref[...]
        c = jnp.broadcast_to(b_ref[...], (ts, f))
        for k in range(k_taps):
            c = c + wv[k:k + 1, :] * apad[HALO16 - (k_taps - 1) + k:HALO16 - (k_taps - 1) + k + ts, :]
        g_ref[...] = (c * _sigmoid(c) * cur_ref[1].astype(F32)).astype(BF16)

    return pl.pallas_call(
        body, name=name, grid=(nb, t // ts),
        in_specs=[pl.BlockSpec((2, None, ts, f), lambda j, i: (0, j, i, 0)),
                  pl.BlockSpec((None, None, HALO16, f), lambda j, i: (0, j, jnp.maximum(i * hb - 1, 0), 0)),
                  pl.BlockSpec((None, k_taps, f), lambda j, i: (j, 0, 0)),
                  pl.BlockSpec((None, 1, f), lambda j, i: (j, 0, 0))],
        out_specs=pl.BlockSpec((None, ts, f), lambda j, i: (j, i, 0)),
        out_shape=jax.ShapeDtypeStruct((nb, t, f), BF16),
        scratch_shapes=[pltpu.VMEM((HALO16 + ts, f), F32)],
        compiler_params=_params("parallel", "parallel"),
    )(up, up, w, b)


def _ffn_bwd(name, up, dg, w, b, seq):
    _, nb, t, f = up.shape
    k_taps = w.shape[1]
    ts = _tile(seq, 256)
    tps = seq // ts
    hb = ts // HALO16
    nhb = t // HALO16
    ext = ts + HALO16

    def body(cur_ref, prev_ref, next_ref, dg_ref, ndg_ref, w_ref, b_ref, dup_ref, dw_ref, db_ref,
             apad, gpad, dgpad, dcpad, dwrows):
        i = pl.program_id(1)
        first = (i % tps) == 0
        last = (i % tps) == tps - 1
        apad[0:HALO16, :] = jnp.where(first, 0.0, prev_ref[...].astype(F32))
        apad[HALO16:HALO16 + ts, :] = cur_ref[0].astype(F32)
        apad[HALO16 + ts:HALO16 + ext, :] = next_ref[0].astype(F32)
        gpad[0:ts, :] = cur_ref[1].astype(F32)
        gpad[ts:ext, :] = next_ref[1].astype(F32)
        dgpad[0:ts, :] = dg_ref[...].astype(F32)
        dgpad[ts:ext, :] = jnp.where(last, 0.0, ndg_ref[...].astype(F32))
        wv = w_ref[...]
        c = jnp.broadcast_to(b_ref[...], (ext, f))
        for k in range(k_taps):
            c = c + wv[k:k + 1, :] * apad[HALO16 - (k_taps - 1) + k:HALO16 - (k_taps - 1) + k + ext, :]
        sig = _sigmoid(c)
        dgv = dgpad[...]
        dcpad[...] = dgv * gpad[...] * sig * (1.0 + c * (1.0 - sig))
        dup_ref[1] = (dgv[0:ts] * (c * sig)[0:ts]).astype(BF16)
        dact = jnp.zeros((ts, f), F32)
        for k in range(k_taps):
            dact = dact + wv[k:k + 1, :] * dcpad[(k_taps - 1) - k:(k_taps - 1) - k + ts, :]
        dup_ref[0] = dact.astype(BF16)

        @pl.when(i == 0)
        def _():
            dw_ref[...] = jnp.zeros_like(dw_ref)
            db_ref[...] = jnp.zeros_like(db_ref)

        dc = dcpad[0:ts, :]
        for k in range(k_taps):
            sh = apad[HALO16 - (k_taps - 1) + k:HALO16 - (k_taps - 1) + k + ts, :]
            dwrows[k:k + 1, :] = jnp.sum(dc * sh, axis=0, keepdims=True)
        dw_ref[...] += dwrows[0:k_taps, :]
        db_ref[...] += jnp.sum(dc, axis=0, keepdims=True)

    return pl.pallas_call(
        body, name=name, grid=(nb, t // ts),
        in_specs=[pl.BlockSpec((2, None, ts, f), lambda j, i: (0, j, i, 0)),
                  pl.BlockSpec((None, None, HALO16, f), lambda j, i: (0, j, jnp.maximum(i * hb - 1, 0), 0)),
                  pl.BlockSpec((2, None, HALO16, f), lambda j, i: (0, j, jnp.minimum((i + 1) * hb, nhb - 1), 0)),
                  pl.BlockSpec((None, ts, f), lambda j, i: (j, i, 0)),
                  pl.BlockSpec((None, HALO16, f), lambda j, i: (j, jnp.minimum((i + 1) * hb, nhb - 1), 0)),
                  pl.BlockSpec((None, k_taps, f), lambda j, i: (j, 0, 0)),
                  pl.BlockSpec((None, 1, f), lambda j, i: (j, 0, 0))],
        out_specs=[pl.BlockSpec((2, None, ts, f), lambda j, i: (0, j, i, 0)),
                   pl.BlockSpec((None, k_taps, f), lambda j, i: (j, 0, 0)),
                   pl.BlockSpec((None, 1, f), lambda j, i: (j, 0, 0))],
        out_shape=[jax.ShapeDtypeStruct((2, nb, t, f), BF16), jax.ShapeDtypeStruct((nb, k_taps, f), F32),
                   jax.ShapeDtypeStruct((nb, 1, f), F32)],
        scratch_shapes=[pltpu.VMEM((HALO16 + ext, f), F32), pltpu.VMEM((ext, f), F32), pltpu.VMEM((ext, f), F32),
                        pltpu.VMEM((ext, f), F32), pltpu.VMEM((8, f), F32)],
        compiler_params=_params("parallel", "arbitrary"),
    )(up, up, up, dg, dg, w, b)


def _sum_rows(name, g):
    ns, r, c = g.shape
    tr = _tile(r, 256)

    def body(g_ref, o_ref):
        acc = g_ref[0]
        for dev in range(1, ns):
            acc = acc + g_ref[dev]
        o_ref[...] = acc

    return pl.pallas_call(
        body, name=name, grid=(r // tr,),
        in_specs=[pl.BlockSpec((ns, tr, c), lambda i: (0, i, 0))],
        out_specs=pl.BlockSpec((tr, c), lambda i: (i, 0)),
        out_shape=jax.ShapeDtypeStruct((r, c), F32), compiler_params=_params("parallel"),
    )(g)


def _adamw(name, gsrc, w, m, v):
    ns, r, c = gsrc.shape
    tr = _tile(r, 256)

    def body(g_ref, w_ref, m_ref, v_ref, go_ref, do_ref, mo_ref, vo_ref):
        g = g_ref[0].astype(F32)
        for dev in range(1, ns):
            g = g + g_ref[dev].astype(F32)
        m_new = ADAM_B1 * m_ref[...] + (1.0 - ADAM_B1) * g
        v_new = ADAM_B2 * v_ref[...] + (1.0 - ADAM_B2) * (g * g)
        m_hat = m_new / (1.0 - ADAM_B1 ** ADAM_STEP)
        v_hat = v_new / (1.0 - ADAM_B2 ** ADAM_STEP)
        go_ref[...] = g
        do_ref[...] = -ADAM_LR * (m_hat / (jnp.sqrt(v_hat) + ADAM_EPS) + ADAM_WD * w_ref[...])
        mo_ref[...] = m_new
        vo_ref[...] = v_new

    row = pl.BlockSpec((tr, c), lambda i: (i, 0))
    return pl.pallas_call(
        body, name=name, grid=(r // tr,),
        in_specs=[pl.BlockSpec((ns, tr, c), lambda i: (0, i, 0)), row, row, row],
        out_specs=[row] * 4, out_shape=[jax.ShapeDtypeStruct((r, c), F32)] * 4,
        compiler_params=_params("parallel"),
    )(gsrc, w, m, v)


def _ffn_forward(tag, r_in, gain, wu, wd, wdw, bdw, seq):
    t, d = r_in.shape
    nbu, _, f = wu.shape
    nb = nbu // 2
    tm = _tile(t, 512)
    h = _rms_fwd(f"{tag}_rms", r_in, gain)
    up = _mm(f"{tag}_up", h, wu, grid=(t // tm, nbu, 1),
             a_spec=pl.BlockSpec((tm, d), lambda i, j, k: (i, 0)),
             b_spec=pl.BlockSpec((None, d, f), lambda i, j, k: (j, 0, 0)),
             out_spec=pl.BlockSpec((None, tm, f), lambda i, j, k: (j, i, 0)),
             out_shape=jax.ShapeDtypeStruct((nbu, t, f), BF16), dims=NN, acc_shape=(tm, f))
    up = up.reshape(2, nb, t, f)
    g = _ffn_fwd(f"{tag}_act", up, wdw, bdw, seq)
    r_out = _mm(f"{tag}_down", g, wd, grid=(t // tm, 1, nb),
                a_spec=pl.BlockSpec((None, tm, f), lambda i, j, k: (k, i, 0)),
                b_spec=pl.BlockSpec((f, d), lambda i, j, k: (k, 0)),
                out_spec=pl.BlockSpec((tm, d), lambda i, j, k: (i, 0)),
                out_shape=jax.ShapeDtypeStruct((t, d), F32), dims=NN, acc_shape=(tm, d),
                extras=(r_in,), extra_specs=(pl.BlockSpec((tm, d), lambda i, j, k: (i, 0)),),
                epilogue=lambda acc, res: res + acc)
    return r_out, (r_in, h, up, g)


def _ffn_backward(tag, dr, saved, gain, wu, wd, wdw, bdw, seq):
    r_in, h, up, g = saved
    t, d = r_in.shape
    nbu, _, f = wu.shape
    nb = nbu // 2
    tm = _tile(t, 512)
    dg = _mm(f"{tag}_dg", dr, wd, grid=(t // tm, nb, 1),
             a_spec=pl.BlockSpec((tm, d), lambda i, j, k: (i, 0)),
             b_spec=pl.BlockSpec((f, d), lambda i, j, k: (j, 0)),
             out_spec=pl.BlockSpec((None, tm, f), lambda i, j, k: (j, i, 0)),
             out_shape=jax.ShapeDtypeStruct((nb, t, f), BF16), dims=NT, acc_shape=(tm, f))
    dwd = _mm(f"{tag}_dwd", g, dr, grid=(nb, 1, t // tm),
              a_spec=pl.BlockSpec((None, tm, f), lambda i, j, k: (i, k, 0)),
              b_spec=pl.BlockSpec((tm, d), lambda i, j, k: (k, 0)),
              out_spec=pl.BlockSpec((f, d), lambda i, j, k: (i, 0)),
              out_shape=jax.ShapeDtypeStruct((nb * f, d), BF16), dims=TN, acc_shape=(f, d))
    dup, dwdw, dbdw = _ffn_bwd(f"{tag}_dact", up, dg, wdw, bdw, seq)
    dup = dup.reshape(nbu, t, f)
    dh = _mm(f"{tag}_dh", dup, wu, grid=(t // tm, 1, nbu),
             a_spec=pl.BlockSpec((None, tm, f), lambda i, j, k: (k, i, 0)),
             b_spec=pl.BlockSpec((None, d, f), lambda i, j, k: (k, 0, 0)),
             out_spec=pl.BlockSpec((tm, d), lambda i, j, k: (i, 0)),
             out_shape=jax.ShapeDtypeStruct((t, d), F32), dims=NT, acc_shape=(tm, d))
    dwu = _mm(f"{tag}_dwu", h, dup, grid=(nbu, 1, t // tm),
              a_spec=pl.BlockSpec((tm, d), lambda i, j, k: (k, 0)),
              b_spec=pl.BlockSpec((None, tm, f), lambda i, j, k: (i, k, 0)),
              out_spec=pl.BlockSpec((None, d, f), lambda i, j, k: (i, 0, 0)),
              out_shape=jax.ShapeDtypeStruct((nbu, d, f), BF16), dims=TN, acc_shape=(d, f))
    dr_in, dgain, colsum = _rms_bwd(f"{tag}_drms", dh, r_in, gain, dr)
    return dr_in, dgain, dwu, dwd, dwdw, dbdw, colsum


def _pad_to(vec, n):
    return jnp.pad(vec, (0, n - vec.shape[0]))


def _pack(parts, width):
    flat = jnp.concatenate([p.reshape(-1).astype(F32) for p in parts])
    n = -(-flat.shape[0] // (8 * width)) * (8 * width)
    return _pad_to(flat, n).reshape(n // width, width)


def _unpack(mat, shapes):
    flat = mat.reshape(-1)
    out, off = [], 0
    for s in shapes:
        n = 1
        for dim in s:
            n *= dim
        out.append(flat[off:off + n].reshape(s))
        off += n
    return out


def kernel(x, norm_mix, norm_ffn, conv_w_pw1, conv_b_pw1, conv_w_dw, conv_b_dw, conv_ln_g, conv_ln_b, conv_w_pw2, conv_b_pw2, pool_w, pool_b, pool_scale, ffn_w_up, ffn_w_dw, ffn_b_dw, ffn_w_down, final_norm, loss_target, m_norm_mix, m_norm_ffn, m_conv_w_pw1, m_conv_b_pw1, m_conv_w_dw, m_conv_b_dw, m_conv_ln_g, m_conv_ln_b, m_conv_w_pw2, m_conv_b_pw2, m_pool_w, m_pool_b, m_pool_scale, m_ffn_w_up, m_ffn_w_dw, m_ffn_b_dw, m_ffn_w_down, m_final_norm, v_norm_mix, v_norm_ffn, v_conv_w_pw1, v_conv_b_pw1, v_conv_w_dw, v_conv_b_dw, v_conv_ln_g, v_conv_ln_b, v_conv_w_pw2, v_conv_b_pw2, v_pool_w, v_pool_b, v_pool_scale, v_ffn_w_up, v_ffn_w_dw, v_ffn_b_dw, v_ffn_w_down, v_final_norm):
    bsz, seq, d = x.shape
    t = bsz * seq
    k_taps = conv_w_dw.shape[1]
    cs1 = conv_w_pw1.shape[2]
    dsh = d // N_DEV
    cg = d // N_GROUPS
    cgs = pool_w.shape[2]
    fu = ffn_w_up.shape[2]
    fd = ffn_w_down.shape[1]
    dff = fd * N_DEV
    nb = N_DEV // 2
    kf = ffn_w_dw.shape[1]
    fsh = ffn_w_dw.shape[2]
    my = _lin(_me())
    tm = _tile(t, 512)

    x2 = x.reshape(t, d)
    tgt2 = loss_target.reshape(t, d)

    small_shapes = [(k_taps, dsh), (dsh,), (dsh,), (2, kf, fsh)]
    small_mine = _pack([conv_w_dw[0], pool_b[0], pool_scale[0], ffn_w_dw], LANE)
    big = [conv_w_pw1[0], conv_w_pw2[0], pool_w[0], ffn_w_up[0], ffn_w_up[1], ffn_w_down[0], ffn_w_down[1]]
    w1, w2, wp, wu0, wu1, wd0, wd1, small_all = _exchange(
        "gather_weights", [w.astype(BF16) for w in big] + [small_mine], scatter=False)
    w2 = w2.reshape(d, d)
    wp = wp.transpose(1, 0, 2, 3).reshape(N_GROUPS, cg, cg)
    wd0 = wd0.reshape(dff, d)
    wd1 = wd1.reshape(dff, d)
    parts = [_unpack(small_all[dev], small_shapes) for dev in range(N_DEV)]
    wdw = jnp.concatenate([p[0] for p in parts], axis=1)
    pool_b_full = jnp.concatenate([p[1] for p in parts]).reshape(1, d)
    pool_s_full = jnp.concatenate([p[2] for p in parts]).reshape(1, d)
    fwdw = jnp.concatenate([p[3] for p in parts], axis=2)
    fwdw = fwdw.reshape(2, kf, nb, fu).transpose(0, 2, 1, 3)
    fbdw = ffn_b_dw.reshape(2, nb, 1, fu)

    h0 = _rms_fwd("l0_rms", x2, norm_mix[0:1])
    a = _mm("l0_pw1", h0, w1, grid=(t // tm, N_DEV, 1),
            a_spec=pl.BlockSpec((tm, d), lambda i, j, k: (i, 0)),
            b_spec=pl.BlockSpec((None, d, cs1), lambda i, j, k: (j, 0, 0)),
            out_spec=pl.BlockSpec((None, tm, cs1), lambda i, j, k: (j // nb, i, j % nb)),
            out_shape=jax.ShapeDtypeStruct((2, t, d), BF16), dims=NN, acc_shape=(tm, cs1),
            extras=(conv_b_pw1,), extra_specs=(pl.BlockSpec((1, cs1), lambda i, j, k: (0, j)),),
            epilogue=lambda acc, b: acc + b)
    v = _conv_fwd("l0_conv", a, wdw, conv_b_dw, seq)
    s = _ln_fwd("l0_ln", v, conv_ln_g, conv_ln_b)
    r1 = _mm("l0_pw2", s, w2, grid=(t // tm, 1, 1),
             a_spec=pl.BlockSpec((tm, d), lambda i, j, k: (i, 0)),
             b_spec=pl.BlockSpec((d, d), lambda i, j, k: (0, 0)),
             out_spec=pl.BlockSpec((tm, d), lambda i, j, k: (i, 0)),
             out_shape=jax.ShapeDtypeStruct((t, d), F32), dims=NN, acc_shape=(tm, d),
             extras=(conv_b_pw2, x2),
             extra_specs=(pl.BlockSpec((1, d), lambda i, j, k: (0, 0)), pl.BlockSpec((tm, d), lambda i, j, k: (i, 0))),
             epilogue=lambda acc, b, res: res + (acc + b))
    r2, ffn0_saved = _ffn_forward("f0", r1, norm_ffn[0:1], wu0, wd0, fwdw[0], fbdw[0], seq)
    pooled = _pool_fwd("l1_pool", r2, norm_mix[1:2], seq)
    r3 = _mm("l1_mix", pooled, wp, grid=(t // tm, N_GROUPS, 1),
             a_spec=pl.BlockSpec((tm, cg), lambda i, j, k: (i, j)),
             b_spec=pl.BlockSpec((None, cg, cg), lambda i, j, k: (j, 0, 0)),
             out_spec=pl.BlockSpec((tm, cg), lambda i, j, k: (i, j)),
             out_shape=jax.ShapeDtypeStruct((t, d), F32), dims=NN, acc_shape=(tm, cg),
             extras=(pool_s_full, pool_b_full, r2),
             extra_specs=(pl.BlockSpec((1, cg), lambda i, j, k: (0, j)), pl.BlockSpec((1, cg), lambda i, j, k: (0, j)),
                          pl.BlockSpec((tm, cg), lambda i, j, k: (i, j))),
             epilogue=lambda acc, sc, b, res: res + sc * (acc + b))
    r4, ffn1_saved = _ffn_forward("f1", r3, norm_ffn[1:2], wu1, wd1, fwdw[1], fbdw[1], seq)

    dr4, loss_part, dfinal, _ = _final("final", r4, tgt2, final_norm.reshape(1, d))
    dr3, dnf1, dwu1, dwd1, dfw1, dfb1, _ = _ffn_backward("f1", dr4, ffn1_saved, norm_ffn[1:2], wu1, wd1,
                                                         fwdw[1], fbdw[1], seq)
    dpooled, dmixed, dpool_s, dpool_b = _pool_bwd_mm("l1_dmix", pooled, wp, dr3, pool_s_full, pool_b_full)
    dwp = _mm("l1_dwp", pooled, dmixed, grid=(N_GROUPS, 1, t // tm),
              a_spec=pl.BlockSpec((tm, cg), lambda i, j, k: (k, i)),
              b_spec=pl.BlockSpec((tm, cg), lambda i, j, k: (k, i)),
              out_spec=pl.BlockSpec((None, cg, cg), lambda i, j, k: (i, 0, 0)),
              out_shape=jax.ShapeDtypeStruct((N_GROUPS, cg, cg), BF16), dims=TN, acc_shape=(cg, cg))
    dr2, dnm1 = _pool_bwd("l1_dpool", dpooled, r2, norm_mix[1:2], dr3, seq)
    dr1, dnf0, dwu0, dwd0, dfw0, dfb0, db2 = _ffn_backward("f0", dr2, ffn0_saved, norm_ffn[0:1], wu0, wd0,
                                                           fwdw[0], fbdw[0], seq)
    ds = _mm("l0_ds", dr1, w2, grid=(t // tm, 1, 1),
             a_spec=pl.BlockSpec((tm, d), lambda i, j, k: (i, 0)),
             b_spec=pl.BlockSpec((d, d), lambda i, j, k: (0, 0)),
             out_spec=pl.BlockSpec((tm, d), lambda i, j, k: (i, 0)),
             out_shape=jax.ShapeDtypeStruct((t, d), F32), dims=NT, acc_shape=(tm, d))
    dw2 = _mm("l0_dw2", s, dr1, grid=(1, 1, t // tm),
              a_spec=pl.BlockSpec((tm, d), lambda i, j, k: (k, 0)),
              b_spec=pl.BlockSpec((tm, d), lambda i, j, k: (k, 0)),
              out_spec=pl.BlockSpec((d, d), lambda i, j, k: (0, 0)),
              out_shape=jax.ShapeDtypeStruct((d, d), BF16), dims=TN, acc_shape=(d, d))
    dv, dlg, dlb, dbdw = _ln_bwd("l0_dln", ds, v, conv_ln_g, conv_ln_b)
    da, dwdw, db1 = _conv_bwd("l0_dconv", a, dv, wdw, seq)
    dh0 = _mm("l0_dh", da, w1, grid=(t // tm, 1, N_DEV),
              a_spec=pl.BlockSpec((None, tm, cs1), lambda i, j, k: (k // nb, i, k % nb)),
              b_spec=pl.BlockSpec((None, d, cs1), lambda i, j, k: (k, 0, 0)),
              out_spec=pl.BlockSpec((tm, d), lambda i, j, k: (i, 0)),
              out_shape=jax.ShapeDtypeStruct((t, d), F32), dims=NT, acc_shape=(tm, d))
    dw1 = _mm("l0_dw1", h0, da, grid=(N_DEV, 1, t // tm),
              a_spec=pl.BlockSpec((tm, d), lambda i, j, k: (k, 0)),
              b_spec=pl.BlockSpec((None, tm, cs1), lambda i, j, k: (i // nb, k, i % nb)),
              out_spec=pl.BlockSpec((None, d, cs1), lambda i, j, k: (i, 0, 0)),
              out_shape=jax.ShapeDtypeStruct((N_DEV, d, cs1), BF16), dims=TN, acc_shape=(d, cs1))
    dx, dnm0, _ = _rms_bwd("l0_drms", dh0, x2, norm_mix[0:1], dr1)

    dwp_b = dwp.reshape(N_GROUPS, N_DEV, cgs, cg).transpose(1, 0, 2, 3)
    g_w1, g_w2, g_wp, g_wu0, g_wu1, g_wd0, g_wd1 = _exchange(
        "scatter_grads",
        [dw1, dw2.reshape(N_DEV, d // N_DEV, d), dwp_b, dwu0, dwu1,
         dwd0.reshape(N_DEV, fd, d), dwd1.reshape(N_DEV, fd, d)], scatter=True)

    dffn_w = jnp.stack([dfw0, dfw1]).transpose(0, 2, 1, 3).reshape(2, kf, dff)
    dffn_b = jnp.stack([dfb0, dfb1]).reshape(2, dff)
    small_parts = [loss_part, jnp.concatenate([dnm0, dnm1]), jnp.concatenate([dnf0, dnf1]), db1, dwdw, dbdw, dlg, dlb,
                   db2, dpool_b, dpool_s, dffn_w, dffn_b, dfinal]
    small_part_shapes = [(1,), (2, d), (2, d), (1, 2 * d), (k_taps, d), (1, d), (1, d), (1, d), (1, d), (1, d), (1, d),
                         (2, kf, dff), (2, dff), (d,)]
    packed = _pack(small_parts, 8 * LANE)
    (all_small,) = _exchange("gather_small_grads", [packed], scatter=False)
    summed = _sum_rows("sum_small_grads", all_small)
    (loss_v, g_nm, g_nf, g_b1, g_wdw, g_bdw, g_lg, g_lb, g_b2, g_pb, g_ps, g_fw, g_fb,
     g_fin) = _unpack(summed, small_part_shapes)
    loss = loss_v[0]
    g_wdw_mine = lax.dynamic_slice_in_dim(g_wdw, my * dsh, dsh, axis=1)[None]
    g_pb_mine = lax.dynamic_slice_in_dim(g_pb, my * dsh, dsh, axis=1)
    g_ps_mine = lax.dynamic_slice_in_dim(g_ps, my * dsh, dsh, axis=1)
    g_fw_mine = lax.dynamic_slice_in_dim(g_fw, my * fsh, fsh, axis=2)

    def big_update(name, recv, w, m, v):
        shape = w.shape
        r, c = recv.shape[-2], recv.shape[-1]
        rows = recv.size // (N_DEV * c)
        outs = _adamw(name, recv.reshape(N_DEV, rows, c), w.reshape(rows, c), m.reshape(rows, c), v.reshape(rows, c))
        return [o.reshape(shape) for o in outs]

    u_w1 = big_update("adam_w1", g_w1, conv_w_pw1, m_conv_w_pw1, v_conv_w_pw1)
    u_w2 = big_update("adam_w2", g_w2, conv_w_pw2, m_conv_w_pw2, v_conv_w_pw2)
    u_wp = big_update("adam_wp", g_wp, pool_w, m_pool_w, v_pool_w)
    u_wu = big_update("adam_wu", jnp.stack([g_wu0, g_wu1], axis=1), ffn_w_up, m_ffn_w_up, v_ffn_w_up)
    u_wd = big_update("adam_wd", jnp.stack([g_wd0, g_wd1], axis=1), ffn_w_down, m_ffn_w_down, v_ffn_w_down)

    small_g = [g_nm, g_nf, g_b1, g_wdw_mine, g_bdw, g_lg, g_lb, g_b2, g_pb_mine, g_ps_mine, g_fw_mine, g_fb, g_fin]
    small_w = [norm_mix, norm_ffn, conv_b_pw1, conv_w_dw, conv_b_dw, conv_ln_g, conv_ln_b, conv_b_pw2, pool_b,
               pool_scale, ffn_w_dw, ffn_b_dw, final_norm]
    small_m = [m_norm_mix, m_norm_ffn, m_conv_b_pw1, m_conv_w_dw, m_conv_b_dw, m_conv_ln_g, m_conv_ln_b,
               m_conv_b_pw2, m_pool_b, m_pool_scale, m_ffn_w_dw, m_ffn_b_dw, m_final_norm]
    small_v = [v_norm_mix, v_norm_ffn, v_conv_b_pw1, v_conv_w_dw, v_conv_b_dw, v_conv_ln_g, v_conv_ln_b,
               v_conv_b_pw2, v_pool_b, v_pool_scale, v_ffn_w_dw, v_ffn_b_dw, v_final_norm]
    shapes = [w.shape for w in small_w]
    outs = _adamw("adam_small", _pack(small_g, 8 * LANE)[None], _pack(small_w, 8 * LANE), _pack(small_m, 8 * LANE),
                  _pack(small_v, 8 * LANE))
    sg, sd, sm, sv = [_unpack(o, shapes) for o in outs]

    def leaf(kind):
        (nm, nf, b1, wdw_, bdw_, lg, lb, b2, pb, ps, fw, fb, fin) = (sg, sd, sm, sv)[kind]
        return [nm, nf, u_w1[kind], b1, wdw_, bdw_, lg, lb, u_w2[kind], b2, u_wp[kind], pb, ps, u_wu[kind], fw, fb,
                u_wd[kind], fin]

    return (loss, dx.reshape(bsz, seq, d), *leaf(0), *leaf(1), *leaf(2), *leaf(3))
```

```python
import functools

import jax
import jax.numpy as jnp
from jax import lax
from jax.experimental import pallas as pl
from jax.experimental.pallas import tpu as pltpu

F32 = jnp.float32
BF16 = jnp.bfloat16
MESH = pl.DeviceIdType.MESH
HBM = pl.BlockSpec(memory_space=pltpu.HBM)

N_DEV = 8
RMS_EPS = 1e-6
LN_EPS = 1e-5
POOL_WINDOWS = (2, 4, 8, 16)
N_GROUPS = len(POOL_WINDOWS)
ADAM_LR = 0.001
ADAM_B1 = 0.9
ADAM_B2 = 0.999
ADAM_EPS = 1e-08
ADAM_WD = 0.01
ADAM_STEP = 10

LANE = 128
HALO = 32
HALO16 = 16
VMEM_LIMIT = 56 * 1024 * 1024


def _params(*sem):
    return pltpu.CompilerParams(dimension_semantics=sem if sem else None, vmem_limit_bytes=VMEM_LIMIT)


def _tile(n, pref):
    for t in range(min(pref, n), 15, -1):
        if n % t == 0 and t % 16 == 0:
            return t
    return n


def _sigmoid(z):
    return 1.0 / (1.0 + jnp.exp(-z))


def _me():
    return lax.axis_index("x"), lax.axis_index("y"), lax.axis_index("c")


def _flip(pos, m):
    x, y, c = pos
    return ((1 - x) if m & 4 else x, (1 - y) if m & 2 else y, (1 - c) if m & 1 else c)


def _lin(pos):
    return 4 * pos[0] + 2 * pos[1] + pos[2]


SEM = pl.BlockSpec(memory_space=pltpu.SEMAPHORE)
ANY = pl.BlockSpec(memory_space=pl.ANY)
EFFECT = pltpu.SideEffectType.DATAFLOW_SIDE_EFFECTING


def _exchange_copies(srcs, lands, send_sems, recv_sems, modes, which, starting):
    me = _me()
    my = _lin(me)
    out = []
    for pos, a in enumerate(which):
        src, land = srcs[pos], lands[pos]

        def block(pid, src=src, a=a):
            return src.at[pid] if modes[a] else src

        local = pltpu.make_async_copy(block(my), land.at[my], send_sems.at[a * N_DEV])
        remote = []
        for m in range(1, N_DEV):
            peer = _flip(me, m)
            pid = _lin(peer)
            sems = dict(send_sem=send_sems.at[a * N_DEV + m], recv_sem=recv_sems.at[a * N_DEV + m],
                        device_id=peer, device_id_type=MESH)
            if starting:
                remote.append(pltpu.make_async_remote_copy(src_ref=block(pid), dst_ref=land.at[my], **sems))
            else:
                remote.append((pltpu.make_async_remote_copy(src_ref=block(pid), dst_ref=land.at[my], **sems),
                               pltpu.make_async_remote_copy(src_ref=block(pid), dst_ref=land.at[pid], **sems)))
        out.append((local, remote))
    return out


def _exchange_start(name, arrs, modes):
    n = len(arrs)
    blocks = [a.shape[1:] if md else a.shape for a, md in zip(arrs, modes)]

    def body(*refs):
        srcs, lands = refs[:n], refs[n:2 * n]
        send_sems, recv_sems = refs[2 * n], refs[2 * n + 1]
        token = refs[-1]
        for local, remote in _exchange_copies(srcs, lands, send_sems, recv_sems, modes, list(range(n)), True):
            local.start()
            for send in remote:
                send.start()
        token[...] = jnp.zeros_like(token)

    lands = [lax.empty((N_DEV,) + tuple(b), a.dtype) for a, b in zip(arrs, blocks)]
    outs = pl.pallas_call(
        body, name=name,
        out_shape=(pltpu.SemaphoreType.DMA((n * N_DEV,)), pltpu.SemaphoreType.DMA((n * N_DEV,)),
                   *[pltpu.HBM(a.shape, a.dtype) for a in arrs], *[pltpu.HBM(l.shape, l.dtype) for l in lands],
                   jax.ShapeDtypeStruct((8, LANE), F32)),
        in_specs=[HBM] * (2 * n),
        out_specs=(SEM, SEM, *[HBM] * (2 * n), pl.BlockSpec(memory_space=pltpu.VMEM)),
        input_output_aliases={i: 2 + i for i in range(2 * n)},
        compiler_params=pltpu.CompilerParams(has_side_effects=EFFECT),
    )(*[pltpu.with_memory_space_constraint(a, pltpu.HBM) for a in arrs],
      *[pltpu.with_memory_space_constraint(l, pltpu.HBM) for l in lands])
    return dict(send=outs[0], recv=outs[1], srcs=list(outs[2:2 + n]), lands=list(outs[2 + n:2 + 2 * n]),
                modes=modes, token=outs[-1])


def _exchange_wait(name, handle, which, after):
    k = len(which)
    modes = handle["modes"]

    def body(*refs):
        srcs, lands = refs[:k], refs[k:2 * k]
        send_sems, recv_sems = refs[2 * k], refs[2 * k + 1]
        for local, remote in _exchange_copies(srcs, lands, send_sems, recv_sems, modes, which, False):
            local.wait()
            for send, arrival in remote:
                send.wait_send()
                arrival.wait_recv()

    srcs = [handle["srcs"][a] for a in which]
    lands = [handle["lands"][a] for a in which]
    outs = pl.pallas_call(
        body, name=name,
        out_shape=tuple(pltpu.HBM(x.shape, x.dtype) for x in srcs + lands),
        in_specs=[HBM] * (2 * k) + [SEM, SEM, ANY], out_specs=tuple([HBM] * (2 * k)),
        input_output_aliases={i: i for i in range(2 * k)},
        compiler_params=pltpu.CompilerParams(has_side_effects=EFFECT),
    )(*srcs, *lands, handle["send"], handle["recv"], after)
    return list(outs[k:])


def _mm(name, a, b, *, grid, a_spec, b_spec, out_spec, out_shape, dims, acc_shape, extras=(), extra_specs=(),
        epilogue=None, token=None):
    nk = grid[2]
    ne = len(extras)
    deps = () if token is None else (token,)
    dep_specs = [pl.BlockSpec((8, LANE), lambda i, j, k: (0, 0))] * len(deps)

    def body(a_ref, b_ref, *rest):
        ex, o_ref, acc_ref = rest[:ne], rest[ne + len(deps)], rest[ne + len(deps) + 1]
        k = pl.program_id(2)
        part = lax.dot_general(a_ref[...].astype(BF16), b_ref[...].astype(BF16), (dims, ((), ())),
                               preferred_element_type=F32)

        def finish(r):
            if epilogue is not None:
                r = epilogue(r, *[e[...] for e in ex])
            o_ref[...] = r.astype(o_ref.dtype)

        if nk == 1:
            finish(part)
            return

        @pl.when(k == 0)
        def _():
            acc_ref[...] = part

        @pl.when((k > 0) & (k < nk - 1))
        def _():
            acc_ref[...] += part

        @pl.when(k == nk - 1)
        def _():
            finish(acc_ref[...] + part)

    return pl.pallas_call(
        body, name=name, grid=grid, in_specs=[a_spec, b_spec, *extra_specs, *dep_specs], out_specs=out_spec,
        out_shape=out_shape, scratch_shapes=[pltpu.VMEM(acc_shape if nk > 1 else (8, LANE), F32)],
        compiler_params=_params("parallel", "parallel", "arbitrary"),
    )(a, b, *extras, *deps)


NN = ((1,), (0,))
NT = ((1,), (1,))
TN = ((0,), (0,))


def _rms_fwd(name, x, gain):
    t, d = x.shape
    tr = _tile(t, 512)

    def body(x_ref, g_ref, h_ref):
        xv = x_ref[...]
        rstd = lax.rsqrt(jnp.mean(xv * xv, axis=-1, keepdims=True) + RMS_EPS)
        h_ref[...] = (xv * rstd * g_ref[...]).astype(BF16)

    return pl.pallas_call(
        body, name=name, grid=(t // tr,),
        in_specs=[pl.BlockSpec((tr, d), lambda i: (i, 0)), pl.BlockSpec((1, d), lambda i: (0, 0))],
        out_specs=pl.BlockSpec((tr, d), lambda i: (i, 0)),
        out_shape=jax.ShapeDtypeStruct((t, d), BF16), compiler_params=_params("parallel"),
    )(x, gain)


def _rms_bwd(name, dh, x, gain, dres):
    t, d = x.shape
    tr = _tile(t, 512)

    def body(dh_ref, x_ref, g_ref, dres_ref, dx_ref, dg_ref, cs_ref):
        i = pl.program_id(0)
        xv = x_ref[...]
        rstd = lax.rsqrt(jnp.mean(xv * xv, axis=-1, keepdims=True) + RMS_EPS)
        xhat = xv * rstd
        dhv = dh_ref[...]
        dxhat = dhv * g_ref[...]
        dx = dres_ref[...] + rstd * (dxhat - xhat * jnp.mean(dxhat * xhat, axis=-1, keepdims=True))
        dx_ref[...] = dx

        @pl.when(i == 0)
        def _():
            dg_ref[...] = jnp.zeros_like(dg_ref)
            cs_ref[...] = jnp.zeros_like(cs_ref)

        dg_ref[...] += jnp.sum(dhv * xhat, axis=0, keepdims=True)
        cs_ref[...] += jnp.sum(dx, axis=0, keepdims=True)

    row = pl.BlockSpec((tr, d), lambda i: (i, 0))
    vec = pl.BlockSpec((1, d), lambda i: (0, 0))
    return pl.pallas_call(
        body, name=name, grid=(t // tr,), in_specs=[row, row, vec, row], out_specs=[row, vec, vec],
        out_shape=[jax.ShapeDtypeStruct((t, d), F32), jax.ShapeDtypeStruct((1, d), F32),
                   jax.ShapeDtypeStruct((1, d), F32)],
        compiler_params=_params("arbitrary"),
    )(dh, x, gain, dres)


def _final(name, x, tgt, gain):
    t, d = x.shape
    tr = _tile(t, 512)

    def body(x_ref, t_ref, g_ref, dx_ref, loss_ref, dg_ref, cs_ref):
        i = pl.program_id(0)
        xv = x_ref[...]
        g = g_ref[...]
        rstd = lax.rsqrt(jnp.mean(xv * xv, axis=-1, keepdims=True) + RMS_EPS)
        xhat = xv * rstd
        err = xhat * g - t_ref[...]
        dy = err / d
        dxhat = dy * g
        dx = rstd * (dxhat - xhat * jnp.mean(dxhat * xhat, axis=-1, keepdims=True))
        dx_ref[...] = dx

        @pl.when(i == 0)
        def _():
            loss_ref[...] = jnp.zeros_like(loss_ref)
            dg_ref[...] = jnp.zeros_like(dg_ref)
            cs_ref[...] = jnp.zeros_like(cs_ref)

        loss_ref[...] += 0.5 * jnp.sum(jnp.mean(err * err, axis=-1, keepdims=True), axis=0, keepdims=True)
        dg_ref[...] += jnp.sum(dy * xhat, axis=0, keepdims=True)
        cs_ref[...] += jnp.sum(dx, axis=0, keepdims=True)

    row = pl.BlockSpec((tr, d), lambda i: (i, 0))
    vec = pl.BlockSpec((1, d), lambda i: (0, 0))
    one = pl.BlockSpec((1, 1), lambda i: (0, 0))
    return pl.pallas_call(
        body, name=name, grid=(t // tr,), in_specs=[row, row, vec], out_specs=[row, one, vec, vec],
        out_shape=[jax.ShapeDtypeStruct((t, d), F32), jax.ShapeDtypeStruct((1, 1), F32),
                   jax.ShapeDtypeStruct((1, d), F32), jax.ShapeDtypeStruct((1, d), F32)],
        compiler_params=_params("arbitrary"),
    )(x, tgt, gain)


def _ln_fwd(name, v, g, b):
    t, d = v.shape
    tr = _tile(t, 512)

    def body(v_ref, g_ref, b_ref, s_ref):
        vv = v_ref[...]
        mu = jnp.mean(vv, axis=-1, keepdims=True)
        cen = vv - mu
        rstd = lax.rsqrt(jnp.mean(cen * cen, axis=-1, keepdims=True) + LN_EPS)
        z = cen * rstd * g_ref[...] + b_ref[...]
        s_ref[...] = (z * _sigmoid(z)).astype(BF16)

    row = pl.BlockSpec((tr, d), lambda i: (i, 0))
    vec = pl.BlockSpec((1, d), lambda i: (0, 0))
    return pl.pallas_call(
        body, name=name, grid=(t // tr,), in_specs=[row, vec, vec], out_specs=row,
        out_shape=jax.ShapeDtypeStruct((t, d), BF16), compiler_params=_params("parallel"),
    )(v, g, b)


def _ln_bwd(name, ds, v, g, b):
    t, d = v.shape
    tr = _tile(t, 512)

    def body(ds_ref, v_ref, g_ref, b_ref, dv_ref, dg_ref, db_ref, cs_ref):
        i = pl.program_id(0)
        vv = v_ref[...]
        g = g_ref[...]
        mu = jnp.mean(vv, axis=-1, keepdims=True)
        cen = vv - mu
        rstd = lax.rsqrt(jnp.mean(cen * cen, axis=-1, keepdims=True) + LN_EPS)
        y = cen * rstd
        z = y * g + b_ref[...]
        sig = _sigmoid(z)
        dz = ds_ref[...] * sig * (1.0 + z * (1.0 - sig))
        dy = dz * g
        dv = rstd * (dy - jnp.mean(dy, axis=-1, keepdims=True) - y * jnp.mean(dy * y, axis=-1, keepdims=True))
        dv_ref[...] = dv

        @pl.when(i == 0)
        def _():
            dg_ref[...] = jnp.zeros_like(dg_ref)
            db_ref[...] = jnp.zeros_like(db_ref)
            cs_ref[...] = jnp.zeros_like(cs_ref)

        dg_ref[...] += jnp.sum(dz * y, axis=0, keepdims=True)
        db_ref[...] += jnp.sum(dz, axis=0, keepdims=True)
        cs_ref[...] += jnp.sum(dv, axis=0, keepdims=True)

    row = pl.BlockSpec((tr, d), lambda i: (i, 0))
    vec = pl.BlockSpec((1, d), lambda i: (0, 0))
    return pl.pallas_call(
        body, name=name, grid=(t // tr,), in_specs=[row, row, vec, vec], out_specs=[row, vec, vec, vec],
        out_shape=[jax.ShapeDtypeStruct((t, d), F32)] + [jax.ShapeDtypeStruct((1, d), F32)] * 3,
        compiler_params=_params("arbitrary"),
    )(ds, v, g, b)


def _conv_tiles(t, seq):
    ts = _tile(seq, 512)
    return ts, seq // ts, _tile(ts, 64)


def _conv_fwd(name, a, w, b, seq):
    _, t, d = a.shape
    k_taps = w.shape[0]
    ts, tps, rc = _conv_tiles(t, seq)
    hb = ts // HALO

    def body(cur_ref, prev_ref, w_ref, b_ref, v_ref, upad):
        i = pl.program_id(1)
        first = (i % tps) == 0
        pv = prev_ref[0].astype(F32)
        pg = prev_ref[1].astype(F32)
        upad[0:HALO, :] = jnp.where(first, 0.0, pv * _sigmoid(pg))
        upad[HALO:HALO + ts, :] = cur_ref[0].astype(F32) * _sigmoid(cur_ref[1].astype(F32))
        wv = w_ref[...]
        bias = jnp.broadcast_to(b_ref[...], (rc, LANE))
        for r0 in range(0, ts, rc):
            acc = bias
            for k in range(k_taps):
                acc = acc + wv[k:k + 1, :] * upad[pl.ds(HALO - (k_taps - 1) + k + r0, rc), :]
            v_ref[pl.ds(r0, rc), :] = acc

    return pl.pallas_call(
        body, name=name, grid=(d // LANE, t // ts),
        in_specs=[pl.BlockSpec((2, ts, LANE), lambda c, i: (0, i, c)),
                  pl.BlockSpec((2, HALO, LANE), lambda c, i: (0, jnp.maximum(i * hb - 1, 0), c)),
                  pl.BlockSpec((k_taps, LANE), lambda c, i: (0, c)),
                  pl.BlockSpec((1, LANE), lambda c, i: (0, c))],
        out_specs=pl.BlockSpec((ts, LANE), lambda c, i: (i, c)),
        out_shape=jax.ShapeDtypeStruct((t, d), F32),
        scratch_shapes=[pltpu.VMEM((HALO + ts, LANE), F32)],
        compiler_params=_params("parallel", "parallel"),
    )(a, a, w, b)


def _conv_bwd(name, a, dv, w, seq):
    _, t, d = a.shape
    k_taps = w.shape[0]
    ts, tps, rc = _conv_tiles(t, seq)
    hb = ts // HALO
    nhb = t // HALO

    def body(cur_ref, prev_ref, dv_ref, ndv_ref, w_ref, da_ref, dw_ref, dbp_ref, upad, dvpad, dwrows):
        i = pl.program_id(1)
        first = (i % tps) == 0
        last = (i % tps) == tps - 1
        pv = prev_ref[0].astype(F32)
        pg = prev_ref[1].astype(F32)
        upad[0:HALO, :] = jnp.where(first, 0.0, pv * _sigmoid(pg))
        upad[HALO:HALO + ts, :] = cur_ref[0].astype(F32) * _sigmoid(cur_ref[1].astype(F32))
        dvpad[0:ts, :] = dv_ref[...]
        dvpad[ts:ts + HALO, :] = jnp.where(last, 0.0, ndv_ref[...])
        wv = w_ref[...]

        @pl.when(i == 0)
        def _():
            dw_ref[...] = jnp.zeros_like(dw_ref)
            dbp_ref[...] = jnp.zeros_like(dbp_ref)

        sv = jnp.zeros((1, LANE), F32)
        sg = jnp.zeros((1, LANE), F32)
        for r0 in range(0, ts, rc):
            du = jnp.zeros((rc, LANE), F32)
            for k in range(k_taps):
                du = du + wv[k:k + 1, :] * dvpad[pl.ds(r0 + (k_taps - 1) - k, rc), :]
            av = cur_ref[0, pl.ds(r0, rc), :].astype(F32)
            sig = _sigmoid(cur_ref[1, pl.ds(r0, rc), :].astype(F32))
            dval = du * sig
            dgate = du * av * sig * (1.0 - sig)
            da_ref[0, pl.ds(r0, rc), :] = dval.astype(BF16)
            da_ref[1, pl.ds(r0, rc), :] = dgate.astype(BF16)
            sv = sv + jnp.sum(dval, axis=0, keepdims=True)
            sg = sg + jnp.sum(dgate, axis=0, keepdims=True)
        dbp_ref[0] += sv
        dbp_ref[1] += sg

        for k in range(k_taps):
            acc = jnp.zeros((rc, LANE), F32)
            for r0 in range(0, ts, rc):
                acc = acc + dvpad[pl.ds(r0, rc), :] * upad[pl.ds(HALO - (k_taps - 1) + k + r0, rc), :]
            dwrows[k:k + 1, :] = jnp.sum(acc, axis=0, keepdims=True)
        dw_ref[...] += dwrows[0:k_taps, :]

    return pl.pallas_call(
        body, name=name, grid=(d // LANE, t // ts),
        in_specs=[pl.BlockSpec((2, ts, LANE), lambda c, i: (0, i, c)),
                  pl.BlockSpec((2, HALO, LANE), lambda c, i: (0, jnp.maximum(i * hb - 1, 0), c)),
                  pl.BlockSpec((ts, LANE), lambda c, i: (i, c)),
                  pl.BlockSpec((HALO, LANE), lambda c, i: (jnp.minimum((i + 1) * hb, nhb - 1), c)),
                  pl.BlockSpec((k_taps, LANE), lambda c, i: (0, c))],
        out_specs=[pl.BlockSpec((2, ts, LANE), lambda c, i: (0, i, c)),
                   pl.BlockSpec((k_taps, LANE), lambda c, i: (0, c)),
                   pl.BlockSpec((2, 1, LANE), lambda c, i: (0, 0, c))],
        out_shape=[jax.ShapeDtypeStruct((2, t, d), BF16), jax.ShapeDtypeStruct((k_taps, d), F32),
                   jax.ShapeDtypeStruct((2, 1, d), F32)],
        scratch_shapes=[pltpu.VMEM((HALO + ts, LANE), F32), pltpu.VMEM((ts + HALO, LANE), F32),
                        pltpu.VMEM((HALO, LANE), F32)],
        compiler_params=_params("parallel", "arbitrary"),
    )(a, a, dv, dv, w)


def _pool_fwd(name, x, gain, seq):
    t, d = x.shape
    ts = _tile(seq, 256)
    tps = seq // ts
    hb = ts // HALO
    cg = d // N_GROUPS

    def body(cur_ref, prev_ref, g_ref, o_ref, hpad):
        i = pl.program_id(0)
        first = (i % tps) == 0
        g = g_ref[...]

        def norm(xv):
            return xv * lax.rsqrt(jnp.mean(xv * xv, axis=-1, keepdims=True) + RMS_EPS) * g

        hpad[0:HALO, :] = jnp.where(first, 0.0, norm(prev_ref[...]))
        hpad[HALO:HALO + ts, :] = norm(cur_ref[...])
        pos = (i % tps) * ts + lax.broadcasted_iota(jnp.int32, (ts, 1), 0)
        for gi, win in enumerate(POOL_WINDOWS):
            sl = slice(gi * cg, (gi + 1) * cg)
            own = hpad[HALO:HALO + ts, sl]
            acc = own
            for j in range(1, win):
                acc = acc + hpad[HALO - j:HALO - j + ts, sl]
            cnt = jnp.minimum(pos + 1, win).astype(F32)
            o_ref[:, sl] = (acc / cnt - own).astype(BF16)

    return pl.pallas_call(
        body, name=name, grid=(t // ts,),
        in_specs=[pl.BlockSpec((ts, d), lambda i: (i, 0)),
                  pl.BlockSpec((HALO, d), lambda i: (jnp.maximum(i * hb - 1, 0), 0)),
                  pl.BlockSpec((1, d), lambda i: (0, 0))],
        out_specs=pl.BlockSpec((ts, d), lambda i: (i, 0)),
        out_shape=jax.ShapeDtypeStruct((t, d), BF16),
        scratch_shapes=[pltpu.VMEM((HALO + ts, d), F32)],
        compiler_params=_params("parallel"),
    )(x, x, gain)


def _pool_bwd_mm(name, pooled, wp, dp, scale, bias):
    t, d = pooled.shape
    cg = d // N_GROUPS
    tm = _tile(t, 512)

    def body(p_ref, w_ref, dp_ref, s_ref, b_ref, dpo_ref, dmx_ref, ds_ref, db_ref):
        i = pl.program_id(1)
        wv = w_ref[...]
        mixed = jnp.dot(p_ref[...], wv, preferred_element_type=F32)
        dpv = dp_ref[...]
        dmx = dpv * s_ref[...]
        dmx16 = dmx.astype(BF16)
        dmx_ref[...] = dmx16
        dpo_ref[...] = lax.dot_general(dmx16, wv, (NT, ((), ())), preferred_element_type=F32)

        @pl.when(i == 0)
        def _():
            ds_ref[...] = jnp.zeros_like(ds_ref)
            db_ref[...] = jnp.zeros_like(db_ref)

        ds_ref[...] += jnp.sum(dpv * (mixed + b_ref[...]), axis=0, keepdims=True)
        db_ref[...] += jnp.sum(dmx, axis=0, keepdims=True)

    blk = pl.BlockSpec((tm, cg), lambda g, i: (i, g))
    vec = pl.BlockSpec((1, cg), lambda g, i: (0, g))
    return pl.pallas_call(
        body, name=name, grid=(N_GROUPS, t // tm),
        in_specs=[blk, pl.BlockSpec((None, cg, cg), lambda g, i: (g, 0, 0)), blk, vec, vec],
        out_specs=[blk, blk, vec, vec],
        out_shape=[jax.ShapeDtypeStruct((t, d), F32), jax.ShapeDtypeStruct((t, d), BF16),
                   jax.ShapeDtypeStruct((1, d), F32), jax.ShapeDtypeStruct((1, d), F32)],
        compiler_params=_params("parallel", "arbitrary"),
    )(pooled, wp, dp, scale, bias)


def _pool_bwd(name, dpooled, x, gain, dres, seq):
    t, d = x.shape
    ts = _tile(seq, 256)
    tps = seq // ts
    hb = ts // HALO
    nhb = t // HALO
    cg = d // N_GROUPS

    def body(dpo_ref, ndpo_ref, x_ref, g_ref, dres_ref, dx_ref, dg_ref, qpad, dh):
        i = pl.program_id(0)
        last = (i % tps) == tps - 1
        pos = (i % tps) * ts + lax.broadcasted_iota(jnp.int32, (ts, 1), 0)
        for gi, win in enumerate(POOL_WINDOWS):
            sl = slice(gi * cg, (gi + 1) * cg)
            cur = dpo_ref[:, sl]
            qpad[0:ts, sl] = cur / jnp.minimum(pos + 1, win).astype(F32)
            qpad[ts:ts + HALO, sl] = jnp.where(last, 0.0, ndpo_ref[:, sl] / float(win))
            acc = -cur
            for j in range(win):
                acc = acc + qpad[j:j + ts, sl]
            dh[:, sl] = acc
        xv = x_ref[...]
        rstd = lax.rsqrt(jnp.mean(xv * xv, axis=-1, keepdims=True) + RMS_EPS)
        xhat = xv * rstd
        dhv = dh[...]
        dxhat = dhv * g_ref[...]
        dx_ref[...] = dres_ref[...] + rstd * (dxhat - xhat * jnp.mean(dxhat * xhat, axis=-1, keepdims=True))

        @pl.when(i == 0)
        def _():
            dg_ref[...] = jnp.zeros_like(dg_ref)

        dg_ref[...] += jnp.sum(dhv * xhat, axis=0, keepdims=True)

    row = pl.BlockSpec((ts, d), lambda i: (i, 0))
    vec = pl.BlockSpec((1, d), lambda i: (0, 0))
    return pl.pallas_call(
        body, name=name, grid=(t // ts,),
        in_specs=[row, pl.BlockSpec((HALO, d), lambda i: (jnp.minimum((i + 1) * hb, nhb - 1), 0)), row, vec, row],
        out_specs=[row, vec],
        out_shape=[jax.ShapeDtypeStruct((t, d), F32), jax.ShapeDtypeStruct((1, d), F32)],
        scratch_shapes=[pltpu.VMEM((ts + HALO, d), F32), pltpu.VMEM((ts, d), F32)],
        compiler_params=_params("arbitrary"),
    )(dpooled, dpooled, x, gain, dres)


def _ffn_fwd(name, up, w, b, seq):
    _, nb, t, f = up.shape
    k_taps = w.shape[1]
    ts = _tile(seq, 256)
    tps = seq // ts
    hb = ts // HALO16

    def body(cur_ref, prev_ref, w_ref, b_ref, g_ref, apad):
        i = pl.program_id(1)
        first = (i % tps) == 0
        apad[0:HALO16, :] = jnp.where(first, 0.0, prev_ref[...].astype(F32))
        apad[HALO16:HALO16 + ts, :] = cur_ref[0].astype(F32)
        wv = w_ref[...]
        c = jnp.broadcast_to(b_ref[...], (ts, f))
        for k in range(k_taps):
            c = c + wv[k:k + 1, :] * apad[HALO16 - (k_taps - 1) + k:HALO16 - (k_taps - 1) + k + ts, :]
        g_ref[...] = (c * _sigmoid(c) * cur_ref[1].astype(F32)).astype(BF16)

    return pl.pallas_call(
        body, name=name, grid=(nb, t // ts),
        in_specs=[pl.BlockSpec((2, None, ts, f), lambda j, i: (0, j, i, 0)),
                  pl.BlockSpec((None, None, HALO16, f), lambda j, i: (0, j, jnp.maximum(i * hb - 1, 0), 0)),
                  pl.BlockSpec((None, k_taps, f), lambda j, i: (j, 0, 0)),
                  pl.BlockSpec((None, 1, f), lambda j, i: (j, 0, 0))],
        out_specs=pl.BlockSpec((None, ts, f), lambda j, i: (j, i, 0)),
        out_shape=jax.ShapeDtypeStruct((nb, t, f), BF16),
        scratch_shapes=[pltpu.VMEM((HALO16 + ts, f), F32)],
        compiler_params=_params("parallel", "parallel"),
    )(up, up, w, b)


def _ffn_bwd(name, up, dg, w, b, seq):
    _, nb, t, f = up.shape
    k_taps = w.shape[1]
    ts = _tile(seq, 256)
    tps = seq // ts
    hb = ts // HALO16
    nhb = t // HALO16
    ext = ts + HALO16

    def body(cur_ref, prev_ref, next_ref, dg_ref, ndg_ref, w_ref, b_ref, dup_ref, dw_ref, db_ref,
             apad, gpad, dgpad, dcpad, dwrows):
        i = pl.program_id(1)
        first = (i % tps) == 0
        last = (i % tps) == tps - 1
        apad[0:HALO16, :] = jnp.where(first, 0.0, prev_ref[...].astype(F32))
        apad[HALO16:HALO16 + ts, :] = cur_ref[0].astype(F32)
        apad[HALO16 + ts:HALO16 + ext, :] = next_ref[0].astype(F32)
        gpad[0:ts, :] = cur_ref[1].astype(F32)
        gpad[ts:ext, :] = next_ref[1].astype(F32)
        dgpad[0:ts, :] = dg_ref[...].astype(F32)
        dgpad[ts:ext, :] = jnp.where(last, 0.0, ndg_ref[...].astype(F32))
        wv = w_ref[...]
        c = jnp.broadcast_to(b_ref[...], (ext, f))
        for k in range(k_taps):
            c = c + wv[k:k + 1, :] * apad[HALO16 - (k_taps - 1) + k:HALO16 - (k_taps - 1) + k + ext, :]
        sig = _sigmoid(c)
        dgv = dgpad[...]
        dcpad[...] = dgv * gpad[...] * sig * (1.0 + c * (1.0 - sig))
        dup_ref[1] = (dgv[0:ts] * (c * sig)[0:ts]).astype(BF16)
        dact = jnp.zeros((ts, f), F32)
        for k in range(k_taps):
            dact = dact + wv[k:k + 1, :] * dcpad[(k_taps - 1) - k:(k_taps - 1) - k + ts, :]
        dup_ref[0] = dact.astype(BF16)

        @pl.when(i == 0)
        def _():
            dw_ref[...] = jnp.zeros_like(dw_ref)
            db_ref[...] = jnp.zeros_like(db_ref)

        dc = dcpad[0:ts, :]
        for k in range(k_taps):
            sh = apad[HALO16 - (k_taps - 1) + k:HALO16 - (k_taps - 1) + k + ts, :]
            dwrows[k:k + 1, :] = jnp.sum(dc * sh, axis=0, keepdims=True)
        dw_ref[...] += dwrows[0:k_taps, :]
        db_ref[...] += jnp.sum(dc, axis=0, keepdims=True)

    return pl.pallas_call(
        body, name=name, grid=(nb, t // ts),
        in_specs=[pl.BlockSpec((2, None, ts, f), lambda j, i: (0, j, i, 0)),
                  pl.BlockSpec((None, None, HALO16, f), lambda j, i: (0, j, jnp.maximum(i * hb - 1, 0), 0)),
                  pl.BlockSpec((2, None, HALO16, f), lambda j, i: (0, j, jnp.minimum((i + 1) * hb, nhb - 1), 0)),
                  pl.BlockSpec((None, ts, f), lambda j, i: (j, i, 0)),
                  pl.BlockSpec((None, HALO16, f), lambda j, i: (j, jnp.minimum((i + 1) * hb, nhb - 1), 0)),
                  pl.BlockSpec((None, k_taps, f), lambda j, i: (j, 0, 0)),
                  pl.BlockSpec((None, 1, f), lambda j, i: (j, 0, 0))],
        out_specs=[pl.BlockSpec((2, None, ts, f), lambda j, i: (0, j, i, 0)),
                   pl.BlockSpec((None, k_taps, f), lambda j, i: (j, 0, 0)),
                   pl.BlockSpec((None, 1, f), lambda j, i: (j, 0, 0))],
        out_shape=[jax.ShapeDtypeStruct((2, nb, t, f), BF16), jax.ShapeDtypeStruct((nb, k_taps, f), F32),
                   jax.ShapeDtypeStruct((nb, 1, f), F32)],
        scratch_shapes=[pltpu.VMEM((HALO16 + ext, f), F32), pltpu.VMEM((ext, f), F32), pltpu.VMEM((ext, f), F32),
                        pltpu.VMEM((ext, f), F32), pltpu.VMEM((8, f), F32)],
        compiler_params=_params("parallel", "arbitrary"),
    )(up, up, up, dg, dg, w, b)


def _sum_rows(name, g):
    ns, r, c = g.shape
    tr = _tile(r, 256)

    def body(g_ref, o_ref):
        acc = g_ref[0]
        for dev in range(1, ns):
            acc = acc + g_ref[dev]
        o_ref[...] = acc

    return pl.pallas_call(
        body, name=name, grid=(r // tr,),
        in_specs=[pl.BlockSpec((ns, tr, c), lambda i: (0, i, 0))],
        out_specs=pl.BlockSpec((tr, c), lambda i: (i, 0)),
        out_shape=jax.ShapeDtypeStruct((r, c), F32), compiler_params=_params("parallel"),
    )(g)


def _adamw(name, gsrc, w, m, v, layer=0, prev=None):
    ns, r, c = gsrc.shape
    nl = w.shape[0]
    tr = _tile(r, 256)
    prev = () if prev is None else tuple(prev)

    def body(g_ref, w_ref, m_ref, v_ref, *rest):
        go_ref, do_ref, mo_ref, vo_ref = rest[len(prev):]
        g = g_ref[0].astype(F32)
        for dev in range(1, ns):
            g = g + g_ref[dev].astype(F32)
        m_new = ADAM_B1 * m_ref[...] + (1.0 - ADAM_B1) * g
        v_new = ADAM_B2 * v_ref[...] + (1.0 - ADAM_B2) * (g * g)
        m_hat = m_new / (1.0 - ADAM_B1 ** ADAM_STEP)
        v_hat = v_new / (1.0 - ADAM_B2 ** ADAM_STEP)
        go_ref[...] = g
        do_ref[...] = -ADAM_LR * (m_hat / (jnp.sqrt(v_hat) + ADAM_EPS) + ADAM_WD * w_ref[...])
        mo_ref[...] = m_new
        vo_ref[...] = v_new

    row = pl.BlockSpec((None, tr, c), lambda i: (layer, i, 0))
    return pl.pallas_call(
        body, name=name, grid=(r // tr,),
        in_specs=[pl.BlockSpec((ns, tr, c), lambda i: (0, i, 0)), row, row, row] + [ANY] * len(prev),
        out_specs=[row] * 4, out_shape=[jax.ShapeDtypeStruct((nl, r, c), F32)] * 4,
        input_output_aliases={4 + i: i for i in range(len(prev))},
        compiler_params=_params("parallel"),
    )(gsrc, w, m, v, *prev)


def _ffn_forward(tag, r_in, gain, get_wu, get_wd, wdw, bdw, seq):
    t, d = r_in.shape
    tm = _tile(t, 512)
    h = _rms_fwd(f"{tag}_rms", r_in, gain)
    wu = get_wu(h)
    nbu, _, f = wu.shape
    nb = nbu // 2
    up = _mm(f"{tag}_up", h, wu, grid=(t // tm, nbu, 1),
             a_spec=pl.BlockSpec((tm, d), lambda i, j, k: (i, 0)),
             b_spec=pl.BlockSpec((None, d, f), lambda i, j, k: (j, 0, 0)),
             out_spec=pl.BlockSpec((None, tm, f), lambda i, j, k: (j, i, 0)),
             out_shape=jax.ShapeDtypeStruct((nbu, t, f), BF16), dims=NN, acc_shape=(tm, f))
    wd = get_wd(up)
    up = up.reshape(2, nb, t, f)
    g = _ffn_fwd(f"{tag}_act", up, wdw, bdw, seq)
    r_out = _mm(f"{tag}_down", g, wd, grid=(t // tm, 1, nb),
                a_spec=pl.BlockSpec((None, tm, f), lambda i, j, k: (k, i, 0)),
                b_spec=pl.BlockSpec((f, d), lambda i, j, k: (k, 0)),
                out_spec=pl.BlockSpec((tm, d), lambda i, j, k: (i, 0)),
                out_shape=jax.ShapeDtypeStruct((t, d), F32), dims=NN, acc_shape=(tm, d),
                extras=(r_in,), extra_specs=(pl.BlockSpec((tm, d), lambda i, j, k: (i, 0)),),
                epilogue=lambda acc, res: res + acc)
    return r_out, (r_in, h, up, g, wu, wd)


def _ffn_backward(tag, dr, saved, gain, wdw, bdw, seq, token=None):
    r_in, h, up, g, wu, wd = saved
    t, d = r_in.shape
    nbu, _, f = wu.shape
    nb = nbu // 2
    tm = _tile(t, 512)
    dg = _mm(f"{tag}_dg", dr, wd, grid=(t // tm, nb, 1),
             a_spec=pl.BlockSpec((tm, d), lambda i, j, k: (i, 0)),
             b_spec=pl.BlockSpec((f, d), lambda i, j, k: (j, 0)),
             out_spec=pl.BlockSpec((None, tm, f), lambda i, j, k: (j, i, 0)),
             out_shape=jax.ShapeDtypeStruct((nb, t, f), BF16), dims=NT, acc_shape=(tm, f), token=token)
    dwd = _mm(f"{tag}_dwd", g, dr, grid=(nb, 1, t // tm),
              a_spec=pl.BlockSpec((None, tm, f), lambda i, j, k: (i, k, 0)),
              b_spec=pl.BlockSpec((tm, d), lambda i, j, k: (k, 0)),
              out_spec=pl.BlockSpec((f, d), lambda i, j, k: (i, 0)),
              out_shape=jax.ShapeDtypeStruct((nb * f, d), BF16), dims=TN, acc_shape=(f, d))
    dup, dwdw, dbdw = _ffn_bwd(f"{tag}_dact", up, dg, wdw, bdw, seq)
    dup = dup.reshape(nbu, t, f)
    dh = _mm(f"{tag}_dh", dup, wu, grid=(t // tm, 1, nbu),
             a_spec=pl.BlockSpec((None, tm, f), lambda i, j, k: (k, i, 0)),
             b_spec=pl.BlockSpec((None, d, f), lambda i, j, k: (k, 0, 0)),
             out_spec=pl.BlockSpec((tm, d), lambda i, j, k: (i, 0)),
             out_shape=jax.ShapeDtypeStruct((t, d), F32), dims=NT, acc_shape=(tm, d))
    dwu = _mm(f"{tag}_dwu", h, dup, grid=(nbu, 1, t // tm),
              a_spec=pl.BlockSpec((tm, d), lambda i, j, k: (k, 0)),
              b_spec=pl.BlockSpec((None, tm, f), lambda i, j, k: (i, k, 0)),
              out_spec=pl.BlockSpec((None, d, f), lambda i, j, k: (i, 0, 0)),
              out_shape=jax.ShapeDtypeStruct((nbu, d, f), BF16), dims=TN, acc_shape=(d, f))
    dr_in, dgain, colsum = _rms_bwd(f"{tag}_drms", dh, r_in, gain, dr)
    return dr_in, dgain, dwu, dwd, dwdw, dbdw, colsum


def _pad_to(vec, n):
    return jnp.pad(vec, (0, n - vec.shape[0]))


def _pack(parts, width):
    flat = jnp.concatenate([p.reshape(-1).astype(F32) for p in parts])
    n = -(-flat.shape[0] // (8 * width)) * (8 * width)
    return _pad_to(flat, n).reshape(n // width, width)


def _unpack(mat, shapes):
    flat = mat.reshape(-1)
    out, off = [], 0
    for s in shapes:
        n = 1
        for dim in s:
            n *= dim
        out.append(flat[off:off + n].reshape(s))
        off += n
    return out


def kernel(x, norm_mix, norm_ffn, conv_w_pw1, conv_b_pw1, conv_w_dw, conv_b_dw, conv_ln_g, conv_ln_b, conv_w_pw2, conv_b_pw2, pool_w, pool_b, pool_scale, ffn_w_up, ffn_w_dw, ffn_b_dw, ffn_w_down, final_norm, loss_target, m_norm_mix, m_norm_ffn, m_conv_w_pw1, m_conv_b_pw1, m_conv_w_dw, m_conv_b_dw, m_conv_ln_g, m_conv_ln_b, m_conv_w_pw2, m_conv_b_pw2, m_pool_w, m_pool_b, m_pool_scale, m_ffn_w_up, m_ffn_w_dw, m_ffn_b_dw, m_ffn_w_down, m_final_norm, v_norm_mix, v_norm_ffn, v_conv_w_pw1, v_conv_b_pw1, v_conv_w_dw, v_conv_b_dw, v_conv_ln_g, v_conv_ln_b, v_conv_w_pw2, v_conv_b_pw2, v_pool_w, v_pool_b, v_pool_scale, v_ffn_w_up, v_ffn_w_dw, v_ffn_b_dw, v_ffn_w_down, v_final_norm):
    bsz, seq, d = x.shape
    t = bsz * seq
    k_taps = conv_w_dw.shape[1]
    cs1 = conv_w_pw1.shape[2]
    dsh = d // N_DEV
    cg = d // N_GROUPS
    cgs = pool_w.shape[2]
    fu = ffn_w_up.shape[2]
    fd = ffn_w_down.shape[1]
    dff = fd * N_DEV
    nb = N_DEV // 2
    kf = ffn_w_dw.shape[1]
    fsh = ffn_w_dw.shape[2]
    my = _lin(_me())
    tm = _tile(t, 512)

    x2 = x.reshape(t, d)
    tgt2 = loss_target.reshape(t, d)

    small_shapes = [(k_taps, dsh), (dsh,), (dsh,), (2, kf, fsh)]
    small_mine = _pack([conv_w_dw[0], pool_b[0], pool_scale[0], ffn_w_dw], LANE)
    big = [conv_w_pw1[0], conv_w_pw2[0], ffn_w_up[0], ffn_w_down[0], pool_w[0], ffn_w_up[1], ffn_w_down[1]]
    gather = _exchange_start("gather_start", [small_mine] + [w.astype(BF16) for w in big], [False] * 8)
    h0 = _rms_fwd("l0_rms", x2, norm_mix[0:1])
    small_all, w1 = _exchange_wait("gather_wait_w1", gather, [0, 1], h0)
    parts = [_unpack(small_all[dev], small_shapes) for dev in range(N_DEV)]
    wdw = jnp.concatenate([p[0] for p in parts], axis=1)
    pool_b_full = jnp.concatenate([p[1] for p in parts]).reshape(1, d)
    pool_s_full = jnp.concatenate([p[2] for p in parts]).reshape(1, d)
    fwdw = jnp.concatenate([p[3] for p in parts], axis=2)
    fwdw = fwdw.reshape(2, kf, nb, fu).transpose(0, 2, 1, 3)
    fbdw = ffn_b_dw.reshape(2, nb, 1, fu)

    a = _mm("l0_pw1", h0, w1, grid=(t // tm, N_DEV, 1),
            a_spec=pl.BlockSpec((tm, d), lambda i, j, k: (i, 0)),
            b_spec=pl.BlockSpec((None, d, cs1), lambda i, j, k: (j, 0, 0)),
            out_spec=pl.BlockSpec((None, tm, cs1), lambda i, j, k: (j // nb, i, j % nb)),
            out_shape=jax.ShapeDtypeStruct((2, t, d), BF16), dims=NN, acc_shape=(tm, cs1),
            extras=(conv_b_pw1,), extra_specs=(pl.BlockSpec((1, cs1), lambda i, j, k: (0, j)),),
            epilogue=lambda acc, b: acc + b)
    (w2,) = _exchange_wait("gather_wait_w2", gather, [2], a)
    w2 = w2.reshape(d, d)
    v = _conv_fwd("l0_conv", a, wdw, conv_b_dw, seq)
    s = _ln_fwd("l0_ln", v, conv_ln_g, conv_ln_b)
    r1 = _mm("l0_pw2", s, w2, grid=(t // tm, 1, 1),
             a_spec=pl.BlockSpec((tm, d), lambda i, j, k: (i, 0)),
             b_spec=pl.BlockSpec((d, d), lambda i, j, k: (0, 0)),
             out_spec=pl.BlockSpec((tm, d), lambda i, j, k: (i, 0)),
             out_shape=jax.ShapeDtypeStruct((t, d), F32), dims=NN, acc_shape=(tm, d),
             extras=(conv_b_pw2, x2),
             extra_specs=(pl.BlockSpec((1, d), lambda i, j, k: (0, 0)), pl.BlockSpec((tm, d), lambda i, j, k: (i, 0))),
             epilogue=lambda acc, b, res: res + (acc + b))
    def weight_getter(name, idx, shape):
        return lambda after: _exchange_wait(name, gather, [idx], after)[0].reshape(shape)

    r2, ffn0_saved = _ffn_forward("f0", r1, norm_ffn[0:1], weight_getter("gather_wait_wu0", 3, (N_DEV, d, fu)),
                                  weight_getter("gather_wait_wd0", 4, (dff, d)), fwdw[0], fbdw[0], seq)
    (wp,) = _exchange_wait("gather_wait_wp", gather, [5], r2)
    wp = wp.transpose(1, 0, 2, 3).reshape(N_GROUPS, cg, cg)
    pooled = _pool_fwd("l1_pool", r2, norm_mix[1:2], seq)
    r3 = _mm("l1_mix", pooled, wp, grid=(t // tm, N_GROUPS, 1),
             a_spec=pl.BlockSpec((tm, cg), lambda i, j, k: (i, j)),
             b_spec=pl.BlockSpec((None, cg, cg), lambda i, j, k: (j, 0, 0)),
             out_spec=pl.BlockSpec((tm, cg), lambda i, j, k: (i, j)),
             out_shape=jax.ShapeDtypeStruct((t, d), F32), dims=NN, acc_shape=(tm, cg),
             extras=(pool_s_full, pool_b_full, r2),
             extra_specs=(pl.BlockSpec((1, cg), lambda i, j, k: (0, j)), pl.BlockSpec((1, cg), lambda i, j, k: (0, j)),
                          pl.BlockSpec((tm, cg), lambda i, j, k: (i, j))),
             epilogue=lambda acc, sc, b, res: res + sc * (acc + b))
    r4, ffn1_saved = _ffn_forward("f1", r3, norm_ffn[1:2], weight_getter("gather_wait_wu1", 6, (N_DEV, d, fu)),
                                  weight_getter("gather_wait_wd1", 7, (dff, d)), fwdw[1], fbdw[1], seq)

    dr4, loss_part, dfinal, _ = _final("final", r4, tgt2, final_norm.reshape(1, d))
    dr3, dnf1, dwu1, dwd1, dfw1, dfb1, _ = _ffn_backward("f1", dr4, ffn1_saved, norm_ffn[1:2], fwdw[1], fbdw[1], seq)
    scatter_a = _exchange_start("scatter_f1_start", [dwu1, dwd1.reshape(N_DEV, fd, d)], [True, True])
    dpooled, dmixed, dpool_s, dpool_b = _pool_bwd_mm("l1_dmix", pooled, wp, dr3,
                                                     pool_s_full + scatter_a["token"][0:1, 0:1], pool_b_full)
    dwp = _mm("l1_dwp", pooled, dmixed, grid=(N_GROUPS, 1, t // tm),
              a_spec=pl.BlockSpec((tm, cg), lambda i, j, k: (k, i)),
              b_spec=pl.BlockSpec((tm, cg), lambda i, j, k: (k, i)),
              out_spec=pl.BlockSpec((None, cg, cg), lambda i, j, k: (i, 0, 0)),
              out_shape=jax.ShapeDtypeStruct((N_GROUPS, cg, cg), BF16), dims=TN, acc_shape=(cg, cg))
    dr2, dnm1 = _pool_bwd("l1_dpool", dpooled, r2, norm_mix[1:2], dr3, seq)
    dr1, dnf0, dwu0, dwd0, dfw0, dfb0, db2 = _ffn_backward("f0", dr2, ffn0_saved, norm_ffn[0:1], fwdw[0], fbdw[0], seq)
    dwp_b = dwp.reshape(N_GROUPS, N_DEV, cgs, cg).transpose(1, 0, 2, 3)
    scatter_b = _exchange_start("scatter_f0_start", [dwu0, dwd0.reshape(N_DEV, fd, d), dwp_b], [True] * 3)
    ds = _mm("l0_ds", dr1, w2, grid=(t // tm, 1, 1),
             a_spec=pl.BlockSpec((tm, d), lambda i, j, k: (i, 0)),
             b_spec=pl.BlockSpec((d, d), lambda i, j, k: (0, 0)),
             out_spec=pl.BlockSpec((tm, d), lambda i, j, k: (i, 0)),
             out_shape=jax.ShapeDtypeStruct((t, d), F32), dims=NT, acc_shape=(tm, d), token=scatter_b["token"])
    dw2 = _mm("l0_dw2", s, dr1, grid=(1, 1, t // tm),
              a_spec=pl.BlockSpec((tm, d), lambda i, j, k: (k, 0)),
              b_spec=pl.BlockSpec((tm, d), lambda i, j, k: (k, 0)),
              out_spec=pl.BlockSpec((d, d), lambda i, j, k: (0, 0)),
              out_shape=jax.ShapeDtypeStruct((d, d), BF16), dims=TN, acc_shape=(d, d))
    dv, dlg, dlb, dbdw = _ln_bwd("l0_dln", ds, v, conv_ln_g, conv_ln_b)
    da, dwdw, db1 = _conv_bwd("l0_dconv", a, dv, wdw, seq)
    dw1 = _mm("l0_dw1", h0, da, grid=(N_DEV, 1, t // tm),
              a_spec=pl.BlockSpec((tm, d), lambda i, j, k: (k, 0)),
              b_spec=pl.BlockSpec((None, tm, cs1), lambda i, j, k: (i // nb, k, i % nb)),
              out_spec=pl.BlockSpec((None, d, cs1), lambda i, j, k: (i, 0, 0)),
              out_shape=jax.ShapeDtypeStruct((N_DEV, d, cs1), BF16), dims=TN, acc_shape=(d, cs1))
    scatter_c = _exchange_start("scatter_l0_start", [dw1, dw2.reshape(N_DEV, d // N_DEV, d)], [True, True])
    dh0 = _mm("l0_dh", da, w1, grid=(t // tm, 1, N_DEV),
              a_spec=pl.BlockSpec((None, tm, cs1), lambda i, j, k: (k // nb, i, k % nb)),
              b_spec=pl.BlockSpec((None, d, cs1), lambda i, j, k: (k, 0, 0)),
              out_spec=pl.BlockSpec((tm, d), lambda i, j, k: (i, 0)),
              out_shape=jax.ShapeDtypeStruct((t, d), F32), dims=NT, acc_shape=(tm, d), token=scatter_c["token"])
    dx, dnm0, _ = _rms_bwd("l0_drms", dh0, x2, norm_mix[0:1], dr1)

    dffn_w = jnp.stack([dfw0, dfw1]).transpose(0, 2, 1, 3).reshape(2, kf, dff)
    dffn_b = jnp.stack([dfb0, dfb1]).reshape(2, dff)
    small_parts = [loss_part, jnp.concatenate([dnm0, dnm1]), jnp.concatenate([dnf0, dnf1]), db1, dwdw, dbdw, dlg, dlb,
                   db2, dpool_b, dpool_s, dffn_w, dffn_b, dfinal]
    small_part_shapes = [(1,), (2, d), (2, d), (1, 2 * d), (k_taps, d), (1, d), (1, d), (1, d), (1, d), (1, d), (1, d),
                         (2, kf, dff), (2, dff), (d,)]
    packed = _pack(small_parts, 8 * LANE)
    gather_small = _exchange_start("gather_small_start", [packed], [False])

    def big_update(name, recv, w, m, v, layer=0, prev=None):
        shape = w.shape
        c = recv.shape[-1]
        rows = recv.size // (N_DEV * c)
        nl = w.size // (rows * c)
        outs = _adamw(name, recv.reshape(N_DEV, rows, c), w.reshape(nl, rows, c), m.reshape(nl, rows, c),
                      v.reshape(nl, rows, c), layer, prev)
        return outs, [o.reshape(shape) for o in outs]

    g_wu1, g_wd1 = _exchange_wait("scatter_f1_wait", scatter_a, [0, 1], gather_small["token"])
    raw_wu, _ = big_update("adam_wu1", g_wu1, ffn_w_up, m_ffn_w_up, v_ffn_w_up, 1)
    raw_wd, _ = big_update("adam_wd1", g_wd1, ffn_w_down, m_ffn_w_down, v_ffn_w_down, 1)
    g_wu0, g_wd0, g_wp = _exchange_wait("scatter_f0_wait", scatter_b, [0, 1, 2], raw_wd[0])
    _, u_wu = big_update("adam_wu0", g_wu0, ffn_w_up, m_ffn_w_up, v_ffn_w_up, 0, raw_wu)
    _, u_wd = big_update("adam_wd0", g_wd0, ffn_w_down, m_ffn_w_down, v_ffn_w_down, 0, raw_wd)
    _, u_wp = big_update("adam_wp", g_wp, pool_w, m_pool_w, v_pool_w)
    g_w1, g_w2 = _exchange_wait("scatter_l0_wait", scatter_c, [0, 1], u_wp[0])
    _, u_w1 = big_update("adam_w1", g_w1, conv_w_pw1, m_conv_w_pw1, v_conv_w_pw1)
    _, u_w2 = big_update("adam_w2", g_w2, conv_w_pw2, m_conv_w_pw2, v_conv_w_pw2)
    (all_small,) = _exchange_wait("gather_small_wait", gather_small, [0], u_w2[0])
    summed = _sum_rows("sum_small_grads", all_small)
    (loss_v, g_nm, g_nf, g_b1, g_wdw, g_bdw, g_lg, g_lb, g_b2, g_pb, g_ps, g_fw, g_fb,
     g_fin) = _unpack(summed, small_part_shapes)
    loss = loss_v[0]
    g_wdw_mine = lax.dynamic_slice_in_dim(g_wdw, my * dsh, dsh, axis=1)[None]
    g_pb_mine = lax.dynamic_slice_in_dim(g_pb, my * dsh, dsh, axis=1)
    g_ps_mine = lax.dynamic_slice_in_dim(g_ps, my * dsh, dsh, axis=1)
    g_fw_mine = lax.dynamic_slice_in_dim(g_fw, my * fsh, fsh, axis=2)

    small_g =[g_nm, g_nf, g_b1, g_wdw_mine, g_bdw, g_lg, g_lb, g_b2, g_pb_mine, g_ps_mine, g_fw_mine, g_fb, g_fin]
    small_w = [norm_mix, norm_ffn, conv_b_pw1, conv_w_dw, conv_b_dw, conv_ln_g, conv_ln_b, conv_b_pw2, pool_b,
               pool_scale, ffn_w_dw, ffn_b_dw, final_norm]
    small_m = [m_norm_mix, m_norm_ffn, m_conv_b_pw1, m_conv_w_dw, m_conv_b_dw, m_conv_ln_g, m_conv_ln_b,
               m_conv_b_pw2, m_pool_b, m_pool_scale, m_ffn_w_dw, m_ffn_b_dw, m_final_norm]
    small_v = [v_norm_mix, v_norm_ffn, v_conv_b_pw1, v_conv_w_dw, v_conv_b_dw, v_conv_ln_g, v_conv_ln_b,
               v_conv_b_pw2, v_pool_b, v_pool_scale, v_ffn_w_dw, v_ffn_b_dw, v_final_norm]
    shapes = [w.shape for w in small_w]
    outs = _adamw("adam_small", _pack(small_g, 8 * LANE)[None], _pack(small_w, 8 * LANE)[None],
                  _pack(small_m, 8 * LANE)[None], _pack(small_v, 8 * LANE)[None])
    sg, sd, sm, sv = [_unpack(o, shapes) for o in outs]

    def leaf(kind):
        (nm, nf, b1, wdw_, bdw_, lg, lb, b2, pb, ps, fw, fb, fin) = (sg, sd, sm, sv)[kind]
        return [nm, nf, u_w1[kind], b1, wdw_, bdw_, lg, lb, u_w2[kind], b2, u_wp[kind], pb, ps, u_wu[kind], fw, fb,
                u_wd[kind], fin]

    return (loss, dx.reshape(bsz, seq, d), *leaf(0), *leaf(1), *leaf(2), *leaf(3))
```

```python
import functools

import jax
import jax.numpy as jnp
from jax import lax
from jax.experimental import pallas as pl
from jax.experimental.pallas import tpu as pltpu

F32 = jnp.float32
BF16 = jnp.bfloat16
MESH = pl.DeviceIdType.MESH
HBM = pl.BlockSpec(memory_space=pltpu.HBM)

N_DEV = 8
RMS_EPS = 1e-6
LN_EPS = 1e-5
POOL_WINDOWS = (2, 4, 8, 16)
N_GROUPS = len(POOL_WINDOWS)
ADAM_LR = 0.001
ADAM_B1 = 0.9
ADAM_B2 = 0.999
ADAM_EPS = 1e-08
ADAM_WD = 0.01
ADAM_STEP = 10

LANE = 128
HALO = 32
HALO16 = 16
VMEM_LIMIT = 56 * 1024 * 1024


def _params(*sem):
    return pltpu.CompilerParams(dimension_semantics=sem if sem else None, vmem_limit_bytes=VMEM_LIMIT)


def _tile(n, pref):
    for t in range(min(pref, n), 15, -1):
        if n % t == 0 and t % 16 == 0:
            return t
    return n


def _sigmoid(z):
    return 1.0 / (1.0 + jnp.exp(-z))


def _me():
    return lax.axis_index("x"), lax.axis_index("y"), lax.axis_index("c")


def _flip(pos, m):
    x, y, c = pos
    return ((1 - x) if m & 4 else x, (1 - y) if m & 2 else y, (1 - c) if m & 1 else c)


def _lin(pos):
    return 4 * pos[0] + 2 * pos[1] + pos[2]


SEM = pl.BlockSpec(memory_space=pltpu.SEMAPHORE)
ANY = pl.BlockSpec(memory_space=pl.ANY)
EFFECT = pltpu.SideEffectType.DATAFLOW_SIDE_EFFECTING


def _exchange_copies(srcs, lands, send_sems, recv_sems, modes, which, starting):
    me = _me()
    my = _lin(me)
    out = []
    for pos, a in enumerate(which):
        src, land = srcs[pos], lands[pos]

        def block(pid, src=src, a=a):
            return src.at[pid] if modes[a] else src

        local = pltpu.make_async_copy(block(my), land.at[my], send_sems.at[a * N_DEV])
        remote = []
        for m in range(1, N_DEV):
            peer = _flip(me, m)
            pid = _lin(peer)
            sems = dict(send_sem=send_sems.at[a * N_DEV + m], recv_sem=recv_sems.at[a * N_DEV + m],
                        device_id=peer, device_id_type=MESH)
            if starting:
                remote.append(pltpu.make_async_remote_copy(src_ref=block(pid), dst_ref=land.at[my], **sems))
            else:
                remote.append((pltpu.make_async_remote_copy(src_ref=block(pid), dst_ref=land.at[my], **sems),
                               pltpu.make_async_remote_copy(src_ref=block(pid), dst_ref=land.at[pid], **sems)))
        out.append((local, remote))
    return out


def _exchange_start(name, arrs, modes):
    n = len(arrs)
    blocks = [a.shape[1:] if md else a.shape for a, md in zip(arrs, modes)]

    def body(*refs):
        srcs, lands = refs[:n], refs[n:2 * n]
        send_sems, recv_sems = refs[2 * n], refs[2 * n + 1]
        token = refs[-1]
        for local, remote in _exchange_copies(srcs, lands, send_sems, recv_sems, modes, list(range(n)), True):
            local.start()
            for send in remote:
                send.start()
        token[...] = jnp.zeros_like(token)

    lands = [lax.empty((N_DEV,) + tuple(b), a.dtype) for a, b in zip(arrs, blocks)]
    outs = pl.pallas_call(
        body, name=name,
        out_shape=(pltpu.SemaphoreType.DMA((n * N_DEV,)), pltpu.SemaphoreType.DMA((n * N_DEV,)),
                   *[pltpu.HBM(a.shape, a.dtype) for a in arrs], *[pltpu.HBM(l.shape, l.dtype) for l in lands],
                   jax.ShapeDtypeStruct((8, LANE), F32)),
        in_specs=[HBM] * (2 * n),
        out_specs=(SEM, SEM, *[HBM] * (2 * n), pl.BlockSpec(memory_space=pltpu.VMEM)),
        input_output_aliases={i: 2 + i for i in range(2 * n)},
        compiler_params=pltpu.CompilerParams(has_side_effects=EFFECT),
    )(*[pltpu.with_memory_space_constraint(a, pltpu.HBM) for a in arrs],
      *[pltpu.with_memory_space_constraint(l, pltpu.HBM) for l in lands])
    return dict(send=outs[0], recv=outs[1], srcs=list(outs[2:2 + n]), lands=list(outs[2 + n:2 + 2 * n]),
                modes=modes, token=outs[-1])


def _exchange_wait(name, handle, which, after):
    k = len(which)
    modes = handle["modes"]

    def body(*refs):
        srcs, lands = refs[:k], refs[k:2 * k]
        send_sems, recv_sems = refs[2 * k], refs[2 * k + 1]
        for local, remote in _exchange_copies(srcs, lands, send_sems, recv_sems, modes, which, False):
            local.wait()
            for send, arrival in remote:
                send.wait_send()
                arrival.wait_recv()

    srcs = [handle["srcs"][a] for a in which]
    lands = [handle["lands"][a] for a in which]
    outs = pl.pallas_call(
        body, name=name,
        out_shape=tuple(pltpu.HBM(x.shape, x.dtype) for x in srcs + lands),
        in_specs=[HBM] * (2 * k) + [SEM, SEM, ANY], out_specs=tuple([HBM] * (2 * k)),
        input_output_aliases={i: i for i in range(2 * k)},
        compiler_params=pltpu.CompilerParams(has_side_effects=EFFECT),
    )(*srcs, *lands, handle["send"], handle["recv"], after)
    return list(outs[k:])


def _mm(name, a, b, *, grid, a_spec, b_spec, out_spec, out_shape, dims, acc_shape, extras=(), extra_specs=(),
        epilogue=None, token=None):
    nk = grid[2]
    ne = len(extras)
    deps = () if token is None else (token,)
    dep_specs = [pl.BlockSpec((8, LANE), lambda i, j, k: (0, 0))] * len(deps)
    n_out = len(out_shape) if isinstance(out_shape, (list, tuple)) else 1

    def body(a_ref, b_ref, *rest):
        ex, o_refs, acc_ref = rest[:ne], rest[ne + len(deps):ne + len(deps) + n_out], rest[ne + len(deps) + n_out]
        k = pl.program_id(2)
        part = lax.dot_general(a_ref[...].astype(BF16), b_ref[...].astype(BF16), (dims, ((), ())),
                               preferred_element_type=F32)

        def finish(r):
            if epilogue is not None:
                r = epilogue(r, *[e[...] for e in ex])
            for o_ref, val in zip(o_refs, r if isinstance(r, tuple) else (r,)):
                o_ref[...] = val.astype(o_ref.dtype)

        if nk == 1:
            finish(part)
            return

        @pl.when(k == 0)
        def _():
            acc_ref[...] = part

        @pl.when((k > 0) & (k < nk - 1))
        def _():
            acc_ref[...] += part

        @pl.when(k == nk - 1)
        def _():
            finish(acc_ref[...] + part)

    return pl.pallas_call(
        body, name=name, grid=grid, in_specs=[a_spec, b_spec, *extra_specs, *dep_specs], out_specs=out_spec,
        out_shape=out_shape, scratch_shapes=[pltpu.VMEM(acc_shape if nk > 1 else (8, LANE), F32)],
        compiler_params=_params("parallel", "parallel", "arbitrary"),
    )(a, b, *extras, *deps)


NN = ((1,), (0,))
NT = ((1,), (1,))
TN = ((0,), (0,))


def _rms_fwd(name, x, gain):
    t, d = x.shape
    tr = _tile(t, 512)

    def body(x_ref, g_ref, h_ref):
        xv = x_ref[...]
        rstd = lax.rsqrt(jnp.mean(xv * xv, axis=-1, keepdims=True) + RMS_EPS)
        h_ref[...] = (xv * rstd * g_ref[...]).astype(BF16)

    return pl.pallas_call(
        body, name=name, grid=(t // tr,),
        in_specs=[pl.BlockSpec((tr, d), lambda i: (i, 0)), pl.BlockSpec((1, d), lambda i: (0, 0))],
        out_specs=pl.BlockSpec((tr, d), lambda i: (i, 0)),
        out_shape=jax.ShapeDtypeStruct((t, d), BF16), compiler_params=_params("parallel"),
    )(x, gain)


def _rms_bwd(name, dh, x, gain, dres):
    t, d = x.shape
    tr = _tile(t, 512)

    def body(dh_ref, x_ref, g_ref, dres_ref, dx_ref, dg_ref, cs_ref):
        i = pl.program_id(0)
        xv = x_ref[...]
        rstd = lax.rsqrt(jnp.mean(xv * xv, axis=-1, keepdims=True) + RMS_EPS)
        xhat = xv * rstd
        dhv = dh_ref[...]
        dxhat = dhv * g_ref[...]
        dx = dres_ref[...] + rstd * (dxhat - xhat * jnp.mean(dxhat * xhat, axis=-1, keepdims=True))
        dx_ref[...] = dx

        @pl.when(i == 0)
        def _():
            dg_ref[...] = jnp.zeros_like(dg_ref)
            cs_ref[...] = jnp.zeros_like(cs_ref)

        dg_ref[...] += jnp.sum(dhv * xhat, axis=0, keepdims=True)
        cs_ref[...] += jnp.sum(dx, axis=0, keepdims=True)

    row = pl.BlockSpec((tr, d), lambda i: (i, 0))
    vec = pl.BlockSpec((1, d), lambda i: (0, 0))
    return pl.pallas_call(
        body, name=name, grid=(t // tr,), in_specs=[row, row, vec, row], out_specs=[row, vec, vec],
        out_shape=[jax.ShapeDtypeStruct((t, d), F32), jax.ShapeDtypeStruct((1, d), F32),
                   jax.ShapeDtypeStruct((1, d), F32)],
        compiler_params=_params("arbitrary"),
    )(dh, x, gain, dres)


def _final(name, x, tgt, gain):
    t, d = x.shape
    tr = _tile(t, 512)

    def body(x_ref, t_ref, g_ref, dx_ref, loss_ref, dg_ref, cs_ref):
        i = pl.program_id(0)
        xv = x_ref[...]
        g = g_ref[...]
        rstd = lax.rsqrt(jnp.mean(xv * xv, axis=-1, keepdims=True) + RMS_EPS)
        xhat = xv * rstd
        err = xhat * g - t_ref[...]
        dy = err / d
        dxhat = dy * g
        dx = rstd * (dxhat - xhat * jnp.mean(dxhat * xhat, axis=-1, keepdims=True))
        dx_ref[...] = dx

        @pl.when(i == 0)
        def _():
            loss_ref[...] = jnp.zeros_like(loss_ref)
            dg_ref[...] = jnp.zeros_like(dg_ref)
            cs_ref[...] = jnp.zeros_like(cs_ref)

        loss_ref[...] += 0.5 * jnp.sum(jnp.mean(err * err, axis=-1, keepdims=True), axis=0, keepdims=True)
        dg_ref[...] += jnp.sum(dy * xhat, axis=0, keepdims=True)
        cs_ref[...] += jnp.sum(dx, axis=0, keepdims=True)

    row = pl.BlockSpec((tr, d), lambda i: (i, 0))
    vec = pl.BlockSpec((1, d), lambda i: (0, 0))
    one = pl.BlockSpec((1, 1), lambda i: (0, 0))
    return pl.pallas_call(
        body, name=name, grid=(t // tr,), in_specs=[row, row, vec], out_specs=[row, one, vec, vec],
        out_shape=[jax.ShapeDtypeStruct((t, d), F32), jax.ShapeDtypeStruct((1, 1), F32),
                   jax.ShapeDtypeStruct((1, d), F32), jax.ShapeDtypeStruct((1, d), F32)],
        compiler_params=_params("arbitrary"),
    )(x, tgt, gain)


def _ln_fwd(name, v, g, b):
    t, d = v.shape
    tr = _tile(t, 512)

    def body(v_ref, g_ref, b_ref, s_ref):
        vv = v_ref[...]
        mu = jnp.mean(vv, axis=-1, keepdims=True)
        cen = vv - mu
        rstd = lax.rsqrt(jnp.mean(cen * cen, axis=-1, keepdims=True) + LN_EPS)
        z = cen * rstd * g_ref[...] + b_ref[...]
        s_ref[...] = (z * _sigmoid(z)).astype(BF16)

    row = pl.BlockSpec((tr, d), lambda i: (i, 0))
    vec = pl.BlockSpec((1, d), lambda i: (0, 0))
    return pl.pallas_call(
        body, name=name, grid=(t // tr,), in_specs=[row, vec, vec], out_specs=row,
        out_shape=jax.ShapeDtypeStruct((t, d), BF16), compiler_params=_params("parallel"),
    )(v, g, b)


def _ln_bwd(name, ds, v, g, b):
    t, d = v.shape
    tr = _tile(t, 512)

    def body(ds_ref, v_ref, g_ref, b_ref, dv_ref, dg_ref, db_ref, cs_ref):
        i = pl.program_id(0)
        vv = v_ref[...]
        g = g_ref[...]
        mu = jnp.mean(vv, axis=-1, keepdims=True)
        cen = vv - mu
        rstd = lax.rsqrt(jnp.mean(cen * cen, axis=-1, keepdims=True) + LN_EPS)
        y = cen * rstd
        z = y * g + b_ref[...]
        sig = _sigmoid(z)
        dz = ds_ref[...] * sig * (1.0 + z * (1.0 - sig))
        dy = dz * g
        dv = rstd * (dy - jnp.mean(dy, axis=-1, keepdims=True) - y * jnp.mean(dy * y, axis=-1, keepdims=True))
        dv_ref[...] = dv

        @pl.when(i == 0)
        def _():
            dg_ref[...] = jnp.zeros_like(dg_ref)
            db_ref[...] = jnp.zeros_like(db_ref)
            cs_ref[...] = jnp.zeros_like(cs_ref)

        dg_ref[...] += jnp.sum(dz * y, axis=0, keepdims=True)
        db_ref[...] += jnp.sum(dz, axis=0, keepdims=True)
        cs_ref[...] += jnp.sum(dv, axis=0, keepdims=True)

    row = pl.BlockSpec((tr, d), lambda i: (i, 0))
    vec = pl.BlockSpec((1, d), lambda i: (0, 0))
    return pl.pallas_call(
        body, name=name, grid=(t // tr,), in_specs=[row, row, vec, vec], out_specs=[row, vec, vec, vec],
        out_shape=[jax.ShapeDtypeStruct((t, d), F32)] + [jax.ShapeDtypeStruct((1, d), F32)] * 3,
        compiler_params=_params("arbitrary"),
    )(ds, v, g, b)


def _conv_tiles(t, seq):
    ts = _tile(seq, 512)
    return ts, seq // ts, _tile(ts, 64)


def _conv_fwd(name, a, w, b, seq):
    _, t, d = a.shape
    k_taps = w.shape[0]
    ts, tps, rc = _conv_tiles(t, seq)
    hb = ts // HALO

    def body(cur_ref, prev_ref, w_ref, b_ref, v_ref, upad):
        i = pl.program_id(1)
        first = (i % tps) == 0
        pv = prev_ref[0].astype(F32)
        pg = prev_ref[1].astype(F32)
        upad[0:HALO, :] = jnp.where(first, 0.0, pv * _sigmoid(pg))
        upad[HALO:HALO + ts, :] = cur_ref[0].astype(F32) * _sigmoid(cur_ref[1].astype(F32))
        wv = w_ref[...]
        bias = jnp.broadcast_to(b_ref[...], (rc, LANE))
        for r0 in range(0, ts, rc):
            acc = bias
            for k in range(k_taps):
                acc = acc + wv[k:k + 1, :] * upad[pl.ds(HALO - (k_taps - 1) + k + r0, rc), :]
            v_ref[pl.ds(r0, rc), :] = acc

    return pl.pallas_call(
        body, name=name, grid=(d // LANE, t // ts),
        in_specs=[pl.BlockSpec((2, ts, LANE), lambda c, i: (0, i, c)),
                  pl.BlockSpec((2, HALO, LANE), lambda c, i: (0, jnp.maximum(i * hb - 1, 0), c)),
                  pl.BlockSpec((k_taps, LANE), lambda c, i: (0, c)),
                  pl.BlockSpec((1, LANE), lambda c, i: (0, c))],
        out_specs=pl.BlockSpec((ts, LANE), lambda c, i: (i, c)),
        out_shape=jax.ShapeDtypeStruct((t, d), F32),
        scratch_shapes=[pltpu.VMEM((HALO + ts, LANE), F32)],
        compiler_params=_params("parallel", "parallel"),
    )(a, a, w, b)


def _conv_bwd(name, a, dv, w, seq):
    _, t, d = a.shape
    k_taps = w.shape[0]
    ts, tps, rc = _conv_tiles(t, seq)
    hb = ts // HALO
    nhb = t // HALO

    def body(cur_ref, prev_ref, dv_ref, ndv_ref, w_ref, da_ref, dw_ref, dbp_ref, upad, dvpad, dwrows):
        i = pl.program_id(1)
        first = (i % tps) == 0
        last = (i % tps) == tps - 1
        pv = prev_ref[0].astype(F32)
        pg = prev_ref[1].astype(F32)
        upad[0:HALO, :] = jnp.where(first, 0.0, pv * _sigmoid(pg))
        upad[HALO:HALO + ts, :] = cur_ref[0].astype(F32) * _sigmoid(cur_ref[1].astype(F32))
        dvpad[0:ts, :] = dv_ref[...]
        dvpad[ts:ts + HALO, :] = jnp.where(last, 0.0, ndv_ref[...])
        wv = w_ref[...]

        @pl.when(i == 0)
        def _():
            dw_ref[...] = jnp.zeros_like(dw_ref)
            dbp_ref[...] = jnp.zeros_like(dbp_ref)

        sv = jnp.zeros((1, LANE), F32)
        sg = jnp.zeros((1, LANE), F32)
        for r0 in range(0, ts, rc):
            du = jnp.zeros((rc, LANE), F32)
            for k in range(k_taps):
                du = du + wv[k:k + 1, :] * dvpad[pl.ds(r0 + (k_taps - 1) - k, rc), :]
            av = cur_ref[0, pl.ds(r0, rc), :].astype(F32)
            sig = _sigmoid(cur_ref[1, pl.ds(r0, rc), :].astype(F32))
            dval = du * sig
            dgate = du * av * sig * (1.0 - sig)
            da_ref[0, pl.ds(r0, rc), :] = dval.astype(BF16)
            da_ref[1, pl.ds(r0, rc), :] = dgate.astype(BF16)
            sv = sv + jnp.sum(dval, axis=0, keepdims=True)
            sg = sg + jnp.sum(dgate, axis=0, keepdims=True)
        dbp_ref[0] += sv
        dbp_ref[1] += sg

        for k in range(k_taps):
            acc = jnp.zeros((rc, LANE), F32)
            for r0 in range(0, ts, rc):
                acc = acc + dvpad[pl.ds(r0, rc), :] * upad[pl.ds(HALO - (k_taps - 1) + k + r0, rc), :]
            dwrows[k:k + 1, :] = jnp.sum(acc, axis=0, keepdims=True)
        dw_ref[...] += dwrows[0:k_taps, :]

    return pl.pallas_call(
        body, name=name, grid=(d // LANE, t // ts),
        in_specs=[pl.BlockSpec((2, ts, LANE), lambda c, i: (0, i, c)),
                  pl.BlockSpec((2, HALO, LANE), lambda c, i: (0, jnp.maximum(i * hb - 1, 0), c)),
                  pl.BlockSpec((ts, LANE), lambda c, i: (i, c)),
                  pl.BlockSpec((HALO, LANE), lambda c, i: (jnp.minimum((i + 1) * hb, nhb - 1), c)),
                  pl.BlockSpec((k_taps, LANE), lambda c, i: (0, c))],
        out_specs=[pl.BlockSpec((2, ts, LANE), lambda c, i: (0, i, c)),
                   pl.BlockSpec((k_taps, LANE), lambda c, i: (0, c)),
                   pl.BlockSpec((2, 1, LANE), lambda c, i: (0, 0, c))],
        out_shape=[jax.ShapeDtypeStruct((2, t, d), BF16), jax.ShapeDtypeStruct((k_taps, d), F32),
                   jax.ShapeDtypeStruct((2, 1, d), F32)],
        scratch_shapes=[pltpu.VMEM((HALO + ts, LANE), F32), pltpu.VMEM((ts + HALO, LANE), F32),
                        pltpu.VMEM((HALO, LANE), F32)],
        compiler_params=_params("parallel", "arbitrary"),
    )(a, a, dv, dv, w)


def _pool_fwd(name, x, gain, seq):
    t, d = x.shape
    ts = _tile(seq, 256)
    tps = seq // ts
    hb = ts // HALO
    cg = d // N_GROUPS

    def body(cur_ref, prev_ref, g_ref, o_ref, hpad):
        i = pl.program_id(0)
        first = (i % tps) == 0
        g = g_ref[...]

        def norm(xv):
            return xv * lax.rsqrt(jnp.mean(xv * xv, axis=-1, keepdims=True) + RMS_EPS) * g

        hpad[0:HALO, :] = jnp.where(first, 0.0, norm(prev_ref[...]))
        hpad[HALO:HALO + ts, :] = norm(cur_ref[...])
        pos = (i % tps) * ts + lax.broadcasted_iota(jnp.int32, (ts, 1), 0)
        for gi, win in enumerate(POOL_WINDOWS):
            sl = slice(gi * cg, (gi + 1) * cg)
            own = hpad[HALO:HALO + ts, sl]
            acc = own
            for j in range(1, win):
                acc = acc + hpad[HALO - j:HALO - j + ts, sl]
            cnt = jnp.minimum(pos + 1, win).astype(F32)
            o_ref[:, sl] = (acc / cnt - own).astype(BF16)

    return pl.pallas_call(
        body, name=name, grid=(t // ts,),
        in_specs=[pl.BlockSpec((ts, d), lambda i: (i, 0)),
                  pl.BlockSpec((HALO, d), lambda i: (jnp.maximum(i * hb - 1, 0), 0)),
                  pl.BlockSpec((1, d), lambda i: (0, 0))],
        out_specs=pl.BlockSpec((ts, d), lambda i: (i, 0)),
        out_shape=jax.ShapeDtypeStruct((t, d), BF16),
        scratch_shapes=[pltpu.VMEM((HALO + ts, d), F32)],
        compiler_params=_params("parallel"),
    )(x, x, gain)


def _pool_bwd_mm(name, pooled, wp, dp, scale, bias):
    t, d = pooled.shape
    cg = d // N_GROUPS
    tm = _tile(t, 512)

    def body(p_ref, w_ref, dp_ref, s_ref, b_ref, dpo_ref, dmx_ref, ds_ref, db_ref):
        i = pl.program_id(1)
        wv = w_ref[...]
        mixed = jnp.dot(p_ref[...], wv, preferred_element_type=F32)
        dpv = dp_ref[...]
        dmx = dpv * s_ref[...]
        dmx16 = dmx.astype(BF16)
        dmx_ref[...] = dmx16
        dpo_ref[...] = lax.dot_general(dmx16, wv, (NT, ((), ())), preferred_element_type=F32)

        @pl.when(i == 0)
        def _():
            ds_ref[...] = jnp.zeros_like(ds_ref)
            db_ref[...] = jnp.zeros_like(db_ref)

        ds_ref[...] += jnp.sum(dpv * (mixed + b_ref[...]), axis=0, keepdims=True)
        db_ref[...] += jnp.sum(dmx, axis=0, keepdims=True)

    blk = pl.BlockSpec((tm, cg), lambda g, i: (i, g))
    vec = pl.BlockSpec((1, cg), lambda g, i: (0, g))
    return pl.pallas_call(
        body, name=name, grid=(N_GROUPS, t // tm),
        in_specs=[blk, pl.BlockSpec((None, cg, cg), lambda g, i: (g, 0, 0)), blk, vec, vec],
        out_specs=[blk, blk, vec, vec],
        out_shape=[jax.ShapeDtypeStruct((t, d), F32), jax.ShapeDtypeStruct((t, d), BF16),
                   jax.ShapeDtypeStruct((1, d), F32), jax.ShapeDtypeStruct((1, d), F32)],
        compiler_params=_params("parallel", "arbitrary"),
    )(pooled, wp, dp, scale, bias)


def _pool_bwd(name, dpooled, x, gain, dres, seq):
    t, d = x.shape
    ts = _tile(seq, 256)
    tps = seq // ts
    hb = ts // HALO
    nhb = t // HALO
    cg = d // N_GROUPS

    def body(dpo_ref, ndpo_ref, x_ref, g_ref, dres_ref, dx_ref, dg_ref, qpad, dh):
        i = pl.program_id(0)
        last = (i % tps) == tps - 1
        pos = (i % tps) * ts + lax.broadcasted_iota(jnp.int32, (ts, 1), 0)
        for gi, win in enumerate(POOL_WINDOWS):
            sl = slice(gi * cg, (gi + 1) * cg)
            cur = dpo_ref[:, sl]
            qpad[0:ts, sl] = cur / jnp.minimum(pos + 1, win).astype(F32)
            qpad[ts:ts + HALO, sl] = jnp.where(last, 0.0, ndpo_ref[:, sl] / float(win))
            acc = -cur
            for j in range(win):
                acc = acc + qpad[j:j + ts, sl]
            dh[:, sl] = acc
        xv = x_ref[...]
        rstd = lax.rsqrt(jnp.mean(xv * xv, axis=-1, keepdims=True) + RMS_EPS)
        xhat = xv * rstd
        dhv = dh[...]
        dxhat = dhv * g_ref[...]
        dx_ref[...] = dres_ref[...] + rstd * (dxhat - xhat * jnp.mean(dxhat * xhat, axis=-1, keepdims=True))

        @pl.when(i == 0)
        def _():
            dg_ref[...] = jnp.zeros_like(dg_ref)

        dg_ref[...] += jnp.sum(dhv * xhat, axis=0, keepdims=True)

    row = pl.BlockSpec((ts, d), lambda i: (i, 0))
    vec = pl.BlockSpec((1, d), lambda i: (0, 0))
    return pl.pallas_call(
        body, name=name, grid=(t // ts,),
        in_specs=[row, pl.BlockSpec((HALO, d), lambda i: (jnp.minimum((i + 1) * hb, nhb - 1), 0)), row, vec, row],
        out_specs=[row, vec],
        out_shape=[jax.ShapeDtypeStruct((t, d), F32), jax.ShapeDtypeStruct((1, d), F32)],
        scratch_shapes=[pltpu.VMEM((ts + HALO, d), F32), pltpu.VMEM((ts, d), F32)],
        compiler_params=_params("arbitrary"),
    )(dpooled, dpooled, x, gain, dres)


def _ctile(n, pref):
    return max(c for c in range(LANE, min(pref, n) + 1, LANE) if n % c == 0)


FFN_COLS = 1408


def _ffn_fwd(name, up, w, b, seq):
    _, t, dff = up.shape
    f = _ctile(dff, FFN_COLS)
    k_taps = w.shape[0]
    ts = _tile(seq, 256)
    tps = seq // ts
    hb = ts // HALO16

    def body(cur_ref, prev_ref, w_ref, b_ref, g_ref, apad):
        i = pl.program_id(1)
        first = (i % tps) == 0
        apad[0:HALO16, :] = jnp.where(first, 0.0, prev_ref[...].astype(F32))
        apad[HALO16:HALO16 + ts, :] = cur_ref[0].astype(F32)
        wv = w_ref[...]
        c = jnp.broadcast_to(b_ref[...], (ts, f))
        for k in range(k_taps):
            c = c + wv[k:k + 1, :] * apad[HALO16 - (k_taps - 1) + k:HALO16 - (k_taps - 1) + k + ts, :]
        g_ref[...] = (c * _sigmoid(c) * cur_ref[1].astype(F32)).astype(BF16)

    return pl.pallas_call(
        body, name=name, grid=(dff // f, t // ts),
        in_specs=[pl.BlockSpec((2, ts, f), lambda j, i: (0, i, j)),
                  pl.BlockSpec((None, HALO16, f), lambda j, i: (0, jnp.maximum(i * hb - 1, 0), j)),
                  pl.BlockSpec((k_taps, f), lambda j, i: (0, j)),
                  pl.BlockSpec((1, f), lambda j, i: (0, j))],
        out_specs=pl.BlockSpec((ts, f), lambda j, i: (i, j)),
        out_shape=jax.ShapeDtypeStruct((t, dff), BF16),
        scratch_shapes=[pltpu.VMEM((HALO16 + ts, f), F32)],
        compiler_params=_params("parallel", "parallel"),
    )(up, up, w, b)


def _ffn_bwd(name, up, dg, w, b, seq):
    _, t, dff = up.shape
    f = _ctile(dff, FFN_COLS)
    k_taps = w.shape[0]
    ts = _tile(seq, 256)
    tps = seq // ts
    hb = ts // HALO16
    nhb = t // HALO16
    ext = ts + HALO16

    def body(cur_ref, prev_ref, next_ref, dg_ref, ndg_ref, w_ref, b_ref, dup_ref, dw_ref, db_ref,
             apad, gpad, dgpad, dcpad, dwrows):
        i = pl.program_id(1)
        first = (i % tps) == 0
        last = (i % tps) == tps - 1
        apad[0:HALO16, :] = jnp.where(first, 0.0, prev_ref[...].astype(F32))
        apad[HALO16:HALO16 + ts, :] = cur_ref[0].astype(F32)
        apad[HALO16 + ts:HALO16 + ext, :] = next_ref[0].astype(F32)
        gpad[0:ts, :] = cur_ref[1].astype(F32)
        gpad[ts:ext, :] = next_ref[1].astype(F32)
        dgpad[0:ts, :] = dg_ref[...].astype(F32)
        dgpad[ts:ext, :] = jnp.where(last, 0.0, ndg_ref[...].astype(F32))
        wv = w_ref[...]
        c = jnp.broadcast_to(b_ref[...], (ext, f))
        for k in range(k_taps):
            c = c + wv[k:k + 1, :] * apad[HALO16 - (k_taps - 1) + k:HALO16 - (k_taps - 1) + k + ext, :]
        sig = _sigmoid(c)
        dgv = dgpad[...]
        dcpad[...] = dgv * gpad[...] * sig * (1.0 + c * (1.0 - sig))
        dup_ref[1] = (dgv[0:ts] * (c * sig)[0:ts]).astype(BF16)
        dact = jnp.zeros((ts, f), F32)
        for k in range(k_taps):
            dact = dact + wv[k:k + 1, :] * dcpad[(k_taps - 1) - k:(k_taps - 1) - k + ts, :]
        dup_ref[0] = dact.astype(BF16)

        @pl.when(i == 0)
        def _():
            dw_ref[...] = jnp.zeros_like(dw_ref)
            db_ref[...] = jnp.zeros_like(db_ref)

        dc = dcpad[0:ts, :]
        for k in range(k_taps):
            sh = apad[HALO16 - (k_taps - 1) + k:HALO16 - (k_taps - 1) + k + ts, :]
            dwrows[k:k + 1, :] = jnp.sum(dc * sh, axis=0, keepdims=True)
        dw_ref[...] += dwrows[0:k_taps, :]
        db_ref[...] += jnp.sum(dc, axis=0, keepdims=True)

    return pl.pallas_call(
        body, name=name, grid=(dff // f, t // ts),
        in_specs=[pl.BlockSpec((2, ts, f), lambda j, i: (0, i, j)),
                  pl.BlockSpec((None, HALO16, f), lambda j, i: (0, jnp.maximum(i * hb - 1, 0), j)),
                  pl.BlockSpec((2, HALO16, f), lambda j, i: (0, jnp.minimum((i + 1) * hb, nhb - 1), j)),
                  pl.BlockSpec((ts, f), lambda j, i: (i, j)),
                  pl.BlockSpec((HALO16, f), lambda j, i: (jnp.minimum((i + 1) * hb, nhb - 1), j)),
                  pl.BlockSpec((k_taps, f), lambda j, i: (0, j)),
                  pl.BlockSpec((1, f), lambda j, i: (0, j))],
        out_specs=[pl.BlockSpec((2, ts, f), lambda j, i: (0, i, j)),
                   pl.BlockSpec((k_taps, f), lambda j, i: (0, j)),
                   pl.BlockSpec((1, f), lambda j, i: (0, j))],
        out_shape=[jax.ShapeDtypeStruct((2, t, dff), BF16), jax.ShapeDtypeStruct((k_taps, dff), F32),
                   jax.ShapeDtypeStruct((1, dff), F32)],
        scratch_shapes=[pltpu.VMEM((HALO16 + ext, f), F32), pltpu.VMEM((ext, f), F32), pltpu.VMEM((ext, f), F32),
                        pltpu.VMEM((ext, f), F32), pltpu.VMEM((8, f), F32)],
        compiler_params=_params("parallel", "arbitrary"),
    )(up, up, up, dg, dg, w, b)


def _sum_rows(name, g):
    ns, r, c = g.shape
    tr = _tile(r, 256)

    def body(g_ref, o_ref):
        acc = g_ref[0]
        for dev in range(1, ns):
            acc = acc + g_ref[dev]
        o_ref[...] = acc

    return pl.pallas_call(
        body, name=name, grid=(r // tr,),
        in_specs=[pl.BlockSpec((ns, tr, c), lambda i: (0, i, 0))],
        out_specs=pl.BlockSpec((tr, c), lambda i: (i, 0)),
        out_shape=jax.ShapeDtypeStruct((r, c), F32), compiler_params=_params("parallel"),
    )(g)


def _adamw(name, gsrc, w, m, v, layer=0, prev=None):
    ns, r, c = gsrc.shape
    nl = w.shape[0]
    tr = _tile(r, 256)
    prev = () if prev is None else tuple(prev)

    def body(g_ref, w_ref, m_ref, v_ref, *rest):
        go_ref, do_ref, mo_ref, vo_ref = rest[len(prev):]
        g = g_ref[0].astype(F32)
        for dev in range(1, ns):
            g = g + g_ref[dev].astype(F32)
        m_new = ADAM_B1 * m_ref[...] + (1.0 - ADAM_B1) * g
        v_new = ADAM_B2 * v_ref[...] + (1.0 - ADAM_B2) * (g * g)
        m_hat = m_new / (1.0 - ADAM_B1 ** ADAM_STEP)
        v_hat = v_new / (1.0 - ADAM_B2 ** ADAM_STEP)
        go_ref[...] = g
        do_ref[...] = -ADAM_LR * (m_hat / (jnp.sqrt(v_hat) + ADAM_EPS) + ADAM_WD * w_ref[...])
        mo_ref[...] = m_new
        vo_ref[...] = v_new

    row = pl.BlockSpec((None, tr, c), lambda i: (layer, i, 0))
    return pl.pallas_call(
        body, name=name, grid=(r // tr,),
        in_specs=[pl.BlockSpec((ns, tr, c), lambda i: (0, i, 0)), row, row, row] + [ANY] * len(prev),
        out_specs=[row] * 4, out_shape=[jax.ShapeDtypeStruct((nl, r, c), F32)] * 4,
        input_output_aliases={4 + i: i for i in range(len(prev))},
        compiler_params=_params("parallel"),
    )(gsrc, w, m, v, *prev)


def _ffn_forward(tag, r_in, gain, get_wu, get_wd, wdw, bdw, seq):
    t, d = r_in.shape
    tm = _tile(t, 512)
    h = _rms_fwd(f"{tag}_rms", r_in, gain)
    wu = get_wu(h)
    dff = wu.shape[1] // 2
    up = _mm(f"{tag}_up", h, wu, grid=(2, t // tm, 1),
             a_spec=pl.BlockSpec((tm, d), lambda j, i, k: (i, 0)),
             b_spec=pl.BlockSpec((d, dff), lambda j, i, k: (0, j)),
             out_spec=pl.BlockSpec((None, tm, dff), lambda j, i, k: (j, i, 0)),
             out_shape=jax.ShapeDtypeStruct((2, t, dff), BF16), dims=NN, acc_shape=(tm, dff))
    wd = get_wd(up)
    g = _ffn_fwd(f"{tag}_act", up, wdw, bdw, seq)
    r_out = _mm(f"{tag}_down", g, wd, grid=(t // tm, 1, 1),
                a_spec=pl.BlockSpec((tm, dff), lambda i, j, k: (i, 0)),
                b_spec=pl.BlockSpec((dff, d), lambda i, j, k: (0, 0)),
                out_spec=pl.BlockSpec((tm, d), lambda i, j, k: (i, 0)),
                out_shape=jax.ShapeDtypeStruct((t, d), F32), dims=NN, acc_shape=(tm, d),
                extras=(r_in,), extra_specs=(pl.BlockSpec((tm, d), lambda i, j, k: (i, 0)),),
                epilogue=lambda acc, res: res + acc)
    return r_out, (r_in, h, up, g, wu, wd)


def _ffn_backward(tag, dr, saved, gain, wdw, bdw, seq, token=None):
    r_in, h, up, g, wu, wd = saved
    t, d = r_in.shape
    dff = wd.shape[0]
    tm = _tile(t, 512)
    tk = _tile(t, 1024)
    cw = _ctile(dff, 1408)
    rm = _ctile(d, 512)
    dg = _mm(f"{tag}_dg", dr, wd, grid=(t // tm, 1, 1),
             a_spec=pl.BlockSpec((tm, d), lambda i, j, k: (i, 0)),
             b_spec=pl.BlockSpec((dff, d), lambda i, j, k: (0, 0)),
             out_spec=pl.BlockSpec((tm, dff), lambda i, j, k: (i, 0)),
             out_shape=jax.ShapeDtypeStruct((t, dff), BF16), dims=NT, acc_shape=(tm, dff), token=token)
    dwd = _mm(f"{tag}_dwd", g, dr, grid=(dff // cw, 1, t // tk),
              a_spec=pl.BlockSpec((tk, cw), lambda i, j, k: (k, i)),
              b_spec=pl.BlockSpec((tk, d), lambda i, j, k: (k, 0)),
              out_spec=pl.BlockSpec((cw, d), lambda i, j, k: (i, 0)),
              out_shape=jax.ShapeDtypeStruct((dff, d), BF16), dims=TN, acc_shape=(cw, d))
    dup, dwdw, dbdw = _ffn_bwd(f"{tag}_dact", up, dg, wdw, bdw, seq)
    dh = _mm(f"{tag}_dh", dup, wu, grid=(t // tm, 1, 2),
             a_spec=pl.BlockSpec((None, tm, dff), lambda i, j, k: (k, i, 0)),
             b_spec=pl.BlockSpec((d, dff), lambda i, j, k: (0, k)),
             out_spec=pl.BlockSpec((tm, d), lambda i, j, k: (i, 0)),
             out_shape=jax.ShapeDtypeStruct((t, d), F32), dims=NT, acc_shape=(tm, d))
    dwu = _mm(f"{tag}_dwu", h, dup, grid=(d // rm, 2, t // tk),
              a_spec=pl.BlockSpec((tk, rm), lambda i, j, k: (k, i)),
              b_spec=pl.BlockSpec((None, tk, dff), lambda i, j, k: (j, k, 0)),
              out_spec=pl.BlockSpec((rm, dff), lambda i, j, k: (i, j)),
              out_shape=jax.ShapeDtypeStruct((d, 2 * dff), BF16), dims=TN, acc_shape=(rm, dff))
    dr_in, dgain, colsum = _rms_bwd(f"{tag}_drms", dh, r_in, gain, dr)
    return dr_in, dgain, dwu, dwd, dwdw, dbdw, colsum


def _pad_to(vec, n):
    return jnp.pad(vec, (0, n - vec.shape[0]))


def _pack(parts, width):
    flat = jnp.concatenate([p.reshape(-1).astype(F32) for p in parts])
    n = -(-flat.shape[0] // (8 * width)) * (8 * width)
    return _pad_to(flat, n).reshape(n // width, width)


def _unpack(mat, shapes):
    flat = mat.reshape(-1)
    out, off = [], 0
    for s in shapes:
        n = 1
        for dim in s:
            n *= dim
        out.append(flat[off:off + n].reshape(s))
        off += n
    return out


def kernel(x, norm_mix, norm_ffn, conv_w_pw1, conv_b_pw1, conv_w_dw, conv_b_dw, conv_ln_g, conv_ln_b, conv_w_pw2, conv_b_pw2, pool_w, pool_b, pool_scale, ffn_w_up, ffn_w_dw, ffn_b_dw, ffn_w_down, final_norm, loss_target, m_norm_mix, m_norm_ffn, m_conv_w_pw1, m_conv_b_pw1, m_conv_w_dw, m_conv_b_dw, m_conv_ln_g, m_conv_ln_b, m_conv_w_pw2, m_conv_b_pw2, m_pool_w, m_pool_b, m_pool_scale, m_ffn_w_up, m_ffn_w_dw, m_ffn_b_dw, m_ffn_w_down, m_final_norm, v_norm_mix, v_norm_ffn, v_conv_w_pw1, v_conv_b_pw1, v_conv_w_dw, v_conv_b_dw, v_conv_ln_g, v_conv_ln_b, v_conv_w_pw2, v_conv_b_pw2, v_pool_w, v_pool_b, v_pool_scale, v_ffn_w_up, v_ffn_w_dw, v_ffn_b_dw, v_ffn_w_down, v_final_norm):
    bsz, seq, d = x.shape
    t = bsz * seq
    k_taps = conv_w_dw.shape[1]
    cs1 = conv_w_pw1.shape[2]
    dsh = d // N_DEV
    cg = d // N_GROUPS
    cgs = pool_w.shape[2]
    fu = ffn_w_up.shape[2]
    fd = ffn_w_down.shape[1]
    dff = fd * N_DEV
    nb = N_DEV // 2
    kf = ffn_w_dw.shape[1]
    fsh = ffn_w_dw.shape[2]
    my = _lin(_me())
    tm = _tile(t, 512)

    x2 = x.reshape(t, d)
    tgt2 = loss_target.reshape(t, d)

    small_shapes = [(k_taps, dsh), (dsh,), (dsh,), (2, kf, fsh)]
    small_mine = _pack([conv_w_dw[0], pool_b[0], pool_scale[0], ffn_w_dw], LANE)
    big = [conv_w_pw1[0], conv_w_pw2[0], ffn_w_up[0], ffn_w_down[0], pool_w[0], ffn_w_up[1], ffn_w_down[1]]
    gather = _exchange_start("gather_start", [small_mine] + [w.astype(BF16) for w in big], [False] * 8)
    h0 = _rms_fwd("l0_rms", x2, norm_mix[0:1])
    small_all, w1 = _exchange_wait("gather_wait_w1", gather, [0, 1], h0)
    parts = [_unpack(small_all[dev], small_shapes) for dev in range(N_DEV)]
    wdw = jnp.concatenate([p[0] for p in parts], axis=1)
    pool_b_full = jnp.concatenate([p[1] for p in parts]).reshape(1, d)
    pool_s_full = jnp.concatenate([p[2] for p in parts]).reshape(1, d)
    fwdw = jnp.concatenate([p[3] for p in parts], axis=2)
    fbdw = ffn_b_dw.reshape(2, 1, dff)

    def columns(w):
        return w.transpose(1, 0, 2).reshape(w.shape[1], N_DEV * w.shape[2])

    def column_shards(w):
        return w.reshape(w.shape[0], N_DEV, w.shape[1] // N_DEV).transpose(1, 0, 2)

    w1 = columns(w1)
    a = _mm("l0_pw1", h0, w1, grid=(2, t // tm, 1),
            a_spec=pl.BlockSpec((tm, d), lambda j, i, k: (i, 0)),
            b_spec=pl.BlockSpec((d, d), lambda j, i, k: (0, j)),
            out_spec=pl.BlockSpec((None, tm, d), lambda j, i, k: (j, i, 0)),
            out_shape=jax.ShapeDtypeStruct((2, t, d), BF16), dims=NN, acc_shape=(tm, d),
            extras=(conv_b_pw1,), extra_specs=(pl.BlockSpec((1, d), lambda j, i, k: (0, j)),),
            epilogue=lambda acc, b: acc + b)
    (w2,) = _exchange_wait("gather_wait_w2", gather, [2], a)
    w2 = w2.reshape(d, d)
    v = _conv_fwd("l0_conv", a, wdw, conv_b_dw, seq)
    s = _ln_fwd("l0_ln", v, conv_ln_g, conv_ln_b)
    r1 = _mm("l0_pw2", s, w2, grid=(t // tm, 1, 1),
             a_spec=pl.BlockSpec((tm, d), lambda i, j, k: (i, 0)),
             b_spec=pl.BlockSpec((d, d), lambda i, j, k: (0, 0)),
             out_spec=pl.BlockSpec((tm, d), lambda i, j, k: (i, 0)),
             out_shape=jax.ShapeDtypeStruct((t, d), F32), dims=NN, acc_shape=(tm, d),
             extras=(conv_b_pw2, x2),
             extra_specs=(pl.BlockSpec((1, d), lambda i, j, k: (0, 0)), pl.BlockSpec((tm, d), lambda i, j, k: (i, 0))),
             epilogue=lambda acc, b, res: res + (acc + b))
    def up_getter(name, idx):
        return lambda after: columns(_exchange_wait(name, gather, [idx], after)[0])

    def down_getter(name, idx):
        return lambda after: _exchange_wait(name, gather, [idx], after)[0].reshape(dff, d)

    r2, ffn0_saved = _ffn_forward("f0", r1, norm_ffn[0:1], up_getter("gather_wait_wu0", 3),
                                  down_getter("gather_wait_wd0", 4), fwdw[0], fbdw[0], seq)
    (wp,) = _exchange_wait("gather_wait_wp", gather, [5], r2)
    wp = wp.transpose(1, 0, 2, 3).reshape(N_GROUPS, cg, cg)
    pooled = _pool_fwd("l1_pool", r2, norm_mix[1:2], seq)
    r3 = _mm("l1_mix", pooled, wp, grid=(t // tm, N_GROUPS, 1),
             a_spec=pl.BlockSpec((tm, cg), lambda i, j, k: (i, j)),
             b_spec=pl.BlockSpec((None, cg, cg), lambda i, j, k: (j, 0, 0)),
             out_spec=pl.BlockSpec((tm, cg), lambda i, j, k: (i, j)),
             out_shape=jax.ShapeDtypeStruct((t, d), F32), dims=NN, acc_shape=(tm, cg),
             extras=(pool_s_full, pool_b_full, r2),
             extra_specs=(pl.BlockSpec((1, cg), lambda i, j, k: (0, j)), pl.BlockSpec((1, cg), lambda i, j, k: (0, j)),
                          pl.BlockSpec((tm, cg), lambda i, j, k: (i, j))),
             epilogue=lambda acc, sc, b, res: res + sc * (acc + b))
    r4, ffn1_saved = _ffn_forward("f1", r3, norm_ffn[1:2], up_getter("gather_wait_wu1", 6),
                                  down_getter("gather_wait_wd1", 7), fwdw[1], fbdw[1], seq)

    dr4, loss_part, dfinal, _ = _final("final", r4, tgt2, final_norm.reshape(1, d))
    dr3, dnf1, dwu1, dwd1, dfw1, dfb1, _ = _ffn_backward("f1", dr4, ffn1_saved, norm_ffn[1:2], fwdw[1], fbdw[1], seq)
    scatter_a = _exchange_start("scatter_f1_start", [column_shards(dwu1), dwd1.reshape(N_DEV, fd, d)], [True, True])
    dpooled, dmixed, dpool_s, dpool_b = _pool_bwd_mm("l1_dmix", pooled, wp, dr3,
                                                     pool_s_full + scatter_a["token"][0:1, 0:1], pool_b_full)
    dwp = _mm("l1_dwp", pooled, dmixed, grid=(N_GROUPS, 1, t // tm),
              a_spec=pl.BlockSpec((tm, cg), lambda i, j, k: (k, i)),
              b_spec=pl.BlockSpec((tm, cg), lambda i, j, k: (k, i)),
              out_spec=pl.BlockSpec((None, cg, cg), lambda i, j, k: (i, 0, 0)),
              out_shape=jax.ShapeDtypeStruct((N_GROUPS, cg, cg), BF16), dims=TN, acc_shape=(cg, cg))
    dr2, dnm1 = _pool_bwd("l1_dpool", dpooled, r2, norm_mix[1:2], dr3, seq)
    dr1, dnf0, dwu0, dwd0, dfw0, dfb0, db2 = _ffn_backward("f0", dr2, ffn0_saved, norm_ffn[0:1], fwdw[0], fbdw[0], seq)
    dwp_b = dwp.reshape(N_GROUPS, N_DEV, cgs, cg).transpose(1, 0, 2, 3)
    scatter_b = _exchange_start("scatter_f0_start", [column_shards(dwu0), dwd0.reshape(N_DEV, fd, d), dwp_b],
                                [True] * 3)
    ds = _mm("l0_ds", dr1, w2, grid=(t // tm, 1, 1),
             a_spec=pl.BlockSpec((tm, d), lambda i, j, k: (i, 0)),
             b_spec=pl.BlockSpec((d, d), lambda i, j, k: (0, 0)),
             out_spec=pl.BlockSpec((tm, d), lambda i, j, k: (i, 0)),
             out_shape=jax.ShapeDtypeStruct((t, d), F32), dims=NT, acc_shape=(tm, d), token=scatter_b["token"])
    dw2 = _mm("l0_dw2", s, dr1, grid=(1, 1, t // tm),
              a_spec=pl.BlockSpec((tm, d), lambda i, j, k: (k, 0)),
              b_spec=pl.BlockSpec((tm, d), lambda i, j, k: (k, 0)),
              out_spec=pl.BlockSpec((d, d), lambda i, j, k: (0, 0)),
              out_shape=jax.ShapeDtypeStruct((d, d), BF16), dims=TN, acc_shape=(d, d))
    dv, dlg, dlb, dbdw = _ln_bwd("l0_dln", ds, v, conv_ln_g, conv_ln_b)
    da, dwdw, db1 = _conv_bwd("l0_dconv", a, dv, wdw, seq)
    tk = _tile(t, 1024)
    dw1 = _mm("l0_dw1", h0, da, grid=(1, 2, t // tk),
              a_spec=pl.BlockSpec((tk, d), lambda i, j, k: (k, 0)),
              b_spec=pl.BlockSpec((None, tk, d), lambda i, j, k: (j, k, 0)),
              out_spec=pl.BlockSpec((d, d), lambda i, j, k: (0, j)),
              out_shape=jax.ShapeDtypeStruct((d, 2 * d), BF16), dims=TN, acc_shape=(d, d))
    scatter_c = _exchange_start("scatter_l0_start", [column_shards(dw1), dw2.reshape(N_DEV, d // N_DEV, d)],
                                [True, True])
    dh0 = _mm("l0_dh", da, w1, grid=(t // tm, 1, 2),
              a_spec=pl.BlockSpec((None, tm, d), lambda i, j, k: (k, i, 0)),
              b_spec=pl.BlockSpec((d, d), lambda i, j, k: (0, k)),
              out_spec=pl.BlockSpec((tm, d), lambda i, j, k: (i, 0)),
              out_shape=jax.ShapeDtypeStruct((t, d), F32), dims=NT, acc_shape=(tm, d), token=scatter_c["token"])
    dx, dnm0, _ = _rms_bwd("l0_drms", dh0, x2, norm_mix[0:1], dr1)

    dffn_w = jnp.stack([dfw0, dfw1])
    dffn_b = jnp.stack([dfb0, dfb1]).reshape(2, dff)
    small_parts = [loss_part, jnp.concatenate([dnm0, dnm1]), jnp.concatenate([dnf0, dnf1]), db1, dwdw, dbdw, dlg, dlb,
                   db2, dpool_b, dpool_s, dffn_w, dffn_b, dfinal]
    small_part_shapes = [(1,), (2, d), (2, d), (1, 2 * d), (k_taps, d), (1, d), (1, d), (1, d), (1, d), (1, d), (1, d),
                         (2, kf, dff), (2, dff), (d,)]
    packed = _pack(small_parts, 8 * LANE)
    gather_small = _exchange_start("gather_small_start", [packed], [False])

    def big_update(name, recv, w, m, v, layer=0, prev=None):
        shape = w.shape
        c = recv.shape[-1]
        rows = recv.size // (N_DEV * c)
        nl = w.size // (rows * c)
        outs = _adamw(name, recv.reshape(N_DEV, rows, c), w.reshape(nl, rows, c), m.reshape(nl, rows, c),
                      v.reshape(nl, rows, c), layer, prev)
        return outs, [o.reshape(shape) for o in outs]

    g_wu1, g_wd1 = _exchange_wait("scatter_f1_wait", scatter_a, [0, 1], gather_small["token"])
    raw_wu, _ = big_update("adam_wu1", g_wu1, ffn_w_up, m_ffn_w_up, v_ffn_w_up, 1)
    raw_wd, _ = big_update("adam_wd1", g_wd1, ffn_w_down, m_ffn_w_down, v_ffn_w_down, 1)
    g_wu0, g_wd0, g_wp = _exchange_wait("scatter_f0_wait", scatter_b, [0, 1, 2], raw_wd[0])
    _, u_wu = big_update("adam_wu0", g_wu0, ffn_w_up, m_ffn_w_up, v_ffn_w_up, 0, raw_wu)
    _, u_wd = big_update("adam_wd0", g_wd0, ffn_w_down, m_ffn_w_down, v_ffn_w_down, 0, raw_wd)
    _, u_wp = big_update("adam_wp", g_wp, pool_w, m_pool_w, v_pool_w)
    g_w1, g_w2 = _exchange_wait("scatter_l0_wait", scatter_c, [0, 1], u_wp[0])
    _, u_w1 = big_update("adam_w1", g_w1, conv_w_pw1, m_conv_w_pw1, v_conv_w_pw1)
    _, u_w2 = big_update("adam_w2", g_w2, conv_w_pw2, m_conv_w_pw2, v_conv_w_pw2)
    (all_small,) = _exchange_wait("gather_small_wait", gather_small, [0], u_w2[0])
    summed = _sum_rows("sum_small_grads", all_small)
    (loss_v, g_nm, g_nf, g_b1, g_wdw, g_bdw, g_lg, g_lb, g_b2, g_pb, g_ps, g_fw, g_fb,
     g_fin) = _unpack(summed, small_part_shapes)
    loss = loss_v[0]
    g_wdw_mine = lax.dynamic_slice_in_dim(g_wdw, my * dsh, dsh, axis=1)[None]
    g_pb_mine = lax.dynamic_slice_in_dim(g_pb, my * dsh, dsh, axis=1)
    g_ps_mine = lax.dynamic_slice_in_dim(g_ps, my * dsh, dsh, axis=1)
    g_fw_mine = lax.dynamic_slice_in_dim(g_fw, my * fsh, fsh, axis=2)

    small_g =[g_nm, g_nf, g_b1, g_wdw_mine, g_bdw, g_lg, g_lb, g_b2, g_pb_mine, g_ps_mine, g_fw_mine, g_fb, g_fin]
    small_w = [norm_mix, norm_ffn, conv_b_pw1, conv_w_dw, conv_b_dw, conv_ln_g, conv_ln_b, conv_b_pw2, pool_b,
               pool_scale, ffn_w_dw, ffn_b_dw, final_norm]
    small_m = [m_norm_mix, m_norm_ffn, m_conv_b_pw1, m_conv_w_dw, m_conv_b_dw, m_conv_ln_g, m_conv_ln_b,
               m_conv_b_pw2, m_pool_b, m_pool_scale, m_ffn_w_dw, m_ffn_b_dw, m_final_norm]
    small_v = [v_norm_mix, v_norm_ffn, v_conv_b_pw1, v_conv_w_dw, v_conv_b_dw, v_conv_ln_g, v_conv_ln_b,
               v_conv_b_pw2, v_pool_b, v_pool_scale, v_ffn_w_dw, v_ffn_b_dw, v_final_norm]
    shapes = [w.shape for w in small_w]
    outs = _adamw("adam_small", _pack(small_g, 8 * LANE)[None], _pack(small_w, 8 * LANE)[None],
                  _pack(small_m, 8 * LANE)[None], _pack(small_v, 8 * LANE)[None])
    sg, sd, sm, sv = [_unpack(o, shapes) for o in outs]

    def leaf(kind):
        (nm, nf, b1, wdw_, bdw_, lg, lb, b2, pb, ps, fw, fb, fin) = (sg, sd, sm, sv)[kind]
        return [nm, nf, u_w1[kind], b1, wdw_, bdw_, lg, lb, u_w2[kind], b2, u_wp[kind], pb, ps, u_wu[kind], fw, fb,
                u_wd[kind], fin]

    return (loss, dx.reshape(bsz, seq, d), *leaf(0), *leaf(1), *leaf(2), *leaf(3))
```

```python
import functools

import jax
import jax.numpy as jnp
from jax import lax
from jax.experimental import pallas as pl
from jax.experimental.pallas import tpu as pltpu

F32 = jnp.float32
BF16 = jnp.bfloat16
MESH = pl.DeviceIdType.MESH
HBM = pl.BlockSpec(memory_space=pltpu.HBM)

N_DEV = 8
RMS_EPS = 1e-6
LN_EPS = 1e-5
POOL_WINDOWS = (2, 4, 8, 16)
N_GROUPS = len(POOL_WINDOWS)
ADAM_LR = 0.001
ADAM_B1 = 0.9
ADAM_B2 = 0.999
ADAM_EPS = 1e-08
ADAM_WD = 0.01
ADAM_STEP = 10

LANE = 128
HALO = 32
HALO16 = 16
VMEM_LIMIT = 56 * 1024 * 1024


def _params(*sem):
    return pltpu.CompilerParams(dimension_semantics=sem if sem else None, vmem_limit_bytes=VMEM_LIMIT)


def _tile(n, pref):
    for t in range(min(pref, n), 15, -1):
        if n % t == 0 and t % 16 == 0:
            return t
    return n


def _sigmoid(z):
    return 1.0 / (1.0 + jnp.exp(-z))


def _me():
    return lax.axis_index("x"), lax.axis_index("y"), lax.axis_index("c")


def _flip(pos, m):
    x, y, c = pos
    return ((1 - x) if m & 4 else x, (1 - y) if m & 2 else y, (1 - c) if m & 1 else c)


def _lin(pos):
    return 4 * pos[0] + 2 * pos[1] + pos[2]


SEM = pl.BlockSpec(memory_space=pltpu.SEMAPHORE)
ANY = pl.BlockSpec(memory_space=pl.ANY)
EFFECT = pltpu.SideEffectType.DATAFLOW_SIDE_EFFECTING


def _exchange_copies(srcs, lands, send_sems, recv_sems, modes, which, starting):
    me = _me()
    my = _lin(me)
    out = []
    for pos, a in enumerate(which):
        src, land = srcs[pos], lands[pos]

        def block(pid, src=src, a=a):
            return src.at[pid] if modes[a] else src

        local = pltpu.make_async_copy(block(my), land.at[my], send_sems.at[a * N_DEV])
        remote = []
        for m in range(1, N_DEV):
            peer = _flip(me, m)
            pid = _lin(peer)
            sems = dict(send_sem=send_sems.at[a * N_DEV + m], recv_sem=recv_sems.at[a * N_DEV + m],
                        device_id=peer, device_id_type=MESH)
            if starting:
                remote.append(pltpu.make_async_remote_copy(src_ref=block(pid), dst_ref=land.at[my], **sems))
            else:
                remote.append((pltpu.make_async_remote_copy(src_ref=block(pid), dst_ref=land.at[my], **sems),
                               pltpu.make_async_remote_copy(src_ref=block(pid), dst_ref=land.at[pid], **sems)))
        out.append((local, remote))
    return out


def _exchange_start(name, arrs, modes):
    n = len(arrs)
    blocks = [a.shape[1:] if md else a.shape for a, md in zip(arrs, modes)]

    def body(*refs):
        srcs, lands = refs[:n], refs[n:2 * n]
        send_sems, recv_sems = refs[2 * n], refs[2 * n + 1]
        token = refs[-1]
        for local, remote in _exchange_copies(srcs, lands, send_sems, recv_sems, modes, list(range(n)), True):
            local.start()
            for send in remote:
                send.start()
        token[...] = jnp.zeros_like(token)

    lands = [lax.empty((N_DEV,) + tuple(b), a.dtype) for a, b in zip(arrs, blocks)]
    outs = pl.pallas_call(
        body, name=name,
        out_shape=(pltpu.SemaphoreType.DMA((n * N_DEV,)), pltpu.SemaphoreType.DMA((n * N_DEV,)),
                   *[pltpu.HBM(a.shape, a.dtype) for a in arrs], *[pltpu.HBM(l.shape, l.dtype) for l in lands],
                   jax.ShapeDtypeStruct((8, LANE), F32)),
        in_specs=[HBM] * (2 * n),
        out_specs=(SEM, SEM, *[HBM] * (2 * n), pl.BlockSpec(memory_space=pltpu.VMEM)),
        input_output_aliases={i: 2 + i for i in range(2 * n)},
        compiler_params=pltpu.CompilerParams(has_side_effects=EFFECT),
    )(*[pltpu.with_memory_space_constraint(a, pltpu.HBM) for a in arrs],
      *[pltpu.with_memory_space_constraint(l, pltpu.HBM) for l in lands])
    return dict(send=outs[0], recv=outs[1], srcs=list(outs[2:2 + n]), lands=list(outs[2 + n:2 + 2 * n]),
                modes=modes, token=outs[-1])


def _exchange_wait(name, handle, which, after):
    k = len(which)
    modes = handle["modes"]

    def body(*refs):
        srcs, lands = refs[:k], refs[k:2 * k]
        send_sems, recv_sems = refs[2 * k], refs[2 * k + 1]
        for local, remote in _exchange_copies(srcs, lands, send_sems, recv_sems, modes, which, False):
            local.wait()
            for send, arrival in remote:
                send.wait_send()
                arrival.wait_recv()

    srcs = [handle["srcs"][a] for a in which]
    lands = [handle["lands"][a] for a in which]
    outs = pl.pallas_call(
        body, name=name,
        out_shape=tuple(pltpu.HBM(x.shape, x.dtype) for x in srcs + lands),
        in_specs=[HBM] * (2 * k) + [SEM, SEM, ANY], out_specs=tuple([HBM] * (2 * k)),
        input_output_aliases={i: i for i in range(2 * k)},
        compiler_params=pltpu.CompilerParams(has_side_effects=EFFECT),
    )(*srcs, *lands, handle["send"], handle["recv"], after)
    return list(outs[k:])


def _mm(name, a, b, *, grid, a_spec, b_spec, out_spec, out_shape, dims, acc_shape, extras=(), extra_specs=(),
        epilogue=None, token=None):
    nk = grid[2]
    ne = len(extras)
    deps = () if token is None else (token,)
    dep_specs = [pl.BlockSpec((8, LANE), lambda i, j, k: (0, 0))] * len(deps)
    n_out = len(out_shape) if isinstance(out_shape, (list, tuple)) else 1

    def body(a_ref, b_ref, *rest):
        ex, o_refs, acc_ref = rest[:ne], rest[ne + len(deps):ne + len(deps) + n_out], rest[ne + len(deps) + n_out]
        k = pl.program_id(2)
        part = lax.dot_general(a_ref[...].astype(BF16), b_ref[...].astype(BF16), (dims, ((), ())),
                               preferred_element_type=F32)

        def finish(r):
            if epilogue is not None:
                r = epilogue(r, *[e[...] for e in ex])
            for o_ref, val in zip(o_refs, r if isinstance(r, tuple) else (r,)):
                o_ref[...] = val.astype(o_ref.dtype)

        if nk == 1:
            finish(part)
            return

        @pl.when(k == 0)
        def _():
            acc_ref[...] = part

        @pl.when((k > 0) & (k < nk - 1))
        def _():
            acc_ref[...] += part

        @pl.when(k == nk - 1)
        def _():
            finish(acc_ref[...] + part)

    return pl.pallas_call(
        body, name=name, grid=grid, in_specs=[a_spec, b_spec, *extra_specs, *dep_specs], out_specs=out_spec,
        out_shape=out_shape, scratch_shapes=[pltpu.VMEM(acc_shape if nk > 1 else (8, LANE), F32)],
        compiler_params=_params("parallel", "parallel", "arbitrary"),
    )(a, b, *extras, *deps)


NN = ((1,), (0,))
NT = ((1,), (1,))
TN = ((0,), (0,))


def _rms_fwd(name, x, gain):
    t, d = x.shape
    tr = _tile(t, 512)

    def body(x_ref, g_ref, h_ref):
        xv = x_ref[...]
        rstd = lax.rsqrt(jnp.mean(xv * xv, axis=-1, keepdims=True) + RMS_EPS)
        h_ref[...] = (xv * rstd * g_ref[...]).astype(BF16)

    return pl.pallas_call(
        body, name=name, grid=(t // tr,),
        in_specs=[pl.BlockSpec((tr, d), lambda i: (i, 0)), pl.BlockSpec((1, d), lambda i: (0, 0))],
        out_specs=pl.BlockSpec((tr, d), lambda i: (i, 0)),
        out_shape=jax.ShapeDtypeStruct((t, d), BF16), compiler_params=_params("parallel"),
    )(x, gain)


def _rms_bwd(name, dh, x, gain, dres):
    t, d = x.shape
    tr = _tile(t, 512)

    def body(dh_ref, x_ref, g_ref, dres_ref, dx_ref, dg_ref, cs_ref):
        i = pl.program_id(0)
        xv = x_ref[...]
        rstd = lax.rsqrt(jnp.mean(xv * xv, axis=-1, keepdims=True) + RMS_EPS)
        xhat = xv * rstd
        dhv = dh_ref[...]
        dxhat = dhv * g_ref[...]
        dx = dres_ref[...] + rstd * (dxhat - xhat * jnp.mean(dxhat * xhat, axis=-1, keepdims=True))
        dx_ref[...] = dx

        @pl.when(i == 0)
        def _():
            dg_ref[...] = jnp.zeros_like(dg_ref)
            cs_ref[...] = jnp.zeros_like(cs_ref)

        dg_ref[...] += jnp.sum(dhv * xhat, axis=0, keepdims=True)
        cs_ref[...] += jnp.sum(dx, axis=0, keepdims=True)

    row = pl.BlockSpec((tr, d), lambda i: (i, 0))
    vec = pl.BlockSpec((1, d), lambda i: (0, 0))
    return pl.pallas_call(
        body, name=name, grid=(t // tr,), in_specs=[row, row, vec, row], out_specs=[row, vec, vec],
        out_shape=[jax.ShapeDtypeStruct((t, d), F32), jax.ShapeDtypeStruct((1, d), F32),
                   jax.ShapeDtypeStruct((1, d), F32)],
        compiler_params=_params("arbitrary"),
    )(dh, x, gain, dres)


def _final(name, x, tgt, gain):
    t, d = x.shape
    tr = _tile(t, 512)

    def body(x_ref, t_ref, g_ref, dx_ref, loss_ref, dg_ref, cs_ref):
        i = pl.program_id(0)
        xv = x_ref[...]
        g = g_ref[...]
        rstd = lax.rsqrt(jnp.mean(xv * xv, axis=-1, keepdims=True) + RMS_EPS)
        xhat = xv * rstd
        err = xhat * g - t_ref[...]
        dy = err / d
        dxhat = dy * g
        dx = rstd * (dxhat - xhat * jnp.mean(dxhat * xhat, axis=-1, keepdims=True))
        dx_ref[...] = dx

        @pl.when(i == 0)
        def _():
            loss_ref[...] = jnp.zeros_like(loss_ref)
            dg_ref[...] = jnp.zeros_like(dg_ref)
            cs_ref[...] = jnp.zeros_like(cs_ref)

        loss_ref[...] += 0.5 * jnp.sum(jnp.mean(err * err, axis=-1, keepdims=True), axis=0, keepdims=True)
        dg_ref[...] += jnp.sum(dy * xhat, axis=0, keepdims=True)
        cs_ref[...] += jnp.sum(dx, axis=0, keepdims=True)

    row = pl.BlockSpec((tr, d), lambda i: (i, 0))
    vec = pl.BlockSpec((1, d), lambda i: (0, 0))
    one = pl.BlockSpec((1, 1), lambda i: (0, 0))
    return pl.pallas_call(
        body, name=name, grid=(t // tr,), in_specs=[row, row, vec], out_specs=[row, one, vec, vec],
        out_shape=[jax.ShapeDtypeStruct((t, d), F32), jax.ShapeDtypeStruct((1, 1), F32),
                   jax.ShapeDtypeStruct((1, d), F32), jax.ShapeDtypeStruct((1, d), F32)],
        compiler_params=_params("arbitrary"),
    )(x, tgt, gain)


def _ln_fwd(name, v, g, b):
    t, d = v.shape
    tr = _tile(t, 512)

    def body(v_ref, g_ref, b_ref, s_ref):
        vv = v_ref[...]
        mu = jnp.mean(vv, axis=-1, keepdims=True)
        cen = vv - mu
        rstd = lax.rsqrt(jnp.mean(cen * cen, axis=-1, keepdims=True) + LN_EPS)
        z = cen * rstd * g_ref[...] + b_ref[...]
        s_ref[...] = (z * _sigmoid(z)).astype(BF16)

    row = pl.BlockSpec((tr, d), lambda i: (i, 0))
    vec = pl.BlockSpec((1, d), lambda i: (0, 0))
    return pl.pallas_call(
        body, name=name, grid=(t // tr,), in_specs=[row, vec, vec], out_specs=row,
        out_shape=jax.ShapeDtypeStruct((t, d), BF16), compiler_params=_params("parallel"),
    )(v, g, b)


def _ln_bwd(name, ds, v, g, b):
    t, d = v.shape
    tr = _tile(t, 512)

    def body(ds_ref, v_ref, g_ref, b_ref, dv_ref, dg_ref, db_ref, cs_ref):
        i = pl.program_id(0)
        vv = v_ref[...]
        g = g_ref[...]
        mu = jnp.mean(vv, axis=-1, keepdims=True)
        cen = vv - mu
        rstd = lax.rsqrt(jnp.mean(cen * cen, axis=-1, keepdims=True) + LN_EPS)
        y = cen * rstd
        z = y * g + b_ref[...]
        sig = _sigmoid(z)
        dz = ds_ref[...] * sig * (1.0 + z * (1.0 - sig))
        dy = dz * g
        dv = rstd * (dy - jnp.mean(dy, axis=-1, keepdims=True) - y * jnp.mean(dy * y, axis=-1, keepdims=True))
        dv_ref[...] = dv

        @pl.when(i == 0)
        def _():
            dg_ref[...] = jnp.zeros_like(dg_ref)
            db_ref[...] = jnp.zeros_like(db_ref)
            cs_ref[...] = jnp.zeros_like(cs_ref)

        dg_ref[...] += jnp.sum(dz * y, axis=0, keepdims=True)
        db_ref[...] += jnp.sum(dz, axis=0, keepdims=True)
        cs_ref[...] += jnp.sum(dv, axis=0, keepdims=True)

    row = pl.BlockSpec((tr, d), lambda i: (i, 0))
    vec = pl.BlockSpec((1, d), lambda i: (0, 0))
    return pl.pallas_call(
        body, name=name, grid=(t // tr,), in_specs=[row, row, vec, vec], out_specs=[row, vec, vec, vec],
        out_shape=[jax.ShapeDtypeStruct((t, d), F32)] + [jax.ShapeDtypeStruct((1, d), F32)] * 3,
        compiler_params=_params("arbitrary"),
    )(ds, v, g, b)


def _conv_tiles(t, seq):
    ts = _tile(seq, 512)
    return ts, seq // ts, _tile(ts, 64)


def _conv_fwd(name, a, w, b, seq):
    _, t, d = a.shape
    k_taps = w.shape[0]
    ts, tps, rc = _conv_tiles(t, seq)
    hb = ts // HALO

    def body(cur_ref, prev_ref, w_ref, b_ref, v_ref, upad):
        i = pl.program_id(1)
        first = (i % tps) == 0
        pv = prev_ref[0].astype(F32)
        pg = prev_ref[1].astype(F32)
        upad[0:HALO, :] = jnp.where(first, 0.0, pv * _sigmoid(pg))
        upad[HALO:HALO + ts, :] = cur_ref[0].astype(F32) * _sigmoid(cur_ref[1].astype(F32))
        wv = w_ref[...]
        bias = jnp.broadcast_to(b_ref[...], (rc, LANE))
        for r0 in range(0, ts, rc):
            acc = bias
            for k in range(k_taps):
                acc = acc + wv[k:k + 1, :] * upad[pl.ds(HALO - (k_taps - 1) + k + r0, rc), :]
            v_ref[pl.ds(r0, rc), :] = acc

    return pl.pallas_call(
        body, name=name, grid=(d // LANE, t // ts),
        in_specs=[pl.BlockSpec((2, ts, LANE), lambda c, i: (0, i, c)),
                  pl.BlockSpec((2, HALO, LANE), lambda c, i: (0, jnp.maximum(i * hb - 1, 0), c)),
                  pl.BlockSpec((k_taps, LANE), lambda c, i: (0, c)),
                  pl.BlockSpec((1, LANE), lambda c, i: (0, c))],
        out_specs=pl.BlockSpec((ts, LANE), lambda c, i: (i, c)),
        out_shape=jax.ShapeDtypeStruct((t, d), F32),
        scratch_shapes=[pltpu.VMEM((HALO + ts, LANE), F32)],
        compiler_params=_params("parallel", "parallel"),
    )(a, a, w, b)


def _conv_bwd(name, a, dv, w, seq):
    _, t, d = a.shape
    k_taps = w.shape[0]
    ts, tps, rc = _conv_tiles(t, seq)
    hb = ts // HALO
    nhb = t // HALO

    def body(cur_ref, prev_ref, dv_ref, ndv_ref, w_ref, da_ref, dw_ref, dbp_ref, upad, dvpad, dwrows):
        i = pl.program_id(1)
        first = (i % tps) == 0
        last = (i % tps) == tps - 1
        pv = prev_ref[0].astype(F32)
        pg = prev_ref[1].astype(F32)
        upad[0:HALO, :] = jnp.where(first, 0.0, pv * _sigmoid(pg))
        upad[HALO:HALO + ts, :] = cur_ref[0].astype(F32) * _sigmoid(cur_ref[1].astype(F32))
        dvpad[0:ts, :] = dv_ref[...]
        dvpad[ts:ts + HALO, :] = jnp.where(last, 0.0, ndv_ref[...])
        wv = w_ref[...]

        @pl.when(i == 0)
        def _():
            dw_ref[...] = jnp.zeros_like(dw_ref)
            dbp_ref[...] = jnp.zeros_like(dbp_ref)

        sv = jnp.zeros((1, LANE), F32)
        sg = jnp.zeros((1, LANE), F32)
        for r0 in range(0, ts, rc):
            du = jnp.zeros((rc, LANE), F32)
            for k in range(k_taps):
                du = du + wv[k:k + 1, :] * dvpad[pl.ds(r0 + (k_taps - 1) - k, rc), :]
            av = cur_ref[0, pl.ds(r0, rc), :].astype(F32)
            sig = _sigmoid(cur_ref[1, pl.ds(r0, rc), :].astype(F32))
            dval = du * sig
            dgate = du * av * sig * (1.0 - sig)
            da_ref[0, pl.ds(r0, rc), :] = dval.astype(BF16)
            da_ref[1, pl.ds(r0, rc), :] = dgate.astype(BF16)
            sv = sv + jnp.sum(dval, axis=0, keepdims=True)
            sg = sg + jnp.sum(dgate, axis=0, keepdims=True)
        dbp_ref[0] += sv
        dbp_ref[1] += sg

        for k in range(k_taps):
            acc = jnp.zeros((rc, LANE), F32)
            for r0 in range(0, ts, rc):
                acc = acc + dvpad[pl.ds(r0, rc), :] * upad[pl.ds(HALO - (k_taps - 1) + k + r0, rc), :]
            dwrows[k:k + 1, :] = jnp.sum(acc, axis=0, keepdims=True)
        dw_ref[...] += dwrows[0:k_taps, :]

    return pl.pallas_call(
        body, name=name, grid=(d // LANE, t // ts),
        in_specs=[pl.BlockSpec((2, ts, LANE), lambda c, i: (0, i, c)),
                  pl.BlockSpec((2, HALO, LANE), lambda c, i: (0, jnp.maximum(i * hb - 1, 0), c)),
                  pl.BlockSpec((ts, LANE), lambda c, i: (i, c)),
                  pl.BlockSpec((HALO, LANE), lambda c, i: (jnp.minimum((i + 1) * hb, nhb - 1), c)),
                  pl.BlockSpec((k_taps, LANE), lambda c, i: (0, c))],
        out_specs=[pl.BlockSpec((2, ts, LANE), lambda c, i: (0, i, c)),
                   pl.BlockSpec((k_taps, LANE), lambda c, i: (0, c)),
                   pl.BlockSpec((2, 1, LANE), lambda c, i: (0, 0, c))],
        out_shape=[jax.ShapeDtypeStruct((2, t, d), BF16), jax.ShapeDtypeStruct((k_taps, d), F32),
                   jax.ShapeDtypeStruct((2, 1, d), F32)],
        scratch_shapes=[pltpu.VMEM((HALO + ts, LANE), F32), pltpu.VMEM((ts + HALO, LANE), F32),
                        pltpu.VMEM((HALO, LANE), F32)],
        compiler_params=_params("parallel", "arbitrary"),
    )(a, a, dv, dv, w)


def _pool_fwd(name, x, gain, seq):
    t, d = x.shape
    ts = _tile(seq, 256)
    tps = seq // ts
    hb = ts // HALO
    cg = d // N_GROUPS

    def body(cur_ref, prev_ref, g_ref, o_ref, hpad):
        i = pl.program_id(0)
        first = (i % tps) == 0
        g = g_ref[...]

        def norm(xv):
            return xv * lax.rsqrt(jnp.mean(xv * xv, axis=-1, keepdims=True) + RMS_EPS) * g

        hpad[0:HALO, :] = jnp.where(first, 0.0, norm(prev_ref[...]))
        hpad[HALO:HALO + ts, :] = norm(cur_ref[...])
        pos = (i % tps) * ts + lax.broadcasted_iota(jnp.int32, (ts, 1), 0)
        for gi, win in enumerate(POOL_WINDOWS):
            sl = slice(gi * cg, (gi + 1) * cg)
            own = hpad[HALO:HALO + ts, sl]
            acc = own
            for j in range(1, win):
                acc = acc + hpad[HALO - j:HALO - j + ts, sl]
            cnt = jnp.minimum(pos + 1, win).astype(F32)
            o_ref[:, sl] = (acc / cnt - own).astype(BF16)

    return pl.pallas_call(
        body, name=name, grid=(t // ts,),
        in_specs=[pl.BlockSpec((ts, d), lambda i: (i, 0)),
                  pl.BlockSpec((HALO, d), lambda i: (jnp.maximum(i * hb - 1, 0), 0)),
                  pl.BlockSpec((1, d), lambda i: (0, 0))],
        out_specs=pl.BlockSpec((ts, d), lambda i: (i, 0)),
        out_shape=jax.ShapeDtypeStruct((t, d), BF16),
        scratch_shapes=[pltpu.VMEM((HALO + ts, d), F32)],
        compiler_params=_params("parallel"),
    )(x, x, gain)


def _pool_bwd_mm(name, pooled, wp, dp, scale, bias):
    t, d = pooled.shape
    cg = d // N_GROUPS
    tm = _tile(t, 512)

    def body(p_ref, w_ref, dp_ref, s_ref, b_ref, dpo_ref, dmx_ref, ds_ref, db_ref):
        i = pl.program_id(1)
        wv = w_ref[...]
        mixed = jnp.dot(p_ref[...], wv, preferred_element_type=F32)
        dpv = dp_ref[...]
        dmx = dpv * s_ref[...]
        dmx16 = dmx.astype(BF16)
        dmx_ref[...] = dmx16
        dpo_ref[...] = lax.dot_general(dmx16, wv, (NT, ((), ())), preferred_element_type=F32)

        @pl.when(i == 0)
        def _():
            ds_ref[...] = jnp.zeros_like(ds_ref)
            db_ref[...] = jnp.zeros_like(db_ref)

        ds_ref[...] += jnp.sum(dpv * (mixed + b_ref[...]), axis=0, keepdims=True)
        db_ref[...] += jnp.sum(dmx, axis=0, keepdims=True)

    blk = pl.BlockSpec((tm, cg), lambda g, i: (i, g))
    vec = pl.BlockSpec((1, cg), lambda g, i: (0, g))
    return pl.pallas_call(
        body, name=name, grid=(N_GROUPS, t // tm),
        in_specs=[blk, pl.BlockSpec((None, cg, cg), lambda g, i: (g, 0, 0)), blk, vec, vec],
        out_specs=[blk, blk, vec, vec],
        out_shape=[jax.ShapeDtypeStruct((t, d), F32), jax.ShapeDtypeStruct((t, d), BF16),
                   jax.ShapeDtypeStruct((1, d), F32), jax.ShapeDtypeStruct((1, d), F32)],
        compiler_params=_params("parallel", "arbitrary"),
    )(pooled, wp, dp, scale, bias)


def _pool_bwd(name, dpooled, x, gain, dres, seq):
    t, d = x.shape
    ts = _tile(seq, 256)
    tps = seq // ts
    hb = ts // HALO
    nhb = t // HALO
    cg = d // N_GROUPS

    def body(dpo_ref, ndpo_ref, x_ref, g_ref, dres_ref, dx_ref, dg_ref, qpad, dh):
        i = pl.program_id(0)
        last = (i % tps) == tps - 1
        pos = (i % tps) * ts + lax.broadcasted_iota(jnp.int32, (ts, 1), 0)
        for gi, win in enumerate(POOL_WINDOWS):
            sl = slice(gi * cg, (gi + 1) * cg)
            cur = dpo_ref[:, sl]
            qpad[0:ts, sl] = cur / jnp.minimum(pos + 1, win).astype(F32)
            qpad[ts:ts + HALO, sl] = jnp.where(last, 0.0, ndpo_ref[:, sl] / float(win))
            acc = -cur
            for j in range(win):
                acc = acc + qpad[j:j + ts, sl]
            dh[:, sl] = acc
        xv = x_ref[...]
        rstd = lax.rsqrt(jnp.mean(xv * xv, axis=-1, keepdims=True) + RMS_EPS)
        xhat = xv * rstd
        dhv = dh[...]
        dxhat = dhv * g_ref[...]
        dx_ref[...] = dres_ref[...] + rstd * (dxhat - xhat * jnp.mean(dxhat * xhat, axis=-1, keepdims=True))

        @pl.when(i == 0)
        def _():
            dg_ref[...] = jnp.zeros_like(dg_ref)

        dg_ref[...] += jnp.sum(dhv * xhat, axis=0, keepdims=True)

    row = pl.BlockSpec((ts, d), lambda i: (i, 0))
    vec = pl.BlockSpec((1, d), lambda i: (0, 0))
    return pl.pallas_call(
        body, name=name, grid=(t // ts,),
        in_specs=[row, pl.BlockSpec((HALO, d), lambda i: (jnp.minimum((i + 1) * hb, nhb - 1), 0)), row, vec, row],
        out_specs=[row, vec],
        out_shape=[jax.ShapeDtypeStruct((t, d), F32), jax.ShapeDtypeStruct((1, d), F32)],
        scratch_shapes=[pltpu.VMEM((ts + HALO, d), F32), pltpu.VMEM((ts, d), F32)],
        compiler_params=_params("arbitrary"),
    )(dpooled, dpooled, x, gain, dres)


def _ctile(n, pref):
    return max(c for c in range(LANE, min(pref, n) + 1, LANE) if n % c == 0)


FFN_COLS = 1408
FFN_ROWS = 32


def _ffn_fwd(name, up, w, b, seq):
    _, t, dff = up.shape
    f = _ctile(dff, FFN_COLS)
    k_taps = w.shape[0]
    ts = _tile(seq, 256)
    tps = seq // ts
    hb = ts // HALO16
    rc = _tile(ts, FFN_ROWS)

    def body(cur_ref, prev_ref, w_ref, b_ref, g_ref, apad):
        i = pl.program_id(1)
        first = (i % tps) == 0
        for ci, c0 in enumerate(range(0, f, LANE)):
            cols = slice(c0, c0 + LANE)
            apad[ci, 0:HALO16, :] = jnp.where(first, 0.0, prev_ref[:, cols].astype(F32))
            apad[ci, HALO16:HALO16 + ts, :] = cur_ref[0, :, cols].astype(F32)
            wv = w_ref[:, cols]
            wk = [jnp.broadcast_to(wv[k:k + 1, :], (rc, LANE)) for k in range(k_taps)]
            bias = jnp.broadcast_to(b_ref[:, cols], (rc, LANE))
            for r0 in range(0, ts, rc):
                c = bias
                for k in range(k_taps):
                    c = c + wk[k] * apad[ci, pl.ds(HALO16 - (k_taps - 1) + k + r0, rc), :]
                gate = cur_ref[1, pl.ds(r0, rc), cols].astype(F32)
                g_ref[pl.ds(r0, rc), cols] = (c * _sigmoid(c) * gate).astype(BF16)

    return pl.pallas_call(
        body, name=name, grid=(dff // f, t // ts),
        in_specs=[pl.BlockSpec((2, ts, f), lambda j, i: (0, i, j)),
                  pl.BlockSpec((None, HALO16, f), lambda j, i: (0, jnp.maximum(i * hb - 1, 0), j)),
                  pl.BlockSpec((k_taps, f), lambda j, i: (0, j)),
                  pl.BlockSpec((1, f), lambda j, i: (0, j))],
        out_specs=pl.BlockSpec((ts, f), lambda j, i: (i, j)),
        out_shape=jax.ShapeDtypeStruct((t, dff), BF16),
        scratch_shapes=[pltpu.VMEM((f // LANE, HALO16 + ts, LANE), F32)],
        compiler_params=_params("parallel", "parallel"),
    )(up, up, w, b)


def _ffn_bwd(name, up, dg, w, b, seq):
    _, t, dff = up.shape
    f = _ctile(dff, FFN_COLS)
    k_taps = w.shape[0]
    ts = _tile(seq, 256)
    tps = seq // ts
    hb = ts // HALO16
    nhb = t // HALO16
    ext = ts + HALO16
    rc = _tile(ts, FFN_ROWS)

    def body(cur_ref, prev_ref, next_ref, dg_ref, ndg_ref, w_ref, b_ref, dup_ref, dw_ref, db_ref, apad, dcpad):
        i = pl.program_id(1)
        first = (i % tps) == 0
        last = (i % tps) == tps - 1
        @pl.when(i == 0)
        def _():
            dw_ref[...] = jnp.zeros_like(dw_ref)
            db_ref[...] = jnp.zeros_like(db_ref)

        for ci, c0 in enumerate(range(0, f, LANE)):
            cols = slice(c0, c0 + LANE)
            apad[ci, 0:HALO16, :] = jnp.where(first, 0.0, prev_ref[:, cols].astype(F32))
            apad[ci, HALO16:HALO16 + ts, :] = cur_ref[0, :, cols].astype(F32)
            apad[ci, HALO16 + ts:HALO16 + ext, :] = next_ref[0, :, cols].astype(F32)
            wv = w_ref[:, cols]
            wk = [jnp.broadcast_to(wv[k:k + 1, :], (rc, LANE)) for k in range(k_taps)]
            bias = jnp.broadcast_to(b_ref[:, cols], (rc, LANE))

            def conv_grad(r0, n, gate, dgv):
                c = bias[0:n]
                for k in range(k_taps):
                    c = c + wk[k][0:n] * apad[ci, pl.ds(HALO16 - (k_taps - 1) + k + r0, n), :]
                sig = _sigmoid(c)
                return dgv * gate * sig * (1.0 + c * (1.0 - sig)), c * sig

            for r0 in range(0, ts, rc):
                dgv = dg_ref[pl.ds(r0, rc), cols].astype(F32)
                dc, silu = conv_grad(r0, rc, cur_ref[1, pl.ds(r0, rc), cols].astype(F32), dgv)
                dcpad[ci, pl.ds(r0, rc), :] = dc
                dup_ref[1, pl.ds(r0, rc), cols] = (dgv * silu).astype(BF16)
            dgv = jnp.where(last, 0.0, ndg_ref[:, cols].astype(F32))
            dc, _ = conv_grad(ts, HALO16, next_ref[1, :, cols].astype(F32), dgv)
            dcpad[ci, ts:ext, :] = dc

            dw_acc = [jnp.zeros((rc, LANE), F32) for _ in range(k_taps)]
            db_acc = jnp.zeros((rc, LANE), F32)
            for r0 in range(0, ts, rc):
                dact = jnp.zeros((rc, LANE), F32)
                for k in range(k_taps):
                    dact = dact + wk[k] * dcpad[ci, pl.ds(r0 + (k_taps - 1) - k, rc), :]
                dup_ref[0, pl.ds(r0, rc), cols] = dact.astype(BF16)
                dc = dcpad[ci, pl.ds(r0, rc), :]
                for k in range(k_taps):
                    dw_acc[k] = dw_acc[k] + dc * apad[ci, pl.ds(HALO16 - (k_taps - 1) + k + r0, rc), :]
                db_acc = db_acc + dc
            for k in range(k_taps):
                dw_ref[k:k + 1, cols] += jnp.sum(dw_acc[k], axis=0, keepdims=True)
            db_ref[:, cols] += jnp.sum(db_acc, axis=0, keepdims=True)

    return pl.pallas_call(
        body, name=name, grid=(dff // f, t // ts),
        in_specs=[pl.BlockSpec((2, ts, f), lambda j, i: (0, i, j)),
                  pl.BlockSpec((None, HALO16, f), lambda j, i: (0, jnp.maximum(i * hb - 1, 0), j)),
                  pl.BlockSpec((2, HALO16, f), lambda j, i: (0, jnp.minimum((i + 1) * hb, nhb - 1), j)),
                  pl.BlockSpec((ts, f), lambda j, i: (i, j)),
                  pl.BlockSpec((HALO16, f), lambda j, i: (jnp.minimum((i + 1) * hb, nhb - 1), j)),
                  pl.BlockSpec((k_taps, f), lambda j, i: (0, j)),
                  pl.BlockSpec((1, f), lambda j, i: (0, j))],
        out_specs=[pl.BlockSpec((2, ts, f), lambda j, i: (0, i, j)),
                   pl.BlockSpec((k_taps, f), lambda j, i: (0, j)),
                   pl.BlockSpec((1, f), lambda j, i: (0, j))],
        out_shape=[jax.ShapeDtypeStruct((2, t, dff), BF16), jax.ShapeDtypeStruct((k_taps, dff), F32),
                   jax.ShapeDtypeStruct((1, dff), F32)],
        scratch_shapes=[pltpu.VMEM((f // LANE, HALO16 + ext, LANE), F32), pltpu.VMEM((f // LANE, ext, LANE), F32)],
        compiler_params=_params("parallel", "arbitrary"),
    )(up, up, up, dg, dg, w, b)


def _sum_rows(name, g):
    ns, r, c = g.shape
    tr = _tile(r, 256)

    def body(g_ref, o_ref):
        acc = g_ref[0]
        for dev in range(1, ns):
            acc = acc + g_ref[dev]
        o_ref[...] = acc

    return pl.pallas_call(
        body, name=name, grid=(r // tr,),
        in_specs=[pl.BlockSpec((ns, tr, c), lambda i: (0, i, 0))],
        out_specs=pl.BlockSpec((tr, c), lambda i: (i, 0)),
        out_shape=jax.ShapeDtypeStruct((r, c), F32), compiler_params=_params("parallel"),
    )(g)


def _adamw(name, gsrc, w, m, v, layer=0, prev=None):
    ns, r, c = gsrc.shape
    nl = w.shape[0]
    tr = _tile(r, 256)
    prev = () if prev is None else tuple(prev)

    def body(g_ref, w_ref, m_ref, v_ref, *rest):
        go_ref, do_ref, mo_ref, vo_ref = rest[len(prev):]
        g = g_ref[0].astype(F32)
        for dev in range(1, ns):
            g = g + g_ref[dev].astype(F32)
        m_new = ADAM_B1 * m_ref[...] + (1.0 - ADAM_B1) * g
        v_new = ADAM_B2 * v_ref[...] + (1.0 - ADAM_B2) * (g * g)
        m_hat = m_new / (1.0 - ADAM_B1 ** ADAM_STEP)
        v_hat = v_new / (1.0 - ADAM_B2 ** ADAM_STEP)
        go_ref[...] = g
        do_ref[...] = -ADAM_LR * (m_hat / (jnp.sqrt(v_hat) + ADAM_EPS) + ADAM_WD * w_ref[...])
        mo_ref[...] = m_new
        vo_ref[...] = v_new

    row = pl.BlockSpec((None, tr, c), lambda i: (layer, i, 0))
    return pl.pallas_call(
        body, name=name, grid=(r // tr,),
        in_specs=[pl.BlockSpec((ns, tr, c), lambda i: (0, i, 0)), row, row, row] + [ANY] * len(prev),
        out_specs=[row] * 4, out_shape=[jax.ShapeDtypeStruct((nl, r, c), F32)] * 4,
        input_output_aliases={4 + i: i for i in range(len(prev))},
        compiler_params=_params("parallel"),
    )(gsrc, w, m, v, *prev)


def _ffn_forward(tag, r_in, gain, get_wu, get_wd, wdw, bdw, seq):
    t, d = r_in.shape
    tm = _tile(t, 512)
    h = _rms_fwd(f"{tag}_rms", r_in, gain)
    wu = get_wu(h)
    dff = wu.shape[1] // 2
    up = _mm(f"{tag}_up", h, wu, grid=(2, t // tm, 1),
             a_spec=pl.BlockSpec((tm, d), lambda j, i, k: (i, 0)),
             b_spec=pl.BlockSpec((d, dff), lambda j, i, k: (0, j)),
             out_spec=pl.BlockSpec((None, tm, dff), lambda j, i, k: (j, i, 0)),
             out_shape=jax.ShapeDtypeStruct((2, t, dff), BF16), dims=NN, acc_shape=(tm, dff))
    wd = get_wd(up)
    g = _ffn_fwd(f"{tag}_act", up, wdw, bdw, seq)
    r_out = _mm(f"{tag}_down", g, wd, grid=(t // tm, 1, 1),
                a_spec=pl.BlockSpec((tm, dff), lambda i, j, k: (i, 0)),
                b_spec=pl.BlockSpec((dff, d), lambda i, j, k: (0, 0)),
                out_spec=pl.BlockSpec((tm, d), lambda i, j, k: (i, 0)),
                out_shape=jax.ShapeDtypeStruct((t, d), F32), dims=NN, acc_shape=(tm, d),
                extras=(r_in,), extra_specs=(pl.BlockSpec((tm, d), lambda i, j, k: (i, 0)),),
                epilogue=lambda acc, res: res + acc)
    return r_out, (r_in, h, up, g, wu, wd)


def _ffn_backward(tag, dr, saved, gain, wdw, bdw, seq, token=None):
    r_in, h, up, g, wu, wd = saved
    t, d = r_in.shape
    dff = wd.shape[0]
    tm = _tile(t, 512)
    tk = _tile(t, 1024)
    cw = _ctile(dff, 1408)
    rm = _ctile(d, 512)
    dg = _mm(f"{tag}_dg", dr, wd, grid=(t // tm, 1, 1),
             a_spec=pl.BlockSpec((tm, d), lambda i, j, k: (i, 0)),
             b_spec=pl.BlockSpec((dff, d), lambda i, j, k: (0, 0)),
             out_spec=pl.BlockSpec((tm, dff), lambda i, j, k: (i, 0)),
             out_shape=jax.ShapeDtypeStruct((t, dff), BF16), dims=NT, acc_shape=(tm, dff), token=token)
    dwd = _mm(f"{tag}_dwd", g, dr, grid=(dff // cw, 1, t // tk),
              a_spec=pl.BlockSpec((tk, cw), lambda i, j, k: (k, i)),
              b_spec=pl.BlockSpec((tk, d), lambda i, j, k: (k, 0)),
              out_spec=pl.BlockSpec((cw, d), lambda i, j, k: (i, 0)),
              out_shape=jax.ShapeDtypeStruct((dff, d), BF16), dims=TN, acc_shape=(cw, d))
    dup, dwdw, dbdw = _ffn_bwd(f"{tag}_dact", up, dg, wdw, bdw, seq)
    dh = _mm(f"{tag}_dh", dup, wu, grid=(t // tm, 1, 2),
             a_spec=pl.BlockSpec((None, tm, dff), lambda i, j, k: (k, i, 0)),
             b_spec=pl.BlockSpec((d, dff), lambda i, j, k: (0, k)),
             out_spec=pl.BlockSpec((tm, d), lambda i, j, k: (i, 0)),
             out_shape=jax.ShapeDtypeStruct((t, d), F32), dims=NT, acc_shape=(tm, d))
    dwu = _mm(f"{tag}_dwu", h, dup, grid=(d // rm, 2, t // tk),
              a_spec=pl.BlockSpec((tk, rm), lambda i, j, k: (k, i)),
              b_spec=pl.BlockSpec((None, tk, dff), lambda i, j, k: (j, k, 0)),
              out_spec=pl.BlockSpec((rm, dff), lambda i, j, k: (i, j)),
              out_shape=jax.ShapeDtypeStruct((d, 2 * dff), BF16), dims=TN, acc_shape=(rm, dff))
    dr_in, dgain, colsum = _rms_bwd(f"{tag}_drms", dh, r_in, gain, dr)
    return dr_in, dgain, dwu, dwd, dwdw, dbdw, colsum


def _pad_to(vec, n):
    return jnp.pad(vec, (0, n - vec.shape[0]))


def _pack(parts, width):
    flat = jnp.concatenate([p.reshape(-1).astype(F32) for p in parts])
    n = -(-flat.shape[0] // (8 * width)) * (8 * width)
    return _pad_to(flat, n).reshape(n // width, width)


def _unpack(mat, shapes):
    flat = mat.reshape(-1)
    out, off = [], 0
    for s in shapes:
        n = 1
        for dim in s:
            n *= dim
        out.append(flat[off:off + n].reshape(s))
        off += n
    return out


def kernel(x, norm_mix, norm_ffn, conv_w_pw1, conv_b_pw1, conv_w_dw, conv_b_dw, conv_ln_g, conv_ln_b, conv_w_pw2, conv_b_pw2, pool_w, pool_b, pool_scale, ffn_w_up, ffn_w_dw, ffn_b_dw, ffn_w_down, final_norm, loss_target, m_norm_mix, m_norm_ffn, m_conv_w_pw1, m_conv_b_pw1, m_conv_w_dw, m_conv_b_dw, m_conv_ln_g, m_conv_ln_b, m_conv_w_pw2, m_conv_b_pw2, m_pool_w, m_pool_b, m_pool_scale, m_ffn_w_up, m_ffn_w_dw, m_ffn_b_dw, m_ffn_w_down, m_final_norm, v_norm_mix, v_norm_ffn, v_conv_w_pw1, v_conv_b_pw1, v_conv_w_dw, v_conv_b_dw, v_conv_ln_g, v_conv_ln_b, v_conv_w_pw2, v_conv_b_pw2, v_pool_w, v_pool_b, v_pool_scale, v_ffn_w_up, v_ffn_w_dw, v_ffn_b_dw, v_ffn_w_down, v_final_norm):
    bsz, seq, d = x.shape
    t = bsz * seq
    k_taps = conv_w_dw.shape[1]
    cs1 = conv_w_pw1.shape[2]
    dsh = d // N_DEV
    cg = d // N_GROUPS
    cgs = pool_w.shape[2]
    fu = ffn_w_up.shape[2]
    fd = ffn_w_down.shape[1]
    dff = fd * N_DEV
    nb = N_DEV // 2
    kf = ffn_w_dw.shape[1]
    fsh = ffn_w_dw.shape[2]
    my = _lin(_me())
    tm = _tile(t, 512)

    x2 = x.reshape(t, d)
    tgt2 = loss_target.reshape(t, d)

    small_shapes = [(k_taps, dsh), (dsh,), (dsh,), (2, kf, fsh)]
    small_mine = _pack([conv_w_dw[0], pool_b[0], pool_scale[0], ffn_w_dw], LANE)
    big = [conv_w_pw1[0], conv_w_pw2[0], ffn_w_up[0], ffn_w_down[0], pool_w[0], ffn_w_up[1], ffn_w_down[1]]
    gather = _exchange_start("gather_start", [small_mine] + [w.astype(BF16) for w in big], [False] * 8)
    h0 = _rms_fwd("l0_rms", x2, norm_mix[0:1])
    small_all, w1 = _exchange_wait("gather_wait_w1", gather, [0, 1], h0)
    parts = [_unpack(small_all[dev], small_shapes) for dev in range(N_DEV)]
    wdw = jnp.concatenate([p[0] for p in parts], axis=1)
    pool_b_full = jnp.concatenate([p[1] for p in parts]).reshape(1, d)
    pool_s_full = jnp.concatenate([p[2] for p in parts]).reshape(1, d)
    fwdw = jnp.concatenate([p[3] for p in parts], axis=2)
    fbdw = ffn_b_dw.reshape(2, 1, dff)

    def columns(w):
        return w.transpose(1, 0, 2).reshape(w.shape[1], N_DEV * w.shape[2])

    def column_shards(w):
        return w.reshape(w.shape[0], N_DEV, w.shape[1] // N_DEV).transpose(1, 0, 2)

    w1 = columns(w1)
    a = _mm("l0_pw1", h0, w1, grid=(2, t // tm, 1),
            a_spec=pl.BlockSpec((tm, d), lambda j, i, k: (i, 0)),
            b_spec=pl.BlockSpec((d, d), lambda j, i, k: (0, j)),
            out_spec=pl.BlockSpec((None, tm, d), lambda j, i, k: (j, i, 0)),
            out_shape=jax.ShapeDtypeStruct((2, t, d), BF16), dims=NN, acc_shape=(tm, d),
            extras=(conv_b_pw1,), extra_specs=(pl.BlockSpec((1, d), lambda j, i, k: (0, j)),),
            epilogue=lambda acc, b: acc + b)
    (w2,) = _exchange_wait("gather_wait_w2", gather, [2], a)
    w2 = w2.reshape(d, d)
    v = _conv_fwd("l0_conv", a, wdw, conv_b_dw, seq)
    s = _ln_fwd("l0_ln", v, conv_ln_g, conv_ln_b)
    r1 = _mm("l0_pw2", s, w2, grid=(t // tm, 1, 1),
             a_spec=pl.BlockSpec((tm, d), lambda i, j, k: (i, 0)),
             b_spec=pl.BlockSpec((d, d), lambda i, j, k: (0, 0)),
             out_spec=pl.BlockSpec((tm, d), lambda i, j, k: (i, 0)),
             out_shape=jax.ShapeDtypeStruct((t, d), F32), dims=NN, acc_shape=(tm, d),
             extras=(conv_b_pw2, x2),
             extra_specs=(pl.BlockSpec((1, d), lambda i, j, k: (0, 0)), pl.BlockSpec((tm, d), lambda i, j, k: (i, 0))),
             epilogue=lambda acc, b, res: res + (acc + b))
    def up_getter(name, idx):
        return lambda after: columns(_exchange_wait(name, gather, [idx], after)[0])

    def down_getter(name, idx):
        return lambda after: _exchange_wait(name, gather, [idx], after)[0].reshape(dff, d)

    r2, ffn0_saved = _ffn_forward("f0", r1, norm_ffn[0:1], up_getter("gather_wait_wu0", 3),
                                  down_getter("gather_wait_wd0", 4), fwdw[0], fbdw[0], seq)
    (wp,) = _exchange_wait("gather_wait_wp", gather, [5], r2)
    wp = wp.transpose(1, 0, 2, 3).reshape(N_GROUPS, cg, cg)
    pooled = _pool_fwd("l1_pool", r2, norm_mix[1:2], seq)
    r3 = _mm("l1_mix", pooled, wp, grid=(t // tm, N_GROUPS, 1),
             a_spec=pl.BlockSpec((tm, cg), lambda i, j, k: (i, j)),
             b_spec=pl.BlockSpec((None, cg, cg), lambda i, j, k: (j, 0, 0)),
             out_spec=pl.BlockSpec((tm, cg), lambda i, j, k: (i, j)),
             out_shape=jax.ShapeDtypeStruct((t, d), F32), dims=NN, acc_shape=(tm, cg),
             extras=(pool_s_full, pool_b_full, r2),
             extra_specs=(pl.BlockSpec((1, cg), lambda i, j, k: (0, j)), pl.BlockSpec((1, cg), lambda i, j, k: (0, j)),
                          pl.BlockSpec((tm, cg), lambda i, j, k: (i, j))),
             epilogue=lambda acc, sc, b, res: res + sc * (acc + b))
    r4, ffn1_saved = _ffn_forward("f1", r3, norm_ffn[1:2], up_getter("gather_wait_wu1", 6),
                                  down_getter("gather_wait_wd1", 7), fwdw[1], fbdw[1], seq)

    dr4, loss_part, dfinal, _ = _final("final", r4, tgt2, final_norm.reshape(1, d))
    dr3, dnf1, dwu1, dwd1, dfw1, dfb1, _ = _ffn_backward("f1", dr4, ffn1_saved, norm_ffn[1:2], fwdw[1], fbdw[1], seq)
    scatter_a = _exchange_start("scatter_f1_start", [column_shards(dwu1), dwd1.reshape(N_DEV, fd, d)], [True, True])
    dpooled, dmixed, dpool_s, dpool_b = _pool_bwd_mm("l1_dmix", pooled, wp, dr3,
                                                     pool_s_full + scatter_a["token"][0:1, 0:1], pool_b_full)
    dwp = _mm("l1_dwp", pooled, dmixed, grid=(N_GROUPS, 1, t // tm),
              a_spec=pl.BlockSpec((tm, cg), lambda i, j, k: (k, i)),
              b_spec=pl.BlockSpec((tm, cg), lambda i, j, k: (k, i)),
              out_spec=pl.BlockSpec((None, cg, cg), lambda i, j, k: (i, 0, 0)),
              out_shape=jax.ShapeDtypeStruct((N_GROUPS, cg, cg), BF16), dims=TN, acc_shape=(cg, cg))
    dr2, dnm1 = _pool_bwd("l1_dpool", dpooled, r2, norm_mix[1:2], dr3, seq)
    dr1, dnf0, dwu0, dwd0, dfw0, dfb0, db2 = _ffn_backward("f0", dr2, ffn0_saved, norm_ffn[0:1], fwdw[0], fbdw[0], seq)
    dwp_b = dwp.reshape(N_GROUPS, N_DEV, cgs, cg).transpose(1, 0, 2, 3)
    scatter_b = _exchange_start("scatter_f0_start", [column_shards(dwu0), dwd0.reshape(N_DEV, fd, d), dwp_b],
                                [True] * 3)
    ds = _mm("l0_ds", dr1, w2, grid=(t // tm, 1, 1),
             a_spec=pl.BlockSpec((tm, d), lambda i, j, k: (i, 0)),
             b_spec=pl.BlockSpec((d, d), lambda i, j, k: (0, 0)),
             out_spec=pl.BlockSpec((tm, d), lambda i, j, k: (i, 0)),
             out_shape=jax.ShapeDtypeStruct((t, d), F32), dims=NT, acc_shape=(tm, d), token=scatter_b["token"])
    dw2 = _mm("l0_dw2", s, dr1, grid=(1, 1, t // tm),
              a_spec=pl.BlockSpec((tm, d), lambda i, j, k: (k, 0)),
              b_spec=pl.BlockSpec((tm, d), lambda i, j, k: (k, 0)),
              out_spec=pl.BlockSpec((d, d), lambda i, j, k: (0, 0)),
              out_shape=jax.ShapeDtypeStruct((d, d), BF16), dims=TN, acc_shape=(d, d))
    dv, dlg, dlb, dbdw = _ln_bwd("l0_dln", ds, v, conv_ln_g, conv_ln_b)
    da, dwdw, db1 = _conv_bwd("l0_dconv", a, dv, wdw, seq)
    tk = _tile(t, 1024)
    dw1 = _mm("l0_dw1", h0, da, grid=(1, 2, t // tk),
              a_spec=pl.BlockSpec((tk, d), lambda i, j, k: (k, 0)),
              b_spec=pl.BlockSpec((None, tk, d), lambda i, j, k: (j, k, 0)),
              out_spec=pl.BlockSpec((d, d), lambda i, j, k: (0, j)),
              out_shape=jax.ShapeDtypeStruct((d, 2 * d), BF16), dims=TN, acc_shape=(d, d))
    scatter_c = _exchange_start("scatter_l0_start", [column_shards(dw1), dw2.reshape(N_DEV, d // N_DEV, d)],
                                [True, True])
    dh0 = _mm("l0_dh", da, w1, grid=(t // tm, 1, 2),
              a_spec=pl.BlockSpec((None, tm, d), lambda i, j, k: (k, i, 0)),
              b_spec=pl.BlockSpec((d, d), lambda i, j, k: (0, k)),
              out_spec=pl.BlockSpec((tm, d), lambda i, j, k: (i, 0)),
              out_shape=jax.ShapeDtypeStruct((t, d), F32), dims=NT, acc_shape=(tm, d), token=scatter_c["token"])
    dx, dnm0, _ = _rms_bwd("l0_drms", dh0, x2, norm_mix[0:1], dr1)

    dffn_w = jnp.stack([dfw0, dfw1])
    dffn_b = jnp.stack([dfb0, dfb1]).reshape(2, dff)
    small_parts = [loss_part, jnp.concatenate([dnm0, dnm1]), jnp.concatenate([dnf0, dnf1]), db1, dwdw, dbdw, dlg, dlb,
                   db2, dpool_b, dpool_s, dffn_w, dffn_b, dfinal]
    small_part_shapes = [(1,), (2, d), (2, d), (1, 2 * d), (k_taps, d), (1, d), (1, d), (1, d), (1, d), (1, d), (1, d),
                         (2, kf, dff), (2, dff), (d,)]
    packed = _pack(small_parts, 8 * LANE)
    gather_small = _exchange_start("gather_small_start", [packed], [False])

    def big_update(name, recv, w, m, v, layer=0, prev=None):
        shape = w.shape
        c = recv.shape[-1]
        rows = recv.size // (N_DEV * c)
        nl = w.size // (rows * c)
        outs = _adamw(name, recv.reshape(N_DEV, rows, c), w.reshape(nl, rows, c), m.reshape(nl, rows, c),
                      v.reshape(nl, rows, c), layer, prev)
        return outs, [o.reshape(shape) for o in outs]

    g_wu1, g_wd1 = _exchange_wait("scatter_f1_wait", scatter_a, [0, 1], gather_small["token"])
    raw_wu, _ = big_update("adam_wu1", g_wu1, ffn_w_up, m_ffn_w_up, v_ffn_w_up, 1)
    raw_wd, _ = big_update("adam_wd1", g_wd1, ffn_w_down, m_ffn_w_down, v_ffn_w_down, 1)
    g_wu0, g_wd0, g_wp = _exchange_wait("scatter_f0_wait", scatter_b, [0, 1, 2], raw_wd[0])
    _, u_wu = big_update("adam_wu0", g_wu0, ffn_w_up, m_ffn_w_up, v_ffn_w_up, 0, raw_wu)
    _, u_wd = big_update("adam_wd0", g_wd0, ffn_w_down, m_ffn_w_down, v_ffn_w_down, 0, raw_wd)
    _, u_wp = big_update("adam_wp", g_wp, pool_w, m_pool_w, v_pool_w)
    g_w1, g_w2 = _exchange_wait("scatter_l0_wait", scatter_c, [0, 1], u_wp[0])
    _, u_w1 = big_update("adam_w1", g_w1, conv_w_pw1, m_conv_w_pw1, v_conv_w_pw1)
    _, u_w2 = big_update("adam_w2", g_w2, conv_w_pw2, m_conv_w_pw2, v_conv_w_pw2)
    (all_small,) = _exchange_wait("gather_small_wait", gather_small, [0], u_w2[0])
    summed = _sum_rows("sum_small_grads", all_small)
    (loss_v, g_nm, g_nf, g_b1, g_wdw, g_bdw, g_lg, g_lb, g_b2, g_pb, g_ps, g_fw, g_fb,
     g_fin) = _unpack(summed, small_part_shapes)
    loss = loss_v[0]
    g_wdw_mine = lax.dynamic_slice_in_dim(g_wdw, my * dsh, dsh, axis=1)[None]
    g_pb_mine = lax.dynamic_slice_in_dim(g_pb, my * dsh, dsh, axis=1)
    g_ps_mine = lax.dynamic_slice_in_dim(g_ps, my * dsh, dsh, axis=1)
    g_fw_mine = lax.dynamic_slice_in_dim(g_fw, my * fsh, fsh, axis=2)

    small_g =[g_nm, g_nf, g_b1, g_wdw_mine, g_bdw, g_lg, g_lb, g_b2, g_pb_mine, g_ps_mine, g_fw_mine, g_fb, g_fin]
    small_w = [norm_mix, norm_ffn, conv_b_pw1, conv_w_dw, conv_b_dw, conv_ln_g, conv_ln_b, conv_b_pw2, pool_b,
               pool_scale, ffn_w_dw, ffn_b_dw, final_norm]
    small_m = [m_norm_mix, m_norm_ffn, m_conv_b_pw1, m_conv_w_dw, m_conv_b_dw, m_conv_ln_g, m_conv_ln_b,
               m_conv_b_pw2, m_pool_b, m_pool_scale, m_ffn_w_dw, m_ffn_b_dw, m_final_norm]
    small_v = [v_norm_mix, v_norm_ffn, v_conv_b_pw1, v_conv_w_dw, v_conv_b_dw, v_conv_ln_g, v_conv_ln_b,
               v_conv_b_pw2, v_pool_b, v_pool_scale, v_ffn_w_dw, v_ffn_b_dw, v_final_norm]
    shapes = [w.shape for w in small_w]
    outs = _adamw("adam_small", _pack(small_g, 8 * LANE)[None], _pack(small_w, 8 * LANE)[None],
                  _pack(small_m, 8 * LANE)[None], _pack(small_v, 8 * LANE)[None])
    sg, sd, sm, sv = [_unpack(o, shapes) for o in outs]

    def leaf(kind):
        (nm, nf, b1, wdw_, bdw_, lg, lb, b2, pb, ps, fw, fb, fin) = (sg, sd, sm, sv)[kind]
        return [nm, nf, u_w1[kind], b1, wdw_, bdw_, lg, lb, u_w2[kind], b2, u_wp[kind], pb, ps, u_wu[kind], fw, fb,
                u_wd[kind], fin]

    return (loss, dx.reshape(bsz, seq, d), *leaf(0), *leaf(1), *leaf(2), *leaf(3))
```

```python
import functools

import jax
import jax.numpy as jnp
from jax import lax
from jax.experimental import pallas as pl
from jax.experimental.pallas import tpu as pltpu

F32 = jnp.float32
BF16 = jnp.bfloat16
MESH = pl.DeviceIdType.MESH
HBM = pl.BlockSpec(memory_space=pltpu.HBM)

N_DEV = 8
RMS_EPS = 1e-6
LN_EPS = 1e-5
POOL_WINDOWS = (2, 4, 8, 16)
N_GROUPS = len(POOL_WINDOWS)
ADAM_LR = 0.001
ADAM_B1 = 0.9
ADAM_B2 = 0.999
ADAM_EPS = 1e-08
ADAM_WD = 0.01
ADAM_STEP = 10

LANE = 128
HALO = 32
HALO16 = 16
VMEM_LIMIT = 56 * 1024 * 1024


def _params(*sem):
    return pltpu.CompilerParams(dimension_semantics=sem if sem else None, vmem_limit_bytes=VMEM_LIMIT)


def _tile(n, pref):
    for t in range(min(pref, n), 15, -1):
        if n % t == 0 and t % 16 == 0:
            return t
    return n


def _sigmoid(z):
    return 1.0 / (1.0 + jnp.exp(-z))


def _me():
    return lax.axis_index("x"), lax.axis_index("y"), lax.axis_index("c")


def _flip(pos, m):
    x, y, c = pos
    return ((1 - x) if m & 4 else x, (1 - y) if m & 2 else y, (1 - c) if m & 1 else c)


def _lin(pos):
    return 4 * pos[0] + 2 * pos[1] + pos[2]


SEM = pl.BlockSpec(memory_space=pltpu.SEMAPHORE)
ANY = pl.BlockSpec(memory_space=pl.ANY)
EFFECT = pltpu.SideEffectType.DATAFLOW_SIDE_EFFECTING


def _exchange_copies(srcs, lands, send_sems, recv_sems, modes, which, starting):
    me = _me()
    my = _lin(me)
    out = []
    for pos, a in enumerate(which):
        src, land = srcs[pos], lands[pos]

        def block(pid, src=src, a=a):
            return src.at[pid] if modes[a] else src

        local = pltpu.make_async_copy(block(my), land.at[my], send_sems.at[a * N_DEV])
        remote = []
        for m in range(1, N_DEV):
            peer = _flip(me, m)
            pid = _lin(peer)
            sems = dict(send_sem=send_sems.at[a * N_DEV + m], recv_sem=recv_sems.at[a * N_DEV + m],
                        device_id=peer, device_id_type=MESH)
            if starting:
                remote.append(pltpu.make_async_remote_copy(src_ref=block(pid), dst_ref=land.at[my], **sems))
            else:
                remote.append((pltpu.make_async_remote_copy(src_ref=block(pid), dst_ref=land.at[my], **sems),
                               pltpu.make_async_remote_copy(src_ref=block(pid), dst_ref=land.at[pid], **sems)))
        out.append((local, remote))
    return out


def _exchange_start(name, arrs, modes):
    n = len(arrs)
    blocks = [a.shape[1:] if md else a.shape for a, md in zip(arrs, modes)]

    def body(*refs):
        srcs, lands = refs[:n], refs[n:2 * n]
        send_sems, recv_sems = refs[2 * n], refs[2 * n + 1]
        token = refs[-1]
        for local, remote in _exchange_copies(srcs, lands, send_sems, recv_sems, modes, list(range(n)), True):
            local.start()
            for send in remote:
                send.start()
        token[...] = jnp.zeros_like(token)

    lands = [lax.empty((N_DEV,) + tuple(b), a.dtype) for a, b in zip(arrs, blocks)]
    outs = pl.pallas_call(
        body, name=name,
        out_shape=(pltpu.SemaphoreType.DMA((n * N_DEV,)), pltpu.SemaphoreType.DMA((n * N_DEV,)),
                   *[pltpu.HBM(a.shape, a.dtype) for a in arrs], *[pltpu.HBM(l.shape, l.dtype) for l in lands],
                   jax.ShapeDtypeStruct((8, LANE), F32)),
        in_specs=[HBM] * (2 * n),
        out_specs=(SEM, SEM, *[HBM] * (2 * n), pl.BlockSpec(memory_space=pltpu.VMEM)),
        input_output_aliases={i: 2 + i for i in range(2 * n)},
        compiler_params=pltpu.CompilerParams(has_side_effects=EFFECT),
    )(*[pltpu.with_memory_space_constraint(a, pltpu.HBM) for a in arrs],
      *[pltpu.with_memory_space_constraint(l, pltpu.HBM) for l in lands])
    return dict(send=outs[0], recv=outs[1], srcs=list(outs[2:2 + n]), lands=list(outs[2 + n:2 + 2 * n]),
                modes=modes, token=outs[-1])


def _exchange_wait(name, handle, which, after):
    k = len(which)
    modes = handle["modes"]

    def body(*refs):
        srcs, lands = refs[:k], refs[k:2 * k]
        send_sems, recv_sems = refs[2 * k], refs[2 * k + 1]
        for local, remote in _exchange_copies(srcs, lands, send_sems, recv_sems, modes, which, False):
            local.wait()
            for send, arrival in remote:
                send.wait_send()
                arrival.wait_recv()

    srcs = [handle["srcs"][a] for a in which]
    lands = [handle["lands"][a] for a in which]
    outs = pl.pallas_call(
        body, name=name,
        out_shape=tuple(pltpu.HBM(x.shape, x.dtype) for x in srcs + lands),
        in_specs=[HBM] * (2 * k) + [SEM, SEM, ANY], out_specs=tuple([HBM] * (2 * k)),
        input_output_aliases={i: i for i in range(2 * k)},
        compiler_params=pltpu.CompilerParams(has_side_effects=EFFECT),
    )(*srcs, *lands, handle["send"], handle["recv"], after)
    return list(outs[k:])


def _mm(name, a, b, *, grid, a_spec, b_spec, out_spec, out_shape, dims, acc_shape, extras=(), extra_specs=(),
        epilogue=None, token=None):
    nk = grid[2]
    ne = len(extras)
    deps = () if token is None else (token,)
    dep_specs = [pl.BlockSpec((8, LANE), lambda i, j, k: (0, 0))] * len(deps)
    n_out = len(out_shape) if isinstance(out_shape, (list, tuple)) else 1

    def body(a_ref, b_ref, *rest):
        ex, o_refs, acc_ref = rest[:ne], rest[ne + len(deps):ne + len(deps) + n_out], rest[ne + len(deps) + n_out]
        k = pl.program_id(2)
        part = lax.dot_general(a_ref[...].astype(BF16), b_ref[...].astype(BF16), (dims, ((), ())),
                               preferred_element_type=F32)

        def finish(r):
            if epilogue is not None:
                r = epilogue(r, *[e[...] for e in ex])
            for o_ref, val in zip(o_refs, r if isinstance(r, tuple) else (r,)):
                o_ref[...] = val.astype(o_ref.dtype)

        if nk == 1:
            finish(part)
            return

        @pl.when(k == 0)
        def _():
            acc_ref[...] = part

        @pl.when((k > 0) & (k < nk - 1))
        def _():
            acc_ref[...] += part

        @pl.when(k == nk - 1)
        def _():
            finish(acc_ref[...] + part)

    return pl.pallas_call(
        body, name=name, grid=grid, in_specs=[a_spec, b_spec, *extra_specs, *dep_specs], out_specs=out_spec,
        out_shape=out_shape, scratch_shapes=[pltpu.VMEM(acc_shape if nk > 1 else (8, LANE), F32)],
        compiler_params=_params("parallel", "parallel", "arbitrary"),
    )(a, b, *extras, *deps)


NN = ((1,), (0,))
NT = ((1,), (1,))
TN = ((0,), (0,))


def _rms_fwd(name, x, gain):
    t, d = x.shape
    tr = _tile(t, 512)

    def body(x_ref, g_ref, h_ref):
        xv = x_ref[...]
        rstd = lax.rsqrt(jnp.mean(xv * xv, axis=-1, keepdims=True) + RMS_EPS)
        h_ref[...] = (xv * rstd * g_ref[...]).astype(BF16)

    return pl.pallas_call(
        body, name=name, grid=(t // tr,),
        in_specs=[pl.BlockSpec((tr, d), lambda i: (i, 0)), pl.BlockSpec((1, d), lambda i: (0, 0))],
        out_specs=pl.BlockSpec((tr, d), lambda i: (i, 0)),
        out_shape=jax.ShapeDtypeStruct((t, d), BF16), compiler_params=_params("parallel"),
    )(x, gain)


def _rms_bwd(name, dh, x, gain, dres):
    t, d = x.shape
    tr = _tile(t, 512)

    def body(dh_ref, x_ref, g_ref, dres_ref, dx_ref, dg_ref, cs_ref):
        i = pl.program_id(0)
        xv = x_ref[...]
        rstd = lax.rsqrt(jnp.mean(xv * xv, axis=-1, keepdims=True) + RMS_EPS)
        xhat = xv * rstd
        dhv = dh_ref[...]
        dxhat = dhv * g_ref[...]
        dx = dres_ref[...] + rstd * (dxhat - xhat * jnp.mean(dxhat * xhat, axis=-1, keepdims=True))
        dx_ref[...] = dx

        @pl.when(i == 0)
        def _():
            dg_ref[...] = jnp.zeros_like(dg_ref)
            cs_ref[...] = jnp.zeros_like(cs_ref)

        dg_ref[...] += jnp.sum(dhv * xhat, axis=0, keepdims=True)
        cs_ref[...] += jnp.sum(dx, axis=0, keepdims=True)

    row = pl.BlockSpec((tr, d), lambda i: (i, 0))
    vec = pl.BlockSpec((1, d), lambda i: (0, 0))
    return pl.pallas_call(
        body, name=name, grid=(t // tr,), in_specs=[row, row, vec, row], out_specs=[row, vec, vec],
        out_shape=[jax.ShapeDtypeStruct((t, d), F32), jax.ShapeDtypeStruct((1, d), F32),
                   jax.ShapeDtypeStruct((1, d), F32)],
        compiler_params=_params("arbitrary"),
    )(dh, x, gain, dres)


def _final(name, x, tgt, gain):
    t, d = x.shape
    tr = _tile(t, 512)

    def body(x_ref, t_ref, g_ref, dx_ref, loss_ref, dg_ref, cs_ref):
        i = pl.program_id(0)
        xv = x_ref[...]
        g = g_ref[...]
        rstd = lax.rsqrt(jnp.mean(xv * xv, axis=-1, keepdims=True) + RMS_EPS)
        xhat = xv * rstd
        err = xhat * g - t_ref[...]
        dy = err / d
        dxhat = dy * g
        dx = rstd * (dxhat - xhat * jnp.mean(dxhat * xhat, axis=-1, keepdims=True))
        dx_ref[...] = dx

        @pl.when(i == 0)
        def _():
            loss_ref[...] = jnp.zeros_like(loss_ref)
            dg_ref[...] = jnp.zeros_like(dg_ref)
            cs_ref[...] = jnp.zeros_like(cs_ref)

        loss_ref[...] += 0.5 * jnp.sum(jnp.mean(err * err, axis=-1, keepdims=True), axis=0, keepdims=True)
        dg_ref[...] += jnp.sum(dy * xhat, axis=0, keepdims=True)
        cs_ref[...] += jnp.sum(dx, axis=0, keepdims=True)

    row = pl.BlockSpec((tr, d), lambda i: (i, 0))
    vec = pl.BlockSpec((1, d), lambda i: (0, 0))
    one = pl.BlockSpec((1, 1), lambda i: (0, 0))
    return pl.pallas_call(
        body, name=name, grid=(t // tr,), in_specs=[row, row, vec], out_specs=[row, one, vec, vec],
        out_shape=[jax.ShapeDtypeStruct((t, d), F32), jax.ShapeDtypeStruct((1, 1), F32),
                   jax.ShapeDtypeStruct((1, d), F32), jax.ShapeDtypeStruct((1, d), F32)],
        compiler_params=_params("arbitrary"),
    )(x, tgt, gain)


def _ln_fwd(name, v, g, b):
    t, d = v.shape
    tr = _tile(t, 512)

    def body(v_ref, g_ref, b_ref, s_ref):
        vv = v_ref[...]
        mu = jnp.mean(vv, axis=-1, keepdims=True)
        cen = vv - mu
        rstd = lax.rsqrt(jnp.mean(cen * cen, axis=-1, keepdims=True) + LN_EPS)
        z = cen * rstd * g_ref[...] + b_ref[...]
        s_ref[...] = (z * _sigmoid(z)).astype(BF16)

    row = pl.BlockSpec((tr, d), lambda i: (i, 0))
    vec = pl.BlockSpec((1, d), lambda i: (0, 0))
    return pl.pallas_call(
        body, name=name, grid=(t // tr,), in_specs=[row, vec, vec], out_specs=row,
        out_shape=jax.ShapeDtypeStruct((t, d), BF16), compiler_params=_params("parallel"),
    )(v, g, b)


def _ln_bwd(name, ds, v, g, b):
    t, d = v.shape
    tr = _tile(t, 512)

    def body(ds_ref, v_ref, g_ref, b_ref, dv_ref, dg_ref, db_ref, cs_ref):
        i = pl.program_id(0)
        vv = v_ref[...]
        g = g_ref[...]
        mu = jnp.mean(vv, axis=-1, keepdims=True)
        cen = vv - mu
        rstd = lax.rsqrt(jnp.mean(cen * cen, axis=-1, keepdims=True) + LN_EPS)
        y = cen * rstd
        z = y * g + b_ref[...]
        sig = _sigmoid(z)
        dz = ds_ref[...] * sig * (1.0 + z * (1.0 - sig))
        dy = dz * g
        dv = rstd * (dy - jnp.mean(dy, axis=-1, keepdims=True) - y * jnp.mean(dy * y, axis=-1, keepdims=True))
        dv_ref[...] = dv

        @pl.when(i == 0)
        def _():
            dg_ref[...] = jnp.zeros_like(dg_ref)
            db_ref[...] = jnp.zeros_like(db_ref)
            cs_ref[...] = jnp.zeros_like(cs_ref)

        dg_ref[...] += jnp.sum(dz * y, axis=0, keepdims=True)
        db_ref[...] += jnp.sum(dz, axis=0, keepdims=True)
        cs_ref[...] += jnp.sum(dv, axis=0, keepdims=True)

    row = pl.BlockSpec((tr, d), lambda i: (i, 0))
    vec = pl.BlockSpec((1, d), lambda i: (0, 0))
    return pl.pallas_call(
        body, name=name, grid=(t // tr,), in_specs=[row, row, vec, vec], out_specs=[row, vec, vec, vec],
        out_shape=[jax.ShapeDtypeStruct((t, d), F32)] + [jax.ShapeDtypeStruct((1, d), F32)] * 3,
        compiler_params=_params("arbitrary"),
    )(ds, v, g, b)


def _conv_tiles(t, seq):
    ts = _tile(seq, 512)
    return ts, seq // ts, _tile(ts, 64)


def _conv_fwd(name, a, w, b, seq):
    _, t, d = a.shape
    k_taps = w.shape[0]
    ts, tps, rc = _conv_tiles(t, seq)
    hb = ts // HALO

    def body(cur_ref, prev_ref, w_ref, b_ref, v_ref, upad):
        i = pl.program_id(1)
        first = (i % tps) == 0
        pv = prev_ref[0].astype(F32)
        pg = prev_ref[1].astype(F32)
        upad[0:HALO, :] = jnp.where(first, 0.0, pv * _sigmoid(pg))
        upad[HALO:HALO + ts, :] = cur_ref[0].astype(F32) * _sigmoid(cur_ref[1].astype(F32))
        wv = w_ref[...]
        bias = jnp.broadcast_to(b_ref[...], (rc, LANE))
        for r0 in range(0, ts, rc):
            acc = bias
            for k in range(k_taps):
                acc = acc + wv[k:k + 1, :] * upad[pl.ds(HALO - (k_taps - 1) + k + r0, rc), :]
            v_ref[pl.ds(r0, rc), :] = acc

    return pl.pallas_call(
        body, name=name, grid=(d // LANE, t // ts),
        in_specs=[pl.BlockSpec((2, ts, LANE), lambda c, i: (0, i, c)),
                  pl.BlockSpec((2, HALO, LANE), lambda c, i: (0, jnp.maximum(i * hb - 1, 0), c)),
                  pl.BlockSpec((k_taps, LANE), lambda c, i: (0, c)),
                  pl.BlockSpec((1, LANE), lambda c, i: (0, c))],
        out_specs=pl.BlockSpec((ts, LANE), lambda c, i: (i, c)),
        out_shape=jax.ShapeDtypeStruct((t, d), F32),
        scratch_shapes=[pltpu.VMEM((HALO + ts, LANE), F32)],
        compiler_params=_params("parallel", "parallel"),
    )(a, a, w, b)


def _conv_bwd(name, a, dv, w, seq):
    _, t, d = a.shape
    k_taps = w.shape[0]
    ts, tps, rc = _conv_tiles(t, seq)
    hb = ts // HALO
    nhb = t // HALO

    def body(cur_ref, prev_ref, dv_ref, ndv_ref, w_ref, da_ref, dw_ref, dbp_ref, upad, dvpad, dwrows):
        i = pl.program_id(1)
        first = (i % tps) == 0
        last = (i % tps) == tps - 1
        pv = prev_ref[0].astype(F32)
        pg = prev_ref[1].astype(F32)
        upad[0:HALO, :] = jnp.where(first, 0.0, pv * _sigmoid(pg))
        upad[HALO:HALO + ts, :] = cur_ref[0].astype(F32) * _sigmoid(cur_ref[1].astype(F32))
        dvpad[0:ts, :] = dv_ref[...]
        dvpad[ts:ts + HALO, :] = jnp.where(last, 0.0, ndv_ref[...])
        wv = w_ref[...]

        @pl.when(i == 0)
        def _():
            dw_ref[...] = jnp.zeros_like(dw_ref)
            dbp_ref[...] = jnp.zeros_like(dbp_ref)

        sv = jnp.zeros((1, LANE), F32)
        sg = jnp.zeros((1, LANE), F32)
        for r0 in range(0, ts, rc):
            du = jnp.zeros((rc, LANE), F32)
            for k in range(k_taps):
                du = du + wv[k:k + 1, :] * dvpad[pl.ds(r0 + (k_taps - 1) - k, rc), :]
            av = cur_ref[0, pl.ds(r0, rc), :].astype(F32)
            sig = _sigmoid(cur_ref[1, pl.ds(r0, rc), :].astype(F32))
            dval = du * sig
            dgate = du * av * sig * (1.0 - sig)
            da_ref[0, pl.ds(r0, rc), :] = dval.astype(BF16)
            da_ref[1, pl.ds(r0, rc), :] = dgate.astype(BF16)
            sv = sv + jnp.sum(dval, axis=0, keepdims=True)
            sg = sg + jnp.sum(dgate, axis=0, keepdims=True)
        dbp_ref[0] += sv
        dbp_ref[1] += sg

        for k in range(k_taps):
            acc = jnp.zeros((rc, LANE), F32)
            for r0 in range(0, ts, rc):
                acc = acc + dvpad[pl.ds(r0, rc), :] * upad[pl.ds(HALO - (k_taps - 1) + k + r0, rc), :]
            dwrows[k:k + 1, :] = jnp.sum(acc, axis=0, keepdims=True)
        dw_ref[...] += dwrows[0:k_taps, :]

    return pl.pallas_call(
        body, name=name, grid=(d // LANE, t // ts),
        in_specs=[pl.BlockSpec((2, ts, LANE), lambda c, i: (0, i, c)),
                  pl.BlockSpec((2, HALO, LANE), lambda c, i: (0, jnp.maximum(i * hb - 1, 0), c)),
                  pl.BlockSpec((ts, LANE), lambda c, i: (i, c)),
                  pl.BlockSpec((HALO, LANE), lambda c, i: (jnp.minimum((i + 1) * hb, nhb - 1), c)),
                  pl.BlockSpec((k_taps, LANE), lambda c, i: (0, c))],
        out_specs=[pl.BlockSpec((2, ts, LANE), lambda c, i: (0, i, c)),
                   pl.BlockSpec((k_taps, LANE), lambda c, i: (0, c)),
                   pl.BlockSpec((2, 1, LANE), lambda c, i: (0, 0, c))],
        out_shape=[jax.ShapeDtypeStruct((2, t, d), BF16), jax.ShapeDtypeStruct((k_taps, d), F32),
                   jax.ShapeDtypeStruct((2, 1, d), F32)],
        scratch_shapes=[pltpu.VMEM((HALO + ts, LANE), F32), pltpu.VMEM((ts + HALO, LANE), F32),
                        pltpu.VMEM((HALO, LANE), F32)],
        compiler_params=_params("parallel", "arbitrary"),
    )(a, a, dv, dv, w)


def _pool_fwd(name, x, gain, seq):
    t, d = x.shape
    ts = _tile(seq, 256)
    tps = seq // ts
    hb = ts // HALO
    cg = d // N_GROUPS

    def body(cur_ref, prev_ref, g_ref, o_ref, hpad):
        i = pl.program_id(0)
        first = (i % tps) == 0
        g = g_ref[...]

        def norm(xv):
            return xv * lax.rsqrt(jnp.mean(xv * xv, axis=-1, keepdims=True) + RMS_EPS) * g

        hpad[0:HALO, :] = jnp.where(first, 0.0, norm(prev_ref[...]))
        hpad[HALO:HALO + ts, :] = norm(cur_ref[...])
        pos = (i % tps) * ts + lax.broadcasted_iota(jnp.int32, (ts, 1), 0)
        for gi, win in enumerate(POOL_WINDOWS):
            sl = slice(gi * cg, (gi + 1) * cg)
            own = hpad[HALO:HALO + ts, sl]
            acc = own
            for j in range(1, win):
                acc = acc + hpad[HALO - j:HALO - j + ts, sl]
            cnt = jnp.minimum(pos + 1, win).astype(F32)
            o_ref[:, sl] = (acc / cnt - own).astype(BF16)

    return pl.pallas_call(
        body, name=name, grid=(t // ts,),
        in_specs=[pl.BlockSpec((ts, d), lambda i: (i, 0)),
                  pl.BlockSpec((HALO, d), lambda i: (jnp.maximum(i * hb - 1, 0), 0)),
                  pl.BlockSpec((1, d), lambda i: (0, 0))],
        out_specs=pl.BlockSpec((ts, d), lambda i: (i, 0)),
        out_shape=jax.ShapeDtypeStruct((t, d), BF16),
        scratch_shapes=[pltpu.VMEM((HALO + ts, d), F32)],
        compiler_params=_params("parallel"),
    )(x, x, gain)


def _pool_bwd_mm(name, pooled, wp, dp, scale, bias):
    t, d = pooled.shape
    cg = d // N_GROUPS
    tm = _tile(t, 512)

    def body(p_ref, w_ref, dp_ref, s_ref, b_ref, dpo_ref, dmx_ref, ds_ref, db_ref):
        i = pl.program_id(1)
        wv = w_ref[...]
        mixed = jnp.dot(p_ref[...], wv, preferred_element_type=F32)
        dpv = dp_ref[...]
        dmx = dpv * s_ref[...]
        dmx16 = dmx.astype(BF16)
        dmx_ref[...] = dmx16
        dpo_ref[...] = lax.dot_general(dmx16, wv, (NT, ((), ())), preferred_element_type=F32)

        @pl.when(i == 0)
        def _():
            ds_ref[...] = jnp.zeros_like(ds_ref)
            db_ref[...] = jnp.zeros_like(db_ref)

        ds_ref[...] += jnp.sum(dpv * (mixed + b_ref[...]), axis=0, keepdims=True)
        db_ref[...] += jnp.sum(dmx, axis=0, keepdims=True)

    blk = pl.BlockSpec((tm, cg), lambda g, i: (i, g))
    vec = pl.BlockSpec((1, cg), lambda g, i: (0, g))
    return pl.pallas_call(
        body, name=name, grid=(N_GROUPS, t // tm),
        in_specs=[blk, pl.BlockSpec((None, cg, cg), lambda g, i: (g, 0, 0)), blk, vec, vec],
        out_specs=[blk, blk, vec, vec],
        out_shape=[jax.ShapeDtypeStruct((t, d), F32), jax.ShapeDtypeStruct((t, d), BF16),
                   jax.ShapeDtypeStruct((1, d), F32), jax.ShapeDtypeStruct((1, d), F32)],
        compiler_params=_params("parallel", "arbitrary"),
    )(pooled, wp, dp, scale, bias)


def _pool_bwd(name, dpooled, x, gain, dres, seq):
    t, d = x.shape
    ts = _tile(seq, 256)
    tps = seq // ts
    hb = ts // HALO
    nhb = t // HALO
    cg = d // N_GROUPS

    def body(dpo_ref, ndpo_ref, x_ref, g_ref, dres_ref, dx_ref, dg_ref, qpad, dh):
        i = pl.program_id(0)
        last = (i % tps) == tps - 1
        pos = (i % tps) * ts + lax.broadcasted_iota(jnp.int32, (ts, 1), 0)
        for gi, win in enumerate(POOL_WINDOWS):
            sl = slice(gi * cg, (gi + 1) * cg)
            cur = dpo_ref[:, sl]
            qpad[0:ts, sl] = cur / jnp.minimum(pos + 1, win).astype(F32)
            qpad[ts:ts + HALO, sl] = jnp.where(last, 0.0, ndpo_ref[:, sl] / float(win))
            acc = -cur
            for j in range(win):
                acc = acc + qpad[j:j + ts, sl]
            dh[:, sl] = acc
        xv = x_ref[...]
        rstd = lax.rsqrt(jnp.mean(xv * xv, axis=-1, keepdims=True) + RMS_EPS)
        xhat = xv * rstd
        dhv = dh[...]
        dxhat = dhv * g_ref[...]
        dx_ref[...] = dres_ref[...] + rstd * (dxhat - xhat * jnp.mean(dxhat * xhat, axis=-1, keepdims=True))

        @pl.when(i == 0)
        def _():
            dg_ref[...] = jnp.zeros_like(dg_ref)

        dg_ref[...] += jnp.sum(dhv * xhat, axis=0, keepdims=True)

    row = pl.BlockSpec((ts, d), lambda i: (i, 0))
    vec = pl.BlockSpec((1, d), lambda i: (0, 0))
    return pl.pallas_call(
        body, name=name, grid=(t // ts,),
        in_specs=[row, pl.BlockSpec((HALO, d), lambda i: (jnp.minimum((i + 1) * hb, nhb - 1), 0)), row, vec, row],
        out_specs=[row, vec],
        out_shape=[jax.ShapeDtypeStruct((t, d), F32), jax.ShapeDtypeStruct((1, d), F32)],
        scratch_shapes=[pltpu.VMEM((ts + HALO, d), F32), pltpu.VMEM((ts, d), F32)],
        compiler_params=_params("arbitrary"),
    )(dpooled, dpooled, x, gain, dres)


def _ctile(n, pref):
    return max(c for c in range(LANE, min(pref, n) + 1, LANE) if n % c == 0)


FFN_COLS = 1408
FFN_ROWS = 32


def _ffn_fwd(name, up, w, b, seq):
    _, t, dff = up.shape
    f = _ctile(dff, FFN_COLS)
    k_taps = w.shape[0]
    ts = _tile(seq, 256)
    tps = seq // ts
    hb = ts // HALO16
    rc = _tile(ts, FFN_ROWS)

    def body(cur_ref, prev_ref, w_ref, b_ref, g_ref, apad):
        i = pl.program_id(1)
        first = (i % tps) == 0
        for ci, c0 in enumerate(range(0, f, LANE)):
            cols = slice(c0, c0 + LANE)
            apad[ci, 0:HALO16, :] = jnp.where(first, 0.0, prev_ref[:, cols].astype(F32))
            apad[ci, HALO16:HALO16 + ts, :] = cur_ref[0, :, cols].astype(F32)
            wv = w_ref[:, cols]
            wk = [jnp.broadcast_to(wv[k:k + 1, :], (rc, LANE)) for k in range(k_taps)]
            bias = jnp.broadcast_to(b_ref[:, cols], (rc, LANE))
            for r0 in range(0, ts, rc):
                c = bias
                for k in range(k_taps):
                    c = c + wk[k] * apad[ci, pl.ds(HALO16 - (k_taps - 1) + k + r0, rc), :]
                gate = cur_ref[1, pl.ds(r0, rc), cols].astype(F32)
                g_ref[pl.ds(r0, rc), cols] = (c * _sigmoid(c) * gate).astype(BF16)

    return pl.pallas_call(
        body, name=name, grid=(dff // f, t // ts),
        in_specs=[pl.BlockSpec((2, ts, f), lambda j, i: (0, i, j)),
                  pl.BlockSpec((None, HALO16, f), lambda j, i: (0, jnp.maximum(i * hb - 1, 0), j)),
                  pl.BlockSpec((k_taps, f), lambda j, i: (0, j)),
                  pl.BlockSpec((1, f), lambda j, i: (0, j))],
        out_specs=pl.BlockSpec((ts, f), lambda j, i: (i, j)),
        out_shape=jax.ShapeDtypeStruct((t, dff), BF16),
        scratch_shapes=[pltpu.VMEM((f // LANE, HALO16 + ts, LANE), F32)],
        compiler_params=_params("parallel", "parallel"),
    )(up, up, w, b)


def _ffn_bwd(name, up, dg, w, b, seq):
    _, t, dff = up.shape
    f = _ctile(dff, FFN_COLS)
    k_taps = w.shape[0]
    ts = _tile(seq, 256)
    tps = seq // ts
    hb = ts // HALO16
    nhb = t // HALO16
    ext = ts + HALO16
    rc = _tile(ts, FFN_ROWS)

    def body(cur_ref, prev_ref, next_ref, dg_ref, ndg_ref, w_ref, b_ref, dup_ref, dw_ref, db_ref, apad, dcpad):
        i = pl.program_id(1)
        first = (i % tps) == 0
        last = (i % tps) == tps - 1
        @pl.when(i == 0)
        def _():
            dw_ref[...] = jnp.zeros_like(dw_ref)
            db_ref[...] = jnp.zeros_like(db_ref)

        for ci, c0 in enumerate(range(0, f, LANE)):
            cols = slice(c0, c0 + LANE)
            apad[ci, 0:HALO16, :] = jnp.where(first, 0.0, prev_ref[:, cols].astype(F32))
            apad[ci, HALO16:HALO16 + ts, :] = cur_ref[0, :, cols].astype(F32)
            apad[ci, HALO16 + ts:HALO16 + ext, :] = next_ref[0, :, cols].astype(F32)
            wv = w_ref[:, cols]
            wk = [jnp.broadcast_to(wv[k:k + 1, :], (rc, LANE)) for k in range(k_taps)]
            bias = jnp.broadcast_to(b_ref[:, cols], (rc, LANE))

            def conv_grad(r0, n, gate, dgv):
                c = bias[0:n]
                for k in range(k_taps):
                    c = c + wk[k][0:n] * apad[ci, pl.ds(HALO16 - (k_taps - 1) + k + r0, n), :]
                sig = _sigmoid(c)
                return dgv * gate * sig * (1.0 + c * (1.0 - sig)), c * sig

            for r0 in range(0, ts, rc):
                dgv = dg_ref[pl.ds(r0, rc), cols].astype(F32)
                dc, silu = conv_grad(r0, rc, cur_ref[1, pl.ds(r0, rc), cols].astype(F32), dgv)
                dcpad[ci, pl.ds(r0, rc), :] = dc
                dup_ref[1, pl.ds(r0, rc), cols] = (dgv * silu).astype(BF16)
            dgv = jnp.where(last, 0.0, ndg_ref[:, cols].astype(F32))
            dc, _ = conv_grad(ts, HALO16, next_ref[1, :, cols].astype(F32), dgv)
            dcpad[ci, ts:ext, :] = dc

            dw_acc = [jnp.zeros((rc, LANE), F32) for _ in range(k_taps)]
            db_acc = jnp.zeros((rc, LANE), F32)
            for r0 in range(0, ts, rc):
                dact = jnp.zeros((rc, LANE), F32)
                for k in range(k_taps):
                    dact = dact + wk[k] * dcpad[ci, pl.ds(r0 + (k_taps - 1) - k, rc), :]
                dup_ref[0, pl.ds(r0, rc), cols] = dact.astype(BF16)
                dc = dcpad[ci, pl.ds(r0, rc), :]
                for k in range(k_taps):
                    dw_acc[k] = dw_acc[k] + dc * apad[ci, pl.ds(HALO16 - (k_taps - 1) + k + r0, rc), :]
                db_acc = db_acc + dc
            for k in range(k_taps):
                dw_ref[k:k + 1, cols] += jnp.sum(dw_acc[k], axis=0, keepdims=True)
            db_ref[:, cols] += jnp.sum(db_acc, axis=0, keepdims=True)

    return pl.pallas_call(
        body, name=name, grid=(dff // f, t // ts),
        in_specs=[pl.BlockSpec((2, ts, f), lambda j, i: (0, i, j)),
                  pl.BlockSpec((None, HALO16, f), lambda j, i: (0, jnp.maximum(i * hb - 1, 0), j)),
                  pl.BlockSpec((2, HALO16, f), lambda j, i: (0, jnp.minimum((i + 1) * hb, nhb - 1), j)),
                  pl.BlockSpec((ts, f), lambda j, i: (i, j)),
                  pl.BlockSpec((HALO16, f), lambda j, i: (jnp.minimum((i + 1) * hb, nhb - 1), j)),
                  pl.BlockSpec((k_taps, f), lambda j, i: (0, j)),
                  pl.BlockSpec((1, f), lambda j, i: (0, j))],
        out_specs=[pl.BlockSpec((2, ts, f), lambda j, i: (0, i, j)),
                   pl.BlockSpec((k_taps, f), lambda j, i: (0, j)),
                   pl.BlockSpec((1, f), lambda j, i: (0, j))],
        out_shape=[jax.ShapeDtypeStruct((2, t, dff), BF16), jax.ShapeDtypeStruct((k_taps, dff), F32),
                   jax.ShapeDtypeStruct((1, dff), F32)],
        scratch_shapes=[pltpu.VMEM((f // LANE, HALO16 + ext, LANE), F32), pltpu.VMEM((f // LANE, ext, LANE), F32)],
        compiler_params=_params("parallel", "arbitrary"),
    )(up, up, up, dg, dg, w, b)


def _sum_rows(name, g):
    ns, r, c = g.shape
    tr = _tile(r, 256)

    def body(g_ref, o_ref):
        acc = g_ref[0]
        for dev in range(1, ns):
            acc = acc + g_ref[dev]
        o_ref[...] = acc

    return pl.pallas_call(
        body, name=name, grid=(r // tr,),
        in_specs=[pl.BlockSpec((ns, tr, c), lambda i: (0, i, 0))],
        out_specs=pl.BlockSpec((tr, c), lambda i: (i, 0)),
        out_shape=jax.ShapeDtypeStruct((r, c), F32), compiler_params=_params("parallel"),
    )(g)


def _adamw(name, gsrc, w, m, v, layer=0, prev=None):
    ns, r, c = gsrc.shape
    nl = w.shape[0]
    tr = _tile(r, 256)
    prev = () if prev is None else tuple(prev)

    def body(g_ref, w_ref, m_ref, v_ref, *rest):
        go_ref, do_ref, mo_ref, vo_ref = rest[len(prev):]
        g = g_ref[0].astype(F32)
        for dev in range(1, ns):
            g = g + g_ref[dev].astype(F32)
        m_new = ADAM_B1 * m_ref[...] + (1.0 - ADAM_B1) * g
        v_new = ADAM_B2 * v_ref[...] + (1.0 - ADAM_B2) * (g * g)
        m_hat = m_new / (1.0 - ADAM_B1 ** ADAM_STEP)
        v_hat = v_new / (1.0 - ADAM_B2 ** ADAM_STEP)
        go_ref[...] = g
        do_ref[...] = -ADAM_LR * (m_hat / (jnp.sqrt(v_hat) + ADAM_EPS) + ADAM_WD * w_ref[...])
        mo_ref[...] = m_new
        vo_ref[...] = v_new

    row = pl.BlockSpec((None, tr, c), lambda i: (layer, i, 0))
    return pl.pallas_call(
        body, name=name, grid=(r // tr,),
        in_specs=[pl.BlockSpec((ns, tr, c), lambda i: (0, i, 0)), row, row, row] + [ANY] * len(prev),
        out_specs=[row] * 4, out_shape=[jax.ShapeDtypeStruct((nl, r, c), F32)] * 4,
        input_output_aliases={4 + i: i for i in range(len(prev))},
        compiler_params=_params("parallel"),
    )(gsrc, w, m, v, *prev)


def _ffn_forward(tag, r_in, gain, get_wu, get_wd, wdw, bdw, seq):
    t, d = r_in.shape
    tm = _tile(t, 512)
    h = _rms_fwd(f"{tag}_rms", r_in, gain)
    wu = get_wu(h)
    dff = wu.shape[0] // 2
    up = _mm(f"{tag}_up", h, wu, grid=(2, t // tm, 1),
             a_spec=pl.BlockSpec((tm, d), lambda j, i, k: (i, 0)),
             b_spec=pl.BlockSpec((dff, d), lambda j, i, k: (j, 0)),
             out_spec=pl.BlockSpec((None, tm, dff), lambda j, i, k: (j, i, 0)),
             out_shape=jax.ShapeDtypeStruct((2, t, dff), BF16), dims=NT, acc_shape=(tm, dff))
    wd = get_wd(up)
    g = _ffn_fwd(f"{tag}_act", up, wdw, bdw, seq)
    r_out = _mm(f"{tag}_down", g, wd, grid=(t // tm, 1, 1),
                a_spec=pl.BlockSpec((tm, dff), lambda i, j, k: (i, 0)),
                b_spec=pl.BlockSpec((dff, d), lambda i, j, k: (0, 0)),
                out_spec=pl.BlockSpec((tm, d), lambda i, j, k: (i, 0)),
                out_shape=jax.ShapeDtypeStruct((t, d), F32), dims=NN, acc_shape=(tm, d),
                extras=(r_in,), extra_specs=(pl.BlockSpec((tm, d), lambda i, j, k: (i, 0)),),
                epilogue=lambda acc, res: res + acc)
    return r_out, (r_in, h, up, g, wu, wd)


def _ffn_backward(tag, dr, saved, gain, wdw, bdw, seq, token=None):
    r_in, h, up, g, wu, wd = saved
    t, d = r_in.shape
    dff = wd.shape[0]
    tm = _tile(t, 512)
    tk = _tile(t, 1024)
    cw = _ctile(dff, 1408)
    nc = dff // cw
    dg = _mm(f"{tag}_dg", dr, wd, grid=(t // tm, 1, 1),
             a_spec=pl.BlockSpec((tm, d), lambda i, j, k: (i, 0)),
             b_spec=pl.BlockSpec((dff, d), lambda i, j, k: (0, 0)),
             out_spec=pl.BlockSpec((tm, dff), lambda i, j, k: (i, 0)),
             out_shape=jax.ShapeDtypeStruct((t, dff), BF16), dims=NT, acc_shape=(tm, dff), token=token)
    dwd = _mm(f"{tag}_dwd", g, dr, grid=(dff // cw, 1, t // tk),
              a_spec=pl.BlockSpec((tk, cw), lambda i, j, k: (k, i)),
              b_spec=pl.BlockSpec((tk, d), lambda i, j, k: (k, 0)),
              out_spec=pl.BlockSpec((cw, d), lambda i, j, k: (i, 0)),
              out_shape=jax.ShapeDtypeStruct((dff, d), BF16), dims=TN, acc_shape=(cw, d))
    dup, dwdw, dbdw = _ffn_bwd(f"{tag}_dact", up, dg, wdw, bdw, seq)
    dh = _mm(f"{tag}_dh", dup, wu, grid=(t // tm, 1, 2),
             a_spec=pl.BlockSpec((None, tm, dff), lambda i, j, k: (k, i, 0)),
             b_spec=pl.BlockSpec((dff, d), lambda i, j, k: (k, 0)),
             out_spec=pl.BlockSpec((tm, d), lambda i, j, k: (i, 0)),
             out_shape=jax.ShapeDtypeStruct((t, d), F32), dims=NN, acc_shape=(tm, d))
    dwu = _mm(f"{tag}_dwu", dup, h, grid=(2 * nc, 1, t // tk),
              a_spec=pl.BlockSpec((None, tk, cw), lambda i, j, k: (i // nc, k, i % nc)),
              b_spec=pl.BlockSpec((tk, d), lambda i, j, k: (k, 0)),
              out_spec=pl.BlockSpec((cw, d), lambda i, j, k: (i, 0)),
              out_shape=jax.ShapeDtypeStruct((2 * dff, d), BF16), dims=TN, acc_shape=(cw, d))
    dr_in, dgain, colsum = _rms_bwd(f"{tag}_drms", dh, r_in, gain, dr)
    return dr_in, dgain, dwu, dwd, dwdw, dbdw, colsum


def _pad_to(vec, n):
    return jnp.pad(vec, (0, n - vec.shape[0]))


def _pack(parts, width):
    flat = jnp.concatenate([p.reshape(-1).astype(F32) for p in parts])
    n = -(-flat.shape[0] // (8 * width)) * (8 * width)
    return _pad_to(flat, n).reshape(n // width, width)


def _unpack(mat, shapes):
    flat = mat.reshape(-1)
    out, off = [], 0
    for s in shapes:
        n = 1
        for dim in s:
            n *= dim
        out.append(flat[off:off + n].reshape(s))
        off += n
    return out


def kernel(x, norm_mix, norm_ffn, conv_w_pw1, conv_b_pw1, conv_w_dw, conv_b_dw, conv_ln_g, conv_ln_b, conv_w_pw2, conv_b_pw2, pool_w, pool_b, pool_scale, ffn_w_up, ffn_w_dw, ffn_b_dw, ffn_w_down, final_norm, loss_target, m_norm_mix, m_norm_ffn, m_conv_w_pw1, m_conv_b_pw1, m_conv_w_dw, m_conv_b_dw, m_conv_ln_g, m_conv_ln_b, m_conv_w_pw2, m_conv_b_pw2, m_pool_w, m_pool_b, m_pool_scale, m_ffn_w_up, m_ffn_w_dw, m_ffn_b_dw, m_ffn_w_down, m_final_norm, v_norm_mix, v_norm_ffn, v_conv_w_pw1, v_conv_b_pw1, v_conv_w_dw, v_conv_b_dw, v_conv_ln_g, v_conv_ln_b, v_conv_w_pw2, v_conv_b_pw2, v_pool_w, v_pool_b, v_pool_scale, v_ffn_w_up, v_ffn_w_dw, v_ffn_b_dw, v_ffn_w_down, v_final_norm):
    bsz, seq, d = x.shape
    t = bsz * seq
    k_taps = conv_w_dw.shape[1]
    cs1 = conv_w_pw1.shape[2]
    dsh = d // N_DEV
    cg = d // N_GROUPS
    cgs = pool_w.shape[2]
    fu = ffn_w_up.shape[2]
    fd = ffn_w_down.shape[1]
    dff = fd * N_DEV
    nb = N_DEV // 2
    kf = ffn_w_dw.shape[1]
    fsh = ffn_w_dw.shape[2]
    my = _lin(_me())
    tm = _tile(t, 512)

    x2 = x.reshape(t, d)
    tgt2 = loss_target.reshape(t, d)

    small_shapes = [(k_taps, dsh), (dsh,), (dsh,), (2, kf, fsh)]
    small_mine = _pack([conv_w_dw[0], pool_b[0], pool_scale[0], ffn_w_dw], LANE)
    big = [conv_w_pw1[0], conv_w_pw2[0], ffn_w_up[0].T, ffn_w_down[0], pool_w[0], ffn_w_up[1].T, ffn_w_down[1]]
    gather = _exchange_start("gather_start", [small_mine] + [w.astype(BF16) for w in big], [False] * 8)
    h0 = _rms_fwd("l0_rms", x2, norm_mix[0:1])
    small_all, w1 = _exchange_wait("gather_wait_w1", gather, [0, 1], h0)
    parts = [_unpack(small_all[dev], small_shapes) for dev in range(N_DEV)]
    wdw = jnp.concatenate([p[0] for p in parts], axis=1)
    pool_b_full = jnp.concatenate([p[1] for p in parts]).reshape(1, d)
    pool_s_full = jnp.concatenate([p[2] for p in parts]).reshape(1, d)
    fwdw = jnp.concatenate([p[3] for p in parts], axis=2)
    fbdw = ffn_b_dw.reshape(2, 1, dff)

    def columns(w):
        return w.transpose(1, 0, 2).reshape(w.shape[1], N_DEV * w.shape[2])

    def column_shards(w):
        return w.reshape(w.shape[0], N_DEV, w.shape[1] // N_DEV).transpose(1, 0, 2)

    w1 = columns(w1)
    a = _mm("l0_pw1", h0, w1, grid=(2, t // tm, 1),
            a_spec=pl.BlockSpec((tm, d), lambda j, i, k: (i, 0)),
            b_spec=pl.BlockSpec((d, d), lambda j, i, k: (0, j)),
            out_spec=pl.BlockSpec((None, tm, d), lambda j, i, k: (j, i, 0)),
            out_shape=jax.ShapeDtypeStruct((2, t, d), BF16), dims=NN, acc_shape=(tm, d),
            extras=(conv_b_pw1,), extra_specs=(pl.BlockSpec((1, d), lambda j, i, k: (0, j)),),
            epilogue=lambda acc, b: acc + b)
    (w2,) = _exchange_wait("gather_wait_w2", gather, [2], a)
    w2 = w2.reshape(d, d)
    v = _conv_fwd("l0_conv", a, wdw, conv_b_dw, seq)
    s = _ln_fwd("l0_ln", v, conv_ln_g, conv_ln_b)
    r1 = _mm("l0_pw2", s, w2, grid=(t // tm, 1, 1),
             a_spec=pl.BlockSpec((tm, d), lambda i, j, k: (i, 0)),
             b_spec=pl.BlockSpec((d, d), lambda i, j, k: (0, 0)),
             out_spec=pl.BlockSpec((tm, d), lambda i, j, k: (i, 0)),
             out_shape=jax.ShapeDtypeStruct((t, d), F32), dims=NN, acc_shape=(tm, d),
             extras=(conv_b_pw2, x2),
             extra_specs=(pl.BlockSpec((1, d), lambda i, j, k: (0, 0)), pl.BlockSpec((tm, d), lambda i, j, k: (i, 0))),
             epilogue=lambda acc, b, res: res + (acc + b))
    def up_getter(name, idx):
        return lambda after: _exchange_wait(name, gather, [idx], after)[0].reshape(2 * dff, d)

    def down_getter(name, idx):
        return lambda after: _exchange_wait(name, gather, [idx], after)[0].reshape(dff, d)

    r2, ffn0_saved = _ffn_forward("f0", r1, norm_ffn[0:1], up_getter("gather_wait_wu0", 3),
                                  down_getter("gather_wait_wd0", 4), fwdw[0], fbdw[0], seq)
    (wp,) = _exchange_wait("gather_wait_wp", gather, [5], r2)
    wp = wp.transpose(1, 0, 2, 3).reshape(N_GROUPS, cg, cg)
    pooled = _pool_fwd("l1_pool", r2, norm_mix[1:2], seq)
    r3 = _mm("l1_mix", pooled, wp, grid=(t // tm, N_GROUPS, 1),
             a_spec=pl.BlockSpec((tm, cg), lambda i, j, k: (i, j)),
             b_spec=pl.BlockSpec((None, cg, cg), lambda i, j, k: (j, 0, 0)),
             out_spec=pl.BlockSpec((tm, cg), lambda i, j, k: (i, j)),
             out_shape=jax.ShapeDtypeStruct((t, d), F32), dims=NN, acc_shape=(tm, cg),
             extras=(pool_s_full, pool_b_full, r2),
             extra_specs=(pl.BlockSpec((1, cg), lambda i, j, k: (0, j)), pl.BlockSpec((1, cg), lambda i, j, k: (0, j)),
                          pl.BlockSpec((tm, cg), lambda i, j, k: (i, j))),
             epilogue=lambda acc, sc, b, res: res + sc * (acc + b))
    r4, ffn1_saved = _ffn_forward("f1", r3, norm_ffn[1:2], up_getter("gather_wait_wu1", 6),
                                  down_getter("gather_wait_wd1", 7), fwdw[1], fbdw[1], seq)

    dr4, loss_part, dfinal, _ = _final("final", r4, tgt2, final_norm.reshape(1, d))
    dr3, dnf1, dwu1, dwd1, dfw1, dfb1, _ = _ffn_backward("f1", dr4, ffn1_saved, norm_ffn[1:2], fwdw[1], fbdw[1], seq)
    scatter_a = _exchange_start("scatter_f1_start", [dwu1.reshape(N_DEV, fu, d), dwd1.reshape(N_DEV, fd, d)],
                                [True, True])
    dpooled, dmixed, dpool_s, dpool_b = _pool_bwd_mm("l1_dmix", pooled, wp, dr3,
                                                     pool_s_full + scatter_a["token"][0:1, 0:1], pool_b_full)
    dwp = _mm("l1_dwp", pooled, dmixed, grid=(N_GROUPS, 1, t // tm),
              a_spec=pl.BlockSpec((tm, cg), lambda i, j, k: (k, i)),
              b_spec=pl.BlockSpec((tm, cg), lambda i, j, k: (k, i)),
              out_spec=pl.BlockSpec((None, cg, cg), lambda i, j, k: (i, 0, 0)),
              out_shape=jax.ShapeDtypeStruct((N_GROUPS, cg, cg), BF16), dims=TN, acc_shape=(cg, cg))
    dr2, dnm1 = _pool_bwd("l1_dpool", dpooled, r2, norm_mix[1:2], dr3, seq)
    dr1, dnf0, dwu0, dwd0, dfw0, dfb0, db2 = _ffn_backward("f0", dr2, ffn0_saved, norm_ffn[0:1], fwdw[0], fbdw[0], seq)
    dwp_b = dwp.reshape(N_GROUPS, N_DEV, cgs, cg).transpose(1, 0, 2, 3)
    scatter_b = _exchange_start("scatter_f0_start", [dwu0.reshape(N_DEV, fu, d), dwd0.reshape(N_DEV, fd, d), dwp_b],
                                [True] * 3)
    ds = _mm("l0_ds", dr1, w2, grid=(t // tm, 1, 1),
             a_spec=pl.BlockSpec((tm, d), lambda i, j, k: (i, 0)),
             b_spec=pl.BlockSpec((d, d), lambda i, j, k: (0, 0)),
             out_spec=pl.BlockSpec((tm, d), lambda i, j, k: (i, 0)),
             out_shape=jax.ShapeDtypeStruct((t, d), F32), dims=NT, acc_shape=(tm, d), token=scatter_b["token"])
    dw2 = _mm("l0_dw2", s, dr1, grid=(1, 1, t // tm),
              a_spec=pl.BlockSpec((tm, d), lambda i, j, k: (k, 0)),
              b_spec=pl.BlockSpec((tm, d), lambda i, j, k: (k, 0)),
              out_spec=pl.BlockSpec((d, d), lambda i, j, k: (0, 0)),
              out_shape=jax.ShapeDtypeStruct((d, d), BF16), dims=TN, acc_shape=(d, d))
    dv, dlg, dlb, dbdw = _ln_bwd("l0_dln", ds, v, conv_ln_g, conv_ln_b)
    da, dwdw, db1 = _conv_bwd("l0_dconv", a, dv, wdw, seq)
    tk = _tile(t, 1024)
    dw1 = _mm("l0_dw1", h0, da, grid=(1, 2, t // tk),
              a_spec=pl.BlockSpec((tk, d), lambda i, j, k: (k, 0)),
              b_spec=pl.BlockSpec((None, tk, d), lambda i, j, k: (j, k, 0)),
              out_spec=pl.BlockSpec((d, d), lambda i, j, k: (0, j)),
              out_shape=jax.ShapeDtypeStruct((d, 2 * d), BF16), dims=TN, acc_shape=(d, d))
    scatter_c = _exchange_start("scatter_l0_start", [column_shards(dw1), dw2.reshape(N_DEV, d // N_DEV, d)],
                                [True, True])
    dh0 = _mm("l0_dh", da, w1, grid=(t // tm, 1, 2),
              a_spec=pl.BlockSpec((None, tm, d), lambda i, j, k: (k, i, 0)),
              b_spec=pl.BlockSpec((d, d), lambda i, j, k: (0, k)),
              out_spec=pl.BlockSpec((tm, d), lambda i, j, k: (i, 0)),
              out_shape=jax.ShapeDtypeStruct((t, d), F32), dims=NT, acc_shape=(tm, d), token=scatter_c["token"])
    dx, dnm0, _ = _rms_bwd("l0_drms", dh0, x2, norm_mix[0:1], dr1)

    dffn_w = jnp.stack([dfw0, dfw1])
    dffn_b = jnp.stack([dfb0, dfb1]).reshape(2, dff)
    small_parts = [loss_part, jnp.concatenate([dnm0, dnm1]), jnp.concatenate([dnf0, dnf1]), db1, dwdw, dbdw, dlg, dlb,
                   db2, dpool_b, dpool_s, dffn_w, dffn_b, dfinal]
    small_part_shapes = [(1,), (2, d), (2, d), (1, 2 * d), (k_taps, d), (1, d), (1, d), (1, d), (1, d), (1, d), (1, d),
                         (2, kf, dff), (2, dff), (d,)]
    packed = _pack(small_parts, 8 * LANE)
    gather_small = _exchange_start("gather_small_start", [packed], [False])

    def big_update(name, recv, w, m, v, layer=0, prev=None):
        shape = w.shape
        c = recv.shape[-1]
        rows = recv.size // (N_DEV * c)
        nl = w.size // (rows * c)
        outs = _adamw(name, recv.reshape(N_DEV, rows, c), w.reshape(nl, rows, c), m.reshape(nl, rows, c),
                      v.reshape(nl, rows, c), layer, prev)
        return outs, [o.reshape(shape) for o in outs]

    wu_t = [p.transpose(0, 2, 1) for p in (ffn_w_up, m_ffn_w_up, v_ffn_w_up)]
    g_wu1, g_wd1 = _exchange_wait("scatter_f1_wait", scatter_a, [0, 1], gather_small["token"])
    raw_wu, _ = big_update("adam_wu1", g_wu1, *wu_t, 1)
    raw_wd, _ = big_update("adam_wd1", g_wd1, ffn_w_down, m_ffn_w_down, v_ffn_w_down, 1)
    g_wu0, g_wd0, g_wp = _exchange_wait("scatter_f0_wait", scatter_b, [0, 1, 2], raw_wd[0])
    _, u_wu = big_update("adam_wu0", g_wu0, *wu_t, 0, raw_wu)
    u_wu = [o.transpose(0, 2, 1) for o in u_wu]
    _, u_wd = big_update("adam_wd0", g_wd0, ffn_w_down, m_ffn_w_down, v_ffn_w_down, 0, raw_wd)
    _, u_wp = big_update("adam_wp", g_wp, pool_w, m_pool_w, v_pool_w)
    g_w1, g_w2 = _exchange_wait("scatter_l0_wait", scatter_c, [0, 1], u_wp[0])
    _, u_w1 = big_update("adam_w1", g_w1, conv_w_pw1, m_conv_w_pw1, v_conv_w_pw1)
    _, u_w2 = big_update("adam_w2", g_w2, conv_w_pw2, m_conv_w_pw2, v_conv_w_pw2)
    (all_small,) = _exchange_wait("gather_small_wait", gather_small, [0], u_w2[0])
    summed = _sum_rows("sum_small_grads", all_small)
    (loss_v, g_nm, g_nf, g_b1, g_wdw, g_bdw, g_lg, g_lb, g_b2, g_pb, g_ps, g_fw, g_fb,
     g_fin) = _unpack(summed, small_part_shapes)
    loss = loss_v[0]
    g_wdw_mine = lax.dynamic_slice_in_dim(g_wdw, my * dsh, dsh, axis=1)[None]
    g_pb_mine = lax.dynamic_slice_in_dim(g_pb, my * dsh, dsh, axis=1)
    g_ps_mine = lax.dynamic_slice_in_dim(g_ps, my * dsh, dsh, axis=1)
    g_fw_mine = lax.dynamic_slice_in_dim(g_fw, my * fsh, fsh, axis=2)

    small_g =[g_nm, g_nf, g_b1, g_wdw_mine, g_bdw, g_lg, g_lb, g_b2, g_pb_mine, g_ps_mine, g_fw_mine, g_fb, g_fin]
    small_w = [norm_mix, norm_ffn, conv_b_pw1, conv_w_dw, conv_b_dw, conv_ln_g, conv_ln_b, conv_b_pw2, pool_b,
               pool_scale, ffn_w_dw, ffn_b_dw, final_norm]
    small_m = [m_norm_mix, m_norm_ffn, m_conv_b_pw1, m_conv_w_dw, m_conv_b_dw, m_conv_ln_g, m_conv_ln_b,
               m_conv_b_pw2, m_pool_b, m_pool_scale, m_ffn_w_dw, m_ffn_b_dw, m_final_norm]
    small_v = [v_norm_mix, v_norm_ffn, v_conv_b_pw1, v_conv_w_dw, v_conv_b_dw, v_conv_ln_g, v_conv_ln_b,
               v_conv_b_pw2, v_pool_b, v_pool_scale, v_ffn_w_dw, v_ffn_b_dw, v_final_norm]
    shapes = [w.shape for w in small_w]
    outs = _adamw("adam_small", _pack(small_g, 8 * LANE)[None], _pack(small_w, 8 * LANE)[None],
                  _pack(small_m, 8 * LANE)[None], _pack(small_v, 8 * LANE)[None])
    sg, sd, sm, sv = [_unpack(o, shapes) for o in outs]

    def leaf(kind):
        (nm, nf, b1, wdw_, bdw_, lg, lb, b2, pb, ps, fw, fb, fin) = (sg, sd, sm, sv)[kind]
        return [nm, nf, u_w1[kind], b1, wdw_, bdw_, lg, lb, u_w2[kind], b2, u_wp[kind], pb, ps, u_wu[kind], fw, fb,
                u_wd[kind], fin]

    return (loss, dx.reshape(bsz, seq, d), *leaf(0), *leaf(1), *leaf(2), *leaf(3))
```

```python
import functools

import jax
import jax.numpy as jnp
from jax import lax
from jax.experimental import pallas as pl
from jax.experimental.pallas import tpu as pltpu

F32 = jnp.float32
BF16 = jnp.bfloat16
MESH = pl.DeviceIdType.MESH
HBM = pl.BlockSpec(memory_space=pltpu.HBM)

N_DEV = 8
RMS_EPS = 1e-6
LN_EPS = 1e-5
POOL_WINDOWS = (2, 4, 8, 16)
N_GROUPS = len(POOL_WINDOWS)
ADAM_LR = 0.001
ADAM_B1 = 0.9
ADAM_B2 = 0.999
ADAM_EPS = 1e-08
ADAM_WD = 0.01
ADAM_STEP = 10

LANE = 128
HALO = 32
HALO16 = 16
VMEM_LIMIT = 56 * 1024 * 1024


def _params(*sem):
    return pltpu.CompilerParams(dimension_semantics=sem if sem else None, vmem_limit_bytes=VMEM_LIMIT)


def _tile(n, pref):
    for t in range(min(pref, n), 15, -1):
        if n % t == 0 and t % 16 == 0:
            return t
    return n


def _sigmoid(z):
    return 1.0 / (1.0 + jnp.exp(-z))


def _me():
    return lax.axis_index("x"), lax.axis_index("y"), lax.axis_index("c")


def _flip(pos, m):
    x, y, c = pos
    return ((1 - x) if m & 4 else x, (1 - y) if m & 2 else y, (1 - c) if m & 1 else c)


def _lin(pos):
    return 4 * pos[0] + 2 * pos[1] + pos[2]


SEM = pl.BlockSpec(memory_space=pltpu.SEMAPHORE)
ANY = pl.BlockSpec(memory_space=pl.ANY)
EFFECT = pltpu.SideEffectType.DATAFLOW_SIDE_EFFECTING


def _exchange_copies(srcs, lands, send_sems, recv_sems, modes, which, starting):
    me = _me()
    my = _lin(me)
    out = []
    for pos, a in enumerate(which):
        src, land = srcs[pos], lands[pos]

        def block(pid, src=src, a=a):
            return src.at[pid] if modes[a] else src

        local = pltpu.make_async_copy(block(my), land.at[my], send_sems.at[a * N_DEV])
        remote = []
        for m in range(1, N_DEV):
            peer = _flip(me, m)
            pid = _lin(peer)
            sems = dict(send_sem=send_sems.at[a * N_DEV + m], recv_sem=recv_sems.at[a * N_DEV + m],
                        device_id=peer, device_id_type=MESH)
            if starting:
                remote.append(pltpu.make_async_remote_copy(src_ref=block(pid), dst_ref=land.at[my], **sems))
            else:
                remote.append((pltpu.make_async_remote_copy(src_ref=block(pid), dst_ref=land.at[my], **sems),
                               pltpu.make_async_remote_copy(src_ref=block(pid), dst_ref=land.at[pid], **sems)))
        out.append((local, remote))
    return out


def _exchange_start(name, arrs, modes):
    n = len(arrs)
    blocks = [a.shape[1:] if md else a.shape for a, md in zip(arrs, modes)]

    def body(*refs):
        srcs, lands = refs[:n], refs[n:2 * n]
        send_sems, recv_sems = refs[2 * n], refs[2 * n + 1]
        token = refs[-1]
        for local, remote in _exchange_copies(srcs, lands, send_sems, recv_sems, modes, list(range(n)), True):
            local.start()
            for send in remote:
                send.start()
        token[...] = jnp.zeros_like(token)

    lands = [lax.empty((N_DEV,) + tuple(b), a.dtype) for a, b in zip(arrs, blocks)]
    outs = pl.pallas_call(
        body, name=name,
        out_shape=(pltpu.SemaphoreType.DMA((n * N_DEV,)), pltpu.SemaphoreType.DMA((n * N_DEV,)),
                   *[pltpu.HBM(a.shape, a.dtype) for a in arrs], *[pltpu.HBM(l.shape, l.dtype) for l in lands],
                   jax.ShapeDtypeStruct((8, LANE), F32)),
        in_specs=[HBM] * (2 * n),
        out_specs=(SEM, SEM, *[HBM] * (2 * n), pl.BlockSpec(memory_space=pltpu.VMEM)),
        input_output_aliases={i: 2 + i for i in range(2 * n)},
        compiler_params=pltpu.CompilerParams(has_side_effects=EFFECT),
    )(*[pltpu.with_memory_space_constraint(a, pltpu.HBM) for a in arrs],
      *[pltpu.with_memory_space_constraint(l, pltpu.HBM) for l in lands])
    return dict(send=outs[0], recv=outs[1], srcs=list(outs[2:2 + n]), lands=list(outs[2 + n:2 + 2 * n]),
                modes=modes, token=outs[-1])


def _exchange_wait(name, handle, which, after):
    k = len(which)
    modes = handle["modes"]

    def body(*refs):
        srcs, lands = refs[:k], refs[k:2 * k]
        send_sems, recv_sems = refs[2 * k], refs[2 * k + 1]
        for local, remote in _exchange_copies(srcs, lands, send_sems, recv_sems, modes, which, False):
            local.wait()
            for send, arrival in remote:
                send.wait_send()
                arrival.wait_recv()

    srcs = [handle["srcs"][a] for a in which]
    lands = [handle["lands"][a] for a in which]
    outs = pl.pallas_call(
        body, name=name,
        out_shape=tuple(pltpu.HBM(x.shape, x.dtype) for x in srcs + lands),
        in_specs=[HBM] * (2 * k) + [SEM, SEM, ANY], out_specs=tuple([HBM] * (2 * k)),
        input_output_aliases={i: i for i in range(2 * k)},
        compiler_params=pltpu.CompilerParams(has_side_effects=EFFECT),
    )(*srcs, *lands, handle["send"], handle["recv"], after)
    return list(outs[k:])


def _mm(name, a, b, *, grid, a_spec, b_spec, out_spec, out_shape, dims, acc_shape, extras=(), extra_specs=(),
        epilogue=None, token=None, prologue=None, n_sums=0):
    nk = grid[2]
    ne = len(extras)
    deps = () if token is None else (token,)
    dep_specs = [pl.BlockSpec((8, LANE), lambda i, j, k: (0, 0))] * len(deps)
    n_out = len(out_shape) if isinstance(out_shape, (list, tuple)) else 1
    has_sums = n_sums > 0

    def body(a_ref, b_ref, *rest):
        ex, o_refs, acc_ref = rest[:ne], rest[ne + len(deps):ne + len(deps) + n_out], rest[ne + len(deps) + n_out]
        k = pl.program_id(2)
        a_blk, saved = a_ref[...], None
        if prologue is not None:
            a_blk, saved = prologue(a_blk, ex)
        part = lax.dot_general(a_blk.astype(BF16), b_ref[...].astype(BF16), (dims, ((), ())),
                               preferred_element_type=F32)

        def finish(r):
            tiles, sums = ((r,), ()) if epilogue is None else epilogue(r, ex)
            if saved is not None:
                tiles = tiles + (saved,)
            for o_ref, val in zip(o_refs, tiles):
                o_ref[...] = val.astype(o_ref.dtype)
            if sums:
                @pl.when((pl.program_id(0) == 0) & (pl.program_id(1) == 0))
                def _():
                    for o_ref in o_refs[len(tiles):]:
                        o_ref[...] = jnp.zeros_like(o_ref)

                for o_ref, val in zip(o_refs[len(tiles):], sums):
                    o_ref[...] += val

        if nk == 1:
            finish(part)
            return

        @pl.when(k == 0)
        def _():
            acc_ref[...] = part

        @pl.when((k > 0) & (k < nk - 1))
        def _():
            acc_ref[...] += part

        @pl.when(k == nk - 1)
        def _():
            finish(acc_ref[...] + part)

    return pl.pallas_call(
        body, name=name, grid=grid, in_specs=[a_spec, b_spec, *extra_specs, *dep_specs], out_specs=out_spec,
        out_shape=out_shape, scratch_shapes=[pltpu.VMEM(acc_shape if nk > 1 else (8, LANE), F32)],
        compiler_params=_params(*(("arbitrary",) * 3 if has_sums else ("parallel", "parallel", "arbitrary"))),
    )(a, b, *extras, *deps)


def _rms(x, gain):
    return x * lax.rsqrt(jnp.mean(x * x, axis=-1, keepdims=True) + RMS_EPS) * gain


def _rms_bwd_tile(dh, x, gain, dres):
    rstd = lax.rsqrt(jnp.mean(x * x, axis=-1, keepdims=True) + RMS_EPS)
    xhat = x * rstd
    dxhat = dh * gain
    dx = dres + rstd * (dxhat - xhat * jnp.mean(dxhat * xhat, axis=-1, keepdims=True))
    return dx, jnp.sum(dh * xhat, axis=0, keepdims=True), jnp.sum(dx, axis=0, keepdims=True)


def _ln_silu_tile(v, g, b):
    mu = jnp.mean(v, axis=-1, keepdims=True)
    cen = v - mu
    z = cen * lax.rsqrt(jnp.mean(cen * cen, axis=-1, keepdims=True) + LN_EPS) * g + b
    return z * _sigmoid(z)


def _ln_silu_bwd_tile(ds, v, g, b):
    mu = jnp.mean(v, axis=-1, keepdims=True)
    cen = v - mu
    rstd = lax.rsqrt(jnp.mean(cen * cen, axis=-1, keepdims=True) + LN_EPS)
    y = cen * rstd
    z = y * g + b
    sig = _sigmoid(z)
    dz = ds * sig * (1.0 + z * (1.0 - sig))
    dy = dz * g
    dv = rstd * (dy - jnp.mean(dy, axis=-1, keepdims=True) - y * jnp.mean(dy * y, axis=-1, keepdims=True))
    return (dv, jnp.sum(dz * y, axis=0, keepdims=True), jnp.sum(dz, axis=0, keepdims=True),
            jnp.sum(dv, axis=0, keepdims=True))


def _loss_tile(x, tgt, gain):
    d = x.shape[-1]
    rstd = lax.rsqrt(jnp.mean(x * x, axis=-1, keepdims=True) + RMS_EPS)
    xhat = x * rstd
    err = xhat * gain - tgt
    dy = err / d
    dxhat = dy * gain
    dx = rstd * (dxhat - xhat * jnp.mean(dxhat * xhat, axis=-1, keepdims=True))
    loss = 0.5 * jnp.sum(jnp.mean(err * err, axis=-1, keepdims=True), axis=0, keepdims=True)
    return dx, loss, jnp.sum(dy * xhat, axis=0, keepdims=True)


NN = ((1,), (0,))
NT = ((1,), (1,))
TN = ((0,), (0,))


def _rms_fwd(name, x, gain):
    t, d = x.shape
    tr = _tile(t, 512)

    def body(x_ref, g_ref, h_ref):
        xv = x_ref[...]
        rstd = lax.rsqrt(jnp.mean(xv * xv, axis=-1, keepdims=True) + RMS_EPS)
        h_ref[...] = (xv * rstd * g_ref[...]).astype(BF16)

    return pl.pallas_call(
        body, name=name, grid=(t // tr,),
        in_specs=[pl.BlockSpec((tr, d), lambda i: (i, 0)), pl.BlockSpec((1, d), lambda i: (0, 0))],
        out_specs=pl.BlockSpec((tr, d), lambda i: (i, 0)),
        out_shape=jax.ShapeDtypeStruct((t, d), BF16), compiler_params=_params("parallel"),
    )(x, gain)


def _conv_tiles(t, seq):
    ts = _tile(seq, 512)
    return ts, seq // ts, _tile(ts, 64)


def _conv_fwd(name, a, w, b, seq):
    _, t, d = a.shape
    k_taps = w.shape[0]
    ts, tps, rc = _conv_tiles(t, seq)
    hb = ts // HALO

    def body(cur_ref, prev_ref, w_ref, b_ref, v_ref, upad):
        i = pl.program_id(1)
        first = (i % tps) == 0
        pv = prev_ref[0].astype(F32)
        pg = prev_ref[1].astype(F32)
        upad[0:HALO, :] = jnp.where(first, 0.0, pv * _sigmoid(pg))
        upad[HALO:HALO + ts, :] = cur_ref[0].astype(F32) * _sigmoid(cur_ref[1].astype(F32))
        wv = w_ref[...]
        bias = jnp.broadcast_to(b_ref[...], (rc, LANE))
        for r0 in range(0, ts, rc):
            acc = bias
            for k in range(k_taps):
                acc = acc + wv[k:k + 1, :] * upad[pl.ds(HALO - (k_taps - 1) + k + r0, rc), :]
            v_ref[pl.ds(r0, rc), :] = acc

    return pl.pallas_call(
        body, name=name, grid=(d // LANE, t // ts),
        in_specs=[pl.BlockSpec((2, ts, LANE), lambda c, i: (0, i, c)),
                  pl.BlockSpec((2, HALO, LANE), lambda c, i: (0, jnp.maximum(i * hb - 1, 0), c)),
                  pl.BlockSpec((k_taps, LANE), lambda c, i: (0, c)),
                  pl.BlockSpec((1, LANE), lambda c, i: (0, c))],
        out_specs=pl.BlockSpec((ts, LANE), lambda c, i: (i, c)),
        out_shape=jax.ShapeDtypeStruct((t, d), F32),
        scratch_shapes=[pltpu.VMEM((HALO + ts, LANE), F32)],
        compiler_params=_params("parallel", "parallel"),
    )(a, a, w, b)


def _conv_bwd(name, a, dv, w, seq):
    _, t, d = a.shape
    k_taps = w.shape[0]
    ts, tps, rc = _conv_tiles(t, seq)
    hb = ts // HALO
    nhb = t // HALO

    def body(cur_ref, prev_ref, dv_ref, ndv_ref, w_ref, da_ref, dw_ref, dbp_ref, upad, dvpad, dwrows):
        i = pl.program_id(1)
        first = (i % tps) == 0
        last = (i % tps) == tps - 1
        pv = prev_ref[0].astype(F32)
        pg = prev_ref[1].astype(F32)
        upad[0:HALO, :] = jnp.where(first, 0.0, pv * _sigmoid(pg))
        upad[HALO:HALO + ts, :] = cur_ref[0].astype(F32) * _sigmoid(cur_ref[1].astype(F32))
        dvpad[0:ts, :] = dv_ref[...]
        dvpad[ts:ts + HALO, :] = jnp.where(last, 0.0, ndv_ref[...])
        wv = w_ref[...]

        @pl.when(i == 0)
        def _():
            dw_ref[...] = jnp.zeros_like(dw_ref)
            dbp_ref[...] = jnp.zeros_like(dbp_ref)

        sv = jnp.zeros((1, LANE), F32)
        sg = jnp.zeros((1, LANE), F32)
        for r0 in range(0, ts, rc):
            du = jnp.zeros((rc, LANE), F32)
            for k in range(k_taps):
                du = du + wv[k:k + 1, :] * dvpad[pl.ds(r0 + (k_taps - 1) - k, rc), :]
            av = cur_ref[0, pl.ds(r0, rc), :].astype(F32)
            sig = _sigmoid(cur_ref[1, pl.ds(r0, rc), :].astype(F32))
            dval = du * sig
            dgate = du * av * sig * (1.0 - sig)
            da_ref[0, pl.ds(r0, rc), :] = dval.astype(BF16)
            da_ref[1, pl.ds(r0, rc), :] = dgate.astype(BF16)
            sv = sv + jnp.sum(dval, axis=0, keepdims=True)
            sg = sg + jnp.sum(dgate, axis=0, keepdims=True)
        dbp_ref[0] += sv
        dbp_ref[1] += sg

        for k in range(k_taps):
            acc = jnp.zeros((rc, LANE), F32)
            for r0 in range(0, ts, rc):
                acc = acc + dvpad[pl.ds(r0, rc), :] * upad[pl.ds(HALO - (k_taps - 1) + k + r0, rc), :]
            dwrows[k:k + 1, :] = jnp.sum(acc, axis=0, keepdims=True)
        dw_ref[...] += dwrows[0:k_taps, :]

    return pl.pallas_call(
        body, name=name, grid=(d // LANE, t // ts),
        in_specs=[pl.BlockSpec((2, ts, LANE), lambda c, i: (0, i, c)),
                  pl.BlockSpec((2, HALO, LANE), lambda c, i: (0, jnp.maximum(i * hb - 1, 0), c)),
                  pl.BlockSpec((ts, LANE), lambda c, i: (i, c)),
                  pl.BlockSpec((HALO, LANE), lambda c, i: (jnp.minimum((i + 1) * hb, nhb - 1), c)),
                  pl.BlockSpec((k_taps, LANE), lambda c, i: (0, c))],
        out_specs=[pl.BlockSpec((2, ts, LANE), lambda c, i: (0, i, c)),
                   pl.BlockSpec((k_taps, LANE), lambda c, i: (0, c)),
                   pl.BlockSpec((2, 1, LANE), lambda c, i: (0, 0, c))],
        out_shape=[jax.ShapeDtypeStruct((2, t, d), BF16), jax.ShapeDtypeStruct((k_taps, d), F32),
                   jax.ShapeDtypeStruct((2, 1, d), F32)],
        scratch_shapes=[pltpu.VMEM((HALO + ts, LANE), F32), pltpu.VMEM((ts + HALO, LANE), F32),
                        pltpu.VMEM((HALO, LANE), F32)],
        compiler_params=_params("parallel", "arbitrary"),
    )(a, a, dv, dv, w)


def _pool_fwd(name, x, gain, seq):
    t, d = x.shape
    ts = _tile(seq, 256)
    tps = seq // ts
    hb = ts // HALO
    cg = d // N_GROUPS

    def body(cur_ref, prev_ref, g_ref, o_ref, hpad):
        i = pl.program_id(0)
        first = (i % tps) == 0
        g = g_ref[...]

        def norm(xv):
            return xv * lax.rsqrt(jnp.mean(xv * xv, axis=-1, keepdims=True) + RMS_EPS) * g

        hpad[0:HALO, :] = jnp.where(first, 0.0, norm(prev_ref[...]))
        hpad[HALO:HALO + ts, :] = norm(cur_ref[...])
        pos = (i % tps) * ts + lax.broadcasted_iota(jnp.int32, (ts, 1), 0)
        for gi, win in enumerate(POOL_WINDOWS):
            sl = slice(gi * cg, (gi + 1) * cg)
            own = hpad[HALO:HALO + ts, sl]
            acc = own
            for j in range(1, win):
                acc = acc + hpad[HALO - j:HALO - j + ts, sl]
            cnt = jnp.minimum(pos + 1, win).astype(F32)
            o_ref[:, sl] = (acc / cnt - own).astype(BF16)

    return pl.pallas_call(
        body, name=name, grid=(t // ts,),
        in_specs=[pl.BlockSpec((ts, d), lambda i: (i, 0)),
                  pl.BlockSpec((HALO, d), lambda i: (jnp.maximum(i * hb - 1, 0), 0)),
                  pl.BlockSpec((1, d), lambda i: (0, 0))],
        out_specs=pl.BlockSpec((ts, d), lambda i: (i, 0)),
        out_shape=jax.ShapeDtypeStruct((t, d), BF16),
        scratch_shapes=[pltpu.VMEM((HALO + ts, d), F32)],
        compiler_params=_params("parallel"),
    )(x, x, gain)


def _pool_bwd_mm(name, pooled, wp, dp, scale, bias):
    t, d = pooled.shape
    cg = d // N_GROUPS
    tm = _tile(t, 512)

    def body(p_ref, w_ref, dp_ref, s_ref, b_ref, dpo_ref, dmx_ref, ds_ref, db_ref):
        i = pl.program_id(1)
        wv = w_ref[...]
        mixed = jnp.dot(p_ref[...], wv, preferred_element_type=F32)
        dpv = dp_ref[...]
        dmx = dpv * s_ref[...]
        dmx16 = dmx.astype(BF16)
        dmx_ref[...] = dmx16
        dpo_ref[...] = lax.dot_general(dmx16, wv, (NT, ((), ())), preferred_element_type=F32)

        @pl.when(i == 0)
        def _():
            ds_ref[...] = jnp.zeros_like(ds_ref)
            db_ref[...] = jnp.zeros_like(db_ref)

        ds_ref[...] += jnp.sum(dpv * (mixed + b_ref[...]), axis=0, keepdims=True)
        db_ref[...] += jnp.sum(dmx, axis=0, keepdims=True)

    blk = pl.BlockSpec((tm, cg), lambda g, i: (i, g))
    vec = pl.BlockSpec((1, cg), lambda g, i: (0, g))
    return pl.pallas_call(
        body, name=name, grid=(N_GROUPS, t // tm),
        in_specs=[blk, pl.BlockSpec((None, cg, cg), lambda g, i: (g, 0, 0)), blk, vec, vec],
        out_specs=[blk, blk, vec, vec],
        out_shape=[jax.ShapeDtypeStruct((t, d), F32), jax.ShapeDtypeStruct((t, d), BF16),
                   jax.ShapeDtypeStruct((1, d), F32), jax.ShapeDtypeStruct((1, d), F32)],
        compiler_params=_params("parallel", "arbitrary"),
    )(pooled, wp, dp, scale, bias)


def _pool_bwd(name, dpooled, x, gain, dres, seq):
    t, d = x.shape
    ts = _tile(seq, 256)
    tps = seq // ts
    hb = ts // HALO
    nhb = t // HALO
    cg = d // N_GROUPS

    def body(dpo_ref, ndpo_ref, x_ref, g_ref, dres_ref, dx_ref, dg_ref, qpad, dh):
        i = pl.program_id(0)
        last = (i % tps) == tps - 1
        pos = (i % tps) * ts + lax.broadcasted_iota(jnp.int32, (ts, 1), 0)
        for gi, win in enumerate(POOL_WINDOWS):
            sl = slice(gi * cg, (gi + 1) * cg)
            cur = dpo_ref[:, sl]
            qpad[0:ts, sl] = cur / jnp.minimum(pos + 1, win).astype(F32)
            qpad[ts:ts + HALO, sl] = jnp.where(last, 0.0, ndpo_ref[:, sl] / float(win))
            acc = -cur
            for j in range(win):
                acc = acc + qpad[j:j + ts, sl]
            dh[:, sl] = acc
        xv = x_ref[...]
        rstd = lax.rsqrt(jnp.mean(xv * xv, axis=-1, keepdims=True) + RMS_EPS)
        xhat = xv * rstd
        dhv = dh[...]
        dxhat = dhv * g_ref[...]
        dx_ref[...] = dres_ref[...] + rstd * (dxhat - xhat * jnp.mean(dxhat * xhat, axis=-1, keepdims=True))

        @pl.when(i == 0)
        def _():
            dg_ref[...] = jnp.zeros_like(dg_ref)

        dg_ref[...] += jnp.sum(dhv * xhat, axis=0, keepdims=True)

    row = pl.BlockSpec((ts, d), lambda i: (i, 0))
    vec = pl.BlockSpec((1, d), lambda i: (0, 0))
    return pl.pallas_call(
        body, name=name, grid=(t // ts,),
        in_specs=[row, pl.BlockSpec((HALO, d), lambda i: (jnp.minimum((i + 1) * hb, nhb - 1), 0)), row, vec, row],
        out_specs=[row, vec],
        out_shape=[jax.ShapeDtypeStruct((t, d), F32), jax.ShapeDtypeStruct((1, d), F32)],
        scratch_shapes=[pltpu.VMEM((ts + HALO, d), F32), pltpu.VMEM((ts, d), F32)],
        compiler_params=_params("arbitrary"),
    )(dpooled, dpooled, x, gain, dres)


def _ctile(n, pref):
    return max(c for c in range(LANE, min(pref, n) + 1, LANE) if n % c == 0)


FFN_COLS = 1408
FFN_ROWS = 32


def _ffn_fwd(name, up, w, b, seq):
    _, t, dff = up.shape
    f = _ctile(dff, FFN_COLS)
    k_taps = w.shape[0]
    ts = _tile(seq, 256)
    tps = seq // ts
    hb = ts // HALO16
    rc = _tile(ts, FFN_ROWS)

    def body(cur_ref, prev_ref, w_ref, b_ref, g_ref, apad):
        i = pl.program_id(1)
        first = (i % tps) == 0
        for ci, c0 in enumerate(range(0, f, LANE)):
            cols = slice(c0, c0 + LANE)
            apad[ci, 0:HALO16, :] = jnp.where(first, 0.0, prev_ref[:, cols].astype(F32))
            apad[ci, HALO16:HALO16 + ts, :] = cur_ref[0, :, cols].astype(F32)
            wv = w_ref[:, cols]
            wk = [jnp.broadcast_to(wv[k:k + 1, :], (rc, LANE)) for k in range(k_taps)]
            bias = jnp.broadcast_to(b_ref[:, cols], (rc, LANE))
            for r0 in range(0, ts, rc):
                c = bias
                for k in range(k_taps):
                    c = c + wk[k] * apad[ci, pl.ds(HALO16 - (k_taps - 1) + k + r0, rc), :]
                gate = cur_ref[1, pl.ds(r0, rc), cols].astype(F32)
                g_ref[pl.ds(r0, rc), cols] = (c * _sigmoid(c) * gate).astype(BF16)

    return pl.pallas_call(
        body, name=name, grid=(dff // f, t // ts),
        in_specs=[pl.BlockSpec((2, ts, f), lambda j, i: (0, i, j)),
                  pl.BlockSpec((None, HALO16, f), lambda j, i: (0, jnp.maximum(i * hb - 1, 0), j)),
                  pl.BlockSpec((k_taps, f), lambda j, i: (0, j)),
                  pl.BlockSpec((1, f), lambda j, i: (0, j))],
        out_specs=pl.BlockSpec((ts, f), lambda j, i: (i, j)),
        out_shape=jax.ShapeDtypeStruct((t, dff), BF16),
        scratch_shapes=[pltpu.VMEM((f // LANE, HALO16 + ts, LANE), F32)],
        compiler_params=_params("parallel", "parallel"),
    )(up, up, w, b)


def _ffn_bwd(name, up, dg, w, b, seq):
    _, t, dff = up.shape
    f = _ctile(dff, FFN_COLS)
    k_taps = w.shape[0]
    ts = _tile(seq, 256)
    tps = seq // ts
    hb = ts // HALO16
    nhb = t // HALO16
    ext = ts + HALO16
    rc = _tile(ts, FFN_ROWS)

    def body(cur_ref, prev_ref, next_ref, dg_ref, ndg_ref, w_ref, b_ref, dup_ref, dw_ref, db_ref, apad, dcpad):
        i = pl.program_id(1)
        first = (i % tps) == 0
        last = (i % tps) == tps - 1
        @pl.when(i == 0)
        def _():
            dw_ref[...] = jnp.zeros_like(dw_ref)
            db_ref[...] = jnp.zeros_like(db_ref)

        for ci, c0 in enumerate(range(0, f, LANE)):
            cols = slice(c0, c0 + LANE)
            apad[ci, 0:HALO16, :] = jnp.where(first, 0.0, prev_ref[:, cols].astype(F32))
            apad[ci, HALO16:HALO16 + ts, :] = cur_ref[0, :, cols].astype(F32)
            apad[ci, HALO16 + ts:HALO16 + ext, :] = next_ref[0, :, cols].astype(F32)
            wv = w_ref[:, cols]
            wk = [jnp.broadcast_to(wv[k:k + 1, :], (rc, LANE)) for k in range(k_taps)]
            bias = jnp.broadcast_to(b_ref[:, cols], (rc, LANE))

            def conv_grad(r0, n, gate, dgv):
                c = bias[0:n]
                for k in range(k_taps):
                    c = c + wk[k][0:n] * apad[ci, pl.ds(HALO16 - (k_taps - 1) + k + r0, n), :]
                sig = _sigmoid(c)
                return dgv * gate * sig * (1.0 + c * (1.0 - sig)), c * sig

            for r0 in range(0, ts, rc):
                dgv = dg_ref[pl.ds(r0, rc), cols].astype(F32)
                dc, silu = conv_grad(r0, rc, cur_ref[1, pl.ds(r0, rc), cols].astype(F32), dgv)
                dcpad[ci, pl.ds(r0, rc), :] = dc
                dup_ref[1, pl.ds(r0, rc), cols] = (dgv * silu).astype(BF16)
            dgv = jnp.where(last, 0.0, ndg_ref[:, cols].astype(F32))
            dc, _ = conv_grad(ts, HALO16, next_ref[1, :, cols].astype(F32), dgv)
            dcpad[ci, ts:ext, :] = dc

            dw_acc = [jnp.zeros((rc, LANE), F32) for _ in range(k_taps)]
            db_acc = jnp.zeros((rc, LANE), F32)
            for r0 in range(0, ts, rc):
                dact = jnp.zeros((rc, LANE), F32)
                for k in range(k_taps):
                    dact = dact + wk[k] * dcpad[ci, pl.ds(r0 + (k_taps - 1) - k, rc), :]
                dup_ref[0, pl.ds(r0, rc), cols] = dact.astype(BF16)
                dc = dcpad[ci, pl.ds(r0, rc), :]
                for k in range(k_taps):
                    dw_acc[k] = dw_acc[k] + dc * apad[ci, pl.ds(HALO16 - (k_taps - 1) + k + r0, rc), :]
                db_acc = db_acc + dc
            for k in range(k_taps):
                dw_ref[k:k + 1, cols] += jnp.sum(dw_acc[k], axis=0, keepdims=True)
            db_ref[:, cols] += jnp.sum(db_acc, axis=0, keepdims=True)

    return pl.pallas_call(
        body, name=name, grid=(dff // f, t // ts),
        in_specs=[pl.BlockSpec((2, ts, f), lambda j, i: (0, i, j)),
                  pl.BlockSpec((None, HALO16, f), lambda j, i: (0, jnp.maximum(i * hb - 1, 0), j)),
                  pl.BlockSpec((2, HALO16, f), lambda j, i: (0, jnp.minimum((i + 1) * hb, nhb - 1), j)),
                  pl.BlockSpec((ts, f), lambda j, i: (i, j)),
                  pl.BlockSpec((HALO16, f), lambda j, i: (jnp.minimum((i + 1) * hb, nhb - 1), j)),
                  pl.BlockSpec((k_taps, f), lambda j, i: (0, j)),
                  pl.BlockSpec((1, f), lambda j, i: (0, j))],
        out_specs=[pl.BlockSpec((2, ts, f), lambda j, i: (0, i, j)),
                   pl.BlockSpec((k_taps, f), lambda j, i: (0, j)),
                   pl.BlockSpec((1, f), lambda j, i: (0, j))],
        out_shape=[jax.ShapeDtypeStruct((2, t, dff), BF16), jax.ShapeDtypeStruct((k_taps, dff), F32),
                   jax.ShapeDtypeStruct((1, dff), F32)],
        scratch_shapes=[pltpu.VMEM((f // LANE, HALO16 + ext, LANE), F32), pltpu.VMEM((f // LANE, ext, LANE), F32)],
        compiler_params=_params("parallel", "arbitrary"),
    )(up, up, up, dg, dg, w, b)


def _sum_rows(name, g):
    ns, r, c = g.shape
    tr = _tile(r, 256)

    def body(g_ref, o_ref):
        acc = g_ref[0]
        for dev in range(1, ns):
            acc = acc + g_ref[dev]
        o_ref[...] = acc

    return pl.pallas_call(
        body, name=name, grid=(r // tr,),
        in_specs=[pl.BlockSpec((ns, tr, c), lambda i: (0, i, 0))],
        out_specs=pl.BlockSpec((tr, c), lambda i: (i, 0)),
        out_shape=jax.ShapeDtypeStruct((r, c), F32), compiler_params=_params("parallel"),
    )(g)


def _adamw(name, gsrc, w, m, v, layer=0, prev=None):
    ns, r, c = gsrc.shape
    nl = w.shape[0]
    tr = _tile(r, 256)
    prev = () if prev is None else tuple(prev)

    def body(g_ref, w_ref, m_ref, v_ref, *rest):
        go_ref, do_ref, mo_ref, vo_ref = rest[len(prev):]
        g = g_ref[0].astype(F32)
        for dev in range(1, ns):
            g = g + g_ref[dev].astype(F32)
        m_new = ADAM_B1 * m_ref[...] + (1.0 - ADAM_B1) * g
        v_new = ADAM_B2 * v_ref[...] + (1.0 - ADAM_B2) * (g * g)
        m_hat = m_new / (1.0 - ADAM_B1 ** ADAM_STEP)
        v_hat = v_new / (1.0 - ADAM_B2 ** ADAM_STEP)
        go_ref[...] = g
        do_ref[...] = -ADAM_LR * (m_hat / (jnp.sqrt(v_hat) + ADAM_EPS) + ADAM_WD * w_ref[...])
        mo_ref[...] = m_new
        vo_ref[...] = v_new

    row = pl.BlockSpec((None, tr, c), lambda i: (layer, i, 0))
    return pl.pallas_call(
        body, name=name, grid=(r // tr,),
        in_specs=[pl.BlockSpec((ns, tr, c), lambda i: (0, i, 0)), row, row, row] + [ANY] * len(prev),
        out_specs=[row] * 4, out_shape=[jax.ShapeDtypeStruct((nl, r, c), F32)] * 4,
        input_output_aliases={4 + i: i for i in range(len(prev))},
        compiler_params=_params("parallel"),
    )(gsrc, w, m, v, *prev)


def _ffn_forward(tag, r_in, h, get_wu, get_wd, wdw, bdw, seq, loss=None):
    t, d = r_in.shape
    tm = _tile(t, 512)
    wu = get_wu(h)
    dff = wu.shape[0] // 2
    up = _mm(f"{tag}_up", h, wu, grid=(2, t // tm, 1),
             a_spec=pl.BlockSpec((tm, d), lambda j, i, k: (i, 0)),
             b_spec=pl.BlockSpec((dff, d), lambda j, i, k: (j, 0)),
             out_spec=pl.BlockSpec((None, tm, dff), lambda j, i, k: (j, i, 0)),
             out_shape=jax.ShapeDtypeStruct((2, t, dff), BF16), dims=NT, acc_shape=(tm, dff))
    wd = get_wd(up)
    g = _ffn_fwd(f"{tag}_act", up, wdw, bdw, seq)
    row = pl.BlockSpec((tm, d), lambda i, j, k: (i, 0))
    vec = pl.BlockSpec((1, d), lambda i, j, k: (0, 0))
    common = dict(grid=(t // tm, 1, 1), a_spec=pl.BlockSpec((tm, dff), lambda i, j, k: (i, 0)),
                  b_spec=pl.BlockSpec((dff, d), lambda i, j, k: (0, 0)), dims=NN, acc_shape=(tm, d))
    if loss is None:
        out = _mm(f"{tag}_down", g, wd, out_spec=row, out_shape=jax.ShapeDtypeStruct((t, d), F32),
                  extras=(r_in,), extra_specs=(row,), epilogue=lambda acc, ex: ((ex[0][...] + acc,), ()), **common)
    else:
        def head(acc, ex):
            dx, part, dgain = _loss_tile(ex[0][...] + acc, ex[1][...], ex[2][...])
            return (dx,), (part, dgain)

        out = _mm(f"{tag}_down", g, wd, out_spec=[row, pl.BlockSpec((1, 1), lambda i, j, k: (0, 0)), vec],
                  out_shape=[jax.ShapeDtypeStruct((t, d), F32), jax.ShapeDtypeStruct((1, 1), F32),
                             jax.ShapeDtypeStruct((1, d), F32)],
                  extras=(r_in, *loss), extra_specs=(row, row, vec), epilogue=head, n_sums=2, **common)
    return out, (r_in, h, up, g, wu, wd)


def _ffn_backward(tag, dr, saved, gain, wdw, bdw, seq, token=None):
    r_in, h, up, g, wu, wd = saved
    t, d = r_in.shape
    dff = wd.shape[0]
    tm = _tile(t, 512)
    tk = _tile(t, 1024)
    cw = _ctile(dff, 1408)
    nc = dff // cw
    dg = _mm(f"{tag}_dg", dr, wd, grid=(t // tm, 1, 1),
             a_spec=pl.BlockSpec((tm, d), lambda i, j, k: (i, 0)),
             b_spec=pl.BlockSpec((dff, d), lambda i, j, k: (0, 0)),
             out_spec=pl.BlockSpec((tm, dff), lambda i, j, k: (i, 0)),
             out_shape=jax.ShapeDtypeStruct((t, dff), BF16), dims=NT, acc_shape=(tm, dff), token=token)
    dwd = _mm(f"{tag}_dwd", g, dr, grid=(dff // cw, 1, t // tk),
              a_spec=pl.BlockSpec((tk, cw), lambda i, j, k: (k, i)),
              b_spec=pl.BlockSpec((tk, d), lambda i, j, k: (k, 0)),
              out_spec=pl.BlockSpec((cw, d), lambda i, j, k: (i, 0)),
              out_shape=jax.ShapeDtypeStruct((dff, d), BF16), dims=TN, acc_shape=(cw, d))
    dup, dwdw, dbdw = _ffn_bwd(f"{tag}_dact", up, dg, wdw, bdw, seq)
    row = pl.BlockSpec((tm, d), lambda i, j, k: (i, 0))
    vec = pl.BlockSpec((1, d), lambda i, j, k: (0, 0))

    def norm_backward(acc, ex):
        dx, dgain, colsum = _rms_bwd_tile(acc, ex[0][...], ex[1][...], ex[2][...])
        return (dx,), (dgain, colsum)

    dr_in, dgain, colsum = _mm(
        f"{tag}_dh", dup, wu, grid=(t // tm, 1, 2),
        a_spec=pl.BlockSpec((None, tm, dff), lambda i, j, k: (k, i, 0)),
        b_spec=pl.BlockSpec((dff, d), lambda i, j, k: (k, 0)), out_spec=[row, vec, vec],
        out_shape=[jax.ShapeDtypeStruct((t, d), F32)] + [jax.ShapeDtypeStruct((1, d), F32)] * 2,
        dims=NN, acc_shape=(tm, d), extras=(r_in, gain, dr), extra_specs=(row, vec, row),
        epilogue=norm_backward, n_sums=2)
    dwu = _mm(f"{tag}_dwu", dup, h, grid=(2 * nc, 1, t // tk),
              a_spec=pl.BlockSpec((None, tk, cw), lambda i, j, k: (i // nc, k, i % nc)),
              b_spec=pl.BlockSpec((tk, d), lambda i, j, k: (k, 0)),
              out_spec=pl.BlockSpec((cw, d), lambda i, j, k: (i, 0)),
              out_shape=jax.ShapeDtypeStruct((2 * dff, d), BF16), dims=TN, acc_shape=(cw, d))
    return dr_in, dgain, dwu, dwd, dwdw, dbdw, colsum


def _pad_to(vec, n):
    return jnp.pad(vec, (0, n - vec.shape[0]))


def _pack(parts, width):
    flat = jnp.concatenate([p.reshape(-1).astype(F32) for p in parts])
    n = -(-flat.shape[0] // (8 * width)) * (8 * width)
    return _pad_to(flat, n).reshape(n // width, width)


def _unpack(mat, shapes):
    flat = mat.reshape(-1)
    out, off = [], 0
    for s in shapes:
        n = 1
        for dim in s:
            n *= dim
        out.append(flat[off:off + n].reshape(s))
        off += n
    return out


def kernel(x, norm_mix, norm_ffn, conv_w_pw1, conv_b_pw1, conv_w_dw, conv_b_dw, conv_ln_g, conv_ln_b, conv_w_pw2, conv_b_pw2, pool_w, pool_b, pool_scale, ffn_w_up, ffn_w_dw, ffn_b_dw, ffn_w_down, final_norm, loss_target, m_norm_mix, m_norm_ffn, m_conv_w_pw1, m_conv_b_pw1, m_conv_w_dw, m_conv_b_dw, m_conv_ln_g, m_conv_ln_b, m_conv_w_pw2, m_conv_b_pw2, m_pool_w, m_pool_b, m_pool_scale, m_ffn_w_up, m_ffn_w_dw, m_ffn_b_dw, m_ffn_w_down, m_final_norm, v_norm_mix, v_norm_ffn, v_conv_w_pw1, v_conv_b_pw1, v_conv_w_dw, v_conv_b_dw, v_conv_ln_g, v_conv_ln_b, v_conv_w_pw2, v_conv_b_pw2, v_pool_w, v_pool_b, v_pool_scale, v_ffn_w_up, v_ffn_w_dw, v_ffn_b_dw, v_ffn_w_down, v_final_norm):
    bsz, seq, d = x.shape
    t = bsz * seq
    k_taps = conv_w_dw.shape[1]
    cs1 = conv_w_pw1.shape[2]
    dsh = d // N_DEV
    cg = d // N_GROUPS
    cgs = pool_w.shape[2]
    fu = ffn_w_up.shape[2]
    fd = ffn_w_down.shape[1]
    dff = fd * N_DEV
    nb = N_DEV // 2
    kf = ffn_w_dw.shape[1]
    fsh = ffn_w_dw.shape[2]
    my = _lin(_me())
    tm = _tile(t, 512)

    x2 = x.reshape(t, d)
    tgt2 = loss_target.reshape(t, d)

    small_shapes = [(k_taps, dsh), (dsh,), (dsh,), (2, kf, fsh)]
    small_mine = _pack([conv_w_dw[0], pool_b[0], pool_scale[0], ffn_w_dw], LANE)
    big = [conv_w_pw1[0], conv_w_pw2[0], ffn_w_up[0].T, ffn_w_down[0], pool_w[0], ffn_w_up[1].T, ffn_w_down[1]]
    gather = _exchange_start("gather_start", [small_mine] + [w.astype(BF16) for w in big], [False] * 8)
    h0 = _rms_fwd("l0_rms", x2, norm_mix[0:1])
    small_all, w1 = _exchange_wait("gather_wait_w1", gather, [0, 1], h0)
    parts = [_unpack(small_all[dev], small_shapes) for dev in range(N_DEV)]
    wdw = jnp.concatenate([p[0] for p in parts], axis=1)
    pool_b_full = jnp.concatenate([p[1] for p in parts]).reshape(1, d)
    pool_s_full = jnp.concatenate([p[2] for p in parts]).reshape(1, d)
    fwdw = jnp.concatenate([p[3] for p in parts], axis=2)
    fbdw = ffn_b_dw.reshape(2, 1, dff)

    def columns(w):
        return w.transpose(1, 0, 2).reshape(w.shape[1], N_DEV * w.shape[2])

    def column_shards(w):
        return w.reshape(w.shape[0], N_DEV, w.shape[1] // N_DEV).transpose(1, 0, 2)

    w1 = columns(w1)
    a = _mm("l0_pw1", h0, w1, grid=(2, t // tm, 1),
            a_spec=pl.BlockSpec((tm, d), lambda j, i, k: (i, 0)),
            b_spec=pl.BlockSpec((d, d), lambda j, i, k: (0, j)),
            out_spec=pl.BlockSpec((None, tm, d), lambda j, i, k: (j, i, 0)),
            out_shape=jax.ShapeDtypeStruct((2, t, d), BF16), dims=NN, acc_shape=(tm, d),
            extras=(conv_b_pw1,), extra_specs=(pl.BlockSpec((1, d), lambda j, i, k: (0, j)),),
            epilogue=lambda acc, ex: ((acc + ex[0][...],), ()))
    (w2,) = _exchange_wait("gather_wait_w2", gather, [2], a)
    w2 = w2.reshape(d, d)
    v = _conv_fwd("l0_conv", a, wdw, conv_b_dw, seq)
    row = pl.BlockSpec((tm, d), lambda i, j, k: (i, 0))
    vec = pl.BlockSpec((1, d), lambda i, j, k: (0, 0))
    square = pl.BlockSpec((d, d), lambda i, j, k: (0, 0))

    def ln_silu(v_blk, ex):
        s_blk = _ln_silu_tile(v_blk, ex[0][...], ex[1][...]).astype(BF16)
        return s_blk, s_blk

    def residual_and_norm(acc, ex):
        r_blk = ex[3][...] + (acc + ex[2][...])
        return (r_blk, _rms(r_blk, ex[4][...])), ()

    r1, h1, s = _mm("l0_pw2", v, w2, grid=(t // tm, 1, 1), a_spec=row, b_spec=square, out_spec=[row, row, row],
                    out_shape=[jax.ShapeDtypeStruct((t, d), F32), jax.ShapeDtypeStruct((t, d), BF16),
                               jax.ShapeDtypeStruct((t, d), BF16)],
                    dims=NN, acc_shape=(tm, d), extras=(conv_ln_g, conv_ln_b, conv_b_pw2, x2, norm_ffn[0:1]),
                    extra_specs=(vec, vec, vec, row, vec), prologue=ln_silu, epilogue=residual_and_norm)

    def up_getter(name, idx):
        return lambda after: _exchange_wait(name, gather, [idx], after)[0].reshape(2 * dff, d)

    def down_getter(name, idx):
        return lambda after: _exchange_wait(name, gather, [idx], after)[0].reshape(dff, d)

    r2, ffn0_saved = _ffn_forward("f0", r1, h1, up_getter("gather_wait_wu0", 3),
                                  down_getter("gather_wait_wd0", 4), fwdw[0], fbdw[0], seq)
    (wp,) = _exchange_wait("gather_wait_wp", gather, [5], r2)
    wp = wp.transpose(1, 0, 2, 3).reshape(N_GROUPS, cg, cg)
    pooled = _pool_fwd("l1_pool", r2, norm_mix[1:2], seq)
    r3 = _mm("l1_mix", pooled, wp, grid=(t // tm, N_GROUPS, 1),
             a_spec=pl.BlockSpec((tm, cg), lambda i, j, k: (i, j)),
             b_spec=pl.BlockSpec((None, cg, cg), lambda i, j, k: (j, 0, 0)),
             out_spec=pl.BlockSpec((tm, cg), lambda i, j, k: (i, j)),
             out_shape=jax.ShapeDtypeStruct((t, d), F32), dims=NN, acc_shape=(tm, cg),
             extras=(pool_s_full, pool_b_full, r2),
             extra_specs=(pl.BlockSpec((1, cg), lambda i, j, k: (0, j)), pl.BlockSpec((1, cg), lambda i, j, k: (0, j)),
                          pl.BlockSpec((tm, cg), lambda i, j, k: (i, j))),
             epilogue=lambda acc, ex: ((ex[2][...] + ex[0][...] * (acc + ex[1][...]),), ()))
    h3 = _rms_fwd("f1_rms", r3, norm_ffn[1:2])
    (dr4, loss_part, dfinal), ffn1_saved = _ffn_forward(
        "f1", r3, h3, up_getter("gather_wait_wu1", 6), down_getter("gather_wait_wd1", 7), fwdw[1], fbdw[1], seq,
        loss=(tgt2, final_norm.reshape(1, d)))

    dr3, dnf1, dwu1, dwd1, dfw1, dfb1, _ = _ffn_backward("f1", dr4, ffn1_saved, norm_ffn[1:2], fwdw[1], fbdw[1], seq)
    scatter_a = _exchange_start("scatter_f1_start", [dwu1.reshape(N_DEV, fu, d), dwd1.reshape(N_DEV, fd, d)],
                                [True, True])
    dpooled, dmixed, dpool_s, dpool_b = _pool_bwd_mm("l1_dmix", pooled, wp, dr3,
                                                     pool_s_full + scatter_a["token"][0:1, 0:1], pool_b_full)
    dwp = _mm("l1_dwp", pooled, dmixed, grid=(N_GROUPS, 1, t // tm),
              a_spec=pl.BlockSpec((tm, cg), lambda i, j, k: (k, i)),
              b_spec=pl.BlockSpec((tm, cg), lambda i, j, k: (k, i)),
              out_spec=pl.BlockSpec((None, cg, cg), lambda i, j, k: (i, 0, 0)),
              out_shape=jax.ShapeDtypeStruct((N_GROUPS, cg, cg), BF16), dims=TN, acc_shape=(cg, cg))
    dr2, dnm1 = _pool_bwd("l1_dpool", dpooled, r2, norm_mix[1:2], dr3, seq)
    dr1, dnf0, dwu0, dwd0, dfw0, dfb0, db2 = _ffn_backward("f0", dr2, ffn0_saved, norm_ffn[0:1], fwdw[0], fbdw[0], seq)
    dwp_b = dwp.reshape(N_GROUPS, N_DEV, cgs, cg).transpose(1, 0, 2, 3)
    scatter_b = _exchange_start("scatter_f0_start", [dwu0.reshape(N_DEV, fu, d), dwd0.reshape(N_DEV, fd, d), dwp_b],
                                [True] * 3)
    def ln_silu_backward(acc, ex):
        dv_blk, dgain, dbias, colsum = _ln_silu_bwd_tile(acc, ex[0][...], ex[1][...], ex[2][...])
        return (dv_blk,), (dgain, dbias, colsum)

    dv, dlg, dlb, dbdw = _mm("l0_ds", dr1, w2, grid=(t // tm, 1, 1), a_spec=row, b_spec=square,
                             out_spec=[row, vec, vec, vec],
                             out_shape=[jax.ShapeDtypeStruct((t, d), F32)] + [jax.ShapeDtypeStruct((1, d), F32)] * 3,
                             dims=NT, acc_shape=(tm, d), extras=(v, conv_ln_g, conv_ln_b), extra_specs=(row, vec, vec),
                             epilogue=ln_silu_backward, n_sums=3, token=scatter_b["token"])
    dw2 = _mm("l0_dw2", s, dr1, grid=(1, 1, t // tm),
              a_spec=pl.BlockSpec((tm, d), lambda i, j, k: (k, 0)),
              b_spec=pl.BlockSpec((tm, d), lambda i, j, k: (k, 0)),
              out_spec=pl.BlockSpec((d, d), lambda i, j, k: (0, 0)),
              out_shape=jax.ShapeDtypeStruct((d, d), BF16), dims=TN, acc_shape=(d, d))
    da, dwdw, db1 = _conv_bwd("l0_dconv", a, dv, wdw, seq)
    tk = _tile(t, 1024)
    dw1 = _mm("l0_dw1", h0, da, grid=(1, 2, t // tk),
              a_spec=pl.BlockSpec((tk, d), lambda i, j, k: (k, 0)),
              b_spec=pl.BlockSpec((None, tk, d), lambda i, j, k: (j, k, 0)),
              out_spec=pl.BlockSpec((d, d), lambda i, j, k: (0, j)),
              out_shape=jax.ShapeDtypeStruct((d, 2 * d), BF16), dims=TN, acc_shape=(d, d))
    scatter_c = _exchange_start("scatter_l0_start", [column_shards(dw1), dw2.reshape(N_DEV, d // N_DEV, d)],
                                [True, True])
    def norm_backward(acc, ex):
        dx_blk, dgain, colsum = _rms_bwd_tile(acc, ex[0][...], ex[1][...], ex[2][...])
        return (dx_blk,), (dgain, colsum)

    dx, dnm0, _ = _mm("l0_dh", da, w1, grid=(t // tm, 1, 2),
                      a_spec=pl.BlockSpec((None, tm, d), lambda i, j, k: (k, i, 0)),
                      b_spec=pl.BlockSpec((d, d), lambda i, j, k: (0, k)), out_spec=[row, vec, vec],
                      out_shape=[jax.ShapeDtypeStruct((t, d), F32)] + [jax.ShapeDtypeStruct((1, d), F32)] * 2,
                      dims=NT, acc_shape=(tm, d), extras=(x2, norm_mix[0:1], dr1), extra_specs=(row, vec, row),
                      epilogue=norm_backward, n_sums=2, token=scatter_c["token"])

    dffn_w = jnp.stack([dfw0, dfw1])
    dffn_b = jnp.stack([dfb0, dfb1]).reshape(2, dff)
    small_parts = [loss_part, jnp.concatenate([dnm0, dnm1]), jnp.concatenate([dnf0, dnf1]), db1, dwdw, dbdw, dlg, dlb,
                   db2, dpool_b, dpool_s, dffn_w, dffn_b, dfinal]
    small_part_shapes = [(1,), (2, d), (2, d), (1, 2 * d), (k_taps, d), (1, d), (1, d), (1, d), (1, d), (1, d), (1, d),
                         (2, kf, dff), (2, dff), (d,)]
    packed = _pack(small_parts, 8 * LANE)
    gather_small = _exchange_start("gather_small_start", [packed], [False])

    def big_update(name, recv, w, m, v, layer=0, prev=None):
        shape = w.shape
        c = recv.shape[-1]
        rows = recv.size // (N_DEV * c)
        nl = w.size // (rows * c)
        outs = _adamw(name, recv.reshape(N_DEV, rows, c), w.reshape(nl, rows, c), m.reshape(nl, rows, c),
                      v.reshape(nl, rows, c), layer, prev)
        return outs, [o.reshape(shape) for o in outs]

    wu_t = [p.transpose(0, 2, 1) for p in (ffn_w_up, m_ffn_w_up, v_ffn_w_up)]
    g_wu1, g_wd1 = _exchange_wait("scatter_f1_wait", scatter_a, [0, 1], gather_small["token"])
    raw_wu, _ = big_update("adam_wu1", g_wu1, *wu_t, 1)
    raw_wd, _ = big_update("adam_wd1", g_wd1, ffn_w_down, m_ffn_w_down, v_ffn_w_down, 1)
    g_wu0, g_wd0, g_wp = _exchange_wait("scatter_f0_wait", scatter_b, [0, 1, 2], raw_wd[0])
    _, u_wu = big_update("adam_wu0", g_wu0, *wu_t, 0, raw_wu)
    u_wu = [o.transpose(0, 2, 1) for o in u_wu]
    _, u_wd = big_update("adam_wd0", g_wd0, ffn_w_down, m_ffn_w_down, v_ffn_w_down, 0, raw_wd)
    _, u_wp = big_update("adam_wp", g_wp, pool_w, m_pool_w, v_pool_w)
    g_w1, g_w2 = _exchange_wait("scatter_l0_wait", scatter_c, [0, 1], u_wp[0])
    _, u_w1 = big_update("adam_w1", g_w1, conv_w_pw1, m_conv_w_pw1, v_conv_w_pw1)
    _, u_w2 = big_update("adam_w2", g_w2, conv_w_pw2, m_conv_w_pw2, v_conv_w_pw2)
    (all_small,) = _exchange_wait("gather_small_wait", gather_small, [0], u_w2[0])
    summed = _sum_rows("sum_small_grads", all_small)
    (loss_v, g_nm, g_nf, g_b1, g_wdw, g_bdw, g_lg, g_lb, g_b2, g_pb, g_ps, g_fw, g_fb,
     g_fin) = _unpack(summed, small_part_shapes)
    loss = loss_v[0]
    g_wdw_mine = lax.dynamic_slice_in_dim(g_wdw, my * dsh, dsh, axis=1)[None]
    g_pb_mine = lax.dynamic_slice_in_dim(g_pb, my * dsh, dsh, axis=1)
    g_ps_mine = lax.dynamic_slice_in_dim(g_ps, my * dsh, dsh, axis=1)
    g_fw_mine = lax.dynamic_slice_in_dim(g_fw, my * fsh, fsh, axis=2)

    small_g =[g_nm, g_nf, g_b1, g_wdw_mine, g_bdw, g_lg, g_lb, g_b2, g_pb_mine, g_ps_mine, g_fw_mine, g_fb, g_fin]
    small_w = [norm_mix, norm_ffn, conv_b_pw1, conv_w_dw, conv_b_dw, conv_ln_g, conv_ln_b, conv_b_pw2, pool_b,
               pool_scale, ffn_w_dw, ffn_b_dw, final_norm]
    small_m = [m_norm_mix, m_norm_ffn, m_conv_b_pw1, m_conv_w_dw, m_conv_b_dw, m_conv_ln_g, m_conv_ln_b,
               m_conv_b_pw2, m_pool_b, m_pool_scale, m_ffn_w_dw, m_ffn_b_dw, m_final_norm]
    small_v = [v_norm_mix, v_norm_ffn, v_conv_b_pw1, v_conv_w_dw, v_conv_b_dw, v_conv_ln_g, v_conv_ln_b,
               v_conv_b_pw2, v_pool_b, v_pool_scale, v_ffn_w_dw, v_ffn_b_dw, v_final_norm]
    shapes = [w.shape for w in small_w]
    outs = _adamw("adam_small", _pack(small_g, 8 * LANE)[None], _pack(small_w, 8 * LANE)[None],
                  _pack(small_m, 8 * LANE)[None], _pack(small_v, 8 * LANE)[None])
    sg, sd, sm, sv = [_unpack(o, shapes) for o in outs]

    def leaf(kind):
        (nm, nf, b1, wdw_, bdw_, lg, lb, b2, pb, ps, fw, fb, fin) = (sg, sd, sm, sv)[kind]
        return [nm, nf, u_w1[kind], b1, wdw_, bdw_, lg, lb, u_w2[kind], b2, u_wp[kind], pb, ps, u_wu[kind], fw, fb,
                u_wd[kind], fin]

    return (loss, dx.reshape(bsz, seq, d), *leaf(0), *leaf(1), *leaf(2), *leaf(3))
```

```python
import functools

import jax
import jax.numpy as jnp
from jax import lax
from jax.experimental import pallas as pl
from jax.experimental.pallas import tpu as pltpu

F32 = jnp.float32
BF16 = jnp.bfloat16
MESH = pl.DeviceIdType.MESH
HBM = pl.BlockSpec(memory_space=pltpu.HBM)

N_DEV = 8
RMS_EPS = 1e-6
LN_EPS = 1e-5
POOL_WINDOWS = (2, 4, 8, 16)
N_GROUPS = len(POOL_WINDOWS)
ADAM_LR = 0.001
ADAM_B1 = 0.9
ADAM_B2 = 0.999
ADAM_EPS = 1e-08
ADAM_WD = 0.01
ADAM_STEP = 10

LANE = 128
HALO = 32
HALO16 = 16
VMEM_LIMIT = 56 * 1024 * 1024


def _params(*sem):
    return pltpu.CompilerParams(dimension_semantics=sem if sem else None, vmem_limit_bytes=VMEM_LIMIT)


def _tile(n, pref):
    for t in range(min(pref, n), 15, -1):
        if n % t == 0 and t % 16 == 0:
            return t
    return n


def _sigmoid(z):
    return 1.0 / (1.0 + jnp.exp(-z))


def _me():
    return lax.axis_index("x"), lax.axis_index("y"), lax.axis_index("c")


def _flip(pos, m):
    x, y, c = pos
    return ((1 - x) if m & 4 else x, (1 - y) if m & 2 else y, (1 - c) if m & 1 else c)


def _lin(pos):
    return 4 * pos[0] + 2 * pos[1] + pos[2]


SEM = pl.BlockSpec(memory_space=pltpu.SEMAPHORE)
ANY = pl.BlockSpec(memory_space=pl.ANY)
EFFECT = pltpu.SideEffectType.DATAFLOW_SIDE_EFFECTING


def _exchange_copies(srcs, lands, send_sems, recv_sems, modes, which, starting):
    me = _me()
    my = _lin(me)
    out = []
    for pos, a in enumerate(which):
        src, land = srcs[pos], lands[pos]

        def block(pid, src=src, a=a):
            return src.at[pid] if modes[a] else src

        local = pltpu.make_async_copy(block(my), land.at[my], send_sems.at[a * N_DEV])
        remote = []
        for m in range(1, N_DEV):
            peer = _flip(me, m)
            pid = _lin(peer)
            sems = dict(send_sem=send_sems.at[a * N_DEV + m], recv_sem=recv_sems.at[a * N_DEV + m],
                        device_id=peer, device_id_type=MESH)
            if starting:
                remote.append(pltpu.make_async_remote_copy(src_ref=block(pid), dst_ref=land.at[my], **sems))
            else:
                remote.append((pltpu.make_async_remote_copy(src_ref=block(pid), dst_ref=land.at[my], **sems),
                               pltpu.make_async_remote_copy(src_ref=block(pid), dst_ref=land.at[pid], **sems)))
        out.append((local, remote))
    return out


def _exchange_start(name, arrs, modes):
    n = len(arrs)
    blocks = [a.shape[1:] if md else a.shape for a, md in zip(arrs, modes)]

    def body(*refs):
        srcs, lands = refs[:n], refs[n:2 * n]
        send_sems, recv_sems = refs[2 * n], refs[2 * n + 1]
        token = refs[-1]
        for local, remote in _exchange_copies(srcs, lands, send_sems, recv_sems, modes, list(range(n)), True):
            local.start()
            for send in remote:
                send.start()
        token[...] = jnp.zeros_like(token)

    lands = [lax.empty((N_DEV,) + tuple(b), a.dtype) for a, b in zip(arrs, blocks)]
    outs = pl.pallas_call(
        body, name=name,
        out_shape=(pltpu.SemaphoreType.DMA((n * N_DEV,)), pltpu.SemaphoreType.DMA((n * N_DEV,)),
                   *[pltpu.HBM(a.shape, a.dtype) for a in arrs], *[pltpu.HBM(l.shape, l.dtype) for l in lands],
                   jax.ShapeDtypeStruct((8, LANE), F32)),
        in_specs=[HBM] * (2 * n),
        out_specs=(SEM, SEM, *[HBM] * (2 * n), pl.BlockSpec(memory_space=pltpu.VMEM)),
        input_output_aliases={i: 2 + i for i in range(2 * n)},
        compiler_params=pltpu.CompilerParams(has_side_effects=EFFECT),
    )(*[pltpu.with_memory_space_constraint(a, pltpu.HBM) for a in arrs],
      *[pltpu.with_memory_space_constraint(l, pltpu.HBM) for l in lands])
    return dict(send=outs[0], recv=outs[1], srcs=list(outs[2:2 + n]), lands=list(outs[2 + n:2 + 2 * n]),
                modes=modes, token=outs[-1])


def _exchange_wait(name, handle, which, after):
    k = len(which)
    modes = handle["modes"]

    def body(*refs):
        srcs, lands = refs[:k], refs[k:2 * k]
        send_sems, recv_sems = refs[2 * k], refs[2 * k + 1]
        for local, remote in _exchange_copies(srcs, lands, send_sems, recv_sems, modes, which, False):
            local.wait()
            for send, arrival in remote:
                send.wait_send()
                arrival.wait_recv()

    srcs = [handle["srcs"][a] for a in which]
    lands = [handle["lands"][a] for a in which]
    outs = pl.pallas_call(
        body, name=name,
        out_shape=tuple(pltpu.HBM(x.shape, x.dtype) for x in srcs + lands),
        in_specs=[HBM] * (2 * k) + [SEM, SEM, ANY], out_specs=tuple([HBM] * (2 * k)),
        input_output_aliases={i: i for i in range(2 * k)},
        compiler_params=pltpu.CompilerParams(has_side_effects=EFFECT),
    )(*srcs, *lands, handle["send"], handle["recv"], after)
    return list(outs[k:])


def _mm(name, a, b, *, grid, a_spec, b_spec, out_spec, out_shape, dims, acc_shape, extras=(), extra_specs=(),
        epilogue=None, token=None, prologue=None, n_sums=0):
    nk = grid[2]
    ne = len(extras)
    deps = () if token is None else (token,)
    dep_specs = [pl.BlockSpec((8, LANE), lambda i, j, k: (0, 0))] * len(deps)
    n_out = len(out_shape) if isinstance(out_shape, (list, tuple)) else 1
    has_sums = n_sums > 0

    def body(a_ref, b_ref, *rest):
        ex, o_refs, acc_ref = rest[:ne], rest[ne + len(deps):ne + len(deps) + n_out], rest[ne + len(deps) + n_out]
        k = pl.program_id(2)
        a_blk, saved = a_ref[...], None
        if prologue is not None:
            a_blk, saved = prologue(a_blk, ex)
        part = lax.dot_general(a_blk.astype(BF16), b_ref[...].astype(BF16), (dims, ((), ())),
                               preferred_element_type=F32)

        def finish(r):
            tiles, sums = ((r,), ()) if epilogue is None else epilogue(r, ex)
            if saved is not None:
                tiles = tiles + (saved,)
            for o_ref, val in zip(o_refs, tiles):
                o_ref[...] = val.astype(o_ref.dtype)
            if sums:
                @pl.when((pl.program_id(0) == 0) & (pl.program_id(1) == 0))
                def _():
                    for o_ref in o_refs[len(tiles):]:
                        o_ref[...] = jnp.zeros_like(o_ref)

                for o_ref, val in zip(o_refs[len(tiles):], sums):
                    o_ref[...] += val

        if nk == 1:
            finish(part)
            return

        @pl.when(k == 0)
        def _():
            acc_ref[...] = part

        @pl.when((k > 0) & (k < nk - 1))
        def _():
            acc_ref[...] += part

        @pl.when(k == nk - 1)
        def _():
            finish(acc_ref[...] + part)

    return pl.pallas_call(
        body, name=name, grid=grid, in_specs=[a_spec, b_spec, *extra_specs, *dep_specs], out_specs=out_spec,
        out_shape=out_shape, scratch_shapes=[pltpu.VMEM(acc_shape if nk > 1 else (8, LANE), F32)],
        compiler_params=_params(*(("arbitrary",) * 3 if has_sums else ("parallel", "parallel", "arbitrary"))),
    )(a, b, *extras, *deps)


def _rms(x, gain):
    return x * lax.rsqrt(jnp.mean(x * x, axis=-1, keepdims=True) + RMS_EPS) * gain


def _rms_bwd_tile(dh, x, gain, dres):
    rstd = lax.rsqrt(jnp.mean(x * x, axis=-1, keepdims=True) + RMS_EPS)
    xhat = x * rstd
    dxhat = dh * gain
    dx = dres + rstd * (dxhat - xhat * jnp.mean(dxhat * xhat, axis=-1, keepdims=True))
    return dx, jnp.sum(dh * xhat, axis=0, keepdims=True), jnp.sum(dx, axis=0, keepdims=True)


def _ln_silu_tile(v, g, b):
    mu = jnp.mean(v, axis=-1, keepdims=True)
    cen = v - mu
    z = cen * lax.rsqrt(jnp.mean(cen * cen, axis=-1, keepdims=True) + LN_EPS) * g + b
    return z * _sigmoid(z)


def _ln_silu_bwd_tile(ds, v, g, b):
    mu = jnp.mean(v, axis=-1, keepdims=True)
    cen = v - mu
    rstd = lax.rsqrt(jnp.mean(cen * cen, axis=-1, keepdims=True) + LN_EPS)
    y = cen * rstd
    z = y * g + b
    sig = _sigmoid(z)
    dz = ds * sig * (1.0 + z * (1.0 - sig))
    dy = dz * g
    dv = rstd * (dy - jnp.mean(dy, axis=-1, keepdims=True) - y * jnp.mean(dy * y, axis=-1, keepdims=True))
    return (dv, jnp.sum(dz * y, axis=0, keepdims=True), jnp.sum(dz, axis=0, keepdims=True),
            jnp.sum(dv, axis=0, keepdims=True))


def _loss_tile(x, tgt, gain):
    d = x.shape[-1]
    rstd = lax.rsqrt(jnp.mean(x * x, axis=-1, keepdims=True) + RMS_EPS)
    xhat = x * rstd
    err = xhat * gain - tgt
    dy = err / d
    dxhat = dy * gain
    dx = rstd * (dxhat - xhat * jnp.mean(dxhat * xhat, axis=-1, keepdims=True))
    loss = 0.5 * jnp.sum(jnp.mean(err * err, axis=-1, keepdims=True), axis=0, keepdims=True)
    return dx, loss, jnp.sum(dy * xhat, axis=0, keepdims=True)


NN = ((1,), (0,))
NT = ((1,), (1,))
TN = ((0,), (0,))


def _rms_fwd(name, x, gain):
    t, d = x.shape
    tr = _tile(t, 512)

    def body(x_ref, g_ref, h_ref):
        xv = x_ref[...]
        rstd = lax.rsqrt(jnp.mean(xv * xv, axis=-1, keepdims=True) + RMS_EPS)
        h_ref[...] = (xv * rstd * g_ref[...]).astype(BF16)

    return pl.pallas_call(
        body, name=name, grid=(t // tr,),
        in_specs=[pl.BlockSpec((tr, d), lambda i: (i, 0)), pl.BlockSpec((1, d), lambda i: (0, 0))],
        out_specs=pl.BlockSpec((tr, d), lambda i: (i, 0)),
        out_shape=jax.ShapeDtypeStruct((t, d), BF16), compiler_params=_params("parallel"),
    )(x, gain)


def _conv_tiles(t, seq):
    ts = _tile(seq, 512)
    return ts, seq // ts, _tile(ts, 64)


def _conv_fwd(name, a, w, b, seq):
    _, t, d = a.shape
    k_taps = w.shape[0]
    ts, tps, rc = _conv_tiles(t, seq)
    hb = ts // HALO

    def body(cur_ref, prev_ref, w_ref, b_ref, v_ref, upad):
        i = pl.program_id(1)
        first = (i % tps) == 0
        pv = prev_ref[0].astype(F32)
        pg = prev_ref[1].astype(F32)
        upad[0:HALO, :] = jnp.where(first, 0.0, pv * _sigmoid(pg))
        upad[HALO:HALO + ts, :] = cur_ref[0].astype(F32) * _sigmoid(cur_ref[1].astype(F32))
        wv = w_ref[...]
        bias = jnp.broadcast_to(b_ref[...], (rc, LANE))
        for r0 in range(0, ts, rc):
            acc = bias
            for k in range(k_taps):
                acc = acc + wv[k:k + 1, :] * upad[pl.ds(HALO - (k_taps - 1) + k + r0, rc), :]
            v_ref[pl.ds(r0, rc), :] = acc

    return pl.pallas_call(
        body, name=name, grid=(d // LANE, t // ts),
        in_specs=[pl.BlockSpec((2, ts, LANE), lambda c, i: (0, i, c)),
                  pl.BlockSpec((2, HALO, LANE), lambda c, i: (0, jnp.maximum(i * hb - 1, 0), c)),
                  pl.BlockSpec((k_taps, LANE), lambda c, i: (0, c)),
                  pl.BlockSpec((1, LANE), lambda c, i: (0, c))],
        out_specs=pl.BlockSpec((ts, LANE), lambda c, i: (i, c)),
        out_shape=jax.ShapeDtypeStruct((t, d), F32),
        scratch_shapes=[pltpu.VMEM((HALO + ts, LANE), F32)],
        compiler_params=_params("parallel", "parallel"),
    )(a, a, w, b)


def _conv_bwd(name, a, dv, w, seq):
    _, t, d = a.shape
    k_taps = w.shape[0]
    ts, tps, rc = _conv_tiles(t, seq)
    hb = ts // HALO
    nhb = t // HALO

    def body(cur_ref, prev_ref, dv_ref, ndv_ref, w_ref, da_ref, dw_ref, dbp_ref, upad, dvpad, dwrows):
        i = pl.program_id(1)
        first = (i % tps) == 0
        last = (i % tps) == tps - 1
        pv = prev_ref[0].astype(F32)
        pg = prev_ref[1].astype(F32)
        upad[0:HALO, :] = jnp.where(first, 0.0, pv * _sigmoid(pg))
        upad[HALO:HALO + ts, :] = cur_ref[0].astype(F32) * _sigmoid(cur_ref[1].astype(F32))
        dvpad[0:ts, :] = dv_ref[...]
        dvpad[ts:ts + HALO, :] = jnp.where(last, 0.0, ndv_ref[...])
        wv = w_ref[...]

        @pl.when(i == 0)
        def _():
            dw_ref[...] = jnp.zeros_like(dw_ref)
            dbp_ref[...] = jnp.zeros_like(dbp_ref)

        sv = jnp.zeros((1, LANE), F32)
        sg = jnp.zeros((1, LANE), F32)
        for r0 in range(0, ts, rc):
            du = jnp.zeros((rc, LANE), F32)
            for k in range(k_taps):
                du = du + wv[k:k + 1, :] * dvpad[pl.ds(r0 + (k_taps - 1) - k, rc), :]
            av = cur_ref[0, pl.ds(r0, rc), :].astype(F32)
            sig = _sigmoid(cur_ref[1, pl.ds(r0, rc), :].astype(F32))
            dval = du * sig
            dgate = du * av * sig * (1.0 - sig)
            da_ref[0, pl.ds(r0, rc), :] = dval.astype(BF16)
            da_ref[1, pl.ds(r0, rc), :] = dgate.astype(BF16)
            sv = sv + jnp.sum(dval, axis=0, keepdims=True)
            sg = sg + jnp.sum(dgate, axis=0, keepdims=True)
        dbp_ref[0] += sv
        dbp_ref[1] += sg

        for k in range(k_taps):
            acc = jnp.zeros((rc, LANE), F32)
            for r0 in range(0, ts, rc):
                acc = acc + dvpad[pl.ds(r0, rc), :] * upad[pl.ds(HALO - (k_taps - 1) + k + r0, rc), :]
            dwrows[k:k + 1, :] = jnp.sum(acc, axis=0, keepdims=True)
        dw_ref[...] += dwrows[0:k_taps, :]

    return pl.pallas_call(
        body, name=name, grid=(d // LANE, t // ts),
        in_specs=[pl.BlockSpec((2, ts, LANE), lambda c, i: (0, i, c)),
                  pl.BlockSpec((2, HALO, LANE), lambda c, i: (0, jnp.maximum(i * hb - 1, 0), c)),
                  pl.BlockSpec((ts, LANE), lambda c, i: (i, c)),
                  pl.BlockSpec((HALO, LANE), lambda c, i: (jnp.minimum((i + 1) * hb, nhb - 1), c)),
                  pl.BlockSpec((k_taps, LANE), lambda c, i: (0, c))],
        out_specs=[pl.BlockSpec((2, ts, LANE), lambda c, i: (0, i, c)),
                   pl.BlockSpec((k_taps, LANE), lambda c, i: (0, c)),
                   pl.BlockSpec((2, 1, LANE), lambda c, i: (0, 0, c))],
        out_shape=[jax.ShapeDtypeStruct((2, t, d), BF16), jax.ShapeDtypeStruct((k_taps, d), F32),
                   jax.ShapeDtypeStruct((2, 1, d), F32)],
        scratch_shapes=[pltpu.VMEM((HALO + ts, LANE), F32), pltpu.VMEM((ts + HALO, LANE), F32),
                        pltpu.VMEM((HALO, LANE), F32)],
        compiler_params=_params("parallel", "arbitrary"),
    )(a, a, dv, dv, w)


def _pool_mix_fwd(name, x, gain, wp, scale, bias, next_gain, seq):
    t, d = x.shape
    ts = _tile(seq, 256)
    tps = seq // ts
    hb = ts // HALO
    cg = d // N_GROUPS

    def body(cur_ref, prev_ref, g_ref, w_ref, s_ref, b_ref, ng_ref, p_ref, r_ref, h_ref, hpad):
        i = pl.program_id(0)
        first = (i % tps) == 0
        g = g_ref[...]
        hpad[0:HALO, :] = jnp.where(first, 0.0, _rms(prev_ref[...], g))
        hpad[HALO:HALO + ts, :] = _rms(cur_ref[...], g)
        pos = (i % tps) * ts + lax.broadcasted_iota(jnp.int32, (ts, 1), 0)
        for gi, win in enumerate(POOL_WINDOWS):
            sl = slice(gi * cg, (gi + 1) * cg)
            own = hpad[HALO:HALO + ts, sl]
            acc = own
            for j in range(1, win):
                acc = acc + hpad[HALO - j:HALO - j + ts, sl]
            cnt = jnp.minimum(pos + 1, win).astype(F32)
            pooled = (acc / cnt - own).astype(BF16)
            p_ref[:, sl] = pooled
            mixed = jnp.dot(pooled, w_ref[gi], preferred_element_type=F32)
            r_ref[:, sl] = cur_ref[:, sl] + s_ref[:, sl] * (mixed + b_ref[:, sl])
        h_ref[...] = _rms(r_ref[...], ng_ref[...]).astype(BF16)

    row = pl.BlockSpec((ts, d), lambda i: (i, 0))
    vec = pl.BlockSpec((1, d), lambda i: (0, 0))
    return pl.pallas_call(
        body, name=name, grid=(t // ts,),
        in_specs=[row, pl.BlockSpec((HALO, d), lambda i: (jnp.maximum(i * hb - 1, 0), 0)), vec,
                  pl.BlockSpec((N_GROUPS, cg, cg), lambda i: (0, 0, 0)), vec, vec, vec],
        out_specs=[row, row, row],
        out_shape=[jax.ShapeDtypeStruct((t, d), BF16), jax.ShapeDtypeStruct((t, d), F32),
                   jax.ShapeDtypeStruct((t, d), BF16)],
        scratch_shapes=[pltpu.VMEM((HALO + ts, d), F32)],
        compiler_params=_params("parallel"),
    )(x, x, gain, wp, scale, bias, next_gain)


def _pool_mix_bwd(name, pooled, wp, dr, x, gain, scale, bias, seq):
    t, d = x.shape
    ts = _tile(seq, 256)
    tps = seq // ts
    hb = ts // HALO
    nhb = t // HALO
    cg = d // N_GROUPS

    def body(p_ref, w_ref, dr_ref, ndr_ref, x_ref, g_ref, s_ref, b_ref, dx_ref, dmx_ref, ds_ref, db_ref, dg_ref,
             qpad, dh):
        i = pl.program_id(0)
        last = (i % tps) == tps - 1
        pos = (i % tps) * ts + lax.broadcasted_iota(jnp.int32, (ts, 1), 0)

        @pl.when(i == 0)
        def _():
            ds_ref[...] = jnp.zeros_like(ds_ref)
            db_ref[...] = jnp.zeros_like(db_ref)
            dg_ref[...] = jnp.zeros_like(dg_ref)

        for gi, win in enumerate(POOL_WINDOWS):
            sl = slice(gi * cg, (gi + 1) * cg)
            wv = w_ref[gi]
            sc = s_ref[:, sl]
            drv = dr_ref[:, sl]
            dmx = drv * sc
            dmx16 = dmx.astype(BF16)
            dmx_ref[:, sl] = dmx16
            mixed = jnp.dot(p_ref[:, sl], wv, preferred_element_type=F32)
            ds_ref[:, sl] += jnp.sum(drv * (mixed + b_ref[:, sl]), axis=0, keepdims=True)
            db_ref[:, sl] += jnp.sum(dmx, axis=0, keepdims=True)
            cur = lax.dot_general(dmx16, wv, (NT, ((), ())), preferred_element_type=F32)
            nxt = lax.dot_general((ndr_ref[:, sl] * sc).astype(BF16), wv, (NT, ((), ())),
                                  preferred_element_type=F32)
            qpad[0:ts, sl] = cur / jnp.minimum(pos + 1, win).astype(F32)
            qpad[ts:ts + HALO, sl] = jnp.where(last, 0.0, nxt / float(win))
            acc = -cur
            for j in range(win):
                acc = acc + qpad[j:j + ts, sl]
            dh[:, sl] = acc
        dx, dgain, _ = _rms_bwd_tile(dh[...], x_ref[...], g_ref[...], dr_ref[...])
        dx_ref[...] = dx
        dg_ref[...] += dgain

    row = pl.BlockSpec((ts, d), lambda i: (i, 0))
    vec = pl.BlockSpec((1, d), lambda i: (0, 0))
    return pl.pallas_call(
        body, name=name, grid=(t // ts,),
        in_specs=[row, pl.BlockSpec((N_GROUPS, cg, cg), lambda i: (0, 0, 0)), row,
                  pl.BlockSpec((HALO, d), lambda i: (jnp.minimum((i + 1) * hb, nhb - 1), 0)), row, vec, vec, vec],
        out_specs=[row, row, vec, vec, vec],
        out_shape=[jax.ShapeDtypeStruct((t, d), F32), jax.ShapeDtypeStruct((t, d), BF16)]
        + [jax.ShapeDtypeStruct((1, d), F32)] * 3,
        scratch_shapes=[pltpu.VMEM((ts + HALO, d), F32), pltpu.VMEM((ts, d), F32)],
        compiler_params=_params("arbitrary"),
    )(pooled, wp, dr, dr, x, gain, scale, bias)


def _ctile(n, pref):
    return max(c for c in range(LANE, min(pref, n) + 1, LANE) if n % c == 0)


FFN_COLS = 1408
FFN_ROWS = 32


def _ffn_fwd(name, up, w, b, seq):
    _, t, dff = up.shape
    f = _ctile(dff, FFN_COLS)
    k_taps = w.shape[0]
    ts = _tile(seq, 256)
    tps = seq // ts
    hb = ts // HALO16
    rc = _tile(ts, FFN_ROWS)

    def body(cur_ref, prev_ref, w_ref, b_ref, g_ref, apad):
        i = pl.program_id(1)
        first = (i % tps) == 0
        for ci, c0 in enumerate(range(0, f, LANE)):
            cols = slice(c0, c0 + LANE)
            apad[ci, 0:HALO16, :] = jnp.where(first, 0.0, prev_ref[:, cols].astype(F32))
            apad[ci, HALO16:HALO16 + ts, :] = cur_ref[0, :, cols].astype(F32)
            wv = w_ref[:, cols]
            wk = [jnp.broadcast_to(wv[k:k + 1, :], (rc, LANE)) for k in range(k_taps)]
            bias = jnp.broadcast_to(b_ref[:, cols], (rc, LANE))
            for r0 in range(0, ts, rc):
                c = bias
                for k in range(k_taps):
                    c = c + wk[k] * apad[ci, pl.ds(HALO16 - (k_taps - 1) + k + r0, rc), :]
                gate = cur_ref[1, pl.ds(r0, rc), cols].astype(F32)
                g_ref[pl.ds(r0, rc), cols] = (c * _sigmoid(c) * gate).astype(BF16)

    return pl.pallas_call(
        body, name=name, grid=(dff // f, t // ts),
        in_specs=[pl.BlockSpec((2, ts, f), lambda j, i: (0, i, j)),
                  pl.BlockSpec((None, HALO16, f), lambda j, i: (0, jnp.maximum(i * hb - 1, 0), j)),
                  pl.BlockSpec((k_taps, f), lambda j, i: (0, j)),
                  pl.BlockSpec((1, f), lambda j, i: (0, j))],
        out_specs=pl.BlockSpec((ts, f), lambda j, i: (i, j)),
        out_shape=jax.ShapeDtypeStruct((t, dff), BF16),
        scratch_shapes=[pltpu.VMEM((f // LANE, HALO16 + ts, LANE), F32)],
        compiler_params=_params("parallel", "parallel"),
    )(up, up, w, b)


def _ffn_bwd(name, up, dg, w, b, seq):
    _, t, dff = up.shape
    f = _ctile(dff, FFN_COLS)
    k_taps = w.shape[0]
    ts = _tile(seq, 256)
    tps = seq // ts
    hb = ts // HALO16
    nhb = t // HALO16
    ext = ts + HALO16
    rc = _tile(ts, FFN_ROWS)

    def body(cur_ref, prev_ref, next_ref, dg_ref, ndg_ref, w_ref, b_ref, dup_ref, dw_ref, db_ref, apad, dcpad):
        i = pl.program_id(1)
        first = (i % tps) == 0
        last = (i % tps) == tps - 1
        @pl.when(i == 0)
        def _():
            dw_ref[...] = jnp.zeros_like(dw_ref)
            db_ref[...] = jnp.zeros_like(db_ref)

        for ci, c0 in enumerate(range(0, f, LANE)):
            cols = slice(c0, c0 + LANE)
            apad[ci, 0:HALO16, :] = jnp.where(first, 0.0, prev_ref[:, cols].astype(F32))
            apad[ci, HALO16:HALO16 + ts, :] = cur_ref[0, :, cols].astype(F32)
            apad[ci, HALO16 + ts:HALO16 + ext, :] = next_ref[0, :, cols].astype(F32)
            wv = w_ref[:, cols]
            wk = [jnp.broadcast_to(wv[k:k + 1, :], (rc, LANE)) for k in range(k_taps)]
            bias = jnp.broadcast_to(b_ref[:, cols], (rc, LANE))

            def conv_grad(r0, n, gate, dgv):
                c = bias[0:n]
                for k in range(k_taps):
                    c = c + wk[k][0:n] * apad[ci, pl.ds(HALO16 - (k_taps - 1) + k + r0, n), :]
                sig = _sigmoid(c)
                return dgv * gate * sig * (1.0 + c * (1.0 - sig)), c * sig

            for r0 in range(0, ts, rc):
                dgv = dg_ref[pl.ds(r0, rc), cols].astype(F32)
                dc, silu = conv_grad(r0, rc, cur_ref[1, pl.ds(r0, rc), cols].astype(F32), dgv)
                dcpad[ci, pl.ds(r0, rc), :] = dc
                dup_ref[1, pl.ds(r0, rc), cols] = (dgv * silu).astype(BF16)
            dgv = jnp.where(last, 0.0, ndg_ref[:, cols].astype(F32))
            dc, _ = conv_grad(ts, HALO16, next_ref[1, :, cols].astype(F32), dgv)
            dcpad[ci, ts:ext, :] = dc

            dw_acc = [jnp.zeros((rc, LANE), F32) for _ in range(k_taps)]
            db_acc = jnp.zeros((rc, LANE), F32)
            for r0 in range(0, ts, rc):
                dact = jnp.zeros((rc, LANE), F32)
                for k in range(k_taps):
                    dact = dact + wk[k] * dcpad[ci, pl.ds(r0 + (k_taps - 1) - k, rc), :]
                dup_ref[0, pl.ds(r0, rc), cols] = dact.astype(BF16)
                dc = dcpad[ci, pl.ds(r0, rc), :]
                for k in range(k_taps):
                    dw_acc[k] = dw_acc[k] + dc * apad[ci, pl.ds(HALO16 - (k_taps - 1) + k + r0, rc), :]
                db_acc = db_acc + dc
            for k in range(k_taps):
                dw_ref[k:k + 1, cols] += jnp.sum(dw_acc[k], axis=0, keepdims=True)
            db_ref[:, cols] += jnp.sum(db_acc, axis=0, keepdims=True)

    return pl.pallas_call(
        body, name=name, grid=(dff // f, t // ts),
        in_specs=[pl.BlockSpec((2, ts, f), lambda j, i: (0, i, j)),
                  pl.BlockSpec((None, HALO16, f), lambda j, i: (0, jnp.maximum(i * hb - 1, 0), j)),
                  pl.BlockSpec((2, HALO16, f), lambda j, i: (0, jnp.minimum((i + 1) * hb, nhb - 1), j)),
                  pl.BlockSpec((ts, f), lambda j, i: (i, j)),
                  pl.BlockSpec((HALO16, f), lambda j, i: (jnp.minimum((i + 1) * hb, nhb - 1), j)),
                  pl.BlockSpec((k_taps, f), lambda j, i: (0, j)),
                  pl.BlockSpec((1, f), lambda j, i: (0, j))],
        out_specs=[pl.BlockSpec((2, ts, f), lambda j, i: (0, i, j)),
                   pl.BlockSpec((k_taps, f), lambda j, i: (0, j)),
                   pl.BlockSpec((1, f), lambda j, i: (0, j))],
        out_shape=[jax.ShapeDtypeStruct((2, t, dff), BF16), jax.ShapeDtypeStruct((k_taps, dff), F32),
                   jax.ShapeDtypeStruct((1, dff), F32)],
        scratch_shapes=[pltpu.VMEM((f // LANE, HALO16 + ext, LANE), F32), pltpu.VMEM((f // LANE, ext, LANE), F32)],
        compiler_params=_params("parallel", "arbitrary"),
    )(up, up, up, dg, dg, w, b)


def _sum_rows(name, g):
    ns, r, c = g.shape
    tr = _tile(r, 256)

    def body(g_ref, o_ref):
        acc = g_ref[0]
        for dev in range(1, ns):
            acc = acc + g_ref[dev]
        o_ref[...] = acc

    return pl.pallas_call(
        body, name=name, grid=(r // tr,),
        in_specs=[pl.BlockSpec((ns, tr, c), lambda i: (0, i, 0))],
        out_specs=pl.BlockSpec((tr, c), lambda i: (i, 0)),
        out_shape=jax.ShapeDtypeStruct((r, c), F32), compiler_params=_params("parallel"),
    )(g)


def _adamw(name, gsrc, w, m, v, layer=0, prev=None):
    ns, r, c = gsrc.shape
    nl = w.shape[0]
    tr = _tile(r, 256)
    prev = () if prev is None else tuple(prev)

    def body(g_ref, w_ref, m_ref, v_ref, *rest):
        go_ref, do_ref, mo_ref, vo_ref = rest[len(prev):]
        g = g_ref[0].astype(F32)
        for dev in range(1, ns):
            g = g + g_ref[dev].astype(F32)
        m_new = ADAM_B1 * m_ref[...] + (1.0 - ADAM_B1) * g
        v_new = ADAM_B2 * v_ref[...] + (1.0 - ADAM_B2) * (g * g)
        m_hat = m_new / (1.0 - ADAM_B1 ** ADAM_STEP)
        v_hat = v_new / (1.0 - ADAM_B2 ** ADAM_STEP)
        go_ref[...] = g
        do_ref[...] = -ADAM_LR * (m_hat / (jnp.sqrt(v_hat) + ADAM_EPS) + ADAM_WD * w_ref[...])
        mo_ref[...] = m_new
        vo_ref[...] = v_new

    row = pl.BlockSpec((None, tr, c), lambda i: (layer, i, 0))
    return pl.pallas_call(
        body, name=name, grid=(r // tr,),
        in_specs=[pl.BlockSpec((ns, tr, c), lambda i: (0, i, 0)), row, row, row] + [ANY] * len(prev),
        out_specs=[row] * 4, out_shape=[jax.ShapeDtypeStruct((nl, r, c), F32)] * 4,
        input_output_aliases={4 + i: i for i in range(len(prev))},
        compiler_params=_params("parallel"),
    )(gsrc, w, m, v, *prev)


def _ffn_forward(tag, r_in, h, get_wu, get_wd, wdw, bdw, seq, loss=None):
    t, d = r_in.shape
    tm = _tile(t, 512)
    wu = get_wu(h)
    dff = wu.shape[0] // 2
    tu = _tile(t, 1024)
    up = _mm(f"{tag}_up", h, wu, grid=(2, t // tu, 1),
             a_spec=pl.BlockSpec((tu, d), lambda j, i, k: (i, 0)),
             b_spec=pl.BlockSpec((dff, d), lambda j, i, k: (j, 0)),
             out_spec=pl.BlockSpec((None, tu, dff), lambda j, i, k: (j, i, 0)),
             out_shape=jax.ShapeDtypeStruct((2, t, dff), BF16), dims=NT, acc_shape=(tu, dff))
    wd = get_wd(up)
    g = _ffn_fwd(f"{tag}_act", up, wdw, bdw, seq)
    row = pl.BlockSpec((tm, d), lambda i, j, k: (i, 0))
    vec = pl.BlockSpec((1, d), lambda i, j, k: (0, 0))
    common = dict(grid=(t // tm, 1, 1), a_spec=pl.BlockSpec((tm, dff), lambda i, j, k: (i, 0)),
                  b_spec=pl.BlockSpec((dff, d), lambda i, j, k: (0, 0)), dims=NN, acc_shape=(tm, d))
    if loss is None:
        out = _mm(f"{tag}_down", g, wd, out_spec=row, out_shape=jax.ShapeDtypeStruct((t, d), F32),
                  extras=(r_in,), extra_specs=(row,), epilogue=lambda acc, ex: ((ex[0][...] + acc,), ()), **common)
    else:
        def head(acc, ex):
            dx, part, dgain = _loss_tile(ex[0][...] + acc, ex[1][...], ex[2][...])
            return (dx,), (part, dgain)

        out = _mm(f"{tag}_down", g, wd, out_spec=[row, pl.BlockSpec((1, 1), lambda i, j, k: (0, 0)), vec],
                  out_shape=[jax.ShapeDtypeStruct((t, d), F32), jax.ShapeDtypeStruct((1, 1), F32),
                             jax.ShapeDtypeStruct((1, d), F32)],
                  extras=(r_in, *loss), extra_specs=(row, row, vec), epilogue=head, n_sums=2, **common)
    return out, (r_in, h, up, g, wu, wd)


def _ffn_backward(tag, dr, saved, gain, wdw, bdw, seq, token=None):
    r_in, h, up, g, wu, wd = saved
    t, d = r_in.shape
    dff = wd.shape[0]
    tm = _tile(t, 512)
    tk = _tile(t, 2048)
    cw = _ctile(dff, 1408)
    nc = dff // cw
    dg = _mm(f"{tag}_dg", dr, wd, grid=(t // tm, 1, 1),
             a_spec=pl.BlockSpec((tm, d), lambda i, j, k: (i, 0)),
             b_spec=pl.BlockSpec((dff, d), lambda i, j, k: (0, 0)),
             out_spec=pl.BlockSpec((tm, dff), lambda i, j, k: (i, 0)),
             out_shape=jax.ShapeDtypeStruct((t, dff), BF16), dims=NT, acc_shape=(tm, dff), token=token)
    dwd = _mm(f"{tag}_dwd", g, dr, grid=(dff // cw, 1, t // tk),
              a_spec=pl.BlockSpec((tk, cw), lambda i, j, k: (k, i)),
              b_spec=pl.BlockSpec((tk, d), lambda i, j, k: (k, 0)),
              out_spec=pl.BlockSpec((cw, d), lambda i, j, k: (i, 0)),
              out_shape=jax.ShapeDtypeStruct((dff, d), BF16), dims=TN, acc_shape=(cw, d))
    dup, dwdw, dbdw = _ffn_bwd(f"{tag}_dact", up, dg, wdw, bdw, seq)
    row = pl.BlockSpec((tm, d), lambda i, j, k: (i, 0))
    vec = pl.BlockSpec((1, d), lambda i, j, k: (0, 0))

    def norm_backward(acc, ex):
        dx, dgain, colsum = _rms_bwd_tile(acc, ex[0][...], ex[1][...], ex[2][...])
        return (dx,), (dgain, colsum)

    dr_in, dgain, colsum = _mm(
        f"{tag}_dh", dup, wu, grid=(t // tm, 1, 2),
        a_spec=pl.BlockSpec((None, tm, dff), lambda i, j, k: (k, i, 0)),
        b_spec=pl.BlockSpec((dff, d), lambda i, j, k: (k, 0)), out_spec=[row, vec, vec],
        out_shape=[jax.ShapeDtypeStruct((t, d), F32)] + [jax.ShapeDtypeStruct((1, d), F32)] * 2,
        dims=NN, acc_shape=(tm, d), extras=(r_in, gain, dr), extra_specs=(row, vec, row),
        epilogue=norm_backward, n_sums=2)
    dwu = _mm(f"{tag}_dwu", dup, h, grid=(2 * nc, 1, t // tk),
              a_spec=pl.BlockSpec((None, tk, cw), lambda i, j, k: (i // nc, k, i % nc)),
              b_spec=pl.BlockSpec((tk, d), lambda i, j, k: (k, 0)),
              out_spec=pl.BlockSpec((cw, d), lambda i, j, k: (i, 0)),
              out_shape=jax.ShapeDtypeStruct((2 * dff, d), BF16), dims=TN, acc_shape=(cw, d))
    return dr_in, dgain, dwu, dwd, dwdw, dbdw, colsum


def _pad_to(vec, n):
    return jnp.pad(vec, (0, n - vec.shape[0]))


def _pack(parts, width):
    flat = jnp.concatenate([p.reshape(-1).astype(F32) for p in parts])
    n = -(-flat.shape[0] // (8 * width)) * (8 * width)
    return _pad_to(flat, n).reshape(n // width, width)


def _unpack(mat, shapes):
    flat = mat.reshape(-1)
    out, off = [], 0
    for s in shapes:
        n = 1
        for dim in s:
            n *= dim
        out.append(flat[off:off + n].reshape(s))
        off += n
    return out


def kernel(x, norm_mix, norm_ffn, conv_w_pw1, conv_b_pw1, conv_w_dw, conv_b_dw, conv_ln_g, conv_ln_b, conv_w_pw2, conv_b_pw2, pool_w, pool_b, pool_scale, ffn_w_up, ffn_w_dw, ffn_b_dw, ffn_w_down, final_norm, loss_target, m_norm_mix, m_norm_ffn, m_conv_w_pw1, m_conv_b_pw1, m_conv_w_dw, m_conv_b_dw, m_conv_ln_g, m_conv_ln_b, m_conv_w_pw2, m_conv_b_pw2, m_pool_w, m_pool_b, m_pool_scale, m_ffn_w_up, m_ffn_w_dw, m_ffn_b_dw, m_ffn_w_down, m_final_norm, v_norm_mix, v_norm_ffn, v_conv_w_pw1, v_conv_b_pw1, v_conv_w_dw, v_conv_b_dw, v_conv_ln_g, v_conv_ln_b, v_conv_w_pw2, v_conv_b_pw2, v_pool_w, v_pool_b, v_pool_scale, v_ffn_w_up, v_ffn_w_dw, v_ffn_b_dw, v_ffn_w_down, v_final_norm):
    bsz, seq, d = x.shape
    t = bsz * seq
    k_taps = conv_w_dw.shape[1]
    cs1 = conv_w_pw1.shape[2]
    dsh = d // N_DEV
    cg = d // N_GROUPS
    cgs = pool_w.shape[2]
    fu = ffn_w_up.shape[2]
    fd = ffn_w_down.shape[1]
    dff = fd * N_DEV
    nb = N_DEV // 2
    kf = ffn_w_dw.shape[1]
    fsh = ffn_w_dw.shape[2]
    my = _lin(_me())
    tm = _tile(t, 512)

    x2 = x.reshape(t, d)
    tgt2 = loss_target.reshape(t, d)

    small_shapes = [(k_taps, dsh), (dsh,), (dsh,), (2, kf, fsh)]
    small_mine = _pack([conv_w_dw[0], pool_b[0], pool_scale[0], ffn_w_dw], LANE)
    big = [conv_w_pw1[0], conv_w_pw2[0], ffn_w_up[0].T, ffn_w_down[0], pool_w[0], ffn_w_up[1].T, ffn_w_down[1]]
    gather = _exchange_start("gather_start", [small_mine] + [w.astype(BF16) for w in big], [False] * 8)
    h0 = _rms_fwd("l0_rms", x2, norm_mix[0:1])
    small_all, w1 = _exchange_wait("gather_wait_w1", gather, [0, 1], h0)
    parts = [_unpack(small_all[dev], small_shapes) for dev in range(N_DEV)]
    wdw = jnp.concatenate([p[0] for p in parts], axis=1)
    pool_b_full = jnp.concatenate([p[1] for p in parts]).reshape(1, d)
    pool_s_full = jnp.concatenate([p[2] for p in parts]).reshape(1, d)
    fwdw = jnp.concatenate([p[3] for p in parts], axis=2)
    fbdw = ffn_b_dw.reshape(2, 1, dff)

    def columns(w):
        return w.transpose(1, 0, 2).reshape(w.shape[1], N_DEV * w.shape[2])

    def column_shards(w):
        return w.reshape(w.shape[0], N_DEV, w.shape[1] // N_DEV).transpose(1, 0, 2)

    w1 = columns(w1)
    a = _mm("l0_pw1", h0, w1, grid=(2, t // tm, 1),
            a_spec=pl.BlockSpec((tm, d), lambda j, i, k: (i, 0)),
            b_spec=pl.BlockSpec((d, d), lambda j, i, k: (0, j)),
            out_spec=pl.BlockSpec((None, tm, d), lambda j, i, k: (j, i, 0)),
            out_shape=jax.ShapeDtypeStruct((2, t, d), BF16), dims=NN, acc_shape=(tm, d),
            extras=(conv_b_pw1,), extra_specs=(pl.BlockSpec((1, d), lambda j, i, k: (0, j)),),
            epilogue=lambda acc, ex: ((acc + ex[0][...],), ()))
    (w2,) = _exchange_wait("gather_wait_w2", gather, [2], a)
    w2 = w2.reshape(d, d)
    v = _conv_fwd("l0_conv", a, wdw, conv_b_dw, seq)
    row = pl.BlockSpec((tm, d), lambda i, j, k: (i, 0))
    vec = pl.BlockSpec((1, d), lambda i, j, k: (0, 0))
    square = pl.BlockSpec((d, d), lambda i, j, k: (0, 0))

    def ln_silu(v_blk, ex):
        s_blk = _ln_silu_tile(v_blk, ex[0][...], ex[1][...]).astype(BF16)
        return s_blk, s_blk

    def residual_and_norm(acc, ex):
        r_blk = ex[3][...] + (acc + ex[2][...])
        return (r_blk, _rms(r_blk, ex[4][...])), ()

    r1, h1, s = _mm("l0_pw2", v, w2, grid=(t // tm, 1, 1), a_spec=row, b_spec=square, out_spec=[row, row, row],
                    out_shape=[jax.ShapeDtypeStruct((t, d), F32), jax.ShapeDtypeStruct((t, d), BF16),
                               jax.ShapeDtypeStruct((t, d), BF16)],
                    dims=NN, acc_shape=(tm, d), extras=(conv_ln_g, conv_ln_b, conv_b_pw2, x2, norm_ffn[0:1]),
                    extra_specs=(vec, vec, vec, row, vec), prologue=ln_silu, epilogue=residual_and_norm)

    def up_getter(name, idx):
        return lambda after: _exchange_wait(name, gather, [idx], after)[0].reshape(2 * dff, d)

    def down_getter(name, idx):
        return lambda after: _exchange_wait(name, gather, [idx], after)[0].reshape(dff, d)

    r2, ffn0_saved = _ffn_forward("f0", r1, h1, up_getter("gather_wait_wu0", 3),
                                  down_getter("gather_wait_wd0", 4), fwdw[0], fbdw[0], seq)
    (wp,) = _exchange_wait("gather_wait_wp", gather, [5], r2)
    wp = wp.transpose(1, 0, 2, 3).reshape(N_GROUPS, cg, cg)
    pooled, r3, h3 = _pool_mix_fwd("l1_mix", r2, norm_mix[1:2], wp, pool_s_full, pool_b_full, norm_ffn[1:2], seq)
    (dr4, loss_part, dfinal), ffn1_saved = _ffn_forward(
        "f1", r3, h3, up_getter("gather_wait_wu1", 6), down_getter("gather_wait_wd1", 7), fwdw[1], fbdw[1], seq,
        loss=(tgt2, final_norm.reshape(1, d)))

    dr3, dnf1, dwu1, dwd1, dfw1, dfb1, _ = _ffn_backward("f1", dr4, ffn1_saved, norm_ffn[1:2], fwdw[1], fbdw[1], seq)
    scatter_a = _exchange_start("scatter_f1_start", [dwu1.reshape(N_DEV, fu, d), dwd1.reshape(N_DEV, fd, d)],
                                [True, True])
    dr2, dmixed, dpool_s, dpool_b, dnm1 = _pool_mix_bwd(
        "l1_dmix", pooled, wp, dr3, r2, norm_mix[1:2], pool_s_full + scatter_a["token"][0:1, 0:1], pool_b_full, seq)
    dwp = _mm("l1_dwp", pooled, dmixed, grid=(N_GROUPS, 1, t // tm),
              a_spec=pl.BlockSpec((tm, cg), lambda i, j, k: (k, i)),
              b_spec=pl.BlockSpec((tm, cg), lambda i, j, k: (k, i)),
              out_spec=pl.BlockSpec((None, cg, cg), lambda i, j, k: (i, 0, 0)),
              out_shape=jax.ShapeDtypeStruct((N_GROUPS, cg, cg), BF16), dims=TN, acc_shape=(cg, cg))
    dr1, dnf0, dwu0, dwd0, dfw0, dfb0, db2 = _ffn_backward("f0", dr2, ffn0_saved, norm_ffn[0:1], fwdw[0], fbdw[0], seq)
    dwp_b = dwp.reshape(N_GROUPS, N_DEV, cgs, cg).transpose(1, 0, 2, 3)
    scatter_b = _exchange_start("scatter_f0_start", [dwu0.reshape(N_DEV, fu, d), dwd0.reshape(N_DEV, fd, d), dwp_b],
                                [True] * 3)
    def ln_silu_backward(acc, ex):
        dv_blk, dgain, dbias, colsum = _ln_silu_bwd_tile(acc, ex[0][...], ex[1][...], ex[2][...])
        return (dv_blk,), (dgain, dbias, colsum)

    dv, dlg, dlb, dbdw = _mm("l0_ds", dr1, w2, grid=(t // tm, 1, 1), a_spec=row, b_spec=square,
                             out_spec=[row, vec, vec, vec],
                             out_shape=[jax.ShapeDtypeStruct((t, d), F32)] + [jax.ShapeDtypeStruct((1, d), F32)] * 3,
                             dims=NT, acc_shape=(tm, d), extras=(v, conv_ln_g, conv_ln_b), extra_specs=(row, vec, vec),
                             epilogue=ln_silu_backward, n_sums=3, token=scatter_b["token"])
    tk = _tile(t, 2048)
    dw2 = _mm("l0_dw2", s, dr1, grid=(1, 1, t // tk),
              a_spec=pl.BlockSpec((tk, d), lambda i, j, k: (k, 0)),
              b_spec=pl.BlockSpec((tk, d), lambda i, j, k: (k, 0)),
              out_spec=pl.BlockSpec((d, d), lambda i, j, k: (0, 0)),
              out_shape=jax.ShapeDtypeStruct((d, d), BF16), dims=TN, acc_shape=(d, d))
    da, dwdw, db1 = _conv_bwd("l0_dconv", a, dv, wdw, seq)
    dw1 = _mm("l0_dw1", h0, da, grid=(1, 2, t // tk),
              a_spec=pl.BlockSpec((tk, d), lambda i, j, k: (k, 0)),
              b_spec=pl.BlockSpec((None, tk, d), lambda i, j, k: (j, k, 0)),
              out_spec=pl.BlockSpec((d, d), lambda i, j, k: (0, j)),
              out_shape=jax.ShapeDtypeStruct((d, 2 * d), BF16), dims=TN, acc_shape=(d, d))
    scatter_c = _exchange_start("scatter_l0_start", [column_shards(dw1), dw2.reshape(N_DEV, d // N_DEV, d)],
                                [True, True])
    def norm_backward(acc, ex):
        dx_blk, dgain, colsum = _rms_bwd_tile(acc, ex[0][...], ex[1][...], ex[2][...])
        return (dx_blk,), (dgain, colsum)

    dx, dnm0, _ = _mm("l0_dh", da, w1, grid=(t // tm, 1, 2),
                      a_spec=pl.BlockSpec((None, tm, d), lambda i, j, k: (k, i, 0)),
                      b_spec=pl.BlockSpec((d, d), lambda i, j, k: (0, k)), out_spec=[row, vec, vec],
                      out_shape=[jax.ShapeDtypeStruct((t, d), F32)] + [jax.ShapeDtypeStruct((1, d), F32)] * 2,
                      dims=NT, acc_shape=(tm, d), extras=(x2, norm_mix[0:1], dr1), extra_specs=(row, vec, row),
                      epilogue=norm_backward, n_sums=2, token=scatter_c["token"])

    dffn_w = jnp.stack([dfw0, dfw1])
    dffn_b = jnp.stack([dfb0, dfb1]).reshape(2, dff)
    small_parts = [loss_part, jnp.concatenate([dnm0, dnm1]), jnp.concatenate([dnf0, dnf1]), db1, dwdw, dbdw, dlg, dlb,
                   db2, dpool_b, dpool_s, dffn_w, dffn_b, dfinal]
    small_part_shapes = [(1,), (2, d), (2, d), (1, 2 * d), (k_taps, d), (1, d), (1, d), (1, d), (1, d), (1, d), (1, d),
                         (2, kf, dff), (2, dff), (d,)]
    packed = _pack(small_parts, 8 * LANE)
    gather_small = _exchange_start("gather_small_start", [packed], [False])

    def big_update(name, recv, w, m, v, layer=0, prev=None):
        shape = w.shape
        c = recv.shape[-1]
        rows = recv.size // (N_DEV * c)
        nl = w.size // (rows * c)
        outs = _adamw(name, recv.reshape(N_DEV, rows, c), w.reshape(nl, rows, c), m.reshape(nl, rows, c),
                      v.reshape(nl, rows, c), layer, prev)
        return outs, [o.reshape(shape) for o in outs]

    wu_t = [p.transpose(0, 2, 1) for p in (ffn_w_up, m_ffn_w_up, v_ffn_w_up)]
    g_wu1, g_wd1 = _exchange_wait("scatter_f1_wait", scatter_a, [0, 1], gather_small["token"])
    raw_wu, _ = big_update("adam_wu1", g_wu1, *wu_t, 1)
    raw_wd, _ = big_update("adam_wd1", g_wd1, ffn_w_down, m_ffn_w_down, v_ffn_w_down, 1)
    g_wu0, g_wd0, g_wp = _exchange_wait("scatter_f0_wait", scatter_b, [0, 1, 2], raw_wd[0])
    _, u_wu = big_update("adam_wu0", g_wu0, *wu_t, 0, raw_wu)
    u_wu = [o.transpose(0, 2, 1) for o in u_wu]
    _, u_wd = big_update("adam_wd0", g_wd0, ffn_w_down, m_ffn_w_down, v_ffn_w_down, 0, raw_wd)
    _, u_wp = big_update("adam_wp", g_wp, pool_w, m_pool_w, v_pool_w)
    g_w1, g_w2 = _exchange_wait("scatter_l0_wait", scatter_c, [0, 1], u_wp[0])
    _, u_w1 = big_update("adam_w1", g_w1, conv_w_pw1, m_conv_w_pw1, v_conv_w_pw1)
    _, u_w2 = big_update("adam_w2", g_w2, conv_w_pw2, m_conv_w_pw2, v_conv_w_pw2)
    (all_small,) = _exchange_wait("gather_small_wait", gather_small, [0], u_w2[0])
    summed = _sum_rows("sum_small_grads", all_small)
    (loss_v, g_nm, g_nf, g_b1, g_wdw, g_bdw, g_lg, g_lb, g_b2, g_pb, g_ps, g_fw, g_fb,
     g_fin) = _unpack(summed, small_part_shapes)
    loss = loss_v[0]
    g_wdw_mine = lax.dynamic_slice_in_dim(g_wdw, my * dsh, dsh, axis=1)[None]
    g_pb_mine = lax.dynamic_slice_in_dim(g_pb, my * dsh, dsh, axis=1)
    g_ps_mine = lax.dynamic_slice_in_dim(g_ps, my * dsh, dsh, axis=1)
    g_fw_mine = lax.dynamic_slice_in_dim(g_fw, my * fsh, fsh, axis=2)

    small_g =[g_nm, g_nf, g_b1, g_wdw_mine, g_bdw, g_lg, g_lb, g_b2, g_pb_mine, g_ps_mine, g_fw_mine, g_fb, g_fin]
    small_w = [norm_mix, norm_ffn, conv_b_pw1, conv_w_dw, conv_b_dw, conv_ln_g, conv_ln_b, conv_b_pw2, pool_b,
               pool_scale, ffn_w_dw, ffn_b_dw, final_norm]
    small_m = [m_norm_mix, m_norm_ffn, m_conv_b_pw1, m_conv_w_dw, m_conv_b_dw, m_conv_ln_g, m_conv_ln_b,
               m_conv_b_pw2, m_pool_b, m_pool_scale, m_ffn_w_dw, m_ffn_b_dw, m_final_norm]
    small_v = [v_norm_mix, v_norm_ffn, v_conv_b_pw1, v_conv_w_dw, v_conv_b_dw, v_conv_ln_g, v_conv_ln_b,
               v_conv_b_pw2, v_pool_b, v_pool_scale, v_ffn_w_dw, v_ffn_b_dw, v_final_norm]
    shapes = [w.shape for w in small_w]
    outs = _adamw("adam_small", _pack(small_g, 8 * LANE)[None], _pack(small_w, 8 * LANE)[None],
                  _pack(small_m, 8 * LANE)[None], _pack(small_v, 8 * LANE)[None])
    sg, sd, sm, sv = [_unpack(o, shapes) for o in outs]

    def leaf(kind):
        (nm, nf, b1, wdw_, bdw_, lg, lb, b2, pb, ps, fw, fb, fin) = (sg, sd, sm, sv)[kind]
        return [nm, nf, u_w1[kind], b1, wdw_, bdw_, lg, lb, u_w2[kind], b2, u_wp[kind], pb, ps, u_wu[kind], fw, fb,
                u_wd[kind], fin]

    return (loss, dx.reshape(bsz, seq, d), *leaf(0), *leaf(1), *leaf(2), *leaf(3))
```

```python
import functools

import jax
import jax.numpy as jnp
from jax import lax
from jax.experimental import pallas as pl
from jax.experimental.pallas import tpu as pltpu

F32 = jnp.float32
BF16 = jnp.bfloat16
MESH = pl.DeviceIdType.MESH
HBM = pl.BlockSpec(memory_space=pltpu.HBM)

N_DEV = 8
RMS_EPS = 1e-6
LN_EPS = 1e-5
POOL_WINDOWS = (2, 4, 8, 16)
N_GROUPS = len(POOL_WINDOWS)
ADAM_LR = 0.001
ADAM_B1 = 0.9
ADAM_B2 = 0.999
ADAM_EPS = 1e-08
ADAM_WD = 0.01
ADAM_STEP = 10

LANE = 128
HALO = 32
HALO16 = 16
VMEM_LIMIT = 56 * 1024 * 1024


def _params(*sem):
    return pltpu.CompilerParams(dimension_semantics=sem if sem else None, vmem_limit_bytes=VMEM_LIMIT)


def _tile(n, pref):
    for t in range(min(pref, n), 15, -1):
        if n % t == 0 and t % 16 == 0:
            return t
    return n


def _sigmoid(z):
    return 1.0 / (1.0 + jnp.exp(-z))


def _me():
    return lax.axis_index("x"), lax.axis_index("y"), lax.axis_index("c")


def _flip(pos, m):
    x, y, c = pos
    return ((1 - x) if m & 4 else x, (1 - y) if m & 2 else y, (1 - c) if m & 1 else c)


def _lin(pos):
    return 4 * pos[0] + 2 * pos[1] + pos[2]


SEM = pl.BlockSpec(memory_space=pltpu.SEMAPHORE)
ANY = pl.BlockSpec(memory_space=pl.ANY)
EFFECT = pltpu.SideEffectType.DATAFLOW_SIDE_EFFECTING


def _exchange_copies(srcs, lands, send_sems, recv_sems, modes, which, starting):
    me = _me()
    my = _lin(me)
    out = []
    for pos, a in enumerate(which):
        src, land = srcs[pos], lands[pos]

        def block(pid, src=src, a=a):
            return src.at[pid] if modes[a] else src

        local = pltpu.make_async_copy(block(my), land.at[my], send_sems.at[a * N_DEV])
        remote = []
        for m in range(1, N_DEV):
            peer = _flip(me, m)
            pid = _lin(peer)
            sems = dict(send_sem=send_sems.at[a * N_DEV + m], recv_sem=recv_sems.at[a * N_DEV + m],
                        device_id=peer, device_id_type=MESH)
            if starting:
                remote.append(pltpu.make_async_remote_copy(src_ref=block(pid), dst_ref=land.at[my], **sems))
            else:
                remote.append((pltpu.make_async_remote_copy(src_ref=block(pid), dst_ref=land.at[my], **sems),
                               pltpu.make_async_remote_copy(src_ref=block(pid), dst_ref=land.at[pid], **sems)))
        out.append((local, remote))
    return out


def _exchange_start(name, arrs, modes):
    n = len(arrs)
    blocks = [a.shape[1:] if md else a.shape for a, md in zip(arrs, modes)]

    def body(*refs):
        srcs, lands = refs[:n], refs[n:2 * n]
        send_sems, recv_sems = refs[2 * n], refs[2 * n + 1]
        token = refs[-1]
        for local, remote in _exchange_copies(srcs, lands, send_sems, recv_sems, modes, list(range(n)), True):
            local.start()
            for send in remote:
                send.start()
        token[...] = jnp.zeros_like(token)

    lands = [lax.empty((N_DEV,) + tuple(b), a.dtype) for a, b in zip(arrs, blocks)]
    outs = pl.pallas_call(
        body, name=name,
        out_shape=(pltpu.SemaphoreType.DMA((n * N_DEV,)), pltpu.SemaphoreType.DMA((n * N_DEV,)),
                   *[pltpu.HBM(a.shape, a.dtype) for a in arrs], *[pltpu.HBM(l.shape, l.dtype) for l in lands],
                   jax.ShapeDtypeStruct((8, LANE), F32)),
        in_specs=[HBM] * (2 * n),
        out_specs=(SEM, SEM, *[HBM] * (2 * n), pl.BlockSpec(memory_space=pltpu.VMEM)),
        input_output_aliases={i: 2 + i for i in range(2 * n)},
        compiler_params=pltpu.CompilerParams(has_side_effects=EFFECT),
    )(*[pltpu.with_memory_space_constraint(a, pltpu.HBM) for a in arrs],
      *[pltpu.with_memory_space_constraint(l, pltpu.HBM) for l in lands])
    return dict(send=outs[0], recv=outs[1], srcs=list(outs[2:2 + n]), lands=list(outs[2 + n:2 + 2 * n]),
                modes=modes, token=outs[-1])


def _exchange_wait(name, handle, which, after):
    k = len(which)
    modes = handle["modes"]

    def body(*refs):
        srcs, lands = refs[:k], refs[k:2 * k]
        send_sems, recv_sems = refs[2 * k], refs[2 * k + 1]
        for local, remote in _exchange_copies(srcs, lands, send_sems, recv_sems, modes, which, False):
            local.wait()
            for send, arrival in remote:
                send.wait_send()
                arrival.wait_recv()

    srcs = [handle["srcs"][a] for a in which]
    lands = [handle["lands"][a] for a in which]
    outs = pl.pallas_call(
        body, name=name,
        out_shape=tuple(pltpu.HBM(x.shape, x.dtype) for x in srcs + lands),
        in_specs=[HBM] * (2 * k) + [SEM, SEM, ANY], out_specs=tuple([HBM] * (2 * k)),
        input_output_aliases={i: i for i in range(2 * k)},
        compiler_params=pltpu.CompilerParams(has_side_effects=EFFECT),
    )(*srcs, *lands, handle["send"], handle["recv"], after)
    return list(outs[k:])


def _mm(name, a, b, *, grid, a_spec, b_spec, out_spec, out_shape, dims, acc_shape, extras=(), extra_specs=(),
        epilogue=None, token=None, prologue=None, n_sums=0, chunked=False):
    nk = grid[2]
    ne = len(extras)
    deps = () if token is None else (token,)
    dep_specs = [pl.BlockSpec((8, LANE), lambda i, j, k: (0, 0))] * len(deps)
    n_out = len(out_shape) if isinstance(out_shape, (list, tuple)) else 1
    n_tiles = n_out - n_sums - (1 if prologue is not None else 0)

    def body(a_ref, b_ref, *rest):
        ex, o_refs, acc_ref = rest[:ne], rest[ne + len(deps):ne + len(deps) + n_out], rest[ne + len(deps) + n_out]
        k = pl.program_id(2)
        a_blk, saved = a_ref[...], None
        if prologue is not None:
            a_blk, saved = prologue(a_blk, ex)
            o_refs[n_tiles][...] = saved
        part = lax.dot_general(a_blk.astype(BF16), b_ref[...].astype(BF16), (dims, ((), ())),
                               preferred_element_type=F32)
        sum_refs = o_refs[n_out - n_sums:]

        def add_sums(terms):
            @pl.when((pl.program_id(0) == 0) & (pl.program_id(1) == 0))
            def _():
                for o_ref in sum_refs:
                    o_ref[...] = jnp.zeros_like(o_ref)

            for o_ref, term in zip(sum_refs, terms):
                o_ref[...] += jnp.sum(term, axis=0, keepdims=True)

        def finish(r):
            if not chunked:
                tiles, terms = ((r,), ()) if epilogue is None else epilogue(r, ex, slice(None))
                for o_ref, val in zip(o_refs, tiles):
                    o_ref[...] = val.astype(o_ref.dtype)
                if n_sums:
                    add_sums(terms)
                return
            acc_ref[...] = r
            rows_total = acc_ref.shape[0]

            def one_chunk(c, carry):
                rows = pl.ds(pl.multiple_of(c * EPILOGUE_ROWS, EPILOGUE_ROWS), EPILOGUE_ROWS)
                tiles, terms = epilogue(acc_ref[rows, :], ex, rows)
                for o_ref, val in zip(o_refs, tiles):
                    o_ref[rows, :] = val.astype(o_ref.dtype)
                return tuple(s + term for s, term in zip(carry, terms))

            init = tuple(jnp.zeros((EPILOGUE_ROWS, o_ref.shape[1]), F32) for o_ref in sum_refs)
            terms = lax.fori_loop(0, rows_total // EPILOGUE_ROWS, one_chunk, init)
            if n_sums:
                add_sums(terms)

        if nk == 1:
            finish(part)
            return

        @pl.when(k == 0)
        def _():
            acc_ref[...] = part

        @pl.when((k > 0) & (k < nk - 1))
        def _():
            acc_ref[...] += part

        @pl.when(k == nk - 1)
        def _():
            finish(acc_ref[...] + part)

    return pl.pallas_call(
        body, name=name, grid=grid, in_specs=[a_spec, b_spec, *extra_specs, *dep_specs], out_specs=out_spec,
        out_shape=out_shape, scratch_shapes=[pltpu.VMEM(acc_shape if nk > 1 or chunked else (8, LANE), F32)],
        compiler_params=_params(*(("arbitrary",) * 3 if n_sums else ("parallel", "parallel", "arbitrary"))),
    )(a, b, *extras, *deps)


EPILOGUE_ROWS = 8


def _rms(x, gain):
    return x * lax.rsqrt(jnp.mean(x * x, axis=-1, keepdims=True) + RMS_EPS) * gain


def _rms_bwd_tile(dh, x, gain, dres):
    rstd = lax.rsqrt(jnp.mean(x * x, axis=-1, keepdims=True) + RMS_EPS)
    xhat = x * rstd
    dxhat = dh * gain
    dx = dres + rstd * (dxhat - xhat * jnp.mean(dxhat * xhat, axis=-1, keepdims=True))
    return dx, dh * xhat, dx


def _ln_silu_tile(v, g, b):
    mu = jnp.mean(v, axis=-1, keepdims=True)
    cen = v - mu
    z = cen * lax.rsqrt(jnp.mean(cen * cen, axis=-1, keepdims=True) + LN_EPS) * g + b
    return z * _sigmoid(z)


def _ln_silu_bwd_tile(ds, v, g, b):
    mu = jnp.mean(v, axis=-1, keepdims=True)
    cen = v - mu
    rstd = lax.rsqrt(jnp.mean(cen * cen, axis=-1, keepdims=True) + LN_EPS)
    y = cen * rstd
    z = y * g + b
    sig = _sigmoid(z)
    dz = ds * sig * (1.0 + z * (1.0 - sig))
    dy = dz * g
    dv = rstd * (dy - jnp.mean(dy, axis=-1, keepdims=True) - y * jnp.mean(dy * y, axis=-1, keepdims=True))
    return dv, dz * y, dz, dv


def _loss_tile(x, tgt, gain):
    d = x.shape[-1]
    rstd = lax.rsqrt(jnp.mean(x * x, axis=-1, keepdims=True) + RMS_EPS)
    xhat = x * rstd
    err = xhat * gain - tgt
    dy = err / d
    dxhat = dy * gain
    dx = rstd * (dxhat - xhat * jnp.mean(dxhat * xhat, axis=-1, keepdims=True))
    return dx, 0.5 * jnp.mean(err * err, axis=-1, keepdims=True), dy * xhat


NN = ((1,), (0,))
NT = ((1,), (1,))
TN = ((0,), (0,))


def _conv_tiles(t, seq):
    ts = _tile(seq, 512)
    return ts, seq // ts, _tile(ts, 64)


def _conv_fwd(name, a, w, b, seq):
    _, t, d = a.shape
    k_taps = w.shape[0]
    ts, tps, rc = _conv_tiles(t, seq)
    hb = ts // HALO

    def body(cur_ref, prev_ref, w_ref, b_ref, v_ref, upad):
        i = pl.program_id(1)
        first = (i % tps) == 0
        pv = prev_ref[0].astype(F32)
        pg = prev_ref[1].astype(F32)
        upad[0:HALO, :] = jnp.where(first, 0.0, pv * _sigmoid(pg))
        upad[HALO:HALO + ts, :] = cur_ref[0].astype(F32) * _sigmoid(cur_ref[1].astype(F32))
        wv = w_ref[...]
        bias = jnp.broadcast_to(b_ref[...], (rc, LANE))
        for r0 in range(0, ts, rc):
            acc = bias
            for k in range(k_taps):
                acc = acc + wv[k:k + 1, :] * upad[pl.ds(HALO - (k_taps - 1) + k + r0, rc), :]
            v_ref[pl.ds(r0, rc), :] = acc

    return pl.pallas_call(
        body, name=name, grid=(d // LANE, t // ts),
        in_specs=[pl.BlockSpec((2, ts, LANE), lambda c, i: (0, i, c)),
                  pl.BlockSpec((2, HALO, LANE), lambda c, i: (0, jnp.maximum(i * hb - 1, 0), c)),
                  pl.BlockSpec((k_taps, LANE), lambda c, i: (0, c)),
                  pl.BlockSpec((1, LANE), lambda c, i: (0, c))],
        out_specs=pl.BlockSpec((ts, LANE), lambda c, i: (i, c)),
        out_shape=jax.ShapeDtypeStruct((t, d), F32),
        scratch_shapes=[pltpu.VMEM((HALO + ts, LANE), F32)],
        compiler_params=_params("parallel", "parallel"),
    )(a, a, w, b)


def _conv_bwd(name, a, dv, w, seq):
    _, t, d = a.shape
    k_taps = w.shape[0]
    ts, tps, rc = _conv_tiles(t, seq)
    hb = ts // HALO
    nhb = t // HALO

    def body(cur_ref, prev_ref, dv_ref, ndv_ref, w_ref, da_ref, dw_ref, dbp_ref, upad, dvpad, dwrows):
        i = pl.program_id(1)
        first = (i % tps) == 0
        last = (i % tps) == tps - 1
        pv = prev_ref[0].astype(F32)
        pg = prev_ref[1].astype(F32)
        upad[0:HALO, :] = jnp.where(first, 0.0, pv * _sigmoid(pg))
        upad[HALO:HALO + ts, :] = cur_ref[0].astype(F32) * _sigmoid(cur_ref[1].astype(F32))
        dvpad[0:ts, :] = dv_ref[...]
        dvpad[ts:ts + HALO, :] = jnp.where(last, 0.0, ndv_ref[...])
        wv = w_ref[...]

        @pl.when(i == 0)
        def _():
            dw_ref[...] = jnp.zeros_like(dw_ref)
            dbp_ref[...] = jnp.zeros_like(dbp_ref)

        sv = jnp.zeros((1, LANE), F32)
        sg = jnp.zeros((1, LANE), F32)
        for r0 in range(0, ts, rc):
            du = jnp.zeros((rc, LANE), F32)
            for k in range(k_taps):
                du = du + wv[k:k + 1, :] * dvpad[pl.ds(r0 + (k_taps - 1) - k, rc), :]
            av = cur_ref[0, pl.ds(r0, rc), :].astype(F32)
            sig = _sigmoid(cur_ref[1, pl.ds(r0, rc), :].astype(F32))
            dval = du * sig
            dgate = du * av * sig * (1.0 - sig)
            da_ref[0, pl.ds(r0, rc), :] = dval.astype(BF16)
            da_ref[1, pl.ds(r0, rc), :] = dgate.astype(BF16)
            sv = sv + jnp.sum(dval, axis=0, keepdims=True)
            sg = sg + jnp.sum(dgate, axis=0, keepdims=True)
        dbp_ref[0] += sv
        dbp_ref[1] += sg

        for k in range(k_taps):
            acc = jnp.zeros((rc, LANE), F32)
            for r0 in range(0, ts, rc):
                acc = acc + dvpad[pl.ds(r0, rc), :] * upad[pl.ds(HALO - (k_taps - 1) + k + r0, rc), :]
            dwrows[k:k + 1, :] = jnp.sum(acc, axis=0, keepdims=True)
        dw_ref[...] += dwrows[0:k_taps, :]

    return pl.pallas_call(
        body, name=name, grid=(d // LANE, t // ts),
        in_specs=[pl.BlockSpec((2, ts, LANE), lambda c, i: (0, i, c)),
                  pl.BlockSpec((2, HALO, LANE), lambda c, i: (0, jnp.maximum(i * hb - 1, 0), c)),
                  pl.BlockSpec((ts, LANE), lambda c, i: (i, c)),
                  pl.BlockSpec((HALO, LANE), lambda c, i: (jnp.minimum((i + 1) * hb, nhb - 1), c)),
                  pl.BlockSpec((k_taps, LANE), lambda c, i: (0, c))],
        out_specs=[pl.BlockSpec((2, ts, LANE), lambda c, i: (0, i, c)),
                   pl.BlockSpec((k_taps, LANE), lambda c, i: (0, c)),
                   pl.BlockSpec((2, 1, LANE), lambda c, i: (0, 0, c))],
        out_shape=[jax.ShapeDtypeStruct((2, t, d), BF16), jax.ShapeDtypeStruct((k_taps, d), F32),
                   jax.ShapeDtypeStruct((2, 1, d), F32)],
        scratch_shapes=[pltpu.VMEM((HALO + ts, LANE), F32), pltpu.VMEM((ts + HALO, LANE), F32),
                        pltpu.VMEM((HALO, LANE), F32)],
        compiler_params=_params("parallel", "arbitrary"),
    )(a, a, dv, dv, w)


def _pool_mix_fwd(name, x, gain, wp, scale, bias, next_gain, seq):
    t, d = x.shape
    ts = _tile(seq, 256)
    tps = seq // ts
    hb = ts // HALO
    cg = d // N_GROUPS

    def body(cur_ref, prev_ref, g_ref, w_ref, s_ref, b_ref, ng_ref, p_ref, r_ref, h_ref, hpad):
        i = pl.program_id(0)
        first = (i % tps) == 0
        g = g_ref[...]
        hpad[0:HALO, :] = jnp.where(first, 0.0, _rms(prev_ref[...], g))
        hpad[HALO:HALO + ts, :] = _rms(cur_ref[...], g)
        pos = (i % tps) * ts + lax.broadcasted_iota(jnp.int32, (ts, 1), 0)
        for gi, win in enumerate(POOL_WINDOWS):
            sl = slice(gi * cg, (gi + 1) * cg)
            own = hpad[HALO:HALO + ts, sl]
            acc = own
            for j in range(1, win):
                acc = acc + hpad[HALO - j:HALO - j + ts, sl]
            cnt = jnp.minimum(pos + 1, win).astype(F32)
            pooled = (acc / cnt - own).astype(BF16)
            p_ref[:, sl] = pooled
            mixed = jnp.dot(pooled, w_ref[gi], preferred_element_type=F32)
            r_ref[:, sl] = cur_ref[:, sl] + s_ref[:, sl] * (mixed + b_ref[:, sl])
        h_ref[...] = _rms(r_ref[...], ng_ref[...]).astype(BF16)

    row = pl.BlockSpec((ts, d), lambda i: (i, 0))
    vec = pl.BlockSpec((1, d), lambda i: (0, 0))
    return pl.pallas_call(
        body, name=name, grid=(t // ts,),
        in_specs=[row, pl.BlockSpec((HALO, d), lambda i: (jnp.maximum(i * hb - 1, 0), 0)), vec,
                  pl.BlockSpec((N_GROUPS, cg, cg), lambda i: (0, 0, 0)), vec, vec, vec],
        out_specs=[row, row, row],
        out_shape=[jax.ShapeDtypeStruct((t, d), BF16), jax.ShapeDtypeStruct((t, d), F32),
                   jax.ShapeDtypeStruct((t, d), BF16)],
        scratch_shapes=[pltpu.VMEM((HALO + ts, d), F32)],
        compiler_params=_params("parallel"),
    )(x, x, gain, wp, scale, bias, next_gain)


def _pool_mix_bwd(name, pooled, wp, dr, x, gain, scale, bias, seq):
    t, d = x.shape
    ts = _tile(seq, 256)
    tps = seq // ts
    hb = ts // HALO
    nhb = t // HALO
    cg = d // N_GROUPS

    def body(p_ref, w_ref, dr_ref, ndr_ref, x_ref, g_ref, s_ref, b_ref, dx_ref, dw_ref, ds_ref, db_ref, dg_ref,
             qpad, dh):
        i = pl.program_id(0)
        last = (i % tps) == tps - 1
        pos = (i % tps) * ts + lax.broadcasted_iota(jnp.int32, (ts, 1), 0)

        @pl.when(i == 0)
        def _():
            dw_ref[...] = jnp.zeros_like(dw_ref)
            ds_ref[...] = jnp.zeros_like(ds_ref)
            db_ref[...] = jnp.zeros_like(db_ref)
            dg_ref[...] = jnp.zeros_like(dg_ref)

        for gi, win in enumerate(POOL_WINDOWS):
            sl = slice(gi * cg, (gi + 1) * cg)
            wv = w_ref[gi]
            sc = s_ref[:, sl]
            drv = dr_ref[:, sl]
            dmx = drv * sc
            dmx16 = dmx.astype(BF16)
            pooled = p_ref[:, sl]
            dw_ref[gi] += lax.dot_general(pooled, dmx16, (TN, ((), ())), preferred_element_type=F32)
            mixed = jnp.dot(pooled, wv, preferred_element_type=F32)
            ds_ref[:, sl] += jnp.sum(drv * (mixed + b_ref[:, sl]), axis=0, keepdims=True)
            db_ref[:, sl] += jnp.sum(dmx, axis=0, keepdims=True)
            cur = lax.dot_general(dmx16, wv, (NT, ((), ())), preferred_element_type=F32)
            nxt = lax.dot_general((ndr_ref[:, sl] * sc).astype(BF16), wv, (NT, ((), ())),
                                  preferred_element_type=F32)
            qpad[0:ts, sl] = cur / jnp.minimum(pos + 1, win).astype(F32)
            qpad[ts:ts + HALO, sl] = jnp.where(last, 0.0, nxt / float(win))
            acc = -cur
            for j in range(win):
                acc = acc + qpad[j:j + ts, sl]
            dh[:, sl] = acc
        dx, dgain_term, _ = _rms_bwd_tile(dh[...], x_ref[...], g_ref[...], dr_ref[...])
        dx_ref[...] = dx
        dg_ref[...] += jnp.sum(dgain_term, axis=0, keepdims=True)

    row = pl.BlockSpec((ts, d), lambda i: (i, 0))
    vec = pl.BlockSpec((1, d), lambda i: (0, 0))
    return pl.pallas_call(
        body, name=name, grid=(t // ts,),
        in_specs=[row, pl.BlockSpec((N_GROUPS, cg, cg), lambda i: (0, 0, 0)), row,
                  pl.BlockSpec((HALO, d), lambda i: (jnp.minimum((i + 1) * hb, nhb - 1), 0)), row, vec, vec, vec],
        out_specs=[row, pl.BlockSpec((N_GROUPS, cg, cg), lambda i: (0, 0, 0)), vec, vec, vec],
        out_shape=[jax.ShapeDtypeStruct((t, d), F32), jax.ShapeDtypeStruct((N_GROUPS, cg, cg), F32)]
        + [jax.ShapeDtypeStruct((1, d), F32)] * 3,
        scratch_shapes=[pltpu.VMEM((ts + HALO, d), F32), pltpu.VMEM((ts, d), F32)],
        compiler_params=_params("arbitrary"),
    )(pooled, wp, dr, dr, x, gain, scale, bias)


def _ctile(n, pref):
    return max(c for c in range(LANE, min(pref, n) + 1, LANE) if n % c == 0)


FFN_COLS = 1408
FFN_ROWS = 32


def _ffn_fwd(name, up, w, b, seq):
    _, t, dff = up.shape
    f = _ctile(dff, FFN_COLS)
    k_taps = w.shape[0]
    ts = _tile(seq, 256)
    tps = seq // ts
    hb = ts // HALO16
    rc = _tile(ts, FFN_ROWS)

    def body(cur_ref, prev_ref, w_ref, b_ref, g_ref, apad):
        i = pl.program_id(1)
        first = (i % tps) == 0
        for ci, c0 in enumerate(range(0, f, LANE)):
            cols = slice(c0, c0 + LANE)
            apad[ci, 0:HALO16, :] = jnp.where(first, 0.0, prev_ref[:, cols].astype(F32))
            apad[ci, HALO16:HALO16 + ts, :] = cur_ref[0, :, cols].astype(F32)
            wv = w_ref[:, cols]
            wk = [jnp.broadcast_to(wv[k:k + 1, :], (rc, LANE)) for k in range(k_taps)]
            bias = jnp.broadcast_to(b_ref[:, cols], (rc, LANE))
            for r0 in range(0, ts, rc):
                c = bias
                for k in range(k_taps):
                    c = c + wk[k] * apad[ci, pl.ds(HALO16 - (k_taps - 1) + k + r0, rc), :]
                gate = cur_ref[1, pl.ds(r0, rc), cols].astype(F32)
                g_ref[pl.ds(r0, rc), cols] = (c * _sigmoid(c) * gate).astype(BF16)

    return pl.pallas_call(
        body, name=name, grid=(dff // f, t // ts),
        in_specs=[pl.BlockSpec((2, ts, f), lambda j, i: (0, i, j)),
                  pl.BlockSpec((None, HALO16, f), lambda j, i: (0, jnp.maximum(i * hb - 1, 0), j)),
                  pl.BlockSpec((k_taps, f), lambda j, i: (0, j)),
                  pl.BlockSpec((1, f), lambda j, i: (0, j))],
        out_specs=pl.BlockSpec((ts, f), lambda j, i: (i, j)),
        out_shape=jax.ShapeDtypeStruct((t, dff), BF16),
        scratch_shapes=[pltpu.VMEM((f // LANE, HALO16 + ts, LANE), F32)],
        compiler_params=_params("parallel", "parallel"),
    )(up, up, w, b)


def _ffn_bwd(name, up, dg, w, b, seq):
    _, t, dff = up.shape
    f = _ctile(dff, FFN_COLS)
    k_taps = w.shape[0]
    ts = _tile(seq, 256)
    tps = seq // ts
    hb = ts // HALO16
    nhb = t // HALO16
    ext = ts + HALO16
    rc = _tile(ts, FFN_ROWS)

    def body(cur_ref, prev_ref, next_ref, dg_ref, ndg_ref, w_ref, b_ref, dup_ref, dw_ref, db_ref, apad, dcpad):
        i = pl.program_id(1)
        first = (i % tps) == 0
        last = (i % tps) == tps - 1
        @pl.when(i == 0)
        def _():
            dw_ref[...] = jnp.zeros_like(dw_ref)
            db_ref[...] = jnp.zeros_like(db_ref)

        for ci, c0 in enumerate(range(0, f, LANE)):
            cols = slice(c0, c0 + LANE)
            apad[ci, 0:HALO16, :] = jnp.where(first, 0.0, prev_ref[:, cols].astype(F32))
            apad[ci, HALO16:HALO16 + ts, :] = cur_ref[0, :, cols].astype(F32)
            apad[ci, HALO16 + ts:HALO16 + ext, :] = next_ref[0, :, cols].astype(F32)
            wv = w_ref[:, cols]
            wk = [jnp.broadcast_to(wv[k:k + 1, :], (rc, LANE)) for k in range(k_taps)]
            bias = jnp.broadcast_to(b_ref[:, cols], (rc, LANE))

            def conv_grad(r0, n, gate, dgv):
                c = bias[0:n]
                for k in range(k_taps):
                    c = c + wk[k][0:n] * apad[ci, pl.ds(HALO16 - (k_taps - 1) + k + r0, n), :]
                sig = _sigmoid(c)
                return dgv * gate * sig * (1.0 + c * (1.0 - sig)), c * sig

            for r0 in range(0, ts, rc):
                dgv = dg_ref[pl.ds(r0, rc), cols].astype(F32)
                dc, silu = conv_grad(r0, rc, cur_ref[1, pl.ds(r0, rc), cols].astype(F32), dgv)
                dcpad[ci, pl.ds(r0, rc), :] = dc
                dup_ref[1, pl.ds(r0, rc), cols] = (dgv * silu).astype(BF16)
            dgv = jnp.where(last, 0.0, ndg_ref[:, cols].astype(F32))
            dc, _ = conv_grad(ts, HALO16, next_ref[1, :, cols].astype(F32), dgv)
            dcpad[ci, ts:ext, :] = dc

            dw_acc = [jnp.zeros((rc, LANE), F32) for _ in range(k_taps)]
            db_acc = jnp.zeros((rc, LANE), F32)
            for r0 in range(0, ts, rc):
                dact = jnp.zeros((rc, LANE), F32)
                for k in range(k_taps):
                    dact = dact + wk[k] * dcpad[ci, pl.ds(r0 + (k_taps - 1) - k, rc), :]
                dup_ref[0, pl.ds(r0, rc), cols] = dact.astype(BF16)
                dc = dcpad[ci, pl.ds(r0, rc), :]
                for k in range(k_taps):
                    dw_acc[k] = dw_acc[k] + dc * apad[ci, pl.ds(HALO16 - (k_taps - 1) + k + r0, rc), :]
                db_acc = db_acc + dc
            for k in range(k_taps):
                dw_ref[k:k + 1, cols] += jnp.sum(dw_acc[k], axis=0, keepdims=True)
            db_ref[:, cols] += jnp.sum(db_acc, axis=0, keepdims=True)

    return pl.pallas_call(
        body, name=name, grid=(dff // f, t // ts),
        in_specs=[pl.BlockSpec((2, ts, f), lambda j, i: (0, i, j)),
                  pl.BlockSpec((None, HALO16, f), lambda j, i: (0, jnp.maximum(i * hb - 1, 0), j)),
                  pl.BlockSpec((2, HALO16, f), lambda j, i: (0, jnp.minimum((i + 1) * hb, nhb - 1), j)),
                  pl.BlockSpec((ts, f), lambda j, i: (i, j)),
                  pl.BlockSpec((HALO16, f), lambda j, i: (jnp.minimum((i + 1) * hb, nhb - 1), j)),
                  pl.BlockSpec((k_taps, f), lambda j, i: (0, j)),
                  pl.BlockSpec((1, f), lambda j, i: (0, j))],
        out_specs=[pl.BlockSpec((2, ts, f), lambda j, i: (0, i, j)),
                   pl.BlockSpec((k_taps, f), lambda j, i: (0, j)),
                   pl.BlockSpec((1, f), lambda j, i: (0, j))],
        out_shape=[jax.ShapeDtypeStruct((2, t, dff), BF16), jax.ShapeDtypeStruct((k_taps, dff), F32),
                   jax.ShapeDtypeStruct((1, dff), F32)],
        scratch_shapes=[pltpu.VMEM((f // LANE, HALO16 + ext, LANE), F32), pltpu.VMEM((f // LANE, ext, LANE), F32)],
        compiler_params=_params("parallel", "arbitrary"),
    )(up, up, up, dg, dg, w, b)


def _sum_rows(name, g):
    ns, r, c = g.shape
    tr = _tile(r, 256)

    def body(g_ref, o_ref):
        acc = g_ref[0]
        for dev in range(1, ns):
            acc = acc + g_ref[dev]
        o_ref[...] = acc

    return pl.pallas_call(
        body, name=name, grid=(r // tr,),
        in_specs=[pl.BlockSpec((ns, tr, c), lambda i: (0, i, 0))],
        out_specs=pl.BlockSpec((tr, c), lambda i: (i, 0)),
        out_shape=jax.ShapeDtypeStruct((r, c), F32), compiler_params=_params("parallel"),
    )(g)


def _adamw(name, gsrc, w, m, v, layer=0, prev=None):
    ns, r, c = gsrc.shape
    nl = w.shape[0]
    tr = _tile(r, 256)
    prev = () if prev is None else tuple(prev)

    def body(g_ref, w_ref, m_ref, v_ref, *rest):
        go_ref, do_ref, mo_ref, vo_ref = rest[len(prev):]
        g = g_ref[0].astype(F32)
        for dev in range(1, ns):
            g = g + g_ref[dev].astype(F32)
        m_new = ADAM_B1 * m_ref[...] + (1.0 - ADAM_B1) * g
        v_new = ADAM_B2 * v_ref[...] + (1.0 - ADAM_B2) * (g * g)
        m_hat = m_new / (1.0 - ADAM_B1 ** ADAM_STEP)
        v_hat = v_new / (1.0 - ADAM_B2 ** ADAM_STEP)
        go_ref[...] = g
        do_ref[...] = -ADAM_LR * (m_hat / (jnp.sqrt(v_hat) + ADAM_EPS) + ADAM_WD * w_ref[...])
        mo_ref[...] = m_new
        vo_ref[...] = v_new

    row = pl.BlockSpec((None, tr, c), lambda i: (layer, i, 0))
    return pl.pallas_call(
        body, name=name, grid=(r // tr,),
        in_specs=[pl.BlockSpec((ns, tr, c), lambda i: (0, i, 0)), row, row, row] + [ANY] * len(prev),
        out_specs=[row] * 4, out_shape=[jax.ShapeDtypeStruct((nl, r, c), F32)] * 4,
        input_output_aliases={4 + i: i for i in range(len(prev))},
        compiler_params=_params("parallel"),
    )(gsrc, w, m, v, *prev)


def _ffn_forward(tag, r_in, h, get_wu, get_wd, wdw, bdw, seq, loss=None):
    t, d = r_in.shape
    tm = _tile(t, 512)
    wu = get_wu(h)
    dff = wu.shape[0] // 2
    tu = _tile(t, 1024)
    up = _mm(f"{tag}_up", h, wu, grid=(2, t // tu, 1),
             a_spec=pl.BlockSpec((tu, d), lambda j, i, k: (i, 0)),
             b_spec=pl.BlockSpec((dff, d), lambda j, i, k: (j, 0)),
             out_spec=pl.BlockSpec((None, tu, dff), lambda j, i, k: (j, i, 0)),
             out_shape=jax.ShapeDtypeStruct((2, t, dff), BF16), dims=NT, acc_shape=(tu, dff))
    wd = get_wd(up)
    g = _ffn_fwd(f"{tag}_act", up, wdw, bdw, seq)
    row = pl.BlockSpec((tm, d), lambda i, j, k: (i, 0))
    vec = pl.BlockSpec((1, d), lambda i, j, k: (0, 0))
    common = dict(grid=(t // tm, 1, 1), a_spec=pl.BlockSpec((tm, dff), lambda i, j, k: (i, 0)),
                  b_spec=pl.BlockSpec((dff, d), lambda i, j, k: (0, 0)), dims=NN, acc_shape=(tm, d))
    if loss is None:
        out = _mm(f"{tag}_down", g, wd, out_spec=row, out_shape=jax.ShapeDtypeStruct((t, d), F32),
                  extras=(r_in,), extra_specs=(row,), epilogue=lambda acc, ex, rows: ((ex[0][rows, :] + acc,), ()),
                  **common)
    else:
        def head(acc, ex, rows):
            dx, part, dgain = _loss_tile(ex[0][rows, :] + acc, ex[1][rows, :], ex[2][...])
            return (dx,), (part, dgain)

        out = _mm(f"{tag}_down", g, wd, out_spec=[row, pl.BlockSpec((1, 1), lambda i, j, k: (0, 0)), vec],
                  out_shape=[jax.ShapeDtypeStruct((t, d), F32), jax.ShapeDtypeStruct((1, 1), F32),
                             jax.ShapeDtypeStruct((1, d), F32)],
                  extras=(r_in, *loss), extra_specs=(row, row, vec), epilogue=head, n_sums=2, chunked=True, **common)
    return out, (r_in, h, up, g, wu, wd)


def _ffn_backward(tag, dr, saved, gain, wdw, bdw, seq, token=None):
    r_in, h, up, g, wu, wd = saved
    t, d = r_in.shape
    dff = wd.shape[0]
    tm = _tile(t, 512)
    tk = _tile(t, 2048)
    cw = _ctile(dff, 1408)
    nc = dff // cw
    dg = _mm(f"{tag}_dg", dr, wd, grid=(t // tm, 1, 1),
             a_spec=pl.BlockSpec((tm, d), lambda i, j, k: (i, 0)),
             b_spec=pl.BlockSpec((dff, d), lambda i, j, k: (0, 0)),
             out_spec=pl.BlockSpec((tm, dff), lambda i, j, k: (i, 0)),
             out_shape=jax.ShapeDtypeStruct((t, dff), BF16), dims=NT, acc_shape=(tm, dff), token=token)
    dwd = _mm(f"{tag}_dwd", g, dr, grid=(dff // cw, 1, t // tk),
              a_spec=pl.BlockSpec((tk, cw), lambda i, j, k: (k, i)),
              b_spec=pl.BlockSpec((tk, d), lambda i, j, k: (k, 0)),
              out_spec=pl.BlockSpec((cw, d), lambda i, j, k: (i, 0)),
              out_shape=jax.ShapeDtypeStruct((dff, d), BF16), dims=TN, acc_shape=(cw, d))
    dup, dwdw, dbdw = _ffn_bwd(f"{tag}_dact", up, dg, wdw, bdw, seq)
    row = pl.BlockSpec((tm, d), lambda i, j, k: (i, 0))
    vec = pl.BlockSpec((1, d), lambda i, j, k: (0, 0))

    def norm_backward(acc, ex, rows):
        dx, dgain, colsum = _rms_bwd_tile(acc, ex[0][rows, :], ex[1][...], ex[2][rows, :])
        return (dx,), (dgain, colsum)

    dr_in, dgain, colsum = _mm(
        f"{tag}_dh", dup, wu, grid=(t // tm, 1, 2),
        a_spec=pl.BlockSpec((None, tm, dff), lambda i, j, k: (k, i, 0)),
        b_spec=pl.BlockSpec((dff, d), lambda i, j, k: (k, 0)), out_spec=[row, vec, vec],
        out_shape=[jax.ShapeDtypeStruct((t, d), F32)] + [jax.ShapeDtypeStruct((1, d), F32)] * 2,
        dims=NN, acc_shape=(tm, d), extras=(r_in, gain, dr), extra_specs=(row, vec, row),
        epilogue=norm_backward, n_sums=2, chunked=True)
    dwu = _mm(f"{tag}_dwu", dup, h, grid=(2 * nc, 1, t // tk),
              a_spec=pl.BlockSpec((None, tk, cw), lambda i, j, k: (i // nc, k, i % nc)),
              b_spec=pl.BlockSpec((tk, d), lambda i, j, k: (k, 0)),
              out_spec=pl.BlockSpec((cw, d), lambda i, j, k: (i, 0)),
              out_shape=jax.ShapeDtypeStruct((2 * dff, d), BF16), dims=TN, acc_shape=(cw, d))
    return dr_in, dgain, dwu, dwd, dwdw, dbdw, colsum


def _pad_to(vec, n):
    return jnp.pad(vec, (0, n - vec.shape[0]))


def _pack(parts, width):
    flat = jnp.concatenate([p.reshape(-1).astype(F32) for p in parts])
    n = -(-flat.shape[0] // (8 * width)) * (8 * width)
    return _pad_to(flat, n).reshape(n // width, width)


def _unpack(mat, shapes):
    flat = mat.reshape(-1)
    out, off = [], 0
    for s in shapes:
        n = 1
        for dim in s:
            n *= dim
        out.append(flat[off:off + n].reshape(s))
        off += n
    return out


def kernel(x, norm_mix, norm_ffn, conv_w_pw1, conv_b_pw1, conv_w_dw, conv_b_dw, conv_ln_g, conv_ln_b, conv_w_pw2, conv_b_pw2, pool_w, pool_b, pool_scale, ffn_w_up, ffn_w_dw, ffn_b_dw, ffn_w_down, final_norm, loss_target, m_norm_mix, m_norm_ffn, m_conv_w_pw1, m_conv_b_pw1, m_conv_w_dw, m_conv_b_dw, m_conv_ln_g, m_conv_ln_b, m_conv_w_pw2, m_conv_b_pw2, m_pool_w, m_pool_b, m_pool_scale, m_ffn_w_up, m_ffn_w_dw, m_ffn_b_dw, m_ffn_w_down, m_final_norm, v_norm_mix, v_norm_ffn, v_conv_w_pw1, v_conv_b_pw1, v_conv_w_dw, v_conv_b_dw, v_conv_ln_g, v_conv_ln_b, v_conv_w_pw2, v_conv_b_pw2, v_pool_w, v_pool_b, v_pool_scale, v_ffn_w_up, v_ffn_w_dw, v_ffn_b_dw, v_ffn_w_down, v_final_norm):
    bsz, seq, d = x.shape
    t = bsz * seq
    k_taps = conv_w_dw.shape[1]
    cs1 = conv_w_pw1.shape[2]
    dsh = d // N_DEV
    cg = d // N_GROUPS
    cgs = pool_w.shape[2]
    fu = ffn_w_up.shape[2]
    fd = ffn_w_down.shape[1]
    dff = fd * N_DEV
    nb = N_DEV // 2
    kf = ffn_w_dw.shape[1]
    fsh = ffn_w_dw.shape[2]
    my = _lin(_me())
    tm = _tile(t, 512)

    x2 = x.reshape(t, d)
    tgt2 = loss_target.reshape(t, d)

    small_shapes = [(k_taps, dsh), (dsh,), (dsh,), (2, kf, fsh)]
    small_mine = _pack([conv_w_dw[0], pool_b[0], pool_scale[0], ffn_w_dw], LANE)
    big = [conv_w_pw1[0], conv_w_pw2[0], ffn_w_up[0].T, ffn_w_down[0], pool_w[0], ffn_w_up[1].T, ffn_w_down[1]]
    gather = _exchange_start("gather_start", [small_mine] + [w.astype(BF16) for w in big], [False] * 8)
    small_all, w1 = _exchange_wait("gather_wait_w1", gather, [0, 1], x2)
    parts = [_unpack(small_all[dev], small_shapes) for dev in range(N_DEV)]
    wdw = jnp.concatenate([p[0] for p in parts], axis=1)
    pool_b_full = jnp.concatenate([p[1] for p in parts]).reshape(1, d)
    pool_s_full = jnp.concatenate([p[2] for p in parts]).reshape(1, d)
    fwdw = jnp.concatenate([p[3] for p in parts], axis=2)
    fbdw = ffn_b_dw.reshape(2, 1, dff)

    def columns(w):
        return w.transpose(1, 0, 2).reshape(w.shape[1], N_DEV * w.shape[2])

    def column_shards(w):
        return w.reshape(w.shape[0], N_DEV, w.shape[1] // N_DEV).transpose(1, 0, 2)

    w1 = columns(w1)
    def first_norm(x_blk, ex):
        h_blk = _rms(x_blk, ex[1][...]).astype(BF16)
        return h_blk, h_blk

    a, h0 = _mm("l0_pw1", x2, w1, grid=(t // tm, 2, 1),
                a_spec=pl.BlockSpec((tm, d), lambda i, j, k: (i, 0)),
                b_spec=pl.BlockSpec((d, d), lambda i, j, k: (0, j)),
                out_spec=[pl.BlockSpec((None, tm, d), lambda i, j, k: (j, i, 0)),
                          pl.BlockSpec((tm, d), lambda i, j, k: (i, 0))],
                out_shape=[jax.ShapeDtypeStruct((2, t, d), BF16), jax.ShapeDtypeStruct((t, d), BF16)],
                dims=NN, acc_shape=(tm, d), extras=(conv_b_pw1, norm_mix[0:1]),
                extra_specs=(pl.BlockSpec((1, d), lambda i, j, k: (0, j)), pl.BlockSpec((1, d), lambda i, j, k: (0, 0))),
                prologue=first_norm, epilogue=lambda acc, ex, rows: ((acc + ex[0][...],), ()))
    (w2,) = _exchange_wait("gather_wait_w2", gather, [2], a)
    w2 = w2.reshape(d, d)
    v = _conv_fwd("l0_conv", a, wdw, conv_b_dw, seq)
    row = pl.BlockSpec((tm, d), lambda i, j, k: (i, 0))
    vec = pl.BlockSpec((1, d), lambda i, j, k: (0, 0))
    square = pl.BlockSpec((d, d), lambda i, j, k: (0, 0))

    def ln_silu(v_blk, ex):
        s_blk = _ln_silu_tile(v_blk, ex[0][...], ex[1][...]).astype(BF16)
        return s_blk, s_blk

    def residual_and_norm(acc, ex, rows):
        r_blk = ex[3][rows, :] + (acc + ex[2][...])
        return (r_blk, _rms(r_blk, ex[4][...])), ()

    r1, h1, s = _mm("l0_pw2", v, w2, grid=(t // tm, 1, 1), a_spec=row, b_spec=square, out_spec=[row, row, row],
                    out_shape=[jax.ShapeDtypeStruct((t, d), F32), jax.ShapeDtypeStruct((t, d), BF16),
                               jax.ShapeDtypeStruct((t, d), BF16)],
                    dims=NN, acc_shape=(tm, d), extras=(conv_ln_g, conv_ln_b, conv_b_pw2, x2, norm_ffn[0:1]),
                    extra_specs=(vec, vec, vec, row, vec), prologue=ln_silu, epilogue=residual_and_norm,
                    chunked=True)

    def up_getter(name, idx):
        return lambda after: _exchange_wait(name, gather, [idx], after)[0].reshape(2 * dff, d)

    def down_getter(name, idx):
        return lambda after: _exchange_wait(name, gather, [idx], after)[0].reshape(dff, d)

    r2, ffn0_saved = _ffn_forward("f0", r1, h1, up_getter("gather_wait_wu0", 3),
                                  down_getter("gather_wait_wd0", 4), fwdw[0], fbdw[0], seq)
    (wp,) = _exchange_wait("gather_wait_wp", gather, [5], r2)
    wp = wp.transpose(1, 0, 2, 3).reshape(N_GROUPS, cg, cg)
    pooled, r3, h3 = _pool_mix_fwd("l1_mix", r2, norm_mix[1:2], wp, pool_s_full, pool_b_full, norm_ffn[1:2], seq)
    (dr4, loss_part, dfinal), ffn1_saved = _ffn_forward(
        "f1", r3, h3, up_getter("gather_wait_wu1", 6), down_getter("gather_wait_wd1", 7), fwdw[1], fbdw[1], seq,
        loss=(tgt2, final_norm.reshape(1, d)))

    dr3, dnf1, dwu1, dwd1, dfw1, dfb1, _ = _ffn_backward("f1", dr4, ffn1_saved, norm_ffn[1:2], fwdw[1], fbdw[1], seq)
    scatter_a = _exchange_start("scatter_f1_start", [dwu1.reshape(N_DEV, fu, d), dwd1.reshape(N_DEV, fd, d)],
                                [True, True])
    dr2, dwp, dpool_s, dpool_b, dnm1 = _pool_mix_bwd(
        "l1_dmix", pooled, wp, dr3, r2, norm_mix[1:2], pool_s_full + scatter_a["token"][0:1, 0:1], pool_b_full, seq)
    dr1, dnf0, dwu0, dwd0, dfw0, dfb0, db2 = _ffn_backward("f0", dr2, ffn0_saved, norm_ffn[0:1], fwdw[0], fbdw[0], seq)
    dwp_b = dwp.astype(BF16).reshape(N_GROUPS, N_DEV, cgs, cg).transpose(1, 0, 2, 3)
    scatter_b = _exchange_start("scatter_f0_start", [dwu0.reshape(N_DEV, fu, d), dwd0.reshape(N_DEV, fd, d), dwp_b],
                                [True] * 3)
    def ln_silu_backward(acc, ex, rows):
        dv_blk, dgain, dbias, colsum = _ln_silu_bwd_tile(acc, ex[0][rows, :], ex[1][...], ex[2][...])
        return (dv_blk,), (dgain, dbias, colsum)

    dv, dlg, dlb, dbdw = _mm("l0_ds", dr1, w2, grid=(t // tm, 1, 1), a_spec=row, b_spec=square,
                             out_spec=[row, vec, vec, vec],
                             out_shape=[jax.ShapeDtypeStruct((t, d), F32)] + [jax.ShapeDtypeStruct((1, d), F32)] * 3,
                             dims=NT, acc_shape=(tm, d), extras=(v, conv_ln_g, conv_ln_b), extra_specs=(row, vec, vec),
                             epilogue=ln_silu_backward, n_sums=3, chunked=True, token=scatter_b["token"])
    tk = _tile(t, 2048)
    dw2 = _mm("l0_dw2", s, dr1, grid=(1, 1, t // tk),
              a_spec=pl.BlockSpec((tk, d), lambda i, j, k: (k, 0)),
              b_spec=pl.BlockSpec((tk, d), lambda i, j, k: (k, 0)),
              out_spec=pl.BlockSpec((d, d), lambda i, j, k: (0, 0)),
              out_shape=jax.ShapeDtypeStruct((d, d), BF16), dims=TN, acc_shape=(d, d))
    da, dwdw, db1 = _conv_bwd("l0_dconv", a, dv, wdw, seq)
    dw1 = _mm("l0_dw1", h0, da, grid=(1, 2, t // tk),
              a_spec=pl.BlockSpec((tk, d), lambda i, j, k: (k, 0)),
              b_spec=pl.BlockSpec((None, tk, d), lambda i, j, k: (j, k, 0)),
              out_spec=pl.BlockSpec((d, d), lambda i, j, k: (0, j)),
              out_shape=jax.ShapeDtypeStruct((d, 2 * d), BF16), dims=TN, acc_shape=(d, d))
    scatter_c = _exchange_start("scatter_l0_start", [column_shards(dw1), dw2.reshape(N_DEV, d // N_DEV, d)],
                                [True, True])
    def norm_backward(acc, ex, rows):
        dx_blk, dgain, colsum = _rms_bwd_tile(acc, ex[0][rows, :], ex[1][...], ex[2][rows, :])
        return (dx_blk,), (dgain, colsum)

    dx, dnm0, _ = _mm("l0_dh", da, w1, grid=(t // tm, 1, 2),
                      a_spec=pl.BlockSpec((None, tm, d), lambda i, j, k: (k, i, 0)),
                      b_spec=pl.BlockSpec((d, d), lambda i, j, k: (0, k)), out_spec=[row, vec, vec],
                      out_shape=[jax.ShapeDtypeStruct((t, d), F32)] + [jax.ShapeDtypeStruct((1, d), F32)] * 2,
                      dims=NT, acc_shape=(tm, d), extras=(x2, norm_mix[0:1], dr1), extra_specs=(row, vec, row),
                      epilogue=norm_backward, n_sums=2, chunked=True, token=scatter_c["token"])

    dffn_w = jnp.stack([dfw0, dfw1])
    dffn_b = jnp.stack([dfb0, dfb1]).reshape(2, dff)
    small_parts = [loss_part, jnp.concatenate([dnm0, dnm1]), jnp.concatenate([dnf0, dnf1]), db1, dwdw, dbdw, dlg, dlb,
                   db2, dpool_b, dpool_s, dffn_w, dffn_b, dfinal]
    small_part_shapes = [(1,), (2, d), (2, d), (1, 2 * d), (k_taps, d), (1, d), (1, d), (1, d), (1, d), (1, d), (1, d),
                         (2, kf, dff), (2, dff), (d,)]
    packed = _pack(small_parts, 8 * LANE)
    gather_small = _exchange_start("gather_small_start", [packed], [False])

    def big_update(name, recv, w, m, v, layer=0, prev=None):
        shape = w.shape
        c = recv.shape[-1]
        rows = recv.size // (N_DEV * c)
        nl = w.size // (rows * c)
        outs = _adamw(name, recv.reshape(N_DEV, rows, c), w.reshape(nl, rows, c), m.reshape(nl, rows, c),
                      v.reshape(nl, rows, c), layer, prev)
        return outs, [o.reshape(shape) for o in outs]

    wu_t = [p.transpose(0, 2, 1) for p in (ffn_w_up, m_ffn_w_up, v_ffn_w_up)]
    g_wu1, g_wd1 = _exchange_wait("scatter_f1_wait", scatter_a, [0, 1], gather_small["token"])
    raw_wu, _ = big_update("adam_wu1", g_wu1, *wu_t, 1)
    raw_wd, _ = big_update("adam_wd1", g_wd1, ffn_w_down, m_ffn_w_down, v_ffn_w_down, 1)
    g_wu0, g_wd0, g_wp = _exchange_wait("scatter_f0_wait", scatter_b, [0, 1, 2], raw_wd[0])
    _, u_wu = big_update("adam_wu0", g_wu0, *wu_t, 0, raw_wu)
    u_wu = [o.transpose(0, 2, 1) for o in u_wu]
    _, u_wd = big_update("adam_wd0", g_wd0, ffn_w_down, m_ffn_w_down, v_ffn_w_down, 0, raw_wd)
    _, u_wp = big_update("adam_wp", g_wp, pool_w, m_pool_w, v_pool_w)
    g_w1, g_w2 = _exchange_wait("scatter_l0_wait", scatter_c, [0, 1], u_wp[0])
    _, u_w1 = big_update("adam_w1", g_w1, conv_w_pw1, m_conv_w_pw1, v_conv_w_pw1)
    _, u_w2 = big_update("adam_w2", g_w2, conv_w_pw2, m_conv_w_pw2, v_conv_w_pw2)
    (all_small,) = _exchange_wait("gather_small_wait", gather_small, [0], u_w2[0])
    summed = _sum_rows("sum_small_grads", all_small)
    (loss_v, g_nm, g_nf, g_b1, g_wdw, g_bdw, g_lg, g_lb, g_b2, g_pb, g_ps, g_fw, g_fb,
     g_fin) = _unpack(summed, small_part_shapes)
    loss = loss_v[0]
    g_wdw_mine = lax.dynamic_slice_in_dim(g_wdw, my * dsh, dsh, axis=1)[None]
    g_pb_mine = lax.dynamic_slice_in_dim(g_pb, my * dsh, dsh, axis=1)
    g_ps_mine = lax.dynamic_slice_in_dim(g_ps, my * dsh, dsh, axis=1)
    g_fw_mine = lax.dynamic_slice_in_dim(g_fw, my * fsh, fsh, axis=2)

    small_g =[g_nm, g_nf, g_b1, g_wdw_mine, g_bdw, g_lg, g_lb, g_b2, g_pb_mine, g_ps_mine, g_fw_mine, g_fb, g_fin]
    small_w = [norm_mix, norm_ffn, conv_b_pw1, conv_w_dw, conv_b_dw, conv_ln_g, conv_ln_b, conv_b_pw2, pool_b,
               pool_scale, ffn_w_dw, ffn_b_dw, final_norm]
    small_m = [m_norm_mix, m_norm_ffn, m_conv_b_pw1, m_conv_w_dw, m_conv_b_dw, m_conv_ln_g, m_conv_ln_b,
               m_conv_b_pw2, m_pool_b, m_pool_scale, m_ffn_w_dw, m_ffn_b_dw, m_final_norm]
    small_v = [v_norm_mix, v_norm_ffn, v_conv_b_pw1, v_conv_w_dw, v_conv_b_dw, v_conv_ln_g, v_conv_ln_b,
               v_conv_b_pw2, v_pool_b, v_pool_scale, v_ffn_w_dw, v_ffn_b_dw, v_final_norm]
    shapes = [w.shape for w in small_w]
    outs = _adamw("adam_small", _pack(small_g, 8 * LANE)[None], _pack(small_w, 8 * LANE)[None],
                  _pack(small_m, 8 * LANE)[None], _pack(small_v, 8 * LANE)[None])
    sg, sd, sm, sv = [_unpack(o, shapes) for o in outs]

    def leaf(kind):
        (nm, nf, b1, wdw_, bdw_, lg, lb, b2, pb, ps, fw, fb, fin) = (sg, sd, sm, sv)[kind]
        return [nm, nf, u_w1[kind], b1, wdw_, bdw_, lg, lb, u_w2[kind], b2, u_wp[kind], pb, ps, u_wu[kind], fw, fb,
                u_wd[kind], fin]

    return (loss, dx.reshape(bsz, seq, d), *leaf(0), *leaf(1), *leaf(2), *leaf(3))
```

```python
import functools

import jax
import jax.numpy as jnp
from jax import lax
from jax.experimental import pallas as pl
from jax.experimental.pallas import tpu as pltpu

F32 = jnp.float32
BF16 = jnp.bfloat16
MESH = pl.DeviceIdType.MESH
HBM = pl.BlockSpec(memory_space=pltpu.HBM)

N_DEV = 8
RMS_EPS = 1e-6
LN_EPS = 1e-5
POOL_WINDOWS = (2, 4, 8, 16)
N_GROUPS = len(POOL_WINDOWS)
ADAM_LR = 0.001
ADAM_B1 = 0.9
ADAM_B2 = 0.999
ADAM_EPS = 1e-08
ADAM_WD = 0.01
ADAM_STEP = 10

LANE = 128
HALO = 32
HALO16 = 16
VMEM_LIMIT = 56 * 1024 * 1024


def _params(*sem):
    return pltpu.CompilerParams(dimension_semantics=sem if sem else None, vmem_limit_bytes=VMEM_LIMIT)


def _tile(n, pref):
    for t in range(min(pref, n), 15, -1):
        if n % t == 0 and t % 16 == 0:
            return t
    return n


def _sigmoid(z):
    return 1.0 / (1.0 + jnp.exp(-z))


def _me():
    return lax.axis_index("x"), lax.axis_index("y"), lax.axis_index("c")


def _flip(pos, m):
    x, y, c = pos
    return ((1 - x) if m & 4 else x, (1 - y) if m & 2 else y, (1 - c) if m & 1 else c)


def _lin(pos):
    return 4 * pos[0] + 2 * pos[1] + pos[2]


SEM = pl.BlockSpec(memory_space=pltpu.SEMAPHORE)
ANY = pl.BlockSpec(memory_space=pl.ANY)
EFFECT = pltpu.SideEffectType.DATAFLOW_SIDE_EFFECTING


def _exchange_copies(srcs, lands, send_sems, recv_sems, modes, which, starting):
    me = _me()
    my = _lin(me)
    out = []
    for pos, a in enumerate(which):
        src, land = srcs[pos], lands[pos]

        def block(pid, src=src, a=a):
            return src.at[pid] if modes[a] else src

        local = pltpu.make_async_copy(block(my), land.at[my], send_sems.at[a * N_DEV])
        remote = []
        for m in range(1, N_DEV):
            peer = _flip(me, m)
            pid = _lin(peer)
            sems = dict(send_sem=send_sems.at[a * N_DEV + m], recv_sem=recv_sems.at[a * N_DEV + m],
                        device_id=peer, device_id_type=MESH)
            if starting:
                remote.append(pltpu.make_async_remote_copy(src_ref=block(pid), dst_ref=land.at[my], **sems))
            else:
                remote.append((pltpu.make_async_remote_copy(src_ref=block(pid), dst_ref=land.at[my], **sems),
                               pltpu.make_async_remote_copy(src_ref=block(pid), dst_ref=land.at[pid], **sems)))
        out.append((local, remote))
    return out


def _exchange_start(name, arrs, modes):
    n = len(arrs)
    blocks = [a.shape[1:] if md else a.shape for a, md in zip(arrs, modes)]

    def body(*refs):
        srcs, lands = refs[:n], refs[n:2 * n]
        send_sems, recv_sems = refs[2 * n], refs[2 * n + 1]
        token = refs[-1]
        for local, remote in _exchange_copies(srcs, lands, send_sems, recv_sems, modes, list(range(n)), True):
            local.start()
            for send in remote:
                send.start()
        token[...] = jnp.zeros_like(token)

    lands = [lax.empty((N_DEV,) + tuple(b), a.dtype) for a, b in zip(arrs, blocks)]
    outs = pl.pallas_call(
        body, name=name,
        out_shape=(pltpu.SemaphoreType.DMA((n * N_DEV,)), pltpu.SemaphoreType.DMA((n * N_DEV,)),
                   *[pltpu.HBM(a.shape, a.dtype) for a in arrs], *[pltpu.HBM(l.shape, l.dtype) for l in lands],
                   jax.ShapeDtypeStruct((8, LANE), F32)),
        in_specs=[HBM] * (2 * n),
        out_specs=(SEM, SEM, *[HBM] * (2 * n), pl.BlockSpec(memory_space=pltpu.VMEM)),
        input_output_aliases={i: 2 + i for i in range(2 * n)},
        compiler_params=pltpu.CompilerParams(has_side_effects=EFFECT),
    )(*[pltpu.with_memory_space_constraint(a, pltpu.HBM) for a in arrs],
      *[pltpu.with_memory_space_constraint(l, pltpu.HBM) for l in lands])
    return dict(send=outs[0], recv=outs[1], srcs=list(outs[2:2 + n]), lands=list(outs[2 + n:2 + 2 * n]),
                modes=modes, token=outs[-1])


def _exchange_wait(name, handle, which, after):
    k = len(which)
    modes = handle["modes"]

    def body(*refs):
        srcs, lands = refs[:k], refs[k:2 * k]
        send_sems, recv_sems = refs[2 * k], refs[2 * k + 1]
        for local, remote in _exchange_copies(srcs, lands, send_sems, recv_sems, modes, which, False):
            local.wait()
            for send, arrival in remote:
                send.wait_send()
                arrival.wait_recv()

    srcs = [handle["srcs"][a] for a in which]
    lands = [handle["lands"][a] for a in which]
    outs = pl.pallas_call(
        body, name=name,
        out_shape=tuple(pltpu.HBM(x.shape, x.dtype) for x in srcs + lands),
        in_specs=[HBM] * (2 * k) + [SEM, SEM, ANY], out_specs=tuple([HBM] * (2 * k)),
        input_output_aliases={i: i for i in range(2 * k)},
        compiler_params=pltpu.CompilerParams(has_side_effects=EFFECT),
    )(*srcs, *lands, handle["send"], handle["recv"], after)
    return list(outs[k:])


def _mm(name, a, b, *, grid, a_spec, b_spec, out_spec, out_shape, dims, acc_shape, extras=(), extra_specs=(),
        epilogue=None, token=None, prologue=None, n_sums=0):
    nk = grid[2]
    ne = len(extras)
    deps = () if token is None else (token,)
    dep_specs = [pl.BlockSpec((8, LANE), lambda i, j, k: (0, 0))] * len(deps)
    n_out = len(out_shape) if isinstance(out_shape, (list, tuple)) else 1
    n_tiles = n_out - n_sums - (1 if prologue is not None else 0)

    def body(a_ref, b_ref, *rest):
        ex, o_refs, acc_ref = rest[:ne], rest[ne + len(deps):ne + len(deps) + n_out], rest[ne + len(deps) + n_out]
        k = pl.program_id(2)
        a_blk, saved = a_ref[...], None
        if prologue is not None:
            a_blk, saved = prologue(a_blk, ex)
            o_refs[n_tiles][...] = saved
        part = lax.dot_general(a_blk.astype(BF16), b_ref[...].astype(BF16), (dims, ((), ())),
                               preferred_element_type=F32)
        sum_refs = o_refs[n_out - n_sums:]

        def add_sums(terms):
            @pl.when((pl.program_id(0) == 0) & (pl.program_id(1) == 0))
            def _():
                for o_ref in sum_refs:
                    o_ref[...] = jnp.zeros_like(o_ref)

            for o_ref, term in zip(sum_refs, terms):
                o_ref[...] += jnp.sum(term, axis=0, keepdims=True)

        def finish(r):
            tiles, terms = ((r,), ()) if epilogue is None else epilogue(r, ex, slice(None))
            for o_ref, val in zip(o_refs, tiles):
                o_ref[...] = val.astype(o_ref.dtype)
            if n_sums:
                add_sums(terms)

        if nk == 1:
            finish(part)
            return

        @pl.when(k == 0)
        def _():
            acc_ref[...] = part

        @pl.when((k > 0) & (k < nk - 1))
        def _():
            acc_ref[...] += part

        @pl.when(k == nk - 1)
        def _():
            finish(acc_ref[...] + part)

    return pl.pallas_call(
        body, name=name, grid=grid, in_specs=[a_spec, b_spec, *extra_specs, *dep_specs], out_specs=out_spec,
        out_shape=out_shape, scratch_shapes=[pltpu.VMEM(acc_shape if nk > 1 else (8, LANE), F32)],
        compiler_params=_params(*(("arbitrary",) * 3 if n_sums else ("parallel", "parallel", "arbitrary"))),
    )(a, b, *extras, *deps)


def _rms(x, gain):
    return x * lax.rsqrt(jnp.mean(x * x, axis=-1, keepdims=True) + RMS_EPS) * gain


def _rms_bwd_tile(dh, x, gain, dres):
    rstd = lax.rsqrt(jnp.mean(x * x, axis=-1, keepdims=True) + RMS_EPS)
    xhat = x * rstd
    dxhat = dh * gain
    dx = dres + rstd * (dxhat - xhat * jnp.mean(dxhat * xhat, axis=-1, keepdims=True))
    return dx, dh * xhat, dx


def _ln_silu_tile(v, g, b):
    mu = jnp.mean(v, axis=-1, keepdims=True)
    cen = v - mu
    z = cen * lax.rsqrt(jnp.mean(cen * cen, axis=-1, keepdims=True) + LN_EPS) * g + b
    return z * _sigmoid(z)


def _ln_silu_bwd_tile(ds, v, g, b):
    mu = jnp.mean(v, axis=-1, keepdims=True)
    cen = v - mu
    rstd = lax.rsqrt(jnp.mean(cen * cen, axis=-1, keepdims=True) + LN_EPS)
    y = cen * rstd
    z = y * g + b
    sig = _sigmoid(z)
    dz = ds * sig * (1.0 + z * (1.0 - sig))
    dy = dz * g
    dv = rstd * (dy - jnp.mean(dy, axis=-1, keepdims=True) - y * jnp.mean(dy * y, axis=-1, keepdims=True))
    return dv, dz * y, dz, dv


def _loss_tile(x, tgt, gain):
    d = x.shape[-1]
    rstd = lax.rsqrt(jnp.mean(x * x, axis=-1, keepdims=True) + RMS_EPS)
    xhat = x * rstd
    err = xhat * gain - tgt
    dy = err / d
    dxhat = dy * gain
    dx = rstd * (dxhat - xhat * jnp.mean(dxhat * xhat, axis=-1, keepdims=True))
    return dx, 0.5 * jnp.mean(err * err, axis=-1, keepdims=True), dy * xhat


NN = ((1,), (0,))
NT = ((1,), (1,))
TN = ((0,), (0,))


def _rms_fwd(name, x, gain):
    t, d = x.shape
    tr = _tile(t, 512)

    def body(x_ref, g_ref, h_ref):
        h_ref[...] = _rms(x_ref[...], g_ref[...]).astype(BF16)

    return pl.pallas_call(
        body, name=name, grid=(t // tr,),
        in_specs=[pl.BlockSpec((tr, d), lambda i: (i, 0)), pl.BlockSpec((1, d), lambda i: (0, 0))],
        out_specs=pl.BlockSpec((tr, d), lambda i: (i, 0)),
        out_shape=jax.ShapeDtypeStruct((t, d), BF16), compiler_params=_params("parallel"),
    )(x, gain)


def _conv_tiles(t, seq):
    ts = _tile(seq, 512)
    return ts, seq // ts, _tile(ts, 64)


def _conv_fwd(name, a, w, b, seq):
    _, t, d = a.shape
    k_taps = w.shape[0]
    ts, tps, rc = _conv_tiles(t, seq)
    hb = ts // HALO

    def body(cur_ref, prev_ref, w_ref, b_ref, v_ref, upad):
        i = pl.program_id(1)
        first = (i % tps) == 0
        pv = prev_ref[0].astype(F32)
        pg = prev_ref[1].astype(F32)
        upad[0:HALO, :] = jnp.where(first, 0.0, pv * _sigmoid(pg))
        upad[HALO:HALO + ts, :] = cur_ref[0].astype(F32) * _sigmoid(cur_ref[1].astype(F32))
        wv = w_ref[...]
        bias = jnp.broadcast_to(b_ref[...], (rc, LANE))
        for r0 in range(0, ts, rc):
            acc = bias
            for k in range(k_taps):
                acc = acc + wv[k:k + 1, :] * upad[pl.ds(HALO - (k_taps - 1) + k + r0, rc), :]
            v_ref[pl.ds(r0, rc), :] = acc

    return pl.pallas_call(
        body, name=name, grid=(d // LANE, t // ts),
        in_specs=[pl.BlockSpec((2, ts, LANE), lambda c, i: (0, i, c)),
                  pl.BlockSpec((2, HALO, LANE), lambda c, i: (0, jnp.maximum(i * hb - 1, 0), c)),
                  pl.BlockSpec((k_taps, LANE), lambda c, i: (0, c)),
                  pl.BlockSpec((1, LANE), lambda c, i: (0, c))],
        out_specs=pl.BlockSpec((ts, LANE), lambda c, i: (i, c)),
        out_shape=jax.ShapeDtypeStruct((t, d), F32),
        scratch_shapes=[pltpu.VMEM((HALO + ts, LANE), F32)],
        compiler_params=_params("parallel", "parallel"),
    )(a, a, w, b)


def _conv_bwd(name, a, dv, w, seq):
    _, t, d = a.shape
    k_taps = w.shape[0]
    ts, tps, rc = _conv_tiles(t, seq)
    hb = ts // HALO
    nhb = t // HALO

    def body(cur_ref, prev_ref, dv_ref, ndv_ref, w_ref, da_ref, dw_ref, dbp_ref, upad, dvpad, dwrows):
        i = pl.program_id(1)
        first = (i % tps) == 0
        last = (i % tps) == tps - 1
        pv = prev_ref[0].astype(F32)
        pg = prev_ref[1].astype(F32)
        upad[0:HALO, :] = jnp.where(first, 0.0, pv * _sigmoid(pg))
        upad[HALO:HALO + ts, :] = cur_ref[0].astype(F32) * _sigmoid(cur_ref[1].astype(F32))
        dvpad[0:ts, :] = dv_ref[...]
        dvpad[ts:ts + HALO, :] = jnp.where(last, 0.0, ndv_ref[...])
        wv = w_ref[...]

        @pl.when(i == 0)
        def _():
            dw_ref[...] = jnp.zeros_like(dw_ref)
            dbp_ref[...] = jnp.zeros_like(dbp_ref)

        sv = jnp.zeros((1, LANE), F32)
        sg = jnp.zeros((1, LANE), F32)
        for r0 in range(0, ts, rc):
            du = jnp.zeros((rc, LANE), F32)
            for k in range(k_taps):
                du = du + wv[k:k + 1, :] * dvpad[pl.ds(r0 + (k_taps - 1) - k, rc), :]
            av = cur_ref[0, pl.ds(r0, rc), :].astype(F32)
            sig = _sigmoid(cur_ref[1, pl.ds(r0, rc), :].astype(F32))
            dval = du * sig
            dgate = du * av * sig * (1.0 - sig)
            da_ref[0, pl.ds(r0, rc), :] = dval.astype(BF16)
            da_ref[1, pl.ds(r0, rc), :] = dgate.astype(BF16)
            sv = sv + jnp.sum(dval, axis=0, keepdims=True)
            sg = sg + jnp.sum(dgate, axis=0, keepdims=True)
        dbp_ref[0] += sv
        dbp_ref[1] += sg

        for k in range(k_taps):
            acc = jnp.zeros((rc, LANE), F32)
            for r0 in range(0, ts, rc):
                acc = acc + dvpad[pl.ds(r0, rc), :] * upad[pl.ds(HALO - (k_taps - 1) + k + r0, rc), :]
            dwrows[k:k + 1, :] = jnp.sum(acc, axis=0, keepdims=True)
        dw_ref[...] += dwrows[0:k_taps, :]

    return pl.pallas_call(
        body, name=name, grid=(d // LANE, t // ts),
        in_specs=[pl.BlockSpec((2, ts, LANE), lambda c, i: (0, i, c)),
                  pl.BlockSpec((2, HALO, LANE), lambda c, i: (0, jnp.maximum(i * hb - 1, 0), c)),
                  pl.BlockSpec((ts, LANE), lambda c, i: (i, c)),
                  pl.BlockSpec((HALO, LANE), lambda c, i: (jnp.minimum((i + 1) * hb, nhb - 1), c)),
                  pl.BlockSpec((k_taps, LANE), lambda c, i: (0, c))],
        out_specs=[pl.BlockSpec((2, ts, LANE), lambda c, i: (0, i, c)),
                   pl.BlockSpec((k_taps, LANE), lambda c, i: (0, c)),
                   pl.BlockSpec((2, 1, LANE), lambda c, i: (0, 0, c))],
        out_shape=[jax.ShapeDtypeStruct((2, t, d), BF16), jax.ShapeDtypeStruct((k_taps, d), F32),
                   jax.ShapeDtypeStruct((2, 1, d), F32)],
        scratch_shapes=[pltpu.VMEM((HALO + ts, LANE), F32), pltpu.VMEM((ts + HALO, LANE), F32),
                        pltpu.VMEM((HALO, LANE), F32)],
        compiler_params=_params("parallel", "arbitrary"),
    )(a, a, dv, dv, w)


def _pool_mix_fwd(name, x, gain, wp, scale, bias, next_gain, seq):
    t, d = x.shape
    ts = _tile(seq, 256)
    tps = seq // ts
    hb = ts // HALO
    cg = d // N_GROUPS

    def body(cur_ref, prev_ref, g_ref, w_ref, s_ref, b_ref, ng_ref, p_ref, r_ref, h_ref, hpad):
        i = pl.program_id(0)
        first = (i % tps) == 0
        g = g_ref[...]
        hpad[0:HALO, :] = jnp.where(first, 0.0, _rms(prev_ref[...], g))
        hpad[HALO:HALO + ts, :] = _rms(cur_ref[...], g)
        pos = (i % tps) * ts + lax.broadcasted_iota(jnp.int32, (ts, 1), 0)
        for gi, win in enumerate(POOL_WINDOWS):
            sl = slice(gi * cg, (gi + 1) * cg)
            own = hpad[HALO:HALO + ts, sl]
            acc = own
            for j in range(1, win):
                acc = acc + hpad[HALO - j:HALO - j + ts, sl]
            cnt = jnp.minimum(pos + 1, win).astype(F32)
            pooled = (acc / cnt - own).astype(BF16)
            p_ref[:, sl] = pooled
            mixed = jnp.dot(pooled, w_ref[gi], preferred_element_type=F32)
            r_ref[:, sl] = cur_ref[:, sl] + s_ref[:, sl] * (mixed + b_ref[:, sl])
        h_ref[...] = _rms(r_ref[...], ng_ref[...]).astype(BF16)

    row = pl.BlockSpec((ts, d), lambda i: (i, 0))
    vec = pl.BlockSpec((1, d), lambda i: (0, 0))
    return pl.pallas_call(
        body, name=name, grid=(t // ts,),
        in_specs=[row, pl.BlockSpec((HALO, d), lambda i: (jnp.maximum(i * hb - 1, 0), 0)), vec,
                  pl.BlockSpec((N_GROUPS, cg, cg), lambda i: (0, 0, 0)), vec, vec, vec],
        out_specs=[row, row, row],
        out_shape=[jax.ShapeDtypeStruct((t, d), BF16), jax.ShapeDtypeStruct((t, d), F32),
                   jax.ShapeDtypeStruct((t, d), BF16)],
        scratch_shapes=[pltpu.VMEM((HALO + ts, d), F32)],
        compiler_params=_params("parallel"),
    )(x, x, gain, wp, scale, bias, next_gain)


def _pool_mix_bwd(name, pooled, wp, dr, x, gain, scale, bias, seq):
    t, d = x.shape
    ts = _tile(seq, 256)
    tps = seq // ts
    hb = ts // HALO
    nhb = t // HALO
    cg = d // N_GROUPS

    def body(p_ref, w_ref, dr_ref, ndr_ref, x_ref, g_ref, s_ref, b_ref, dx_ref, dw_ref, ds_ref, db_ref, dg_ref,
             qpad, dh):
        i = pl.program_id(0)
        last = (i % tps) == tps - 1
        pos = (i % tps) * ts + lax.broadcasted_iota(jnp.int32, (ts, 1), 0)

        @pl.when(i == 0)
        def _():
            dw_ref[...] = jnp.zeros_like(dw_ref)
            ds_ref[...] = jnp.zeros_like(ds_ref)
            db_ref[...] = jnp.zeros_like(db_ref)
            dg_ref[...] = jnp.zeros_like(dg_ref)

        for gi, win in enumerate(POOL_WINDOWS):
            sl = slice(gi * cg, (gi + 1) * cg)
            wv = w_ref[gi]
            sc = s_ref[:, sl]
            drv = dr_ref[:, sl]
            dmx = drv * sc
            dmx16 = dmx.astype(BF16)
            pooled = p_ref[:, sl]
            dw_ref[gi] += lax.dot_general(pooled, dmx16, (TN, ((), ())), preferred_element_type=F32)
            mixed = jnp.dot(pooled, wv, preferred_element_type=F32)
            ds_ref[:, sl] += jnp.sum(drv * (mixed + b_ref[:, sl]), axis=0, keepdims=True)
            db_ref[:, sl] += jnp.sum(dmx, axis=0, keepdims=True)
            cur = lax.dot_general(dmx16, wv, (NT, ((), ())), preferred_element_type=F32)
            nxt = lax.dot_general((ndr_ref[:, sl] * sc).astype(BF16), wv, (NT, ((), ())),
                                  preferred_element_type=F32)
            qpad[0:ts, sl] = cur / jnp.minimum(pos + 1, win).astype(F32)
            qpad[ts:ts + HALO, sl] = jnp.where(last, 0.0, nxt / float(win))
            acc = -cur
            for j in range(win):
                acc = acc + qpad[j:j + ts, sl]
            dh[:, sl] = acc
        dx, dgain_term, _ = _rms_bwd_tile(dh[...], x_ref[...], g_ref[...], dr_ref[...])
        dx_ref[...] = dx
        dg_ref[...] += jnp.sum(dgain_term, axis=0, keepdims=True)

    row = pl.BlockSpec((ts, d), lambda i: (i, 0))
    vec = pl.BlockSpec((1, d), lambda i: (0, 0))
    return pl.pallas_call(
        body, name=name, grid=(t // ts,),
        in_specs=[row, pl.BlockSpec((N_GROUPS, cg, cg), lambda i: (0, 0, 0)), row,
                  pl.BlockSpec((HALO, d), lambda i: (jnp.minimum((i + 1) * hb, nhb - 1), 0)), row, vec, vec, vec],
        out_specs=[row, pl.BlockSpec((N_GROUPS, cg, cg), lambda i: (0, 0, 0)), vec, vec, vec],
        out_shape=[jax.ShapeDtypeStruct((t, d), F32), jax.ShapeDtypeStruct((N_GROUPS, cg, cg), F32)]
        + [jax.ShapeDtypeStruct((1, d), F32)] * 3,
        scratch_shapes=[pltpu.VMEM((ts + HALO, d), F32), pltpu.VMEM((ts, d), F32)],
        compiler_params=_params("arbitrary"),
    )(pooled, wp, dr, dr, x, gain, scale, bias)


def _ctile(n, pref):
    return max(c for c in range(LANE, min(pref, n) + 1, LANE) if n % c == 0)


FFN_COLS = 1408
FFN_ROWS = 32


def _ffn_fwd(name, up, w, b, seq):
    _, t, dff = up.shape
    f = _ctile(dff, FFN_COLS)
    k_taps = w.shape[0]
    ts = _tile(seq, 256)
    tps = seq // ts
    hb = ts // HALO16
    rc = _tile(ts, FFN_ROWS)

    def body(cur_ref, prev_ref, w_ref, b_ref, g_ref, apad):
        i = pl.program_id(1)
        first = (i % tps) == 0
        for ci, c0 in enumerate(range(0, f, LANE)):
            cols = slice(c0, c0 + LANE)
            apad[ci, 0:HALO16, :] = jnp.where(first, 0.0, prev_ref[:, cols].astype(F32))
            apad[ci, HALO16:HALO16 + ts, :] = cur_ref[0, :, cols].astype(F32)
            wv = w_ref[:, cols]
            wk = [jnp.broadcast_to(wv[k:k + 1, :], (rc, LANE)) for k in range(k_taps)]
            bias = jnp.broadcast_to(b_ref[:, cols], (rc, LANE))
            for r0 in range(0, ts, rc):
                c = bias
                for k in range(k_taps):
                    c = c + wk[k] * apad[ci, pl.ds(HALO16 - (k_taps - 1) + k + r0, rc), :]
                gate = cur_ref[1, pl.ds(r0, rc), cols].astype(F32)
                g_ref[pl.ds(r0, rc), cols] = (c * _sigmoid(c) * gate).astype(BF16)

    return pl.pallas_call(
        body, name=name, grid=(dff // f, t // ts),
        in_specs=[pl.BlockSpec((2, ts, f), lambda j, i: (0, i, j)),
                  pl.BlockSpec((None, HALO16, f), lambda j, i: (0, jnp.maximum(i * hb - 1, 0), j)),
                  pl.BlockSpec((k_taps, f), lambda j, i: (0, j)),
                  pl.BlockSpec((1, f), lambda j, i: (0, j))],
        out_specs=pl.BlockSpec((ts, f), lambda j, i: (i, j)),
        out_shape=jax.ShapeDtypeStruct((t, dff), BF16),
        scratch_shapes=[pltpu.VMEM((f // LANE, HALO16 + ts, LANE), F32)],
        compiler_params=_params("parallel", "parallel"),
    )(up, up, w, b)


def _ffn_bwd(name, up, dg, w, b, seq):
    _, t, dff = up.shape
    f = _ctile(dff, FFN_COLS)
    k_taps = w.shape[0]
    ts = _tile(seq, 256)
    tps = seq // ts
    hb = ts // HALO16
    nhb = t // HALO16
    ext = ts + HALO16
    rc = _tile(ts, FFN_ROWS)

    def body(cur_ref, prev_ref, next_ref, dg_ref, ndg_ref, w_ref, b_ref, dup_ref, dw_ref, db_ref, apad, dcpad):
        i = pl.program_id(1)
        first = (i % tps) == 0
        last = (i % tps) == tps - 1
        @pl.when(i == 0)
        def _():
            dw_ref[...] = jnp.zeros_like(dw_ref)
            db_ref[...] = jnp.zeros_like(db_ref)

        for ci, c0 in enumerate(range(0, f, LANE)):
            cols = slice(c0, c0 + LANE)
            apad[ci, 0:HALO16, :] = jnp.where(first, 0.0, prev_ref[:, cols].astype(F32))
            apad[ci, HALO16:HALO16 + ts, :] = cur_ref[0, :, cols].astype(F32)
            apad[ci, HALO16 + ts:HALO16 + ext, :] = next_ref[0, :, cols].astype(F32)
            wv = w_ref[:, cols]
            wk = [jnp.broadcast_to(wv[k:k + 1, :], (rc, LANE)) for k in range(k_taps)]
            bias = jnp.broadcast_to(b_ref[:, cols], (rc, LANE))

            def conv_grad(r0, n, gate, dgv):
                c = bias[0:n]
                for k in range(k_taps):
                    c = c + wk[k][0:n] * apad[ci, pl.ds(HALO16 - (k_taps - 1) + k + r0, n), :]
                sig = _sigmoid(c)
                return dgv * gate * sig * (1.0 + c * (1.0 - sig)), c * sig

            for r0 in range(0, ts, rc):
                dgv = dg_ref[pl.ds(r0, rc), cols].astype(F32)
                dc, silu = conv_grad(r0, rc, cur_ref[1, pl.ds(r0, rc), cols].astype(F32), dgv)
                dcpad[ci, pl.ds(r0, rc), :] = dc
                dup_ref[1, pl.ds(r0, rc), cols] = (dgv * silu).astype(BF16)
            dgv = jnp.where(last, 0.0, ndg_ref[:, cols].astype(F32))
            dc, _ = conv_grad(ts, HALO16, next_ref[1, :, cols].astype(F32), dgv)
            dcpad[ci, ts:ext, :] = dc

            dw_acc = [jnp.zeros((rc, LANE), F32) for _ in range(k_taps)]
            db_acc = jnp.zeros((rc, LANE), F32)
            for r0 in range(0, ts, rc):
                dact = jnp.zeros((rc, LANE), F32)
                for k in range(k_taps):
                    dact = dact + wk[k] * dcpad[ci, pl.ds(r0 + (k_taps - 1) - k, rc), :]
                dup_ref[0, pl.ds(r0, rc), cols] = dact.astype(BF16)
                dc = dcpad[ci, pl.ds(r0, rc), :]
                for k in range(k_taps):
                    dw_acc[k] = dw_acc[k] + dc * apad[ci, pl.ds(HALO16 - (k_taps - 1) + k + r0, rc), :]
                db_acc = db_acc + dc
            for k in range(k_taps):
                dw_ref[k:k + 1, cols] += jnp.sum(dw_acc[k], axis=0, keepdims=True)
            db_ref[:, cols] += jnp.sum(db_acc, axis=0, keepdims=True)

    return pl.pallas_call(
        body, name=name, grid=(dff // f, t // ts),
        in_specs=[pl.BlockSpec((2, ts, f), lambda j, i: (0, i, j)),
                  pl.BlockSpec((None, HALO16, f), lambda j, i: (0, jnp.maximum(i * hb - 1, 0), j)),
                  pl.BlockSpec((2, HALO16, f), lambda j, i: (0, jnp.minimum((i + 1) * hb, nhb - 1), j)),
                  pl.BlockSpec((ts, f), lambda j, i: (i, j)),
                  pl.BlockSpec((HALO16, f), lambda j, i: (jnp.minimum((i + 1) * hb, nhb - 1), j)),
                  pl.BlockSpec((k_taps, f), lambda j, i: (0, j)),
                  pl.BlockSpec((1, f), lambda j, i: (0, j))],
        out_specs=[pl.BlockSpec((2, ts, f), lambda j, i: (0, i, j)),
                   pl.BlockSpec((k_taps, f), lambda j, i: (0, j)),
                   pl.BlockSpec((1, f), lambda j, i: (0, j))],
        out_shape=[jax.ShapeDtypeStruct((2, t, dff), BF16), jax.ShapeDtypeStruct((k_taps, dff), F32),
                   jax.ShapeDtypeStruct((1, dff), F32)],
        scratch_shapes=[pltpu.VMEM((f // LANE, HALO16 + ext, LANE), F32), pltpu.VMEM((f // LANE, ext, LANE), F32)],
        compiler_params=_params("parallel", "arbitrary"),
    )(up, up, up, dg, dg, w, b)


def _sum_rows(name, g):
    ns, r, c = g.shape
    tr = _tile(r, 256)

    def body(g_ref, o_ref):
        acc = g_ref[0]
        for dev in range(1, ns):
            acc = acc + g_ref[dev]
        o_ref[...] = acc

    return pl.pallas_call(
        body, name=name, grid=(r // tr,),
        in_specs=[pl.BlockSpec((ns, tr, c), lambda i: (0, i, 0))],
        out_specs=pl.BlockSpec((tr, c), lambda i: (i, 0)),
        out_shape=jax.ShapeDtypeStruct((r, c), F32), compiler_params=_params("parallel"),
    )(g)


def _adamw(name, gsrc, w, m, v, layer=0, prev=None):
    ns, r, c = gsrc.shape
    nl = w.shape[0]
    tr = _tile(r, 256)
    prev = () if prev is None else tuple(prev)

    def body(g_ref, w_ref, m_ref, v_ref, *rest):
        go_ref, do_ref, mo_ref, vo_ref = rest[len(prev):]
        g = g_ref[0].astype(F32)
        for dev in range(1, ns):
            g = g + g_ref[dev].astype(F32)
        m_new = ADAM_B1 * m_ref[...] + (1.0 - ADAM_B1) * g
        v_new = ADAM_B2 * v_ref[...] + (1.0 - ADAM_B2) * (g * g)
        m_hat = m_new / (1.0 - ADAM_B1 ** ADAM_STEP)
        v_hat = v_new / (1.0 - ADAM_B2 ** ADAM_STEP)
        go_ref[...] = g
        do_ref[...] = -ADAM_LR * (m_hat / (jnp.sqrt(v_hat) + ADAM_EPS) + ADAM_WD * w_ref[...])
        mo_ref[...] = m_new
        vo_ref[...] = v_new

    row = pl.BlockSpec((None, tr, c), lambda i: (layer, i, 0))
    return pl.pallas_call(
        body, name=name, grid=(r // tr,),
        in_specs=[pl.BlockSpec((ns, tr, c), lambda i: (0, i, 0)), row, row, row] + [ANY] * len(prev),
        out_specs=[row] * 4, out_shape=[jax.ShapeDtypeStruct((nl, r, c), F32)] * 4,
        input_output_aliases={4 + i: i for i in range(len(prev))},
        compiler_params=_params("parallel"),
    )(gsrc, w, m, v, *prev)


def _ffn_forward(tag, r_in, h, get_wu, get_wd, wdw, bdw, seq, loss=None):
    t, d = r_in.shape
    tm = _tile(t, 512)
    wu = get_wu(h)
    dff = wu.shape[0] // 2
    tu = _tile(t, 1024)
    up = _mm(f"{tag}_up", h, wu, grid=(2, t // tu, 1),
             a_spec=pl.BlockSpec((tu, d), lambda j, i, k: (i, 0)),
             b_spec=pl.BlockSpec((dff, d), lambda j, i, k: (j, 0)),
             out_spec=pl.BlockSpec((None, tu, dff), lambda j, i, k: (j, i, 0)),
             out_shape=jax.ShapeDtypeStruct((2, t, dff), BF16), dims=NT, acc_shape=(tu, dff))
    wd = get_wd(up)
    g = _ffn_fwd(f"{tag}_act", up, wdw, bdw, seq)
    row = pl.BlockSpec((tm, d), lambda i, j, k: (i, 0))
    vec = pl.BlockSpec((1, d), lambda i, j, k: (0, 0))
    common = dict(grid=(t // tm, 1, 1), a_spec=pl.BlockSpec((tm, dff), lambda i, j, k: (i, 0)),
                  b_spec=pl.BlockSpec((dff, d), lambda i, j, k: (0, 0)), dims=NN, acc_shape=(tm, d))
    if loss is None:
        out = _mm(f"{tag}_down", g, wd, out_spec=row, out_shape=jax.ShapeDtypeStruct((t, d), F32),
                  extras=(r_in,), extra_specs=(row,), epilogue=lambda acc, ex, rows: ((ex[0][rows, :] + acc,), ()),
                  **common)
    else:
        def head(acc, ex, rows):
            dx, part, dgain = _loss_tile(ex[0][rows, :] + acc, ex[1][rows, :], ex[2][...])
            return (dx,), (part, dgain)

        out = _mm(f"{tag}_down", g, wd, out_spec=[row, pl.BlockSpec((1, 1), lambda i, j, k: (0, 0)), vec],
                  out_shape=[jax.ShapeDtypeStruct((t, d), F32), jax.ShapeDtypeStruct((1, 1), F32),
                             jax.ShapeDtypeStruct((1, d), F32)],
                  extras=(r_in, *loss), extra_specs=(row, row, vec), epilogue=head, n_sums=2, **common)
    return out, (r_in, h, up, g, wu, wd)


def _ffn_backward(tag, dr, saved, gain, wdw, bdw, seq, token=None):
    r_in, h, up, g, wu, wd = saved
    t, d = r_in.shape
    dff = wd.shape[0]
    tm = _tile(t, 512)
    tk = _tile(t, 2048)
    cw = _ctile(dff, 1408)
    nc = dff // cw
    dg = _mm(f"{tag}_dg", dr, wd, grid=(t // tm, 1, 1),
             a_spec=pl.BlockSpec((tm, d), lambda i, j, k: (i, 0)),
             b_spec=pl.BlockSpec((dff, d), lambda i, j, k: (0, 0)),
             out_spec=pl.BlockSpec((tm, dff), lambda i, j, k: (i, 0)),
             out_shape=jax.ShapeDtypeStruct((t, dff), BF16), dims=NT, acc_shape=(tm, dff), token=token)
    dwd = _mm(f"{tag}_dwd", g, dr, grid=(dff // cw, 1, t // tk),
              a_spec=pl.BlockSpec((tk, cw), lambda i, j, k: (k, i)),
              b_spec=pl.BlockSpec((tk, d), lambda i, j, k: (k, 0)),
              out_spec=pl.BlockSpec((cw, d), lambda i, j, k: (i, 0)),
              out_shape=jax.ShapeDtypeStruct((dff, d), BF16), dims=TN, acc_shape=(cw, d))
    dup, dwdw, dbdw = _ffn_bwd(f"{tag}_dact", up, dg, wdw, bdw, seq)
    row = pl.BlockSpec((tm, d), lambda i, j, k: (i, 0))
    vec = pl.BlockSpec((1, d), lambda i, j, k: (0, 0))

    def norm_backward(acc, ex, rows):
        dx, dgain, colsum = _rms_bwd_tile(acc, ex[0][rows, :], ex[1][...], ex[2][rows, :])
        return (dx,), (dgain, colsum)

    dr_in, dgain, colsum = _mm(
        f"{tag}_dh", dup, wu, grid=(t // tm, 1, 2),
        a_spec=pl.BlockSpec((None, tm, dff), lambda i, j, k: (k, i, 0)),
        b_spec=pl.BlockSpec((dff, d), lambda i, j, k: (k, 0)), out_spec=[row, vec, vec],
        out_shape=[jax.ShapeDtypeStruct((t, d), F32)] + [jax.ShapeDtypeStruct((1, d), F32)] * 2,
        dims=NN, acc_shape=(tm, d), extras=(r_in, gain, dr), extra_specs=(row, vec, row),
        epilogue=norm_backward, n_sums=2)
    dwu = _mm(f"{tag}_dwu", dup, h, grid=(2 * nc, 1, t // tk),
              a_spec=pl.BlockSpec((None, tk, cw), lambda i, j, k: (i // nc, k, i % nc)),
              b_spec=pl.BlockSpec((tk, d), lambda i, j, k: (k, 0)),
              out_spec=pl.BlockSpec((cw, d), lambda i, j, k: (i, 0)),
              out_shape=jax.ShapeDtypeStruct((2 * dff, d), BF16), dims=TN, acc_shape=(cw, d))
    return dr_in, dgain, dwu, dwd, dwdw, dbdw, colsum


def _pad_to(vec, n):
    return jnp.pad(vec, (0, n - vec.shape[0]))


def _pack(parts, width):
    flat = jnp.concatenate([p.reshape(-1).astype(F32) for p in parts])
    n = -(-flat.shape[0] // (8 * width)) * (8 * width)
    return _pad_to(flat, n).reshape(n // width, width)


def _unpack(mat, shapes):
    flat = mat.reshape(-1)
    out, off = [], 0
    for s in shapes:
        n = 1
        for dim in s:
            n *= dim
        out.append(flat[off:off + n].reshape(s))
        off += n
    return out


def kernel(x, norm_mix, norm_ffn, conv_w_pw1, conv_b_pw1, conv_w_dw, conv_b_dw, conv_ln_g, conv_ln_b, conv_w_pw2, conv_b_pw2, pool_w, pool_b, pool_scale, ffn_w_up, ffn_w_dw, ffn_b_dw, ffn_w_down, final_norm, loss_target, m_norm_mix, m_norm_ffn, m_conv_w_pw1, m_conv_b_pw1, m_conv_w_dw, m_conv_b_dw, m_conv_ln_g, m_conv_ln_b, m_conv_w_pw2, m_conv_b_pw2, m_pool_w, m_pool_b, m_pool_scale, m_ffn_w_up, m_ffn_w_dw, m_ffn_b_dw, m_ffn_w_down, m_final_norm, v_norm_mix, v_norm_ffn, v_conv_w_pw1, v_conv_b_pw1, v_conv_w_dw, v_conv_b_dw, v_conv_ln_g, v_conv_ln_b, v_conv_w_pw2, v_conv_b_pw2, v_pool_w, v_pool_b, v_pool_scale, v_ffn_w_up, v_ffn_w_dw, v_ffn_b_dw, v_ffn_w_down, v_final_norm):
    bsz, seq, d = x.shape
    t = bsz * seq
    k_taps = conv_w_dw.shape[1]
    cs1 = conv_w_pw1.shape[2]
    dsh = d // N_DEV
    cg = d // N_GROUPS
    cgs = pool_w.shape[2]
    fu = ffn_w_up.shape[2]
    fd = ffn_w_down.shape[1]
    dff = fd * N_DEV
    nb = N_DEV // 2
    kf = ffn_w_dw.shape[1]
    fsh = ffn_w_dw.shape[2]
    my = _lin(_me())
    tm = _tile(t, 512)

    x2 = x.reshape(t, d)
    tgt2 = loss_target.reshape(t, d)

    small_shapes = [(k_taps, dsh), (dsh,), (dsh,), (2, kf, fsh)]
    small_mine = _pack([conv_w_dw[0], pool_b[0], pool_scale[0], ffn_w_dw], LANE)
    big = [conv_w_pw1[0], conv_w_pw2[0], ffn_w_up[0].T, ffn_w_down[0], pool_w[0], ffn_w_up[1].T, ffn_w_down[1]]
    gather = _exchange_start("gather_start", [small_mine] + [w.astype(BF16) for w in big], [False] * 8)
    h0 = _rms_fwd("l0_rms", x2, norm_mix[0:1])
    small_all, w1 = _exchange_wait("gather_wait_w1", gather, [0, 1], h0)
    parts = [_unpack(small_all[dev], small_shapes) for dev in range(N_DEV)]
    wdw = jnp.concatenate([p[0] for p in parts], axis=1)
    pool_b_full = jnp.concatenate([p[1] for p in parts]).reshape(1, d)
    pool_s_full = jnp.concatenate([p[2] for p in parts]).reshape(1, d)
    fwdw = jnp.concatenate([p[3] for p in parts], axis=2)
    fbdw = ffn_b_dw.reshape(2, 1, dff)

    def columns(w):
        return w.transpose(1, 0, 2).reshape(w.shape[1], N_DEV * w.shape[2])

    def column_shards(w):
        return w.reshape(w.shape[0], N_DEV, w.shape[1] // N_DEV).transpose(1, 0, 2)

    w1 = columns(w1)
    a = _mm("l0_pw1", h0, w1, grid=(2, t // tm, 1),
            a_spec=pl.BlockSpec((tm, d), lambda j, i, k: (i, 0)),
            b_spec=pl.BlockSpec((d, d), lambda j, i, k: (0, j)),
            out_spec=pl.BlockSpec((None, tm, d), lambda j, i, k: (j, i, 0)),
            out_shape=jax.ShapeDtypeStruct((2, t, d), BF16), dims=NN, acc_shape=(tm, d),
            extras=(conv_b_pw1,), extra_specs=(pl.BlockSpec((1, d), lambda j, i, k: (0, j)),),
            epilogue=lambda acc, ex, rows: ((acc + ex[0][...],), ()))
    (w2,) = _exchange_wait("gather_wait_w2", gather, [2], a)
    w2 = w2.reshape(d, d)
    v = _conv_fwd("l0_conv", a, wdw, conv_b_dw, seq)
    row = pl.BlockSpec((tm, d), lambda i, j, k: (i, 0))
    vec = pl.BlockSpec((1, d), lambda i, j, k: (0, 0))
    square = pl.BlockSpec((d, d), lambda i, j, k: (0, 0))

    def ln_silu(v_blk, ex):
        s_blk = _ln_silu_tile(v_blk, ex[0][...], ex[1][...]).astype(BF16)
        return s_blk, s_blk

    def residual_and_norm(acc, ex, rows):
        r_blk = ex[3][rows, :] + (acc + ex[2][...])
        return (r_blk, _rms(r_blk, ex[4][...])), ()

    r1, h1, s = _mm("l0_pw2", v, w2, grid=(t // tm, 1, 1), a_spec=row, b_spec=square, out_spec=[row, row, row],
                    out_shape=[jax.ShapeDtypeStruct((t, d), F32), jax.ShapeDtypeStruct((t, d), BF16),
                               jax.ShapeDtypeStruct((t, d), BF16)],
                    dims=NN, acc_shape=(tm, d), extras=(conv_ln_g, conv_ln_b, conv_b_pw2, x2, norm_ffn[0:1]),
                    extra_specs=(vec, vec, vec, row, vec), prologue=ln_silu, epilogue=residual_and_norm)

    def up_getter(name, idx):
        return lambda after: _exchange_wait(name, gather, [idx], after)[0].reshape(2 * dff, d)

    def down_getter(name, idx):
        return lambda after: _exchange_wait(name, gather, [idx], after)[0].reshape(dff, d)

    r2, ffn0_saved = _ffn_forward("f0", r1, h1, up_getter("gather_wait_wu0", 3),
                                  down_getter("gather_wait_wd0", 4), fwdw[0], fbdw[0], seq)
    (wp,) = _exchange_wait("gather_wait_wp", gather, [5], r2)
    wp = wp.transpose(1, 0, 2, 3).reshape(N_GROUPS, cg, cg)
    pooled, r3, h3 = _pool_mix_fwd("l1_mix", r2, norm_mix[1:2], wp, pool_s_full, pool_b_full, norm_ffn[1:2], seq)
    (dr4, loss_part, dfinal), ffn1_saved = _ffn_forward(
        "f1", r3, h3, up_getter("gather_wait_wu1", 6), down_getter("gather_wait_wd1", 7), fwdw[1], fbdw[1], seq,
        loss=(tgt2, final_norm.reshape(1, d)))

    dr3, dnf1, dwu1, dwd1, dfw1, dfb1, _ = _ffn_backward("f1", dr4, ffn1_saved, norm_ffn[1:2], fwdw[1], fbdw[1], seq)
    scatter_a = _exchange_start("scatter_f1_start", [dwu1.reshape(N_DEV, fu, d), dwd1.reshape(N_DEV, fd, d)],
                                [True, True])
    dr2, dwp, dpool_s, dpool_b, dnm1 = _pool_mix_bwd(
        "l1_dmix", pooled, wp, dr3, r2, norm_mix[1:2], pool_s_full + scatter_a["token"][0:1, 0:1], pool_b_full, seq)
    dr1, dnf0, dwu0, dwd0, dfw0, dfb0, db2 = _ffn_backward("f0", dr2, ffn0_saved, norm_ffn[0:1], fwdw[0], fbdw[0], seq)
    dwp_b = dwp.astype(BF16).reshape(N_GROUPS, N_DEV, cgs, cg).transpose(1, 0, 2, 3)
    scatter_b = _exchange_start("scatter_f0_start", [dwu0.reshape(N_DEV, fu, d), dwd0.reshape(N_DEV, fd, d), dwp_b],
                                [True] * 3)
    def ln_silu_backward(acc, ex, rows):
        dv_blk, dgain, dbias, colsum = _ln_silu_bwd_tile(acc, ex[0][rows, :], ex[1][...], ex[2][...])
        return (dv_blk,), (dgain, dbias, colsum)

    dv, dlg, dlb, dbdw = _mm("l0_ds", dr1, w2, grid=(t // tm, 1, 1), a_spec=row, b_spec=square,
                             out_spec=[row, vec, vec, vec],
                             out_shape=[jax.ShapeDtypeStruct((t, d), F32)] + [jax.ShapeDtypeStruct((1, d), F32)] * 3,
                             dims=NT, acc_shape=(tm, d), extras=(v, conv_ln_g, conv_ln_b), extra_specs=(row, vec, vec),
                             epilogue=ln_silu_backward, n_sums=3, token=scatter_b["token"])
    tk = _tile(t, 2048)
    dw2 = _mm("l0_dw2", s, dr1, grid=(1, 1, t // tk),
              a_spec=pl.BlockSpec((tk, d), lambda i, j, k: (k, 0)),
              b_spec=pl.BlockSpec((tk, d), lambda i, j, k: (k, 0)),
              out_spec=pl.BlockSpec((d, d), lambda i, j, k: (0, 0)),
              out_shape=jax.ShapeDtypeStruct((d, d), BF16), dims=TN, acc_shape=(d, d))
    da, dwdw, db1 = _conv_bwd("l0_dconv", a, dv, wdw, seq)
    dw1 = _mm("l0_dw1", h0, da, grid=(1, 2, t // tk),
              a_spec=pl.BlockSpec((tk, d), lambda i, j, k: (k, 0)),
              b_spec=pl.BlockSpec((None, tk, d), lambda i, j, k: (j, k, 0)),
              out_spec=pl.BlockSpec((d, d), lambda i, j, k: (0, j)),
              out_shape=jax.ShapeDtypeStruct((d, 2 * d), BF16), dims=TN, acc_shape=(d, d))
    scatter_c = _exchange_start("scatter_l0_start", [column_shards(dw1), dw2.reshape(N_DEV, d // N_DEV, d)],
                                [True, True])
    def norm_backward(acc, ex, rows):
        dx_blk, dgain, colsum = _rms_bwd_tile(acc, ex[0][rows, :], ex[1][...], ex[2][rows, :])
        return (dx_blk,), (dgain, colsum)

    dx, dnm0, _ = _mm("l0_dh", da, w1, grid=(t // tm, 1, 2),
                      a_spec=pl.BlockSpec((None, tm, d), lambda i, j, k: (k, i, 0)),
                      b_spec=pl.BlockSpec((d, d), lambda i, j, k: (0, k)), out_spec=[row, vec, vec],
                      out_shape=[jax.ShapeDtypeStruct((t, d), F32)] + [jax.ShapeDtypeStruct((1, d), F32)] * 2,
                      dims=NT, acc_shape=(tm, d), extras=(x2, norm_mix[0:1], dr1), extra_specs=(row, vec, row),
                      epilogue=norm_backward, n_sums=2, token=scatter_c["token"])

    dffn_w = jnp.stack([dfw0, dfw1])
    dffn_b = jnp.stack([dfb0, dfb1]).reshape(2, dff)
    small_parts = [loss_part, jnp.concatenate([dnm0, dnm1]), jnp.concatenate([dnf0, dnf1]), db1, dwdw, dbdw, dlg, dlb,
                   db2, dpool_b, dpool_s, dffn_w, dffn_b, dfinal]
    small_part_shapes = [(1,), (2, d), (2, d), (1, 2 * d), (k_taps, d), (1, d), (1, d), (1, d), (1, d), (1, d), (1, d),
                         (2, kf, dff), (2, dff), (d,)]
    packed = _pack(small_parts, 8 * LANE)
    gather_small = _exchange_start("gather_small_start", [packed], [False])

    def big_update(name, recv, w, m, v, layer=0, prev=None):
        shape = w.shape
        c = recv.shape[-1]
        rows = recv.size // (N_DEV * c)
        nl = w.size // (rows * c)
        outs = _adamw(name, recv.reshape(N_DEV, rows, c), w.reshape(nl, rows, c), m.reshape(nl, rows, c),
                      v.reshape(nl, rows, c), layer, prev)
        return outs, [o.reshape(shape) for o in outs]

    wu_t = [p.transpose(0, 2, 1) for p in (ffn_w_up, m_ffn_w_up, v_ffn_w_up)]
    g_wu1, g_wd1 = _exchange_wait("scatter_f1_wait", scatter_a, [0, 1], gather_small["token"])
    raw_wu, _ = big_update("adam_wu1", g_wu1, *wu_t, 1)
    raw_wd, _ = big_update("adam_wd1", g_wd1, ffn_w_down, m_ffn_w_down, v_ffn_w_down, 1)
    g_wu0, g_wd0, g_wp = _exchange_wait("scatter_f0_wait", scatter_b, [0, 1, 2], raw_wd[0])
    _, u_wu = big_update("adam_wu0", g_wu0, *wu_t, 0, raw_wu)
    u_wu = [o.transpose(0, 2, 1) for o in u_wu]
    _, u_wd = big_update("adam_wd0", g_wd0, ffn_w_down, m_ffn_w_down, v_ffn_w_down, 0, raw_wd)
    _, u_wp = big_update("adam_wp", g_wp, pool_w, m_pool_w, v_pool_w)
    g_w1, g_w2 = _exchange_wait("scatter_l0_wait", scatter_c, [0, 1], u_wp[0])
    _, u_w1 = big_update("adam_w1", g_w1, conv_w_pw1, m_conv_w_pw1, v_conv_w_pw1)
    _, u_w2 = big_update("adam_w2", g_w2, conv_w_pw2, m_conv_w_pw2, v_conv_w_pw2)
    (all_small,) = _exchange_wait("gather_small_wait", gather_small, [0], u_w2[0])
    summed = _sum_rows("sum_small_grads", all_small)
    (loss_v, g_nm, g_nf, g_b1, g_wdw, g_bdw, g_lg, g_lb, g_b2, g_pb, g_ps, g_fw, g_fb,
     g_fin) = _unpack(summed, small_part_shapes)
    loss = loss_v[0]
    g_wdw_mine = lax.dynamic_slice_in_dim(g_wdw, my * dsh, dsh, axis=1)[None]
    g_pb_mine = lax.dynamic_slice_in_dim(g_pb, my * dsh, dsh, axis=1)
    g_ps_mine = lax.dynamic_slice_in_dim(g_ps, my * dsh, dsh, axis=1)
    g_fw_mine = lax.dynamic_slice_in_dim(g_fw, my * fsh, fsh, axis=2)

    small_g =[g_nm, g_nf, g_b1, g_wdw_mine, g_bdw, g_lg, g_lb, g_b2, g_pb_mine, g_ps_mine, g_fw_mine, g_fb, g_fin]
    small_w = [norm_mix, norm_ffn, conv_b_pw1, conv_w_dw, conv_b_dw, conv_ln_g, conv_ln_b, conv_b_pw2, pool_b,
               pool_scale, ffn_w_dw, ffn_b_dw, final_norm]
    small_m = [m_norm_mix, m_norm_ffn, m_conv_b_pw1, m_conv_w_dw, m_conv_b_dw, m_conv_ln_g, m_conv_ln_b,
               m_conv_b_pw2, m_pool_b, m_pool_scale, m_ffn_w_dw, m_ffn_b_dw, m_final_norm]
    small_v = [v_norm_mix, v_norm_ffn, v_conv_b_pw1, v_conv_w_dw, v_conv_b_dw, v_conv_ln_g, v_conv_ln_b,
               v_conv_b_pw2, v_pool_b, v_pool_scale, v_ffn_w_dw, v_ffn_b_dw, v_final_norm]
    shapes = [w.shape for w in small_w]
    outs = _adamw("adam_small", _pack(small_g, 8 * LANE)[None], _pack(small_w, 8 * LANE)[None],
                  _pack(small_m, 8 * LANE)[None], _pack(small_v, 8 * LANE)[None])
    sg, sd, sm, sv = [_unpack(o, shapes) for o in outs]

    def leaf(kind):
        (nm, nf, b1, wdw_, bdw_, lg, lb, b2, pb, ps, fw, fb, fin) = (sg, sd, sm, sv)[kind]
        return [nm, nf, u_w1[kind], b1, wdw_, bdw_, lg, lb, u_w2[kind], b2, u_wp[kind], pb, ps, u_wu[kind], fw, fb,
                u_wd[kind], fin]

    return (loss, dx.reshape(bsz, seq, d), *leaf(0), *leaf(1), *leaf(2), *leaf(3))
```

```python
import functools

import jax
import jax.numpy as jnp
from jax import lax
from jax.experimental import pallas as pl
from jax.experimental.pallas import tpu as pltpu

F32 = jnp.float32
BF16 = jnp.bfloat16
MESH = pl.DeviceIdType.MESH
HBM = pl.BlockSpec(memory_space=pltpu.HBM)

N_DEV = 8
RMS_EPS = 1e-6
LN_EPS = 1e-5
POOL_WINDOWS = (2, 4, 8, 16)
N_GROUPS = len(POOL_WINDOWS)
ADAM_LR = 0.001
ADAM_B1 = 0.9
ADAM_B2 = 0.999
ADAM_EPS = 1e-08
ADAM_WD = 0.01
ADAM_STEP = 10

LANE = 128
HALO = 32
HALO16 = 16
VMEM_LIMIT = 56 * 1024 * 1024


def _params(*sem):
    return pltpu.CompilerParams(dimension_semantics=sem if sem else None, vmem_limit_bytes=VMEM_LIMIT)


def _tile(n, pref):
    for t in range(min(pref, n), 15, -1):
        if n % t == 0 and t % 16 == 0:
            return t
    return n


def _sigmoid(z):
    return 1.0 / (1.0 + jnp.exp(-z))


def _me():
    return lax.axis_index("x"), lax.axis_index("y"), lax.axis_index("c")


def _flip(pos, m):
    x, y, c = pos
    return ((1 - x) if m & 4 else x, (1 - y) if m & 2 else y, (1 - c) if m & 1 else c)


def _lin(pos):
    return 4 * pos[0] + 2 * pos[1] + pos[2]


SEM = pl.BlockSpec(memory_space=pltpu.SEMAPHORE)
ANY = pl.BlockSpec(memory_space=pl.ANY)
EFFECT = pltpu.SideEffectType.DATAFLOW_SIDE_EFFECTING


def _exchange_copies(srcs, lands, send_sems, recv_sems, modes, which, starting):
    me = _me()
    my = _lin(me)
    out = []
    for pos, a in enumerate(which):
        src, land = srcs[pos], lands[pos]

        def block(pid, src=src, a=a):
            return src.at[pid] if modes[a] else src

        local = pltpu.make_async_copy(block(my), land.at[my], send_sems.at[a * N_DEV])
        remote = []
        for m in range(1, N_DEV):
            peer = _flip(me, m)
            pid = _lin(peer)
            sems = dict(send_sem=send_sems.at[a * N_DEV + m], recv_sem=recv_sems.at[a * N_DEV + m],
                        device_id=peer, device_id_type=MESH)
            if starting:
                remote.append(pltpu.make_async_remote_copy(src_ref=block(pid), dst_ref=land.at[my], **sems))
            else:
                remote.append((pltpu.make_async_remote_copy(src_ref=block(pid), dst_ref=land.at[my], **sems),
                               pltpu.make_async_remote_copy(src_ref=block(pid), dst_ref=land.at[pid], **sems)))
        out.append((local, remote))
    return out


def _exchange_start(name, arrs, modes):
    n = len(arrs)
    blocks = [a.shape[1:] if md else a.shape for a, md in zip(arrs, modes)]

    def body(*refs):
        srcs, lands = refs[:n], refs[n:2 * n]
        send_sems, recv_sems = refs[2 * n], refs[2 * n + 1]
        token = refs[-1]
        for local, remote in _exchange_copies(srcs, lands, send_sems, recv_sems, modes, list(range(n)), True):
            local.start()
            for send in remote:
                send.start()
        token[...] = jnp.zeros_like(token)

    lands = [lax.empty((N_DEV,) + tuple(b), a.dtype) for a, b in zip(arrs, blocks)]
    outs = pl.pallas_call(
        body, name=name,
        out_shape=(pltpu.SemaphoreType.DMA((n * N_DEV,)), pltpu.SemaphoreType.DMA((n * N_DEV,)),
                   *[pltpu.HBM(a.shape, a.dtype) for a in arrs], *[pltpu.HBM(l.shape, l.dtype) for l in lands],
                   jax.ShapeDtypeStruct((8, LANE), F32)),
        in_specs=[HBM] * (2 * n),
        out_specs=(SEM, SEM, *[HBM] * (2 * n), pl.BlockSpec(memory_space=pltpu.VMEM)),
        input_output_aliases={i: 2 + i for i in range(2 * n)},
        compiler_params=pltpu.CompilerParams(has_side_effects=EFFECT),
    )(*[pltpu.with_memory_space_constraint(a, pltpu.HBM) for a in arrs],
      *[pltpu.with_memory_space_constraint(l, pltpu.HBM) for l in lands])
    return dict(send=outs[0], recv=outs[1], srcs=list(outs[2:2 + n]), lands=list(outs[2 + n:2 + 2 * n]),
                modes=modes, token=outs[-1])


def _exchange_wait(name, handle, which, after):
    k = len(which)
    modes = handle["modes"]

    def body(*refs):
        srcs, lands = refs[:k], refs[k:2 * k]
        send_sems, recv_sems = refs[2 * k], refs[2 * k + 1]
        for local, remote in _exchange_copies(srcs, lands, send_sems, recv_sems, modes, which, False):
            local.wait()
            for send, arrival in remote:
                send.wait_send()
                arrival.wait_recv()

    srcs = [handle["srcs"][a] for a in which]
    lands = [handle["lands"][a] for a in which]
    outs = pl.pallas_call(
        body, name=name,
        out_shape=tuple(pltpu.HBM(x.shape, x.dtype) for x in srcs + lands),
        in_specs=[HBM] * (2 * k) + [SEM, SEM, ANY], out_specs=tuple([HBM] * (2 * k)),
        input_output_aliases={i: i for i in range(2 * k)},
        compiler_params=pltpu.CompilerParams(has_side_effects=EFFECT),
    )(*srcs, *lands, handle["send"], handle["recv"], after)
    return list(outs[k:])


def _mm(name, a, b, *, grid, a_spec, b_spec, out_spec, out_shape, dims, acc_shape, extras=(), extra_specs=(),
        epilogue=None, token=None, prologue=None, n_sums=0, parts=1):
    nk = grid[2]
    ne = len(extras)
    deps = () if token is None else (token,)
    dep_specs = [pl.BlockSpec((8, LANE), lambda i, j, k: (0, 0))] * len(deps)
    n_out = len(out_shape) if isinstance(out_shape, (list, tuple)) else 1
    n_tiles = n_out - n_sums - (1 if prologue is not None else 0)

    def body(a_ref, b_ref, *rest):
        ex, o_refs, acc_ref = rest[:ne], rest[ne + len(deps):ne + len(deps) + n_out], rest[ne + len(deps) + n_out]
        k = pl.program_id(2)
        if parts == 1:
            a_blk, saved = a_ref[...], None
            if prologue is not None:
                a_blk, saved = prologue(a_blk, ex)
                o_refs[n_tiles][...] = saved
            part = lax.dot_general(a_blk.astype(BF16), b_ref[...].astype(BF16), (dims, ((), ())),
                                   preferred_element_type=F32)
        else:
            kb = b_ref.shape[dims[1][0]] // parts
            part = None
            for p in range(parts):
                b_blk = b_ref[p * kb:(p + 1) * kb, :] if dims[1][0] == 0 else b_ref[:, p * kb:(p + 1) * kb]
                term = lax.dot_general(a_ref[p].astype(BF16), b_blk.astype(BF16), (dims, ((), ())),
                                       preferred_element_type=F32)
                part = term if part is None else part + term
        sum_refs = o_refs[n_out - n_sums:]

        def add_sums(terms):
            @pl.when((pl.program_id(0) == 0) & (pl.program_id(1) == 0))
            def _():
                for o_ref in sum_refs:
                    o_ref[...] = jnp.zeros_like(o_ref)

            for o_ref, term in zip(sum_refs, terms):
                o_ref[...] += jnp.sum(term, axis=0, keepdims=True)

        def finish(r):
            tiles, terms = ((r,), ()) if epilogue is None else epilogue(r, ex, slice(None))
            for o_ref, val in zip(o_refs, tiles):
                o_ref[...] = val.astype(o_ref.dtype)
            if n_sums:
                add_sums(terms)

        if nk == 1:
            finish(part)
            return

        @pl.when(k == 0)
        def _():
            acc_ref[...] = part

        @pl.when((k > 0) & (k < nk - 1))
        def _():
            acc_ref[...] += part

        @pl.when(k == nk - 1)
        def _():
            finish(acc_ref[...] + part)

    return pl.pallas_call(
        body, name=name, grid=grid, in_specs=[a_spec, b_spec, *extra_specs, *dep_specs], out_specs=out_spec,
        out_shape=out_shape, scratch_shapes=[pltpu.VMEM(acc_shape if nk > 1 else (8, LANE), F32)],
        compiler_params=_params(*(("arbitrary",) * 3 if n_sums else ("parallel", "parallel", "arbitrary"))),
    )(a, b, *extras, *deps)


def _rms(x, gain):
    return x * lax.rsqrt(jnp.mean(x * x, axis=-1, keepdims=True) + RMS_EPS) * gain


def _rms_bwd_tile(dh, x, gain, dres):
    rstd = lax.rsqrt(jnp.mean(x * x, axis=-1, keepdims=True) + RMS_EPS)
    xhat = x * rstd
    dxhat = dh * gain
    dx = dres + rstd * (dxhat - xhat * jnp.mean(dxhat * xhat, axis=-1, keepdims=True))
    return dx, dh * xhat, dx


def _ln_silu_tile(v, g, b):
    mu = jnp.mean(v, axis=-1, keepdims=True)
    cen = v - mu
    z = cen * lax.rsqrt(jnp.mean(cen * cen, axis=-1, keepdims=True) + LN_EPS) * g + b
    return z * _sigmoid(z)


def _ln_silu_bwd_tile(ds, v, g, b):
    mu = jnp.mean(v, axis=-1, keepdims=True)
    cen = v - mu
    rstd = lax.rsqrt(jnp.mean(cen * cen, axis=-1, keepdims=True) + LN_EPS)
    y = cen * rstd
    z = y * g + b
    sig = _sigmoid(z)
    dz = ds * sig * (1.0 + z * (1.0 - sig))
    dy = dz * g
    dv = rstd * (dy - jnp.mean(dy, axis=-1, keepdims=True) - y * jnp.mean(dy * y, axis=-1, keepdims=True))
    return dv, dz * y, dz, dv


def _loss_tile(x, tgt, gain):
    d = x.shape[-1]
    rstd = lax.rsqrt(jnp.mean(x * x, axis=-1, keepdims=True) + RMS_EPS)
    xhat = x * rstd
    err = xhat * gain - tgt
    dy = err / d
    dxhat = dy * gain
    dx = rstd * (dxhat - xhat * jnp.mean(dxhat * xhat, axis=-1, keepdims=True))
    return dx, 0.5 * jnp.mean(err * err, axis=-1, keepdims=True), dy * xhat


NN = ((1,), (0,))
NT = ((1,), (1,))
TN = ((0,), (0,))


def _rms_fwd(name, x, gain):
    t, d = x.shape
    tr = _tile(t, 512)

    def body(x_ref, g_ref, h_ref):
        h_ref[...] = _rms(x_ref[...], g_ref[...]).astype(BF16)

    return pl.pallas_call(
        body, name=name, grid=(t // tr,),
        in_specs=[pl.BlockSpec((tr, d), lambda i: (i, 0)), pl.BlockSpec((1, d), lambda i: (0, 0))],
        out_specs=pl.BlockSpec((tr, d), lambda i: (i, 0)),
        out_shape=jax.ShapeDtypeStruct((t, d), BF16), compiler_params=_params("parallel"),
    )(x, gain)


def _conv_tiles(t, seq):
    ts = _tile(seq, 512)
    return ts, seq // ts, _tile(ts, 64)


def _conv_fwd(name, a, w, b, seq):
    _, t, d = a.shape
    k_taps = w.shape[0]
    ts, tps, rc = _conv_tiles(t, seq)
    hb = ts // HALO

    def body(cur_ref, prev_ref, w_ref, b_ref, v_ref, upad):
        i = pl.program_id(1)
        first = (i % tps) == 0
        pv = prev_ref[0].astype(F32)
        pg = prev_ref[1].astype(F32)
        upad[0:HALO, :] = jnp.where(first, 0.0, pv * _sigmoid(pg))
        upad[HALO:HALO + ts, :] = cur_ref[0].astype(F32) * _sigmoid(cur_ref[1].astype(F32))
        wv = w_ref[...]
        bias = jnp.broadcast_to(b_ref[...], (rc, LANE))
        for r0 in range(0, ts, rc):
            acc = bias
            for k in range(k_taps):
                acc = acc + wv[k:k + 1, :] * upad[pl.ds(HALO - (k_taps - 1) + k + r0, rc), :]
            v_ref[pl.ds(r0, rc), :] = acc

    return pl.pallas_call(
        body, name=name, grid=(d // LANE, t // ts),
        in_specs=[pl.BlockSpec((2, ts, LANE), lambda c, i: (0, i, c)),
                  pl.BlockSpec((2, HALO, LANE), lambda c, i: (0, jnp.maximum(i * hb - 1, 0), c)),
                  pl.BlockSpec((k_taps, LANE), lambda c, i: (0, c)),
                  pl.BlockSpec((1, LANE), lambda c, i: (0, c))],
        out_specs=pl.BlockSpec((ts, LANE), lambda c, i: (i, c)),
        out_shape=jax.ShapeDtypeStruct((t, d), F32),
        scratch_shapes=[pltpu.VMEM((HALO + ts, LANE), F32)],
        compiler_params=_params("parallel", "parallel"),
    )(a, a, w, b)


def _conv_bwd(name, a, dv, w, seq):
    _, t, d = a.shape
    k_taps = w.shape[0]
    ts, tps, rc = _conv_tiles(t, seq)
    hb = ts // HALO
    nhb = t // HALO

    def body(cur_ref, prev_ref, dv_ref, ndv_ref, w_ref, da_ref, dw_ref, dbp_ref, upad, dvpad, dwrows):
        i = pl.program_id(1)
        first = (i % tps) == 0
        last = (i % tps) == tps - 1
        pv = prev_ref[0].astype(F32)
        pg = prev_ref[1].astype(F32)
        upad[0:HALO, :] = jnp.where(first, 0.0, pv * _sigmoid(pg))
        upad[HALO:HALO + ts, :] = cur_ref[0].astype(F32) * _sigmoid(cur_ref[1].astype(F32))
        dvpad[0:ts, :] = dv_ref[...]
        dvpad[ts:ts + HALO, :] = jnp.where(last, 0.0, ndv_ref[...])
        wv = w_ref[...]

        @pl.when(i == 0)
        def _():
            dw_ref[...] = jnp.zeros_like(dw_ref)
            dbp_ref[...] = jnp.zeros_like(dbp_ref)

        sv = jnp.zeros((1, LANE), F32)
        sg = jnp.zeros((1, LANE), F32)
        for r0 in range(0, ts, rc):
            du = jnp.zeros((rc, LANE), F32)
            for k in range(k_taps):
                du = du + wv[k:k + 1, :] * dvpad[pl.ds(r0 + (k_taps - 1) - k, rc), :]
            av = cur_ref[0, pl.ds(r0, rc), :].astype(F32)
            sig = _sigmoid(cur_ref[1, pl.ds(r0, rc), :].astype(F32))
            dval = du * sig
            dgate = du * av * sig * (1.0 - sig)
            da_ref[0, pl.ds(r0, rc), :] = dval.astype(BF16)
            da_ref[1, pl.ds(r0, rc), :] = dgate.astype(BF16)
            sv = sv + jnp.sum(dval, axis=0, keepdims=True)
            sg = sg + jnp.sum(dgate, axis=0, keepdims=True)
        dbp_ref[0] += sv
        dbp_ref[1] += sg

        for k in range(k_taps):
            acc = jnp.zeros((rc, LANE), F32)
            for r0 in range(0, ts, rc):
                acc = acc + dvpad[pl.ds(r0, rc), :] * upad[pl.ds(HALO - (k_taps - 1) + k + r0, rc), :]
            dwrows[k:k + 1, :] = jnp.sum(acc, axis=0, keepdims=True)
        dw_ref[...] += dwrows[0:k_taps, :]

    return pl.pallas_call(
        body, name=name, grid=(d // LANE, t // ts),
        in_specs=[pl.BlockSpec((2, ts, LANE), lambda c, i: (0, i, c)),
                  pl.BlockSpec((2, HALO, LANE), lambda c, i: (0, jnp.maximum(i * hb - 1, 0), c)),
                  pl.BlockSpec((ts, LANE), lambda c, i: (i, c)),
                  pl.BlockSpec((HALO, LANE), lambda c, i: (jnp.minimum((i + 1) * hb, nhb - 1), c)),
                  pl.BlockSpec((k_taps, LANE), lambda c, i: (0, c))],
        out_specs=[pl.BlockSpec((2, ts, LANE), lambda c, i: (0, i, c)),
                   pl.BlockSpec((k_taps, LANE), lambda c, i: (0, c)),
                   pl.BlockSpec((2, 1, LANE), lambda c, i: (0, 0, c))],
        out_shape=[jax.ShapeDtypeStruct((2, t, d), BF16), jax.ShapeDtypeStruct((k_taps, d), F32),
                   jax.ShapeDtypeStruct((2, 1, d), F32)],
        scratch_shapes=[pltpu.VMEM((HALO + ts, LANE), F32), pltpu.VMEM((ts + HALO, LANE), F32),
                        pltpu.VMEM((HALO, LANE), F32)],
        compiler_params=_params("parallel", "arbitrary"),
    )(a, a, dv, dv, w)


def _pool_mix_fwd(name, x, gain, wp, scale, bias, next_gain, seq):
    t, d = x.shape
    ts = _tile(seq, 256)
    tps = seq // ts
    hb = ts // HALO
    cg = d // N_GROUPS

    def body(cur_ref, prev_ref, g_ref, w_ref, s_ref, b_ref, ng_ref, p_ref, r_ref, h_ref, hpad):
        i = pl.program_id(0)
        first = (i % tps) == 0
        g = g_ref[...]
        hpad[0:HALO, :] = jnp.where(first, 0.0, _rms(prev_ref[...], g))
        hpad[HALO:HALO + ts, :] = _rms(cur_ref[...], g)
        pos = (i % tps) * ts + lax.broadcasted_iota(jnp.int32, (ts, 1), 0)
        for gi, win in enumerate(POOL_WINDOWS):
            sl = slice(gi * cg, (gi + 1) * cg)
            own = hpad[HALO:HALO + ts, sl]
            acc = own
            for j in range(1, win):
                acc = acc + hpad[HALO - j:HALO - j + ts, sl]
            cnt = jnp.minimum(pos + 1, win).astype(F32)
            pooled = (acc / cnt - own).astype(BF16)
            p_ref[:, sl] = pooled
            mixed = jnp.dot(pooled, w_ref[gi], preferred_element_type=F32)
            r_ref[:, sl] = cur_ref[:, sl] + s_ref[:, sl] * (mixed + b_ref[:, sl])
        h_ref[...] = _rms(r_ref[...], ng_ref[...]).astype(BF16)

    row = pl.BlockSpec((ts, d), lambda i: (i, 0))
    vec = pl.BlockSpec((1, d), lambda i: (0, 0))
    return pl.pallas_call(
        body, name=name, grid=(t // ts,),
        in_specs=[row, pl.BlockSpec((HALO, d), lambda i: (jnp.maximum(i * hb - 1, 0), 0)), vec,
                  pl.BlockSpec((N_GROUPS, cg, cg), lambda i: (0, 0, 0)), vec, vec, vec],
        out_specs=[row, row, row],
        out_shape=[jax.ShapeDtypeStruct((t, d), BF16), jax.ShapeDtypeStruct((t, d), F32),
                   jax.ShapeDtypeStruct((t, d), BF16)],
        scratch_shapes=[pltpu.VMEM((HALO + ts, d), F32)],
        compiler_params=_params("parallel"),
    )(x, x, gain, wp, scale, bias, next_gain)


def _pool_mix_bwd(name, pooled, wp, dr, x, gain, scale, bias, seq):
    t, d = x.shape
    ts = _tile(seq, 256)
    tps = seq // ts
    hb = ts // HALO
    nhb = t // HALO
    cg = d // N_GROUPS

    def body(p_ref, w_ref, dr_ref, ndr_ref, x_ref, g_ref, s_ref, b_ref, dx_ref, dw_ref, ds_ref, db_ref, dg_ref,
             qpad, dh):
        i = pl.program_id(0)
        last = (i % tps) == tps - 1
        pos = (i % tps) * ts + lax.broadcasted_iota(jnp.int32, (ts, 1), 0)

        @pl.when(i == 0)
        def _():
            dw_ref[...] = jnp.zeros_like(dw_ref)
            ds_ref[...] = jnp.zeros_like(ds_ref)
            db_ref[...] = jnp.zeros_like(db_ref)
            dg_ref[...] = jnp.zeros_like(dg_ref)

        for gi, win in enumerate(POOL_WINDOWS):
            sl = slice(gi * cg, (gi + 1) * cg)
            wv = w_ref[gi]
            sc = s_ref[:, sl]
            drv = dr_ref[:, sl]
            dmx = drv * sc
            dmx16 = dmx.astype(BF16)
            pooled = p_ref[:, sl]
            dw_ref[gi] += lax.dot_general(pooled, dmx16, (TN, ((), ())), preferred_element_type=F32)
            mixed = jnp.dot(pooled, wv, preferred_element_type=F32)
            ds_ref[:, sl] += jnp.sum(drv * (mixed + b_ref[:, sl]), axis=0, keepdims=True)
            db_ref[:, sl] += jnp.sum(dmx, axis=0, keepdims=True)
            cur = lax.dot_general(dmx16, wv, (NT, ((), ())), preferred_element_type=F32)
            nxt = lax.dot_general((ndr_ref[:, sl] * sc).astype(BF16), wv, (NT, ((), ())),
                                  preferred_element_type=F32)
            qpad[0:ts, sl] = cur / jnp.minimum(pos + 1, win).astype(F32)
            qpad[ts:ts + HALO, sl] = jnp.where(last, 0.0, nxt / float(win))
            acc = -cur
            for j in range(win):
                acc = acc + qpad[j:j + ts, sl]
            dh[:, sl] = acc
        dx, dgain_term, _ = _rms_bwd_tile(dh[...], x_ref[...], g_ref[...], dr_ref[...])
        dx_ref[...] = dx
        dg_ref[...] += jnp.sum(dgain_term, axis=0, keepdims=True)

    row = pl.BlockSpec((ts, d), lambda i: (i, 0))
    vec = pl.BlockSpec((1, d), lambda i: (0, 0))
    return pl.pallas_call(
        body, name=name, grid=(t // ts,),
        in_specs=[row, pl.BlockSpec((N_GROUPS, cg, cg), lambda i: (0, 0, 0)), row,
                  pl.BlockSpec((HALO, d), lambda i: (jnp.minimum((i + 1) * hb, nhb - 1), 0)), row, vec, vec, vec],
        out_specs=[row, pl.BlockSpec((N_GROUPS, cg, cg), lambda i: (0, 0, 0)), vec, vec, vec],
        out_shape=[jax.ShapeDtypeStruct((t, d), F32), jax.ShapeDtypeStruct((N_GROUPS, cg, cg), F32)]
        + [jax.ShapeDtypeStruct((1, d), F32)] * 3,
        scratch_shapes=[pltpu.VMEM((ts + HALO, d), F32), pltpu.VMEM((ts, d), F32)],
        compiler_params=_params("arbitrary"),
    )(pooled, wp, dr, dr, x, gain, scale, bias)


def _ctile(n, pref):
    return max(c for c in range(LANE, min(pref, n) + 1, LANE) if n % c == 0)


FFN_COLS = 1408
FFN_ROWS = 32


def _ffn_fwd(name, up, w, b, seq):
    _, t, dff = up.shape
    f = _ctile(dff, FFN_COLS)
    k_taps = w.shape[0]
    ts = _tile(seq, 256)
    tps = seq // ts
    hb = ts // HALO16
    rc = _tile(ts, FFN_ROWS)

    def body(cur_ref, prev_ref, w_ref, b_ref, g_ref, apad):
        i = pl.program_id(1)
        first = (i % tps) == 0
        for ci, c0 in enumerate(range(0, f, LANE)):
            cols = slice(c0, c0 + LANE)
            apad[ci, 0:HALO16, :] = jnp.where(first, 0.0, prev_ref[:, cols].astype(F32))
            apad[ci, HALO16:HALO16 + ts, :] = cur_ref[0, :, cols].astype(F32)
            wv = w_ref[:, cols]
            wk = [jnp.broadcast_to(wv[k:k + 1, :], (rc, LANE)) for k in range(k_taps)]
            bias = jnp.broadcast_to(b_ref[:, cols], (rc, LANE))
            for r0 in range(0, ts, rc):
                c = bias
                for k in range(k_taps):
                    c = c + wk[k] * apad[ci, pl.ds(HALO16 - (k_taps - 1) + k + r0, rc), :]
                gate = cur_ref[1, pl.ds(r0, rc), cols].astype(F32)
                g_ref[pl.ds(r0, rc), cols] = (c * _sigmoid(c) * gate).astype(BF16)

    return pl.pallas_call(
        body, name=name, grid=(dff // f, t // ts),
        in_specs=[pl.BlockSpec((2, ts, f), lambda j, i: (0, i, j)),
                  pl.BlockSpec((None, HALO16, f), lambda j, i: (0, jnp.maximum(i * hb - 1, 0), j)),
                  pl.BlockSpec((k_taps, f), lambda j, i: (0, j)),
                  pl.BlockSpec((1, f), lambda j, i: (0, j))],
        out_specs=pl.BlockSpec((ts, f), lambda j, i: (i, j)),
        out_shape=jax.ShapeDtypeStruct((t, dff), BF16),
        scratch_shapes=[pltpu.VMEM((f // LANE, HALO16 + ts, LANE), F32)],
        compiler_params=_params("parallel", "parallel"),
    )(up, up, w, b)


def _ffn_bwd(name, up, dg, w, b, seq):
    _, t, dff = up.shape
    f = _ctile(dff, FFN_COLS)
    k_taps = w.shape[0]
    ts = _tile(seq, 256)
    tps = seq // ts
    hb = ts // HALO16
    nhb = t // HALO16
    ext = ts + HALO16
    rc = _tile(ts, FFN_ROWS)

    def body(cur_ref, prev_ref, next_ref, dg_ref, ndg_ref, w_ref, b_ref, dup_ref, dw_ref, db_ref, apad, dcpad):
        i = pl.program_id(1)
        first = (i % tps) == 0
        last = (i % tps) == tps - 1

        @pl.when(i == 0)
        def _():
            dw_ref[...] = jnp.zeros_like(dw_ref)
            db_ref[...] = jnp.zeros_like(db_ref)

        for ci, c0 in enumerate(range(0, f, LANE)):
            cols = slice(c0, c0 + LANE)
            apad[ci, 0:HALO16, :] = jnp.where(first, 0.0, prev_ref[:, cols].astype(F32))
            apad[ci, HALO16:HALO16 + ts, :] = cur_ref[0, :, cols].astype(F32)
            apad[ci, HALO16 + ts:HALO16 + ext, :] = next_ref[0, :, cols].astype(F32)
            wv = w_ref[:, cols]
            wk = [jnp.broadcast_to(wv[k:k + 1, :], (rc, LANE)) for k in range(k_taps)]
            bias = jnp.broadcast_to(b_ref[:, cols], (rc, LANE))

            def conv_grad(r0, n, gate, dgv):
                c = bias[0:n]
                for k in range(k_taps):
                    c = c + wk[k][0:n] * apad[ci, pl.ds(HALO16 - (k_taps - 1) + k + r0, n), :]
                sig = _sigmoid(c)
                return dgv * gate * sig * (1.0 + c * (1.0 - sig)), c * sig

            for r0 in range(0, ts, rc):
                dgv = dg_ref[pl.ds(r0, rc), cols].astype(F32)
                dc, silu = conv_grad(r0, rc, cur_ref[1, pl.ds(r0, rc), cols].astype(F32), dgv)
                dcpad[ci, pl.ds(r0, rc), :] = dc
                dup_ref[1, pl.ds(r0, rc), cols] = (dgv * silu).astype(BF16)
            dgv = jnp.where(last, 0.0, ndg_ref[:, cols].astype(F32))
            dc, _ = conv_grad(ts, HALO16, next_ref[1, :, cols].astype(F32), dgv)
            dcpad[ci, ts:ext, :] = dc

            dw_acc = [jnp.zeros((rc, LANE), F32) for _ in range(k_taps)]
            db_acc = jnp.zeros((rc, LANE), F32)
            for r0 in range(0, ts, rc):
                dact = jnp.zeros((rc, LANE), F32)
                for k in range(k_taps):
                    dact = dact + wk[k] * dcpad[ci, pl.ds(r0 + (k_taps - 1) - k, rc), :]
                dup_ref[0, pl.ds(r0, rc), cols] = dact.astype(BF16)
                dc = dcpad[ci, pl.ds(r0, rc), :]
                for k in range(k_taps):
                    dw_acc[k] = dw_acc[k] + dc * apad[ci, pl.ds(HALO16 - (k_taps - 1) + k + r0, rc), :]
                db_acc = db_acc + dc
            for k in range(k_taps):
                dw_ref[k:k + 1, cols] += jnp.sum(dw_acc[k], axis=0, keepdims=True)
            db_ref[:, cols] += jnp.sum(db_acc, axis=0, keepdims=True)

    return pl.pallas_call(
        body, name=name, grid=(dff // f, t // ts),
        in_specs=[pl.BlockSpec((2, ts, f), lambda j, i: (0, i, j)),
                  pl.BlockSpec((None, HALO16, f), lambda j, i: (0, jnp.maximum(i * hb - 1, 0), j)),
                  pl.BlockSpec((2, HALO16, f), lambda j, i: (0, jnp.minimum((i + 1) * hb, nhb - 1), j)),
                  pl.BlockSpec((ts, f), lambda j, i: (i, j)),
                  pl.BlockSpec((HALO16, f), lambda j, i: (jnp.minimum((i + 1) * hb, nhb - 1), j)),
                  pl.BlockSpec((k_taps, f), lambda j, i: (0, j)),
                  pl.BlockSpec((1, f), lambda j, i: (0, j))],
        out_specs=[pl.BlockSpec((2, ts, f), lambda j, i: (0, i, j)),
                   pl.BlockSpec((k_taps, f), lambda j, i: (0, j)),
                   pl.BlockSpec((1, f), lambda j, i: (0, j))],
        out_shape=[jax.ShapeDtypeStruct((2, t, dff), BF16), jax.ShapeDtypeStruct((k_taps, dff), F32),
                   jax.ShapeDtypeStruct((1, dff), F32)],
        scratch_shapes=[pltpu.VMEM((f // LANE, HALO16 + ext, LANE), F32), pltpu.VMEM((f // LANE, ext, LANE), F32)],
        compiler_params=_params("parallel", "arbitrary"),
    )(up, up, up, dg, dg, w, b)


def _sum_rows(name, g):
    ns, r, c = g.shape
    tr = _tile(r, 256)

    def body(g_ref, o_ref):
        acc = g_ref[0]
        for dev in range(1, ns):
            acc = acc + g_ref[dev]
        o_ref[...] = acc

    return pl.pallas_call(
        body, name=name, grid=(r // tr,),
        in_specs=[pl.BlockSpec((ns, tr, c), lambda i: (0, i, 0))],
        out_specs=pl.BlockSpec((tr, c), lambda i: (i, 0)),
        out_shape=jax.ShapeDtypeStruct((r, c), F32), compiler_params=_params("parallel"),
    )(g)


def _adamw(name, gsrc, w, m, v, layer=0, prev=None):
    ns, r, c = gsrc.shape
    nl = w.shape[0]
    tr = _tile(r, 256)
    prev = () if prev is None else tuple(prev)

    def body(g_ref, w_ref, m_ref, v_ref, *rest):
        go_ref, do_ref, mo_ref, vo_ref = rest[len(prev):]
        g = g_ref[0].astype(F32)
        for dev in range(1, ns):
            g = g + g_ref[dev].astype(F32)
        m_new = ADAM_B1 * m_ref[...] + (1.0 - ADAM_B1) * g
        v_new = ADAM_B2 * v_ref[...] + (1.0 - ADAM_B2) * (g * g)
        m_hat = m_new / (1.0 - ADAM_B1 ** ADAM_STEP)
        v_hat = v_new / (1.0 - ADAM_B2 ** ADAM_STEP)
        go_ref[...] = g
        do_ref[...] = -ADAM_LR * (m_hat / (jnp.sqrt(v_hat) + ADAM_EPS) + ADAM_WD * w_ref[...])
        mo_ref[...] = m_new
        vo_ref[...] = v_new

    row = pl.BlockSpec((None, tr, c), lambda i: (layer, i, 0))
    return pl.pallas_call(
        body, name=name, grid=(r // tr,),
        in_specs=[pl.BlockSpec((ns, tr, c), lambda i: (0, i, 0)), row, row, row] + [ANY] * len(prev),
        out_specs=[row] * 4, out_shape=[jax.ShapeDtypeStruct((nl, r, c), F32)] * 4,
        input_output_aliases={4 + i: i for i in range(len(prev))},
        compiler_params=_params("parallel"),
    )(gsrc, w, m, v, *prev)


def _ffn_forward(tag, r_in, h, get_wu, get_wd, wdw, bdw, seq, loss=None):
    t, d = r_in.shape
    tm = _tile(t, 512)
    wu = get_wu(h)
    dff = wu.shape[0] // 2
    tu = _tile(t, 1024)
    up = _mm(f"{tag}_up", h, wu, grid=(2, t // tu, 1),
             a_spec=pl.BlockSpec((tu, d), lambda j, i, k: (i, 0)),
             b_spec=pl.BlockSpec((dff, d), lambda j, i, k: (j, 0)),
             out_spec=pl.BlockSpec((None, tu, dff), lambda j, i, k: (j, i, 0)),
             out_shape=jax.ShapeDtypeStruct((2, t, dff), BF16), dims=NT, acc_shape=(tu, dff))
    wd = get_wd(up)
    g = _ffn_fwd(f"{tag}_act", up, wdw, bdw, seq)
    row = pl.BlockSpec((tm, d), lambda i, j, k: (i, 0))
    vec = pl.BlockSpec((1, d), lambda i, j, k: (0, 0))
    common = dict(grid=(t // tm, 1, 1), a_spec=pl.BlockSpec((tm, dff), lambda i, j, k: (i, 0)),
                  b_spec=pl.BlockSpec((dff, d), lambda i, j, k: (0, 0)), dims=NN, acc_shape=(tm, d))
    if loss is None:
        out = _mm(f"{tag}_down", g, wd, out_spec=row, out_shape=jax.ShapeDtypeStruct((t, d), F32),
                  extras=(r_in,), extra_specs=(row,), epilogue=lambda acc, ex, rows: ((ex[0][rows, :] + acc,), ()),
                  **common)
    else:
        def head(acc, ex, rows):
            dx, part, dgain = _loss_tile(ex[0][rows, :] + acc, ex[1][rows, :], ex[2][...])
            return (dx,), (part, dgain)

        out = _mm(f"{tag}_down", g, wd, out_spec=[row, pl.BlockSpec((1, 1), lambda i, j, k: (0, 0)), vec],
                  out_shape=[jax.ShapeDtypeStruct((t, d), F32), jax.ShapeDtypeStruct((1, 1), F32),
                             jax.ShapeDtypeStruct((1, d), F32)],
                  extras=(r_in, *loss), extra_specs=(row, row, vec), epilogue=head, n_sums=2, **common)
    return out, (r_in, h, up, g, wu, wd)


def _ffn_backward(tag, dr, saved, gain, wdw, bdw, seq, token=None):
    r_in, h, up, g, wu, wd = saved
    t, d = r_in.shape
    dff = wd.shape[0]
    tm = _tile(t, 512)
    tk = _tile(t, 2048)
    cw = _ctile(dff, 1408)
    nc = dff // cw
    dg = _mm(f"{tag}_dg", dr, wd, grid=(t // tm, 1, 1),
             a_spec=pl.BlockSpec((tm, d), lambda i, j, k: (i, 0)),
             b_spec=pl.BlockSpec((dff, d), lambda i, j, k: (0, 0)),
             out_spec=pl.BlockSpec((tm, dff), lambda i, j, k: (i, 0)),
             out_shape=jax.ShapeDtypeStruct((t, dff), BF16), dims=NT, acc_shape=(tm, dff), token=token)
    dwd = _mm(f"{tag}_dwd", g, dr, grid=(dff // cw, 1, t // tk),
              a_spec=pl.BlockSpec((tk, cw), lambda i, j, k: (k, i)),
              b_spec=pl.BlockSpec((tk, d), lambda i, j, k: (k, 0)),
              out_spec=pl.BlockSpec((cw, d), lambda i, j, k: (i, 0)),
              out_shape=jax.ShapeDtypeStruct((dff, d), BF16), dims=TN, acc_shape=(cw, d))
    dup, dwdw, dbdw = _ffn_bwd(f"{tag}_dact", up, dg, wdw, bdw, seq)
    row = pl.BlockSpec((tm, d), lambda i, j, k: (i, 0))
    vec = pl.BlockSpec((1, d), lambda i, j, k: (0, 0))

    def norm_backward(acc, ex, rows):
        dx, dgain, colsum = _rms_bwd_tile(acc, ex[0][rows, :], ex[1][...], ex[2][rows, :])
        return (dx,), (dgain, colsum)

    dr_in, dgain, colsum = _mm(
        f"{tag}_dh", dup, wu, grid=(t // tm, 1, 1),
        a_spec=pl.BlockSpec((2, tm, dff), lambda i, j, k: (0, i, 0)),
        b_spec=pl.BlockSpec((2 * dff, d), lambda i, j, k: (0, 0), pipeline_mode=pl.Buffered(1)),
        out_spec=[row, vec, vec],
        out_shape=[jax.ShapeDtypeStruct((t, d), F32)] + [jax.ShapeDtypeStruct((1, d), F32)] * 2,
        dims=NN, acc_shape=(tm, d), extras=(r_in, gain, dr), extra_specs=(row, vec, row),
        epilogue=norm_backward, n_sums=2, parts=2)
    dwu = _mm(f"{tag}_dwu", dup, h, grid=(2 * nc, 1, t // tk),
              a_spec=pl.BlockSpec((None, tk, cw), lambda i, j, k: (i // nc, k, i % nc)),
              b_spec=pl.BlockSpec((tk, d), lambda i, j, k: (k, 0)),
              out_spec=pl.BlockSpec((cw, d), lambda i, j, k: (i, 0)),
              out_shape=jax.ShapeDtypeStruct((2 * dff, d), BF16), dims=TN, acc_shape=(cw, d))
    return dr_in, dgain, dwu, dwd, dwdw, dbdw, colsum


def _pad_to(vec, n):
    return jnp.pad(vec, (0, n - vec.shape[0]))


def _pack(parts, width):
    flat = jnp.concatenate([p.reshape(-1).astype(F32) for p in parts])
    n = -(-flat.shape[0] // (8 * width)) * (8 * width)
    return _pad_to(flat, n).reshape(n // width, width)


def _unpack(mat, shapes):
    flat = mat.reshape(-1)
    out, off = [], 0
    for s in shapes:
        n = 1
        for dim in s:
            n *= dim
        out.append(flat[off:off + n].reshape(s))
        off += n
    return out


def kernel(x, norm_mix, norm_ffn, conv_w_pw1, conv_b_pw1, conv_w_dw, conv_b_dw, conv_ln_g, conv_ln_b, conv_w_pw2, conv_b_pw2, pool_w, pool_b, pool_scale, ffn_w_up, ffn_w_dw, ffn_b_dw, ffn_w_down, final_norm, loss_target, m_norm_mix, m_norm_ffn, m_conv_w_pw1, m_conv_b_pw1, m_conv_w_dw, m_conv_b_dw, m_conv_ln_g, m_conv_ln_b, m_conv_w_pw2, m_conv_b_pw2, m_pool_w, m_pool_b, m_pool_scale, m_ffn_w_up, m_ffn_w_dw, m_ffn_b_dw, m_ffn_w_down, m_final_norm, v_norm_mix, v_norm_ffn, v_conv_w_pw1, v_conv_b_pw1, v_conv_w_dw, v_conv_b_dw, v_conv_ln_g, v_conv_ln_b, v_conv_w_pw2, v_conv_b_pw2, v_pool_w, v_pool_b, v_pool_scale, v_ffn_w_up, v_ffn_w_dw, v_ffn_b_dw, v_ffn_w_down, v_final_norm):
    bsz, seq, d = x.shape
    t = bsz * seq
    k_taps = conv_w_dw.shape[1]
    cs1 = conv_w_pw1.shape[2]
    dsh = d // N_DEV
    cg = d // N_GROUPS
    cgs = pool_w.shape[2]
    fu = ffn_w_up.shape[2]
    fd = ffn_w_down.shape[1]
    dff = fd * N_DEV
    nb = N_DEV // 2
    kf = ffn_w_dw.shape[1]
    fsh = ffn_w_dw.shape[2]
    my = _lin(_me())
    tm = _tile(t, 512)

    x2 = x.reshape(t, d)
    tgt2 = loss_target.reshape(t, d)

    small_shapes = [(k_taps, dsh), (dsh,), (dsh,), (2, kf, fsh)]
    small_mine = _pack([conv_w_dw[0], pool_b[0], pool_scale[0], ffn_w_dw], LANE)
    big = [conv_w_pw1[0], conv_w_pw2[0], ffn_w_up[0].T, ffn_w_down[0], pool_w[0], ffn_w_up[1].T, ffn_w_down[1]]
    gather = _exchange_start("gather_start", [small_mine] + [w.astype(BF16) for w in big], [False] * 8)
    h0 = _rms_fwd("l0_rms", x2, norm_mix[0:1])
    small_all, w1 = _exchange_wait("gather_wait_w1", gather, [0, 1], h0)
    parts = [_unpack(small_all[dev], small_shapes) for dev in range(N_DEV)]
    wdw = jnp.concatenate([p[0] for p in parts], axis=1)
    pool_b_full = jnp.concatenate([p[1] for p in parts]).reshape(1, d)
    pool_s_full = jnp.concatenate([p[2] for p in parts]).reshape(1, d)
    fwdw = jnp.concatenate([p[3] for p in parts], axis=2)
    fbdw = ffn_b_dw.reshape(2, 1, dff)

    def columns(w):
        return w.transpose(1, 0, 2).reshape(w.shape[1], N_DEV * w.shape[2])

    def column_shards(w):
        return w.reshape(w.shape[0], N_DEV, w.shape[1] // N_DEV).transpose(1, 0, 2)

    w1 = columns(w1)
    a = _mm("l0_pw1", h0, w1, grid=(2, t // tm, 1),
            a_spec=pl.BlockSpec((tm, d), lambda j, i, k: (i, 0)),
            b_spec=pl.BlockSpec((d, d), lambda j, i, k: (0, j)),
            out_spec=pl.BlockSpec((None, tm, d), lambda j, i, k: (j, i, 0)),
            out_shape=jax.ShapeDtypeStruct((2, t, d), BF16), dims=NN, acc_shape=(tm, d),
            extras=(conv_b_pw1,), extra_specs=(pl.BlockSpec((1, d), lambda j, i, k: (0, j)),),
            epilogue=lambda acc, ex, rows: ((acc + ex[0][...],), ()))
    (w2,) = _exchange_wait("gather_wait_w2", gather, [2], a)
    w2 = w2.reshape(d, d)
    v = _conv_fwd("l0_conv", a, wdw, conv_b_dw, seq)
    row = pl.BlockSpec((tm, d), lambda i, j, k: (i, 0))
    vec = pl.BlockSpec((1, d), lambda i, j, k: (0, 0))
    square = pl.BlockSpec((d, d), lambda i, j, k: (0, 0))

    def ln_silu(v_blk, ex):
        s_blk = _ln_silu_tile(v_blk, ex[0][...], ex[1][...]).astype(BF16)
        return s_blk, s_blk

    def residual_and_norm(acc, ex, rows):
        r_blk = ex[3][rows, :] + (acc + ex[2][...])
        return (r_blk, _rms(r_blk, ex[4][...])), ()

    r1, h1, s = _mm("l0_pw2", v, w2, grid=(t // tm, 1, 1), a_spec=row, b_spec=square, out_spec=[row, row, row],
                    out_shape=[jax.ShapeDtypeStruct((t, d), F32), jax.ShapeDtypeStruct((t, d), BF16),
                               jax.ShapeDtypeStruct((t, d), BF16)],
                    dims=NN, acc_shape=(tm, d), extras=(conv_ln_g, conv_ln_b, conv_b_pw2, x2, norm_ffn[0:1]),
                    extra_specs=(vec, vec, vec, row, vec), prologue=ln_silu, epilogue=residual_and_norm)

    def up_getter(name, idx):
        return lambda after: _exchange_wait(name, gather, [idx], after)[0].reshape(2 * dff, d)

    def down_getter(name, idx):
        return lambda after: _exchange_wait(name, gather, [idx], after)[0].reshape(dff, d)

    r2, ffn0_saved = _ffn_forward("f0", r1, h1, up_getter("gather_wait_wu0", 3),
                                  down_getter("gather_wait_wd0", 4), fwdw[0], fbdw[0], seq)
    (wp,) = _exchange_wait("gather_wait_wp", gather, [5], r2)
    wp = wp.transpose(1, 0, 2, 3).reshape(N_GROUPS, cg, cg)
    pooled, r3, h3 = _pool_mix_fwd("l1_mix", r2, norm_mix[1:2], wp, pool_s_full, pool_b_full, norm_ffn[1:2], seq)
    (dr4, loss_part, dfinal), ffn1_saved = _ffn_forward(
        "f1", r3, h3, up_getter("gather_wait_wu1", 6), down_getter("gather_wait_wd1", 7), fwdw[1], fbdw[1], seq,
        loss=(tgt2, final_norm.reshape(1, d)))

    dr3, dnf1, dwu1, dwd1, dfw1, dfb1, _ = _ffn_backward("f1", dr4, ffn1_saved, norm_ffn[1:2], fwdw[1], fbdw[1], seq)
    scatter_a = _exchange_start("scatter_f1_start", [dwu1.reshape(N_DEV, fu, d), dwd1.reshape(N_DEV, fd, d)],
                                [True, True])
    dr2, dwp, dpool_s, dpool_b, dnm1 = _pool_mix_bwd(
        "l1_dmix", pooled, wp, dr3, r2, norm_mix[1:2], pool_s_full + scatter_a["token"][0:1, 0:1], pool_b_full, seq)
    dr1, dnf0, dwu0, dwd0, dfw0, dfb0, db2 = _ffn_backward("f0", dr2, ffn0_saved, norm_ffn[0:1], fwdw[0], fbdw[0], seq)
    dwp_b = dwp.astype(BF16).reshape(N_GROUPS, N_DEV, cgs, cg).transpose(1, 0, 2, 3)
    scatter_b = _exchange_start("scatter_f0_start", [dwu0.reshape(N_DEV, fu, d), dwd0.reshape(N_DEV, fd, d), dwp_b],
                                [True] * 3)
    def ln_silu_backward(acc, ex, rows):
        dv_blk, dgain, dbias, colsum = _ln_silu_bwd_tile(acc, ex[0][rows, :], ex[1][...], ex[2][...])
        return (dv_blk,), (dgain, dbias, colsum)

    dv, dlg, dlb, dbdw = _mm("l0_ds", dr1, w2, grid=(t // tm, 1, 1), a_spec=row, b_spec=square,
                             out_spec=[row, vec, vec, vec],
                             out_shape=[jax.ShapeDtypeStruct((t, d), F32)] + [jax.ShapeDtypeStruct((1, d), F32)] * 3,
                             dims=NT, acc_shape=(tm, d), extras=(v, conv_ln_g, conv_ln_b), extra_specs=(row, vec, vec),
                             epilogue=ln_silu_backward, n_sums=3, token=scatter_b["token"])
    tk = _tile(t, 2048)
    dw2 = _mm("l0_dw2", s, dr1, grid=(1, 1, t // tk),
              a_spec=pl.BlockSpec((tk, d), lambda i, j, k: (k, 0)),
              b_spec=pl.BlockSpec((tk, d), lambda i, j, k: (k, 0)),
              out_spec=pl.BlockSpec((d, d), lambda i, j, k: (0, 0)),
              out_shape=jax.ShapeDtypeStruct((d, d), BF16), dims=TN, acc_shape=(d, d))
    da, dwdw, db1 = _conv_bwd("l0_dconv", a, dv, wdw, seq)
    dw1 = _mm("l0_dw1", h0, da, grid=(1, 2, t // tk),
              a_spec=pl.BlockSpec((tk, d), lambda i, j, k: (k, 0)),
              b_spec=pl.BlockSpec((None, tk, d), lambda i, j, k: (j, k, 0)),
              out_spec=pl.BlockSpec((d, d), lambda i, j, k: (0, j)),
              out_shape=jax.ShapeDtypeStruct((d, 2 * d), BF16), dims=TN, acc_shape=(d, d))
    scatter_c = _exchange_start("scatter_l0_start", [column_shards(dw1), dw2.reshape(N_DEV, d // N_DEV, d)],
                                [True, True])
    def norm_backward(acc, ex, rows):
        dx_blk, dgain, colsum = _rms_bwd_tile(acc, ex[0][rows, :], ex[1][...], ex[2][rows, :])
        return (dx_blk,), (dgain, colsum)

    dx, dnm0, _ = _mm("l0_dh", da, w1, grid=(t // tm, 1, 1),
                      a_spec=pl.BlockSpec((2, tm, d), lambda i, j, k: (0, i, 0)),
                      b_spec=pl.BlockSpec((d, 2 * d), lambda i, j, k: (0, 0), pipeline_mode=pl.Buffered(1)),
                      out_spec=[row, vec, vec],
                      out_shape=[jax.ShapeDtypeStruct((t, d), F32)] + [jax.ShapeDtypeStruct((1, d), F32)] * 2,
                      dims=NT, acc_shape=(tm, d), extras=(x2, norm_mix[0:1], dr1), extra_specs=(row, vec, row),
                      epilogue=norm_backward, n_sums=2, parts=2, token=scatter_c["token"])

    dffn_w = jnp.stack([dfw0, dfw1])
    dffn_b = jnp.stack([dfb0, dfb1]).reshape(2, dff)
    small_parts = [loss_part, jnp.concatenate([dnm0, dnm1]), jnp.concatenate([dnf0, dnf1]), db1, dwdw, dbdw, dlg, dlb,
                   db2, dpool_b, dpool_s, dffn_w, dffn_b, dfinal]
    small_part_shapes = [(1,), (2, d), (2, d), (1, 2 * d), (k_taps, d), (1, d), (1, d), (1, d), (1, d), (1, d), (1, d),
                         (2, kf, dff), (2, dff), (d,)]
    packed = _pack(small_parts, 8 * LANE)
    gather_small = _exchange_start("gather_small_start", [packed], [False])

    def big_update(name, recv, w, m, v, layer=0, prev=None):
        shape = w.shape
        c = recv.shape[-1]
        rows = recv.size // (N_DEV * c)
        nl = w.size // (rows * c)
        outs = _adamw(name, recv.reshape(N_DEV, rows, c), w.reshape(nl, rows, c), m.reshape(nl, rows, c),
                      v.reshape(nl, rows, c), layer, prev)
        return outs, [o.reshape(shape) for o in outs]

    wu_t = [p.transpose(0, 2, 1) for p in (ffn_w_up, m_ffn_w_up, v_ffn_w_up)]
    g_wu1, g_wd1 = _exchange_wait("scatter_f1_wait", scatter_a, [0, 1], gather_small["token"])
    raw_wu, _ = big_update("adam_wu1", g_wu1, *wu_t, 1)
    raw_wd, _ = big_update("adam_wd1", g_wd1, ffn_w_down, m_ffn_w_down, v_ffn_w_down, 1)
    g_wu0, g_wd0, g_wp = _exchange_wait("scatter_f0_wait", scatter_b, [0, 1, 2], raw_wd[0])
    _, u_wu = big_update("adam_wu0", g_wu0, *wu_t, 0, raw_wu)
    u_wu = [o.transpose(0, 2, 1) for o in u_wu]
    _, u_wd = big_update("adam_wd0", g_wd0, ffn_w_down, m_ffn_w_down, v_ffn_w_down, 0, raw_wd)
    _, u_wp = big_update("adam_wp", g_wp, pool_w, m_pool_w, v_pool_w)
    g_w1, g_w2 = _exchange_wait("scatter_l0_wait", scatter_c, [0, 1], u_wp[0])
    _, u_w1 = big_update("adam_w1", g_w1, conv_w_pw1, m_conv_w_pw1, v_conv_w_pw1)
    _, u_w2 = big_update("adam_w2", g_w2, conv_w_pw2, m_conv_w_pw2, v_conv_w_pw2)
    (all_small,) = _exchange_wait("gather_small_wait", gather_small, [0], u_w2[0])
    summed = _sum_rows("sum_small_grads", all_small)
    (loss_v, g_nm, g_nf, g_b1, g_wdw, g_bdw, g_lg, g_lb, g_b2, g_pb, g_ps, g_fw, g_fb,
     g_fin) = _unpack(summed, small_part_shapes)
    loss = loss_v[0]
    g_wdw_mine = lax.dynamic_slice_in_dim(g_wdw, my * dsh, dsh, axis=1)[None]
    g_pb_mine = lax.dynamic_slice_in_dim(g_pb, my * dsh, dsh, axis=1)
    g_ps_mine = lax.dynamic_slice_in_dim(g_ps, my * dsh, dsh, axis=1)
    g_fw_mine = lax.dynamic_slice_in_dim(g_fw, my * fsh, fsh, axis=2)

    small_g =[g_nm, g_nf, g_b1, g_wdw_mine, g_bdw, g_lg, g_lb, g_b2, g_pb_mine, g_ps_mine, g_fw_mine, g_fb, g_fin]
    small_w = [norm_mix, norm_ffn, conv_b_pw1, conv_w_dw, conv_b_dw, conv_ln_g, conv_ln_b, conv_b_pw2, pool_b,
               pool_scale, ffn_w_dw, ffn_b_dw, final_norm]
    small_m = [m_norm_mix, m_norm_ffn, m_conv_b_pw1, m_conv_w_dw, m_conv_b_dw, m_conv_ln_g, m_conv_ln_b,
               m_conv_b_pw2, m_pool_b, m_pool_scale, m_ffn_w_dw, m_ffn_b_dw, m_final_norm]
    small_v = [v_norm_mix, v_norm_ffn, v_conv_b_pw1, v_conv_w_dw, v_conv_b_dw, v_conv_ln_g, v_conv_ln_b,
               v_conv_b_pw2, v_pool_b, v_pool_scale, v_ffn_w_dw, v_ffn_b_dw, v_final_norm]
    shapes = [w.shape for w in small_w]
    outs = _adamw("adam_small", _pack(small_g, 8 * LANE)[None], _pack(small_w, 8 * LANE)[None],
                  _pack(small_m, 8 * LANE)[None], _pack(small_v, 8 * LANE)[None])
    sg, sd, sm, sv = [_unpack(o, shapes) for o in outs]

    def leaf(kind):
        (nm, nf, b1, wdw_, bdw_, lg, lb, b2, pb, ps, fw, fb, fin) = (sg, sd, sm, sv)[kind]
        return [nm, nf, u_w1[kind], b1, wdw_, bdw_, lg, lb, u_w2[kind], b2, u_wp[kind], pb, ps, u_wu[kind], fw, fb,
                u_wd[kind], fin]

    return (loss, dx.reshape(bsz, seq, d), *leaf(0), *leaf(1), *leaf(2), *leaf(3))
```

```python
import functools

import jax
import jax.numpy as jnp
from jax import lax
from jax.experimental import pallas as pl
from jax.experimental.pallas import tpu as pltpu

F32 = jnp.float32
BF16 = jnp.bfloat16
MESH = pl.DeviceIdType.MESH
HBM = pl.BlockSpec(memory_space=pltpu.HBM)

N_DEV = 8
RMS_EPS = 1e-6
LN_EPS = 1e-5
POOL_WINDOWS = (2, 4, 8, 16)
N_GROUPS = len(POOL_WINDOWS)
ADAM_LR = 0.001
ADAM_B1 = 0.9
ADAM_B2 = 0.999
ADAM_EPS = 1e-08
ADAM_WD = 0.01
ADAM_STEP = 10

LANE = 128
HALO = 32
HALO16 = 16
VMEM_LIMIT = 56 * 1024 * 1024


def _params(*sem):
    return pltpu.CompilerParams(dimension_semantics=sem if sem else None, vmem_limit_bytes=VMEM_LIMIT)


def _tile(n, pref):
    for t in range(min(pref, n), 15, -1):
        if n % t == 0 and t % 16 == 0:
            return t
    return n


def _sigmoid(z):
    return 1.0 / (1.0 + jnp.exp(-z))


def _me():
    return lax.axis_index("x"), lax.axis_index("y"), lax.axis_index("c")


def _flip(pos, m):
    x, y, c = pos
    return ((1 - x) if m & 4 else x, (1 - y) if m & 2 else y, (1 - c) if m & 1 else c)


def _lin(pos):
    return 4 * pos[0] + 2 * pos[1] + pos[2]


SEM = pl.BlockSpec(memory_space=pltpu.SEMAPHORE)
ANY = pl.BlockSpec(memory_space=pl.ANY)
EFFECT = pltpu.SideEffectType.DATAFLOW_SIDE_EFFECTING


def _exchange_copies(srcs, lands, send_sems, recv_sems, modes, which, starting):
    me = _me()
    my = _lin(me)
    out = []
    for pos, a in enumerate(which):
        src, land = srcs[pos], lands[pos]

        def block(pid, src=src, a=a):
            return src.at[pid] if modes[a] else src

        local = pltpu.make_async_copy(block(my), land.at[my], send_sems.at[a * N_DEV])
        remote = []
        for m in range(1, N_DEV):
            peer = _flip(me, m)
            pid = _lin(peer)
            sems = dict(send_sem=send_sems.at[a * N_DEV + m], recv_sem=recv_sems.at[a * N_DEV + m],
                        device_id=peer, device_id_type=MESH)
            if starting:
                remote.append(pltpu.make_async_remote_copy(src_ref=block(pid), dst_ref=land.at[my], **sems))
            else:
                remote.append((pltpu.make_async_remote_copy(src_ref=block(pid), dst_ref=land.at[my], **sems),
                               pltpu.make_async_remote_copy(src_ref=block(pid), dst_ref=land.at[pid], **sems)))
        out.append((local, remote))
    return out


def _exchange_start(name, arrs, modes):
    n = len(arrs)
    blocks = [a.shape[1:] if md else a.shape for a, md in zip(arrs, modes)]

    def body(*refs):
        srcs, lands = refs[:n], refs[n:2 * n]
        send_sems, recv_sems = refs[2 * n], refs[2 * n + 1]
        token = refs[-1]
        for local, remote in _exchange_copies(srcs, lands, send_sems, recv_sems, modes, list(range(n)), True):
            local.start()
            for send in remote:
                send.start()
        token[...] = jnp.zeros_like(token)

    lands = [lax.empty((N_DEV,) + tuple(b), a.dtype) for a, b in zip(arrs, blocks)]
    outs = pl.pallas_call(
        body, name=name,
        out_shape=(pltpu.SemaphoreType.DMA((n * N_DEV,)), pltpu.SemaphoreType.DMA((n * N_DEV,)),
                   *[pltpu.HBM(a.shape, a.dtype) for a in arrs], *[pltpu.HBM(l.shape, l.dtype) for l in lands],
                   jax.ShapeDtypeStruct((8, LANE), F32)),
        in_specs=[HBM] * (2 * n),
        out_specs=(SEM, SEM, *[HBM] * (2 * n), pl.BlockSpec(memory_space=pltpu.VMEM)),
        input_output_aliases={i: 2 + i for i in range(2 * n)},
        compiler_params=pltpu.CompilerParams(has_side_effects=EFFECT),
    )(*[pltpu.with_memory_space_constraint(a, pltpu.HBM) for a in arrs],
      *[pltpu.with_memory_space_constraint(l, pltpu.HBM) for l in lands])
    return dict(send=outs[0], recv=outs[1], srcs=list(outs[2:2 + n]), lands=list(outs[2 + n:2 + 2 * n]),
                modes=modes, token=outs[-1])


def _exchange_wait(name, handle, which, after):
    k = len(which)
    modes = handle["modes"]

    def body(*refs):
        srcs, lands = refs[:k], refs[k:2 * k]
        send_sems, recv_sems = refs[2 * k], refs[2 * k + 1]
        for local, remote in _exchange_copies(srcs, lands, send_sems, recv_sems, modes, which, False):
            local.wait()
            for send, arrival in remote:
                send.wait_send()
                arrival.wait_recv()

    srcs = [handle["srcs"][a] for a in which]
    lands = [handle["lands"][a] for a in which]
    outs = pl.pallas_call(
        body, name=name,
        out_shape=tuple(pltpu.HBM(x.shape, x.dtype) for x in srcs + lands),
        in_specs=[HBM] * (2 * k) + [SEM, SEM, ANY], out_specs=tuple([HBM] * (2 * k)),
        input_output_aliases={i: i for i in range(2 * k)},
        compiler_params=pltpu.CompilerParams(has_side_effects=EFFECT),
    )(*srcs, *lands, handle["send"], handle["recv"], after)
    return list(outs[k:])


def _mm(name, a, b, *, grid, a_spec, b_spec, out_spec, out_shape, dims, acc_shape, extras=(), extra_specs=(),
        epilogue=None, token=None, prologue=None, n_sums=0, parts=1):
    nk = grid[2]
    ne = len(extras)
    deps = () if token is None else (token,)
    dep_specs = [pl.BlockSpec((8, LANE), lambda i, j, k: (0, 0))] * len(deps)
    n_out = len(out_shape) if isinstance(out_shape, (list, tuple)) else 1
    n_tiles = n_out - n_sums - (1 if prologue is not None else 0)

    def body(a_ref, b_ref, *rest):
        ex, o_refs, acc_ref = rest[:ne], rest[ne + len(deps):ne + len(deps) + n_out], rest[ne + len(deps) + n_out]
        k = pl.program_id(2)
        if parts == 1:
            a_blk, saved = a_ref[...], None
            if prologue is not None:
                a_blk, saved = prologue(a_blk, ex)
                o_refs[n_tiles][...] = saved
            part = lax.dot_general(a_blk.astype(BF16), b_ref[...].astype(BF16), (dims, ((), ())),
                                   preferred_element_type=F32)
        else:
            kb = b_ref.shape[dims[1][0]] // parts
            part = None
            for p in range(parts):
                b_blk = b_ref[p * kb:(p + 1) * kb, :] if dims[1][0] == 0 else b_ref[:, p * kb:(p + 1) * kb]
                term = lax.dot_general(a_ref[p].astype(BF16), b_blk.astype(BF16), (dims, ((), ())),
                                       preferred_element_type=F32)
                part = term if part is None else part + term
        sum_refs = o_refs[n_out - n_sums:]

        def add_sums(terms):
            @pl.when((pl.program_id(0) == 0) & (pl.program_id(1) == 0))
            def _():
                for o_ref in sum_refs:
                    o_ref[...] = jnp.zeros_like(o_ref)

            for o_ref, term in zip(sum_refs, terms):
                o_ref[...] += jnp.sum(term, axis=0, keepdims=True)

        def finish(r):
            tiles, terms = ((r,), ()) if epilogue is None else epilogue(r, ex, slice(None))
            for o_ref, val in zip(o_refs, tiles):
                o_ref[...] = val.astype(o_ref.dtype)
            if n_sums:
                add_sums(terms)

        if nk == 1:
            finish(part)
            return

        @pl.when(k == 0)
        def _():
            acc_ref[...] = part

        @pl.when((k > 0) & (k < nk - 1))
        def _():
            acc_ref[...] += part

        @pl.when(k == nk - 1)
        def _():
            finish(acc_ref[...] + part)

    return pl.pallas_call(
        body, name=name, grid=grid, in_specs=[a_spec, b_spec, *extra_specs, *dep_specs], out_specs=out_spec,
        out_shape=out_shape, scratch_shapes=[pltpu.VMEM(acc_shape if nk > 1 else (8, LANE), F32)],
        compiler_params=_params(*(("arbitrary",) * 3 if n_sums else ("parallel", "parallel", "arbitrary"))),
    )(a, b, *extras, *deps)


def _rms(x, gain):
    return x * lax.rsqrt(jnp.mean(x * x, axis=-1, keepdims=True) + RMS_EPS) * gain


def _rms_bwd_tile(dh, x, gain, dres):
    rstd = lax.rsqrt(jnp.mean(x * x, axis=-1, keepdims=True) + RMS_EPS)
    xhat = x * rstd
    dxhat = dh * gain
    dx = dres + rstd * (dxhat - xhat * jnp.mean(dxhat * xhat, axis=-1, keepdims=True))
    return dx, dh * xhat, dx


def _ln_silu_tile(v, g, b):
    mu = jnp.mean(v, axis=-1, keepdims=True)
    cen = v - mu
    z = cen * lax.rsqrt(jnp.mean(cen * cen, axis=-1, keepdims=True) + LN_EPS) * g + b
    return z * _sigmoid(z)


def _ln_silu_bwd_tile(ds, v, g, b):
    mu = jnp.mean(v, axis=-1, keepdims=True)
    cen = v - mu
    rstd = lax.rsqrt(jnp.mean(cen * cen, axis=-1, keepdims=True) + LN_EPS)
    y = cen * rstd
    z = y * g + b
    sig = _sigmoid(z)
    dz = ds * sig * (1.0 + z * (1.0 - sig))
    dy = dz * g
    dv = rstd * (dy - jnp.mean(dy, axis=-1, keepdims=True) - y * jnp.mean(dy * y, axis=-1, keepdims=True))
    return dv, dz * y, dz, dv


def _loss_tile(x, tgt, gain):
    d = x.shape[-1]
    rstd = lax.rsqrt(jnp.mean(x * x, axis=-1, keepdims=True) + RMS_EPS)
    xhat = x * rstd
    err = xhat * gain - tgt
    dy = err / d
    dxhat = dy * gain
    dx = rstd * (dxhat - xhat * jnp.mean(dxhat * xhat, axis=-1, keepdims=True))
    return dx, 0.5 * jnp.mean(err * err, axis=-1, keepdims=True), dy * xhat


NN = ((1,), (0,))
NT = ((1,), (1,))
TN = ((0,), (0,))


def _rms_fwd(name, x, gain):
    t, d = x.shape
    tr = _tile(t, 512)

    def body(x_ref, g_ref, h_ref):
        h_ref[...] = _rms(x_ref[...], g_ref[...]).astype(BF16)

    return pl.pallas_call(
        body, name=name, grid=(t // tr,),
        in_specs=[pl.BlockSpec((tr, d), lambda i: (i, 0)), pl.BlockSpec((1, d), lambda i: (0, 0))],
        out_specs=pl.BlockSpec((tr, d), lambda i: (i, 0)),
        out_shape=jax.ShapeDtypeStruct((t, d), BF16), compiler_params=_params("parallel"),
    )(x, gain)


def _conv_tiles(t, seq):
    ts = _tile(seq, 512)
    return ts, seq // ts, _tile(ts, 64)


def _conv_fwd(name, a, w, b, seq):
    _, t, d = a.shape
    k_taps = w.shape[0]
    ts, tps, rc = _conv_tiles(t, seq)
    hb = ts // HALO

    def body(cur_ref, prev_ref, w_ref, b_ref, v_ref, upad):
        i = pl.program_id(1)
        first = (i % tps) == 0
        pv = prev_ref[0].astype(F32)
        pg = prev_ref[1].astype(F32)
        upad[0:HALO, :] = jnp.where(first, 0.0, pv * _sigmoid(pg))
        upad[HALO:HALO + ts, :] = cur_ref[0].astype(F32) * _sigmoid(cur_ref[1].astype(F32))
        wv = w_ref[...]
        bias = jnp.broadcast_to(b_ref[...], (rc, LANE))
        for r0 in range(0, ts, rc):
            acc = bias
            for k in range(k_taps):
                acc = acc + wv[k:k + 1, :] * upad[pl.ds(HALO - (k_taps - 1) + k + r0, rc), :]
            v_ref[pl.ds(r0, rc), :] = acc

    return pl.pallas_call(
        body, name=name, grid=(d // LANE, t // ts),
        in_specs=[pl.BlockSpec((2, ts, LANE), lambda c, i: (0, i, c)),
                  pl.BlockSpec((2, HALO, LANE), lambda c, i: (0, jnp.maximum(i * hb - 1, 0), c)),
                  pl.BlockSpec((k_taps, LANE), lambda c, i: (0, c)),
                  pl.BlockSpec((1, LANE), lambda c, i: (0, c))],
        out_specs=pl.BlockSpec((ts, LANE), lambda c, i: (i, c)),
        out_shape=jax.ShapeDtypeStruct((t, d), F32),
        scratch_shapes=[pltpu.VMEM((HALO + ts, LANE), F32)],
        compiler_params=_params("parallel", "parallel"),
    )(a, a, w, b)


def _conv_bwd(name, a, dv, w, seq):
    _, t, d = a.shape
    k_taps = w.shape[0]
    ts, tps, rc = _conv_tiles(t, seq)
    hb = ts // HALO
    nhb = t // HALO

    def body(cur_ref, prev_ref, dv_ref, ndv_ref, w_ref, da_ref, dw_ref, dbp_ref, upad, dvpad, dwrows):
        i = pl.program_id(1)
        first = (i % tps) == 0
        last = (i % tps) == tps - 1
        pv = prev_ref[0].astype(F32)
        pg = prev_ref[1].astype(F32)
        upad[0:HALO, :] = jnp.where(first, 0.0, pv * _sigmoid(pg))
        upad[HALO:HALO + ts, :] = cur_ref[0].astype(F32) * _sigmoid(cur_ref[1].astype(F32))
        dvpad[0:ts, :] = dv_ref[...]
        dvpad[ts:ts + HALO, :] = jnp.where(last, 0.0, ndv_ref[...])
        wv = w_ref[...]

        @pl.when(i == 0)
        def _():
            dw_ref[...] = jnp.zeros_like(dw_ref)
            dbp_ref[...] = jnp.zeros_like(dbp_ref)

        sv = jnp.zeros((1, LANE), F32)
        sg = jnp.zeros((1, LANE), F32)
        for r0 in range(0, ts, rc):
            du = jnp.zeros((rc, LANE), F32)
            for k in range(k_taps):
                du = du + wv[k:k + 1, :] * dvpad[pl.ds(r0 + (k_taps - 1) - k, rc), :]
            av = cur_ref[0, pl.ds(r0, rc), :].astype(F32)
            sig = _sigmoid(cur_ref[1, pl.ds(r0, rc), :].astype(F32))
            dval = du * sig
            dgate = du * av * sig * (1.0 - sig)
            da_ref[0, pl.ds(r0, rc), :] = dval.astype(BF16)
            da_ref[1, pl.ds(r0, rc), :] = dgate.astype(BF16)
            sv = sv + jnp.sum(dval, axis=0, keepdims=True)
            sg = sg + jnp.sum(dgate, axis=0, keepdims=True)
        dbp_ref[0] += sv
        dbp_ref[1] += sg

        for k in range(k_taps):
            acc = jnp.zeros((rc, LANE), F32)
            for r0 in range(0, ts, rc):
                acc = acc + dvpad[pl.ds(r0, rc), :] * upad[pl.ds(HALO - (k_taps - 1) + k + r0, rc), :]
            dwrows[k:k + 1, :] = jnp.sum(acc, axis=0, keepdims=True)
        dw_ref[...] += dwrows[0:k_taps, :]

    return pl.pallas_call(
        body, name=name, grid=(d // LANE, t // ts),
        in_specs=[pl.BlockSpec((2, ts, LANE), lambda c, i: (0, i, c)),
                  pl.BlockSpec((2, HALO, LANE), lambda c, i: (0, jnp.maximum(i * hb - 1, 0), c)),
                  pl.BlockSpec((ts, LANE), lambda c, i: (i, c)),
                  pl.BlockSpec((HALO, LANE), lambda c, i: (jnp.minimum((i + 1) * hb, nhb - 1), c)),
                  pl.BlockSpec((k_taps, LANE), lambda c, i: (0, c))],
        out_specs=[pl.BlockSpec((2, ts, LANE), lambda c, i: (0, i, c)),
                   pl.BlockSpec((k_taps, LANE), lambda c, i: (0, c)),
                   pl.BlockSpec((2, 1, LANE), lambda c, i: (0, 0, c))],
        out_shape=[jax.ShapeDtypeStruct((2, t, d), BF16), jax.ShapeDtypeStruct((k_taps, d), F32),
                   jax.ShapeDtypeStruct((2, 1, d), F32)],
        scratch_shapes=[pltpu.VMEM((HALO + ts, LANE), F32), pltpu.VMEM((ts + HALO, LANE), F32),
                        pltpu.VMEM((HALO, LANE), F32)],
        compiler_params=_params("parallel", "arbitrary"),
    )(a, a, dv, dv, w)


def _pool_mix_fwd(name, x, gain, wp, scale, bias, next_gain, seq):
    t, d = x.shape
    ts = _tile(seq, 256)
    tps = seq // ts
    hb = ts // HALO
    cg = d // N_GROUPS

    def body(cur_ref, prev_ref, g_ref, w_ref, s_ref, b_ref, ng_ref, p_ref, r_ref, h_ref, hpad):
        i = pl.program_id(0)
        first = (i % tps) == 0
        g = g_ref[...]
        hpad[0:HALO, :] = jnp.where(first, 0.0, _rms(prev_ref[...], g))
        hpad[HALO:HALO + ts, :] = _rms(cur_ref[...], g)
        pos = (i % tps) * ts + lax.broadcasted_iota(jnp.int32, (ts, 1), 0)
        for gi, win in enumerate(POOL_WINDOWS):
            sl = slice(gi * cg, (gi + 1) * cg)
            own = hpad[HALO:HALO + ts, sl]
            acc = own
            for j in range(1, win):
                acc = acc + hpad[HALO - j:HALO - j + ts, sl]
            cnt = jnp.minimum(pos + 1, win).astype(F32)
            pooled = (acc / cnt - own).astype(BF16)
            p_ref[:, sl] = pooled
            mixed = jnp.dot(pooled, w_ref[gi], preferred_element_type=F32)
            r_ref[:, sl] = cur_ref[:, sl] + s_ref[:, sl] * (mixed + b_ref[:, sl])
        h_ref[...] = _rms(r_ref[...], ng_ref[...]).astype(BF16)

    row = pl.BlockSpec((ts, d), lambda i: (i, 0))
    vec = pl.BlockSpec((1, d), lambda i: (0, 0))
    return pl.pallas_call(
        body, name=name, grid=(t // ts,),
        in_specs=[row, pl.BlockSpec((HALO, d), lambda i: (jnp.maximum(i * hb - 1, 0), 0)), vec,
                  pl.BlockSpec((N_GROUPS, cg, cg), lambda i: (0, 0, 0)), vec, vec, vec],
        out_specs=[row, row, row],
        out_shape=[jax.ShapeDtypeStruct((t, d), BF16), jax.ShapeDtypeStruct((t, d), F32),
                   jax.ShapeDtypeStruct((t, d), BF16)],
        scratch_shapes=[pltpu.VMEM((HALO + ts, d), F32)],
        compiler_params=_params("parallel"),
    )(x, x, gain, wp, scale, bias, next_gain)


def _pool_mix_bwd(name, pooled, wp, dr, x, gain, scale, bias, seq):
    t, d = x.shape
    ts = _tile(seq, 256)
    tps = seq // ts
    hb = ts // HALO
    nhb = t // HALO
    cg = d // N_GROUPS

    def body(p_ref, w_ref, dr_ref, ndr_ref, x_ref, g_ref, s_ref, b_ref, dx_ref, dw_ref, ds_ref, db_ref, dg_ref,
             qpad, dh):
        i = pl.program_id(0)
        last = (i % tps) == tps - 1
        pos = (i % tps) * ts + lax.broadcasted_iota(jnp.int32, (ts, 1), 0)

        @pl.when(i == 0)
        def _():
            dw_ref[...] = jnp.zeros_like(dw_ref)
            ds_ref[...] = jnp.zeros_like(ds_ref)
            db_ref[...] = jnp.zeros_like(db_ref)
            dg_ref[...] = jnp.zeros_like(dg_ref)

        for gi, win in enumerate(POOL_WINDOWS):
            sl = slice(gi * cg, (gi + 1) * cg)
            wv = w_ref[gi]
            sc = s_ref[:, sl]
            drv = dr_ref[:, sl]
            dmx = drv * sc
            dmx16 = dmx.astype(BF16)
            pooled = p_ref[:, sl]
            dw_ref[gi] += lax.dot_general(pooled, dmx16, (TN, ((), ())), preferred_element_type=F32)
            mixed = jnp.dot(pooled, wv, preferred_element_type=F32)
            ds_ref[:, sl] += jnp.sum(drv * (mixed + b_ref[:, sl]), axis=0, keepdims=True)
            db_ref[:, sl] += jnp.sum(dmx, axis=0, keepdims=True)
            cur = lax.dot_general(dmx16, wv, (NT, ((), ())), preferred_element_type=F32)
            nxt = lax.dot_general((ndr_ref[:, sl] * sc).astype(BF16), wv, (NT, ((), ())),
                                  preferred_element_type=F32)
            qpad[0:ts, sl] = cur / jnp.minimum(pos + 1, win).astype(F32)
            qpad[ts:ts + HALO, sl] = jnp.where(last, 0.0, nxt / float(win))
            acc = -cur
            for j in range(win):
                acc = acc + qpad[j:j + ts, sl]
            dh[:, sl] = acc
        dx, dgain_term, _ = _rms_bwd_tile(dh[...], x_ref[...], g_ref[...], dr_ref[...])
        dx_ref[...] = dx
        dg_ref[...] += jnp.sum(dgain_term, axis=0, keepdims=True)

    row = pl.BlockSpec((ts, d), lambda i: (i, 0))
    vec = pl.BlockSpec((1, d), lambda i: (0, 0))
    return pl.pallas_call(
        body, name=name, grid=(t // ts,),
        in_specs=[row, pl.BlockSpec((N_GROUPS, cg, cg), lambda i: (0, 0, 0)), row,
                  pl.BlockSpec((HALO, d), lambda i: (jnp.minimum((i + 1) * hb, nhb - 1), 0)), row, vec, vec, vec],
        out_specs=[row, pl.BlockSpec((N_GROUPS, cg, cg), lambda i: (0, 0, 0)), vec, vec, vec],
        out_shape=[jax.ShapeDtypeStruct((t, d), F32), jax.ShapeDtypeStruct((N_GROUPS, cg, cg), F32)]
        + [jax.ShapeDtypeStruct((1, d), F32)] * 3,
        scratch_shapes=[pltpu.VMEM((ts + HALO, d), F32), pltpu.VMEM((ts, d), F32)],
        compiler_params=_params("arbitrary"),
    )(pooled, wp, dr, dr, x, gain, scale, bias)


def _ctile(n, pref):
    return max(c for c in range(LANE, min(pref, n) + 1, LANE) if n % c == 0)


FFN_COLS = 1408
FFN_ROWS = 32


def _ffn_fwd(name, up, w, b, seq):
    _, t, dff = up.shape
    f = _ctile(dff, FFN_COLS)
    k_taps = w.shape[0]
    ts = _tile(seq, 256)
    tps = seq // ts
    hb = ts // HALO16
    rc = _tile(ts, FFN_ROWS)

    def body(cur_ref, prev_ref, w_ref, b_ref, g_ref, apad):
        i = pl.program_id(1)
        first = (i % tps) == 0
        for ci, c0 in enumerate(range(0, f, LANE)):
            cols = slice(c0, c0 + LANE)
            apad[ci, 0:HALO16, :] = jnp.where(first, 0.0, prev_ref[:, cols].astype(F32))
            apad[ci, HALO16:HALO16 + ts, :] = cur_ref[0, :, cols].astype(F32)
            wv = w_ref[:, cols]
            wk = [jnp.broadcast_to(wv[k:k + 1, :], (rc, LANE)) for k in range(k_taps)]
            bias = jnp.broadcast_to(b_ref[:, cols], (rc, LANE))
            for r0 in range(0, ts, rc):
                c = bias
                for k in range(k_taps):
                    c = c + wk[k] * apad[ci, pl.ds(HALO16 - (k_taps - 1) + k + r0, rc), :]
                gate = cur_ref[1, pl.ds(r0, rc), cols].astype(F32)
                g_ref[pl.ds(r0, rc), cols] = (c * _sigmoid(c) * gate).astype(BF16)

    return pl.pallas_call(
        body, name=name, grid=(dff // f, t // ts),
        in_specs=[pl.BlockSpec((2, ts, f), lambda j, i: (0, i, j)),
                  pl.BlockSpec((None, HALO16, f), lambda j, i: (0, jnp.maximum(i * hb - 1, 0), j)),
                  pl.BlockSpec((k_taps, f), lambda j, i: (0, j)),
                  pl.BlockSpec((1, f), lambda j, i: (0, j))],
        out_specs=pl.BlockSpec((ts, f), lambda j, i: (i, j)),
        out_shape=jax.ShapeDtypeStruct((t, dff), BF16),
        scratch_shapes=[pltpu.VMEM((f // LANE, HALO16 + ts, LANE), F32)],
        compiler_params=_params("parallel", "parallel"),
    )(up, up, w, b)


def _ffn_bwd(name, up, dg, w, b, seq):
    _, t, dff = up.shape
    f = _ctile(dff, FFN_COLS)
    k_taps = w.shape[0]
    ts = _tile(seq, 256)
    tps = seq // ts
    hb = ts // HALO16
    nhb = t // HALO16
    ext = ts + HALO16
    rc = _tile(ts, FFN_ROWS)

    def body(cur_ref, prev_ref, next_ref, dg_ref, ndg_ref, w_ref, b_ref, dup_ref, dw_ref, db_ref, apad, dcpad):
        i = pl.program_id(1)
        first = (i % tps) == 0
        last = (i % tps) == tps - 1

        @pl.when(i == 0)
        def _():
            dw_ref[...] = jnp.zeros_like(dw_ref)
            db_ref[...] = jnp.zeros_like(db_ref)

        for ci, c0 in enumerate(range(0, f, LANE)):
            cols = slice(c0, c0 + LANE)
            apad[ci, 0:HALO16, :] = jnp.where(first, 0.0, prev_ref[:, cols].astype(F32))
            apad[ci, HALO16:HALO16 + ts, :] = cur_ref[0, :, cols].astype(F32)
            apad[ci, HALO16 + ts:HALO16 + ext, :] = next_ref[0, :, cols].astype(F32)
            wv = w_ref[:, cols]
            wk = [jnp.broadcast_to(wv[k:k + 1, :], (rc, LANE)) for k in range(k_taps)]
            bias = jnp.broadcast_to(b_ref[:, cols], (rc, LANE))

            def conv_grad(r0, n, gate, dgv):
                c = bias[0:n]
                for k in range(k_taps):
                    c = c + wk[k][0:n] * apad[ci, pl.ds(HALO16 - (k_taps - 1) + k + r0, n), :]
                sig = _sigmoid(c)
                return dgv * gate * sig * (1.0 + c * (1.0 - sig)), c * sig

            for r0 in range(0, ts, rc):
                dgv = dg_ref[pl.ds(r0, rc), cols].astype(F32)
                dc, silu = conv_grad(r0, rc, cur_ref[1, pl.ds(r0, rc), cols].astype(F32), dgv)
                dcpad[ci, pl.ds(r0, rc), :] = dc
                dup_ref[1, pl.ds(r0, rc), cols] = (dgv * silu).astype(BF16)
            dgv = jnp.where(last, 0.0, ndg_ref[:, cols].astype(F32))
            dc, _ = conv_grad(ts, HALO16, next_ref[1, :, cols].astype(F32), dgv)
            dcpad[ci, ts:ext, :] = dc

            dw_acc = [jnp.zeros((rc, LANE), F32) for _ in range(k_taps)]
            db_acc = jnp.zeros((rc, LANE), F32)
            for r0 in range(0, ts, rc):
                dact = jnp.zeros((rc, LANE), F32)
                for k in range(k_taps):
                    dact = dact + wk[k] * dcpad[ci, pl.ds(r0 + (k_taps - 1) - k, rc), :]
                dup_ref[0, pl.ds(r0, rc), cols] = dact.astype(BF16)
                dc = dcpad[ci, pl.ds(r0, rc), :]
                for k in range(k_taps):
                    dw_acc[k] = dw_acc[k] + dc * apad[ci, pl.ds(HALO16 - (k_taps - 1) + k + r0, rc), :]
                db_acc = db_acc + dc
            for k in range(k_taps):
                dw_ref[k:k + 1, cols] += jnp.sum(dw_acc[k], axis=0, keepdims=True)
            db_ref[:, cols] += jnp.sum(db_acc, axis=0, keepdims=True)

    return pl.pallas_call(
        body, name=name, grid=(dff // f, t // ts),
        in_specs=[pl.BlockSpec((2, ts, f), lambda j, i: (0, i, j)),
                  pl.BlockSpec((None, HALO16, f), lambda j, i: (0, jnp.maximum(i * hb - 1, 0), j)),
                  pl.BlockSpec((2, HALO16, f), lambda j, i: (0, jnp.minimum((i + 1) * hb, nhb - 1), j)),
                  pl.BlockSpec((ts, f), lambda j, i: (i, j)),
                  pl.BlockSpec((HALO16, f), lambda j, i: (jnp.minimum((i + 1) * hb, nhb - 1), j)),
                  pl.BlockSpec((k_taps, f), lambda j, i: (0, j)),
                  pl.BlockSpec((1, f), lambda j, i: (0, j))],
        out_specs=[pl.BlockSpec((2, ts, f), lambda j, i: (0, i, j)),
                   pl.BlockSpec((k_taps, f), lambda j, i: (0, j)),
                   pl.BlockSpec((1, f), lambda j, i: (0, j))],
        out_shape=[jax.ShapeDtypeStruct((2, t, dff), BF16), jax.ShapeDtypeStruct((k_taps, dff), F32),
                   jax.ShapeDtypeStruct((1, dff), F32)],
        scratch_shapes=[pltpu.VMEM((f // LANE, HALO16 + ext, LANE), F32), pltpu.VMEM((f // LANE, ext, LANE), F32)],
        compiler_params=_params("parallel", "arbitrary"),
    )(up, up, up, dg, dg, w, b)


def _sum_rows(name, g):
    ns, r, c = g.shape
    tr = _tile(r, 256)

    def body(g_ref, o_ref):
        acc = g_ref[0]
        for dev in range(1, ns):
            acc = acc + g_ref[dev]
        o_ref[...] = acc

    return pl.pallas_call(
        body, name=name, grid=(r // tr,),
        in_specs=[pl.BlockSpec((ns, tr, c), lambda i: (0, i, 0))],
        out_specs=pl.BlockSpec((tr, c), lambda i: (i, 0)),
        out_shape=jax.ShapeDtypeStruct((r, c), F32), compiler_params=_params("parallel"),
    )(g)


def _adamw(name, gsrc, w, m, v, layer=0, prev=None):
    ns, r, c = gsrc.shape
    nl = w.shape[0]
    tr = _tile(r, 256)
    prev = () if prev is None else tuple(prev)

    def body(g_ref, w_ref, m_ref, v_ref, *rest):
        go_ref, do_ref, mo_ref, vo_ref = rest[len(prev):]
        g = g_ref[0].astype(F32)
        for dev in range(1, ns):
            g = g + g_ref[dev].astype(F32)
        m_new = ADAM_B1 * m_ref[...] + (1.0 - ADAM_B1) * g
        v_new = ADAM_B2 * v_ref[...] + (1.0 - ADAM_B2) * (g * g)
        m_hat = m_new / (1.0 - ADAM_B1 ** ADAM_STEP)
        v_hat = v_new / (1.0 - ADAM_B2 ** ADAM_STEP)
        go_ref[...] = g
        do_ref[...] = -ADAM_LR * (m_hat / (jnp.sqrt(v_hat) + ADAM_EPS) + ADAM_WD * w_ref[...])
        mo_ref[...] = m_new
        vo_ref[...] = v_new

    row = pl.BlockSpec((None, tr, c), lambda i: (layer, i, 0))
    return pl.pallas_call(
        body, name=name, grid=(r // tr,),
        in_specs=[pl.BlockSpec((ns, tr, c), lambda i: (0, i, 0)), row, row, row] + [ANY] * len(prev),
        out_specs=[row] * 4, out_shape=[jax.ShapeDtypeStruct((nl, r, c), F32)] * 4,
        input_output_aliases={4 + i: i for i in range(len(prev))},
        compiler_params=_params("parallel"),
    )(gsrc, w, m, v, *prev)


def _ffn_forward(tag, r_in, h, get_wu, get_wd, wdw, bdw, seq, loss=None):
    t, d = r_in.shape
    tm = _tile(t, 512)
    wu = get_wu(h)
    dff = wu.shape[0] // 2
    tu = _tile(t, 1024)
    up = _mm(f"{tag}_up", h, wu, grid=(2, t // tu, 1),
             a_spec=pl.BlockSpec((tu, d), lambda j, i, k: (i, 0)),
             b_spec=pl.BlockSpec((dff, d), lambda j, i, k: (j, 0)),
             out_spec=pl.BlockSpec((None, tu, dff), lambda j, i, k: (j, i, 0)),
             out_shape=jax.ShapeDtypeStruct((2, t, dff), BF16), dims=NT, acc_shape=(tu, dff))
    wd = get_wd(up)
    g = _ffn_fwd(f"{tag}_act", up, wdw, bdw, seq)
    row = pl.BlockSpec((tm, d), lambda i, j, k: (i, 0))
    vec = pl.BlockSpec((1, d), lambda i, j, k: (0, 0))
    common = dict(grid=(t // tm, 1, 1), a_spec=pl.BlockSpec((tm, dff), lambda i, j, k: (i, 0)),
                  b_spec=pl.BlockSpec((dff, d), lambda i, j, k: (0, 0)), dims=NN, acc_shape=(tm, d))
    if loss is None:
        out = _mm(f"{tag}_down", g, wd, out_spec=row, out_shape=jax.ShapeDtypeStruct((t, d), F32),
                  extras=(r_in,), extra_specs=(row,), epilogue=lambda acc, ex, rows: ((ex[0][rows, :] + acc,), ()),
                  **common)
    else:
        def head(acc, ex, rows):
            dx, part, dgain = _loss_tile(ex[0][rows, :] + acc, ex[1][rows, :], ex[2][...])
            return (dx,), (part, dgain)

        out = _mm(f"{tag}_down", g, wd, out_spec=[row, pl.BlockSpec((1, 1), lambda i, j, k: (0, 0)), vec],
                  out_shape=[jax.ShapeDtypeStruct((t, d), F32), jax.ShapeDtypeStruct((1, 1), F32),
                             jax.ShapeDtypeStruct((1, d), F32)],
                  extras=(r_in, *loss), extra_specs=(row, row, vec), epilogue=head, n_sums=2, **common)
    return out, (r_in, h, up, g, wu, wd)


def _ffn_backward(tag, dr, saved, gain, wdw, bdw, seq, token=None):
    r_in, h, up, g, wu, wd = saved
    t, d = r_in.shape
    dff = wd.shape[0]
    tm = _tile(t, 512)
    tk = _tile(t, 2048)
    tku = _tile(t, 4096)
    cw = _ctile(dff, 1408)
    nc = dff // cw
    once = dict(pipeline_mode=pl.Buffered(1)) if tku == t else {}
    dg = _mm(f"{tag}_dg", dr, wd, grid=(t // tm, 1, 1),
             a_spec=pl.BlockSpec((tm, d), lambda i, j, k: (i, 0)),
             b_spec=pl.BlockSpec((dff, d), lambda i, j, k: (0, 0)),
             out_spec=pl.BlockSpec((tm, dff), lambda i, j, k: (i, 0)),
             out_shape=jax.ShapeDtypeStruct((t, dff), BF16), dims=NT, acc_shape=(tm, dff), token=token)
    dwd = _mm(f"{tag}_dwd", g, dr, grid=(dff // cw, 1, t // tk),
              a_spec=pl.BlockSpec((tk, cw), lambda i, j, k: (k, i)),
              b_spec=pl.BlockSpec((tk, d), lambda i, j, k: (k, 0)),
              out_spec=pl.BlockSpec((cw, d), lambda i, j, k: (i, 0)),
              out_shape=jax.ShapeDtypeStruct((dff, d), BF16), dims=TN, acc_shape=(cw, d))
    dup, dwdw, dbdw = _ffn_bwd(f"{tag}_dact", up, dg, wdw, bdw, seq)
    row = pl.BlockSpec((tm, d), lambda i, j, k: (i, 0))
    vec = pl.BlockSpec((1, d), lambda i, j, k: (0, 0))

    def norm_backward(acc, ex, rows):
        dx, dgain, colsum = _rms_bwd_tile(acc, ex[0][rows, :], ex[1][...], ex[2][rows, :])
        return (dx,), (dgain, colsum)

    dr_in, dgain, colsum = _mm(
        f"{tag}_dh", dup, wu, grid=(t // tm, 1, 1),
        a_spec=pl.BlockSpec((2, tm, dff), lambda i, j, k: (0, i, 0)),
        b_spec=pl.BlockSpec((2 * dff, d), lambda i, j, k: (0, 0), pipeline_mode=pl.Buffered(1)),
        out_spec=[row, vec, vec],
        out_shape=[jax.ShapeDtypeStruct((t, d), F32)] + [jax.ShapeDtypeStruct((1, d), F32)] * 2,
        dims=NN, acc_shape=(tm, d), extras=(r_in, gain, dr), extra_specs=(row, vec, row),
        epilogue=norm_backward, n_sums=2, parts=2)
    dwu = _mm(f"{tag}_dwu", dup, h, grid=(2 * nc, 1, t // tku),
              a_spec=pl.BlockSpec((None, tku, cw), lambda i, j, k: (i // nc, k, i % nc)),
              b_spec=pl.BlockSpec((tku, d), lambda i, j, k: (k, 0), **once),
              out_spec=pl.BlockSpec((cw, d), lambda i, j, k: (i, 0)),
              out_shape=jax.ShapeDtypeStruct((2 * dff, d), BF16), dims=TN, acc_shape=(cw, d))
    return dr_in, dgain, dwu, dwd, dwdw, dbdw, colsum


def _pad_to(vec, n):
    return jnp.pad(vec, (0, n - vec.shape[0]))


def _pack(parts, width):
    flat = jnp.concatenate([p.reshape(-1).astype(F32) for p in parts])
    n = -(-flat.shape[0] // (8 * width)) * (8 * width)
    return _pad_to(flat, n).reshape(n // width, width)


def _unpack(mat, shapes):
    flat = mat.reshape(-1)
    out, off = [], 0
    for s in shapes:
        n = 1
        for dim in s:
            n *= dim
        out.append(flat[off:off + n].reshape(s))
        off += n
    return out


def kernel(x, norm_mix, norm_ffn, conv_w_pw1, conv_b_pw1, conv_w_dw, conv_b_dw, conv_ln_g, conv_ln_b, conv_w_pw2, conv_b_pw2, pool_w, pool_b, pool_scale, ffn_w_up, ffn_w_dw, ffn_b_dw, ffn_w_down, final_norm, loss_target, m_norm_mix, m_norm_ffn, m_conv_w_pw1, m_conv_b_pw1, m_conv_w_dw, m_conv_b_dw, m_conv_ln_g, m_conv_ln_b, m_conv_w_pw2, m_conv_b_pw2, m_pool_w, m_pool_b, m_pool_scale, m_ffn_w_up, m_ffn_w_dw, m_ffn_b_dw, m_ffn_w_down, m_final_norm, v_norm_mix, v_norm_ffn, v_conv_w_pw1, v_conv_b_pw1, v_conv_w_dw, v_conv_b_dw, v_conv_ln_g, v_conv_ln_b, v_conv_w_pw2, v_conv_b_pw2, v_pool_w, v_pool_b, v_pool_scale, v_ffn_w_up, v_ffn_w_dw, v_ffn_b_dw, v_ffn_w_down, v_final_norm):
    bsz, seq, d = x.shape
    t = bsz * seq
    k_taps = conv_w_dw.shape[1]
    cs1 = conv_w_pw1.shape[2]
    dsh = d // N_DEV
    cg = d // N_GROUPS
    cgs = pool_w.shape[2]
    fu = ffn_w_up.shape[2]
    fd = ffn_w_down.shape[1]
    dff = fd * N_DEV
    nb = N_DEV // 2
    kf = ffn_w_dw.shape[1]
    fsh = ffn_w_dw.shape[2]
    my = _lin(_me())
    tm = _tile(t, 512)

    x2 = x.reshape(t, d)
    tgt2 = loss_target.reshape(t, d)

    small_shapes = [(k_taps, dsh), (dsh,), (dsh,), (2, kf, fsh)]
    small_mine = _pack([conv_w_dw[0], pool_b[0], pool_scale[0], ffn_w_dw], LANE)
    big = [conv_w_pw1[0], conv_w_pw2[0], ffn_w_up[0].T, ffn_w_down[0], pool_w[0], ffn_w_up[1].T, ffn_w_down[1]]
    gather = _exchange_start("gather_start", [small_mine] + [w.astype(BF16) for w in big], [False] * 8)
    h0 = _rms_fwd("l0_rms", x2, norm_mix[0:1])
    small_all, w1 = _exchange_wait("gather_wait_w1", gather, [0, 1], h0)
    parts = [_unpack(small_all[dev], small_shapes) for dev in range(N_DEV)]
    wdw = jnp.concatenate([p[0] for p in parts], axis=1)
    pool_b_full = jnp.concatenate([p[1] for p in parts]).reshape(1, d)
    pool_s_full = jnp.concatenate([p[2] for p in parts]).reshape(1, d)
    fwdw = jnp.concatenate([p[3] for p in parts], axis=2)
    fbdw = ffn_b_dw.reshape(2, 1, dff)

    def columns(w):
        return w.transpose(1, 0, 2).reshape(w.shape[1], N_DEV * w.shape[2])

    def column_shards(w):
        return w.reshape(w.shape[0], N_DEV, w.shape[1] // N_DEV).transpose(1, 0, 2)

    w1 = columns(w1)
    a = _mm("l0_pw1", h0, w1, grid=(2, t // tm, 1),
            a_spec=pl.BlockSpec((tm, d), lambda j, i, k: (i, 0)),
            b_spec=pl.BlockSpec((d, d), lambda j, i, k: (0, j)),
            out_spec=pl.BlockSpec((None, tm, d), lambda j, i, k: (j, i, 0)),
            out_shape=jax.ShapeDtypeStruct((2, t, d), BF16), dims=NN, acc_shape=(tm, d),
            extras=(conv_b_pw1,), extra_specs=(pl.BlockSpec((1, d), lambda j, i, k: (0, j)),),
            epilogue=lambda acc, ex, rows: ((acc + ex[0][...],), ()))
    (w2,) = _exchange_wait("gather_wait_w2", gather, [2], a)
    w2 = w2.reshape(d, d)
    v = _conv_fwd("l0_conv", a, wdw, conv_b_dw, seq)
    row = pl.BlockSpec((tm, d), lambda i, j, k: (i, 0))
    vec = pl.BlockSpec((1, d), lambda i, j, k: (0, 0))
    square = pl.BlockSpec((d, d), lambda i, j, k: (0, 0))

    def ln_silu(v_blk, ex):
        s_blk = _ln_silu_tile(v_blk, ex[0][...], ex[1][...]).astype(BF16)
        return s_blk, s_blk

    def residual_and_norm(acc, ex, rows):
        r_blk = ex[3][rows, :] + (acc + ex[2][...])
        return (r_blk, _rms(r_blk, ex[4][...])), ()

    r1, h1, s = _mm("l0_pw2", v, w2, grid=(t // tm, 1, 1), a_spec=row, b_spec=square, out_spec=[row, row, row],
                    out_shape=[jax.ShapeDtypeStruct((t, d), F32), jax.ShapeDtypeStruct((t, d), BF16),
                               jax.ShapeDtypeStruct((t, d), BF16)],
                    dims=NN, acc_shape=(tm, d), extras=(conv_ln_g, conv_ln_b, conv_b_pw2, x2, norm_ffn[0:1]),
                    extra_specs=(vec, vec, vec, row, vec), prologue=ln_silu, epilogue=residual_and_norm)

    def up_getter(name, idx):
        return lambda after: _exchange_wait(name, gather, [idx], after)[0].reshape(2 * dff, d)

    def down_getter(name, idx):
        return lambda after: _exchange_wait(name, gather, [idx], after)[0].reshape(dff, d)

    r2, ffn0_saved = _ffn_forward("f0", r1, h1, up_getter("gather_wait_wu0", 3),
                                  down_getter("gather_wait_wd0", 4), fwdw[0], fbdw[0], seq)
    (wp,) = _exchange_wait("gather_wait_wp", gather, [5], r2)
    wp = wp.transpose(1, 0, 2, 3).reshape(N_GROUPS, cg, cg)
    pooled, r3, h3 = _pool_mix_fwd("l1_mix", r2, norm_mix[1:2], wp, pool_s_full, pool_b_full, norm_ffn[1:2], seq)
    (dr4, loss_part, dfinal), ffn1_saved = _ffn_forward(
        "f1", r3, h3, up_getter("gather_wait_wu1", 6), down_getter("gather_wait_wd1", 7), fwdw[1], fbdw[1], seq,
        loss=(tgt2, final_norm.reshape(1, d)))

    dr3, dnf1, dwu1, dwd1, dfw1, dfb1, _ = _ffn_backward("f1", dr4, ffn1_saved, norm_ffn[1:2], fwdw[1], fbdw[1], seq)
    scatter_a = _exchange_start("scatter_f1_start", [dwu1.reshape(N_DEV, fu, d), dwd1.reshape(N_DEV, fd, d)],
                                [True, True])
    dr2, dwp, dpool_s, dpool_b, dnm1 = _pool_mix_bwd(
        "l1_dmix", pooled, wp, dr3, r2, norm_mix[1:2], pool_s_full + scatter_a["token"][0:1, 0:1], pool_b_full, seq)
    dr1, dnf0, dwu0, dwd0, dfw0, dfb0, db2 = _ffn_backward("f0", dr2, ffn0_saved, norm_ffn[0:1], fwdw[0], fbdw[0], seq)
    dwp_b = dwp.astype(BF16).reshape(N_GROUPS, N_DEV, cgs, cg).transpose(1, 0, 2, 3)
    scatter_b = _exchange_start("scatter_f0_start", [dwu0.reshape(N_DEV, fu, d), dwd0.reshape(N_DEV, fd, d), dwp_b],
                                [True] * 3)
    def ln_silu_backward(acc, ex, rows):
        dv_blk, dgain, dbias, colsum = _ln_silu_bwd_tile(acc, ex[0][rows, :], ex[1][...], ex[2][...])
        return (dv_blk,), (dgain, dbias, colsum)

    dv, dlg, dlb, dbdw = _mm("l0_ds", dr1, w2, grid=(t // tm, 1, 1), a_spec=row, b_spec=square,
                             out_spec=[row, vec, vec, vec],
                             out_shape=[jax.ShapeDtypeStruct((t, d), F32)] + [jax.ShapeDtypeStruct((1, d), F32)] * 3,
                             dims=NT, acc_shape=(tm, d), extras=(v, conv_ln_g, conv_ln_b), extra_specs=(row, vec, vec),
                             epilogue=ln_silu_backward, n_sums=3, token=scatter_b["token"])
    tk = _tile(t, 2048)
    dw2 = _mm("l0_dw2", s, dr1, grid=(1, 1, t // tk),
              a_spec=pl.BlockSpec((tk, d), lambda i, j, k: (k, 0)),
              b_spec=pl.BlockSpec((tk, d), lambda i, j, k: (k, 0)),
              out_spec=pl.BlockSpec((d, d), lambda i, j, k: (0, 0)),
              out_shape=jax.ShapeDtypeStruct((d, d), BF16), dims=TN, acc_shape=(d, d))
    da, dwdw, db1 = _conv_bwd("l0_dconv", a, dv, wdw, seq)
    tk1 = _tile(t, 4096)
    once = dict(pipeline_mode=pl.Buffered(1)) if tk1 == t else {}
    dw1 = _mm("l0_dw1", h0, da, grid=(1, 2, t // tk1),
              a_spec=pl.BlockSpec((tk1, d), lambda i, j, k: (k, 0), **once),
              b_spec=pl.BlockSpec((None, tk1, d), lambda i, j, k: (j, k, 0)),
              out_spec=pl.BlockSpec((d, d), lambda i, j, k: (0, j)),
              out_shape=jax.ShapeDtypeStruct((d, 2 * d), BF16), dims=TN, acc_shape=(d, d))
    scatter_c = _exchange_start("scatter_l0_start", [column_shards(dw1), dw2.reshape(N_DEV, d // N_DEV, d)],
                                [True, True])
    def norm_backward(acc, ex, rows):
        dx_blk, dgain, colsum = _rms_bwd_tile(acc, ex[0][rows, :], ex[1][...], ex[2][rows, :])
        return (dx_blk,), (dgain, colsum)

    dx, dnm0, _ = _mm("l0_dh", da, w1, grid=(t // tm, 1, 1),
                      a_spec=pl.BlockSpec((2, tm, d), lambda i, j, k: (0, i, 0)),
                      b_spec=pl.BlockSpec((d, 2 * d), lambda i, j, k: (0, 0), pipeline_mode=pl.Buffered(1)),
                      out_spec=[row, vec, vec],
                      out_shape=[jax.ShapeDtypeStruct((t, d), F32)] + [jax.ShapeDtypeStruct((1, d), F32)] * 2,
                      dims=NT, acc_shape=(tm, d), extras=(x2, norm_mix[0:1], dr1), extra_specs=(row, vec, row),
                      epilogue=norm_backward, n_sums=2, parts=2, token=scatter_c["token"])

    dffn_w = jnp.stack([dfw0, dfw1])
    dffn_b = jnp.stack([dfb0, dfb1]).reshape(2, dff)
    small_parts = [loss_part, jnp.concatenate([dnm0, dnm1]), jnp.concatenate([dnf0, dnf1]), db1, dwdw, dbdw, dlg, dlb,
                   db2, dpool_b, dpool_s, dffn_w, dffn_b, dfinal]
    small_part_shapes = [(1,), (2, d), (2, d), (1, 2 * d), (k_taps, d), (1, d), (1, d), (1, d), (1, d), (1, d), (1, d),
                         (2, kf, dff), (2, dff), (d,)]
    packed = _pack(small_parts, 8 * LANE)
    gather_small = _exchange_start("gather_small_start", [packed], [False])

    def big_update(name, recv, w, m, v, layer=0, prev=None):
        shape = w.shape
        c = recv.shape[-1]
        rows = recv.size // (N_DEV * c)
        nl = w.size // (rows * c)
        outs = _adamw(name, recv.reshape(N_DEV, rows, c), w.reshape(nl, rows, c), m.reshape(nl, rows, c),
                      v.reshape(nl, rows, c), layer, prev)
        return outs, [o.reshape(shape) for o in outs]

    wu_t = [p.transpose(0, 2, 1) for p in (ffn_w_up, m_ffn_w_up, v_ffn_w_up)]
    g_wu1, g_wd1 = _exchange_wait("scatter_f1_wait", scatter_a, [0, 1], gather_small["token"])
    raw_wu, _ = big_update("adam_wu1", g_wu1, *wu_t, 1)
    raw_wd, _ = big_update("adam_wd1", g_wd1, ffn_w_down, m_ffn_w_down, v_ffn_w_down, 1)
    g_wu0, g_wd0, g_wp = _exchange_wait("scatter_f0_wait", scatter_b, [0, 1, 2], raw_wd[0])
    _, u_wu = big_update("adam_wu0", g_wu0, *wu_t, 0, raw_wu)
    u_wu = [o.transpose(0, 2, 1) for o in u_wu]
    _, u_wd = big_update("adam_wd0", g_wd0, ffn_w_down, m_ffn_w_down, v_ffn_w_down, 0, raw_wd)
    _, u_wp = big_update("adam_wp", g_wp, pool_w, m_pool_w, v_pool_w)
    g_w1, g_w2 = _exchange_wait("scatter_l0_wait", scatter_c, [0, 1], u_wp[0])
    _, u_w1 = big_update("adam_w1", g_w1, conv_w_pw1, m_conv_w_pw1, v_conv_w_pw1)
    _, u_w2 = big_update("adam_w2", g_w2, conv_w_pw2, m_conv_w_pw2, v_conv_w_pw2)
    (all_small,) = _exchange_wait("gather_small_wait", gather_small, [0], u_w2[0])
    summed = _sum_rows("sum_small_grads", all_small)
    (loss_v, g_nm, g_nf, g_b1, g_wdw, g_bdw, g_lg, g_lb, g_b2, g_pb, g_ps, g_fw, g_fb,
     g_fin) = _unpack(summed, small_part_shapes)
    loss = loss_v[0]
    g_wdw_mine = lax.dynamic_slice_in_dim(g_wdw, my * dsh, dsh, axis=1)[None]
    g_pb_mine = lax.dynamic_slice_in_dim(g_pb, my * dsh, dsh, axis=1)
    g_ps_mine = lax.dynamic_slice_in_dim(g_ps, my * dsh, dsh, axis=1)
    g_fw_mine = lax.dynamic_slice_in_dim(g_fw, my * fsh, fsh, axis=2)

    small_g =[g_nm, g_nf, g_b1, g_wdw_mine, g_bdw, g_lg, g_lb, g_b2, g_pb_mine, g_ps_mine, g_fw_mine, g_fb, g_fin]
    small_w = [norm_mix, norm_ffn, conv_b_pw1, conv_w_dw, conv_b_dw, conv_ln_g, conv_ln_b, conv_b_pw2, pool_b,
               pool_scale, ffn_w_dw, ffn_b_dw, final_norm]
    small_m = [m_norm_mix, m_norm_ffn, m_conv_b_pw1, m_conv_w_dw, m_conv_b_dw, m_conv_ln_g, m_conv_ln_b,
               m_conv_b_pw2, m_pool_b, m_pool_scale, m_ffn_w_dw, m_ffn_b_dw, m_final_norm]
    small_v = [v_norm_mix, v_norm_ffn, v_conv_b_pw1, v_conv_w_dw, v_conv_b_dw, v_conv_ln_g, v_conv_ln_b,
               v_conv_b_pw2, v_pool_b, v_pool_scale, v_ffn_w_dw, v_ffn_b_dw, v_final_norm]
    shapes = [w.shape for w in small_w]
    outs = _adamw("adam_small", _pack(small_g, 8 * LANE)[None], _pack(small_w, 8 * LANE)[None],
                  _pack(small_m, 8 * LANE)[None], _pack(small_v, 8 * LANE)[None])
    sg, sd, sm, sv = [_unpack(o, shapes) for o in outs]

    def leaf(kind):
        (nm, nf, b1, wdw_, bdw_, lg, lb, b2, pb, ps, fw, fb, fin) = (sg, sd, sm, sv)[kind]
        return [nm, nf, u_w1[kind], b1, wdw_, bdw_, lg, lb, u_w2[kind], b2, u_wp[kind], pb, ps, u_wu[kind], fw, fb,
                u_wd[kind], fin]

    return (loss, dx.reshape(bsz, seq, d), *leaf(0), *leaf(1), *leaf(2), *leaf(3))
```

```python
import functools

import jax
import jax.numpy as jnp
from jax import lax
from jax.experimental import pallas as pl
from jax.experimental.pallas import tpu as pltpu

F32 = jnp.float32
BF16 = jnp.bfloat16
MESH = pl.DeviceIdType.MESH
HBM = pl.BlockSpec(memory_space=pltpu.HBM)

N_DEV = 8
RMS_EPS = 1e-6
LN_EPS = 1e-5
POOL_WINDOWS = (2, 4, 8, 16)
N_GROUPS = len(POOL_WINDOWS)
ADAM_LR = 0.001
ADAM_B1 = 0.9
ADAM_B2 = 0.999
ADAM_EPS = 1e-08
ADAM_WD = 0.01
ADAM_STEP = 10

LANE = 128
HALO = 32
HALO16 = 16
VMEM_LIMIT = 56 * 1024 * 1024


def _params(*sem):
    return pltpu.CompilerParams(dimension_semantics=sem if sem else None, vmem_limit_bytes=VMEM_LIMIT)


def _tile(n, pref):
    for t in range(min(pref, n), 15, -1):
        if n % t == 0 and t % 16 == 0:
            return t
    return n


def _sigmoid(z):
    return 1.0 / (1.0 + jnp.exp(-z))


def _me():
    return lax.axis_index("x"), lax.axis_index("y"), lax.axis_index("c")


def _flip(pos, m):
    x, y, c = pos
    return ((1 - x) if m & 4 else x, (1 - y) if m & 2 else y, (1 - c) if m & 1 else c)


def _lin(pos):
    return 4 * pos[0] + 2 * pos[1] + pos[2]


SEM = pl.BlockSpec(memory_space=pltpu.SEMAPHORE)
ANY = pl.BlockSpec(memory_space=pl.ANY)
EFFECT = pltpu.SideEffectType.DATAFLOW_SIDE_EFFECTING


def _exchange_copies(srcs, lands, send_sems, recv_sems, modes, which, starting):
    me = _me()
    my = _lin(me)
    out = []
    for pos, a in enumerate(which):
        src, land = srcs[pos], lands[pos]

        def block(pid, src=src, a=a):
            return src.at[pid] if modes[a] else src

        local = pltpu.make_async_copy(block(my), land.at[my], send_sems.at[a * N_DEV])
        remote = []
        for m in range(1, N_DEV):
            peer = _flip(me, m)
            pid = _lin(peer)
            sems = dict(send_sem=send_sems.at[a * N_DEV + m], recv_sem=recv_sems.at[a * N_DEV + m],
                        device_id=peer, device_id_type=MESH)
            if starting:
                remote.append(pltpu.make_async_remote_copy(src_ref=block(pid), dst_ref=land.at[my], **sems))
            else:
                remote.append((pltpu.make_async_remote_copy(src_ref=block(pid), dst_ref=land.at[my], **sems),
                               pltpu.make_async_remote_copy(src_ref=block(pid), dst_ref=land.at[pid], **sems)))
        out.append((local, remote))
    return out


def _exchange_start(name, arrs, modes):
    n = len(arrs)
    blocks = [a.shape[1:] if md else a.shape for a, md in zip(arrs, modes)]

    def body(*refs):
        srcs, lands = refs[:n], refs[n:2 * n]
        send_sems, recv_sems = refs[2 * n], refs[2 * n + 1]
        token = refs[-1]
        for local, remote in _exchange_copies(srcs, lands, send_sems, recv_sems, modes, list(range(n)), True):
            local.start()
            for send in remote:
                send.start()
        token[...] = jnp.zeros_like(token)

    lands = [lax.empty((N_DEV,) + tuple(b), a.dtype) for a, b in zip(arrs, blocks)]
    outs = pl.pallas_call(
        body, name=name,
        out_shape=(pltpu.SemaphoreType.DMA((n * N_DEV,)), pltpu.SemaphoreType.DMA((n * N_DEV,)),
                   *[pltpu.HBM(a.shape, a.dtype) for a in arrs], *[pltpu.HBM(l.shape, l.dtype) for l in lands],
                   jax.ShapeDtypeStruct((8, LANE), F32)),
        in_specs=[HBM] * (2 * n),
        out_specs=(SEM, SEM, *[HBM] * (2 * n), pl.BlockSpec(memory_space=pltpu.VMEM)),
        input_output_aliases={i: 2 + i for i in range(2 * n)},
        compiler_params=pltpu.CompilerParams(has_side_effects=EFFECT),
    )(*[pltpu.with_memory_space_constraint(a, pltpu.HBM) for a in arrs],
      *[pltpu.with_memory_space_constraint(l, pltpu.HBM) for l in lands])
    return dict(send=outs[0], recv=outs[1], srcs=list(outs[2:2 + n]), lands=list(outs[2 + n:2 + 2 * n]),
                modes=modes, token=outs[-1])


def _exchange_wait(name, handle, which, after, sources=None, sends=True):
    k = len(which)
    modes = handle["modes"]

    def body(*refs):
        srcs, lands = refs[:k], refs[k:2 * k]
        send_sems, recv_sems = refs[2 * k], refs[2 * k + 1]
        if sources is None:
            for local, remote in _exchange_copies(srcs, lands, send_sems, recv_sems, modes, which, False):
                local.wait()
                for send, arrival in remote:
                    send.wait_send()
                    arrival.wait_recv()
            return
        me = _me()
        for pos, a in enumerate(which):
            src, land = srcs[pos], lands[pos]
            for s in sources:
                m = (4 * jnp.abs(me[0] - ((s >> 2) & 1)) + 2 * jnp.abs(me[1] - ((s >> 1) & 1))
                     + jnp.abs(me[2] - (s & 1)))

                @pl.when(m == 0)
                def _():
                    pltpu.make_async_copy(src, land.at[s], send_sems.at[a * N_DEV]).wait()

                @pl.when(m != 0)
                def _():
                    pltpu.make_async_remote_copy(
                        src_ref=src, dst_ref=land.at[s], send_sem=send_sems.at[a * N_DEV + m],
                        recv_sem=recv_sems.at[a * N_DEV + m], device_id=me, device_id_type=MESH).wait_recv()
        if sends:
            for _, remote in _exchange_copies(srcs, lands, send_sems, recv_sems, modes, which, True):
                for send in remote:
                    send.wait_send()

    srcs = [handle["srcs"][a] for a in which]
    lands = [handle["lands"][a] for a in which]
    outs = pl.pallas_call(
        body, name=name,
        out_shape=tuple(pltpu.HBM(x.shape, x.dtype) for x in srcs + lands),
        in_specs=[HBM] * (2 * k) + [SEM, SEM, ANY], out_specs=tuple([HBM] * (2 * k)),
        input_output_aliases={i: i for i in range(2 * k)},
        compiler_params=pltpu.CompilerParams(has_side_effects=EFFECT),
    )(*srcs, *lands, handle["send"], handle["recv"], after)
    for pos, a in enumerate(which):
        handle["srcs"][a], handle["lands"][a] = outs[pos], outs[k + pos]
    return list(outs[k:])


def _mm(name, a, b, *, grid, a_spec, b_spec, out_spec, out_shape, dims, acc_shape, extras=(), extra_specs=(),
        epilogue=None, token=None, prologue=None, n_sums=0, parts=1, carry=None):
    nk = grid[2]
    ne = len(extras)
    deps = () if token is None else (token,)
    dep_specs = [pl.BlockSpec((8, LANE), lambda i, j, k: (0, 0))] * len(deps)
    if carry is not None:
        deps, dep_specs = deps + (carry,), dep_specs + [ANY]
    n_out = len(out_shape) if isinstance(out_shape, (list, tuple)) else 1
    n_tiles = n_out - n_sums - (1 if prologue is not None else 0)

    def body(a_ref, b_ref, *rest):
        ex, o_refs, acc_ref = rest[:ne], rest[ne + len(deps):ne + len(deps) + n_out], rest[ne + len(deps) + n_out]
        k = pl.program_id(2)
        if parts == 1:
            a_blk, saved = a_ref[...], None
            if prologue is not None:
                a_blk, saved = prologue(a_blk, ex)
                o_refs[n_tiles][...] = saved
            part = lax.dot_general(a_blk.astype(BF16), b_ref[...].astype(BF16), (dims, ((), ())),
                                   preferred_element_type=F32)
        else:
            kb = b_ref.shape[dims[1][0]] // parts
            part = None
            for p in range(parts):
                b_blk = b_ref[p * kb:(p + 1) * kb, :] if dims[1][0] == 0 else b_ref[:, p * kb:(p + 1) * kb]
                term = lax.dot_general(a_ref[p].astype(BF16), b_blk.astype(BF16), (dims, ((), ())),
                                       preferred_element_type=F32)
                part = term if part is None else part + term
        sum_refs = o_refs[n_out - n_sums:]

        def add_sums(terms):
            @pl.when((pl.program_id(0) == 0) & (pl.program_id(1) == 0))
            def _():
                for o_ref in sum_refs:
                    o_ref[...] = jnp.zeros_like(o_ref)

            for o_ref, term in zip(sum_refs, terms):
                o_ref[...] += jnp.sum(term, axis=0, keepdims=True)

        def finish(r):
            tiles, terms = ((r,), ()) if epilogue is None else epilogue(r, ex, slice(None))
            for o_ref, val in zip(o_refs, tiles):
                o_ref[...] = val.astype(o_ref.dtype)
            if n_sums:
                add_sums(terms)

        if nk == 1:
            finish(part)
            return

        @pl.when(k == 0)
        def _():
            acc_ref[...] = part

        @pl.when((k > 0) & (k < nk - 1))
        def _():
            acc_ref[...] += part

        @pl.when(k == nk - 1)
        def _():
            finish(acc_ref[...] + part)

    return pl.pallas_call(
        body, name=name, grid=grid, in_specs=[a_spec, b_spec, *extra_specs, *dep_specs], out_specs=out_spec,
        out_shape=out_shape, scratch_shapes=[pltpu.VMEM(acc_shape if nk > 1 else (8, LANE), F32)],
        input_output_aliases={} if carry is None else {2 + ne + len(deps) - 1: 0},
        compiler_params=_params(*(("arbitrary",) * 3 if n_sums else ("parallel", "parallel", "arbitrary"))),
    )(a, b, *extras, *deps)


def _rms(x, gain):
    return x * lax.rsqrt(jnp.mean(x * x, axis=-1, keepdims=True) + RMS_EPS) * gain


def _rms_bwd_tile(dh, x, gain, dres):
    rstd = lax.rsqrt(jnp.mean(x * x, axis=-1, keepdims=True) + RMS_EPS)
    xhat = x * rstd
    dxhat = dh * gain
    dx = dres + rstd * (dxhat - xhat * jnp.mean(dxhat * xhat, axis=-1, keepdims=True))
    return dx, dh * xhat, dx


def _ln_silu_tile(v, g, b):
    mu = jnp.mean(v, axis=-1, keepdims=True)
    cen = v - mu
    z = cen * lax.rsqrt(jnp.mean(cen * cen, axis=-1, keepdims=True) + LN_EPS) * g + b
    return z * _sigmoid(z)


def _ln_silu_bwd_tile(ds, v, g, b):
    mu = jnp.mean(v, axis=-1, keepdims=True)
    cen = v - mu
    rstd = lax.rsqrt(jnp.mean(cen * cen, axis=-1, keepdims=True) + LN_EPS)
    y = cen * rstd
    z = y * g + b
    sig = _sigmoid(z)
    dz = ds * sig * (1.0 + z * (1.0 - sig))
    dy = dz * g
    dv = rstd * (dy - jnp.mean(dy, axis=-1, keepdims=True) - y * jnp.mean(dy * y, axis=-1, keepdims=True))
    return dv, dz * y, dz, dv


def _loss_tile(x, tgt, gain):
    d = x.shape[-1]
    rstd = lax.rsqrt(jnp.mean(x * x, axis=-1, keepdims=True) + RMS_EPS)
    xhat = x * rstd
    err = xhat * gain - tgt
    dy = err / d
    dxhat = dy * gain
    dx = rstd * (dxhat - xhat * jnp.mean(dxhat * xhat, axis=-1, keepdims=True))
    return dx, 0.5 * jnp.mean(err * err, axis=-1, keepdims=True), dy * xhat


NN = ((1,), (0,))
NT = ((1,), (1,))
TN = ((0,), (0,))


def _rms_fwd(name, x, gain):
    t, d = x.shape
    tr = _tile(t, 512)

    def body(x_ref, g_ref, h_ref):
        h_ref[...] = _rms(x_ref[...], g_ref[...]).astype(BF16)

    return pl.pallas_call(
        body, name=name, grid=(t // tr,),
        in_specs=[pl.BlockSpec((tr, d), lambda i: (i, 0)), pl.BlockSpec((1, d), lambda i: (0, 0))],
        out_specs=pl.BlockSpec((tr, d), lambda i: (i, 0)),
        out_shape=jax.ShapeDtypeStruct((t, d), BF16), compiler_params=_params("parallel"),
    )(x, gain)


def _conv_tiles(t, seq):
    ts = _tile(seq, 512)
    return ts, seq // ts, _tile(ts, 64)


def _conv_fwd(name, a, w, b, seq):
    _, t, d = a.shape
    k_taps = w.shape[0]
    ts, tps, rc = _conv_tiles(t, seq)
    hb = ts // HALO

    def body(cur_ref, prev_ref, w_ref, b_ref, v_ref, upad):
        i = pl.program_id(1)
        first = (i % tps) == 0
        pv = prev_ref[0].astype(F32)
        pg = prev_ref[1].astype(F32)
        upad[0:HALO, :] = jnp.where(first, 0.0, pv * _sigmoid(pg))
        upad[HALO:HALO + ts, :] = cur_ref[0].astype(F32) * _sigmoid(cur_ref[1].astype(F32))
        wv = w_ref[...]
        bias = jnp.broadcast_to(b_ref[...], (rc, LANE))
        for r0 in range(0, ts, rc):
            acc = bias
            for k in range(k_taps):
                acc = acc + wv[k:k + 1, :] * upad[pl.ds(HALO - (k_taps - 1) + k + r0, rc), :]
            v_ref[pl.ds(r0, rc), :] = acc

    return pl.pallas_call(
        body, name=name, grid=(d // LANE, t // ts),
        in_specs=[pl.BlockSpec((2, ts, LANE), lambda c, i: (0, i, c)),
                  pl.BlockSpec((2, HALO, LANE), lambda c, i: (0, jnp.maximum(i * hb - 1, 0), c)),
                  pl.BlockSpec((k_taps, LANE), lambda c, i: (0, c)),
                  pl.BlockSpec((1, LANE), lambda c, i: (0, c))],
        out_specs=pl.BlockSpec((ts, LANE), lambda c, i: (i, c)),
        out_shape=jax.ShapeDtypeStruct((t, d), F32),
        scratch_shapes=[pltpu.VMEM((HALO + ts, LANE), F32)],
        compiler_params=_params("parallel", "parallel"),
    )(a, a, w, b)


def _conv_bwd(name, a, dv, w, seq):
    _, t, d = a.shape
    k_taps = w.shape[0]
    ts, tps, rc = _conv_tiles(t, seq)
    hb = ts // HALO
    nhb = t // HALO

    def body(cur_ref, prev_ref, dv_ref, ndv_ref, w_ref, da_ref, dw_ref, dbp_ref, upad, dvpad, dwrows):
        i = pl.program_id(1)
        first = (i % tps) == 0
        last = (i % tps) == tps - 1
        pv = prev_ref[0].astype(F32)
        pg = prev_ref[1].astype(F32)
        upad[0:HALO, :] = jnp.where(first, 0.0, pv * _sigmoid(pg))
        upad[HALO:HALO + ts, :] = cur_ref[0].astype(F32) * _sigmoid(cur_ref[1].astype(F32))
        dvpad[0:ts, :] = dv_ref[...]
        dvpad[ts:ts + HALO, :] = jnp.where(last, 0.0, ndv_ref[...])
        wv = w_ref[...]

        @pl.when(i == 0)
        def _():
            dw_ref[...] = jnp.zeros_like(dw_ref)
            dbp_ref[...] = jnp.zeros_like(dbp_ref)

        sv = jnp.zeros((1, LANE), F32)
        sg = jnp.zeros((1, LANE), F32)
        for r0 in range(0, ts, rc):
            du = jnp.zeros((rc, LANE), F32)
            for k in range(k_taps):
                du = du + wv[k:k + 1, :] * dvpad[pl.ds(r0 + (k_taps - 1) - k, rc), :]
            av = cur_ref[0, pl.ds(r0, rc), :].astype(F32)
            sig = _sigmoid(cur_ref[1, pl.ds(r0, rc), :].astype(F32))
            dval = du * sig
            dgate = du * av * sig * (1.0 - sig)
            da_ref[0, pl.ds(r0, rc), :] = dval.astype(BF16)
            da_ref[1, pl.ds(r0, rc), :] = dgate.astype(BF16)
            sv = sv + jnp.sum(dval, axis=0, keepdims=True)
            sg = sg + jnp.sum(dgate, axis=0, keepdims=True)
        dbp_ref[0] += sv
        dbp_ref[1] += sg

        for k in range(k_taps):
            acc = jnp.zeros((rc, LANE), F32)
            for r0 in range(0, ts, rc):
                acc = acc + dvpad[pl.ds(r0, rc), :] * upad[pl.ds(HALO - (k_taps - 1) + k + r0, rc), :]
            dwrows[k:k + 1, :] = jnp.sum(acc, axis=0, keepdims=True)
        dw_ref[...] += dwrows[0:k_taps, :]

    return pl.pallas_call(
        body, name=name, grid=(d // LANE, t // ts),
        in_specs=[pl.BlockSpec((2, ts, LANE), lambda c, i: (0, i, c)),
                  pl.BlockSpec((2, HALO, LANE), lambda c, i: (0, jnp.maximum(i * hb - 1, 0), c)),
                  pl.BlockSpec((ts, LANE), lambda c, i: (i, c)),
                  pl.BlockSpec((HALO, LANE), lambda c, i: (jnp.minimum((i + 1) * hb, nhb - 1), c)),
                  pl.BlockSpec((k_taps, LANE), lambda c, i: (0, c))],
        out_specs=[pl.BlockSpec((2, ts, LANE), lambda c, i: (0, i, c)),
                   pl.BlockSpec((k_taps, LANE), lambda c, i: (0, c)),
                   pl.BlockSpec((2, 1, LANE), lambda c, i: (0, 0, c))],
        out_shape=[jax.ShapeDtypeStruct((2, t, d), BF16), jax.ShapeDtypeStruct((k_taps, d), F32),
                   jax.ShapeDtypeStruct((2, 1, d), F32)],
        scratch_shapes=[pltpu.VMEM((HALO + ts, LANE), F32), pltpu.VMEM((ts + HALO, LANE), F32),
                        pltpu.VMEM((HALO, LANE), F32)],
        compiler_params=_params("parallel", "arbitrary"),
    )(a, a, dv, dv, w)


def _pool_mix_fwd(name, x, gain, wp, scale, bias, next_gain, seq):
    t, d = x.shape
    ts = _tile(seq, 256)
    tps = seq // ts
    hb = ts // HALO
    cg = d // N_GROUPS

    def body(cur_ref, prev_ref, g_ref, w_ref, s_ref, b_ref, ng_ref, p_ref, r_ref, h_ref, hpad):
        i = pl.program_id(0)
        first = (i % tps) == 0
        g = g_ref[...]
        hpad[0:HALO, :] = jnp.where(first, 0.0, _rms(prev_ref[...], g))
        hpad[HALO:HALO + ts, :] = _rms(cur_ref[...], g)
        pos = (i % tps) * ts + lax.broadcasted_iota(jnp.int32, (ts, 1), 0)
        for gi, win in enumerate(POOL_WINDOWS):
            sl = slice(gi * cg, (gi + 1) * cg)
            own = hpad[HALO:HALO + ts, sl]
            acc = own
            for j in range(1, win):
                acc = acc + hpad[HALO - j:HALO - j + ts, sl]
            cnt = jnp.minimum(pos + 1, win).astype(F32)
            pooled = (acc / cnt - own).astype(BF16)
            p_ref[:, sl] = pooled
            mixed = jnp.dot(pooled, w_ref[gi], preferred_element_type=F32)
            r_ref[:, sl] = cur_ref[:, sl] + s_ref[:, sl] * (mixed + b_ref[:, sl])
        h_ref[...] = _rms(r_ref[...], ng_ref[...]).astype(BF16)

    row = pl.BlockSpec((ts, d), lambda i: (i, 0))
    vec = pl.BlockSpec((1, d), lambda i: (0, 0))
    return pl.pallas_call(
        body, name=name, grid=(t // ts,),
        in_specs=[row, pl.BlockSpec((HALO, d), lambda i: (jnp.maximum(i * hb - 1, 0), 0)), vec,
                  pl.BlockSpec((N_GROUPS, cg, cg), lambda i: (0, 0, 0)), vec, vec, vec],
        out_specs=[row, row, row],
        out_shape=[jax.ShapeDtypeStruct((t, d), BF16), jax.ShapeDtypeStruct((t, d), F32),
                   jax.ShapeDtypeStruct((t, d), BF16)],
        scratch_shapes=[pltpu.VMEM((HALO + ts, d), F32)],
        compiler_params=_params("parallel"),
    )(x, x, gain, wp, scale, bias, next_gain)


def _pool_mix_bwd(name, pooled, wp, dr, x, gain, scale, bias, seq):
    t, d = x.shape
    ts = _tile(seq, 256)
    tps = seq // ts
    hb = ts // HALO
    nhb = t // HALO
    cg = d // N_GROUPS

    def body(p_ref, w_ref, dr_ref, ndr_ref, x_ref, g_ref, s_ref, b_ref, dx_ref, dw_ref, ds_ref, db_ref, dg_ref,
             qpad, dh):
        i = pl.program_id(0)
        last = (i % tps) == tps - 1
        pos = (i % tps) * ts + lax.broadcasted_iota(jnp.int32, (ts, 1), 0)

        @pl.when(i == 0)
        def _():
            dw_ref[...] = jnp.zeros_like(dw_ref)
            ds_ref[...] = jnp.zeros_like(ds_ref)
            db_ref[...] = jnp.zeros_like(db_ref)
            dg_ref[...] = jnp.zeros_like(dg_ref)

        for gi, win in enumerate(POOL_WINDOWS):
            sl = slice(gi * cg, (gi + 1) * cg)
            wv = w_ref[gi]
            sc = s_ref[:, sl]
            drv = dr_ref[:, sl]
            dmx = drv * sc
            dmx16 = dmx.astype(BF16)
            pooled = p_ref[:, sl]
            dw_ref[gi] += lax.dot_general(pooled, dmx16, (TN, ((), ())), preferred_element_type=F32)
            mixed = jnp.dot(pooled, wv, preferred_element_type=F32)
            ds_ref[:, sl] += jnp.sum(drv * (mixed + b_ref[:, sl]), axis=0, keepdims=True)
            db_ref[:, sl] += jnp.sum(dmx, axis=0, keepdims=True)
            cur = lax.dot_general(dmx16, wv, (NT, ((), ())), preferred_element_type=F32)
            nxt = lax.dot_general((ndr_ref[:, sl] * sc).astype(BF16), wv, (NT, ((), ())),
                                  preferred_element_type=F32)
            qpad[0:ts, sl] = cur / jnp.minimum(pos + 1, win).astype(F32)
            qpad[ts:ts + HALO, sl] = jnp.where(last, 0.0, nxt / float(win))
            acc = -cur
            for j in range(win):
                acc = acc + qpad[j:j + ts, sl]
            dh[:, sl] = acc
        dx, dgain_term, _ = _rms_bwd_tile(dh[...], x_ref[...], g_ref[...], dr_ref[...])
        dx_ref[...] = dx
        dg_ref[...] += jnp.sum(dgain_term, axis=0, keepdims=True)

    row = pl.BlockSpec((ts, d), lambda i: (i, 0))
    vec = pl.BlockSpec((1, d), lambda i: (0, 0))
    return pl.pallas_call(
        body, name=name, grid=(t // ts,),
        in_specs=[row, pl.BlockSpec((N_GROUPS, cg, cg), lambda i: (0, 0, 0)), row,
                  pl.BlockSpec((HALO, d), lambda i: (jnp.minimum((i + 1) * hb, nhb - 1), 0)), row, vec, vec, vec],
        out_specs=[row, pl.BlockSpec((N_GROUPS, cg, cg), lambda i: (0, 0, 0)), vec, vec, vec],
        out_shape=[jax.ShapeDtypeStruct((t, d), F32), jax.ShapeDtypeStruct((N_GROUPS, cg, cg), F32)]
        + [jax.ShapeDtypeStruct((1, d), F32)] * 3,
        scratch_shapes=[pltpu.VMEM((ts + HALO, d), F32), pltpu.VMEM((ts, d), F32)],
        compiler_params=_params("arbitrary"),
    )(pooled, wp, dr, dr, x, gain, scale, bias)


def _ctile(n, pref):
    return max(c for c in range(LANE, min(pref, n) + 1, LANE) if n % c == 0)


FFN_COLS = 1408
FFN_ROWS = 32


def _ffn_fwd(name, up, w, b, seq):
    _, t, dff = up.shape
    f = _ctile(dff, FFN_COLS)
    k_taps = w.shape[0]
    ts = _tile(seq, 256)
    tps = seq // ts
    hb = ts // HALO16
    rc = _tile(ts, FFN_ROWS)

    def body(cur_ref, prev_ref, w_ref, b_ref, g_ref, apad):
        i = pl.program_id(1)
        first = (i % tps) == 0
        for ci, c0 in enumerate(range(0, f, LANE)):
            cols = slice(c0, c0 + LANE)
            apad[ci, 0:HALO16, :] = jnp.where(first, 0.0, prev_ref[:, cols].astype(F32))
            apad[ci, HALO16:HALO16 + ts, :] = cur_ref[0, :, cols].astype(F32)
            wv = w_ref[:, cols]
            wk = [jnp.broadcast_to(wv[k:k + 1, :], (rc, LANE)) for k in range(k_taps)]
            bias = jnp.broadcast_to(b_ref[:, cols], (rc, LANE))
            for r0 in range(0, ts, rc):
                c = bias
                for k in range(k_taps):
                    c = c + wk[k] * apad[ci, pl.ds(HALO16 - (k_taps - 1) + k + r0, rc), :]
                gate = cur_ref[1, pl.ds(r0, rc), cols].astype(F32)
                g_ref[pl.ds(r0, rc), cols] = (c * _sigmoid(c) * gate).astype(BF16)

    return pl.pallas_call(
        body, name=name, grid=(dff // f, t // ts),
        in_specs=[pl.BlockSpec((2, ts, f), lambda j, i: (0, i, j)),
                  pl.BlockSpec((None, HALO16, f), lambda j, i: (0, jnp.maximum(i * hb - 1, 0), j)),
                  pl.BlockSpec((k_taps, f), lambda j, i: (0, j)),
                  pl.BlockSpec((1, f), lambda j, i: (0, j))],
        out_specs=pl.BlockSpec((ts, f), lambda j, i: (i, j)),
        out_shape=jax.ShapeDtypeStruct((t, dff), BF16),
        scratch_shapes=[pltpu.VMEM((f // LANE, HALO16 + ts, LANE), F32)],
        compiler_params=_params("parallel", "parallel"),
    )(up, up, w, b)


def _ffn_bwd(name, up, dg, w, b, seq):
    _, t, dff = up.shape
    f = _ctile(dff, FFN_COLS)
    k_taps = w.shape[0]
    ts = _tile(seq, 256)
    tps = seq // ts
    hb = ts // HALO16
    nhb = t // HALO16
    ext = ts + HALO16
    rc = _tile(ts, FFN_ROWS)

    def body(cur_ref, prev_ref, next_ref, dg_ref, ndg_ref, w_ref, b_ref, dup_ref, dw_ref, db_ref, apad, dcpad):
        i = pl.program_id(1)
        first = (i % tps) == 0
        last = (i % tps) == tps - 1

        @pl.when(i == 0)
        def _():
            dw_ref[...] = jnp.zeros_like(dw_ref)
            db_ref[...] = jnp.zeros_like(db_ref)

        for ci, c0 in enumerate(range(0, f, LANE)):
            cols = slice(c0, c0 + LANE)
            apad[ci, 0:HALO16, :] = jnp.where(first, 0.0, prev_ref[:, cols].astype(F32))
            apad[ci, HALO16:HALO16 + ts, :] = cur_ref[0, :, cols].astype(F32)
            apad[ci, HALO16 + ts:HALO16 + ext, :] = next_ref[0, :, cols].astype(F32)
            wv = w_ref[:, cols]
            wk = [jnp.broadcast_to(wv[k:k + 1, :], (rc, LANE)) for k in range(k_taps)]
            bias = jnp.broadcast_to(b_ref[:, cols], (rc, LANE))

            def conv_grad(r0, n, gate, dgv):
                c = bias[0:n]
                for k in range(k_taps):
                    c = c + wk[k][0:n] * apad[ci, pl.ds(HALO16 - (k_taps - 1) + k + r0, n), :]
                sig = _sigmoid(c)
                return dgv * gate * sig * (1.0 + c * (1.0 - sig)), c * sig

            for r0 in range(0, ts, rc):
                dgv = dg_ref[pl.ds(r0, rc), cols].astype(F32)
                dc, silu = conv_grad(r0, rc, cur_ref[1, pl.ds(r0, rc), cols].astype(F32), dgv)
                dcpad[ci, pl.ds(r0, rc), :] = dc
                dup_ref[1, pl.ds(r0, rc), cols] = (dgv * silu).astype(BF16)
            dgv = jnp.where(last, 0.0, ndg_ref[:, cols].astype(F32))
            dc, _ = conv_grad(ts, HALO16, next_ref[1, :, cols].astype(F32), dgv)
            dcpad[ci, ts:ext, :] = dc

            dw_acc = [jnp.zeros((rc, LANE), F32) for _ in range(k_taps)]
            db_acc = jnp.zeros((rc, LANE), F32)
            for r0 in range(0, ts, rc):
                dact = jnp.zeros((rc, LANE), F32)
                for k in range(k_taps):
                    dact = dact + wk[k] * dcpad[ci, pl.ds(r0 + (k_taps - 1) - k, rc), :]
                dup_ref[0, pl.ds(r0, rc), cols] = dact.astype(BF16)
                dc = dcpad[ci, pl.ds(r0, rc), :]
                for k in range(k_taps):
                    dw_acc[k] = dw_acc[k] + dc * apad[ci, pl.ds(HALO16 - (k_taps - 1) + k + r0, rc), :]
                db_acc = db_acc + dc
            for k in range(k_taps):
                dw_ref[k:k + 1, cols] += jnp.sum(dw_acc[k], axis=0, keepdims=True)
            db_ref[:, cols] += jnp.sum(db_acc, axis=0, keepdims=True)

    return pl.pallas_call(
        body, name=name, grid=(dff // f, t // ts),
        in_specs=[pl.BlockSpec((2, ts, f), lambda j, i: (0, i, j)),
                  pl.BlockSpec((None, HALO16, f), lambda j, i: (0, jnp.maximum(i * hb - 1, 0), j)),
                  pl.BlockSpec((2, HALO16, f), lambda j, i: (0, jnp.minimum((i + 1) * hb, nhb - 1), j)),
                  pl.BlockSpec((ts, f), lambda j, i: (i, j)),
                  pl.BlockSpec((HALO16, f), lambda j, i: (jnp.minimum((i + 1) * hb, nhb - 1), j)),
                  pl.BlockSpec((k_taps, f), lambda j, i: (0, j)),
                  pl.BlockSpec((1, f), lambda j, i: (0, j))],
        out_specs=[pl.BlockSpec((2, ts, f), lambda j, i: (0, i, j)),
                   pl.BlockSpec((k_taps, f), lambda j, i: (0, j)),
                   pl.BlockSpec((1, f), lambda j, i: (0, j))],
        out_shape=[jax.ShapeDtypeStruct((2, t, dff), BF16), jax.ShapeDtypeStruct((k_taps, dff), F32),
                   jax.ShapeDtypeStruct((1, dff), F32)],
        scratch_shapes=[pltpu.VMEM((f // LANE, HALO16 + ext, LANE), F32), pltpu.VMEM((f // LANE, ext, LANE), F32)],
        compiler_params=_params("parallel", "arbitrary"),
    )(up, up, up, dg, dg, w, b)


def _sum_rows(name, g):
    ns, r, c = g.shape
    tr = _tile(r, 256)

    def body(g_ref, o_ref):
        acc = g_ref[0]
        for dev in range(1, ns):
            acc = acc + g_ref[dev]
        o_ref[...] = acc

    return pl.pallas_call(
        body, name=name, grid=(r // tr,),
        in_specs=[pl.BlockSpec((ns, tr, c), lambda i: (0, i, 0))],
        out_specs=pl.BlockSpec((tr, c), lambda i: (i, 0)),
        out_shape=jax.ShapeDtypeStruct((r, c), F32), compiler_params=_params("parallel"),
    )(g)


def _adamw(name, gsrc, w, m, v, layer=0, prev=None):
    ns, r, c = gsrc.shape
    nl = w.shape[0]
    tr = _tile(r, 256)
    prev = () if prev is None else tuple(prev)

    def body(g_ref, w_ref, m_ref, v_ref, *rest):
        go_ref, do_ref, mo_ref, vo_ref = rest[len(prev):]
        g = g_ref[0].astype(F32)
        for dev in range(1, ns):
            g = g + g_ref[dev].astype(F32)
        m_new = ADAM_B1 * m_ref[...] + (1.0 - ADAM_B1) * g
        v_new = ADAM_B2 * v_ref[...] + (1.0 - ADAM_B2) * (g * g)
        m_hat = m_new / (1.0 - ADAM_B1 ** ADAM_STEP)
        v_hat = v_new / (1.0 - ADAM_B2 ** ADAM_STEP)
        go_ref[...] = g
        do_ref[...] = -ADAM_LR * (m_hat / (jnp.sqrt(v_hat) + ADAM_EPS) + ADAM_WD * w_ref[...])
        mo_ref[...] = m_new
        vo_ref[...] = v_new

    row = pl.BlockSpec((None, tr, c), lambda i: (layer, i, 0))
    return pl.pallas_call(
        body, name=name, grid=(r // tr,),
        in_specs=[pl.BlockSpec((ns, tr, c), lambda i: (0, i, 0)), row, row, row] + [ANY] * len(prev),
        out_specs=[row] * 4, out_shape=[jax.ShapeDtypeStruct((nl, r, c), F32)] * 4,
        input_output_aliases={4 + i: i for i in range(len(prev))},
        compiler_params=_params("parallel"),
    )(gsrc, w, m, v, *prev)


def _ffn_forward(tag, r_in, h, get_wu, get_wd, wdw, bdw, seq, loss=None):
    t, d = r_in.shape
    tm = _tile(t, 512)
    tu = _tile(t, 1024)
    wu = get_wu[0](h)
    dff = wu.shape[0] // 2
    half = dict(grid=(1, t // tu, 1), a_spec=pl.BlockSpec((tu, d), lambda j, i, k: (i, 0)),
                out_shape=jax.ShapeDtypeStruct((2, t, dff), BF16), dims=NT, acc_shape=(tu, dff))
    up = _mm(f"{tag}_up_act", h, wu, b_spec=pl.BlockSpec((dff, d), lambda j, i, k: (0, 0)),
             out_spec=pl.BlockSpec((None, tu, dff), lambda j, i, k: (0, i, 0)), **half)
    wu = get_wu[1](up)
    up = _mm(f"{tag}_up_gate", h, wu, b_spec=pl.BlockSpec((dff, d), lambda j, i, k: (1, 0)),
             out_spec=pl.BlockSpec((None, tu, dff), lambda j, i, k: (1, i, 0)), carry=up, **half)
    wd = get_wd(up)
    g = _ffn_fwd(f"{tag}_act", up, wdw, bdw, seq)
    row = pl.BlockSpec((tm, d), lambda i, j, k: (i, 0))
    vec = pl.BlockSpec((1, d), lambda i, j, k: (0, 0))
    common = dict(grid=(t // tm, 1, 1), a_spec=pl.BlockSpec((tm, dff), lambda i, j, k: (i, 0)),
                  b_spec=pl.BlockSpec((dff, d), lambda i, j, k: (0, 0)), dims=NN, acc_shape=(tm, d))
    if loss is None:
        out = _mm(f"{tag}_down", g, wd, out_spec=row, out_shape=jax.ShapeDtypeStruct((t, d), F32),
                  extras=(r_in,), extra_specs=(row,), epilogue=lambda acc, ex, rows: ((ex[0][rows, :] + acc,), ()),
                  **common)
    else:
        def head(acc, ex, rows):
            dx, part, dgain = _loss_tile(ex[0][rows, :] + acc, ex[1][rows, :], ex[2][...])
            return (dx,), (part, dgain)

        out = _mm(f"{tag}_down", g, wd, out_spec=[row, pl.BlockSpec((1, 1), lambda i, j, k: (0, 0)), vec],
                  out_shape=[jax.ShapeDtypeStruct((t, d), F32), jax.ShapeDtypeStruct((1, 1), F32),
                             jax.ShapeDtypeStruct((1, d), F32)],
                  extras=(r_in, *loss), extra_specs=(row, row, vec), epilogue=head, n_sums=2, **common)
    return out, (r_in, h, up, g, wu, wd)


def _ffn_backward(tag, dr, saved, gain, wdw, bdw, seq, token=None):
    r_in, h, up, g, wu, wd = saved
    t, d = r_in.shape
    dff = wd.shape[0]
    tm = _tile(t, 512)
    tk = _tile(t, 2048)
    tku = _tile(t, 4096)
    cw = _ctile(dff, 1408)
    nc = dff // cw
    once = dict(pipeline_mode=pl.Buffered(1)) if tku == t else {}
    dg = _mm(f"{tag}_dg", dr, wd, grid=(t // tm, 1, 1),
             a_spec=pl.BlockSpec((tm, d), lambda i, j, k: (i, 0)),
             b_spec=pl.BlockSpec((dff, d), lambda i, j, k: (0, 0)),
             out_spec=pl.BlockSpec((tm, dff), lambda i, j, k: (i, 0)),
             out_shape=jax.ShapeDtypeStruct((t, dff), BF16), dims=NT, acc_shape=(tm, dff), token=token)
    dwd = _mm(f"{tag}_dwd", g, dr, grid=(dff // cw, 1, t // tk),
              a_spec=pl.BlockSpec((tk, cw), lambda i, j, k: (k, i)),
              b_spec=pl.BlockSpec((tk, d), lambda i, j, k: (k, 0)),
              out_spec=pl.BlockSpec((cw, d), lambda i, j, k: (i, 0)),
              out_shape=jax.ShapeDtypeStruct((dff, d), BF16), dims=TN, acc_shape=(cw, d))
    dup, dwdw, dbdw = _ffn_bwd(f"{tag}_dact", up, dg, wdw, bdw, seq)
    row = pl.BlockSpec((tm, d), lambda i, j, k: (i, 0))
    vec = pl.BlockSpec((1, d), lambda i, j, k: (0, 0))

    def norm_backward(acc, ex, rows):
        dx, dgain, colsum = _rms_bwd_tile(acc, ex[0][rows, :], ex[1][...], ex[2][rows, :])
        return (dx,), (dgain, colsum)

    dr_in, dgain, colsum = _mm(
        f"{tag}_dh", dup, wu, grid=(t // tm, 1, 1),
        a_spec=pl.BlockSpec((2, tm, dff), lambda i, j, k: (0, i, 0)),
        b_spec=pl.BlockSpec((2 * dff, d), lambda i, j, k: (0, 0), pipeline_mode=pl.Buffered(1)),
        out_spec=[row, vec, vec],
        out_shape=[jax.ShapeDtypeStruct((t, d), F32)] + [jax.ShapeDtypeStruct((1, d), F32)] * 2,
        dims=NN, acc_shape=(tm, d), extras=(r_in, gain, dr), extra_specs=(row, vec, row),
        epilogue=norm_backward, n_sums=2, parts=2)
    dwu = _mm(f"{tag}_dwu", dup, h, grid=(2 * nc, 1, t // tku),
              a_spec=pl.BlockSpec((None, tku, cw), lambda i, j, k: (i // nc, k, i % nc)),
              b_spec=pl.BlockSpec((tku, d), lambda i, j, k: (k, 0), **once),
              out_spec=pl.BlockSpec((cw, d), lambda i, j, k: (i, 0)),
              out_shape=jax.ShapeDtypeStruct((2 * dff, d), BF16), dims=TN, acc_shape=(cw, d))
    return dr_in, dgain, dwu, dwd, dwdw, dbdw, colsum


def _pad_to(vec, n):
    return jnp.pad(vec, (0, n - vec.shape[0]))


def _pack(parts, width):
    flat = jnp.concatenate([p.reshape(-1).astype(F32) for p in parts])
    n = -(-flat.shape[0] // (8 * width)) * (8 * width)
    return _pad_to(flat, n).reshape(n // width, width)


def _unpack(mat, shapes):
    flat = mat.reshape(-1)
    out, off = [], 0
    for s in shapes:
        n = 1
        for dim in s:
            n *= dim
        out.append(flat[off:off + n].reshape(s))
        off += n
    return out


def kernel(x, norm_mix, norm_ffn, conv_w_pw1, conv_b_pw1, conv_w_dw, conv_b_dw, conv_ln_g, conv_ln_b, conv_w_pw2, conv_b_pw2, pool_w, pool_b, pool_scale, ffn_w_up, ffn_w_dw, ffn_b_dw, ffn_w_down, final_norm, loss_target, m_norm_mix, m_norm_ffn, m_conv_w_pw1, m_conv_b_pw1, m_conv_w_dw, m_conv_b_dw, m_conv_ln_g, m_conv_ln_b, m_conv_w_pw2, m_conv_b_pw2, m_pool_w, m_pool_b, m_pool_scale, m_ffn_w_up, m_ffn_w_dw, m_ffn_b_dw, m_ffn_w_down, m_final_norm, v_norm_mix, v_norm_ffn, v_conv_w_pw1, v_conv_b_pw1, v_conv_w_dw, v_conv_b_dw, v_conv_ln_g, v_conv_ln_b, v_conv_w_pw2, v_conv_b_pw2, v_pool_w, v_pool_b, v_pool_scale, v_ffn_w_up, v_ffn_w_dw, v_ffn_b_dw, v_ffn_w_down, v_final_norm):
    bsz, seq, d = x.shape
    t = bsz * seq
    k_taps = conv_w_dw.shape[1]
    cs1 = conv_w_pw1.shape[2]
    dsh = d // N_DEV
    cg = d // N_GROUPS
    cgs = pool_w.shape[2]
    fu = ffn_w_up.shape[2]
    fd = ffn_w_down.shape[1]
    dff = fd * N_DEV
    nb = N_DEV // 2
    kf = ffn_w_dw.shape[1]
    fsh = ffn_w_dw.shape[2]
    my = _lin(_me())
    tm = _tile(t, 512)

    x2 = x.reshape(t, d)
    tgt2 = loss_target.reshape(t, d)

    small_shapes = [(k_taps, dsh), (dsh,), (dsh,), (2, kf, fsh)]
    small_mine = _pack([conv_w_dw[0], pool_b[0], pool_scale[0], ffn_w_dw], LANE)
    big = [conv_w_pw1[0], conv_w_pw2[0], ffn_w_up[0].T, ffn_w_down[0], pool_w[0], ffn_w_up[1].T, ffn_w_down[1]]
    gather = _exchange_start("gather_start", [small_mine] + [w.astype(BF16) for w in big], [False] * 8)
    h0 = _rms_fwd("l0_rms", x2, norm_mix[0:1])
    small_all, w1 = _exchange_wait("gather_wait_w1", gather, [0, 1], h0)
    parts = [_unpack(small_all[dev], small_shapes) for dev in range(N_DEV)]
    wdw = jnp.concatenate([p[0] for p in parts], axis=1)
    pool_b_full = jnp.concatenate([p[1] for p in parts]).reshape(1, d)
    pool_s_full = jnp.concatenate([p[2] for p in parts]).reshape(1, d)
    fwdw = jnp.concatenate([p[3] for p in parts], axis=2)
    fbdw = ffn_b_dw.reshape(2, 1, dff)

    def columns(w):
        return w.transpose(1, 0, 2).reshape(w.shape[1], N_DEV * w.shape[2])

    def column_shards(w):
        return w.reshape(w.shape[0], N_DEV, w.shape[1] // N_DEV).transpose(1, 0, 2)

    w1 = columns(w1)
    a = _mm("l0_pw1", h0, w1, grid=(2, t // tm, 1),
            a_spec=pl.BlockSpec((tm, d), lambda j, i, k: (i, 0)),
            b_spec=pl.BlockSpec((d, d), lambda j, i, k: (0, j)),
            out_spec=pl.BlockSpec((None, tm, d), lambda j, i, k: (j, i, 0)),
            out_shape=jax.ShapeDtypeStruct((2, t, d), BF16), dims=NN, acc_shape=(tm, d),
            extras=(conv_b_pw1,), extra_specs=(pl.BlockSpec((1, d), lambda j, i, k: (0, j)),),
            epilogue=lambda acc, ex, rows: ((acc + ex[0][...],), ()))
    (w2,) = _exchange_wait("gather_wait_w2", gather, [2], a)
    w2 = w2.reshape(d, d)
    v = _conv_fwd("l0_conv", a, wdw, conv_b_dw, seq)
    row = pl.BlockSpec((tm, d), lambda i, j, k: (i, 0))
    vec = pl.BlockSpec((1, d), lambda i, j, k: (0, 0))
    square = pl.BlockSpec((d, d), lambda i, j, k: (0, 0))

    def ln_silu(v_blk, ex):
        s_blk = _ln_silu_tile(v_blk, ex[0][...], ex[1][...]).astype(BF16)
        return s_blk, s_blk

    def residual_and_norm(acc, ex, rows):
        r_blk = ex[3][rows, :] + (acc + ex[2][...])
        return (r_blk, _rms(r_blk, ex[4][...])), ()

    r1, h1, s = _mm("l0_pw2", v, w2, grid=(t // tm, 1, 1), a_spec=row, b_spec=square, out_spec=[row, row, row],
                    out_shape=[jax.ShapeDtypeStruct((t, d), F32), jax.ShapeDtypeStruct((t, d), BF16),
                               jax.ShapeDtypeStruct((t, d), BF16)],
                    dims=NN, acc_shape=(tm, d), extras=(conv_ln_g, conv_ln_b, conv_b_pw2, x2, norm_ffn[0:1]),
                    extra_specs=(vec, vec, vec, row, vec), prologue=ln_silu, epilogue=residual_and_norm)

    def up_getter(name, idx):
        def part(suffix, sources, sends):
            return lambda after: _exchange_wait(name + suffix, gather, [idx], after, sources=sources,
                                                sends=sends)[0].reshape(2 * dff, d)
        return part("_act", (0, 1, 2, 3), False), part("_gate", (4, 5, 6, 7), True)

    def down_getter(name, idx):
        return lambda after: _exchange_wait(name, gather, [idx], after)[0].reshape(dff, d)

    r2, ffn0_saved = _ffn_forward("f0", r1, h1, up_getter("gather_wait_wu0", 3),
                                  down_getter("gather_wait_wd0", 4), fwdw[0], fbdw[0], seq)
    (wp,) = _exchange_wait("gather_wait_wp", gather, [5], r2)
    wp = wp.transpose(1, 0, 2, 3).reshape(N_GROUPS, cg, cg)
    pooled, r3, h3 = _pool_mix_fwd("l1_mix", r2, norm_mix[1:2], wp, pool_s_full, pool_b_full, norm_ffn[1:2], seq)
    (dr4, loss_part, dfinal), ffn1_saved = _ffn_forward(
        "f1", r3, h3, up_getter("gather_wait_wu1", 6), down_getter("gather_wait_wd1", 7), fwdw[1], fbdw[1], seq,
        loss=(tgt2, final_norm.reshape(1, d)))

    dr3, dnf1, dwu1, dwd1, dfw1, dfb1, _ = _ffn_backward("f1", dr4, ffn1_saved, norm_ffn[1:2], fwdw[1], fbdw[1], seq)
    scatter_a = _exchange_start("scatter_f1_start", [dwu1.reshape(N_DEV, fu, d), dwd1.reshape(N_DEV, fd, d)],
                                [True, True])
    dr2, dwp, dpool_s, dpool_b, dnm1 = _pool_mix_bwd(
        "l1_dmix", pooled, wp, dr3, r2, norm_mix[1:2], pool_s_full + scatter_a["token"][0:1, 0:1], pool_b_full, seq)
    dr1, dnf0, dwu0, dwd0, dfw0, dfb0, db2 = _ffn_backward("f0", dr2, ffn0_saved, norm_ffn[0:1], fwdw[0], fbdw[0], seq)
    dwp_b = dwp.astype(BF16).reshape(N_GROUPS, N_DEV, cgs, cg).transpose(1, 0, 2, 3)
    scatter_b = _exchange_start("scatter_f0_start", [dwu0.reshape(N_DEV, fu, d), dwd0.reshape(N_DEV, fd, d), dwp_b],
                                [True] * 3)
    def ln_silu_backward(acc, ex, rows):
        dv_blk, dgain, dbias, colsum = _ln_silu_bwd_tile(acc, ex[0][rows, :], ex[1][...], ex[2][...])
        return (dv_blk,), (dgain, dbias, colsum)

    dv, dlg, dlb, dbdw = _mm("l0_ds", dr1, w2, grid=(t // tm, 1, 1), a_spec=row, b_spec=square,
                             out_spec=[row, vec, vec, vec],
                             out_shape=[jax.ShapeDtypeStruct((t, d), F32)] + [jax.ShapeDtypeStruct((1, d), F32)] * 3,
                             dims=NT, acc_shape=(tm, d), extras=(v, conv_ln_g, conv_ln_b), extra_specs=(row, vec, vec),
                             epilogue=ln_silu_backward, n_sums=3, token=scatter_b["token"])
    tk = _tile(t, 2048)
    dw2 = _mm("l0_dw2", s, dr1, grid=(1, 1, t // tk),
              a_spec=pl.BlockSpec((tk, d), lambda i, j, k: (k, 0)),
              b_spec=pl.BlockSpec((tk, d), lambda i, j, k: (k, 0)),
              out_spec=pl.BlockSpec((d, d), lambda i, j, k: (0, 0)),
              out_shape=jax.ShapeDtypeStruct((d, d), BF16), dims=TN, acc_shape=(d, d))
    da, dwdw, db1 = _conv_bwd("l0_dconv", a, dv, wdw, seq)
    tk1 = _tile(t, 4096)
    once = dict(pipeline_mode=pl.Buffered(1)) if tk1 == t else {}
    dw1 = _mm("l0_dw1", h0, da, grid=(1, 2, t // tk1),
              a_spec=pl.BlockSpec((tk1, d), lambda i, j, k: (k, 0), **once),
              b_spec=pl.BlockSpec((None, tk1, d), lambda i, j, k: (j, k, 0)),
              out_spec=pl.BlockSpec((d, d), lambda i, j, k: (0, j)),
              out_shape=jax.ShapeDtypeStruct((d, 2 * d), BF16), dims=TN, acc_shape=(d, d))
    scatter_c = _exchange_start("scatter_l0_start", [column_shards(dw1), dw2.reshape(N_DEV, d // N_DEV, d)],
                                [True, True])
    def norm_backward(acc, ex, rows):
        dx_blk, dgain, colsum = _rms_bwd_tile(acc, ex[0][rows, :], ex[1][...], ex[2][rows, :])
        return (dx_blk,), (dgain, colsum)

    dx, dnm0, _ = _mm("l0_dh", da, w1, grid=(t // tm, 1, 1),
                      a_spec=pl.BlockSpec((2, tm, d), lambda i, j, k: (0, i, 0)),
                      b_spec=pl.BlockSpec((d, 2 * d), lambda i, j, k: (0, 0), pipeline_mode=pl.Buffered(1)),
                      out_spec=[row, vec, vec],
                      out_shape=[jax.ShapeDtypeStruct((t, d), F32)] + [jax.ShapeDtypeStruct((1, d), F32)] * 2,
                      dims=NT, acc_shape=(tm, d), extras=(x2, norm_mix[0:1], dr1), extra_specs=(row, vec, row),
                      epilogue=norm_backward, n_sums=2, parts=2, token=scatter_c["token"])

    dffn_w = jnp.stack([dfw0, dfw1])
    dffn_b = jnp.stack([dfb0, dfb1]).reshape(2, dff)
    small_parts = [loss_part, jnp.concatenate([dnm0, dnm1]), jnp.concatenate([dnf0, dnf1]), db1, dwdw, dbdw, dlg, dlb,
                   db2, dpool_b, dpool_s, dffn_w, dffn_b, dfinal]
    small_part_shapes = [(1,), (2, d), (2, d), (1, 2 * d), (k_taps, d), (1, d), (1, d), (1, d), (1, d), (1, d), (1, d),
                         (2, kf, dff), (2, dff), (d,)]
    packed = _pack(small_parts, 8 * LANE)
    gather_small = _exchange_start("gather_small_start", [packed], [False])

    def big_update(name, recv, w, m, v, layer=0, prev=None):
        shape = w.shape
        c = recv.shape[-1]
        rows = recv.size // (N_DEV * c)
        nl = w.size // (rows * c)
        outs = _adamw(name, recv.reshape(N_DEV, rows, c), w.reshape(nl, rows, c), m.reshape(nl, rows, c),
                      v.reshape(nl, rows, c), layer, prev)
        return outs, [o.reshape(shape) for o in outs]

    wu_t = [p.transpose(0, 2, 1) for p in (ffn_w_up, m_ffn_w_up, v_ffn_w_up)]
    g_wu1, g_wd1 = _exchange_wait("scatter_f1_wait", scatter_a, [0, 1], gather_small["token"])
    raw_wu, _ = big_update("adam_wu1", g_wu1, *wu_t, 1)
    raw_wd, _ = big_update("adam_wd1", g_wd1, ffn_w_down, m_ffn_w_down, v_ffn_w_down, 1)
    g_wu0, g_wd0, g_wp = _exchange_wait("scatter_f0_wait", scatter_b, [0, 1, 2], raw_wd[0])
    _, u_wu = big_update("adam_wu0", g_wu0, *wu_t, 0, raw_wu)
    u_wu = [o.transpose(0, 2, 1) for o in u_wu]
    _, u_wd = big_update("adam_wd0", g_wd0, ffn_w_down, m_ffn_w_down, v_ffn_w_down, 0, raw_wd)
    _, u_wp = big_update("adam_wp", g_wp, pool_w, m_pool_w, v_pool_w)
    g_w1, g_w2 = _exchange_wait("scatter_l0_wait", scatter_c, [0, 1], u_wp[0])
    _, u_w1 = big_update("adam_w1", g_w1, conv_w_pw1, m_conv_w_pw1, v_conv_w_pw1)
    _, u_w2 = big_update("adam_w2", g_w2, conv_w_pw2, m_conv_w_pw2, v_conv_w_pw2)
    (all_small,) = _exchange_wait("gather_small_wait", gather_small, [0], u_w2[0])
    summed = _sum_rows("sum_small_grads", all_small)
    (loss_v, g_nm, g_nf, g_b1, g_wdw, g_bdw, g_lg, g_lb, g_b2, g_pb, g_ps, g_fw, g_fb,
     g_fin) = _unpack(summed, small_part_shapes)
    loss = loss_v[0]
    g_wdw_mine = lax.dynamic_slice_in_dim(g_wdw, my * dsh, dsh, axis=1)[None]
    g_pb_mine = lax.dynamic_slice_in_dim(g_pb, my * dsh, dsh, axis=1)
    g_ps_mine = lax.dynamic_slice_in_dim(g_ps, my * dsh, dsh, axis=1)
    g_fw_mine = lax.dynamic_slice_in_dim(g_fw, my * fsh, fsh, axis=2)

    small_g =[g_nm, g_nf, g_b1, g_wdw_mine, g_bdw, g_lg, g_lb, g_b2, g_pb_mine, g_ps_mine, g_fw_mine, g_fb, g_fin]
    small_w = [norm_mix, norm_ffn, conv_b_pw1, conv_w_dw, conv_b_dw, conv_ln_g, conv_ln_b, conv_b_pw2, pool_b,
               pool_scale, ffn_w_dw, ffn_b_dw, final_norm]
    small_m = [m_norm_mix, m_norm_ffn, m_conv_b_pw1, m_conv_w_dw, m_conv_b_dw, m_conv_ln_g, m_conv_ln_b,
               m_conv_b_pw2, m_pool_b, m_pool_scale, m_ffn_w_dw, m_ffn_b_dw, m_final_norm]
    small_v = [v_norm_mix, v_norm_ffn, v_conv_b_pw1, v_conv_w_dw, v_conv_b_dw, v_conv_ln_g, v_conv_ln_b,
               v_conv_b_pw2, v_pool_b, v_pool_scale, v_ffn_w_dw, v_ffn_b_dw, v_final_norm]
    shapes = [w.shape for w in small_w]
    outs = _adamw("adam_small", _pack(small_g, 8 * LANE)[None], _pack(small_w, 8 * LANE)[None],
                  _pack(small_m, 8 * LANE)[None], _pack(small_v, 8 * LANE)[None])
    sg, sd, sm, sv = [_unpack(o, shapes) for o in outs]

    def leaf(kind):
        (nm, nf, b1, wdw_, bdw_, lg, lb, b2, pb, ps, fw, fb, fin) = (sg, sd, sm, sv)[kind]
        return [nm, nf, u_w1[kind], b1, wdw_, bdw_, lg, lb, u_w2[kind], b2, u_wp[kind], pb, ps, u_wu[kind], fw, fb,
                u_wd[kind], fin]

    return (loss, dx.reshape(bsz, seq, d), *leaf(0), *leaf(1), *leaf(2), *leaf(3))
```

```python
import functools

import jax
import jax.numpy as jnp
from jax import lax
from jax.experimental import pallas as pl
from jax.experimental.pallas import tpu as pltpu

F32 = jnp.float32
BF16 = jnp.bfloat16
MESH = pl.DeviceIdType.MESH
HBM = pl.BlockSpec(memory_space=pltpu.HBM)

N_DEV = 8
RMS_EPS = 1e-6
LN_EPS = 1e-5
POOL_WINDOWS = (2, 4, 8, 16)
N_GROUPS = len(POOL_WINDOWS)
ADAM_LR = 0.001
ADAM_B1 = 0.9
ADAM_B2 = 0.999
ADAM_EPS = 1e-08
ADAM_WD = 0.01
ADAM_STEP = 10

LANE = 128
HALO = 32
HALO16 = 16
VMEM_LIMIT = 56 * 1024 * 1024


def _params(*sem):
    return pltpu.CompilerParams(dimension_semantics=sem if sem else None, vmem_limit_bytes=VMEM_LIMIT)


def _tile(n, pref):
    for t in range(min(pref, n), 15, -1):
        if n % t == 0 and t % 16 == 0:
            return t
    return n


def _sigmoid(z):
    return 1.0 / (1.0 + jnp.exp(-z))


def _me():
    return lax.axis_index("x"), lax.axis_index("y"), lax.axis_index("c")


def _flip(pos, m):
    x, y, c = pos
    return ((1 - x) if m & 4 else x, (1 - y) if m & 2 else y, (1 - c) if m & 1 else c)


def _lin(pos):
    return 4 * pos[0] + 2 * pos[1] + pos[2]


SEM = pl.BlockSpec(memory_space=pltpu.SEMAPHORE)
ANY = pl.BlockSpec(memory_space=pl.ANY)
EFFECT = pltpu.SideEffectType.DATAFLOW_SIDE_EFFECTING


SCATTER = "scatter"
GATHER = "gather"
GATHER2 = "gather2"
ALL_MASKS = (1, 2, 3, 4, 5, 6, 7)
SIBLING = 1
CHIPS = (2, 4, 6)


class _Copies:
    def __init__(self, a, mode, src, land, send_sems, recv_sems):
        self.a, self.mode, self.src, self.land = a, mode, src, land
        self.send_sems, self.recv_sems = send_sems, recv_sems
        self.me = _me()
        self.first = (SIBLING,) + CHIPS if mode == GATHER2 else ALL_MASKS

    def _sems(self, m, to):
        return dict(send_sem=self.send_sems.at[self.a * N_DEV + m], recv_sem=self.recv_sems.at[self.a * N_DEV + m],
                    device_id=to, device_id_type=MESH)

    def _block(self, pid):
        return self.src.at[pid] if self.mode == SCATTER else self.src

    def local(self):
        my = _lin(self.me)
        return pltpu.make_async_copy(self._block(my), self.land.at[my], self.send_sems.at[self.a * N_DEV])

    def send(self, m):
        peer = _flip(self.me, m)
        return pltpu.make_async_remote_copy(src_ref=self._block(_lin(peer)), dst_ref=self.land.at[_lin(self.me)],
                                            **self._sems(m, peer))

    def arrival(self, m):
        rows = self.land.at[_lin(_flip(self.me, m))]
        return pltpu.make_async_remote_copy(src_ref=rows, dst_ref=rows, **self._sems(m, _flip(self.me, m)))

    def forward(self, m):
        rows = self.land.at[_lin(_flip(self.me, m))]
        return pltpu.make_async_remote_copy(src_ref=rows, dst_ref=rows, **self._sems(m | 1, _flip(self.me, SIBLING)))


def _exchange_start(name, arrs, modes):
    n = len(arrs)
    blocks = [a.shape[1:] if md == SCATTER else a.shape for a, md in zip(arrs, modes)]

    def body(*refs):
        srcs, lands = refs[:n], refs[n:2 * n]
        send_sems, recv_sems = refs[2 * n], refs[2 * n + 1]
        token = refs[-1]
        for a in range(n):
            cp = _Copies(a, modes[a], srcs[a], lands[a], send_sems, recv_sems)
            cp.local().start()
            for m in cp.first:
                cp.send(m).start()
        token[...] = jnp.zeros_like(token)

    lands = [lax.empty((N_DEV,) + tuple(b), a.dtype) for a, b in zip(arrs, blocks)]
    outs = pl.pallas_call(
        body, name=name,
        out_shape=(pltpu.SemaphoreType.DMA((n * N_DEV,)), pltpu.SemaphoreType.DMA((n * N_DEV,)),
                   *[pltpu.HBM(a.shape, a.dtype) for a in arrs], *[pltpu.HBM(l.shape, l.dtype) for l in lands],
                   jax.ShapeDtypeStruct((8, LANE), F32)),
        in_specs=[HBM] * (2 * n),
        out_specs=(SEM, SEM, *[HBM] * (2 * n), pl.BlockSpec(memory_space=pltpu.VMEM)),
        input_output_aliases={i: 2 + i for i in range(2 * n)},
        compiler_params=pltpu.CompilerParams(has_side_effects=EFFECT),
    )(*[pltpu.with_memory_space_constraint(a, pltpu.HBM) for a in arrs],
      *[pltpu.with_memory_space_constraint(l, pltpu.HBM) for l in lands])
    return dict(send=outs[0], recv=outs[1], srcs=list(outs[2:2 + n]), lands=list(outs[2 + n:2 + 2 * n]),
                modes=modes, token=outs[-1])


def _exchange_forward(name, handle, which, after):
    k = len(which)

    def half(wait):
        def body(*refs):
            lands = refs[:k]
            send_sems, recv_sems = refs[k], refs[k + 1]
            token = refs[-1]
            for pos, a in enumerate(which):
                cp = _Copies(a, GATHER2, None, lands[pos], send_sems, recv_sems)
                for m in CHIPS:
                    if wait:
                        cp.arrival(m).wait_recv()
                    else:
                        cp.forward(m).start()
            token[...] = jnp.zeros_like(token)
        return body

    def call(body, call_name, lands, after):
        outs = pl.pallas_call(
            body, name=call_name,
            out_shape=(*[pltpu.HBM(x.shape, x.dtype) for x in lands], jax.ShapeDtypeStruct((8, LANE), F32)),
            in_specs=[HBM] * k + [SEM, SEM, ANY], out_specs=(*[HBM] * k, pl.BlockSpec(memory_space=pltpu.VMEM)),
            input_output_aliases={i: i for i in range(k)},
            compiler_params=pltpu.CompilerParams(has_side_effects=EFFECT),
        )(*lands, handle["send"], handle["recv"], after)
        return list(outs[:k]), outs[-1]

    lands, arrived = call(half(True), name + "_arrived", [handle["lands"][a] for a in which], after)
    lands, token = call(half(False), name, lands, arrived)
    for pos, a in enumerate(which):
        handle["lands"][a] = lands[pos]
    return token


def _exchange_wait(name, handle, which, after):
    k = len(which)
    modes = handle["modes"]

    def body(*refs):
        srcs, lands = refs[:k], refs[k:2 * k]
        send_sems, recv_sems = refs[2 * k], refs[2 * k + 1]
        for pos, a in enumerate(which):
            cp = _Copies(a, modes[a], srcs[pos], lands[pos], send_sems, recv_sems)
            cp.local().wait()
            for m in cp.first:
                cp.send(m).wait_send()
            if modes[a] == GATHER2:
                for m in CHIPS:
                    cp.forward(m).wait_send()
                arrivals = (SIBLING,) + tuple(m | 1 for m in CHIPS)
            else:
                arrivals = ALL_MASKS
            for m in arrivals:
                cp.arrival(m).wait_recv()

    srcs = [handle["srcs"][a] for a in which]
    lands = [handle["lands"][a] for a in which]
    outs = pl.pallas_call(
        body, name=name,
        out_shape=tuple(pltpu.HBM(x.shape, x.dtype) for x in srcs + lands),
        in_specs=[HBM] * (2 * k) + [SEM, SEM, ANY], out_specs=tuple([HBM] * (2 * k)),
        input_output_aliases={i: i for i in range(2 * k)},
        compiler_params=pltpu.CompilerParams(has_side_effects=EFFECT),
    )(*srcs, *lands, handle["send"], handle["recv"], after)
    for pos, a in enumerate(which):
        handle["srcs"][a], handle["lands"][a] = outs[pos], outs[k + pos]
    return list(outs[k:])


def _mm(name, a, b, *, grid, a_spec, b_spec, out_spec, out_shape, dims, acc_shape, extras=(), extra_specs=(),
        epilogue=None, token=None, prologue=None, n_sums=0, parts=1):
    nk = grid[2]
    ne = len(extras)
    deps = () if token is None else (token,)
    dep_specs = [pl.BlockSpec((8, LANE), lambda i, j, k: (0, 0))] * len(deps)
    n_out = len(out_shape) if isinstance(out_shape, (list, tuple)) else 1
    n_tiles = n_out - n_sums - (1 if prologue is not None else 0)

    def body(a_ref, b_ref, *rest):
        ex, o_refs, acc_ref = rest[:ne], rest[ne + len(deps):ne + len(deps) + n_out], rest[ne + len(deps) + n_out]
        k = pl.program_id(2)
        if parts == 1:
            a_blk, saved = a_ref[...], None
            if prologue is not None:
                a_blk, saved = prologue(a_blk, ex)
                o_refs[n_tiles][...] = saved
            part = lax.dot_general(a_blk.astype(BF16), b_ref[...].astype(BF16), (dims, ((), ())),
                                   preferred_element_type=F32)
        else:
            kb = b_ref.shape[dims[1][0]] // parts
            part = None
            for p in range(parts):
                b_blk = b_ref[p * kb:(p + 1) * kb, :] if dims[1][0] == 0 else b_ref[:, p * kb:(p + 1) * kb]
                term = lax.dot_general(a_ref[p].astype(BF16), b_blk.astype(BF16), (dims, ((), ())),
                                       preferred_element_type=F32)
                part = term if part is None else part + term
        sum_refs = o_refs[n_out - n_sums:]

        def add_sums(terms):
            @pl.when((pl.program_id(0) == 0) & (pl.program_id(1) == 0))
            def _():
                for o_ref in sum_refs:
                    o_ref[...] = jnp.zeros_like(o_ref)

            for o_ref, term in zip(sum_refs, terms):
                o_ref[...] += jnp.sum(term, axis=0, keepdims=True)

        def finish(r):
            tiles, terms = ((r,), ()) if epilogue is None else epilogue(r, ex, slice(None))
            for o_ref, val in zip(o_refs, tiles):
                o_ref[...] = val.astype(o_ref.dtype)
            if n_sums:
                add_sums(terms)

        if nk == 1:
            finish(part)
            return

        @pl.when(k == 0)
        def _():
            acc_ref[...] = part

        @pl.when((k > 0) & (k < nk - 1))
        def _():
            acc_ref[...] += part

        @pl.when(k == nk - 1)
        def _():
            finish(acc_ref[...] + part)

    return pl.pallas_call(
        body, name=name, grid=grid, in_specs=[a_spec, b_spec, *extra_specs, *dep_specs], out_specs=out_spec,
        out_shape=out_shape, scratch_shapes=[pltpu.VMEM(acc_shape if nk > 1 else (8, LANE), F32)],
        compiler_params=_params(*(("arbitrary",) * 3 if n_sums else ("parallel", "parallel", "arbitrary"))),
    )(a, b, *extras, *deps)


def _rms(x, gain):
    return x * lax.rsqrt(jnp.mean(x * x, axis=-1, keepdims=True) + RMS_EPS) * gain


def _rms_bwd_tile(dh, x, gain, dres):
    rstd = lax.rsqrt(jnp.mean(x * x, axis=-1, keepdims=True) + RMS_EPS)
    xhat = x * rstd
    dxhat = dh * gain
    dx = dres + rstd * (dxhat - xhat * jnp.mean(dxhat * xhat, axis=-1, keepdims=True))
    return dx, dh * xhat, dx


def _ln_silu_tile(v, g, b):
    mu = jnp.mean(v, axis=-1, keepdims=True)
    cen = v - mu
    z = cen * lax.rsqrt(jnp.mean(cen * cen, axis=-1, keepdims=True) + LN_EPS) * g + b
    return z * _sigmoid(z)


def _ln_silu_bwd_tile(ds, v, g, b):
    mu = jnp.mean(v, axis=-1, keepdims=True)
    cen = v - mu
    rstd = lax.rsqrt(jnp.mean(cen * cen, axis=-1, keepdims=True) + LN_EPS)
    y = cen * rstd
    z = y * g + b
    sig = _sigmoid(z)
    dz = ds * sig * (1.0 + z * (1.0 - sig))
    dy = dz * g
    dv = rstd * (dy - jnp.mean(dy, axis=-1, keepdims=True) - y * jnp.mean(dy * y, axis=-1, keepdims=True))
    return dv, dz * y, dz, dv


def _loss_tile(x, tgt, gain):
    d = x.shape[-1]
    rstd = lax.rsqrt(jnp.mean(x * x, axis=-1, keepdims=True) + RMS_EPS)
    xhat = x * rstd
    err = xhat * gain - tgt
    dy = err / d
    dxhat = dy * gain
    dx = rstd * (dxhat - xhat * jnp.mean(dxhat * xhat, axis=-1, keepdims=True))
    return dx, 0.5 * jnp.mean(err * err, axis=-1, keepdims=True), dy * xhat


NN = ((1,), (0,))
NT = ((1,), (1,))
TN = ((0,), (0,))


def _rms_fwd(name, x, gain):
    t, d = x.shape
    tr = _tile(t, 512)

    def body(x_ref, g_ref, h_ref):
        h_ref[...] = _rms(x_ref[...], g_ref[...]).astype(BF16)

    return pl.pallas_call(
        body, name=name, grid=(t // tr,),
        in_specs=[pl.BlockSpec((tr, d), lambda i: (i, 0)), pl.BlockSpec((1, d), lambda i: (0, 0))],
        out_specs=pl.BlockSpec((tr, d), lambda i: (i, 0)),
        out_shape=jax.ShapeDtypeStruct((t, d), BF16), compiler_params=_params("parallel"),
    )(x, gain)


def _conv_tiles(t, seq):
    ts = _tile(seq, 512)
    return ts, seq // ts, _tile(ts, 64)


def _conv_fwd(name, a, w, b, seq):
    _, t, d = a.shape
    k_taps = w.shape[0]
    ts, tps, rc = _conv_tiles(t, seq)
    hb = ts // HALO

    def body(cur_ref, prev_ref, w_ref, b_ref, v_ref, upad):
        i = pl.program_id(1)
        first = (i % tps) == 0
        pv = prev_ref[0].astype(F32)
        pg = prev_ref[1].astype(F32)
        upad[0:HALO, :] = jnp.where(first, 0.0, pv * _sigmoid(pg))
        upad[HALO:HALO + ts, :] = cur_ref[0].astype(F32) * _sigmoid(cur_ref[1].astype(F32))
        wv = w_ref[...]
        bias = jnp.broadcast_to(b_ref[...], (rc, LANE))
        for r0 in range(0, ts, rc):
            acc = bias
            for k in range(k_taps):
                acc = acc + wv[k:k + 1, :] * upad[pl.ds(HALO - (k_taps - 1) + k + r0, rc), :]
            v_ref[pl.ds(r0, rc), :] = acc

    return pl.pallas_call(
        body, name=name, grid=(d // LANE, t // ts),
        in_specs=[pl.BlockSpec((2, ts, LANE), lambda c, i: (0, i, c)),
                  pl.BlockSpec((2, HALO, LANE), lambda c, i: (0, jnp.maximum(i * hb - 1, 0), c)),
                  pl.BlockSpec((k_taps, LANE), lambda c, i: (0, c)),
                  pl.BlockSpec((1, LANE), lambda c, i: (0, c))],
        out_specs=pl.BlockSpec((ts, LANE), lambda c, i: (i, c)),
        out_shape=jax.ShapeDtypeStruct((t, d), F32),
        scratch_shapes=[pltpu.VMEM((HALO + ts, LANE), F32)],
        compiler_params=_params("parallel", "parallel"),
    )(a, a, w, b)


def _conv_bwd(name, a, dv, w, seq):
    _, t, d = a.shape
    k_taps = w.shape[0]
    ts, tps, rc = _conv_tiles(t, seq)
    hb = ts // HALO
    nhb = t // HALO

    def body(cur_ref, prev_ref, dv_ref, ndv_ref, w_ref, da_ref, dw_ref, dbp_ref, upad, dvpad, dwrows):
        i = pl.program_id(1)
        first = (i % tps) == 0
        last = (i % tps) == tps - 1
        pv = prev_ref[0].astype(F32)
        pg = prev_ref[1].astype(F32)
        upad[0:HALO, :] = jnp.where(first, 0.0, pv * _sigmoid(pg))
        upad[HALO:HALO + ts, :] = cur_ref[0].astype(F32) * _sigmoid(cur_ref[1].astype(F32))
        dvpad[0:ts, :] = dv_ref[...]
        dvpad[ts:ts + HALO, :] = jnp.where(last, 0.0, ndv_ref[...])
        wv = w_ref[...]

        @pl.when(i == 0)
        def _():
            dw_ref[...] = jnp.zeros_like(dw_ref)
            dbp_ref[...] = jnp.zeros_like(dbp_ref)

        sv = jnp.zeros((1, LANE), F32)
        sg = jnp.zeros((1, LANE), F32)
        for r0 in range(0, ts, rc):
            du = jnp.zeros((rc, LANE), F32)
            for k in range(k_taps):
                du = du + wv[k:k + 1, :] * dvpad[pl.ds(r0 + (k_taps - 1) - k, rc), :]
            av = cur_ref[0, pl.ds(r0, rc), :].astype(F32)
            sig = _sigmoid(cur_ref[1, pl.ds(r0, rc), :].astype(F32))
            dval = du * sig
            dgate = du * av * sig * (1.0 - sig)
            da_ref[0, pl.ds(r0, rc), :] = dval.astype(BF16)
            da_ref[1, pl.ds(r0, rc), :] = dgate.astype(BF16)
            sv = sv + jnp.sum(dval, axis=0, keepdims=True)
            sg = sg + jnp.sum(dgate, axis=0, keepdims=True)
        dbp_ref[0] += sv
        dbp_ref[1] += sg

        for k in range(k_taps):
            acc = jnp.zeros((rc, LANE), F32)
            for r0 in range(0, ts, rc):
                acc = acc + dvpad[pl.ds(r0, rc), :] * upad[pl.ds(HALO - (k_taps - 1) + k + r0, rc), :]
            dwrows[k:k + 1, :] = jnp.sum(acc, axis=0, keepdims=True)
        dw_ref[...] += dwrows[0:k_taps, :]

    return pl.pallas_call(
        body, name=name, grid=(d // LANE, t // ts),
        in_specs=[pl.BlockSpec((2, ts, LANE), lambda c, i: (0, i, c)),
                  pl.BlockSpec((2, HALO, LANE), lambda c, i: (0, jnp.maximum(i * hb - 1, 0), c)),
                  pl.BlockSpec((ts, LANE), lambda c, i: (i, c)),
                  pl.BlockSpec((HALO, LANE), lambda c, i: (jnp.minimum((i + 1) * hb, nhb - 1), c)),
                  pl.BlockSpec((k_taps, LANE), lambda c, i: (0, c))],
        out_specs=[pl.BlockSpec((2, ts, LANE), lambda c, i: (0, i, c)),
                   pl.BlockSpec((k_taps, LANE), lambda c, i: (0, c)),
                   pl.BlockSpec((2, 1, LANE), lambda c, i: (0, 0, c))],
        out_shape=[jax.ShapeDtypeStruct((2, t, d), BF16), jax.ShapeDtypeStruct((k_taps, d), F32),
                   jax.ShapeDtypeStruct((2, 1, d), F32)],
        scratch_shapes=[pltpu.VMEM((HALO + ts, LANE), F32), pltpu.VMEM((ts + HALO, LANE), F32),
                        pltpu.VMEM((HALO, LANE), F32)],
        compiler_params=_params("parallel", "arbitrary"),
    )(a, a, dv, dv, w)


def _pool_mix_fwd(name, x, gain, wp, scale, bias, next_gain, seq):
    t, d = x.shape
    ts = _tile(seq, 256)
    tps = seq // ts
    hb = ts // HALO
    cg = d // N_GROUPS

    def body(cur_ref, prev_ref, g_ref, w_ref, s_ref, b_ref, ng_ref, p_ref, r_ref, h_ref, hpad):
        i = pl.program_id(0)
        first = (i % tps) == 0
        g = g_ref[...]
        hpad[0:HALO, :] = jnp.where(first, 0.0, _rms(prev_ref[...], g))
        hpad[HALO:HALO + ts, :] = _rms(cur_ref[...], g)
        pos = (i % tps) * ts + lax.broadcasted_iota(jnp.int32, (ts, 1), 0)
        for gi, win in enumerate(POOL_WINDOWS):
            sl = slice(gi * cg, (gi + 1) * cg)
            own = hpad[HALO:HALO + ts, sl]
            acc = own
            for j in range(1, win):
                acc = acc + hpad[HALO - j:HALO - j + ts, sl]
            cnt = jnp.minimum(pos + 1, win).astype(F32)
            pooled = (acc / cnt - own).astype(BF16)
            p_ref[:, sl] = pooled
            mixed = jnp.dot(pooled, w_ref[gi], preferred_element_type=F32)
            r_ref[:, sl] = cur_ref[:, sl] + s_ref[:, sl] * (mixed + b_ref[:, sl])
        h_ref[...] = _rms(r_ref[...], ng_ref[...]).astype(BF16)

    row = pl.BlockSpec((ts, d), lambda i: (i, 0))
    vec = pl.BlockSpec((1, d), lambda i: (0, 0))
    return pl.pallas_call(
        body, name=name, grid=(t // ts,),
        in_specs=[row, pl.BlockSpec((HALO, d), lambda i: (jnp.maximum(i * hb - 1, 0), 0)), vec,
                  pl.BlockSpec((N_GROUPS, cg, cg), lambda i: (0, 0, 0)), vec, vec, vec],
        out_specs=[row, row, row],
        out_shape=[jax.ShapeDtypeStruct((t, d), BF16), jax.ShapeDtypeStruct((t, d), F32),
                   jax.ShapeDtypeStruct((t, d), BF16)],
        scratch_shapes=[pltpu.VMEM((HALO + ts, d), F32)],
        compiler_params=_params("parallel"),
    )(x, x, gain, wp, scale, bias, next_gain)


def _pool_mix_bwd(name, pooled, wp, dr, x, gain, scale, bias, seq):
    t, d = x.shape
    ts = _tile(seq, 256)
    tps = seq // ts
    hb = ts // HALO
    nhb = t // HALO
    cg = d // N_GROUPS

    def body(p_ref, w_ref, dr_ref, ndr_ref, x_ref, g_ref, s_ref, b_ref, dx_ref, dw_ref, ds_ref, db_ref, dg_ref,
             qpad, dh):
        i = pl.program_id(0)
        last = (i % tps) == tps - 1
        pos = (i % tps) * ts + lax.broadcasted_iota(jnp.int32, (ts, 1), 0)

        @pl.when(i == 0)
        def _():
            dw_ref[...] = jnp.zeros_like(dw_ref)
            ds_ref[...] = jnp.zeros_like(ds_ref)
            db_ref[...] = jnp.zeros_like(db_ref)
            dg_ref[...] = jnp.zeros_like(dg_ref)

        for gi, win in enumerate(POOL_WINDOWS):
            sl = slice(gi * cg, (gi + 1) * cg)
            wv = w_ref[gi]
            sc = s_ref[:, sl]
            drv = dr_ref[:, sl]
            dmx = drv * sc
            dmx16 = dmx.astype(BF16)
            pooled = p_ref[:, sl]
            dw_ref[gi] += lax.dot_general(pooled, dmx16, (TN, ((), ())), preferred_element_type=F32)
            mixed = jnp.dot(pooled, wv, preferred_element_type=F32)
            ds_ref[:, sl] += jnp.sum(drv * (mixed + b_ref[:, sl]), axis=0, keepdims=True)
            db_ref[:, sl] += jnp.sum(dmx, axis=0, keepdims=True)
            cur = lax.dot_general(dmx16, wv, (NT, ((), ())), preferred_element_type=F32)
            nxt = lax.dot_general((ndr_ref[:, sl] * sc).astype(BF16), wv, (NT, ((), ())),
                                  preferred_element_type=F32)
            qpad[0:ts, sl] = cur / jnp.minimum(pos + 1, win).astype(F32)
            qpad[ts:ts + HALO, sl] = jnp.where(last, 0.0, nxt / float(win))
            acc = -cur
            for j in range(win):
                acc = acc + qpad[j:j + ts, sl]
            dh[:, sl] = acc
        dx, dgain_term, _ = _rms_bwd_tile(dh[...], x_ref[...], g_ref[...], dr_ref[...])
        dx_ref[...] = dx
        dg_ref[...] += jnp.sum(dgain_term, axis=0, keepdims=True)

    row = pl.BlockSpec((ts, d), lambda i: (i, 0))
    vec = pl.BlockSpec((1, d), lambda i: (0, 0))
    return pl.pallas_call(
        body, name=name, grid=(t // ts,),
        in_specs=[row, pl.BlockSpec((N_GROUPS, cg, cg), lambda i: (0, 0, 0)), row,
                  pl.BlockSpec((HALO, d), lambda i: (jnp.minimum((i + 1) * hb, nhb - 1), 0)), row, vec, vec, vec],
        out_specs=[row, pl.BlockSpec((N_GROUPS, cg, cg), lambda i: (0, 0, 0)), vec, vec, vec],
        out_shape=[jax.ShapeDtypeStruct((t, d), F32), jax.ShapeDtypeStruct((N_GROUPS, cg, cg), F32)]
        + [jax.ShapeDtypeStruct((1, d), F32)] * 3,
        scratch_shapes=[pltpu.VMEM((ts + HALO, d), F32), pltpu.VMEM((ts, d), F32)],
        compiler_params=_params("arbitrary"),
    )(pooled, wp, dr, dr, x, gain, scale, bias)


def _ctile(n, pref):
    return max(c for c in range(LANE, min(pref, n) + 1, LANE) if n % c == 0)


FFN_COLS = 1408
FFN_ROWS = 32


def _ffn_fwd(name, up, w, b, seq):
    _, t, dff = up.shape
    f = _ctile(dff, FFN_COLS)
    k_taps = w.shape[0]
    ts = _tile(seq, 256)
    tps = seq // ts
    hb = ts // HALO16
    rc = _tile(ts, FFN_ROWS)

    def body(cur_ref, prev_ref, w_ref, b_ref, g_ref, apad):
        i = pl.program_id(1)
        first = (i % tps) == 0
        for ci, c0 in enumerate(range(0, f, LANE)):
            cols = slice(c0, c0 + LANE)
            apad[ci, 0:HALO16, :] = jnp.where(first, 0.0, prev_ref[:, cols].astype(F32))
            apad[ci, HALO16:HALO16 + ts, :] = cur_ref[0, :, cols].astype(F32)
            wv = w_ref[:, cols]
            wk = [jnp.broadcast_to(wv[k:k + 1, :], (rc, LANE)) for k in range(k_taps)]
            bias = jnp.broadcast_to(b_ref[:, cols], (rc, LANE))
            for r0 in range(0, ts, rc):
                c = bias
                for k in range(k_taps):
                    c = c + wk[k] * apad[ci, pl.ds(HALO16 - (k_taps - 1) + k + r0, rc), :]
                gate = cur_ref[1, pl.ds(r0, rc), cols].astype(F32)
                g_ref[pl.ds(r0, rc), cols] = (c * _sigmoid(c) * gate).astype(BF16)

    return pl.pallas_call(
        body, name=name, grid=(dff // f, t // ts),
        in_specs=[pl.BlockSpec((2, ts, f), lambda j, i: (0, i, j)),
                  pl.BlockSpec((None, HALO16, f), lambda j, i: (0, jnp.maximum(i * hb - 1, 0), j)),
                  pl.BlockSpec((k_taps, f), lambda j, i: (0, j)),
                  pl.BlockSpec((1, f), lambda j, i: (0, j))],
        out_specs=pl.BlockSpec((ts, f), lambda j, i: (i, j)),
        out_shape=jax.ShapeDtypeStruct((t, dff), BF16),
        scratch_shapes=[pltpu.VMEM((f // LANE, HALO16 + ts, LANE), F32)],
        compiler_params=_params("parallel", "parallel"),
    )(up, up, w, b)


def _ffn_bwd(name, up, dg, w, b, seq):
    _, t, dff = up.shape
    f = _ctile(dff, FFN_COLS)
    k_taps = w.shape[0]
    ts = _tile(seq, 256)
    tps = seq // ts
    hb = ts // HALO16
    nhb = t // HALO16
    ext = ts + HALO16
    rc = _tile(ts, FFN_ROWS)

    def body(cur_ref, prev_ref, next_ref, dg_ref, ndg_ref, w_ref, b_ref, dup_ref, dw_ref, db_ref, apad, dcpad):
        i = pl.program_id(1)
        first = (i % tps) == 0
        last = (i % tps) == tps - 1

        @pl.when(i == 0)
        def _():
            dw_ref[...] = jnp.zeros_like(dw_ref)
            db_ref[...] = jnp.zeros_like(db_ref)

        for ci, c0 in enumerate(range(0, f, LANE)):
            cols = slice(c0, c0 + LANE)
            apad[ci, 0:HALO16, :] = jnp.where(first, 0.0, prev_ref[:, cols].astype(F32))
            apad[ci, HALO16:HALO16 + ts, :] = cur_ref[0, :, cols].astype(F32)
            apad[ci, HALO16 + ts:HALO16 + ext, :] = next_ref[0, :, cols].astype(F32)
            wv = w_ref[:, cols]
            wk = [jnp.broadcast_to(wv[k:k + 1, :], (rc, LANE)) for k in range(k_taps)]
            bias = jnp.broadcast_to(b_ref[:, cols], (rc, LANE))

            def conv_grad(r0, n, gate, dgv):
                c = bias[0:n]
                for k in range(k_taps):
                    c = c + wk[k][0:n] * apad[ci, pl.ds(HALO16 - (k_taps - 1) + k + r0, n), :]
                sig = _sigmoid(c)
                return dgv * gate * sig * (1.0 + c * (1.0 - sig)), c * sig

            for r0 in range(0, ts, rc):
                dgv = dg_ref[pl.ds(r0, rc), cols].astype(F32)
                dc, silu = conv_grad(r0, rc, cur_ref[1, pl.ds(r0, rc), cols].astype(F32), dgv)
                dcpad[ci, pl.ds(r0, rc), :] = dc
                dup_ref[1, pl.ds(r0, rc), cols] = (dgv * silu).astype(BF16)
            dgv = jnp.where(last, 0.0, ndg_ref[:, cols].astype(F32))
            dc, _ = conv_grad(ts, HALO16, next_ref[1, :, cols].astype(F32), dgv)
            dcpad[ci, ts:ext, :] = dc

            dw_acc = [jnp.zeros((rc, LANE), F32) for _ in range(k_taps)]
            db_acc = jnp.zeros((rc, LANE), F32)
            for r0 in range(0, ts, rc):
                dact = jnp.zeros((rc, LANE), F32)
                for k in range(k_taps):
                    dact = dact + wk[k] * dcpad[ci, pl.ds(r0 + (k_taps - 1) - k, rc), :]
                dup_ref[0, pl.ds(r0, rc), cols] = dact.astype(BF16)
                dc = dcpad[ci, pl.ds(r0, rc), :]
                for k in range(k_taps):
                    dw_acc[k] = dw_acc[k] + dc * apad[ci, pl.ds(HALO16 - (k_taps - 1) + k + r0, rc), :]
                db_acc = db_acc + dc
            for k in range(k_taps):
                dw_ref[k:k + 1, cols] += jnp.sum(dw_acc[k], axis=0, keepdims=True)
            db_ref[:, cols] += jnp.sum(db_acc, axis=0, keepdims=True)

    return pl.pallas_call(
        body, name=name, grid=(dff // f, t // ts),
        in_specs=[pl.BlockSpec((2, ts, f), lambda j, i: (0, i, j)),
                  pl.BlockSpec((None, HALO16, f), lambda j, i: (0, jnp.maximum(i * hb - 1, 0), j)),
                  pl.BlockSpec((2, HALO16, f), lambda j, i: (0, jnp.minimum((i + 1) * hb, nhb - 1), j)),
                  pl.BlockSpec((ts, f), lambda j, i: (i, j)),
                  pl.BlockSpec((HALO16, f), lambda j, i: (jnp.minimum((i + 1) * hb, nhb - 1), j)),
                  pl.BlockSpec((k_taps, f), lambda j, i: (0, j)),
                  pl.BlockSpec((1, f), lambda j, i: (0, j))],
        out_specs=[pl.BlockSpec((2, ts, f), lambda j, i: (0, i, j)),
                   pl.BlockSpec((k_taps, f), lambda j, i: (0, j)),
                   pl.BlockSpec((1, f), lambda j, i: (0, j))],
        out_shape=[jax.ShapeDtypeStruct((2, t, dff), BF16), jax.ShapeDtypeStruct((k_taps, dff), F32),
                   jax.ShapeDtypeStruct((1, dff), F32)],
        scratch_shapes=[pltpu.VMEM((f // LANE, HALO16 + ext, LANE), F32), pltpu.VMEM((f // LANE, ext, LANE), F32)],
        compiler_params=_params("parallel", "arbitrary"),
    )(up, up, up, dg, dg, w, b)


def _sum_rows(name, g):
    ns, r, c = g.shape
    tr = _tile(r, 256)

    def body(g_ref, o_ref):
        acc = g_ref[0]
        for dev in range(1, ns):
            acc = acc + g_ref[dev]
        o_ref[...] = acc

    return pl.pallas_call(
        body, name=name, grid=(r // tr,),
        in_specs=[pl.BlockSpec((ns, tr, c), lambda i: (0, i, 0))],
        out_specs=pl.BlockSpec((tr, c), lambda i: (i, 0)),
        out_shape=jax.ShapeDtypeStruct((r, c), F32), compiler_params=_params("parallel"),
    )(g)


def _adamw(name, gsrc, w, m, v, layer=0, prev=None):
    ns, r, c = gsrc.shape
    nl = w.shape[0]
    tr = _tile(r, 256)
    prev = () if prev is None else tuple(prev)

    def body(g_ref, w_ref, m_ref, v_ref, *rest):
        go_ref, do_ref, mo_ref, vo_ref = rest[len(prev):]
        g = g_ref[0].astype(F32)
        for dev in range(1, ns):
            g = g + g_ref[dev].astype(F32)
        m_new = ADAM_B1 * m_ref[...] + (1.0 - ADAM_B1) * g
        v_new = ADAM_B2 * v_ref[...] + (1.0 - ADAM_B2) * (g * g)
        m_hat = m_new / (1.0 - ADAM_B1 ** ADAM_STEP)
        v_hat = v_new / (1.0 - ADAM_B2 ** ADAM_STEP)
        go_ref[...] = g
        do_ref[...] = -ADAM_LR * (m_hat / (jnp.sqrt(v_hat) + ADAM_EPS) + ADAM_WD * w_ref[...])
        mo_ref[...] = m_new
        vo_ref[...] = v_new

    row = pl.BlockSpec((None, tr, c), lambda i: (layer, i, 0))
    return pl.pallas_call(
        body, name=name, grid=(r // tr,),
        in_specs=[pl.BlockSpec((ns, tr, c), lambda i: (0, i, 0)), row, row, row] + [ANY] * len(prev),
        out_specs=[row] * 4, out_shape=[jax.ShapeDtypeStruct((nl, r, c), F32)] * 4,
        input_output_aliases={4 + i: i for i in range(len(prev))},
        compiler_params=_params("parallel"),
    )(gsrc, w, m, v, *prev)


def _ffn_forward(tag, r_in, h, get_wu, get_wd, wdw, bdw, seq, loss=None):
    t, d = r_in.shape
    tm = _tile(t, 512)
    wu = get_wu(h)
    dff = wu.shape[0] // 2
    tu = _tile(t, 1024)
    up = _mm(f"{tag}_up", h, wu, grid=(2, t // tu, 1),
             a_spec=pl.BlockSpec((tu, d), lambda j, i, k: (i, 0)),
             b_spec=pl.BlockSpec((dff, d), lambda j, i, k: (j, 0)),
             out_spec=pl.BlockSpec((None, tu, dff), lambda j, i, k: (j, i, 0)),
             out_shape=jax.ShapeDtypeStruct((2, t, dff), BF16), dims=NT, acc_shape=(tu, dff))
    wd = get_wd(up)
    g = _ffn_fwd(f"{tag}_act", up, wdw, bdw, seq)
    row = pl.BlockSpec((tm, d), lambda i, j, k: (i, 0))
    vec = pl.BlockSpec((1, d), lambda i, j, k: (0, 0))
    common = dict(grid=(t // tm, 1, 1), a_spec=pl.BlockSpec((tm, dff), lambda i, j, k: (i, 0)),
                  b_spec=pl.BlockSpec((dff, d), lambda i, j, k: (0, 0)), dims=NN, acc_shape=(tm, d))
    if loss is None:
        out = _mm(f"{tag}_down", g, wd, out_spec=row, out_shape=jax.ShapeDtypeStruct((t, d), F32),
                  extras=(r_in,), extra_specs=(row,), epilogue=lambda acc, ex, rows: ((ex[0][rows, :] + acc,), ()),
                  **common)
    else:
        def head(acc, ex, rows):
            dx, part, dgain = _loss_tile(ex[0][rows, :] + acc, ex[1][rows, :], ex[2][...])
            return (dx,), (part, dgain)

        out = _mm(f"{tag}_down", g, wd, out_spec=[row, pl.BlockSpec((1, 1), lambda i, j, k: (0, 0)), vec],
                  out_shape=[jax.ShapeDtypeStruct((t, d), F32), jax.ShapeDtypeStruct((1, 1), F32),
                             jax.ShapeDtypeStruct((1, d), F32)],
                  extras=(r_in, *loss), extra_specs=(row, row, vec), epilogue=head, n_sums=2, **common)
    return out, (r_in, h, up, g, wu, wd)


def _ffn_backward(tag, dr, saved, gain, wdw, bdw, seq, token=None):
    r_in, h, up, g, wu, wd = saved
    t, d = r_in.shape
    dff = wd.shape[0]
    tm = _tile(t, 512)
    tk = _tile(t, 2048)
    tku = _tile(t, 4096)
    cw = _ctile(dff, 1408)
    nc = dff // cw
    once = dict(pipeline_mode=pl.Buffered(1)) if tku == t else {}
    dg = _mm(f"{tag}_dg", dr, wd, grid=(t // tm, 1, 1),
             a_spec=pl.BlockSpec((tm, d), lambda i, j, k: (i, 0)),
             b_spec=pl.BlockSpec((dff, d), lambda i, j, k: (0, 0)),
             out_spec=pl.BlockSpec((tm, dff), lambda i, j, k: (i, 0)),
             out_shape=jax.ShapeDtypeStruct((t, dff), BF16), dims=NT, acc_shape=(tm, dff), token=token)
    dwd = _mm(f"{tag}_dwd", g, dr, grid=(dff // cw, 1, t // tk),
              a_spec=pl.BlockSpec((tk, cw), lambda i, j, k: (k, i)),
              b_spec=pl.BlockSpec((tk, d), lambda i, j, k: (k, 0)),
              out_spec=pl.BlockSpec((cw, d), lambda i, j, k: (i, 0)),
              out_shape=jax.ShapeDtypeStruct((dff, d), BF16), dims=TN, acc_shape=(cw, d))
    dup, dwdw, dbdw = _ffn_bwd(f"{tag}_dact", up, dg, wdw, bdw, seq)
    row = pl.BlockSpec((tm, d), lambda i, j, k: (i, 0))
    vec = pl.BlockSpec((1, d), lambda i, j, k: (0, 0))

    def norm_backward(acc, ex, rows):
        dx, dgain, colsum = _rms_bwd_tile(acc, ex[0][rows, :], ex[1][...], ex[2][rows, :])
        return (dx,), (dgain, colsum)

    dr_in, dgain, colsum = _mm(
        f"{tag}_dh", dup, wu, grid=(t // tm, 1, 1),
        a_spec=pl.BlockSpec((2, tm, dff), lambda i, j, k: (0, i, 0)),
        b_spec=pl.BlockSpec((2 * dff, d), lambda i, j, k: (0, 0), pipeline_mode=pl.Buffered(1)),
        out_spec=[row, vec, vec],
        out_shape=[jax.ShapeDtypeStruct((t, d), F32)] + [jax.ShapeDtypeStruct((1, d), F32)] * 2,
        dims=NN, acc_shape=(tm, d), extras=(r_in, gain, dr), extra_specs=(row, vec, row),
        epilogue=norm_backward, n_sums=2, parts=2)
    dwu = _mm(f"{tag}_dwu", dup, h, grid=(2 * nc, 1, t // tku),
              a_spec=pl.BlockSpec((None, tku, cw), lambda i, j, k: (i // nc, k, i % nc)),
              b_spec=pl.BlockSpec((tku, d), lambda i, j, k: (k, 0), **once),
              out_spec=pl.BlockSpec((cw, d), lambda i, j, k: (i, 0)),
              out_shape=jax.ShapeDtypeStruct((2 * dff, d), BF16), dims=TN, acc_shape=(cw, d))
    return dr_in, dgain, dwu, dwd, dwdw, dbdw, colsum


def _pad_to(vec, n):
    return jnp.pad(vec, (0, n - vec.shape[0]))


def _pack(parts, width):
    flat = jnp.concatenate([p.reshape(-1).astype(F32) for p in parts])
    n = -(-flat.shape[0] // (8 * width)) * (8 * width)
    return _pad_to(flat, n).reshape(n // width, width)


def _unpack(mat, shapes):
    flat = mat.reshape(-1)
    out, off = [], 0
    for s in shapes:
        n = 1
        for dim in s:
            n *= dim
        out.append(flat[off:off + n].reshape(s))
        off += n
    return out


def kernel(x, norm_mix, norm_ffn, conv_w_pw1, conv_b_pw1, conv_w_dw, conv_b_dw, conv_ln_g, conv_ln_b, conv_w_pw2, conv_b_pw2, pool_w, pool_b, pool_scale, ffn_w_up, ffn_w_dw, ffn_b_dw, ffn_w_down, final_norm, loss_target, m_norm_mix, m_norm_ffn, m_conv_w_pw1, m_conv_b_pw1, m_conv_w_dw, m_conv_b_dw, m_conv_ln_g, m_conv_ln_b, m_conv_w_pw2, m_conv_b_pw2, m_pool_w, m_pool_b, m_pool_scale, m_ffn_w_up, m_ffn_w_dw, m_ffn_b_dw, m_ffn_w_down, m_final_norm, v_norm_mix, v_norm_ffn, v_conv_w_pw1, v_conv_b_pw1, v_conv_w_dw, v_conv_b_dw, v_conv_ln_g, v_conv_ln_b, v_conv_w_pw2, v_conv_b_pw2, v_pool_w, v_pool_b, v_pool_scale, v_ffn_w_up, v_ffn_w_dw, v_ffn_b_dw, v_ffn_w_down, v_final_norm):
    bsz, seq, d = x.shape
    t = bsz * seq
    k_taps = conv_w_dw.shape[1]
    cs1 = conv_w_pw1.shape[2]
    dsh = d // N_DEV
    cg = d // N_GROUPS
    cgs = pool_w.shape[2]
    fu = ffn_w_up.shape[2]
    fd = ffn_w_down.shape[1]
    dff = fd * N_DEV
    nb = N_DEV // 2
    kf = ffn_w_dw.shape[1]
    fsh = ffn_w_dw.shape[2]
    my = _lin(_me())
    tm = _tile(t, 512)

    x2 = x.reshape(t, d)
    tgt2 = loss_target.reshape(t, d)

    small_shapes = [(k_taps, dsh), (dsh,), (dsh,), (2, kf, fsh)]
    small_mine = _pack([conv_w_dw[0], pool_b[0], pool_scale[0], ffn_w_dw], LANE)
    big = [conv_w_pw1[0], conv_w_pw2[0], ffn_w_up[0].T, ffn_w_down[0], pool_w[0], ffn_w_up[1].T, ffn_w_down[1]]
    gather = _exchange_start("gather_start", [small_mine] + [w.astype(BF16) for w in big], [GATHER2] * 8)
    h0 = _rms_fwd("l0_rms", x2, norm_mix[0:1])
    forwarded = _exchange_forward("gather_forward_w1", gather, [0, 1], h0)
    small_all, w1 = _exchange_wait("gather_wait_w1", gather, [0, 1], forwarded)
    parts = [_unpack(small_all[dev], small_shapes) for dev in range(N_DEV)]
    wdw = jnp.concatenate([p[0] for p in parts], axis=1)
    pool_b_full = jnp.concatenate([p[1] for p in parts]).reshape(1, d)
    pool_s_full = jnp.concatenate([p[2] for p in parts]).reshape(1, d)
    fwdw = jnp.concatenate([p[3] for p in parts], axis=2)
    fbdw = ffn_b_dw.reshape(2, 1, dff)

    def columns(w):
        return w.transpose(1, 0, 2).reshape(w.shape[1], N_DEV * w.shape[2])

    def column_shards(w):
        return w.reshape(w.shape[0], N_DEV, w.shape[1] // N_DEV).transpose(1, 0, 2)

    w1 = columns(w1)
    a = _mm("l0_pw1", h0, w1, grid=(2, t // tm, 1),
            a_spec=pl.BlockSpec((tm, d), lambda j, i, k: (i, 0)),
            b_spec=pl.BlockSpec((d, d), lambda j, i, k: (0, j)),
            out_spec=pl.BlockSpec((None, tm, d), lambda j, i, k: (j, i, 0)),
            out_shape=jax.ShapeDtypeStruct((2, t, d), BF16), dims=NN, acc_shape=(tm, d),
            extras=(conv_b_pw1,), extra_specs=(pl.BlockSpec((1, d), lambda j, i, k: (0, j)),),
            epilogue=lambda acc, ex, rows: ((acc + ex[0][...],), ()))
    v = _conv_fwd("l0_conv", a, wdw, conv_b_dw, seq)
    forwarded = _exchange_forward("gather_forward_wu0", gather, [2, 3], v)
    (w2,) = _exchange_wait("gather_wait_w2", gather, [2], forwarded)
    w2 = w2.reshape(d, d)
    row = pl.BlockSpec((tm, d), lambda i, j, k: (i, 0))
    vec = pl.BlockSpec((1, d), lambda i, j, k: (0, 0))
    square = pl.BlockSpec((d, d), lambda i, j, k: (0, 0))

    def ln_silu(v_blk, ex):
        s_blk = _ln_silu_tile(v_blk, ex[0][...], ex[1][...]).astype(BF16)
        return s_blk, s_blk

    def residual_and_norm(acc, ex, rows):
        r_blk = ex[3][rows, :] + (acc + ex[2][...])
        return (r_blk, _rms(r_blk, ex[4][...])), ()

    r1, h1, s = _mm("l0_pw2", v, w2, grid=(t // tm, 1, 1), a_spec=row, b_spec=square, out_spec=[row, row, row],
                    out_shape=[jax.ShapeDtypeStruct((t, d), F32), jax.ShapeDtypeStruct((t, d), BF16),
                               jax.ShapeDtypeStruct((t, d), BF16)],
                    dims=NN, acc_shape=(tm, d), extras=(conv_ln_g, conv_ln_b, conv_b_pw2, x2, norm_ffn[0:1]),
                    extra_specs=(vec, vec, vec, row, vec), prologue=ln_silu, epilogue=residual_and_norm)

    def up_getter(name, idx):
        return lambda after: _exchange_wait(name, gather, [idx], after)[0].reshape(2 * dff, d)

    def down_getter(name, idx, forward=None):
        def get(after):
            if forward is not None:
                after = _exchange_forward(forward[0], gather, forward[1], after)
            return _exchange_wait(name, gather, [idx], after)[0].reshape(dff, d)
        return get

    r2, ffn0_saved = _ffn_forward("f0", r1, h1, up_getter("gather_wait_wu0", 3),
                                  down_getter("gather_wait_wd0", 4, ("gather_forward_wu1", [4, 5, 6])),
                                  fwdw[0], fbdw[0], seq)
    forwarded = _exchange_forward("gather_forward_wd1", gather, [7], r2)
    (wp,) = _exchange_wait("gather_wait_wp", gather, [5], forwarded)
    wp = wp.transpose(1, 0, 2, 3).reshape(N_GROUPS, cg, cg)
    pooled, r3, h3 = _pool_mix_fwd("l1_mix", r2, norm_mix[1:2], wp, pool_s_full, pool_b_full, norm_ffn[1:2], seq)
    (dr4, loss_part, dfinal), ffn1_saved = _ffn_forward(
        "f1", r3, h3, up_getter("gather_wait_wu1", 6), down_getter("gather_wait_wd1", 7), fwdw[1], fbdw[1], seq,
        loss=(tgt2, final_norm.reshape(1, d)))

    dr3, dnf1, dwu1, dwd1, dfw1, dfb1, _ = _ffn_backward("f1", dr4, ffn1_saved, norm_ffn[1:2], fwdw[1], fbdw[1], seq)
    scatter_a = _exchange_start("scatter_f1_start", [dwu1.reshape(N_DEV, fu, d), dwd1.reshape(N_DEV, fd, d)],
                                [SCATTER, SCATTER])
    dr2, dwp, dpool_s, dpool_b, dnm1 = _pool_mix_bwd(
        "l1_dmix", pooled, wp, dr3, r2, norm_mix[1:2], pool_s_full + scatter_a["token"][0:1, 0:1], pool_b_full, seq)
    dr1, dnf0, dwu0, dwd0, dfw0, dfb0, db2 = _ffn_backward("f0", dr2, ffn0_saved, norm_ffn[0:1], fwdw[0], fbdw[0], seq)
    dwp_b = dwp.astype(BF16).reshape(N_GROUPS, N_DEV, cgs, cg).transpose(1, 0, 2, 3)
    scatter_b = _exchange_start("scatter_f0_start", [dwu0.reshape(N_DEV, fu, d), dwd0.reshape(N_DEV, fd, d), dwp_b],
                                [SCATTER] * 3)
    def ln_silu_backward(acc, ex, rows):
        dv_blk, dgain, dbias, colsum = _ln_silu_bwd_tile(acc, ex[0][rows, :], ex[1][...], ex[2][...])
        return (dv_blk,), (dgain, dbias, colsum)

    dv, dlg, dlb, dbdw = _mm("l0_ds", dr1, w2, grid=(t // tm, 1, 1), a_spec=row, b_spec=square,
                             out_spec=[row, vec, vec, vec],
                             out_shape=[jax.ShapeDtypeStruct((t, d), F32)] + [jax.ShapeDtypeStruct((1, d), F32)] * 3,
                             dims=NT, acc_shape=(tm, d), extras=(v, conv_ln_g, conv_ln_b), extra_specs=(row, vec, vec),
                             epilogue=ln_silu_backward, n_sums=3, token=scatter_b["token"])
    tk = _tile(t, 2048)
    dw2 = _mm("l0_dw2", s, dr1, grid=(1, 1, t // tk),
              a_spec=pl.BlockSpec((tk, d), lambda i, j, k: (k, 0)),
              b_spec=pl.BlockSpec((tk, d), lambda i, j, k: (k, 0)),
              out_spec=pl.BlockSpec((d, d), lambda i, j, k: (0, 0)),
              out_shape=jax.ShapeDtypeStruct((d, d), BF16), dims=TN, acc_shape=(d, d))
    da, dwdw, db1 = _conv_bwd("l0_dconv", a, dv, wdw, seq)
    tk1 = _tile(t, 4096)
    once = dict(pipeline_mode=pl.Buffered(1)) if tk1 == t else {}
    dw1 = _mm("l0_dw1", h0, da, grid=(1, 2, t // tk1),
              a_spec=pl.BlockSpec((tk1, d), lambda i, j, k: (k, 0), **once),
              b_spec=pl.BlockSpec((None, tk1, d), lambda i, j, k: (j, k, 0)),
              out_spec=pl.BlockSpec((d, d), lambda i, j, k: (0, j)),
              out_shape=jax.ShapeDtypeStruct((d, 2 * d), BF16), dims=TN, acc_shape=(d, d))
    scatter_c = _exchange_start("scatter_l0_start", [column_shards(dw1), dw2.reshape(N_DEV, d // N_DEV, d)],
                                [SCATTER, SCATTER])
    def norm_backward(acc, ex, rows):
        dx_blk, dgain, colsum = _rms_bwd_tile(acc, ex[0][rows, :], ex[1][...], ex[2][rows, :])
        return (dx_blk,), (dgain, colsum)

    dx, dnm0, _ = _mm("l0_dh", da, w1, grid=(t // tm, 1, 1),
                      a_spec=pl.BlockSpec((2, tm, d), lambda i, j, k: (0, i, 0)),
                      b_spec=pl.BlockSpec((d, 2 * d), lambda i, j, k: (0, 0), pipeline_mode=pl.Buffered(1)),
                      out_spec=[row, vec, vec],
                      out_shape=[jax.ShapeDtypeStruct((t, d), F32)] + [jax.ShapeDtypeStruct((1, d), F32)] * 2,
                      dims=NT, acc_shape=(tm, d), extras=(x2, norm_mix[0:1], dr1), extra_specs=(row, vec, row),
                      epilogue=norm_backward, n_sums=2, parts=2, token=scatter_c["token"])

    dffn_w = jnp.stack([dfw0, dfw1])
    dffn_b = jnp.stack([dfb0, dfb1]).reshape(2, dff)
    small_parts = [loss_part, jnp.concatenate([dnm0, dnm1]), jnp.concatenate([dnf0, dnf1]), db1, dwdw, dbdw, dlg, dlb,
                   db2, dpool_b, dpool_s, dffn_w, dffn_b, dfinal]
    small_part_shapes = [(1,), (2, d), (2, d), (1, 2 * d), (k_taps, d), (1, d), (1, d), (1, d), (1, d), (1, d), (1, d),
                         (2, kf, dff), (2, dff), (d,)]
    packed = _pack(small_parts, 8 * LANE)
    gather_small = _exchange_start("gather_small_start", [packed], [GATHER])

    def big_update(name, recv, w, m, v, layer=0, prev=None):
        shape = w.shape
        c = recv.shape[-1]
        rows = recv.size // (N_DEV * c)
        nl = w.size // (rows * c)
        outs = _adamw(name, recv.reshape(N_DEV, rows, c), w.reshape(nl, rows, c), m.reshape(nl, rows, c),
                      v.reshape(nl, rows, c), layer, prev)
        return outs, [o.reshape(shape) for o in outs]

    wu_t = [p.transpose(0, 2, 1) for p in (ffn_w_up, m_ffn_w_up, v_ffn_w_up)]
    g_wu1, g_wd1 = _exchange_wait("scatter_f1_wait", scatter_a, [0, 1], gather_small["token"])
    raw_wu, _ = big_update("adam_wu1", g_wu1, *wu_t, 1)
    raw_wd, _ = big_update("adam_wd1", g_wd1, ffn_w_down, m_ffn_w_down, v_ffn_w_down, 1)
    g_wu0, g_wd0, g_wp = _exchange_wait("scatter_f0_wait", scatter_b, [0, 1, 2], raw_wd[0])
    _, u_wu = big_update("adam_wu0", g_wu0, *wu_t, 0, raw_wu)
    u_wu = [o.transpose(0, 2, 1) for o in u_wu]
    _, u_wd = big_update("adam_wd0", g_wd0, ffn_w_down, m_ffn_w_down, v_ffn_w_down, 0, raw_wd)
    _, u_wp = big_update("adam_wp", g_wp, pool_w, m_pool_w, v_pool_w)
    g_w1, g_w2 = _exchange_wait("scatter_l0_wait", scatter_c, [0, 1], u_wp[0])
    _, u_w1 = big_update("adam_w1", g_w1, conv_w_pw1, m_conv_w_pw1, v_conv_w_pw1)
    _, u_w2 = big_update("adam_w2", g_w2, conv_w_pw2, m_conv_w_pw2, v_conv_w_pw2)
    (all_small,) = _exchange_wait("gather_small_wait", gather_small, [0], u_w2[0])
    summed = _sum_rows("sum_small_grads", all_small)
    (loss_v, g_nm, g_nf, g_b1, g_wdw, g_bdw, g_lg, g_lb, g_b2, g_pb, g_ps, g_fw, g_fb,
     g_fin) = _unpack(summed, small_part_shapes)
    loss = loss_v[0]
    g_wdw_mine = lax.dynamic_slice_in_dim(g_wdw, my * dsh, dsh, axis=1)[None]
    g_pb_mine = lax.dynamic_slice_in_dim(g_pb, my * dsh, dsh, axis=1)
    g_ps_mine = lax.dynamic_slice_in_dim(g_ps, my * dsh, dsh, axis=1)
    g_fw_mine = lax.dynamic_slice_in_dim(g_fw, my * fsh, fsh, axis=2)

    small_g =[g_nm, g_nf, g_b1, g_wdw_mine, g_bdw, g_lg, g_lb, g_b2, g_pb_mine, g_ps_mine, g_fw_mine, g_fb, g_fin]
    small_w = [norm_mix, norm_ffn, conv_b_pw1, conv_w_dw, conv_b_dw, conv_ln_g, conv_ln_b, conv_b_pw2, pool_b,
               pool_scale, ffn_w_dw, ffn_b_dw, final_norm]
    small_m = [m_norm_mix, m_norm_ffn, m_conv_b_pw1, m_conv_w_dw, m_conv_b_dw, m_conv_ln_g, m_conv_ln_b,
               m_conv_b_pw2, m_pool_b, m_pool_scale, m_ffn_w_dw, m_ffn_b_dw, m_final_norm]
    small_v = [v_norm_mix, v_norm_ffn, v_conv_b_pw1, v_conv_w_dw, v_conv_b_dw, v_conv_ln_g, v_conv_ln_b,
               v_conv_b_pw2, v_pool_b, v_pool_scale, v_ffn_w_dw, v_ffn_b_dw, v_final_norm]
    shapes = [w.shape for w in small_w]
    outs = _adamw("adam_small", _pack(small_g, 8 * LANE)[None], _pack(small_w, 8 * LANE)[None],
                  _pack(small_m, 8 * LANE)[None], _pack(small_v, 8 * LANE)[None])
    sg, sd, sm, sv = [_unpack(o, shapes) for o in outs]

    def leaf(kind):
        (nm, nf, b1, wdw_, bdw_, lg, lb, b2, pb, ps, fw, fb, fin) = (sg, sd, sm, sv)[kind]
        return [nm, nf, u_w1[kind], b1, wdw_, bdw_, lg, lb, u_w2[kind], b2, u_wp[kind], pb, ps, u_wu[kind], fw, fb,
                u_wd[kind], fin]

    return (loss, dx.reshape(bsz, seq, d), *leaf(0), *leaf(1), *leaf(2), *leaf(3))
```

```python
import functools

import jax
import jax.numpy as jnp
from jax import lax
from jax.experimental import pallas as pl
from jax.experimental.pallas import tpu as pltpu

F32 = jnp.float32
BF16 = jnp.bfloat16
MESH = pl.DeviceIdType.MESH
HBM = pl.BlockSpec(memory_space=pltpu.HBM)

N_DEV = 8
RMS_EPS = 1e-6
LN_EPS = 1e-5
POOL_WINDOWS = (2, 4, 8, 16)
N_GROUPS = len(POOL_WINDOWS)
ADAM_LR = 0.001
ADAM_B1 = 0.9
ADAM_B2 = 0.999
ADAM_EPS = 1e-08
ADAM_WD = 0.01
ADAM_STEP = 10

LANE = 128
HALO = 32
HALO16 = 16
VMEM_LIMIT = 56 * 1024 * 1024


def _params(*sem):
    return pltpu.CompilerParams(dimension_semantics=sem if sem else None, vmem_limit_bytes=VMEM_LIMIT)


def _tile(n, pref):
    for t in range(min(pref, n), 15, -1):
        if n % t == 0 and t % 16 == 0:
            return t
    return n


def _sigmoid(z):
    return 1.0 / (1.0 + jnp.exp(-z))


def _me():
    return lax.axis_index("x"), lax.axis_index("y"), lax.axis_index("c")


def _flip(pos, m):
    x, y, c = pos
    return ((1 - x) if m & 4 else x, (1 - y) if m & 2 else y, (1 - c) if m & 1 else c)


def _lin(pos):
    return 4 * pos[0] + 2 * pos[1] + pos[2]


SEM = pl.BlockSpec(memory_space=pltpu.SEMAPHORE)
ANY = pl.BlockSpec(memory_space=pl.ANY)
EFFECT = pltpu.SideEffectType.DATAFLOW_SIDE_EFFECTING


SCATTER = "scatter"
GATHER = "gather"
GATHER2 = "gather2"
ALL_MASKS = (1, 2, 3, 4, 5, 6, 7)
SIBLING = 1
CHIPS = (2, 4, 6)


class _Copies:
    def __init__(self, a, mode, src, land, send_sems, recv_sems):
        self.a, self.mode, self.src, self.land = a, mode, src, land
        self.send_sems, self.recv_sems = send_sems, recv_sems
        self.me = _me()
        self.first = (SIBLING,) + CHIPS if mode == GATHER2 else ALL_MASKS

    def _sems(self, m, to):
        return dict(send_sem=self.send_sems.at[self.a * N_DEV + m], recv_sem=self.recv_sems.at[self.a * N_DEV + m],
                    device_id=to, device_id_type=MESH)

    def _block(self, pid):
        return self.src.at[pid] if self.mode == SCATTER else self.src

    def local(self):
        my = _lin(self.me)
        return pltpu.make_async_copy(self._block(my), self.land.at[my], self.send_sems.at[self.a * N_DEV])

    def send(self, m):
        peer = _flip(self.me, m)
        return pltpu.make_async_remote_copy(src_ref=self._block(_lin(peer)), dst_ref=self.land.at[_lin(self.me)],
                                            **self._sems(m, peer))

    def arrival(self, m):
        rows = self.land.at[_lin(_flip(self.me, m))]
        return pltpu.make_async_remote_copy(src_ref=rows, dst_ref=rows, **self._sems(m, _flip(self.me, m)))

    def forward(self, m):
        rows = self.land.at[_lin(_flip(self.me, m))]
        return pltpu.make_async_remote_copy(src_ref=rows, dst_ref=rows, **self._sems(m | 1, _flip(self.me, SIBLING)))


def _exchange_start(name, arrs, modes):
    n = len(arrs)
    blocks = [a.shape[1:] if md == SCATTER else a.shape for a, md in zip(arrs, modes)]

    def body(*refs):
        srcs, lands = refs[:n], refs[n:2 * n]
        send_sems, recv_sems = refs[2 * n], refs[2 * n + 1]
        token = refs[-1]
        for a in range(n):
            cp = _Copies(a, modes[a], srcs[a], lands[a], send_sems, recv_sems)
            cp.local().start()
            for m in cp.first:
                cp.send(m).start()
        token[...] = jnp.zeros_like(token)

    lands = [lax.empty((N_DEV,) + tuple(b), a.dtype) for a, b in zip(arrs, blocks)]
    outs = pl.pallas_call(
        body, name=name,
        out_shape=(pltpu.SemaphoreType.DMA((n * N_DEV,)), pltpu.SemaphoreType.DMA((n * N_DEV,)),
                   *[pltpu.HBM(a.shape, a.dtype) for a in arrs], *[pltpu.HBM(l.shape, l.dtype) for l in lands],
                   jax.ShapeDtypeStruct((8, LANE), F32)),
        in_specs=[HBM] * (2 * n),
        out_specs=(SEM, SEM, *[HBM] * (2 * n), pl.BlockSpec(memory_space=pltpu.VMEM)),
        input_output_aliases={i: 2 + i for i in range(2 * n)},
        compiler_params=pltpu.CompilerParams(has_side_effects=EFFECT),
    )(*[pltpu.with_memory_space_constraint(a, pltpu.HBM) for a in arrs],
      *[pltpu.with_memory_space_constraint(l, pltpu.HBM) for l in lands])
    return dict(send=outs[0], recv=outs[1], srcs=list(outs[2:2 + n]), lands=list(outs[2 + n:2 + 2 * n]),
                modes=modes, token=outs[-1])


def _exchange_forward(name, handle, which, after):
    k = len(which)

    def half(wait):
        def body(*refs):
            lands = refs[:k]
            send_sems, recv_sems = refs[k], refs[k + 1]
            token = refs[-1]
            for pos, a in enumerate(which):
                cp = _Copies(a, GATHER2, None, lands[pos], send_sems, recv_sems)
                for m in CHIPS:
                    if wait:
                        cp.arrival(m).wait_recv()
                    else:
                        cp.forward(m).start()
            token[...] = jnp.zeros_like(token)
        return body

    def call(body, call_name, lands, after):
        outs = pl.pallas_call(
            body, name=call_name,
            out_shape=(*[pltpu.HBM(x.shape, x.dtype) for x in lands], jax.ShapeDtypeStruct((8, LANE), F32)),
            in_specs=[HBM] * k + [SEM, SEM, ANY], out_specs=(*[HBM] * k, pl.BlockSpec(memory_space=pltpu.VMEM)),
            input_output_aliases={i: i for i in range(k)},
            compiler_params=pltpu.CompilerParams(has_side_effects=EFFECT),
        )(*lands, handle["send"], handle["recv"], after)
        return list(outs[:k]), outs[-1]

    lands, arrived = call(half(True), name + "_arrived", [handle["lands"][a] for a in which], after)
    lands, token = call(half(False), name, lands, arrived)
    for pos, a in enumerate(which):
        handle["lands"][a] = lands[pos]
    return token


def _exchange_wait(name, handle, which, after):
    k = len(which)
    modes = handle["modes"]

    def body(*refs):
        srcs, lands = refs[:k], refs[k:2 * k]
        send_sems, recv_sems = refs[2 * k], refs[2 * k + 1]
        for pos, a in enumerate(which):
            cp = _Copies(a, modes[a], srcs[pos], lands[pos], send_sems, recv_sems)
            cp.local().wait()
            for m in cp.first:
                cp.send(m).wait_send()
            if modes[a] == GATHER2:
                for m in CHIPS:
                    cp.forward(m).wait_send()
                arrivals = (SIBLING,) + tuple(m | 1 for m in CHIPS)
            else:
                arrivals = ALL_MASKS
            for m in arrivals:
                cp.arrival(m).wait_recv()

    srcs = [handle["srcs"][a] for a in which]
    lands = [handle["lands"][a] for a in which]
    outs = pl.pallas_call(
        body, name=name,
        out_shape=tuple(pltpu.HBM(x.shape, x.dtype) for x in srcs + lands),
        in_specs=[HBM] * (2 * k) + [SEM, SEM, ANY], out_specs=tuple([HBM] * (2 * k)),
        input_output_aliases={i: i for i in range(2 * k)},
        compiler_params=pltpu.CompilerParams(has_side_effects=EFFECT),
    )(*srcs, *lands, handle["send"], handle["recv"], after)
    for pos, a in enumerate(which):
        handle["srcs"][a], handle["lands"][a] = outs[pos], outs[k + pos]
    return list(outs[k:])


def _mm(name, a, b, *, grid, a_spec, b_spec, out_spec, out_shape, dims, acc_shape, extras=(), extra_specs=(),
        epilogue=None, token=None, prologue=None, n_sums=0, parts=1):
    nk = grid[2]
    ne = len(extras)
    deps = () if token is None else (token,)
    dep_specs = [pl.BlockSpec((8, LANE), lambda i, j, k: (0, 0))] * len(deps)
    n_out = len(out_shape) if isinstance(out_shape, (list, tuple)) else 1
    n_tiles = n_out - n_sums - (1 if prologue is not None else 0)

    def body(a_ref, b_ref, *rest):
        ex, o_refs, acc_ref = rest[:ne], rest[ne + len(deps):ne + len(deps) + n_out], rest[ne + len(deps) + n_out]
        k = pl.program_id(2)
        if parts == 1:
            a_blk, saved = a_ref[...], None
            if prologue is not None:
                a_blk, saved = prologue(a_blk, ex)
                o_refs[n_tiles][...] = saved
            part = lax.dot_general(a_blk.astype(BF16), b_ref[...].astype(BF16), (dims, ((), ())),
                                   preferred_element_type=F32)
        else:
            kb = b_ref.shape[dims[1][0]] // parts
            part = None
            for p in range(parts):
                b_blk = b_ref[p * kb:(p + 1) * kb, :] if dims[1][0] == 0 else b_ref[:, p * kb:(p + 1) * kb]
                term = lax.dot_general(a_ref[p].astype(BF16), b_blk.astype(BF16), (dims, ((), ())),
                                       preferred_element_type=F32)
                part = term if part is None else part + term
        sum_refs = o_refs[n_out - n_sums:]

        def add_sums(terms):
            @pl.when((pl.program_id(0) == 0) & (pl.program_id(1) == 0))
            def _():
                for o_ref in sum_refs:
                    o_ref[...] = jnp.zeros_like(o_ref)

            for o_ref, term in zip(sum_refs, terms):
                o_ref[...] += jnp.sum(term, axis=0, keepdims=True)

        def finish(r):
            tiles, terms = ((r,), ()) if epilogue is None else epilogue(r, ex, slice(None))
            for o_ref, val in zip(o_refs, tiles):
                o_ref[...] = val.astype(o_ref.dtype)
            if n_sums:
                add_sums(terms)

        if nk == 1:
            finish(part)
            return

        @pl.when(k == 0)
        def _():
            acc_ref[...] = part

        @pl.when((k > 0) & (k < nk - 1))
        def _():
            acc_ref[...] += part

        @pl.when(k == nk - 1)
        def _():
            finish(acc_ref[...] + part)

    return pl.pallas_call(
        body, name=name, grid=grid, in_specs=[a_spec, b_spec, *extra_specs, *dep_specs], out_specs=out_spec,
        out_shape=out_shape, scratch_shapes=[pltpu.VMEM(acc_shape if nk > 1 else (8, LANE), F32)],
        compiler_params=_params(*(("arbitrary",) * 3 if n_sums else ("parallel", "parallel", "arbitrary"))),
    )(a, b, *extras, *deps)


def _rms(x, gain):
    return x * lax.rsqrt(jnp.mean(x * x, axis=-1, keepdims=True) + RMS_EPS) * gain


def _rms_bwd_tile(dh, x, gain, dres):
    rstd = lax.rsqrt(jnp.mean(x * x, axis=-1, keepdims=True) + RMS_EPS)
    xhat = x * rstd
    dxhat = dh * gain
    dx = dres + rstd * (dxhat - xhat * jnp.mean(dxhat * xhat, axis=-1, keepdims=True))
    return dx, dh * xhat, dx


def _ln_silu_tile(v, g, b):
    mu = jnp.mean(v, axis=-1, keepdims=True)
    cen = v - mu
    z = cen * lax.rsqrt(jnp.mean(cen * cen, axis=-1, keepdims=True) + LN_EPS) * g + b
    return z * _sigmoid(z)


def _ln_silu_bwd_tile(ds, v, g, b):
    mu = jnp.mean(v, axis=-1, keepdims=True)
    cen = v - mu
    rstd = lax.rsqrt(jnp.mean(cen * cen, axis=-1, keepdims=True) + LN_EPS)
    y = cen * rstd
    z = y * g + b
    sig = _sigmoid(z)
    dz = ds * sig * (1.0 + z * (1.0 - sig))
    dy = dz * g
    dv = rstd * (dy - jnp.mean(dy, axis=-1, keepdims=True) - y * jnp.mean(dy * y, axis=-1, keepdims=True))
    return dv, dz * y, dz, dv


def _loss_tile(x, tgt, gain):
    d = x.shape[-1]
    rstd = lax.rsqrt(jnp.mean(x * x, axis=-1, keepdims=True) + RMS_EPS)
    xhat = x * rstd
    err = xhat * gain - tgt
    dy = err / d
    dxhat = dy * gain
    dx = rstd * (dxhat - xhat * jnp.mean(dxhat * xhat, axis=-1, keepdims=True))
    return dx, 0.5 * jnp.mean(err * err, axis=-1, keepdims=True), dy * xhat


NN = ((1,), (0,))
NT = ((1,), (1,))
TN = ((0,), (0,))


def _rms_fwd(name, x, gain):
    t, d = x.shape
    tr = _tile(t, 512)

    def body(x_ref, g_ref, h_ref):
        h_ref[...] = _rms(x_ref[...], g_ref[...]).astype(BF16)

    return pl.pallas_call(
        body, name=name, grid=(t // tr,),
        in_specs=[pl.BlockSpec((tr, d), lambda i: (i, 0)), pl.BlockSpec((1, d), lambda i: (0, 0))],
        out_specs=pl.BlockSpec((tr, d), lambda i: (i, 0)),
        out_shape=jax.ShapeDtypeStruct((t, d), BF16), compiler_params=_params("parallel"),
    )(x, gain)


def _conv_tiles(t, seq):
    ts = _tile(seq, 512)
    return ts, seq // ts, _tile(ts, 64)


def _conv_fwd(name, a, w, b, seq):
    _, t, d = a.shape
    k_taps = w.shape[0]
    ts, tps, rc = _conv_tiles(t, seq)
    hb = ts // HALO

    def body(cur_ref, prev_ref, w_ref, b_ref, v_ref, upad):
        i = pl.program_id(1)
        first = (i % tps) == 0
        pv = prev_ref[0].astype(F32)
        pg = prev_ref[1].astype(F32)
        upad[0:HALO, :] = jnp.where(first, 0.0, pv * _sigmoid(pg))
        upad[HALO:HALO + ts, :] = cur_ref[0].astype(F32) * _sigmoid(cur_ref[1].astype(F32))
        wv = w_ref[...]
        bias = jnp.broadcast_to(b_ref[...], (rc, LANE))
        for r0 in range(0, ts, rc):
            acc = bias
            for k in range(k_taps):
                acc = acc + wv[k:k + 1, :] * upad[pl.ds(HALO - (k_taps - 1) + k + r0, rc), :]
            v_ref[pl.ds(r0, rc), :] = acc

    return pl.pallas_call(
        body, name=name, grid=(d // LANE, t // ts),
        in_specs=[pl.BlockSpec((2, ts, LANE), lambda c, i: (0, i, c)),
                  pl.BlockSpec((2, HALO, LANE), lambda c, i: (0, jnp.maximum(i * hb - 1, 0), c)),
                  pl.BlockSpec((k_taps, LANE), lambda c, i: (0, c)),
                  pl.BlockSpec((1, LANE), lambda c, i: (0, c))],
        out_specs=pl.BlockSpec((ts, LANE), lambda c, i: (i, c)),
        out_shape=jax.ShapeDtypeStruct((t, d), F32),
        scratch_shapes=[pltpu.VMEM((HALO + ts, LANE), F32)],
        compiler_params=_params("parallel", "parallel"),
    )(a, a, w, b)


def _conv_bwd(name, a, dv, w, seq):
    _, t, d = a.shape
    k_taps = w.shape[0]
    ts, tps, rc = _conv_tiles(t, seq)
    hb = ts // HALO
    nhb = t // HALO

    def body(cur_ref, prev_ref, dv_ref, ndv_ref, w_ref, da_ref, dw_ref, dbp_ref, upad, dvpad, dwrows):
        i = pl.program_id(1)
        first = (i % tps) == 0
        last = (i % tps) == tps - 1
        pv = prev_ref[0].astype(F32)
        pg = prev_ref[1].astype(F32)
        upad[0:HALO, :] = jnp.where(first, 0.0, pv * _sigmoid(pg))
        upad[HALO:HALO + ts, :] = cur_ref[0].astype(F32) * _sigmoid(cur_ref[1].astype(F32))
        dvpad[0:ts, :] = dv_ref[...]
        dvpad[ts:ts + HALO, :] = jnp.where(last, 0.0, ndv_ref[...])
        wv = w_ref[...]

        @pl.when(i == 0)
        def _():
            dw_ref[...] = jnp.zeros_like(dw_ref)
            dbp_ref[...] = jnp.zeros_like(dbp_ref)

        sv = jnp.zeros((1, LANE), F32)
        sg = jnp.zeros((1, LANE), F32)
        for r0 in range(0, ts, rc):
            du = jnp.zeros((rc, LANE), F32)
            for k in range(k_taps):
                du = du + wv[k:k + 1, :] * dvpad[pl.ds(r0 + (k_taps - 1) - k, rc), :]
            av = cur_ref[0, pl.ds(r0, rc), :].astype(F32)
            sig = _sigmoid(cur_ref[1, pl.ds(r0, rc), :].astype(F32))
            dval = du * sig
            dgate = du * av * sig * (1.0 - sig)
            da_ref[0, pl.ds(r0, rc), :] = dval.astype(BF16)
            da_ref[1, pl.ds(r0, rc), :] = dgate.astype(BF16)
            sv = sv + jnp.sum(dval, axis=0, keepdims=True)
            sg = sg + jnp.sum(dgate, axis=0, keepdims=True)
        dbp_ref[0] += sv
        dbp_ref[1] += sg

        for k in range(k_taps):
            acc = jnp.zeros((rc, LANE), F32)
            for r0 in range(0, ts, rc):
                acc = acc + dvpad[pl.ds(r0, rc), :] * upad[pl.ds(HALO - (k_taps - 1) + k + r0, rc), :]
            dwrows[k:k + 1, :] = jnp.sum(acc, axis=0, keepdims=True)
        dw_ref[...] += dwrows[0:k_taps, :]

    return pl.pallas_call(
        body, name=name, grid=(d // LANE, t // ts),
        in_specs=[pl.BlockSpec((2, ts, LANE), lambda c, i: (0, i, c)),
                  pl.BlockSpec((2, HALO, LANE), lambda c, i: (0, jnp.maximum(i * hb - 1, 0), c)),
                  pl.BlockSpec((ts, LANE), lambda c, i: (i, c)),
                  pl.BlockSpec((HALO, LANE), lambda c, i: (jnp.minimum((i + 1) * hb, nhb - 1), c)),
                  pl.BlockSpec((k_taps, LANE), lambda c, i: (0, c))],
        out_specs=[pl.BlockSpec((2, ts, LANE), lambda c, i: (0, i, c)),
                   pl.BlockSpec((k_taps, LANE), lambda c, i: (0, c)),
                   pl.BlockSpec((2, 1, LANE), lambda c, i: (0, 0, c))],
        out_shape=[jax.ShapeDtypeStruct((2, t, d), BF16), jax.ShapeDtypeStruct((k_taps, d), F32),
                   jax.ShapeDtypeStruct((2, 1, d), F32)],
        scratch_shapes=[pltpu.VMEM((HALO + ts, LANE), F32), pltpu.VMEM((ts + HALO, LANE), F32),
                        pltpu.VMEM((HALO, LANE), F32)],
        compiler_params=_params("parallel", "arbitrary"),
    )(a, a, dv, dv, w)


def _pool_mix_fwd(name, x, gain, wp, scale, bias, next_gain, seq):
    t, d = x.shape
    ts = _tile(seq, 256)
    tps = seq // ts
    hb = ts // HALO
    cg = d // N_GROUPS

    def body(cur_ref, prev_ref, g_ref, w_ref, s_ref, b_ref, ng_ref, p_ref, r_ref, h_ref, hpad):
        i = pl.program_id(0)
        first = (i % tps) == 0
        g = g_ref[...]
        hpad[0:HALO, :] = jnp.where(first, 0.0, _rms(prev_ref[...], g))
        hpad[HALO:HALO + ts, :] = _rms(cur_ref[...], g)
        pos = (i % tps) * ts + lax.broadcasted_iota(jnp.int32, (ts, 1), 0)
        for gi, win in enumerate(POOL_WINDOWS):
            sl = slice(gi * cg, (gi + 1) * cg)
            own = hpad[HALO:HALO + ts, sl]
            acc = own
            for j in range(1, win):
                acc = acc + hpad[HALO - j:HALO - j + ts, sl]
            cnt = jnp.minimum(pos + 1, win).astype(F32)
            pooled = (acc / cnt - own).astype(BF16)
            p_ref[:, sl] = pooled
            mixed = jnp.dot(pooled, w_ref[gi], preferred_element_type=F32)
            r_ref[:, sl] = cur_ref[:, sl] + s_ref[:, sl] * (mixed + b_ref[:, sl])
        h_ref[...] = _rms(r_ref[...], ng_ref[...]).astype(BF16)

    row = pl.BlockSpec((ts, d), lambda i: (i, 0))
    vec = pl.BlockSpec((1, d), lambda i: (0, 0))
    return pl.pallas_call(
        body, name=name, grid=(t // ts,),
        in_specs=[row, pl.BlockSpec((HALO, d), lambda i: (jnp.maximum(i * hb - 1, 0), 0)), vec,
                  pl.BlockSpec((N_GROUPS, cg, cg), lambda i: (0, 0, 0)), vec, vec, vec],
        out_specs=[row, row, row],
        out_shape=[jax.ShapeDtypeStruct((t, d), BF16), jax.ShapeDtypeStruct((t, d), F32),
                   jax.ShapeDtypeStruct((t, d), BF16)],
        scratch_shapes=[pltpu.VMEM((HALO + ts, d), F32)],
        compiler_params=_params("parallel"),
    )(x, x, gain, wp, scale, bias, next_gain)


def _pool_mix_bwd(name, pooled, wp, dr, x, gain, scale, bias, seq):
    t, d = x.shape
    ts = _tile(seq, 256)
    tps = seq // ts
    hb = ts // HALO
    nhb = t // HALO
    cg = d // N_GROUPS

    def body(p_ref, w_ref, dr_ref, ndr_ref, x_ref, g_ref, s_ref, b_ref, dx_ref, dw_ref, ds_ref, db_ref, dg_ref,
             qpad, dh):
        i = pl.program_id(0)
        last = (i % tps) == tps - 1
        pos = (i % tps) * ts + lax.broadcasted_iota(jnp.int32, (ts, 1), 0)

        @pl.when(i == 0)
        def _():
            dw_ref[...] = jnp.zeros_like(dw_ref)
            ds_ref[...] = jnp.zeros_like(ds_ref)
            db_ref[...] = jnp.zeros_like(db_ref)
            dg_ref[...] = jnp.zeros_like(dg_ref)

        for gi, win in enumerate(POOL_WINDOWS):
            sl = slice(gi * cg, (gi + 1) * cg)
            wv = w_ref[gi]
            sc = s_ref[:, sl]
            drv = dr_ref[:, sl]
            dmx = drv * sc
            dmx16 = dmx.astype(BF16)
            pooled = p_ref[:, sl]
            dw_ref[gi] += lax.dot_general(pooled, dmx16, (TN, ((), ())), preferred_element_type=F32)
            mixed = jnp.dot(pooled, wv, preferred_element_type=F32)
            ds_ref[:, sl] += jnp.sum(drv * (mixed + b_ref[:, sl]), axis=0, keepdims=True)
            db_ref[:, sl] += jnp.sum(dmx, axis=0, keepdims=True)
            cur = lax.dot_general(dmx16, wv, (NT, ((), ())), preferred_element_type=F32)
            nxt = lax.dot_general((ndr_ref[:, sl] * sc).astype(BF16), wv, (NT, ((), ())),
                                  preferred_element_type=F32)
            qpad[0:ts, sl] = cur / jnp.minimum(pos + 1, win).astype(F32)
            qpad[ts:ts + HALO, sl] = jnp.where(last, 0.0, nxt / float(win))
            acc = -cur
            for j in range(win):
                acc = acc + qpad[j:j + ts, sl]
            dh[:, sl] = acc
        dx, dgain_term, _ = _rms_bwd_tile(dh[...], x_ref[...], g_ref[...], dr_ref[...])
        dx_ref[...] = dx
        dg_ref[...] += jnp.sum(dgain_term, axis=0, keepdims=True)

    row = pl.BlockSpec((ts, d), lambda i: (i, 0))
    vec = pl.BlockSpec((1, d), lambda i: (0, 0))
    return pl.pallas_call(
        body, name=name, grid=(t // ts,),
        in_specs=[row, pl.BlockSpec((N_GROUPS, cg, cg), lambda i: (0, 0, 0)), row,
                  pl.BlockSpec((HALO, d), lambda i: (jnp.minimum((i + 1) * hb, nhb - 1), 0)), row, vec, vec, vec],
        out_specs=[row, pl.BlockSpec((N_GROUPS, cg, cg), lambda i: (0, 0, 0)), vec, vec, vec],
        out_shape=[jax.ShapeDtypeStruct((t, d), F32), jax.ShapeDtypeStruct((N_GROUPS, cg, cg), F32)]
        + [jax.ShapeDtypeStruct((1, d), F32)] * 3,
        scratch_shapes=[pltpu.VMEM((ts + HALO, d), F32), pltpu.VMEM((ts, d), F32)],
        compiler_params=_params("arbitrary"),
    )(pooled, wp, dr, dr, x, gain, scale, bias)


def _ctile(n, pref):
    return max(c for c in range(LANE, min(pref, n) + 1, LANE) if n % c == 0)


FFN_COLS = 1408
FFN_ROWS = 32


def _ffn_fwd(name, up, w, b, seq):
    _, t, dff = up.shape
    f = _ctile(dff, FFN_COLS)
    k_taps = w.shape[0]
    ts = _tile(seq, 256)
    tps = seq // ts
    hb = ts // HALO16
    rc = _tile(ts, FFN_ROWS)

    def body(cur_ref, prev_ref, w_ref, b_ref, g_ref, apad):
        i = pl.program_id(1)
        first = (i % tps) == 0
        for ci, c0 in enumerate(range(0, f, LANE)):
            cols = slice(c0, c0 + LANE)
            apad[ci, 0:HALO16, :] = jnp.where(first, 0.0, prev_ref[:, cols].astype(F32))
            apad[ci, HALO16:HALO16 + ts, :] = cur_ref[0, :, cols].astype(F32)
            wv = w_ref[:, cols]
            wk = [jnp.broadcast_to(wv[k:k + 1, :], (rc, LANE)) for k in range(k_taps)]
            bias = jnp.broadcast_to(b_ref[:, cols], (rc, LANE))
            for r0 in range(0, ts, rc):
                c = bias
                for k in range(k_taps):
                    c = c + wk[k] * apad[ci, pl.ds(HALO16 - (k_taps - 1) + k + r0, rc), :]
                gate = cur_ref[1, pl.ds(r0, rc), cols].astype(F32)
                g_ref[pl.ds(r0, rc), cols] = (c * _sigmoid(c) * gate).astype(BF16)

    return pl.pallas_call(
        body, name=name, grid=(dff // f, t // ts),
        in_specs=[pl.BlockSpec((2, ts, f), lambda j, i: (0, i, j)),
                  pl.BlockSpec((None, HALO16, f), lambda j, i: (0, jnp.maximum(i * hb - 1, 0), j)),
                  pl.BlockSpec((k_taps, f), lambda j, i: (0, j)),
                  pl.BlockSpec((1, f), lambda j, i: (0, j))],
        out_specs=pl.BlockSpec((ts, f), lambda j, i: (i, j)),
        out_shape=jax.ShapeDtypeStruct((t, dff), BF16),
        scratch_shapes=[pltpu.VMEM((f // LANE, HALO16 + ts, LANE), F32)],
        compiler_params=_params("parallel", "parallel"),
    )(up, up, w, b)


def _ffn_bwd(name, up, dg, w, b, seq):
    _, t, dff = up.shape
    f = _ctile(dff, FFN_COLS)
    k_taps = w.shape[0]
    ts = _tile(seq, 256)
    tps = seq // ts
    hb = ts // HALO16
    nhb = t // HALO16
    ext = ts + HALO16
    rc = _tile(ts, FFN_ROWS)

    def body(cur_ref, prev_ref, next_ref, dg_ref, ndg_ref, w_ref, b_ref, dup_ref, dw_ref, db_ref, apad, dcpad):
        i = pl.program_id(1)
        first = (i % tps) == 0
        last = (i % tps) == tps - 1

        @pl.when(i == 0)
        def _():
            dw_ref[...] = jnp.zeros_like(dw_ref)
            db_ref[...] = jnp.zeros_like(db_ref)

        for ci, c0 in enumerate(range(0, f, LANE)):
            cols = slice(c0, c0 + LANE)
            apad[ci, 0:HALO16, :] = jnp.where(first, 0.0, prev_ref[:, cols].astype(F32))
            apad[ci, HALO16:HALO16 + ts, :] = cur_ref[0, :, cols].astype(F32)
            apad[ci, HALO16 + ts:HALO16 + ext, :] = next_ref[0, :, cols].astype(F32)
            wv = w_ref[:, cols]
            wk = [jnp.broadcast_to(wv[k:k + 1, :], (rc, LANE)) for k in range(k_taps)]
            bias = jnp.broadcast_to(b_ref[:, cols], (rc, LANE))

            def conv_grad(r0, n, gate, dgv):
                c = bias[0:n]
                for k in range(k_taps):
                    c = c + wk[k][0:n] * apad[ci, pl.ds(HALO16 - (k_taps - 1) + k + r0, n), :]
                sig = _sigmoid(c)
                return dgv * gate * sig * (1.0 + c * (1.0 - sig)), c * sig

            for r0 in range(0, ts, rc):
                dgv = dg_ref[pl.ds(r0, rc), cols].astype(F32)
                dc, silu = conv_grad(r0, rc, cur_ref[1, pl.ds(r0, rc), cols].astype(F32), dgv)
                dcpad[ci, pl.ds(r0, rc), :] = dc
                dup_ref[1, pl.ds(r0, rc), cols] = (dgv * silu).astype(BF16)
            dgv = jnp.where(last, 0.0, ndg_ref[:, cols].astype(F32))
            dc, _ = conv_grad(ts, HALO16, next_ref[1, :, cols].astype(F32), dgv)
            dcpad[ci, ts:ext, :] = dc

            dw_acc = [jnp.zeros((rc, LANE), F32) for _ in range(k_taps)]
            db_acc = jnp.zeros((rc, LANE), F32)
            for r0 in range(0, ts, rc):
                dact = jnp.zeros((rc, LANE), F32)
                for k in range(k_taps):
                    dact = dact + wk[k] * dcpad[ci, pl.ds(r0 + (k_taps - 1) - k, rc), :]
                dup_ref[0, pl.ds(r0, rc), cols] = dact.astype(BF16)
                dc = dcpad[ci, pl.ds(r0, rc), :]
                for k in range(k_taps):
                    dw_acc[k] = dw_acc[k] + dc * apad[ci, pl.ds(HALO16 - (k_taps - 1) + k + r0, rc), :]
                db_acc = db_acc + dc
            for k in range(k_taps):
                dw_ref[k:k + 1, cols] += jnp.sum(dw_acc[k], axis=0, keepdims=True)
            db_ref[:, cols] += jnp.sum(db_acc, axis=0, keepdims=True)

    return pl.pallas_call(
        body, name=name, grid=(dff // f, t // ts),
        in_specs=[pl.BlockSpec((2, ts, f), lambda j, i: (0, i, j)),
                  pl.BlockSpec((None, HALO16, f), lambda j, i: (0, jnp.maximum(i * hb - 1, 0), j)),
                  pl.BlockSpec((2, HALO16, f), lambda j, i: (0, jnp.minimum((i + 1) * hb, nhb - 1), j)),
                  pl.BlockSpec((ts, f), lambda j, i: (i, j)),
                  pl.BlockSpec((HALO16, f), lambda j, i: (jnp.minimum((i + 1) * hb, nhb - 1), j)),
                  pl.BlockSpec((k_taps, f), lambda j, i: (0, j)),
                  pl.BlockSpec((1, f), lambda j, i: (0, j))],
        out_specs=[pl.BlockSpec((2, ts, f), lambda j, i: (0, i, j)),
                   pl.BlockSpec((k_taps, f), lambda j, i: (0, j)),
                   pl.BlockSpec((1, f), lambda j, i: (0, j))],
        out_shape=[jax.ShapeDtypeStruct((2, t, dff), BF16), jax.ShapeDtypeStruct((k_taps, dff), F32),
                   jax.ShapeDtypeStruct((1, dff), F32)],
        scratch_shapes=[pltpu.VMEM((f // LANE, HALO16 + ext, LANE), F32), pltpu.VMEM((f // LANE, ext, LANE), F32)],
        compiler_params=_params("parallel", "arbitrary"),
    )(up, up, up, dg, dg, w, b)


def _sum_rows(name, g):
    ns, r, c = g.shape
    tr = _tile(r, 256)

    def body(g_ref, o_ref):
        acc = g_ref[0]
        for dev in range(1, ns):
            acc = acc + g_ref[dev]
        o_ref[...] = acc

    return pl.pallas_call(
        body, name=name, grid=(r // tr,),
        in_specs=[pl.BlockSpec((ns, tr, c), lambda i: (0, i, 0))],
        out_specs=pl.BlockSpec((tr, c), lambda i: (i, 0)),
        out_shape=jax.ShapeDtypeStruct((r, c), F32), compiler_params=_params("parallel"),
    )(g)


def _adamw(name, gsrc, w, m, v, layer=0, prev=None):
    ns, r, c = gsrc.shape
    nl = w.shape[0]
    tr = _tile(r, 256)
    prev = () if prev is None else tuple(prev)

    def body(g_ref, w_ref, m_ref, v_ref, *rest):
        go_ref, do_ref, mo_ref, vo_ref = rest[len(prev):]
        g = g_ref[0].astype(F32)
        for dev in range(1, ns):
            g = g + g_ref[dev].astype(F32)
        m_new = ADAM_B1 * m_ref[...] + (1.0 - ADAM_B1) * g
        v_new = ADAM_B2 * v_ref[...] + (1.0 - ADAM_B2) * (g * g)
        m_hat = m_new / (1.0 - ADAM_B1 ** ADAM_STEP)
        v_hat = v_new / (1.0 - ADAM_B2 ** ADAM_STEP)
        go_ref[...] = g
        do_ref[...] = -ADAM_LR * (m_hat / (jnp.sqrt(v_hat) + ADAM_EPS) + ADAM_WD * w_ref[...])
        mo_ref[...] = m_new
        vo_ref[...] = v_new

    row = pl.BlockSpec((None, tr, c), lambda i: (layer, i, 0))
    return pl.pallas_call(
        body, name=name, grid=(r // tr,),
        in_specs=[pl.BlockSpec((ns, tr, c), lambda i: (0, i, 0)), row, row, row] + [ANY] * len(prev),
        out_specs=[row] * 4, out_shape=[jax.ShapeDtypeStruct((nl, r, c), F32)] * 4,
        input_output_aliases={4 + i: i for i in range(len(prev))},
        compiler_params=_params("parallel"),
    )(gsrc, w, m, v, *prev)


def _ffn_forward(tag, r_in, h, get_wu, get_wd, wdw, bdw, seq, loss=None):
    t, d = r_in.shape
    tm = _tile(t, 512)
    wu = get_wu(h)
    dff = wu.shape[0] // 2
    tu = _tile(t, 1024)
    up = _mm(f"{tag}_up", h, wu, grid=(2, t // tu, 1),
             a_spec=pl.BlockSpec((tu, d), lambda j, i, k: (i, 0)),
             b_spec=pl.BlockSpec((dff, d), lambda j, i, k: (j, 0)),
             out_spec=pl.BlockSpec((None, tu, dff), lambda j, i, k: (j, i, 0)),
             out_shape=jax.ShapeDtypeStruct((2, t, dff), BF16), dims=NT, acc_shape=(tu, dff))
    wd = get_wd(up)
    g = _ffn_fwd(f"{tag}_act", up, wdw, bdw, seq)
    row = pl.BlockSpec((tm, d), lambda i, j, k: (i, 0))
    vec = pl.BlockSpec((1, d), lambda i, j, k: (0, 0))
    common = dict(grid=(t // tm, 1, 1), a_spec=pl.BlockSpec((tm, dff), lambda i, j, k: (i, 0)),
                  b_spec=pl.BlockSpec((dff, d), lambda i, j, k: (0, 0)), dims=NN, acc_shape=(tm, d))
    if loss is None:
        out = _mm(f"{tag}_down", g, wd, out_spec=row, out_shape=jax.ShapeDtypeStruct((t, d), F32),
                  extras=(r_in,), extra_specs=(row,), epilogue=lambda acc, ex, rows: ((ex[0][rows, :] + acc,), ()),
                  **common)
    else:
        def head(acc, ex, rows):
            dx, part, dgain = _loss_tile(ex[0][rows, :] + acc, ex[1][rows, :], ex[2][...])
            return (dx,), (part, dgain)

        out = _mm(f"{tag}_down", g, wd, out_spec=[row, pl.BlockSpec((1, 1), lambda i, j, k: (0, 0)), vec],
                  out_shape=[jax.ShapeDtypeStruct((t, d), F32), jax.ShapeDtypeStruct((1, 1), F32),
                             jax.ShapeDtypeStruct((1, d), F32)],
                  extras=(r_in, *loss), extra_specs=(row, row, vec), epilogue=head, n_sums=2, **common)
    return out, (r_in, h, up, g, wu, wd)


def _ffn_backward(tag, dr, saved, gain, wdw, bdw, seq, token=None):
    r_in, h, up, g, wu, wd = saved
    t, d = r_in.shape
    dff = wd.shape[0]
    tm = _tile(t, 512)
    tk = _tile(t, 2048)
    tku = _tile(t, 4096)
    cw = _ctile(dff, 1408)
    nc = dff // cw
    once = dict(pipeline_mode=pl.Buffered(1)) if tku == t else {}
    dg = _mm(f"{tag}_dg", dr, wd, grid=(t // tm, 1, 1),
             a_spec=pl.BlockSpec((tm, d), lambda i, j, k: (i, 0)),
             b_spec=pl.BlockSpec((dff, d), lambda i, j, k: (0, 0)),
             out_spec=pl.BlockSpec((tm, dff), lambda i, j, k: (i, 0)),
             out_shape=jax.ShapeDtypeStruct((t, dff), BF16), dims=NT, acc_shape=(tm, dff), token=token)
    dwd = _mm(f"{tag}_dwd", g, dr, grid=(dff // cw, 1, t // tk),
              a_spec=pl.BlockSpec((tk, cw), lambda i, j, k: (k, i)),
              b_spec=pl.BlockSpec((tk, d), lambda i, j, k: (k, 0)),
              out_spec=pl.BlockSpec((cw, d), lambda i, j, k: (i, 0)),
              out_shape=jax.ShapeDtypeStruct((dff, d), BF16), dims=TN, acc_shape=(cw, d))
    dup, dwdw, dbdw = _ffn_bwd(f"{tag}_dact", up, dg, wdw, bdw, seq)
    row = pl.BlockSpec((tm, d), lambda i, j, k: (i, 0))
    vec = pl.BlockSpec((1, d), lambda i, j, k: (0, 0))

    def norm_backward(acc, ex, rows):
        dx, dgain, colsum = _rms_bwd_tile(acc, ex[0][rows, :], ex[1][...], ex[2][rows, :])
        return (dx,), (dgain, colsum)

    dr_in, dgain, colsum = _mm(
        f"{tag}_dh", dup, wu, grid=(t // tm, 1, 1),
        a_spec=pl.BlockSpec((2, tm, dff), lambda i, j, k: (0, i, 0)),
        b_spec=pl.BlockSpec((2 * dff, d), lambda i, j, k: (0, 0), pipeline_mode=pl.Buffered(1)),
        out_spec=[row, vec, vec],
        out_shape=[jax.ShapeDtypeStruct((t, d), F32)] + [jax.ShapeDtypeStruct((1, d), F32)] * 2,
        dims=NN, acc_shape=(tm, d), extras=(r_in, gain, dr), extra_specs=(row, vec, row),
        epilogue=norm_backward, n_sums=2, parts=2)
    dwu = _mm(f"{tag}_dwu", dup, h, grid=(2 * nc, 1, t // tku),
              a_spec=pl.BlockSpec((None, tku, cw), lambda i, j, k: (i // nc, k, i % nc)),
              b_spec=pl.BlockSpec((tku, d), lambda i, j, k: (k, 0), **once),
              out_spec=pl.BlockSpec((cw, d), lambda i, j, k: (i, 0)),
              out_shape=jax.ShapeDtypeStruct((2 * dff, d), BF16), dims=TN, acc_shape=(cw, d))
    return dr_in, dgain, dwu, dwd, dwdw, dbdw, colsum


def _pad_to(vec, n):
    return jnp.pad(vec, (0, n - vec.shape[0]))


def _pack(parts, width):
    flat = jnp.concatenate([p.reshape(-1).astype(F32) for p in parts])
    n = -(-flat.shape[0] // (8 * width)) * (8 * width)
    return _pad_to(flat, n).reshape(n // width, width)


def _unpack(mat, shapes):
    flat = mat.reshape(-1)
    out, off = [], 0
    for s in shapes:
        n = 1
        for dim in s:
            n *= dim
        out.append(flat[off:off + n].reshape(s))
        off += n
    return out


def kernel(x, norm_mix, norm_ffn, conv_w_pw1, conv_b_pw1, conv_w_dw, conv_b_dw, conv_ln_g, conv_ln_b, conv_w_pw2, conv_b_pw2, pool_w, pool_b, pool_scale, ffn_w_up, ffn_w_dw, ffn_b_dw, ffn_w_down, final_norm, loss_target, m_norm_mix, m_norm_ffn, m_conv_w_pw1, m_conv_b_pw1, m_conv_w_dw, m_conv_b_dw, m_conv_ln_g, m_conv_ln_b, m_conv_w_pw2, m_conv_b_pw2, m_pool_w, m_pool_b, m_pool_scale, m_ffn_w_up, m_ffn_w_dw, m_ffn_b_dw, m_ffn_w_down, m_final_norm, v_norm_mix, v_norm_ffn, v_conv_w_pw1, v_conv_b_pw1, v_conv_w_dw, v_conv_b_dw, v_conv_ln_g, v_conv_ln_b, v_conv_w_pw2, v_conv_b_pw2, v_pool_w, v_pool_b, v_pool_scale, v_ffn_w_up, v_ffn_w_dw, v_ffn_b_dw, v_ffn_w_down, v_final_norm):
    bsz, seq, d = x.shape
    t = bsz * seq
    k_taps = conv_w_dw.shape[1]
    cs1 = conv_w_pw1.shape[2]
    dsh = d // N_DEV
    cg = d // N_GROUPS
    cgs = pool_w.shape[2]
    fu = ffn_w_up.shape[2]
    fd = ffn_w_down.shape[1]
    dff = fd * N_DEV
    nb = N_DEV // 2
    kf = ffn_w_dw.shape[1]
    fsh = ffn_w_dw.shape[2]
    my = _lin(_me())
    tm = _tile(t, 512)

    x2 = x.reshape(t, d)
    tgt2 = loss_target.reshape(t, d)

    small_shapes = [(k_taps, dsh), (dsh,), (dsh,), (2, kf, fsh)]
    small_mine = _pack([conv_w_dw[0], pool_b[0], pool_scale[0], ffn_w_dw], LANE)
    big = [conv_w_pw1[0], conv_w_pw2[0], ffn_w_up[0].T, ffn_w_down[0], pool_w[0], ffn_w_up[1].T, ffn_w_down[1]]
    gather = _exchange_start("gather_start", [small_mine] + [w.astype(BF16) for w in big], [GATHER2] * 8)
    h0 = _rms_fwd("l0_rms", x2, norm_mix[0:1])
    forwarded = _exchange_forward("gather_forward_w1", gather, [0, 1], h0)
    small_all, w1 = _exchange_wait("gather_wait_w1", gather, [0, 1], forwarded)
    parts = [_unpack(small_all[dev], small_shapes) for dev in range(N_DEV)]
    wdw = jnp.concatenate([p[0] for p in parts], axis=1)
    pool_b_full = jnp.concatenate([p[1] for p in parts]).reshape(1, d)
    pool_s_full = jnp.concatenate([p[2] for p in parts]).reshape(1, d)
    fwdw = jnp.concatenate([p[3] for p in parts], axis=2)
    fbdw = ffn_b_dw.reshape(2, 1, dff)

    def columns(w):
        return w.transpose(1, 0, 2).reshape(w.shape[1], N_DEV * w.shape[2])

    def column_shards(w):
        return w.reshape(w.shape[0], N_DEV, w.shape[1] // N_DEV).transpose(1, 0, 2)

    w1 = columns(w1)
    a = _mm("l0_pw1", h0, w1, grid=(2, t // tm, 1),
            a_spec=pl.BlockSpec((tm, d), lambda j, i, k: (i, 0)),
            b_spec=pl.BlockSpec((d, d), lambda j, i, k: (0, j)),
            out_spec=pl.BlockSpec((None, tm, d), lambda j, i, k: (j, i, 0)),
            out_shape=jax.ShapeDtypeStruct((2, t, d), BF16), dims=NN, acc_shape=(tm, d),
            extras=(conv_b_pw1,), extra_specs=(pl.BlockSpec((1, d), lambda j, i, k: (0, j)),),
            epilogue=lambda acc, ex, rows: ((acc + ex[0][...],), ()))
    v = _conv_fwd("l0_conv", a, wdw, conv_b_dw, seq)
    forwarded = _exchange_forward("gather_forward_wu0", gather, [2, 3], v)
    (w2,) = _exchange_wait("gather_wait_w2", gather, [2], forwarded)
    w2 = w2.reshape(d, d)
    row = pl.BlockSpec((tm, d), lambda i, j, k: (i, 0))
    vec = pl.BlockSpec((1, d), lambda i, j, k: (0, 0))
    square = pl.BlockSpec((d, d), lambda i, j, k: (0, 0))

    def ln_silu(v_blk, ex):
        s_blk = _ln_silu_tile(v_blk, ex[0][...], ex[1][...]).astype(BF16)
        return s_blk, s_blk

    def residual_and_norm(acc, ex, rows):
        r_blk = ex[3][rows, :] + (acc + ex[2][...])
        return (r_blk, _rms(r_blk, ex[4][...])), ()

    r1, h1, s = _mm("l0_pw2", v, w2, grid=(t // tm, 1, 1), a_spec=row, b_spec=square, out_spec=[row, row, row],
                    out_shape=[jax.ShapeDtypeStruct((t, d), F32), jax.ShapeDtypeStruct((t, d), BF16),
                               jax.ShapeDtypeStruct((t, d), BF16)],
                    dims=NN, acc_shape=(tm, d), extras=(conv_ln_g, conv_ln_b, conv_b_pw2, x2, norm_ffn[0:1]),
                    extra_specs=(vec, vec, vec, row, vec), prologue=ln_silu, epilogue=residual_and_norm)

    def up_getter(name, idx):
        return lambda after: _exchange_wait(name, gather, [idx], after)[0].reshape(2 * dff, d)

    def down_getter(name, idx, forward=None):
        def get(after):
            if forward is not None:
                after = _exchange_forward(forward[0], gather, forward[1], after)
            return _exchange_wait(name, gather, [idx], after)[0].reshape(dff, d)
        return get

    r2, ffn0_saved = _ffn_forward("f0", r1, h1, up_getter("gather_wait_wu0", 3),
                                  down_getter("gather_wait_wd0", 4, ("gather_forward_wu1", [4, 5, 6])),
                                  fwdw[0], fbdw[0], seq)
    forwarded = _exchange_forward("gather_forward_wd1", gather, [7], r2)
    (wp,) = _exchange_wait("gather_wait_wp", gather, [5], forwarded)
    wp = wp.transpose(1, 0, 2, 3).reshape(N_GROUPS, cg, cg)
    pooled, r3, h3 = _pool_mix_fwd("l1_mix", r2, norm_mix[1:2], wp, pool_s_full, pool_b_full, norm_ffn[1:2], seq)
    (dr4, loss_part, dfinal), ffn1_saved = _ffn_forward(
        "f1", r3, h3, up_getter("gather_wait_wu1", 6), down_getter("gather_wait_wd1", 7), fwdw[1], fbdw[1], seq,
        loss=(tgt2, final_norm.reshape(1, d)))

    dr3, dnf1, dwu1, dwd1, dfw1, dfb1, _ = _ffn_backward("f1", dr4, ffn1_saved, norm_ffn[1:2], fwdw[1], fbdw[1], seq)
    scatter_a = _exchange_start("scatter_f1_start", [dwu1.reshape(N_DEV, fu, d), dwd1.reshape(N_DEV, fd, d)],
                                [SCATTER, SCATTER])
    dr2, dwp, dpool_s, dpool_b, dnm1 = _pool_mix_bwd(
        "l1_dmix", pooled, wp, dr3, r2, norm_mix[1:2], pool_s_full + scatter_a["token"][0:1, 0:1], pool_b_full, seq)
    dr1, dnf0, dwu0, dwd0, dfw0, dfb0, db2 = _ffn_backward("f0", dr2, ffn0_saved, norm_ffn[0:1], fwdw[0], fbdw[0], seq)
    dwp_b = dwp.astype(BF16).reshape(N_GROUPS, N_DEV, cgs, cg).transpose(1, 0, 2, 3)
    tk = _tile(t, 2048)
    dw2 = _mm("l0_dw2", s, dr1, grid=(1, 1, t // tk),
              a_spec=pl.BlockSpec((tk, d), lambda i, j, k: (k, 0)),
              b_spec=pl.BlockSpec((tk, d), lambda i, j, k: (k, 0)),
              out_spec=pl.BlockSpec((d, d), lambda i, j, k: (0, 0)),
              out_shape=jax.ShapeDtypeStruct((d, d), BF16), dims=TN, acc_shape=(d, d))
    scatter_b = _exchange_start("scatter_f0_start", [dwu0.reshape(N_DEV, fu, d), dwd0.reshape(N_DEV, fd, d), dwp_b,
                                                     dw2.reshape(N_DEV, d // N_DEV, d)], [SCATTER] * 4)

    def ln_silu_backward(acc, ex, rows):
        dv_blk, dgain, dbias, colsum = _ln_silu_bwd_tile(acc, ex[0][rows, :], ex[1][...], ex[2][...])
        return (dv_blk,), (dgain, dbias, colsum)

    dv, dlg, dlb, dbdw = _mm("l0_ds", dr1, w2, grid=(t // tm, 1, 1), a_spec=row, b_spec=square,
                             out_spec=[row, vec, vec, vec],
                             out_shape=[jax.ShapeDtypeStruct((t, d), F32)] + [jax.ShapeDtypeStruct((1, d), F32)] * 3,
                             dims=NT, acc_shape=(tm, d), extras=(v, conv_ln_g, conv_ln_b), extra_specs=(row, vec, vec),
                             epilogue=ln_silu_backward, n_sums=3, token=scatter_b["token"])
    da, dwdw, db1 = _conv_bwd("l0_dconv", a, dv, wdw, seq)
    tk1 = _tile(t, 4096)
    once = dict(pipeline_mode=pl.Buffered(1)) if tk1 == t else {}
    dw1 = _mm("l0_dw1", h0, da, grid=(1, 2, t // tk1),
              a_spec=pl.BlockSpec((tk1, d), lambda i, j, k: (k, 0), **once),
              b_spec=pl.BlockSpec((None, tk1, d), lambda i, j, k: (j, k, 0)),
              out_spec=pl.BlockSpec((d, d), lambda i, j, k: (0, j)),
              out_shape=jax.ShapeDtypeStruct((d, 2 * d), BF16), dims=TN, acc_shape=(d, d))
    scatter_c = _exchange_start("scatter_l0_start", [column_shards(dw1)], [SCATTER])
    def norm_backward(acc, ex, rows):
        dx_blk, dgain, colsum = _rms_bwd_tile(acc, ex[0][rows, :], ex[1][...], ex[2][rows, :])
        return (dx_blk,), (dgain, colsum)

    dx, dnm0, _ = _mm("l0_dh", da, w1, grid=(t // tm, 1, 1),
                      a_spec=pl.BlockSpec((2, tm, d), lambda i, j, k: (0, i, 0)),
                      b_spec=pl.BlockSpec((d, 2 * d), lambda i, j, k: (0, 0), pipeline_mode=pl.Buffered(1)),
                      out_spec=[row, vec, vec],
                      out_shape=[jax.ShapeDtypeStruct((t, d), F32)] + [jax.ShapeDtypeStruct((1, d), F32)] * 2,
                      dims=NT, acc_shape=(tm, d), extras=(x2, norm_mix[0:1], dr1), extra_specs=(row, vec, row),
                      epilogue=norm_backward, n_sums=2, parts=2, token=scatter_c["token"])

    dffn_w = jnp.stack([dfw0, dfw1])
    dffn_b = jnp.stack([dfb0, dfb1]).reshape(2, dff)
    small_parts = [loss_part, jnp.concatenate([dnm0, dnm1]), jnp.concatenate([dnf0, dnf1]), db1, dwdw, dbdw, dlg, dlb,
                   db2, dpool_b, dpool_s, dffn_w, dffn_b, dfinal]
    small_part_shapes = [(1,), (2, d), (2, d), (1, 2 * d), (k_taps, d), (1, d), (1, d), (1, d), (1, d), (1, d), (1, d),
                         (2, kf, dff), (2, dff), (d,)]
    packed = _pack(small_parts, 8 * LANE)
    gather_small = _exchange_start("gather_small_start", [packed], [GATHER])

    def big_update(name, recv, w, m, v, layer=0, prev=None):
        shape = w.shape
        c = recv.shape[-1]
        rows = recv.size // (N_DEV * c)
        nl = w.size // (rows * c)
        outs = _adamw(name, recv.reshape(N_DEV, rows, c), w.reshape(nl, rows, c), m.reshape(nl, rows, c),
                      v.reshape(nl, rows, c), layer, prev)
        return outs, [o.reshape(shape) for o in outs]

    wu_t = [p.transpose(0, 2, 1) for p in (ffn_w_up, m_ffn_w_up, v_ffn_w_up)]
    g_wu1, g_wd1 = _exchange_wait("scatter_f1_wait", scatter_a, [0, 1], gather_small["token"])
    raw_wu, _ = big_update("adam_wu1", g_wu1, *wu_t, 1)
    raw_wd, _ = big_update("adam_wd1", g_wd1, ffn_w_down, m_ffn_w_down, v_ffn_w_down, 1)
    g_wu0, g_wd0, g_wp, g_w2 = _exchange_wait("scatter_f0_wait", scatter_b, [0, 1, 2, 3], raw_wd[0])
    _, u_wu = big_update("adam_wu0", g_wu0, *wu_t, 0, raw_wu)
    u_wu = [o.transpose(0, 2, 1) for o in u_wu]
    _, u_wd = big_update("adam_wd0", g_wd0, ffn_w_down, m_ffn_w_down, v_ffn_w_down, 0, raw_wd)
    _, u_wp = big_update("adam_wp", g_wp, pool_w, m_pool_w, v_pool_w)
    (g_w1,) = _exchange_wait("scatter_l0_wait", scatter_c, [0], u_wp[0])
    _, u_w1 = big_update("adam_w1", g_w1, conv_w_pw1, m_conv_w_pw1, v_conv_w_pw1)
    _, u_w2 = big_update("adam_w2", g_w2, conv_w_pw2, m_conv_w_pw2, v_conv_w_pw2)
    (all_small,) = _exchange_wait("gather_small_wait", gather_small, [0], u_w2[0])
    summed = _sum_rows("sum_small_grads", all_small)
    (loss_v, g_nm, g_nf, g_b1, g_wdw, g_bdw, g_lg, g_lb, g_b2, g_pb, g_ps, g_fw, g_fb,
     g_fin) = _unpack(summed, small_part_shapes)
    loss = loss_v[0]
    g_wdw_mine = lax.dynamic_slice_in_dim(g_wdw, my * dsh, dsh, axis=1)[None]
    g_pb_mine = lax.dynamic_slice_in_dim(g_pb, my * dsh, dsh, axis=1)
    g_ps_mine = lax.dynamic_slice_in_dim(g_ps, my * dsh, dsh, axis=1)
    g_fw_mine = lax.dynamic_slice_in_dim(g_fw, my * fsh, fsh, axis=2)

    small_g =[g_nm, g_nf, g_b1, g_wdw_mine, g_bdw, g_lg, g_lb, g_b2, g_pb_mine, g_ps_mine, g_fw_mine, g_fb, g_fin]
    small_w = [norm_mix, norm_ffn, conv_b_pw1, conv_w_dw, conv_b_dw, conv_ln_g, conv_ln_b, conv_b_pw2, pool_b,
               pool_scale, ffn_w_dw, ffn_b_dw, final_norm]
    small_m = [m_norm_mix, m_norm_ffn, m_conv_b_pw1, m_conv_w_dw, m_conv_b_dw, m_conv_ln_g, m_conv_ln_b,
               m_conv_b_pw2, m_pool_b, m_pool_scale, m_ffn_w_dw, m_ffn_b_dw, m_final_norm]
    small_v = [v_norm_mix, v_norm_ffn, v_conv_b_pw1, v_conv_w_dw, v_conv_b_dw, v_conv_ln_g, v_conv_ln_b,
               v_conv_b_pw2, v_pool_b, v_pool_scale, v_ffn_w_dw, v_ffn_b_dw, v_final_norm]
    shapes = [w.shape for w in small_w]
    outs = _adamw("adam_small", _pack(small_g, 8 * LANE)[None], _pack(small_w, 8 * LANE)[None],
                  _pack(small_m, 8 * LANE)[None], _pack(small_v, 8 * LANE)[None])
    sg, sd, sm, sv = [_unpack(o, shapes) for o in outs]

    def leaf(kind):
        (nm, nf, b1, wdw_, bdw_, lg, lb, b2, pb, ps, fw, fb, fin) = (sg, sd, sm, sv)[kind]
        return [nm, nf, u_w1[kind], b1, wdw_, bdw_, lg, lb, u_w2[kind], b2, u_wp[kind], pb, ps, u_wu[kind], fw, fb,
                u_wd[kind], fin]

    return (loss, dx.reshape(bsz, seq, d), *leaf(0), *leaf(1), *leaf(2), *leaf(3))
```

```python
import functools

import jax
import jax.numpy as jnp
from jax import lax
from jax.experimental import pallas as pl
from jax.experimental.pallas import tpu as pltpu

F32 = jnp.float32
BF16 = jnp.bfloat16
MESH = pl.DeviceIdType.MESH
HBM = pl.BlockSpec(memory_space=pltpu.HBM)

N_DEV = 8
RMS_EPS = 1e-6
LN_EPS = 1e-5
POOL_WINDOWS = (2, 4, 8, 16)
N_GROUPS = len(POOL_WINDOWS)
ADAM_LR = 0.001
ADAM_B1 = 0.9
ADAM_B2 = 0.999
ADAM_EPS = 1e-08
ADAM_WD = 0.01
ADAM_STEP = 10

LANE = 128
HALO = 32
HALO16 = 16
VMEM_LIMIT = 56 * 1024 * 1024


def _params(*sem):
    return pltpu.CompilerParams(dimension_semantics=sem if sem else None, vmem_limit_bytes=VMEM_LIMIT)


def _tile(n, pref):
    for t in range(min(pref, n), 15, -1):
        if n % t == 0 and t % 16 == 0:
            return t
    return n


def _sigmoid(z):
    return 1.0 / (1.0 + jnp.exp(-z))


def _me():
    return lax.axis_index("x"), lax.axis_index("y"), lax.axis_index("c")


def _flip(pos, m):
    x, y, c = pos
    return ((1 - x) if m & 4 else x, (1 - y) if m & 2 else y, (1 - c) if m & 1 else c)


def _lin(pos):
    return 4 * pos[0] + 2 * pos[1] + pos[2]


SEM = pl.BlockSpec(memory_space=pltpu.SEMAPHORE)
ANY = pl.BlockSpec(memory_space=pl.ANY)
EFFECT = pltpu.SideEffectType.DATAFLOW_SIDE_EFFECTING


SCATTER = "scatter"
GATHER = "gather"
GATHER2 = "gather2"
ALL_MASKS = (1, 2, 3, 4, 5, 6, 7)
SIBLING = 1
CHIPS = (2, 4, 6)


class _Copies:
    def __init__(self, a, mode, src, land, send_sems, recv_sems):
        self.a, self.mode, self.src, self.land = a, mode, src, land
        self.send_sems, self.recv_sems = send_sems, recv_sems
        self.me = _me()
        self.first = (SIBLING,) + CHIPS if mode == GATHER2 else ALL_MASKS

    def _sems(self, m, to):
        return dict(send_sem=self.send_sems.at[self.a * N_DEV + m], recv_sem=self.recv_sems.at[self.a * N_DEV + m],
                    device_id=to, device_id_type=MESH)

    def _block(self, pid):
        return self.src.at[pid] if self.mode == SCATTER else self.src

    def local(self):
        my = _lin(self.me)
        return pltpu.make_async_copy(self._block(my), self.land.at[my], self.send_sems.at[self.a * N_DEV])

    def send(self, m):
        peer = _flip(self.me, m)
        return pltpu.make_async_remote_copy(src_ref=self._block(_lin(peer)), dst_ref=self.land.at[_lin(self.me)],
                                            **self._sems(m, peer))

    def arrival(self, m):
        rows = self.land.at[_lin(_flip(self.me, m))]
        return pltpu.make_async_remote_copy(src_ref=rows, dst_ref=rows, **self._sems(m, _flip(self.me, m)))

    def forward(self, m):
        rows = self.land.at[_lin(_flip(self.me, m))]
        return pltpu.make_async_remote_copy(src_ref=rows, dst_ref=rows, **self._sems(m | 1, _flip(self.me, SIBLING)))


def _exchange_start(name, arrs, modes):
    n = len(arrs)
    blocks = [a.shape[1:] if md == SCATTER else a.shape for a, md in zip(arrs, modes)]

    def body(*refs):
        srcs, lands = refs[:n], refs[n:2 * n]
        send_sems, recv_sems = refs[2 * n], refs[2 * n + 1]
        token = refs[-1]
        for a in range(n):
            cp = _Copies(a, modes[a], srcs[a], lands[a], send_sems, recv_sems)
            cp.local().start()
            for m in cp.first:
                cp.send(m).start()
        token[...] = jnp.zeros_like(token)

    lands = [lax.empty((N_DEV,) + tuple(b), a.dtype) for a, b in zip(arrs, blocks)]
    outs = pl.pallas_call(
        body, name=name,
        out_shape=(pltpu.SemaphoreType.DMA((n * N_DEV,)), pltpu.SemaphoreType.DMA((n * N_DEV,)),
                   *[pltpu.HBM(a.shape, a.dtype) for a in arrs], *[pltpu.HBM(l.shape, l.dtype) for l in lands],
                   jax.ShapeDtypeStruct((8, LANE), F32)),
        in_specs=[HBM] * (2 * n),
        out_specs=(SEM, SEM, *[HBM] * (2 * n), pl.BlockSpec(memory_space=pltpu.VMEM)),
        input_output_aliases={i: 2 + i for i in range(2 * n)},
        compiler_params=pltpu.CompilerParams(has_side_effects=EFFECT),
    )(*[pltpu.with_memory_space_constraint(a, pltpu.HBM) for a in arrs],
      *[pltpu.with_memory_space_constraint(l, pltpu.HBM) for l in lands])
    return dict(send=outs[0], recv=outs[1], srcs=list(outs[2:2 + n]), lands=list(outs[2 + n:2 + 2 * n]),
                modes=modes, token=outs[-1])


def _exchange_forward(name, handle, which, after):
    k = len(which)

    def half(wait):
        def body(*refs):
            lands = refs[:k]
            send_sems, recv_sems = refs[k], refs[k + 1]
            token = refs[-1]
            for pos, a in enumerate(which):
                cp = _Copies(a, GATHER2, None, lands[pos], send_sems, recv_sems)
                for m in CHIPS:
                    if wait:
                        cp.arrival(m).wait_recv()
                    else:
                        cp.forward(m).start()
            token[...] = jnp.zeros_like(token)
        return body

    def call(body, call_name, lands, after):
        outs = pl.pallas_call(
            body, name=call_name,
            out_shape=(*[pltpu.HBM(x.shape, x.dtype) for x in lands], jax.ShapeDtypeStruct((8, LANE), F32)),
            in_specs=[HBM] * k + [SEM, SEM, ANY], out_specs=(*[HBM] * k, pl.BlockSpec(memory_space=pltpu.VMEM)),
            input_output_aliases={i: i for i in range(k)},
            compiler_params=pltpu.CompilerParams(has_side_effects=EFFECT),
        )(*lands, handle["send"], handle["recv"], after)
        return list(outs[:k]), outs[-1]

    lands, arrived = call(half(True), name + "_arrived", [handle["lands"][a] for a in which], after)
    lands, token = call(half(False), name, lands, arrived)
    for pos, a in enumerate(which):
        handle["lands"][a] = lands[pos]
    return token


def _exchange_wait(name, handle, which, after):
    k = len(which)
    modes = handle["modes"]

    def body(*refs):
        srcs, lands = refs[:k], refs[k:2 * k]
        send_sems, recv_sems = refs[2 * k], refs[2 * k + 1]
        for pos, a in enumerate(which):
            cp = _Copies(a, modes[a], srcs[pos], lands[pos], send_sems, recv_sems)
            cp.local().wait()
            for m in cp.first:
                cp.send(m).wait_send()
            if modes[a] == GATHER2:
                for m in CHIPS:
                    cp.forward(m).wait_send()
                arrivals = (SIBLING,) + tuple(m | 1 for m in CHIPS)
            else:
                arrivals = ALL_MASKS
            for m in arrivals:
                cp.arrival(m).wait_recv()

    srcs = [handle["srcs"][a] for a in which]
    lands = [handle["lands"][a] for a in which]
    outs = pl.pallas_call(
        body, name=name,
        out_shape=tuple(pltpu.HBM(x.shape, x.dtype) for x in srcs + lands),
        in_specs=[HBM] * (2 * k) + [SEM, SEM, ANY], out_specs=tuple([HBM] * (2 * k)),
        input_output_aliases={i: i for i in range(2 * k)},
        compiler_params=pltpu.CompilerParams(has_side_effects=EFFECT),
    )(*srcs, *lands, handle["send"], handle["recv"], after)
    for pos, a in enumerate(which):
        handle["srcs"][a], handle["lands"][a] = outs[pos], outs[k + pos]
    return list(outs[k:])


def _mm(name, a, b, *, grid, a_spec, b_spec, out_spec, out_shape, dims, acc_shape, extras=(), extra_specs=(),
        epilogue=None, token=None, prologue=None, n_sums=0, parts=1):
    nk = grid[2]
    ne = len(extras)
    deps = () if token is None else (token,)
    dep_specs = [pl.BlockSpec((8, LANE), lambda i, j, k: (0, 0))] * len(deps)
    n_out = len(out_shape) if isinstance(out_shape, (list, tuple)) else 1
    n_tiles = n_out - n_sums - (1 if prologue is not None else 0)

    def body(a_ref, b_ref, *rest):
        ex, o_refs, acc_ref = rest[:ne], rest[ne + len(deps):ne + len(deps) + n_out], rest[ne + len(deps) + n_out]
        k = pl.program_id(2)
        if parts == 1:
            a_blk, saved = a_ref[...], None
            if prologue is not None:
                a_blk, saved = prologue(a_blk, ex)
                o_refs[n_tiles][...] = saved
            part = lax.dot_general(a_blk.astype(BF16), b_ref[...].astype(BF16), (dims, ((), ())),
                                   preferred_element_type=F32)
        else:
            kb = b_ref.shape[dims[1][0]] // parts
            part = None
            for p in range(parts):
                b_blk = b_ref[p * kb:(p + 1) * kb, :] if dims[1][0] == 0 else b_ref[:, p * kb:(p + 1) * kb]
                term = lax.dot_general(a_ref[p].astype(BF16), b_blk.astype(BF16), (dims, ((), ())),
                                       preferred_element_type=F32)
                part = term if part is None else part + term
        sum_refs = o_refs[n_out - n_sums:]

        def add_sums(terms):
            @pl.when((pl.program_id(0) == 0) & (pl.program_id(1) == 0))
            def _():
                for o_ref in sum_refs:
                    o_ref[...] = jnp.zeros_like(o_ref)

            for o_ref, term in zip(sum_refs, terms):
                o_ref[...] += jnp.sum(term, axis=0, keepdims=True)

        def finish(r):
            tiles, terms = ((r,), ()) if epilogue is None else epilogue(r, ex, slice(None))
            for o_ref, val in zip(o_refs, tiles):
                o_ref[...] = val.astype(o_ref.dtype)
            if n_sums:
                add_sums(terms)

        if nk == 1:
            finish(part)
            return

        @pl.when(k == 0)
        def _():
            acc_ref[...] = part

        @pl.when((k > 0) & (k < nk - 1))
        def _():
            acc_ref[...] += part

        @pl.when(k == nk - 1)
        def _():
            finish(acc_ref[...] + part)

    return pl.pallas_call(
        body, name=name, grid=grid, in_specs=[a_spec, b_spec, *extra_specs, *dep_specs], out_specs=out_spec,
        out_shape=out_shape, scratch_shapes=[pltpu.VMEM(acc_shape if nk > 1 else (8, LANE), F32)],
        compiler_params=_params(*(("arbitrary",) * 3 if n_sums else ("parallel", "parallel", "arbitrary"))),
    )(a, b, *extras, *deps)


def _rms(x, gain):
    return x * lax.rsqrt(jnp.mean(x * x, axis=-1, keepdims=True) + RMS_EPS) * gain


def _rms_bwd_tile(dh, x, gain, dres):
    rstd = lax.rsqrt(jnp.mean(x * x, axis=-1, keepdims=True) + RMS_EPS)
    xhat = x * rstd
    dxhat = dh * gain
    dx = dres + rstd * (dxhat - xhat * jnp.mean(dxhat * xhat, axis=-1, keepdims=True))
    return dx, dh * xhat, dx


def _ln_silu_tile(v, g, b):
    mu = jnp.mean(v, axis=-1, keepdims=True)
    cen = v - mu
    z = cen * lax.rsqrt(jnp.mean(cen * cen, axis=-1, keepdims=True) + LN_EPS) * g + b
    return z * _sigmoid(z)


def _ln_silu_bwd_tile(ds, v, g, b):
    mu = jnp.mean(v, axis=-1, keepdims=True)
    cen = v - mu
    rstd = lax.rsqrt(jnp.mean(cen * cen, axis=-1, keepdims=True) + LN_EPS)
    y = cen * rstd
    z = y * g + b
    sig = _sigmoid(z)
    dz = ds * sig * (1.0 + z * (1.0 - sig))
    dy = dz * g
    dv = rstd * (dy - jnp.mean(dy, axis=-1, keepdims=True) - y * jnp.mean(dy * y, axis=-1, keepdims=True))
    return dv, dz * y, dz, dv


def _loss_tile(x, tgt, gain):
    d = x.shape[-1]
    rstd = lax.rsqrt(jnp.mean(x * x, axis=-1, keepdims=True) + RMS_EPS)
    xhat = x * rstd
    err = xhat * gain - tgt
    dy = err / d
    dxhat = dy * gain
    dx = rstd * (dxhat - xhat * jnp.mean(dxhat * xhat, axis=-1, keepdims=True))
    return dx, 0.5 * jnp.mean(err * err, axis=-1, keepdims=True), dy * xhat


NN = ((1,), (0,))
NT = ((1,), (1,))
TN = ((0,), (0,))


def _rms_fwd(name, x, gain):
    t, d = x.shape
    tr = _tile(t, 512)

    def body(x_ref, g_ref, h_ref):
        h_ref[...] = _rms(x_ref[...], g_ref[...]).astype(BF16)

    return pl.pallas_call(
        body, name=name, grid=(t // tr,),
        in_specs=[pl.BlockSpec((tr, d), lambda i: (i, 0)), pl.BlockSpec((1, d), lambda i: (0, 0))],
        out_specs=pl.BlockSpec((tr, d), lambda i: (i, 0)),
        out_shape=jax.ShapeDtypeStruct((t, d), BF16), compiler_params=_params("parallel"),
    )(x, gain)


def _conv_tiles(t, seq):
    ts = _tile(seq, 1024)
    return ts, seq // ts, _tile(ts, 64)


def _conv_fwd(name, a, w, b, seq):
    _, t, d = a.shape
    k_taps = w.shape[0]
    ts, tps, rc = _conv_tiles(t, seq)
    hb = ts // HALO

    def body(cur_ref, prev_ref, w_ref, b_ref, v_ref, upad):
        i = pl.program_id(1)
        first = (i % tps) == 0
        pv = prev_ref[0].astype(F32)
        pg = prev_ref[1].astype(F32)
        upad[0:HALO, :] = jnp.where(first, 0.0, pv * _sigmoid(pg))
        upad[HALO:HALO + ts, :] = cur_ref[0].astype(F32) * _sigmoid(cur_ref[1].astype(F32))
        wv = w_ref[...]
        bias = jnp.broadcast_to(b_ref[...], (rc, LANE))
        for r0 in range(0, ts, rc):
            acc = bias
            for k in range(k_taps):
                acc = acc + wv[k:k + 1, :] * upad[pl.ds(HALO - (k_taps - 1) + k + r0, rc), :]
            v_ref[pl.ds(r0, rc), :] = acc

    return pl.pallas_call(
        body, name=name, grid=(d // LANE, t // ts),
        in_specs=[pl.BlockSpec((2, ts, LANE), lambda c, i: (0, i, c)),
                  pl.BlockSpec((2, HALO, LANE), lambda c, i: (0, jnp.maximum(i * hb - 1, 0), c)),
                  pl.BlockSpec((k_taps, LANE), lambda c, i: (0, c)),
                  pl.BlockSpec((1, LANE), lambda c, i: (0, c))],
        out_specs=pl.BlockSpec((ts, LANE), lambda c, i: (i, c)),
        out_shape=jax.ShapeDtypeStruct((t, d), F32),
        scratch_shapes=[pltpu.VMEM((HALO + ts, LANE), F32)],
        compiler_params=_params("parallel", "parallel"),
    )(a, a, w, b)


def _conv_bwd(name, a, dv, w, seq):
    _, t, d = a.shape
    k_taps = w.shape[0]
    ts, tps, rc = _conv_tiles(t, seq)
    hb = ts // HALO
    nhb = t // HALO

    def body(cur_ref, prev_ref, dv_ref, ndv_ref, w_ref, da_ref, dw_ref, dbp_ref, upad, dvpad, dwrows):
        i = pl.program_id(1)
        first = (i % tps) == 0
        last = (i % tps) == tps - 1
        pv = prev_ref[0].astype(F32)
        pg = prev_ref[1].astype(F32)
        upad[0:HALO, :] = jnp.where(first, 0.0, pv * _sigmoid(pg))
        upad[HALO:HALO + ts, :] = cur_ref[0].astype(F32) * _sigmoid(cur_ref[1].astype(F32))
        dvpad[0:ts, :] = dv_ref[...]
        dvpad[ts:ts + HALO, :] = jnp.where(last, 0.0, ndv_ref[...])
        wv = w_ref[...]

        @pl.when(i == 0)
        def _():
            dw_ref[...] = jnp.zeros_like(dw_ref)
            dbp_ref[...] = jnp.zeros_like(dbp_ref)

        sv = jnp.zeros((1, LANE), F32)
        sg = jnp.zeros((1, LANE), F32)
        for r0 in range(0, ts, rc):
            du = jnp.zeros((rc, LANE), F32)
            for k in range(k_taps):
                du = du + wv[k:k + 1, :] * dvpad[pl.ds(r0 + (k_taps - 1) - k, rc), :]
            av = cur_ref[0, pl.ds(r0, rc), :].astype(F32)
            sig = _sigmoid(cur_ref[1, pl.ds(r0, rc), :].astype(F32))
            dval = du * sig
            dgate = du * av * sig * (1.0 - sig)
            da_ref[0, pl.ds(r0, rc), :] = dval.astype(BF16)
            da_ref[1, pl.ds(r0, rc), :] = dgate.astype(BF16)
            sv = sv + jnp.sum(dval, axis=0, keepdims=True)
            sg = sg + jnp.sum(dgate, axis=0, keepdims=True)
        dbp_ref[0] += sv
        dbp_ref[1] += sg

        for k in range(k_taps):
            acc = jnp.zeros((rc, LANE), F32)
            for r0 in range(0, ts, rc):
                acc = acc + dvpad[pl.ds(r0, rc), :] * upad[pl.ds(HALO - (k_taps - 1) + k + r0, rc), :]
            dwrows[k:k + 1, :] = jnp.sum(acc, axis=0, keepdims=True)
        dw_ref[...] += dwrows[0:k_taps, :]

    return pl.pallas_call(
        body, name=name, grid=(d // LANE, t // ts),
        in_specs=[pl.BlockSpec((2, ts, LANE), lambda c, i: (0, i, c)),
                  pl.BlockSpec((2, HALO, LANE), lambda c, i: (0, jnp.maximum(i * hb - 1, 0), c)),
                  pl.BlockSpec((ts, LANE), lambda c, i: (i, c)),
                  pl.BlockSpec((HALO, LANE), lambda c, i: (jnp.minimum((i + 1) * hb, nhb - 1), c)),
                  pl.BlockSpec((k_taps, LANE), lambda c, i: (0, c))],
        out_specs=[pl.BlockSpec((2, ts, LANE), lambda c, i: (0, i, c)),
                   pl.BlockSpec((k_taps, LANE), lambda c, i: (0, c)),
                   pl.BlockSpec((2, 1, LANE), lambda c, i: (0, 0, c))],
        out_shape=[jax.ShapeDtypeStruct((2, t, d), BF16), jax.ShapeDtypeStruct((k_taps, d), F32),
                   jax.ShapeDtypeStruct((2, 1, d), F32)],
        scratch_shapes=[pltpu.VMEM((HALO + ts, LANE), F32), pltpu.VMEM((ts + HALO, LANE), F32),
                        pltpu.VMEM((HALO, LANE), F32)],
        compiler_params=_params("parallel", "arbitrary"),
    )(a, a, dv, dv, w)


def _pool_mix_fwd(name, x, gain, wp, scale, bias, next_gain, seq):
    t, d = x.shape
    ts = _tile(seq, 256)
    tps = seq // ts
    hb = ts // HALO
    cg = d // N_GROUPS

    def body(cur_ref, prev_ref, g_ref, w_ref, s_ref, b_ref, ng_ref, p_ref, r_ref, h_ref, hpad):
        i = pl.program_id(0)
        first = (i % tps) == 0
        g = g_ref[...]
        hpad[0:HALO, :] = jnp.where(first, 0.0, _rms(prev_ref[...], g))
        hpad[HALO:HALO + ts, :] = _rms(cur_ref[...], g)
        pos = (i % tps) * ts + lax.broadcasted_iota(jnp.int32, (ts, 1), 0)
        for gi, win in enumerate(POOL_WINDOWS):
            sl = slice(gi * cg, (gi + 1) * cg)
            own = hpad[HALO:HALO + ts, sl]
            acc = own
            for j in range(1, win):
                acc = acc + hpad[HALO - j:HALO - j + ts, sl]
            cnt = jnp.minimum(pos + 1, win).astype(F32)
            pooled = (acc / cnt - own).astype(BF16)
            p_ref[:, sl] = pooled
            mixed = jnp.dot(pooled, w_ref[gi], preferred_element_type=F32)
            r_ref[:, sl] = cur_ref[:, sl] + s_ref[:, sl] * (mixed + b_ref[:, sl])
        h_ref[...] = _rms(r_ref[...], ng_ref[...]).astype(BF16)

    row = pl.BlockSpec((ts, d), lambda i: (i, 0))
    vec = pl.BlockSpec((1, d), lambda i: (0, 0))
    return pl.pallas_call(
        body, name=name, grid=(t // ts,),
        in_specs=[row, pl.BlockSpec((HALO, d), lambda i: (jnp.maximum(i * hb - 1, 0), 0)), vec,
                  pl.BlockSpec((N_GROUPS, cg, cg), lambda i: (0, 0, 0)), vec, vec, vec],
        out_specs=[row, row, row],
        out_shape=[jax.ShapeDtypeStruct((t, d), BF16), jax.ShapeDtypeStruct((t, d), F32),
                   jax.ShapeDtypeStruct((t, d), BF16)],
        scratch_shapes=[pltpu.VMEM((HALO + ts, d), F32)],
        compiler_params=_params("parallel"),
    )(x, x, gain, wp, scale, bias, next_gain)


def _pool_mix_bwd(name, pooled, wp, dr, x, gain, scale, bias, seq):
    t, d = x.shape
    ts = _tile(seq, 256)
    tps = seq // ts
    hb = ts // HALO
    nhb = t // HALO
    cg = d // N_GROUPS

    def body(p_ref, w_ref, dr_ref, ndr_ref, x_ref, g_ref, s_ref, b_ref, dx_ref, dw_ref, ds_ref, db_ref, dg_ref,
             qpad, dh):
        i = pl.program_id(0)
        last = (i % tps) == tps - 1
        pos = (i % tps) * ts + lax.broadcasted_iota(jnp.int32, (ts, 1), 0)

        @pl.when(i == 0)
        def _():
            dw_ref[...] = jnp.zeros_like(dw_ref)
            ds_ref[...] = jnp.zeros_like(ds_ref)
            db_ref[...] = jnp.zeros_like(db_ref)
            dg_ref[...] = jnp.zeros_like(dg_ref)

        for gi, win in enumerate(POOL_WINDOWS):
            sl = slice(gi * cg, (gi + 1) * cg)
            wv = w_ref[gi]
            sc = s_ref[:, sl]
            drv = dr_ref[:, sl]
            dmx = drv * sc
            dmx16 = dmx.astype(BF16)
            pooled = p_ref[:, sl]
            dw_ref[gi] += lax.dot_general(pooled, dmx16, (TN, ((), ())), preferred_element_type=F32)
            mixed = jnp.dot(pooled, wv, preferred_element_type=F32)
            ds_ref[:, sl] += jnp.sum(drv * (mixed + b_ref[:, sl]), axis=0, keepdims=True)
            db_ref[:, sl] += jnp.sum(dmx, axis=0, keepdims=True)
            cur = lax.dot_general(dmx16, wv, (NT, ((), ())), preferred_element_type=F32)
            nxt = lax.dot_general((ndr_ref[:, sl] * sc).astype(BF16), wv, (NT, ((), ())),
                                  preferred_element_type=F32)
            qpad[0:ts, sl] = cur / jnp.minimum(pos + 1, win).astype(F32)
            qpad[ts:ts + HALO, sl] = jnp.where(last, 0.0, nxt / float(win))
            acc = -cur
            for j in range(win):
                acc = acc + qpad[j:j + ts, sl]
            dh[:, sl] = acc
        dx, dgain_term, _ = _rms_bwd_tile(dh[...], x_ref[...], g_ref[...], dr_ref[...])
        dx_ref[...] = dx
        dg_ref[...] += jnp.sum(dgain_term, axis=0, keepdims=True)

    row = pl.BlockSpec((ts, d), lambda i: (i, 0))
    vec = pl.BlockSpec((1, d), lambda i: (0, 0))
    return pl.pallas_call(
        body, name=name, grid=(t // ts,),
        in_specs=[row, pl.BlockSpec((N_GROUPS, cg, cg), lambda i: (0, 0, 0)), row,
                  pl.BlockSpec((HALO, d), lambda i: (jnp.minimum((i + 1) * hb, nhb - 1), 0)), row, vec, vec, vec],
        out_specs=[row, pl.BlockSpec((N_GROUPS, cg, cg), lambda i: (0, 0, 0)), vec, vec, vec],
        out_shape=[jax.ShapeDtypeStruct((t, d), F32), jax.ShapeDtypeStruct((N_GROUPS, cg, cg), F32)]
        + [jax.ShapeDtypeStruct((1, d), F32)] * 3,
        scratch_shapes=[pltpu.VMEM((ts + HALO, d), F32), pltpu.VMEM((ts, d), F32)],
        compiler_params=_params("arbitrary"),
    )(pooled, wp, dr, dr, x, gain, scale, bias)


def _ctile(n, pref):
    return max(c for c in range(LANE, min(pref, n) + 1, LANE) if n % c == 0)


FFN_COLS = 1408
FFN_ROWS = 32
FFN_TILE = 512


def _ffn_fwd(name, up, w, b, seq):
    _, t, dff = up.shape
    f = _ctile(dff, FFN_COLS)
    k_taps = w.shape[0]
    ts = _tile(seq, FFN_TILE)
    tps = seq // ts
    hb = ts // HALO16
    rc = _tile(ts, FFN_ROWS)

    def body(cur_ref, prev_ref, w_ref, b_ref, g_ref, apad):
        i = pl.program_id(1)
        first = (i % tps) == 0
        for ci, c0 in enumerate(range(0, f, LANE)):
            cols = slice(c0, c0 + LANE)
            apad[ci, 0:HALO16, :] = jnp.where(first, 0.0, prev_ref[:, cols].astype(F32))
            apad[ci, HALO16:HALO16 + ts, :] = cur_ref[0, :, cols].astype(F32)
            wv = w_ref[:, cols]
            wk = [jnp.broadcast_to(wv[k:k + 1, :], (rc, LANE)) for k in range(k_taps)]
            bias = jnp.broadcast_to(b_ref[:, cols], (rc, LANE))
            for r0 in range(0, ts, rc):
                c = bias
                for k in range(k_taps):
                    c = c + wk[k] * apad[ci, pl.ds(HALO16 - (k_taps - 1) + k + r0, rc), :]
                gate = cur_ref[1, pl.ds(r0, rc), cols].astype(F32)
                g_ref[pl.ds(r0, rc), cols] = (c * _sigmoid(c) * gate).astype(BF16)

    return pl.pallas_call(
        body, name=name, grid=(dff // f, t // ts),
        in_specs=[pl.BlockSpec((2, ts, f), lambda j, i: (0, i, j)),
                  pl.BlockSpec((None, HALO16, f), lambda j, i: (0, jnp.maximum(i * hb - 1, 0), j)),
                  pl.BlockSpec((k_taps, f), lambda j, i: (0, j)),
                  pl.BlockSpec((1, f), lambda j, i: (0, j))],
        out_specs=pl.BlockSpec((ts, f), lambda j, i: (i, j)),
        out_shape=jax.ShapeDtypeStruct((t, dff), BF16),
        scratch_shapes=[pltpu.VMEM((f // LANE, HALO16 + ts, LANE), F32)],
        compiler_params=_params("parallel", "parallel"),
    )(up, up, w, b)


def _ffn_bwd(name, up, dg, w, b, seq):
    _, t, dff = up.shape
    f = _ctile(dff, FFN_COLS)
    k_taps = w.shape[0]
    ts = _tile(seq, FFN_TILE)
    tps = seq // ts
    hb = ts // HALO16
    nhb = t // HALO16
    ext = ts + HALO16
    rc = _tile(ts, FFN_ROWS)

    def body(cur_ref, prev_ref, next_ref, dg_ref, ndg_ref, w_ref, b_ref, dup_ref, dw_ref, db_ref, apad, dcpad):
        i = pl.program_id(1)
        first = (i % tps) == 0
        last = (i % tps) == tps - 1

        @pl.when(i == 0)
        def _():
            dw_ref[...] = jnp.zeros_like(dw_ref)
            db_ref[...] = jnp.zeros_like(db_ref)

        for ci, c0 in enumerate(range(0, f, LANE)):
            cols = slice(c0, c0 + LANE)
            apad[ci, 0:HALO16, :] = jnp.where(first, 0.0, prev_ref[:, cols].astype(F32))
            apad[ci, HALO16:HALO16 + ts, :] = cur_ref[0, :, cols].astype(F32)
            apad[ci, HALO16 + ts:HALO16 + ext, :] = next_ref[0, :, cols].astype(F32)
            wv = w_ref[:, cols]
            wk = [jnp.broadcast_to(wv[k:k + 1, :], (rc, LANE)) for k in range(k_taps)]
            bias = jnp.broadcast_to(b_ref[:, cols], (rc, LANE))

            def conv_grad(r0, n, gate, dgv):
                c = bias[0:n]
                for k in range(k_taps):
                    c = c + wk[k][0:n] * apad[ci, pl.ds(HALO16 - (k_taps - 1) + k + r0, n), :]
                sig = _sigmoid(c)
                return dgv * gate * sig * (1.0 + c * (1.0 - sig)), c * sig

            for r0 in range(0, ts, rc):
                dgv = dg_ref[pl.ds(r0, rc), cols].astype(F32)
                dc, silu = conv_grad(r0, rc, cur_ref[1, pl.ds(r0, rc), cols].astype(F32), dgv)
                dcpad[ci, pl.ds(r0, rc), :] = dc
                dup_ref[1, pl.ds(r0, rc), cols] = (dgv * silu).astype(BF16)
            dgv = jnp.where(last, 0.0, ndg_ref[:, cols].astype(F32))
            dc, _ = conv_grad(ts, HALO16, next_ref[1, :, cols].astype(F32), dgv)
            dcpad[ci, ts:ext, :] = dc

            dw_acc = [jnp.zeros((rc, LANE), F32) for _ in range(k_taps)]
            db_acc = jnp.zeros((rc, LANE), F32)
            for r0 in range(0, ts, rc):
                dact = jnp.zeros((rc, LANE), F32)
                for k in range(k_taps):
                    dact = dact + wk[k] * dcpad[ci, pl.ds(r0 + (k_taps - 1) - k, rc), :]
                dup_ref[0, pl.ds(r0, rc), cols] = dact.astype(BF16)
                dc = dcpad[ci, pl.ds(r0, rc), :]
                for k in range(k_taps):
                    dw_acc[k] = dw_acc[k] + dc * apad[ci, pl.ds(HALO16 - (k_taps - 1) + k + r0, rc), :]
                db_acc = db_acc + dc
            for k in range(k_taps):
                dw_ref[k:k + 1, cols] += jnp.sum(dw_acc[k], axis=0, keepdims=True)
            db_ref[:, cols] += jnp.sum(db_acc, axis=0, keepdims=True)

    return pl.pallas_call(
        body, name=name, grid=(dff // f, t // ts),
        in_specs=[pl.BlockSpec((2, ts, f), lambda j, i: (0, i, j)),
                  pl.BlockSpec((None, HALO16, f), lambda j, i: (0, jnp.maximum(i * hb - 1, 0), j)),
                  pl.BlockSpec((2, HALO16, f), lambda j, i: (0, jnp.minimum((i + 1) * hb, nhb - 1), j)),
                  pl.BlockSpec((ts, f), lambda j, i: (i, j)),
                  pl.BlockSpec((HALO16, f), lambda j, i: (jnp.minimum((i + 1) * hb, nhb - 1), j)),
                  pl.BlockSpec((k_taps, f), lambda j, i: (0, j)),
                  pl.BlockSpec((1, f), lambda j, i: (0, j))],
        out_specs=[pl.BlockSpec((2, ts, f), lambda j, i: (0, i, j)),
                   pl.BlockSpec((k_taps, f), lambda j, i: (0, j)),
                   pl.BlockSpec((1, f), lambda j, i: (0, j))],
        out_shape=[jax.ShapeDtypeStruct((2, t, dff), BF16), jax.ShapeDtypeStruct((k_taps, dff), F32),
                   jax.ShapeDtypeStruct((1, dff), F32)],
        scratch_shapes=[pltpu.VMEM((f // LANE, HALO16 + ext, LANE), F32), pltpu.VMEM((f // LANE, ext, LANE), F32)],
        compiler_params=_params("parallel", "arbitrary"),
    )(up, up, up, dg, dg, w, b)


def _sum_rows(name, g):
    ns, r, c = g.shape
    tr = _tile(r, 256)

    def body(g_ref, o_ref):
        acc = g_ref[0]
        for dev in range(1, ns):
            acc = acc + g_ref[dev]
        o_ref[...] = acc

    return pl.pallas_call(
        body, name=name, grid=(r // tr,),
        in_specs=[pl.BlockSpec((ns, tr, c), lambda i: (0, i, 0))],
        out_specs=pl.BlockSpec((tr, c), lambda i: (i, 0)),
        out_shape=jax.ShapeDtypeStruct((r, c), F32), compiler_params=_params("parallel"),
    )(g)


def _adamw(name, gsrc, w, m, v, layer=0, prev=None):
    ns, r, c = gsrc.shape
    nl = w.shape[0]
    tr = _tile(r, 256)
    prev = () if prev is None else tuple(prev)

    def body(g_ref, w_ref, m_ref, v_ref, *rest):
        go_ref, do_ref, mo_ref, vo_ref = rest[len(prev):]
        g = g_ref[0].astype(F32)
        for dev in range(1, ns):
            g = g + g_ref[dev].astype(F32)
        m_new = ADAM_B1 * m_ref[...] + (1.0 - ADAM_B1) * g
        v_new = ADAM_B2 * v_ref[...] + (1.0 - ADAM_B2) * (g * g)
        m_hat = m_new / (1.0 - ADAM_B1 ** ADAM_STEP)
        v_hat = v_new / (1.0 - ADAM_B2 ** ADAM_STEP)
        go_ref[...] = g
        do_ref[...] = -ADAM_LR * (m_hat / (jnp.sqrt(v_hat) + ADAM_EPS) + ADAM_WD * w_ref[...])
        mo_ref[...] = m_new
        vo_ref[...] = v_new

    row = pl.BlockSpec((None, tr, c), lambda i: (layer, i, 0))
    return pl.pallas_call(
        body, name=name, grid=(r // tr,),
        in_specs=[pl.BlockSpec((ns, tr, c), lambda i: (0, i, 0)), row, row, row] + [ANY] * len(prev),
        out_specs=[row] * 4, out_shape=[jax.ShapeDtypeStruct((nl, r, c), F32)] * 4,
        input_output_aliases={4 + i: i for i in range(len(prev))},
        compiler_params=_params("parallel"),
    )(gsrc, w, m, v, *prev)


def _ffn_forward(tag, r_in, h, get_wu, get_wd, wdw, bdw, seq, loss=None):
    t, d = r_in.shape
    tm = _tile(t, 512)
    wu = get_wu(h)
    dff = wu.shape[0] // 2
    tu = _tile(t, 1024)
    up = _mm(f"{tag}_up", h, wu, grid=(2, t // tu, 1),
             a_spec=pl.BlockSpec((tu, d), lambda j, i, k: (i, 0)),
             b_spec=pl.BlockSpec((dff, d), lambda j, i, k: (j, 0)),
             out_spec=pl.BlockSpec((None, tu, dff), lambda j, i, k: (j, i, 0)),
             out_shape=jax.ShapeDtypeStruct((2, t, dff), BF16), dims=NT, acc_shape=(tu, dff))
    wd = get_wd(up)
    g = _ffn_fwd(f"{tag}_act", up, wdw, bdw, seq)
    row = pl.BlockSpec((tm, d), lambda i, j, k: (i, 0))
    vec = pl.BlockSpec((1, d), lambda i, j, k: (0, 0))
    common = dict(grid=(t // tm, 1, 1), a_spec=pl.BlockSpec((tm, dff), lambda i, j, k: (i, 0)),
                  b_spec=pl.BlockSpec((dff, d), lambda i, j, k: (0, 0)), dims=NN, acc_shape=(tm, d))
    if loss is None:
        out = _mm(f"{tag}_down", g, wd, out_spec=row, out_shape=jax.ShapeDtypeStruct((t, d), F32),
                  extras=(r_in,), extra_specs=(row,), epilogue=lambda acc, ex, rows: ((ex[0][rows, :] + acc,), ()),
                  **common)
    else:
        def head(acc, ex, rows):
            dx, part, dgain = _loss_tile(ex[0][rows, :] + acc, ex[1][rows, :], ex[2][...])
            return (dx,), (part, dgain)

        out = _mm(f"{tag}_down", g, wd, out_spec=[row, pl.BlockSpec((1, 1), lambda i, j, k: (0, 0)), vec],
                  out_shape=[jax.ShapeDtypeStruct((t, d), F32), jax.ShapeDtypeStruct((1, 1), F32),
                             jax.ShapeDtypeStruct((1, d), F32)],
                  extras=(r_in, *loss), extra_specs=(row, row, vec), epilogue=head, n_sums=2, **common)
    return out, (r_in, h, up, g, wu, wd)


def _ffn_backward(tag, dr, saved, gain, wdw, bdw, seq, token=None):
    r_in, h, up, g, wu, wd = saved
    t, d = r_in.shape
    dff = wd.shape[0]
    tm = _tile(t, 512)
    tk = _tile(t, 2048)
    tku = _tile(t, 4096)
    cw = _ctile(dff, 1408)
    nc = dff // cw
    once = dict(pipeline_mode=pl.Buffered(1)) if tku == t else {}
    dg = _mm(f"{tag}_dg", dr, wd, grid=(t // tm, 1, 1),
             a_spec=pl.BlockSpec((tm, d), lambda i, j, k: (i, 0)),
             b_spec=pl.BlockSpec((dff, d), lambda i, j, k: (0, 0)),
             out_spec=pl.BlockSpec((tm, dff), lambda i, j, k: (i, 0)),
             out_shape=jax.ShapeDtypeStruct((t, dff), BF16), dims=NT, acc_shape=(tm, dff), token=token)
    dwd = _mm(f"{tag}_dwd", g, dr, grid=(dff // cw, 1, t // tk),
              a_spec=pl.BlockSpec((tk, cw), lambda i, j, k: (k, i)),
              b_spec=pl.BlockSpec((tk, d), lambda i, j, k: (k, 0)),
              out_spec=pl.BlockSpec((cw, d), lambda i, j, k: (i, 0)),
              out_shape=jax.ShapeDtypeStruct((dff, d), BF16), dims=TN, acc_shape=(cw, d))
    dup, dwdw, dbdw = _ffn_bwd(f"{tag}_dact", up, dg, wdw, bdw, seq)
    row = pl.BlockSpec((tm, d), lambda i, j, k: (i, 0))
    vec = pl.BlockSpec((1, d), lambda i, j, k: (0, 0))

    def norm_backward(acc, ex, rows):
        dx, dgain, colsum = _rms_bwd_tile(acc, ex[0][rows, :], ex[1][...], ex[2][rows, :])
        return (dx,), (dgain, colsum)

    dr_in, dgain, colsum = _mm(
        f"{tag}_dh", dup, wu, grid=(t // tm, 1, 1),
        a_spec=pl.BlockSpec((2, tm, dff), lambda i, j, k: (0, i, 0)),
        b_spec=pl.BlockSpec((2 * dff, d), lambda i, j, k: (0, 0), pipeline_mode=pl.Buffered(1)),
        out_spec=[row, vec, vec],
        out_shape=[jax.ShapeDtypeStruct((t, d), F32)] + [jax.ShapeDtypeStruct((1, d), F32)] * 2,
        dims=NN, acc_shape=(tm, d), extras=(r_in, gain, dr), extra_specs=(row, vec, row),
        epilogue=norm_backward, n_sums=2, parts=2)
    dwu = _mm(f"{tag}_dwu", dup, h, grid=(2 * nc, 1, t // tku),
              a_spec=pl.BlockSpec((None, tku, cw), lambda i, j, k: (i // nc, k, i % nc)),
              b_spec=pl.BlockSpec((tku, d), lambda i, j, k: (k, 0), **once),
              out_spec=pl.BlockSpec((cw, d), lambda i, j, k: (i, 0)),
              out_shape=jax.ShapeDtypeStruct((2 * dff, d), BF16), dims=TN, acc_shape=(cw, d))
    return dr_in, dgain, dwu, dwd, dwdw, dbdw, colsum


def _pad_to(vec, n):
    return jnp.pad(vec, (0, n - vec.shape[0]))


def _pack(parts, width):
    flat = jnp.concatenate([p.reshape(-1).astype(F32) for p in parts])
    n = -(-flat.shape[0] // (8 * width)) * (8 * width)
    return _pad_to(flat, n).reshape(n // width, width)


def _unpack(mat, shapes):
    flat = mat.reshape(-1)
    out, off = [], 0
    for s in shapes:
        n = 1
        for dim in s:
            n *= dim
        out.append(flat[off:off + n].reshape(s))
        off += n
    return out


def kernel(x, norm_mix, norm_ffn, conv_w_pw1, conv_b_pw1, conv_w_dw, conv_b_dw, conv_ln_g, conv_ln_b, conv_w_pw2, conv_b_pw2, pool_w, pool_b, pool_scale, ffn_w_up, ffn_w_dw, ffn_b_dw, ffn_w_down, final_norm, loss_target, m_norm_mix, m_norm_ffn, m_conv_w_pw1, m_conv_b_pw1, m_conv_w_dw, m_conv_b_dw, m_conv_ln_g, m_conv_ln_b, m_conv_w_pw2, m_conv_b_pw2, m_pool_w, m_pool_b, m_pool_scale, m_ffn_w_up, m_ffn_w_dw, m_ffn_b_dw, m_ffn_w_down, m_final_norm, v_norm_mix, v_norm_ffn, v_conv_w_pw1, v_conv_b_pw1, v_conv_w_dw, v_conv_b_dw, v_conv_ln_g, v_conv_ln_b, v_conv_w_pw2, v_conv_b_pw2, v_pool_w, v_pool_b, v_pool_scale, v_ffn_w_up, v_ffn_w_dw, v_ffn_b_dw, v_ffn_w_down, v_final_norm):
    bsz, seq, d = x.shape
    t = bsz * seq
    k_taps = conv_w_dw.shape[1]
    cs1 = conv_w_pw1.shape[2]
    dsh = d // N_DEV
    cg = d // N_GROUPS
    cgs = pool_w.shape[2]
    fu = ffn_w_up.shape[2]
    fd = ffn_w_down.shape[1]
    dff = fd * N_DEV
    nb = N_DEV // 2
    kf = ffn_w_dw.shape[1]
    fsh = ffn_w_dw.shape[2]
    my = _lin(_me())
    tm = _tile(t, 512)

    x2 = x.reshape(t, d)
    tgt2 = loss_target.reshape(t, d)

    small_shapes = [(k_taps, dsh), (dsh,), (dsh,), (2, kf, fsh)]
    small_mine = _pack([conv_w_dw[0], pool_b[0], pool_scale[0], ffn_w_dw], LANE)
    big = [conv_w_pw1[0], conv_w_pw2[0], ffn_w_up[0].T, ffn_w_down[0], pool_w[0], ffn_w_up[1].T, ffn_w_down[1]]
    gather = _exchange_start("gather_start", [small_mine] + [w.astype(BF16) for w in big], [GATHER2] * 8)
    h0 = _rms_fwd("l0_rms", x2, norm_mix[0:1])
    forwarded = _exchange_forward("gather_forward_w1", gather, [0, 1], h0)
    small_all, w1 = _exchange_wait("gather_wait_w1", gather, [0, 1], forwarded)
    parts = [_unpack(small_all[dev], small_shapes) for dev in range(N_DEV)]
    wdw = jnp.concatenate([p[0] for p in parts], axis=1)
    pool_b_full = jnp.concatenate([p[1] for p in parts]).reshape(1, d)
    pool_s_full = jnp.concatenate([p[2] for p in parts]).reshape(1, d)
    fwdw = jnp.concatenate([p[3] for p in parts], axis=2)
    fbdw = ffn_b_dw.reshape(2, 1, dff)

    def columns(w):
        return w.transpose(1, 0, 2).reshape(w.shape[1], N_DEV * w.shape[2])

    def column_shards(w):
        return w.reshape(w.shape[0], N_DEV, w.shape[1] // N_DEV).transpose(1, 0, 2)

    w1 = columns(w1)
    a = _mm("l0_pw1", h0, w1, grid=(2, t // tm, 1),
            a_spec=pl.BlockSpec((tm, d), lambda j, i, k: (i, 0)),
            b_spec=pl.BlockSpec((d, d), lambda j, i, k: (0, j)),
            out_spec=pl.BlockSpec((None, tm, d), lambda j, i, k: (j, i, 0)),
            out_shape=jax.ShapeDtypeStruct((2, t, d), BF16), dims=NN, acc_shape=(tm, d),
            extras=(conv_b_pw1,), extra_specs=(pl.BlockSpec((1, d), lambda j, i, k: (0, j)),),
            epilogue=lambda acc, ex, rows: ((acc + ex[0][...],), ()))
    v = _conv_fwd("l0_conv", a, wdw, conv_b_dw, seq)
    forwarded = _exchange_forward("gather_forward_wu0", gather, [2, 3], v)
    (w2,) = _exchange_wait("gather_wait_w2", gather, [2], forwarded)
    w2 = w2.reshape(d, d)
    row = pl.BlockSpec((tm, d), lambda i, j, k: (i, 0))
    vec = pl.BlockSpec((1, d), lambda i, j, k: (0, 0))
    square = pl.BlockSpec((d, d), lambda i, j, k: (0, 0))

    def ln_silu(v_blk, ex):
        s_blk = _ln_silu_tile(v_blk, ex[0][...], ex[1][...]).astype(BF16)
        return s_blk, s_blk

    def residual_and_norm(acc, ex, rows):
        r_blk = ex[3][rows, :] + (acc + ex[2][...])
        return (r_blk, _rms(r_blk, ex[4][...])), ()

    r1, h1, s = _mm("l0_pw2", v, w2, grid=(t // tm, 1, 1), a_spec=row, b_spec=square, out_spec=[row, row, row],
                    out_shape=[jax.ShapeDtypeStruct((t, d), F32), jax.ShapeDtypeStruct((t, d), BF16),
                               jax.ShapeDtypeStruct((t, d), BF16)],
                    dims=NN, acc_shape=(tm, d), extras=(conv_ln_g, conv_ln_b, conv_b_pw2, x2, norm_ffn[0:1]),
                    extra_specs=(vec, vec, vec, row, vec), prologue=ln_silu, epilogue=residual_and_norm)

    def up_getter(name, idx):
        return lambda after: _exchange_wait(name, gather, [idx], after)[0].reshape(2 * dff, d)

    def down_getter(name, idx, forward=None):
        def get(after):
            if forward is not None:
                after = _exchange_forward(forward[0], gather, forward[1], after)
            return _exchange_wait(name, gather, [idx], after)[0].reshape(dff, d)
        return get

    r2, ffn0_saved = _ffn_forward("f0", r1, h1, up_getter("gather_wait_wu0", 3),
                                  down_getter("gather_wait_wd0", 4, ("gather_forward_wu1", [4, 5, 6])),
                                  fwdw[0], fbdw[0], seq)
    forwarded = _exchange_forward("gather_forward_wd1", gather, [7], r2)
    (wp,) = _exchange_wait("gather_wait_wp", gather, [5], forwarded)
    wp = wp.transpose(1, 0, 2, 3).reshape(N_GROUPS, cg, cg)
    pooled, r3, h3 = _pool_mix_fwd("l1_mix", r2, norm_mix[1:2], wp, pool_s_full, pool_b_full, norm_ffn[1:2], seq)
    (dr4, loss_part, dfinal), ffn1_saved = _ffn_forward(
        "f1", r3, h3, up_getter("gather_wait_wu1", 6), down_getter("gather_wait_wd1", 7), fwdw[1], fbdw[1], seq,
        loss=(tgt2, final_norm.reshape(1, d)))

    dr3, dnf1, dwu1, dwd1, dfw1, dfb1, _ = _ffn_backward("f1", dr4, ffn1_saved, norm_ffn[1:2], fwdw[1], fbdw[1], seq)
    scatter_a = _exchange_start("scatter_f1_start", [dwu1.reshape(N_DEV, fu, d), dwd1.reshape(N_DEV, fd, d)],
                                [SCATTER, SCATTER])
    dr2, dwp, dpool_s, dpool_b, dnm1 = _pool_mix_bwd(
        "l1_dmix", pooled, wp, dr3, r2, norm_mix[1:2], pool_s_full + scatter_a["token"][0:1, 0:1], pool_b_full, seq)
    dr1, dnf0, dwu0, dwd0, dfw0, dfb0, db2 = _ffn_backward("f0", dr2, ffn0_saved, norm_ffn[0:1], fwdw[0], fbdw[0], seq)
    dwp_b = dwp.astype(BF16).reshape(N_GROUPS, N_DEV, cgs, cg).transpose(1, 0, 2, 3)
    tk = _tile(t, 2048)
    dw2 = _mm("l0_dw2", s, dr1, grid=(1, 1, t // tk),
              a_spec=pl.BlockSpec((tk, d), lambda i, j, k: (k, 0)),
              b_spec=pl.BlockSpec((tk, d), lambda i, j, k: (k, 0)),
              out_spec=pl.BlockSpec((d, d), lambda i, j, k: (0, 0)),
              out_shape=jax.ShapeDtypeStruct((d, d), BF16), dims=TN, acc_shape=(d, d))
    scatter_b = _exchange_start("scatter_f0_start", [dwu0.reshape(N_DEV, fu, d), dwd0.reshape(N_DEV, fd, d), dwp_b,
                                                     dw2.reshape(N_DEV, d // N_DEV, d)], [SCATTER] * 4)

    def ln_silu_backward(acc, ex, rows):
        dv_blk, dgain, dbias, colsum = _ln_silu_bwd_tile(acc, ex[0][rows, :], ex[1][...], ex[2][...])
        return (dv_blk,), (dgain, dbias, colsum)

    dv, dlg, dlb, dbdw = _mm("l0_ds", dr1, w2, grid=(t // tm, 1, 1), a_spec=row, b_spec=square,
                             out_spec=[row, vec, vec, vec],
                             out_shape=[jax.ShapeDtypeStruct((t, d), F32)] + [jax.ShapeDtypeStruct((1, d), F32)] * 3,
                             dims=NT, acc_shape=(tm, d), extras=(v, conv_ln_g, conv_ln_b), extra_specs=(row, vec, vec),
                             epilogue=ln_silu_backward, n_sums=3, token=scatter_b["token"])
    da, dwdw, db1 = _conv_bwd("l0_dconv", a, dv, wdw, seq)
    tk1 = _tile(t, 4096)
    once = dict(pipeline_mode=pl.Buffered(1)) if tk1 == t else {}
    dw1 = _mm("l0_dw1", h0, da, grid=(1, 2, t // tk1),
              a_spec=pl.BlockSpec((tk1, d), lambda i, j, k: (k, 0), **once),
              b_spec=pl.BlockSpec((None, tk1, d), lambda i, j, k: (j, k, 0)),
              out_spec=pl.BlockSpec((d, d), lambda i, j, k: (0, j)),
              out_shape=jax.ShapeDtypeStruct((d, 2 * d), BF16), dims=TN, acc_shape=(d, d))
    scatter_c = _exchange_start("scatter_l0_start", [column_shards(dw1)], [SCATTER])
    def norm_backward(acc, ex, rows):
        dx_blk, dgain, colsum = _rms_bwd_tile(acc, ex[0][rows, :], ex[1][...], ex[2][rows, :])
        return (dx_blk,), (dgain, colsum)

    dx, dnm0, _ = _mm("l0_dh", da, w1, grid=(t // tm, 1, 1),
                      a_spec=pl.BlockSpec((2, tm, d), lambda i, j, k: (0, i, 0)),
                      b_spec=pl.BlockSpec((d, 2 * d), lambda i, j, k: (0, 0), pipeline_mode=pl.Buffered(1)),
                      out_spec=[row, vec, vec],
                      out_shape=[jax.ShapeDtypeStruct((t, d), F32)] + [jax.ShapeDtypeStruct((1, d), F32)] * 2,
                      dims=NT, acc_shape=(tm, d), extras=(x2, norm_mix[0:1], dr1), extra_specs=(row, vec, row),
                      epilogue=norm_backward, n_sums=2, parts=2, token=scatter_c["token"])

    dffn_w = jnp.stack([dfw0, dfw1])
    dffn_b = jnp.stack([dfb0, dfb1]).reshape(2, dff)
    small_parts = [loss_part, jnp.concatenate([dnm0, dnm1]), jnp.concatenate([dnf0, dnf1]), db1, dwdw, dbdw, dlg, dlb,
                   db2, dpool_b, dpool_s, dffn_w, dffn_b, dfinal]
    small_part_shapes = [(1,), (2, d), (2, d), (1, 2 * d), (k_taps, d), (1, d), (1, d), (1, d), (1, d), (1, d), (1, d),
                         (2, kf, dff), (2, dff), (d,)]
    packed = _pack(small_parts, 8 * LANE)
    gather_small = _exchange_start("gather_small_start", [packed], [GATHER])

    def big_update(name, recv, w, m, v, layer=0, prev=None):
        shape = w.shape
        c = recv.shape[-1]
        rows = recv.size // (N_DEV * c)
        nl = w.size // (rows * c)
        outs = _adamw(name, recv.reshape(N_DEV, rows, c), w.reshape(nl, rows, c), m.reshape(nl, rows, c),
                      v.reshape(nl, rows, c), layer, prev)
        return outs, [o.reshape(shape) for o in outs]

    wu_t = [p.transpose(0, 2, 1) for p in (ffn_w_up, m_ffn_w_up, v_ffn_w_up)]
    g_wu1, g_wd1 = _exchange_wait("scatter_f1_wait", scatter_a, [0, 1], gather_small["token"])
    raw_wu, _ = big_update("adam_wu1", g_wu1, *wu_t, 1)
    raw_wd, _ = big_update("adam_wd1", g_wd1, ffn_w_down, m_ffn_w_down, v_ffn_w_down, 1)
    g_wu0, g_wd0, g_wp, g_w2 = _exchange_wait("scatter_f0_wait", scatter_b, [0, 1, 2, 3], raw_wd[0])
    _, u_wu = big_update("adam_wu0", g_wu0, *wu_t, 0, raw_wu)
    u_wu = [o.transpose(0, 2, 1) for o in u_wu]
    _, u_wd = big_update("adam_wd0", g_wd0, ffn_w_down, m_ffn_w_down, v_ffn_w_down, 0, raw_wd)
    _, u_wp = big_update("adam_wp", g_wp, pool_w, m_pool_w, v_pool_w)
    (g_w1,) = _exchange_wait("scatter_l0_wait", scatter_c, [0], u_wp[0])
    _, u_w1 = big_update("adam_w1", g_w1, conv_w_pw1, m_conv_w_pw1, v_conv_w_pw1)
    _, u_w2 = big_update("adam_w2", g_w2, conv_w_pw2, m_conv_w_pw2, v_conv_w_pw2)
    (all_small,) = _exchange_wait("gather_small_wait", gather_small, [0], u_w2[0])
    summed = _sum_rows("sum_small_grads", all_small)
    (loss_v, g_nm, g_nf, g_b1, g_wdw, g_bdw, g_lg, g_lb, g_b2, g_pb, g_ps, g_fw, g_fb,
     g_fin) = _unpack(summed, small_part_shapes)
    loss = loss_v[0]
    g_wdw_mine = lax.dynamic_slice_in_dim(g_wdw, my * dsh, dsh, axis=1)[None]
    g_pb_mine = lax.dynamic_slice_in_dim(g_pb, my * dsh, dsh, axis=1)
    g_ps_mine = lax.dynamic_slice_in_dim(g_ps, my * dsh, dsh, axis=1)
    g_fw_mine = lax.dynamic_slice_in_dim(g_fw, my * fsh, fsh, axis=2)

    small_g =[g_nm, g_nf, g_b1, g_wdw_mine, g_bdw, g_lg, g_lb, g_b2, g_pb_mine, g_ps_mine, g_fw_mine, g_fb, g_fin]
    small_w = [norm_mix, norm_ffn, conv_b_pw1, conv_w_dw, conv_b_dw, conv_ln_g, conv_ln_b, conv_b_pw2, pool_b,
               pool_scale, ffn_w_dw, ffn_b_dw, final_norm]
    small_m = [m_norm_mix, m_norm_ffn, m_conv_b_pw1, m_conv_w_dw, m_conv_b_dw, m_conv_ln_g, m_conv_ln_b,
               m_conv_b_pw2, m_pool_b, m_pool_scale, m_ffn_w_dw, m_ffn_b_dw, m_final_norm]
    small_v = [v_norm_mix, v_norm_ffn, v_conv_b_pw1, v_conv_w_dw, v_conv_b_dw, v_conv_ln_g, v_conv_ln_b,
               v_conv_b_pw2, v_pool_b, v_pool_scale, v_ffn_w_dw, v_ffn_b_dw, v_final_norm]
    shapes = [w.shape for w in small_w]
    outs = _adamw("adam_small", _pack(small_g, 8 * LANE)[None], _pack(small_w, 8 * LANE)[None],
                  _pack(small_m, 8 * LANE)[None], _pack(small_v, 8 * LANE)[None])
    sg, sd, sm, sv = [_unpack(o, shapes) for o in outs]

    def leaf(kind):
        (nm, nf, b1, wdw_, bdw_, lg, lb, b2, pb, ps, fw, fb, fin) = (sg, sd, sm, sv)[kind]
        return [nm, nf, u_w1[kind], b1, wdw_, bdw_, lg, lb, u_w2[kind], b2, u_wp[kind], pb, ps, u_wu[kind], fw, fb,
                u_wd[kind], fin]

    return (loss, dx.reshape(bsz, seq, d), *leaf(0), *leaf(1), *leaf(2), *leaf(3))
```

```python
import functools

import jax
import jax.numpy as jnp
from jax import lax
from jax.experimental import pallas as pl
from jax.experimental.pallas import tpu as pltpu

F32 = jnp.float32
BF16 = jnp.bfloat16
MESH = pl.DeviceIdType.MESH
HBM = pl.BlockSpec(memory_space=pltpu.HBM)

N_DEV = 8
RMS_EPS = 1e-6
LN_EPS = 1e-5
POOL_WINDOWS = (2, 4, 8, 16)
N_GROUPS = len(POOL_WINDOWS)
ADAM_LR = 0.001
ADAM_B1 = 0.9
ADAM_B2 = 0.999
ADAM_EPS = 1e-08
ADAM_WD = 0.01
ADAM_STEP = 10

LANE = 128
HALO = 32
HALO16 = 16
VMEM_LIMIT = 56 * 1024 * 1024


def _params(*sem):
    return pltpu.CompilerParams(dimension_semantics=sem if sem else None, vmem_limit_bytes=VMEM_LIMIT)


def _tile(n, pref):
    for t in range(min(pref, n), 15, -1):
        if n % t == 0 and t % 16 == 0:
            return t
    return n


def _sigmoid(z):
    return 1.0 / (1.0 + jnp.exp(-z))


def _me():
    return lax.axis_index("x"), lax.axis_index("y"), lax.axis_index("c")


def _flip(pos, m):
    x, y, c = pos
    return ((1 - x) if m & 4 else x, (1 - y) if m & 2 else y, (1 - c) if m & 1 else c)


def _lin(pos):
    return 4 * pos[0] + 2 * pos[1] + pos[2]


SEM = pl.BlockSpec(memory_space=pltpu.SEMAPHORE)
ANY = pl.BlockSpec(memory_space=pl.ANY)
EFFECT = pltpu.SideEffectType.DATAFLOW_SIDE_EFFECTING


SCATTER = "scatter"
GATHER = "gather"
GATHER2 = "gather2"
ALL_MASKS = (1, 2, 3, 4, 5, 6, 7)
SIBLING = 1
CHIPS = (2, 4, 6)


class _Copies:
    def __init__(self, a, mode, src, land, send_sems, recv_sems):
        self.a, self.mode, self.src, self.land = a, mode, src, land
        self.send_sems, self.recv_sems = send_sems, recv_sems
        self.me = _me()
        self.first = (SIBLING,) + CHIPS if mode == GATHER2 else ALL_MASKS

    def _sems(self, m, to):
        return dict(send_sem=self.send_sems.at[self.a * N_DEV + m], recv_sem=self.recv_sems.at[self.a * N_DEV + m],
                    device_id=to, device_id_type=MESH)

    def _block(self, pid):
        return self.src.at[pid] if self.mode == SCATTER else self.src

    def local(self):
        my = _lin(self.me)
        return pltpu.make_async_copy(self._block(my), self.land.at[my], self.send_sems.at[self.a * N_DEV])

    def send(self, m):
        peer = _flip(self.me, m)
        return pltpu.make_async_remote_copy(src_ref=self._block(_lin(peer)), dst_ref=self.land.at[_lin(self.me)],
                                            **self._sems(m, peer))

    def arrival(self, m):
        rows = self.land.at[_lin(_flip(self.me, m))]
        return pltpu.make_async_remote_copy(src_ref=rows, dst_ref=rows, **self._sems(m, _flip(self.me, m)))

    def forward(self, m):
        rows = self.land.at[_lin(_flip(self.me, m))]
        return pltpu.make_async_remote_copy(src_ref=rows, dst_ref=rows, **self._sems(m | 1, _flip(self.me, SIBLING)))


def _exchange_start(name, arrs, modes):
    n = len(arrs)
    blocks = [a.shape[1:] if md == SCATTER else a.shape for a, md in zip(arrs, modes)]

    def body(*refs):
        srcs, lands = refs[:n], refs[n:2 * n]
        send_sems, recv_sems = refs[2 * n], refs[2 * n + 1]
        token = refs[-1]
        for a in range(n):
            cp = _Copies(a, modes[a], srcs[a], lands[a], send_sems, recv_sems)
            cp.local().start()
            for m in cp.first:
                cp.send(m).start()
        token[...] = jnp.zeros_like(token)

    lands = [lax.empty((N_DEV,) + tuple(b), a.dtype) for a, b in zip(arrs, blocks)]
    outs = pl.pallas_call(
        body, name=name,
        out_shape=(pltpu.SemaphoreType.DMA((n * N_DEV,)), pltpu.SemaphoreType.DMA((n * N_DEV,)),
                   *[pltpu.HBM(a.shape, a.dtype) for a in arrs], *[pltpu.HBM(l.shape, l.dtype) for l in lands],
                   jax.ShapeDtypeStruct((8, LANE), F32)),
        in_specs=[HBM] * (2 * n),
        out_specs=(SEM, SEM, *[HBM] * (2 * n), pl.BlockSpec(memory_space=pltpu.VMEM)),
        input_output_aliases={i: 2 + i for i in range(2 * n)},
        compiler_params=pltpu.CompilerParams(has_side_effects=EFFECT),
    )(*[pltpu.with_memory_space_constraint(a, pltpu.HBM) for a in arrs],
      *[pltpu.with_memory_space_constraint(l, pltpu.HBM) for l in lands])
    return dict(send=outs[0], recv=outs[1], srcs=list(outs[2:2 + n]), lands=list(outs[2 + n:2 + 2 * n]),
                modes=modes, token=outs[-1])


def _exchange_forward(name, handle, which, after):
    k = len(which)

    def half(wait):
        def body(*refs):
            lands = refs[:k]
            send_sems, recv_sems = refs[k], refs[k + 1]
            token = refs[-1]
            for pos, a in enumerate(which):
                cp = _Copies(a, GATHER2, None, lands[pos], send_sems, recv_sems)
                for m in CHIPS:
                    if wait:
                        cp.arrival(m).wait_recv()
                    else:
                        cp.forward(m).start()
            token[...] = jnp.zeros_like(token)
        return body

    def call(body, call_name, lands, after):
        outs = pl.pallas_call(
            body, name=call_name,
            out_shape=(*[pltpu.HBM(x.shape, x.dtype) for x in lands], jax.ShapeDtypeStruct((8, LANE), F32)),
            in_specs=[HBM] * k + [SEM, SEM, ANY], out_specs=(*[HBM] * k, pl.BlockSpec(memory_space=pltpu.VMEM)),
            input_output_aliases={i: i for i in range(k)},
            compiler_params=pltpu.CompilerParams(has_side_effects=EFFECT),
        )(*lands, handle["send"], handle["recv"], after)
        return list(outs[:k]), outs[-1]

    lands, arrived = call(half(True), name + "_arrived", [handle["lands"][a] for a in which], after)
    lands, token = call(half(False), name, lands, arrived)
    for pos, a in enumerate(which):
        handle["lands"][a] = lands[pos]
    return token


def _exchange_wait(name, handle, which, after):
    k = len(which)
    modes = handle["modes"]

    def body(*refs):
        srcs, lands = refs[:k], refs[k:2 * k]
        send_sems, recv_sems = refs[2 * k], refs[2 * k + 1]
        for pos, a in enumerate(which):
            cp = _Copies(a, modes[a], srcs[pos], lands[pos], send_sems, recv_sems)
            cp.local().wait()
            for m in cp.first:
                cp.send(m).wait_send()
            if modes[a] == GATHER2:
                for m in CHIPS:
                    cp.forward(m).wait_send()
                arrivals = (SIBLING,) + tuple(m | 1 for m in CHIPS)
            else:
                arrivals = ALL_MASKS
            for m in arrivals:
                cp.arrival(m).wait_recv()

    srcs = [handle["srcs"][a] for a in which]
    lands = [handle["lands"][a] for a in which]
    outs = pl.pallas_call(
        body, name=name,
        out_shape=tuple(pltpu.HBM(x.shape, x.dtype) for x in srcs + lands),
        in_specs=[HBM] * (2 * k) + [SEM, SEM, ANY], out_specs=tuple([HBM] * (2 * k)),
        input_output_aliases={i: i for i in range(2 * k)},
        compiler_params=pltpu.CompilerParams(has_side_effects=EFFECT),
    )(*srcs, *lands, handle["send"], handle["recv"], after)
    for pos, a in enumerate(which):
        handle["srcs"][a], handle["lands"][a] = outs[pos], outs[k + pos]
    return list(outs[k:])


def _mm(name, a, b, *, grid, a_spec, b_spec, out_spec, out_shape, dims, acc_shape, extras=(), extra_specs=(),
        epilogue=None, token=None, prologue=None, n_sums=0, parts=1):
    nk = grid[2]
    ne = len(extras)
    deps = () if token is None else (token,)
    dep_specs = [pl.BlockSpec((8, LANE), lambda i, j, k: (0, 0))] * len(deps)
    n_out = len(out_shape) if isinstance(out_shape, (list, tuple)) else 1
    n_tiles = n_out - n_sums - (1 if prologue is not None else 0)

    def body(a_ref, b_ref, *rest):
        ex, o_refs, acc_ref = rest[:ne], rest[ne + len(deps):ne + len(deps) + n_out], rest[ne + len(deps) + n_out]
        k = pl.program_id(2)
        if parts == 1:
            a_blk, saved = a_ref[...], None
            if prologue is not None:
                a_blk, saved = prologue(a_blk, ex)
                o_refs[n_tiles][...] = saved
            part = lax.dot_general(a_blk.astype(BF16), b_ref[...].astype(BF16), (dims, ((), ())),
                                   preferred_element_type=F32)
        else:
            kb = b_ref.shape[dims[1][0]] // parts
            part = None
            for p in range(parts):
                b_blk = b_ref[p * kb:(p + 1) * kb, :] if dims[1][0] == 0 else b_ref[:, p * kb:(p + 1) * kb]
                term = lax.dot_general(a_ref[p].astype(BF16), b_blk.astype(BF16), (dims, ((), ())),
                                       preferred_element_type=F32)
                part = term if part is None else part + term
        sum_refs = o_refs[n_out - n_sums:]

        def add_sums(terms):
            @pl.when((pl.program_id(0) == 0) & (pl.program_id(1) == 0))
            def _():
                for o_ref in sum_refs:
                    o_ref[...] = jnp.zeros_like(o_ref)

            for o_ref, term in zip(sum_refs, terms):
                o_ref[...] += jnp.sum(term, axis=0, keepdims=True)

        def finish(r):
            tiles, terms = ((r,), ()) if epilogue is None else epilogue(r, ex, slice(None))
            for o_ref, val in zip(o_refs, tiles):
                o_ref[...] = val.astype(o_ref.dtype)
            if n_sums:
                add_sums(terms)

        if nk == 1:
            finish(part)
            return

        @pl.when(k == 0)
        def _():
            acc_ref[...] = part

        @pl.when((k > 0) & (k < nk - 1))
        def _():
            acc_ref[...] += part

        @pl.when(k == nk - 1)
        def _():
            finish(acc_ref[...] + part)

    return pl.pallas_call(
        body, name=name, grid=grid, in_specs=[a_spec, b_spec, *extra_specs, *dep_specs], out_specs=out_spec,
        out_shape=out_shape, scratch_shapes=[pltpu.VMEM(acc_shape if nk > 1 else (8, LANE), F32)],
        compiler_params=_params(*(("arbitrary",) * 3 if n_sums else ("parallel", "parallel", "arbitrary"))),
    )(a, b, *extras, *deps)


def _rms(x, gain):
    return x * lax.rsqrt(jnp.mean(x * x, axis=-1, keepdims=True) + RMS_EPS) * gain


def _rms_bwd_tile(dh, x, gain, dres):
    rstd = lax.rsqrt(jnp.mean(x * x, axis=-1, keepdims=True) + RMS_EPS)
    xhat = x * rstd
    dxhat = dh * gain
    dx = dres + rstd * (dxhat - xhat * jnp.mean(dxhat * xhat, axis=-1, keepdims=True))
    return dx, dh * xhat, dx


def _ln_silu_tile(v, g, b):
    mu = jnp.mean(v, axis=-1, keepdims=True)
    cen = v - mu
    z = cen * lax.rsqrt(jnp.mean(cen * cen, axis=-1, keepdims=True) + LN_EPS) * g + b
    return z * _sigmoid(z)


def _ln_silu_bwd_tile(ds, v, g, b):
    mu = jnp.mean(v, axis=-1, keepdims=True)
    cen = v - mu
    rstd = lax.rsqrt(jnp.mean(cen * cen, axis=-1, keepdims=True) + LN_EPS)
    y = cen * rstd
    z = y * g + b
    sig = _sigmoid(z)
    dz = ds * sig * (1.0 + z * (1.0 - sig))
    dy = dz * g
    dv = rstd * (dy - jnp.mean(dy, axis=-1, keepdims=True) - y * jnp.mean(dy * y, axis=-1, keepdims=True))
    return dv, dz * y, dz, dv


def _loss_tile(x, tgt, gain):
    d = x.shape[-1]
    rstd = lax.rsqrt(jnp.mean(x * x, axis=-1, keepdims=True) + RMS_EPS)
    xhat = x * rstd
    err = xhat * gain - tgt
    dy = err / d
    dxhat = dy * gain
    dx = rstd * (dxhat - xhat * jnp.mean(dxhat * xhat, axis=-1, keepdims=True))
    return dx, 0.5 * jnp.mean(err * err, axis=-1, keepdims=True), dy * xhat


NN = ((1,), (0,))
NT = ((1,), (1,))
TN = ((0,), (0,))


def _rms_fwd(name, x, gain):
    t, d = x.shape
    tr = _tile(t, 512)

    def body(x_ref, g_ref, h_ref):
        h_ref[...] = _rms(x_ref[...], g_ref[...]).astype(BF16)

    return pl.pallas_call(
        body, name=name, grid=(t // tr,),
        in_specs=[pl.BlockSpec((tr, d), lambda i: (i, 0)), pl.BlockSpec((1, d), lambda i: (0, 0))],
        out_specs=pl.BlockSpec((tr, d), lambda i: (i, 0)),
        out_shape=jax.ShapeDtypeStruct((t, d), BF16), compiler_params=_params("parallel"),
    )(x, gain)


def _conv_tiles(t, seq):
    ts = _tile(seq, 1024)
    return ts, seq // ts, _tile(ts, 64)


def _conv_fwd(name, a, w, b, seq):
    _, t, d = a.shape
    k_taps = w.shape[0]
    ts, tps, rc = _conv_tiles(t, seq)
    hb = ts // HALO

    def body(cur_ref, prev_ref, w_ref, b_ref, v_ref, upad):
        i = pl.program_id(1)
        first = (i % tps) == 0
        pv = prev_ref[0].astype(F32)
        pg = prev_ref[1].astype(F32)
        upad[0:HALO, :] = jnp.where(first, 0.0, pv * _sigmoid(pg))
        upad[HALO:HALO + ts, :] = cur_ref[0].astype(F32) * _sigmoid(cur_ref[1].astype(F32))
        wv = w_ref[...]
        bias = jnp.broadcast_to(b_ref[...], (rc, LANE))
        for r0 in range(0, ts, rc):
            acc = bias
            for k in range(k_taps):
                acc = acc + wv[k:k + 1, :] * upad[pl.ds(HALO - (k_taps - 1) + k + r0, rc), :]
            v_ref[pl.ds(r0, rc), :] = acc

    return pl.pallas_call(
        body, name=name, grid=(d // LANE, t // ts),
        in_specs=[pl.BlockSpec((2, ts, LANE), lambda c, i: (0, i, c)),
                  pl.BlockSpec((2, HALO, LANE), lambda c, i: (0, jnp.maximum(i * hb - 1, 0), c)),
                  pl.BlockSpec((k_taps, LANE), lambda c, i: (0, c)),
                  pl.BlockSpec((1, LANE), lambda c, i: (0, c))],
        out_specs=pl.BlockSpec((ts, LANE), lambda c, i: (i, c)),
        out_shape=jax.ShapeDtypeStruct((t, d), F32),
        scratch_shapes=[pltpu.VMEM((HALO + ts, LANE), F32)],
        compiler_params=_params("parallel", "parallel"),
    )(a, a, w, b)


def _conv_bwd(name, a, dv, w, seq):
    _, t, d = a.shape
    k_taps = w.shape[0]
    ts, tps, rc = _conv_tiles(t, seq)
    hb = ts // HALO
    nhb = t // HALO

    def body(cur_ref, prev_ref, dv_ref, ndv_ref, w_ref, da_ref, dw_ref, dbp_ref, upad, dvpad, dwrows):
        i = pl.program_id(1)
        first = (i % tps) == 0
        last = (i % tps) == tps - 1
        pv = prev_ref[0].astype(F32)
        pg = prev_ref[1].astype(F32)
        upad[0:HALO, :] = jnp.where(first, 0.0, pv * _sigmoid(pg))
        upad[HALO:HALO + ts, :] = cur_ref[0].astype(F32) * _sigmoid(cur_ref[1].astype(F32))
        dvpad[0:ts, :] = dv_ref[...]
        dvpad[ts:ts + HALO, :] = jnp.where(last, 0.0, ndv_ref[...])
        wv = w_ref[...]

        @pl.when(i == 0)
        def _():
            dw_ref[...] = jnp.zeros_like(dw_ref)
            dbp_ref[...] = jnp.zeros_like(dbp_ref)

        sv = jnp.zeros((1, LANE), F32)
        sg = jnp.zeros((1, LANE), F32)
        for r0 in range(0, ts, rc):
            du = jnp.zeros((rc, LANE), F32)
            for k in range(k_taps):
                du = du + wv[k:k + 1, :] * dvpad[pl.ds(r0 + (k_taps - 1) - k, rc), :]
            av = cur_ref[0, pl.ds(r0, rc), :].astype(F32)
            sig = _sigmoid(cur_ref[1, pl.ds(r0, rc), :].astype(F32))
            dval = du * sig
            dgate = du * av * sig * (1.0 - sig)
            da_ref[0, pl.ds(r0, rc), :] = dval.astype(BF16)
            da_ref[1, pl.ds(r0, rc), :] = dgate.astype(BF16)
            sv = sv + jnp.sum(dval, axis=0, keepdims=True)
            sg = sg + jnp.sum(dgate, axis=0, keepdims=True)
        dbp_ref[0] += sv
        dbp_ref[1] += sg

        for k in range(k_taps):
            acc = jnp.zeros((rc, LANE), F32)
            for r0 in range(0, ts, rc):
                acc = acc + dvpad[pl.ds(r0, rc), :] * upad[pl.ds(HALO - (k_taps - 1) + k + r0, rc), :]
            dwrows[k:k + 1, :] = jnp.sum(acc, axis=0, keepdims=True)
        dw_ref[...] += dwrows[0:k_taps, :]

    return pl.pallas_call(
        body, name=name, grid=(d // LANE, t // ts),
        in_specs=[pl.BlockSpec((2, ts, LANE), lambda c, i: (0, i, c)),
                  pl.BlockSpec((2, HALO, LANE), lambda c, i: (0, jnp.maximum(i * hb - 1, 0), c)),
                  pl.BlockSpec((ts, LANE), lambda c, i: (i, c)),
                  pl.BlockSpec((HALO, LANE), lambda c, i: (jnp.minimum((i + 1) * hb, nhb - 1), c)),
                  pl.BlockSpec((k_taps, LANE), lambda c, i: (0, c))],
        out_specs=[pl.BlockSpec((2, ts, LANE), lambda c, i: (0, i, c)),
                   pl.BlockSpec((k_taps, LANE), lambda c, i: (0, c)),
                   pl.BlockSpec((2, 1, LANE), lambda c, i: (0, 0, c))],
        out_shape=[jax.ShapeDtypeStruct((2, t, d), BF16), jax.ShapeDtypeStruct((k_taps, d), F32),
                   jax.ShapeDtypeStruct((2, 1, d), F32)],
        scratch_shapes=[pltpu.VMEM((HALO + ts, LANE), F32), pltpu.VMEM((ts + HALO, LANE), F32),
                        pltpu.VMEM((HALO, LANE), F32)],
        compiler_params=_params("parallel", "arbitrary"),
    )(a, a, dv, dv, w)


def _pool_mix_fwd(name, x, gain, wp, scale, bias, next_gain, seq):
    t, d = x.shape
    ts = _tile(seq, 256)
    tps = seq // ts
    hb = ts // HALO
    cg = d // N_GROUPS

    def body(cur_ref, prev_ref, g_ref, w_ref, s_ref, b_ref, ng_ref, p_ref, r_ref, h_ref, hpad):
        i = pl.program_id(0)
        first = (i % tps) == 0
        g = g_ref[...]
        hpad[0:HALO, :] = jnp.where(first, 0.0, _rms(prev_ref[...], g))
        hpad[HALO:HALO + ts, :] = _rms(cur_ref[...], g)
        pos = (i % tps) * ts + lax.broadcasted_iota(jnp.int32, (ts, 1), 0)
        for gi, win in enumerate(POOL_WINDOWS):
            sl = slice(gi * cg, (gi + 1) * cg)
            own = hpad[HALO:HALO + ts, sl]
            acc = own
            for j in range(1, win):
                acc = acc + hpad[HALO - j:HALO - j + ts, sl]
            cnt = jnp.minimum(pos + 1, win).astype(F32)
            pooled = (acc / cnt - own).astype(BF16)
            p_ref[:, sl] = pooled
            mixed = jnp.dot(pooled, w_ref[gi], preferred_element_type=F32)
            r_ref[:, sl] = cur_ref[:, sl] + s_ref[:, sl] * (mixed + b_ref[:, sl])
        h_ref[...] = _rms(r_ref[...], ng_ref[...]).astype(BF16)

    row = pl.BlockSpec((ts, d), lambda i: (i, 0))
    vec = pl.BlockSpec((1, d), lambda i: (0, 0))
    return pl.pallas_call(
        body, name=name, grid=(t // ts,),
        in_specs=[row, pl.BlockSpec((HALO, d), lambda i: (jnp.maximum(i * hb - 1, 0), 0)), vec,
                  pl.BlockSpec((N_GROUPS, cg, cg), lambda i: (0, 0, 0)), vec, vec, vec],
        out_specs=[row, row, row],
        out_shape=[jax.ShapeDtypeStruct((t, d), BF16), jax.ShapeDtypeStruct((t, d), F32),
                   jax.ShapeDtypeStruct((t, d), BF16)],
        scratch_shapes=[pltpu.VMEM((HALO + ts, d), F32)],
        compiler_params=_params("parallel"),
    )(x, x, gain, wp, scale, bias, next_gain)


def _pool_mix_bwd(name, pooled, wp, dr, x, gain, scale, bias, seq):
    t, d = x.shape
    ts = _tile(seq, 256)
    tps = seq // ts
    hb = ts // HALO
    nhb = t // HALO
    cg = d // N_GROUPS

    def body(p_ref, w_ref, dr_ref, ndr_ref, x_ref, g_ref, s_ref, b_ref, dx_ref, dw_ref, ds_ref, db_ref, dg_ref,
             qpad, dh):
        i = pl.program_id(0)
        last = (i % tps) == tps - 1
        pos = (i % tps) * ts + lax.broadcasted_iota(jnp.int32, (ts, 1), 0)

        @pl.when(i == 0)
        def _():
            dw_ref[...] = jnp.zeros_like(dw_ref)
            ds_ref[...] = jnp.zeros_like(ds_ref)
            db_ref[...] = jnp.zeros_like(db_ref)
            dg_ref[...] = jnp.zeros_like(dg_ref)

        for gi, win in enumerate(POOL_WINDOWS):
            sl = slice(gi * cg, (gi + 1) * cg)
            wv = w_ref[gi]
            sc = s_ref[:, sl]
            drv = dr_ref[:, sl]
            dmx = drv * sc
            dmx16 = dmx.astype(BF16)
            pooled = p_ref[:, sl]
            dw_ref[gi] += lax.dot_general(pooled, dmx16, (TN, ((), ())), preferred_element_type=F32)
            mixed = jnp.dot(pooled, wv, preferred_element_type=F32)
            ds_ref[:, sl] += jnp.sum(drv * (mixed + b_ref[:, sl]), axis=0, keepdims=True)
            db_ref[:, sl] += jnp.sum(dmx, axis=0, keepdims=True)
            cur = lax.dot_general(dmx16, wv, (NT, ((), ())), preferred_element_type=F32)
            nxt = lax.dot_general((ndr_ref[:, sl] * sc).astype(BF16), wv, (NT, ((), ())),
                                  preferred_element_type=F32)
            qpad[0:ts, sl] = cur / jnp.minimum(pos + 1, win).astype(F32)
            qpad[ts:ts + HALO, sl] = jnp.where(last, 0.0, nxt / float(win))
            acc = -cur
            for j in range(win):
                acc = acc + qpad[j:j + ts, sl]
            dh[:, sl] = acc
        dx, dgain_term, _ = _rms_bwd_tile(dh[...], x_ref[...], g_ref[...], dr_ref[...])
        dx_ref[...] = dx
        dg_ref[...] += jnp.sum(dgain_term, axis=0, keepdims=True)

    row = pl.BlockSpec((ts, d), lambda i: (i, 0))
    vec = pl.BlockSpec((1, d), lambda i: (0, 0))
    return pl.pallas_call(
        body, name=name, grid=(t // ts,),
        in_specs=[row, pl.BlockSpec((N_GROUPS, cg, cg), lambda i: (0, 0, 0)), row,
                  pl.BlockSpec((HALO, d), lambda i: (jnp.minimum((i + 1) * hb, nhb - 1), 0)), row, vec, vec, vec],
        out_specs=[row, pl.BlockSpec((N_GROUPS, cg, cg), lambda i: (0, 0, 0)), vec, vec, vec],
        out_shape=[jax.ShapeDtypeStruct((t, d), F32), jax.ShapeDtypeStruct((N_GROUPS, cg, cg), F32)]
        + [jax.ShapeDtypeStruct((1, d), F32)] * 3,
        scratch_shapes=[pltpu.VMEM((ts + HALO, d), F32), pltpu.VMEM((ts, d), F32)],
        compiler_params=_params("arbitrary"),
    )(pooled, wp, dr, dr, x, gain, scale, bias)


def _ctile(n, pref):
    return max(c for c in range(LANE, min(pref, n) + 1, LANE) if n % c == 0)


FFN_COLS = 1408
FFN_ROWS = 32
FFN_TILE = 1024


def _ffn_fwd(name, up, w, b, seq):
    _, t, dff = up.shape
    f = _ctile(dff, FFN_COLS)
    k_taps = w.shape[0]
    ts = _tile(seq, FFN_TILE)
    tps = seq // ts
    hb = ts // HALO16
    rc = _tile(ts, FFN_ROWS)

    def body(cur_ref, prev_ref, w_ref, b_ref, g_ref, apad):
        i = pl.program_id(1)
        first = (i % tps) == 0
        for ci, c0 in enumerate(range(0, f, LANE)):
            cols = slice(c0, c0 + LANE)
            apad[ci, 0:HALO16, :] = jnp.where(first, 0.0, prev_ref[:, cols].astype(F32))
            apad[ci, HALO16:HALO16 + ts, :] = cur_ref[0, :, cols].astype(F32)
            wv = w_ref[:, cols]
            wk = [jnp.broadcast_to(wv[k:k + 1, :], (rc, LANE)) for k in range(k_taps)]
            bias = jnp.broadcast_to(b_ref[:, cols], (rc, LANE))
            for r0 in range(0, ts, rc):
                c = bias
                for k in range(k_taps):
                    c = c + wk[k] * apad[ci, pl.ds(HALO16 - (k_taps - 1) + k + r0, rc), :]
                gate = cur_ref[1, pl.ds(r0, rc), cols].astype(F32)
                g_ref[pl.ds(r0, rc), cols] = (c * _sigmoid(c) * gate).astype(BF16)

    return pl.pallas_call(
        body, name=name, grid=(dff // f, t // ts),
        in_specs=[pl.BlockSpec((2, ts, f), lambda j, i: (0, i, j)),
                  pl.BlockSpec((None, HALO16, f), lambda j, i: (0, jnp.maximum(i * hb - 1, 0), j)),
                  pl.BlockSpec((k_taps, f), lambda j, i: (0, j)),
                  pl.BlockSpec((1, f), lambda j, i: (0, j))],
        out_specs=pl.BlockSpec((ts, f), lambda j, i: (i, j)),
        out_shape=jax.ShapeDtypeStruct((t, dff), BF16),
        scratch_shapes=[pltpu.VMEM((f // LANE, HALO16 + ts, LANE), F32)],
        compiler_params=_params("parallel", "parallel"),
    )(up, up, w, b)


def _ffn_bwd(name, up, dg, w, b, seq):
    _, t, dff = up.shape
    f = _ctile(dff, FFN_COLS)
    k_taps = w.shape[0]
    ts = _tile(seq, FFN_TILE)
    tps = seq // ts
    hb = ts // HALO16
    nhb = t // HALO16
    ext = ts + HALO16
    rc = _tile(ts, FFN_ROWS)

    def body(cur_ref, prev_ref, next_ref, dg_ref, ndg_ref, w_ref, b_ref, dup_ref, dw_ref, db_ref, apad, dcpad):
        i = pl.program_id(1)
        first = (i % tps) == 0
        last = (i % tps) == tps - 1

        @pl.when(i == 0)
        def _():
            dw_ref[...] = jnp.zeros_like(dw_ref)
            db_ref[...] = jnp.zeros_like(db_ref)

        for ci, c0 in enumerate(range(0, f, LANE)):
            cols = slice(c0, c0 + LANE)
            apad[ci, 0:HALO16, :] = jnp.where(first, 0.0, prev_ref[:, cols].astype(F32))
            apad[ci, HALO16:HALO16 + ts, :] = cur_ref[0, :, cols].astype(F32)
            apad[ci, HALO16 + ts:HALO16 + ext, :] = next_ref[0, :, cols].astype(F32)
            wv = w_ref[:, cols]
            wk = [jnp.broadcast_to(wv[k:k + 1, :], (rc, LANE)) for k in range(k_taps)]
            bias = jnp.broadcast_to(b_ref[:, cols], (rc, LANE))

            def conv_grad(r0, n, gate, dgv):
                c = bias[0:n]
                for k in range(k_taps):
                    c = c + wk[k][0:n] * apad[ci, pl.ds(HALO16 - (k_taps - 1) + k + r0, n), :]
                sig = _sigmoid(c)
                silu = c * sig
                return dgv * gate * (sig + silu * (1.0 - sig)), silu

            for r0 in range(0, ts, rc):
                dgv = dg_ref[pl.ds(r0, rc), cols].astype(F32)
                dc, silu = conv_grad(r0, rc, cur_ref[1, pl.ds(r0, rc), cols].astype(F32), dgv)
                dcpad[ci, pl.ds(r0, rc), :] = dc
                dup_ref[1, pl.ds(r0, rc), cols] = (dgv * silu).astype(BF16)
            dgv = jnp.where(last, 0.0, ndg_ref[:, cols].astype(F32))
            dc, _ = conv_grad(ts, HALO16, next_ref[1, :, cols].astype(F32), dgv)
            dcpad[ci, ts:ext, :] = dc

            dw_acc = [jnp.zeros((rc, LANE), F32) for _ in range(k_taps)]
            db_acc = jnp.zeros((rc, LANE), F32)
            for r0 in range(0, ts, rc):
                dact = jnp.zeros((rc, LANE), F32)
                for k in range(k_taps):
                    dact = dact + wk[k] * dcpad[ci, pl.ds(r0 + (k_taps - 1) - k, rc), :]
                dup_ref[0, pl.ds(r0, rc), cols] = dact.astype(BF16)
                dc = dcpad[ci, pl.ds(r0, rc), :]
                for k in range(k_taps):
                    dw_acc[k] = dw_acc[k] + dc * apad[ci, pl.ds(HALO16 - (k_taps - 1) + k + r0, rc), :]
                db_acc = db_acc + dc
            for k in range(k_taps):
                dw_ref[k:k + 1, cols] += jnp.sum(dw_acc[k], axis=0, keepdims=True)
            db_ref[:, cols] += jnp.sum(db_acc, axis=0, keepdims=True)

    return pl.pallas_call(
        body, name=name, grid=(dff // f, t // ts),
        in_specs=[pl.BlockSpec((2, ts, f), lambda j, i: (0, i, j)),
                  pl.BlockSpec((None, HALO16, f), lambda j, i: (0, jnp.maximum(i * hb - 1, 0), j)),
                  pl.BlockSpec((2, HALO16, f), lambda j, i: (0, jnp.minimum((i + 1) * hb, nhb - 1), j)),
                  pl.BlockSpec((ts, f), lambda j, i: (i, j)),
                  pl.BlockSpec((HALO16, f), lambda j, i: (jnp.minimum((i + 1) * hb, nhb - 1), j)),
                  pl.BlockSpec((k_taps, f), lambda j, i: (0, j)),
                  pl.BlockSpec((1, f), lambda j, i: (0, j))],
        out_specs=[pl.BlockSpec((2, ts, f), lambda j, i: (0, i, j)),
                   pl.BlockSpec((k_taps, f), lambda j, i: (0, j)),
                   pl.BlockSpec((1, f), lambda j, i: (0, j))],
        out_shape=[jax.ShapeDtypeStruct((2, t, dff), BF16), jax.ShapeDtypeStruct((k_taps, dff), F32),
                   jax.ShapeDtypeStruct((1, dff), F32)],
        scratch_shapes=[pltpu.VMEM((f // LANE, HALO16 + ext, LANE), F32), pltpu.VMEM((f // LANE, ext, LANE), F32)],
        compiler_params=_params("parallel", "arbitrary"),
    )(up, up, up, dg, dg, w, b)


def _sum_rows(name, g):
    ns, r, c = g.shape
    tr = _tile(r, 256)

    def body(g_ref, o_ref):
        acc = g_ref[0]
        for dev in range(1, ns):
            acc = acc + g_ref[dev]
        o_ref[...] = acc

    return pl.pallas_call(
        body, name=name, grid=(r // tr,),
        in_specs=[pl.BlockSpec((ns, tr, c), lambda i: (0, i, 0))],
        out_specs=pl.BlockSpec((tr, c), lambda i: (i, 0)),
        out_shape=jax.ShapeDtypeStruct((r, c), F32), compiler_params=_params("parallel"),
    )(g)


def _adamw(name, gsrc, w, m, v, layer=0, prev=None):
    ns, r, c = gsrc.shape
    nl = w.shape[0]
    tr = _tile(r, 256)
    prev = () if prev is None else tuple(prev)

    def body(g_ref, w_ref, m_ref, v_ref, *rest):
        go_ref, do_ref, mo_ref, vo_ref = rest[len(prev):]
        g = g_ref[0].astype(F32)
        for dev in range(1, ns):
            g = g + g_ref[dev].astype(F32)
        m_new = ADAM_B1 * m_ref[...] + (1.0 - ADAM_B1) * g
        v_new = ADAM_B2 * v_ref[...] + (1.0 - ADAM_B2) * (g * g)
        m_hat = m_new / (1.0 - ADAM_B1 ** ADAM_STEP)
        v_hat = v_new / (1.0 - ADAM_B2 ** ADAM_STEP)
        go_ref[...] = g
        do_ref[...] = -ADAM_LR * (m_hat / (jnp.sqrt(v_hat) + ADAM_EPS) + ADAM_WD * w_ref[...])
        mo_ref[...] = m_new
        vo_ref[...] = v_new

    row = pl.BlockSpec((None, tr, c), lambda i: (layer, i, 0))
    return pl.pallas_call(
        body, name=name, grid=(r // tr,),
        in_specs=[pl.BlockSpec((ns, tr, c), lambda i: (0, i, 0)), row, row, row] + [ANY] * len(prev),
        out_specs=[row] * 4, out_shape=[jax.ShapeDtypeStruct((nl, r, c), F32)] * 4,
        input_output_aliases={4 + i: i for i in range(len(prev))},
        compiler_params=_params("parallel"),
    )(gsrc, w, m, v, *prev)


def _ffn_forward(tag, r_in, h, get_wu, get_wd, wdw, bdw, seq, loss=None):
    t, d = r_in.shape
    tm = _tile(t, 512)
    wu = get_wu(h)
    dff = wu.shape[0] // 2
    tu = _tile(t, 1024)
    up = _mm(f"{tag}_up", h, wu, grid=(2, t // tu, 1),
             a_spec=pl.BlockSpec((tu, d), lambda j, i, k: (i, 0)),
             b_spec=pl.BlockSpec((dff, d), lambda j, i, k: (j, 0)),
             out_spec=pl.BlockSpec((None, tu, dff), lambda j, i, k: (j, i, 0)),
             out_shape=jax.ShapeDtypeStruct((2, t, dff), BF16), dims=NT, acc_shape=(tu, dff))
    wd = get_wd(up)
    g = _ffn_fwd(f"{tag}_act", up, wdw, bdw, seq)
    row = pl.BlockSpec((tm, d), lambda i, j, k: (i, 0))
    vec = pl.BlockSpec((1, d), lambda i, j, k: (0, 0))
    common = dict(grid=(t // tm, 1, 1), a_spec=pl.BlockSpec((tm, dff), lambda i, j, k: (i, 0)),
                  b_spec=pl.BlockSpec((dff, d), lambda i, j, k: (0, 0)), dims=NN, acc_shape=(tm, d))
    if loss is None:
        out = _mm(f"{tag}_down", g, wd, out_spec=row, out_shape=jax.ShapeDtypeStruct((t, d), F32),
                  extras=(r_in,), extra_specs=(row,), epilogue=lambda acc, ex, rows: ((ex[0][rows, :] + acc,), ()),
                  **common)
    else:
        def head(acc, ex, rows):
            dx, part, dgain = _loss_tile(ex[0][rows, :] + acc, ex[1][rows, :], ex[2][...])
            return (dx,), (part, dgain)

        out = _mm(f"{tag}_down", g, wd, out_spec=[row, pl.BlockSpec((1, 1), lambda i, j, k: (0, 0)), vec],
                  out_shape=[jax.ShapeDtypeStruct((t, d), F32), jax.ShapeDtypeStruct((1, 1), F32),
                             jax.ShapeDtypeStruct((1, d), F32)],
                  extras=(r_in, *loss), extra_specs=(row, row, vec), epilogue=head, n_sums=2, **common)
    return out, (r_in, h, up, g, wu, wd)


def _ffn_backward(tag, dr, saved, gain, wdw, bdw, seq, token=None):
    r_in, h, up, g, wu, wd = saved
    t, d = r_in.shape
    dff = wd.shape[0]
    tm = _tile(t, 512)
    tk = _tile(t, 2048)
    tku = _tile(t, 4096)
    cw = _ctile(dff, 1408)
    nc = dff // cw
    once = dict(pipeline_mode=pl.Buffered(1)) if tku == t else {}
    dg = _mm(f"{tag}_dg", dr, wd, grid=(t // tm, 1, 1),
             a_spec=pl.BlockSpec((tm, d), lambda i, j, k: (i, 0)),
             b_spec=pl.BlockSpec((dff, d), lambda i, j, k: (0, 0)),
             out_spec=pl.BlockSpec((tm, dff), lambda i, j, k: (i, 0)),
             out_shape=jax.ShapeDtypeStruct((t, dff), BF16), dims=NT, acc_shape=(tm, dff), token=token)
    dwd = _mm(f"{tag}_dwd", g, dr, grid=(dff // cw, 1, t // tk),
              a_spec=pl.BlockSpec((tk, cw), lambda i, j, k: (k, i)),
              b_spec=pl.BlockSpec((tk, d), lambda i, j, k: (k, 0)),
              out_spec=pl.BlockSpec((cw, d), lambda i, j, k: (i, 0)),
              out_shape=jax.ShapeDtypeStruct((dff, d), BF16), dims=TN, acc_shape=(cw, d))
    dup, dwdw, dbdw = _ffn_bwd(f"{tag}_dact", up, dg, wdw, bdw, seq)
    row = pl.BlockSpec((tm, d), lambda i, j, k: (i, 0))
    vec = pl.BlockSpec((1, d), lambda i, j, k: (0, 0))

    def norm_backward(acc, ex, rows):
        dx, dgain, colsum = _rms_bwd_tile(acc, ex[0][rows, :], ex[1][...], ex[2][rows, :])
        return (dx,), (dgain, colsum)

    dr_in, dgain, colsum = _mm(
        f"{tag}_dh", dup, wu, grid=(t // tm, 1, 1),
        a_spec=pl.BlockSpec((2, tm, dff), lambda i, j, k: (0, i, 0)),
        b_spec=pl.BlockSpec((2 * dff, d), lambda i, j, k: (0, 0), pipeline_mode=pl.Buffered(1)),
        out_spec=[row, vec, vec],
        out_shape=[jax.ShapeDtypeStruct((t, d), F32)] + [jax.ShapeDtypeStruct((1, d), F32)] * 2,
        dims=NN, acc_shape=(tm, d), extras=(r_in, gain, dr), extra_specs=(row, vec, row),
        epilogue=norm_backward, n_sums=2, parts=2)
    dwu = _mm(f"{tag}_dwu", dup, h, grid=(2 * nc, 1, t // tku),
              a_spec=pl.BlockSpec((None, tku, cw), lambda i, j, k: (i // nc, k, i % nc)),
              b_spec=pl.BlockSpec((tku, d), lambda i, j, k: (k, 0), **once),
              out_spec=pl.BlockSpec((cw, d), lambda i, j, k: (i, 0)),
              out_shape=jax.ShapeDtypeStruct((2 * dff, d), BF16), dims=TN, acc_shape=(cw, d))
    return dr_in, dgain, dwu, dwd, dwdw, dbdw, colsum


def _pad_to(vec, n):
    return jnp.pad(vec, (0, n - vec.shape[0]))


def _pack(parts, width):
    flat = jnp.concatenate([p.reshape(-1).astype(F32) for p in parts])
    n = -(-flat.shape[0] // (8 * width)) * (8 * width)
    return _pad_to(flat, n).reshape(n // width, width)


def _unpack(mat, shapes):
    flat = mat.reshape(-1)
    out, off = [], 0
    for s in shapes:
        n = 1
        for dim in s:
            n *= dim
        out.append(flat[off:off + n].reshape(s))
        off += n
    return out


def kernel(x, norm_mix, norm_ffn, conv_w_pw1, conv_b_pw1, conv_w_dw, conv_b_dw, conv_ln_g, conv_ln_b, conv_w_pw2, conv_b_pw2, pool_w, pool_b, pool_scale, ffn_w_up, ffn_w_dw, ffn_b_dw, ffn_w_down, final_norm, loss_target, m_norm_mix, m_norm_ffn, m_conv_w_pw1, m_conv_b_pw1, m_conv_w_dw, m_conv_b_dw, m_conv_ln_g, m_conv_ln_b, m_conv_w_pw2, m_conv_b_pw2, m_pool_w, m_pool_b, m_pool_scale, m_ffn_w_up, m_ffn_w_dw, m_ffn_b_dw, m_ffn_w_down, m_final_norm, v_norm_mix, v_norm_ffn, v_conv_w_pw1, v_conv_b_pw1, v_conv_w_dw, v_conv_b_dw, v_conv_ln_g, v_conv_ln_b, v_conv_w_pw2, v_conv_b_pw2, v_pool_w, v_pool_b, v_pool_scale, v_ffn_w_up, v_ffn_w_dw, v_ffn_b_dw, v_ffn_w_down, v_final_norm):
    bsz, seq, d = x.shape
    t = bsz * seq
    k_taps = conv_w_dw.shape[1]
    cs1 = conv_w_pw1.shape[2]
    dsh = d // N_DEV
    cg = d // N_GROUPS
    cgs = pool_w.shape[2]
    fu = ffn_w_up.shape[2]
    fd = ffn_w_down.shape[1]
    dff = fd * N_DEV
    nb = N_DEV // 2
    kf = ffn_w_dw.shape[1]
    fsh = ffn_w_dw.shape[2]
    my = _lin(_me())
    tm = _tile(t, 512)

    x2 = x.reshape(t, d)
    tgt2 = loss_target.reshape(t, d)

    small_shapes = [(k_taps, dsh), (dsh,), (dsh,), (2, kf, fsh)]
    small_mine = _pack([conv_w_dw[0], pool_b[0], pool_scale[0], ffn_w_dw], LANE)
    big = [conv_w_pw1[0], conv_w_pw2[0], ffn_w_up[0].T, ffn_w_down[0], pool_w[0], ffn_w_up[1].T, ffn_w_down[1]]
    gather = _exchange_start("gather_start", [small_mine] + [w.astype(BF16) for w in big], [GATHER2] * 8)
    h0 = _rms_fwd("l0_rms", x2, norm_mix[0:1])
    forwarded = _exchange_forward("gather_forward_w1", gather, [0, 1], h0)
    small_all, w1 = _exchange_wait("gather_wait_w1", gather, [0, 1], forwarded)
    parts = [_unpack(small_all[dev], small_shapes) for dev in range(N_DEV)]
    wdw = jnp.concatenate([p[0] for p in parts], axis=1)
    pool_b_full = jnp.concatenate([p[1] for p in parts]).reshape(1, d)
    pool_s_full = jnp.concatenate([p[2] for p in parts]).reshape(1, d)
    fwdw = jnp.concatenate([p[3] for p in parts], axis=2)
    fbdw = ffn_b_dw.reshape(2, 1, dff)

    def columns(w):
        return w.transpose(1, 0, 2).reshape(w.shape[1], N_DEV * w.shape[2])

    def column_shards(w):
        return w.reshape(w.shape[0], N_DEV, w.shape[1] // N_DEV).transpose(1, 0, 2)

    w1 = columns(w1)
    a = _mm("l0_pw1", h0, w1, grid=(2, t // tm, 1),
            a_spec=pl.BlockSpec((tm, d), lambda j, i, k: (i, 0)),
            b_spec=pl.BlockSpec((d, d), lambda j, i, k: (0, j)),
            out_spec=pl.BlockSpec((None, tm, d), lambda j, i, k: (j, i, 0)),
            out_shape=jax.ShapeDtypeStruct((2, t, d), BF16), dims=NN, acc_shape=(tm, d),
            extras=(conv_b_pw1,), extra_specs=(pl.BlockSpec((1, d), lambda j, i, k: (0, j)),),
            epilogue=lambda acc, ex, rows: ((acc + ex[0][...],), ()))
    v = _conv_fwd("l0_conv", a, wdw, conv_b_dw, seq)
    forwarded = _exchange_forward("gather_forward_wu0", gather, [2, 3], v)
    (w2,) = _exchange_wait("gather_wait_w2", gather, [2], forwarded)
    w2 = w2.reshape(d, d)
    row = pl.BlockSpec((tm, d), lambda i, j, k: (i, 0))
    vec = pl.BlockSpec((1, d), lambda i, j, k: (0, 0))
    square = pl.BlockSpec((d, d), lambda i, j, k: (0, 0))

    def ln_silu(v_blk, ex):
        s_blk = _ln_silu_tile(v_blk, ex[0][...], ex[1][...]).astype(BF16)
        return s_blk, s_blk

    def residual_and_norm(acc, ex, rows):
        r_blk = ex[3][rows, :] + (acc + ex[2][...])
        return (r_blk, _rms(r_blk, ex[4][...])), ()

    r1, h1, s = _mm("l0_pw2", v, w2, grid=(t // tm, 1, 1), a_spec=row, b_spec=square, out_spec=[row, row, row],
                    out_shape=[jax.ShapeDtypeStruct((t, d), F32), jax.ShapeDtypeStruct((t, d), BF16),
                               jax.ShapeDtypeStruct((t, d), BF16)],
                    dims=NN, acc_shape=(tm, d), extras=(conv_ln_g, conv_ln_b, conv_b_pw2, x2, norm_ffn[0:1]),
                    extra_specs=(vec, vec, vec, row, vec), prologue=ln_silu, epilogue=residual_and_norm)

    def up_getter(name, idx):
        return lambda after: _exchange_wait(name, gather, [idx], after)[0].reshape(2 * dff, d)

    def down_getter(name, idx, forward=None):
        def get(after):
            if forward is not None:
                after = _exchange_forward(forward[0], gather, forward[1], after)
            return _exchange_wait(name, gather, [idx], after)[0].reshape(dff, d)
        return get

    r2, ffn0_saved = _ffn_forward("f0", r1, h1, up_getter("gather_wait_wu0", 3),
                                  down_getter("gather_wait_wd0", 4, ("gather_forward_wu1", [4, 5, 6])),
                                  fwdw[0], fbdw[0], seq)
    forwarded = _exchange_forward("gather_forward_wd1", gather, [7], r2)
    (wp,) = _exchange_wait("gather_wait_wp", gather, [5], forwarded)
    wp = wp.transpose(1, 0, 2, 3).reshape(N_GROUPS, cg, cg)
    pooled, r3, h3 = _pool_mix_fwd("l1_mix", r2, norm_mix[1:2], wp, pool_s_full, pool_b_full, norm_ffn[1:2], seq)
    (dr4, loss_part, dfinal), ffn1_saved = _ffn_forward(
        "f1", r3, h3, up_getter("gather_wait_wu1", 6), down_getter("gather_wait_wd1", 7), fwdw[1], fbdw[1], seq,
        loss=(tgt2, final_norm.reshape(1, d)))

    dr3, dnf1, dwu1, dwd1, dfw1, dfb1, _ = _ffn_backward("f1", dr4, ffn1_saved, norm_ffn[1:2], fwdw[1], fbdw[1], seq)
    scatter_a = _exchange_start("scatter_f1_start", [dwu1.reshape(N_DEV, fu, d), dwd1.reshape(N_DEV, fd, d)],
                                [SCATTER, SCATTER])
    dr2, dwp, dpool_s, dpool_b, dnm1 = _pool_mix_bwd(
        "l1_dmix", pooled, wp, dr3, r2, norm_mix[1:2], pool_s_full + scatter_a["token"][0:1, 0:1], pool_b_full, seq)
    dr1, dnf0, dwu0, dwd0, dfw0, dfb0, db2 = _ffn_backward("f0", dr2, ffn0_saved, norm_ffn[0:1], fwdw[0], fbdw[0], seq)
    dwp_b = dwp.astype(BF16).reshape(N_GROUPS, N_DEV, cgs, cg).transpose(1, 0, 2, 3)
    tk = _tile(t, 2048)
    dw2 = _mm("l0_dw2", s, dr1, grid=(1, 1, t // tk),
              a_spec=pl.BlockSpec((tk, d), lambda i, j, k: (k, 0)),
              b_spec=pl.BlockSpec((tk, d), lambda i, j, k: (k, 0)),
              out_spec=pl.BlockSpec((d, d), lambda i, j, k: (0, 0)),
              out_shape=jax.ShapeDtypeStruct((d, d), BF16), dims=TN, acc_shape=(d, d))
    scatter_b = _exchange_start("scatter_f0_start", [dwu0.reshape(N_DEV, fu, d), dwd0.reshape(N_DEV, fd, d), dwp_b,
                                                     dw2.reshape(N_DEV, d // N_DEV, d)], [SCATTER] * 4)

    def ln_silu_backward(acc, ex, rows):
        dv_blk, dgain, dbias, colsum = _ln_silu_bwd_tile(acc, ex[0][rows, :], ex[1][...], ex[2][...])
        return (dv_blk,), (dgain, dbias, colsum)

    dv, dlg, dlb, dbdw = _mm("l0_ds", dr1, w2, grid=(t // tm, 1, 1), a_spec=row, b_spec=square,
                             out_spec=[row, vec, vec, vec],
                             out_shape=[jax.ShapeDtypeStruct((t, d), F32)] + [jax.ShapeDtypeStruct((1, d), F32)] * 3,
                             dims=NT, acc_shape=(tm, d), extras=(v, conv_ln_g, conv_ln_b), extra_specs=(row, vec, vec),
                             epilogue=ln_silu_backward, n_sums=3, token=scatter_b["token"])
    da, dwdw, db1 = _conv_bwd("l0_dconv", a, dv, wdw, seq)
    tk1 = _tile(t, 4096)
    once = dict(pipeline_mode=pl.Buffered(1)) if tk1 == t else {}
    dw1 = _mm("l0_dw1", h0, da, grid=(1, 2, t // tk1),
              a_spec=pl.BlockSpec((tk1, d), lambda i, j, k: (k, 0), **once),
              b_spec=pl.BlockSpec((None, tk1, d), lambda i, j, k: (j, k, 0)),
              out_spec=pl.BlockSpec((d, d), lambda i, j, k: (0, j)),
              out_shape=jax.ShapeDtypeStruct((d, 2 * d), BF16), dims=TN, acc_shape=(d, d))
    scatter_c = _exchange_start("scatter_l0_start", [column_shards(dw1)], [SCATTER])
    def norm_backward(acc, ex, rows):
        dx_blk, dgain, colsum = _rms_bwd_tile(acc, ex[0][rows, :], ex[1][...], ex[2][rows, :])
        return (dx_blk,), (dgain, colsum)

    dx, dnm0, _ = _mm("l0_dh", da, w1, grid=(t // tm, 1, 1),
                      a_spec=pl.BlockSpec((2, tm, d), lambda i, j, k: (0, i, 0)),
                      b_spec=pl.BlockSpec((d, 2 * d), lambda i, j, k: (0, 0), pipeline_mode=pl.Buffered(1)),
                      out_spec=[row, vec, vec],
                      out_shape=[jax.ShapeDtypeStruct((t, d), F32)] + [jax.ShapeDtypeStruct((1, d), F32)] * 2,
                      dims=NT, acc_shape=(tm, d), extras=(x2, norm_mix[0:1], dr1), extra_specs=(row, vec, row),
                      epilogue=norm_backward, n_sums=2, parts=2, token=scatter_c["token"])

    dffn_w = jnp.stack([dfw0, dfw1])
    dffn_b = jnp.stack([dfb0, dfb1]).reshape(2, dff)
    small_parts = [loss_part, jnp.concatenate([dnm0, dnm1]), jnp.concatenate([dnf0, dnf1]), db1, dwdw, dbdw, dlg, dlb,
                   db2, dpool_b, dpool_s, dffn_w, dffn_b, dfinal]
    small_part_shapes = [(1,), (2, d), (2, d), (1, 2 * d), (k_taps, d), (1, d), (1, d), (1, d), (1, d), (1, d), (1, d),
                         (2, kf, dff), (2, dff), (d,)]
    packed = _pack(small_parts, 8 * LANE)
    gather_small = _exchange_start("gather_small_start", [packed], [GATHER])

    def big_update(name, recv, w, m, v, layer=0, prev=None):
        shape = w.shape
        c = recv.shape[-1]
        rows = recv.size // (N_DEV * c)
        nl = w.size // (rows * c)
        outs = _adamw(name, recv.reshape(N_DEV, rows, c), w.reshape(nl, rows, c), m.reshape(nl, rows, c),
                      v.reshape(nl, rows, c), layer, prev)
        return outs, [o.reshape(shape) for o in outs]

    wu_t = [p.transpose(0, 2, 1) for p in (ffn_w_up, m_ffn_w_up, v_ffn_w_up)]
    g_wu1, g_wd1 = _exchange_wait("scatter_f1_wait", scatter_a, [0, 1], gather_small["token"])
    raw_wu, _ = big_update("adam_wu1", g_wu1, *wu_t, 1)
    raw_wd, _ = big_update("adam_wd1", g_wd1, ffn_w_down, m_ffn_w_down, v_ffn_w_down, 1)
    g_wu0, g_wd0, g_wp, g_w2 = _exchange_wait("scatter_f0_wait", scatter_b, [0, 1, 2, 3], raw_wd[0])
    _, u_wu = big_update("adam_wu0", g_wu0, *wu_t, 0, raw_wu)
    u_wu = [o.transpose(0, 2, 1) for o in u_wu]
    _, u_wd = big_update("adam_wd0", g_wd0, ffn_w_down, m_ffn_w_down, v_ffn_w_down, 0, raw_wd)
    _, u_wp = big_update("adam_wp", g_wp, pool_w, m_pool_w, v_pool_w)
    (g_w1,) = _exchange_wait("scatter_l0_wait", scatter_c, [0], u_wp[0])
    _, u_w1 = big_update("adam_w1", g_w1, conv_w_pw1, m_conv_w_pw1, v_conv_w_pw1)
    _, u_w2 = big_update("adam_w2", g_w2, conv_w_pw2, m_conv_w_pw2, v_conv_w_pw2)
    (all_small,) = _exchange_wait("gather_small_wait", gather_small, [0], u_w2[0])
    summed = _sum_rows("sum_small_grads", all_small)
    (loss_v, g_nm, g_nf, g_b1, g_wdw, g_bdw, g_lg, g_lb, g_b2, g_pb, g_ps, g_fw, g_fb,
     g_fin) = _unpack(summed, small_part_shapes)
    loss = loss_v[0]
    g_wdw_mine = lax.dynamic_slice_in_dim(g_wdw, my * dsh, dsh, axis=1)[None]
    g_pb_mine = lax.dynamic_slice_in_dim(g_pb, my * dsh, dsh, axis=1)
    g_ps_mine = lax.dynamic_slice_in_dim(g_ps, my * dsh, dsh, axis=1)
    g_fw_mine = lax.dynamic_slice_in_dim(g_fw, my * fsh, fsh, axis=2)

    small_g =[g_nm, g_nf, g_b1, g_wdw_mine, g_bdw, g_lg, g_lb, g_b2, g_pb_mine, g_ps_mine, g_fw_mine, g_fb, g_fin]
    small_w = [norm_mix, norm_ffn, conv_b_pw1, conv_w_dw, conv_b_dw, conv_ln_g, conv_ln_b, conv_b_pw2, pool_b,
               pool_scale, ffn_w_dw, ffn_b_dw, final_norm]
    small_m = [m_norm_mix, m_norm_ffn, m_conv_b_pw1, m_conv_w_dw, m_conv_b_dw, m_conv_ln_g, m_conv_ln_b,
               m_conv_b_pw2, m_pool_b, m_pool_scale, m_ffn_w_dw, m_ffn_b_dw, m_final_norm]
    small_v = [v_norm_mix, v_norm_ffn, v_conv_b_pw1, v_conv_w_dw, v_conv_b_dw, v_conv_ln_g, v_conv_ln_b,
               v_conv_b_pw2, v_pool_b, v_pool_scale, v_ffn_w_dw, v_ffn_b_dw, v_final_norm]
    shapes = [w.shape for w in small_w]
    outs = _adamw("adam_small", _pack(small_g, 8 * LANE)[None], _pack(small_w, 8 * LANE)[None],
                  _pack(small_m, 8 * LANE)[None], _pack(small_v, 8 * LANE)[None])
    sg, sd, sm, sv = [_unpack(o, shapes) for o in outs]

    def leaf(kind):
        (nm, nf, b1, wdw_, bdw_, lg, lb, b2, pb, ps, fw, fb, fin) = (sg, sd, sm, sv)[kind]
        return [nm, nf, u_w1[kind], b1, wdw_, bdw_, lg, lb, u_w2[kind], b2, u_wp[kind], pb, ps, u_wu[kind], fw, fb,
                u_wd[kind], fin]

    return (loss, dx.reshape(bsz, seq, d), *leaf(0), *leaf(1), *leaf(2), *leaf(3))
```

```python
import functools

import jax
import jax.numpy as jnp
from jax import lax
from jax.experimental import pallas as pl
from jax.experimental.pallas import tpu as pltpu

F32 = jnp.float32
BF16 = jnp.bfloat16
MESH = pl.DeviceIdType.MESH
HBM = pl.BlockSpec(memory_space=pltpu.HBM)

N_DEV = 8
RMS_EPS = 1e-6
LN_EPS = 1e-5
POOL_WINDOWS = (2, 4, 8, 16)
N_GROUPS = len(POOL_WINDOWS)
ADAM_LR = 0.001
ADAM_B1 = 0.9
ADAM_B2 = 0.999
ADAM_EPS = 1e-08
ADAM_WD = 0.01
ADAM_STEP = 10

LANE = 128
HALO = 32
HALO16 = 16
VMEM_LIMIT = 56 * 1024 * 1024


def _params(*sem):
    return pltpu.CompilerParams(dimension_semantics=sem if sem else None, vmem_limit_bytes=VMEM_LIMIT)


def _tile(n, pref):
    for t in range(min(pref, n), 15, -1):
        if n % t == 0 and t % 16 == 0:
            return t
    return n


def _sigmoid(z):
    return 1.0 / (1.0 + jnp.exp(-z))


def _me():
    return lax.axis_index("x"), lax.axis_index("y"), lax.axis_index("c")


def _flip(pos, m):
    x, y, c = pos
    return ((1 - x) if m & 4 else x, (1 - y) if m & 2 else y, (1 - c) if m & 1 else c)


def _lin(pos):
    return 4 * pos[0] + 2 * pos[1] + pos[2]


SEM = pl.BlockSpec(memory_space=pltpu.SEMAPHORE)
ANY = pl.BlockSpec(memory_space=pl.ANY)
EFFECT = pltpu.SideEffectType.DATAFLOW_SIDE_EFFECTING


SCATTER = "scatter"
GATHER = "gather"
GATHER2 = "gather2"
ALL_MASKS = (1, 2, 3, 4, 5, 6, 7)
SIBLING = 1
CHIPS = (2, 4, 6)


class _Copies:
    def __init__(self, a, mode, src, land, send_sems, recv_sems):
        self.a, self.mode, self.src, self.land = a, mode, src, land
        self.send_sems, self.recv_sems = send_sems, recv_sems
        self.me = _me()
        self.first = (SIBLING,) + CHIPS if mode == GATHER2 else ALL_MASKS

    def _sems(self, m, to):
        return dict(send_sem=self.send_sems.at[self.a * N_DEV + m], recv_sem=self.recv_sems.at[self.a * N_DEV + m],
                    device_id=to, device_id_type=MESH)

    def _block(self, pid):
        return self.src.at[pid] if self.mode == SCATTER else self.src

    def local(self):
        my = _lin(self.me)
        return pltpu.make_async_copy(self._block(my), self.land.at[my], self.send_sems.at[self.a * N_DEV])

    def send(self, m):
        peer = _flip(self.me, m)
        return pltpu.make_async_remote_copy(src_ref=self._block(_lin(peer)), dst_ref=self.land.at[_lin(self.me)],
                                            **self._sems(m, peer))

    def arrival(self, m):
        rows = self.land.at[_lin(_flip(self.me, m))]
        return pltpu.make_async_remote_copy(src_ref=rows, dst_ref=rows, **self._sems(m, _flip(self.me, m)))

    def forward(self, m):
        rows = self.land.at[_lin(_flip(self.me, m))]
        return pltpu.make_async_remote_copy(src_ref=rows, dst_ref=rows, **self._sems(m | 1, _flip(self.me, SIBLING)))


def _exchange_start(name, arrs, modes):
    n = len(arrs)
    blocks = [a.shape[1:] if md == SCATTER else a.shape for a, md in zip(arrs, modes)]

    def body(*refs):
        srcs, lands = refs[:n], refs[n:2 * n]
        send_sems, recv_sems = refs[2 * n], refs[2 * n + 1]
        token = refs[-1]
        for a in range(n):
            cp = _Copies(a, modes[a], srcs[a], lands[a], send_sems, recv_sems)
            cp.local().start()
            for m in cp.first:
                cp.send(m).start()
        token[...] = jnp.zeros_like(token)

    lands = [lax.empty((N_DEV,) + tuple(b), a.dtype) for a, b in zip(arrs, blocks)]
    outs = pl.pallas_call(
        body, name=name,
        out_shape=(pltpu.SemaphoreType.DMA((n * N_DEV,)), pltpu.SemaphoreType.DMA((n * N_DEV,)),
                   *[pltpu.HBM(a.shape, a.dtype) for a in arrs], *[pltpu.HBM(l.shape, l.dtype) for l in lands],
                   jax.ShapeDtypeStruct((8, LANE), F32)),
        in_specs=[HBM] * (2 * n),
        out_specs=(SEM, SEM, *[HBM] * (2 * n), pl.BlockSpec(memory_space=pltpu.VMEM)),
        input_output_aliases={i: 2 + i for i in range(2 * n)},
        compiler_params=pltpu.CompilerParams(has_side_effects=EFFECT),
    )(*[pltpu.with_memory_space_constraint(a, pltpu.HBM) for a in arrs],
      *[pltpu.with_memory_space_constraint(l, pltpu.HBM) for l in lands])
    return dict(send=outs[0], recv=outs[1], srcs=list(outs[2:2 + n]), lands=list(outs[2 + n:2 + 2 * n]),
                modes=modes, token=outs[-1])


def _exchange_forward(name, handle, which, after):
    k = len(which)

    def half(wait):
        def body(*refs):
            lands = refs[:k]
            send_sems, recv_sems = refs[k], refs[k + 1]
            token = refs[-1]
            for pos, a in enumerate(which):
                cp = _Copies(a, GATHER2, None, lands[pos], send_sems, recv_sems)
                for m in CHIPS:
                    if wait:
                        cp.arrival(m).wait_recv()
                    else:
                        cp.forward(m).start()
            token[...] = jnp.zeros_like(token)
        return body

    def call(body, call_name, lands, after):
        outs = pl.pallas_call(
            body, name=call_name,
            out_shape=(*[pltpu.HBM(x.shape, x.dtype) for x in lands], jax.ShapeDtypeStruct((8, LANE), F32)),
            in_specs=[HBM] * k + [SEM, SEM, ANY], out_specs=(*[HBM] * k, pl.BlockSpec(memory_space=pltpu.VMEM)),
            input_output_aliases={i: i for i in range(k)},
            compiler_params=pltpu.CompilerParams(has_side_effects=EFFECT),
        )(*lands, handle["send"], handle["recv"], after)
        return list(outs[:k]), outs[-1]

    lands, arrived = call(half(True), name + "_arrived", [handle["lands"][a] for a in which], after)
    lands, token = call(half(False), name, lands, arrived)
    for pos, a in enumerate(which):
        handle["lands"][a] = lands[pos]
    return token


def _exchange_wait(name, handle, which, after):
    k = len(which)
    modes = handle["modes"]

    def body(*refs):
        srcs, lands = refs[:k], refs[k:2 * k]
        send_sems, recv_sems = refs[2 * k], refs[2 * k + 1]
        for pos, a in enumerate(which):
            cp = _Copies(a, modes[a], srcs[pos], lands[pos], send_sems, recv_sems)
            cp.local().wait()
            for m in cp.first:
                cp.send(m).wait_send()
            if modes[a] == GATHER2:
                for m in CHIPS:
                    cp.forward(m).wait_send()
                arrivals = (SIBLING,) + tuple(m | 1 for m in CHIPS)
            else:
                arrivals = ALL_MASKS
            for m in arrivals:
                cp.arrival(m).wait_recv()

    srcs = [handle["srcs"][a] for a in which]
    lands = [handle["lands"][a] for a in which]
    outs = pl.pallas_call(
        body, name=name,
        out_shape=tuple(pltpu.HBM(x.shape, x.dtype) for x in srcs + lands),
        in_specs=[HBM] * (2 * k) + [SEM, SEM, ANY], out_specs=tuple([HBM] * (2 * k)),
        input_output_aliases={i: i for i in range(2 * k)},
        compiler_params=pltpu.CompilerParams(has_side_effects=EFFECT),
    )(*srcs, *lands, handle["send"], handle["recv"], after)
    for pos, a in enumerate(which):
        handle["srcs"][a], handle["lands"][a] = outs[pos], outs[k + pos]
    return list(outs[k:])


def _mm(name, a, b, *, grid, a_spec, b_spec, out_spec, out_shape, dims, acc_shape, extras=(), extra_specs=(),
        epilogue=None, token=None, prologue=None, n_sums=0, parts=1):
    nk = grid[2]
    ne = len(extras)
    deps = () if token is None else (token,)
    dep_specs = [pl.BlockSpec((8, LANE), lambda i, j, k: (0, 0))] * len(deps)
    n_out = len(out_shape) if isinstance(out_shape, (list, tuple)) else 1
    n_tiles = n_out - n_sums - (1 if prologue is not None else 0)

    def body(a_ref, b_ref, *rest):
        ex, o_refs, acc_ref = rest[:ne], rest[ne + len(deps):ne + len(deps) + n_out], rest[ne + len(deps) + n_out]
        k = pl.program_id(2)
        if parts == 1:
            a_blk, saved = a_ref[...], None
            if prologue is not None:
                a_blk, saved = prologue(a_blk, ex)
                o_refs[n_tiles][...] = saved
            part = lax.dot_general(a_blk.astype(BF16), b_ref[...].astype(BF16), (dims, ((), ())),
                                   preferred_element_type=F32)
        else:
            kb = b_ref.shape[dims[1][0]] // parts
            part = None
            for p in range(parts):
                b_blk = b_ref[p * kb:(p + 1) * kb, :] if dims[1][0] == 0 else b_ref[:, p * kb:(p + 1) * kb]
                term = lax.dot_general(a_ref[p].astype(BF16), b_blk.astype(BF16), (dims, ((), ())),
                                       preferred_element_type=F32)
                part = term if part is None else part + term
        sum_refs = o_refs[n_out - n_sums:]

        def add_sums(terms):
            @pl.when((pl.program_id(0) == 0) & (pl.program_id(1) == 0))
            def _():
                for o_ref in sum_refs:
                    o_ref[...] = jnp.zeros_like(o_ref)

            for o_ref, term in zip(sum_refs, terms):
                o_ref[...] += jnp.sum(term, axis=0, keepdims=True)

        def finish(r):
            tiles, terms = ((r,), ()) if epilogue is None else epilogue(r, ex, slice(None))
            for o_ref, val in zip(o_refs, tiles):
                o_ref[...] = val.astype(o_ref.dtype)
            if n_sums:
                add_sums(terms)

        if nk == 1:
            finish(part)
            return

        @pl.when(k == 0)
        def _():
            acc_ref[...] = part

        @pl.when((k > 0) & (k < nk - 1))
        def _():
            acc_ref[...] += part

        @pl.when(k == nk - 1)
        def _():
            finish(acc_ref[...] + part)

    return pl.pallas_call(
        body, name=name, grid=grid, in_specs=[a_spec, b_spec, *extra_specs, *dep_specs], out_specs=out_spec,
        out_shape=out_shape, scratch_shapes=[pltpu.VMEM(acc_shape if nk > 1 else (8, LANE), F32)],
        compiler_params=_params(*(("arbitrary",) * 3 if n_sums else ("parallel", "parallel", "arbitrary"))),
    )(a, b, *extras, *deps)


def _rms(x, gain):
    return x * lax.rsqrt(jnp.mean(x * x, axis=-1, keepdims=True) + RMS_EPS) * gain


def _rms_bwd_tile(dh, x, gain, dres):
    rstd = lax.rsqrt(jnp.mean(x * x, axis=-1, keepdims=True) + RMS_EPS)
    xhat = x * rstd
    dxhat = dh * gain
    dx = dres + rstd * (dxhat - xhat * jnp.mean(dxhat * xhat, axis=-1, keepdims=True))
    return dx, dh * xhat, dx


def _ln_silu_tile(v, g, b):
    mu = jnp.mean(v, axis=-1, keepdims=True)
    cen = v - mu
    z = cen * lax.rsqrt(jnp.mean(cen * cen, axis=-1, keepdims=True) + LN_EPS) * g + b
    return z * _sigmoid(z)


def _ln_silu_bwd_tile(ds, v, g, b):
    mu = jnp.mean(v, axis=-1, keepdims=True)
    cen = v - mu
    rstd = lax.rsqrt(jnp.mean(cen * cen, axis=-1, keepdims=True) + LN_EPS)
    y = cen * rstd
    z = y * g + b
    sig = _sigmoid(z)
    dz = ds * sig * (1.0 + z * (1.0 - sig))
    dy = dz * g
    dv = rstd * (dy - jnp.mean(dy, axis=-1, keepdims=True) - y * jnp.mean(dy * y, axis=-1, keepdims=True))
    return dv, dz * y, dz, dv


def _loss_tile(x, tgt, gain):
    d = x.shape[-1]
    rstd = lax.rsqrt(jnp.mean(x * x, axis=-1, keepdims=True) + RMS_EPS)
    xhat = x * rstd
    err = xhat * gain - tgt
    dy = err / d
    dxhat = dy * gain
    dx = rstd * (dxhat - xhat * jnp.mean(dxhat * xhat, axis=-1, keepdims=True))
    return dx, 0.5 * jnp.mean(err * err, axis=-1, keepdims=True), dy * xhat


NN = ((1,), (0,))
NT = ((1,), (1,))
TN = ((0,), (0,))


def _rms_fwd(name, x, gain):
    t, d = x.shape
    tr = _tile(t, 512)

    def body(x_ref, g_ref, h_ref):
        h_ref[...] = _rms(x_ref[...], g_ref[...]).astype(BF16)

    return pl.pallas_call(
        body, name=name, grid=(t // tr,),
        in_specs=[pl.BlockSpec((tr, d), lambda i: (i, 0)), pl.BlockSpec((1, d), lambda i: (0, 0))],
        out_specs=pl.BlockSpec((tr, d), lambda i: (i, 0)),
        out_shape=jax.ShapeDtypeStruct((t, d), BF16), compiler_params=_params("parallel"),
    )(x, gain)


def _conv_tiles(t, seq):
    ts = _tile(seq, 1024)
    return ts, seq // ts, _tile(ts, 64)


def _conv_fwd(name, a, w, b, seq):
    _, t, d = a.shape
    k_taps = w.shape[0]
    ts, tps, rc = _conv_tiles(t, seq)
    hb = ts // HALO

    def body(cur_ref, prev_ref, w_ref, b_ref, v_ref, upad):
        i = pl.program_id(1)
        first = (i % tps) == 0
        pv = prev_ref[0].astype(F32)
        pg = prev_ref[1].astype(F32)
        upad[0:HALO, :] = jnp.where(first, 0.0, pv * _sigmoid(pg))
        upad[HALO:HALO + ts, :] = cur_ref[0].astype(F32) * _sigmoid(cur_ref[1].astype(F32))
        wv = w_ref[...]
        bias = jnp.broadcast_to(b_ref[...], (rc, LANE))
        for r0 in range(0, ts, rc):
            acc = bias
            for k in range(k_taps):
                acc = acc + wv[k:k + 1, :] * upad[pl.ds(HALO - (k_taps - 1) + k + r0, rc), :]
            v_ref[pl.ds(r0, rc), :] = acc

    return pl.pallas_call(
        body, name=name, grid=(d // LANE, t // ts),
        in_specs=[pl.BlockSpec((2, ts, LANE), lambda c, i: (0, i, c)),
                  pl.BlockSpec((2, HALO, LANE), lambda c, i: (0, jnp.maximum(i * hb - 1, 0), c)),
                  pl.BlockSpec((k_taps, LANE), lambda c, i: (0, c)),
                  pl.BlockSpec((1, LANE), lambda c, i: (0, c))],
        out_specs=pl.BlockSpec((ts, LANE), lambda c, i: (i, c)),
        out_shape=jax.ShapeDtypeStruct((t, d), F32),
        scratch_shapes=[pltpu.VMEM((HALO + ts, LANE), F32)],
        compiler_params=_params("parallel", "parallel"),
    )(a, a, w, b)


def _conv_bwd(name, a, dv, w, seq):
    _, t, d = a.shape
    k_taps = w.shape[0]
    ts, tps, rc = _conv_tiles(t, seq)
    hb = ts // HALO
    nhb = t // HALO

    def body(cur_ref, prev_ref, dv_ref, ndv_ref, w_ref, da_ref, dw_ref, dbp_ref, upad, dvpad, dwrows):
        i = pl.program_id(1)
        first = (i % tps) == 0
        last = (i % tps) == tps - 1
        pv = prev_ref[0].astype(F32)
        pg = prev_ref[1].astype(F32)
        upad[0:HALO, :] = jnp.where(first, 0.0, pv * _sigmoid(pg))
        upad[HALO:HALO + ts, :] = cur_ref[0].astype(F32) * _sigmoid(cur_ref[1].astype(F32))
        dvpad[0:ts, :] = dv_ref[...]
        dvpad[ts:ts + HALO, :] = jnp.where(last, 0.0, ndv_ref[...])
        wv = w_ref[...]

        @pl.when(i == 0)
        def _():
            dw_ref[...] = jnp.zeros_like(dw_ref)
            dbp_ref[...] = jnp.zeros_like(dbp_ref)

        sv = jnp.zeros((1, LANE), F32)
        sg = jnp.zeros((1, LANE), F32)
        for r0 in range(0, ts, rc):
            du = jnp.zeros((rc, LANE), F32)
            for k in range(k_taps):
                du = du + wv[k:k + 1, :] * dvpad[pl.ds(r0 + (k_taps - 1) - k, rc), :]
            av = cur_ref[0, pl.ds(r0, rc), :].astype(F32)
            sig = _sigmoid(cur_ref[1, pl.ds(r0, rc), :].astype(F32))
            dval = du * sig
            dgate = du * av * sig * (1.0 - sig)
            da_ref[0, pl.ds(r0, rc), :] = dval.astype(BF16)
            da_ref[1, pl.ds(r0, rc), :] = dgate.astype(BF16)
            sv = sv + jnp.sum(dval, axis=0, keepdims=True)
            sg = sg + jnp.sum(dgate, axis=0, keepdims=True)
        dbp_ref[0] += sv
        dbp_ref[1] += sg

        for k in range(k_taps):
            acc = jnp.zeros((rc, LANE), F32)
            for r0 in range(0, ts, rc):
                acc = acc + dvpad[pl.ds(r0, rc), :] * upad[pl.ds(HALO - (k_taps - 1) + k + r0, rc), :]
            dwrows[k:k + 1, :] = jnp.sum(acc, axis=0, keepdims=True)
        dw_ref[...] += dwrows[0:k_taps, :]

    return pl.pallas_call(
        body, name=name, grid=(d // LANE, t // ts),
        in_specs=[pl.BlockSpec((2, ts, LANE), lambda c, i: (0, i, c)),
                  pl.BlockSpec((2, HALO, LANE), lambda c, i: (0, jnp.maximum(i * hb - 1, 0), c)),
                  pl.BlockSpec((ts, LANE), lambda c, i: (i, c)),
                  pl.BlockSpec((HALO, LANE), lambda c, i: (jnp.minimum((i + 1) * hb, nhb - 1), c)),
                  pl.BlockSpec((k_taps, LANE), lambda c, i: (0, c))],
        out_specs=[pl.BlockSpec((2, ts, LANE), lambda c, i: (0, i, c)),
                   pl.BlockSpec((k_taps, LANE), lambda c, i: (0, c)),
                   pl.BlockSpec((2, 1, LANE), lambda c, i: (0, 0, c))],
        out_shape=[jax.ShapeDtypeStruct((2, t, d), BF16), jax.ShapeDtypeStruct((k_taps, d), F32),
                   jax.ShapeDtypeStruct((2, 1, d), F32)],
        scratch_shapes=[pltpu.VMEM((HALO + ts, LANE), F32), pltpu.VMEM((ts + HALO, LANE), F32),
                        pltpu.VMEM((HALO, LANE), F32)],
        compiler_params=_params("parallel", "arbitrary"),
    )(a, a, dv, dv, w)


def _pool_mix_fwd(name, x, gain, wp, scale, bias, next_gain, seq):
    t, d = x.shape
    ts = _tile(seq, 256)
    tps = seq // ts
    hb = ts // HALO
    cg = d // N_GROUPS
    sw = min(cg, LANE)

    def body(cur_ref, prev_ref, g_ref, w_ref, s_ref, b_ref, ng_ref, p_ref, r_ref, h_ref, hpad):
        i = pl.program_id(0)
        first = (i % tps) == 0
        g = g_ref[...]
        h_prev = jnp.where(first, 0.0, _rms(prev_ref[...], g))
        h_cur = _rms(cur_ref[...], g)
        for si in range(d // sw):
            hpad[si, 0:HALO, :] = h_prev[:, si * sw:(si + 1) * sw]
            hpad[si, HALO:HALO + ts, :] = h_cur[:, si * sw:(si + 1) * sw]
        pos = (i % tps) * ts + lax.broadcasted_iota(jnp.int32, (ts, 1), 0)
        for gi, win in enumerate(POOL_WINDOWS):
            sl = slice(gi * cg, (gi + 1) * cg)
            cnt = jnp.minimum(pos + 1, win).astype(F32)
            for si in range(gi * cg // sw, (gi + 1) * cg // sw):
                own = hpad[si, HALO:HALO + ts, :]
                acc = own
                for j in range(1, win):
                    acc = acc + hpad[si, pl.ds(HALO - j, ts), :]
                p_ref[:, si * sw:(si + 1) * sw] = (acc / cnt - own).astype(BF16)
            mixed = jnp.dot(p_ref[:, sl], w_ref[gi], preferred_element_type=F32)
            r_ref[:, sl] = cur_ref[:, sl] + s_ref[:, sl] * (mixed + b_ref[:, sl])
        h_ref[...] = _rms(r_ref[...], ng_ref[...]).astype(BF16)

    row = pl.BlockSpec((ts, d), lambda i: (i, 0))
    vec = pl.BlockSpec((1, d), lambda i: (0, 0))
    return pl.pallas_call(
        body, name=name, grid=(t // ts,),
        in_specs=[row, pl.BlockSpec((HALO, d), lambda i: (jnp.maximum(i * hb - 1, 0), 0)), vec,
                  pl.BlockSpec((N_GROUPS, cg, cg), lambda i: (0, 0, 0)), vec, vec, vec],
        out_specs=[row, row, row],
        out_shape=[jax.ShapeDtypeStruct((t, d), BF16), jax.ShapeDtypeStruct((t, d), F32),
                   jax.ShapeDtypeStruct((t, d), BF16)],
        scratch_shapes=[pltpu.VMEM((d // sw, HALO + ts, sw), F32)],
        compiler_params=_params("parallel"),
    )(x, x, gain, wp, scale, bias, next_gain)


def _pool_mix_bwd(name, pooled, wp, dr, x, gain, scale, bias, seq):
    t, d = x.shape
    ts = _tile(seq, 256)
    tps = seq // ts
    hb = ts // HALO
    nhb = t // HALO
    cg = d // N_GROUPS
    sw = min(cg, LANE)

    def body(p_ref, w_ref, dr_ref, ndr_ref, x_ref, g_ref, s_ref, b_ref, dx_ref, dw_ref, ds_ref, db_ref, dg_ref,
             qpad, dh):
        i = pl.program_id(0)
        last = (i % tps) == tps - 1
        pos = (i % tps) * ts + lax.broadcasted_iota(jnp.int32, (ts, 1), 0)

        @pl.when(i == 0)
        def _():
            dw_ref[...] = jnp.zeros_like(dw_ref)
            ds_ref[...] = jnp.zeros_like(ds_ref)
            db_ref[...] = jnp.zeros_like(db_ref)
            dg_ref[...] = jnp.zeros_like(dg_ref)

        for gi, win in enumerate(POOL_WINDOWS):
            sl = slice(gi * cg, (gi + 1) * cg)
            wv = w_ref[gi]
            sc = s_ref[:, sl]
            drv = dr_ref[:, sl]
            dmx = drv * sc
            dmx16 = dmx.astype(BF16)
            pooled = p_ref[:, sl]
            dw_ref[gi] += lax.dot_general(pooled, dmx16, (TN, ((), ())), preferred_element_type=F32)
            mixed = jnp.dot(pooled, wv, preferred_element_type=F32)
            ds_ref[:, sl] += jnp.sum(drv * (mixed + b_ref[:, sl]), axis=0, keepdims=True)
            db_ref[:, sl] += jnp.sum(dmx, axis=0, keepdims=True)
            cur = lax.dot_general(dmx16, wv, (NT, ((), ())), preferred_element_type=F32)
            nxt = lax.dot_general((ndr_ref[:, sl] * sc).astype(BF16), wv, (NT, ((), ())),
                                  preferred_element_type=F32)
            q_cur = cur / jnp.minimum(pos + 1, win).astype(F32)
            q_nxt = jnp.where(last, 0.0, nxt / float(win))
            for k, si in enumerate(range(gi * cg // sw, (gi + 1) * cg // sw)):
                part = slice(k * sw, (k + 1) * sw)
                qpad[si, 0:ts, :] = q_cur[:, part]
                qpad[si, ts:ts + HALO, :] = q_nxt[:, part]
                acc = -cur[:, part]
                for j in range(win):
                    acc = acc + qpad[si, pl.ds(j, ts), :]
                dh[:, si * sw:(si + 1) * sw] = acc
        dx, dgain_term, _ = _rms_bwd_tile(dh[...], x_ref[...], g_ref[...], dr_ref[...])
        dx_ref[...] = dx
        dg_ref[...] += jnp.sum(dgain_term, axis=0, keepdims=True)

    row = pl.BlockSpec((ts, d), lambda i: (i, 0))
    vec = pl.BlockSpec((1, d), lambda i: (0, 0))
    return pl.pallas_call(
        body, name=name, grid=(t // ts,),
        in_specs=[row, pl.BlockSpec((N_GROUPS, cg, cg), lambda i: (0, 0, 0)), row,
                  pl.BlockSpec((HALO, d), lambda i: (jnp.minimum((i + 1) * hb, nhb - 1), 0)), row, vec, vec, vec],
        out_specs=[row, pl.BlockSpec((N_GROUPS, cg, cg), lambda i: (0, 0, 0)), vec, vec, vec],
        out_shape=[jax.ShapeDtypeStruct((t, d), F32), jax.ShapeDtypeStruct((N_GROUPS, cg, cg), F32)]
        + [jax.ShapeDtypeStruct((1, d), F32)] * 3,
        scratch_shapes=[pltpu.VMEM((d // sw, ts + HALO, sw), F32), pltpu.VMEM((ts, d), F32)],
        compiler_params=_params("arbitrary"),
    )(pooled, wp, dr, dr, x, gain, scale, bias)


def _ctile(n, pref):
    return max(c for c in range(LANE, min(pref, n) + 1, LANE) if n % c == 0)


FFN_COLS = 1408
FFN_ROWS = 32
FFN_TILE = 1024


def _ffn_fwd(name, up, w, b, seq):
    _, t, dff = up.shape
    f = _ctile(dff, FFN_COLS)
    k_taps = w.shape[0]
    ts = _tile(seq, FFN_TILE)
    tps = seq // ts
    hb = ts // HALO16
    rc = _tile(ts, FFN_ROWS)

    def body(cur_ref, prev_ref, w_ref, b_ref, g_ref, apad):
        i = pl.program_id(1)
        first = (i % tps) == 0
        for ci, c0 in enumerate(range(0, f, LANE)):
            cols = slice(c0, c0 + LANE)
            apad[ci, 0:HALO16, :] = jnp.where(first, 0.0, prev_ref[:, cols].astype(F32))
            apad[ci, HALO16:HALO16 + ts, :] = cur_ref[0, :, cols].astype(F32)
            wv = w_ref[:, cols]
            wk = [jnp.broadcast_to(wv[k:k + 1, :], (rc, LANE)) for k in range(k_taps)]
            bias = jnp.broadcast_to(b_ref[:, cols], (rc, LANE))
            for r0 in range(0, ts, rc):
                c = bias
                for k in range(k_taps):
                    c = c + wk[k] * apad[ci, pl.ds(HALO16 - (k_taps - 1) + k + r0, rc), :]
                gate = cur_ref[1, pl.ds(r0, rc), cols].astype(F32)
                g_ref[pl.ds(r0, rc), cols] = (c * _sigmoid(c) * gate).astype(BF16)

    return pl.pallas_call(
        body, name=name, grid=(dff // f, t // ts),
        in_specs=[pl.BlockSpec((2, ts, f), lambda j, i: (0, i, j)),
                  pl.BlockSpec((None, HALO16, f), lambda j, i: (0, jnp.maximum(i * hb - 1, 0), j)),
                  pl.BlockSpec((k_taps, f), lambda j, i: (0, j)),
                  pl.BlockSpec((1, f), lambda j, i: (0, j))],
        out_specs=pl.BlockSpec((ts, f), lambda j, i: (i, j)),
        out_shape=jax.ShapeDtypeStruct((t, dff), BF16),
        scratch_shapes=[pltpu.VMEM((f // LANE, HALO16 + ts, LANE), F32)],
        compiler_params=_params("parallel", "parallel"),
    )(up, up, w, b)


def _ffn_bwd(name, up, dg, w, b, seq):
    _, t, dff = up.shape
    f = _ctile(dff, FFN_COLS)
    k_taps = w.shape[0]
    ts = _tile(seq, FFN_TILE)
    tps = seq // ts
    hb = ts // HALO16
    nhb = t // HALO16
    ext = ts + HALO16
    rc = _tile(ts, FFN_ROWS)

    def body(cur_ref, prev_ref, next_ref, dg_ref, ndg_ref, w_ref, b_ref, dup_ref, dw_ref, db_ref, apad, dcpad):
        i = pl.program_id(1)
        first = (i % tps) == 0
        last = (i % tps) == tps - 1

        @pl.when(i == 0)
        def _():
            dw_ref[...] = jnp.zeros_like(dw_ref)
            db_ref[...] = jnp.zeros_like(db_ref)

        for ci, c0 in enumerate(range(0, f, LANE)):
            cols = slice(c0, c0 + LANE)
            apad[ci, 0:HALO16, :] = jnp.where(first, 0.0, prev_ref[:, cols].astype(F32))
            apad[ci, HALO16:HALO16 + ts, :] = cur_ref[0, :, cols].astype(F32)
            apad[ci, HALO16 + ts:HALO16 + ext, :] = next_ref[0, :, cols].astype(F32)
            wv = w_ref[:, cols]
            wk = [jnp.broadcast_to(wv[k:k + 1, :], (rc, LANE)) for k in range(k_taps)]
            bias = jnp.broadcast_to(b_ref[:, cols], (rc, LANE))

            def conv_grad(r0, n, gate, dgv):
                c = bias[0:n]
                for k in range(k_taps):
                    c = c + wk[k][0:n] * apad[ci, pl.ds(HALO16 - (k_taps - 1) + k + r0, n), :]
                sig = _sigmoid(c)
                silu = c * sig
                return dgv * gate * (sig + silu * (1.0 - sig)), silu

            for r0 in range(0, ts, rc):
                dgv = dg_ref[pl.ds(r0, rc), cols].astype(F32)
                dc, silu = conv_grad(r0, rc, cur_ref[1, pl.ds(r0, rc), cols].astype(F32), dgv)
                dcpad[ci, pl.ds(r0, rc), :] = dc
                dup_ref[1, pl.ds(r0, rc), cols] = (dgv * silu).astype(BF16)
            dgv = jnp.where(last, 0.0, ndg_ref[:, cols].astype(F32))
            dc, _ = conv_grad(ts, HALO16, next_ref[1, :, cols].astype(F32), dgv)
            dcpad[ci, ts:ext, :] = dc

            dw_acc = [jnp.zeros((rc, LANE), F32) for _ in range(k_taps)]
            db_acc = jnp.zeros((rc, LANE), F32)
            for r0 in range(0, ts, rc):
                dact = jnp.zeros((rc, LANE), F32)
                for k in range(k_taps):
                    dact = dact + wk[k] * dcpad[ci, pl.ds(r0 + (k_taps - 1) - k, rc), :]
                dup_ref[0, pl.ds(r0, rc), cols] = dact.astype(BF16)
                dc = dcpad[ci, pl.ds(r0, rc), :]
                for k in range(k_taps):
                    dw_acc[k] = dw_acc[k] + dc * apad[ci, pl.ds(HALO16 - (k_taps - 1) + k + r0, rc), :]
                db_acc = db_acc + dc
            for k in range(k_taps):
                dw_ref[k:k + 1, cols] += jnp.sum(dw_acc[k], axis=0, keepdims=True)
            db_ref[:, cols] += jnp.sum(db_acc, axis=0, keepdims=True)

    return pl.pallas_call(
        body, name=name, grid=(dff // f, t // ts),
        in_specs=[pl.BlockSpec((2, ts, f), lambda j, i: (0, i, j)),
                  pl.BlockSpec((None, HALO16, f), lambda j, i: (0, jnp.maximum(i * hb - 1, 0), j)),
                  pl.BlockSpec((2, HALO16, f), lambda j, i: (0, jnp.minimum((i + 1) * hb, nhb - 1), j)),
                  pl.BlockSpec((ts, f), lambda j, i: (i, j)),
                  pl.BlockSpec((HALO16, f), lambda j, i: (jnp.minimum((i + 1) * hb, nhb - 1), j)),
                  pl.BlockSpec((k_taps, f), lambda j, i: (0, j)),
                  pl.BlockSpec((1, f), lambda j, i: (0, j))],
        out_specs=[pl.BlockSpec((2, ts, f), lambda j, i: (0, i, j)),
                   pl.BlockSpec((k_taps, f), lambda j, i: (0, j)),
                   pl.BlockSpec((1, f), lambda j, i: (0, j))],
        out_shape=[jax.ShapeDtypeStruct((2, t, dff), BF16), jax.ShapeDtypeStruct((k_taps, dff), F32),
                   jax.ShapeDtypeStruct((1, dff), F32)],
        scratch_shapes=[pltpu.VMEM((f // LANE, HALO16 + ext, LANE), F32), pltpu.VMEM((f // LANE, ext, LANE), F32)],
        compiler_params=_params("parallel", "arbitrary"),
    )(up, up, up, dg, dg, w, b)


def _sum_rows(name, g):
    ns, r, c = g.shape
    tr = _tile(r, 256)

    def body(g_ref, o_ref):
        acc = g_ref[0]
        for dev in range(1, ns):
            acc = acc + g_ref[dev]
        o_ref[...] = acc

    return pl.pallas_call(
        body, name=name, grid=(r // tr,),
        in_specs=[pl.BlockSpec((ns, tr, c), lambda i: (0, i, 0))],
        out_specs=pl.BlockSpec((tr, c), lambda i: (i, 0)),
        out_shape=jax.ShapeDtypeStruct((r, c), F32), compiler_params=_params("parallel"),
    )(g)


def _adamw(name, gsrc, w, m, v, layer=0, prev=None):
    ns, r, c = gsrc.shape
    nl = w.shape[0]
    tr = _tile(r, 256)
    prev = () if prev is None else tuple(prev)

    def body(g_ref, w_ref, m_ref, v_ref, *rest):
        go_ref, do_ref, mo_ref, vo_ref = rest[len(prev):]
        g = g_ref[0].astype(F32)
        for dev in range(1, ns):
            g = g + g_ref[dev].astype(F32)
        m_new = ADAM_B1 * m_ref[...] + (1.0 - ADAM_B1) * g
        v_new = ADAM_B2 * v_ref[...] + (1.0 - ADAM_B2) * (g * g)
        m_hat = m_new / (1.0 - ADAM_B1 ** ADAM_STEP)
        v_hat = v_new / (1.0 - ADAM_B2 ** ADAM_STEP)
        go_ref[...] = g
        do_ref[...] = -ADAM_LR * (m_hat / (jnp.sqrt(v_hat) + ADAM_EPS) + ADAM_WD * w_ref[...])
        mo_ref[...] = m_new
        vo_ref[...] = v_new

    row = pl.BlockSpec((None, tr, c), lambda i: (layer, i, 0))
    return pl.pallas_call(
        body, name=name, grid=(r // tr,),
        in_specs=[pl.BlockSpec((ns, tr, c), lambda i: (0, i, 0)), row, row, row] + [ANY] * len(prev),
        out_specs=[row] * 4, out_shape=[jax.ShapeDtypeStruct((nl, r, c), F32)] * 4,
        input_output_aliases={4 + i: i for i in range(len(prev))},
        compiler_params=_params("parallel"),
    )(gsrc, w, m, v, *prev)


def _ffn_forward(tag, r_in, h, get_wu, get_wd, wdw, bdw, seq, loss=None):
    t, d = r_in.shape
    tm = _tile(t, 512)
    wu = get_wu(h)
    dff = wu.shape[0] // 2
    tu = _tile(t, 1024)
    up = _mm(f"{tag}_up", h, wu, grid=(2, t // tu, 1),
             a_spec=pl.BlockSpec((tu, d), lambda j, i, k: (i, 0)),
             b_spec=pl.BlockSpec((dff, d), lambda j, i, k: (j, 0)),
             out_spec=pl.BlockSpec((None, tu, dff), lambda j, i, k: (j, i, 0)),
             out_shape=jax.ShapeDtypeStruct((2, t, dff), BF16), dims=NT, acc_shape=(tu, dff))
    wd = get_wd(up)
    g = _ffn_fwd(f"{tag}_act", up, wdw, bdw, seq)
    row = pl.BlockSpec((tm, d), lambda i, j, k: (i, 0))
    vec = pl.BlockSpec((1, d), lambda i, j, k: (0, 0))
    common = dict(grid=(t // tm, 1, 1), a_spec=pl.BlockSpec((tm, dff), lambda i, j, k: (i, 0)),
                  b_spec=pl.BlockSpec((dff, d), lambda i, j, k: (0, 0)), dims=NN, acc_shape=(tm, d))
    if loss is None:
        out = _mm(f"{tag}_down", g, wd, out_spec=row, out_shape=jax.ShapeDtypeStruct((t, d), F32),
                  extras=(r_in,), extra_specs=(row,), epilogue=lambda acc, ex, rows: ((ex[0][rows, :] + acc,), ()),
                  **common)
    else:
        def head(acc, ex, rows):
            dx, part, dgain = _loss_tile(ex[0][rows, :] + acc, ex[1][rows, :], ex[2][...])
            return (dx,), (part, dgain)

        out = _mm(f"{tag}_down", g, wd, out_spec=[row, pl.BlockSpec((1, 1), lambda i, j, k: (0, 0)), vec],
                  out_shape=[jax.ShapeDtypeStruct((t, d), F32), jax.ShapeDtypeStruct((1, 1), F32),
                             jax.ShapeDtypeStruct((1, d), F32)],
                  extras=(r_in, *loss), extra_specs=(row, row, vec), epilogue=head, n_sums=2, **common)
    return out, (r_in, h, up, g, wu, wd)


def _ffn_backward(tag, dr, saved, gain, wdw, bdw, seq, token=None):
    r_in, h, up, g, wu, wd = saved
    t, d = r_in.shape
    dff = wd.shape[0]
    tm = _tile(t, 512)
    tk = _tile(t, 2048)
    tku = _tile(t, 4096)
    cw = _ctile(dff, 1408)
    nc = dff // cw
    once = dict(pipeline_mode=pl.Buffered(1)) if tku == t else {}
    dg = _mm(f"{tag}_dg", dr, wd, grid=(t // tm, 1, 1),
             a_spec=pl.BlockSpec((tm, d), lambda i, j, k: (i, 0)),
             b_spec=pl.BlockSpec((dff, d), lambda i, j, k: (0, 0)),
             out_spec=pl.BlockSpec((tm, dff), lambda i, j, k: (i, 0)),
             out_shape=jax.ShapeDtypeStruct((t, dff), BF16), dims=NT, acc_shape=(tm, dff), token=token)
    dwd = _mm(f"{tag}_dwd", g, dr, grid=(dff // cw, 1, t // tk),
              a_spec=pl.BlockSpec((tk, cw), lambda i, j, k: (k, i)),
              b_spec=pl.BlockSpec((tk, d), lambda i, j, k: (k, 0)),
              out_spec=pl.BlockSpec((cw, d), lambda i, j, k: (i, 0)),
              out_shape=jax.ShapeDtypeStruct((dff, d), BF16), dims=TN, acc_shape=(cw, d))
    dup, dwdw, dbdw = _ffn_bwd(f"{tag}_dact", up, dg, wdw, bdw, seq)
    row = pl.BlockSpec((tm, d), lambda i, j, k: (i, 0))
    vec = pl.BlockSpec((1, d), lambda i, j, k: (0, 0))

    def norm_backward(acc, ex, rows):
        dx, dgain, colsum = _rms_bwd_tile(acc, ex[0][rows, :], ex[1][...], ex[2][rows, :])
        return (dx,), (dgain, colsum)

    dr_in, dgain, colsum = _mm(
        f"{tag}_dh", dup, wu, grid=(t // tm, 1, 1),
        a_spec=pl.BlockSpec((2, tm, dff), lambda i, j, k: (0, i, 0)),
        b_spec=pl.BlockSpec((2 * dff, d), lambda i, j, k: (0, 0), pipeline_mode=pl.Buffered(1)),
        out_spec=[row, vec, vec],
        out_shape=[jax.ShapeDtypeStruct((t, d), F32)] + [jax.ShapeDtypeStruct((1, d), F32)] * 2,
        dims=NN, acc_shape=(tm, d), extras=(r_in, gain, dr), extra_specs=(row, vec, row),
        epilogue=norm_backward, n_sums=2, parts=2)
    dwu = _mm(f"{tag}_dwu", dup, h, grid=(2 * nc, 1, t // tku),
              a_spec=pl.BlockSpec((None, tku, cw), lambda i, j, k: (i // nc, k, i % nc)),
              b_spec=pl.BlockSpec((tku, d), lambda i, j, k: (k, 0), **once),
              out_spec=pl.BlockSpec((cw, d), lambda i, j, k: (i, 0)),
              out_shape=jax.ShapeDtypeStruct((2 * dff, d), BF16), dims=TN, acc_shape=(cw, d))
    return dr_in, dgain, dwu, dwd, dwdw, dbdw, colsum


def _pad_to(vec, n):
    return jnp.pad(vec, (0, n - vec.shape[0]))


def _pack(parts, width):
    flat = jnp.concatenate([p.reshape(-1).astype(F32) for p in parts])
    n = -(-flat.shape[0] // (8 * width)) * (8 * width)
    return _pad_to(flat, n).reshape(n // width, width)


def _unpack(mat, shapes):
    flat = mat.reshape(-1)
    out, off = [], 0
    for s in shapes:
        n = 1
        for dim in s:
            n *= dim
        out.append(flat[off:off + n].reshape(s))
        off += n
    return out


def kernel(x, norm_mix, norm_ffn, conv_w_pw1, conv_b_pw1, conv_w_dw, conv_b_dw, conv_ln_g, conv_ln_b, conv_w_pw2, conv_b_pw2, pool_w, pool_b, pool_scale, ffn_w_up, ffn_w_dw, ffn_b_dw, ffn_w_down, final_norm, loss_target, m_norm_mix, m_norm_ffn, m_conv_w_pw1, m_conv_b_pw1, m_conv_w_dw, m_conv_b_dw, m_conv_ln_g, m_conv_ln_b, m_conv_w_pw2, m_conv_b_pw2, m_pool_w, m_pool_b, m_pool_scale, m_ffn_w_up, m_ffn_w_dw, m_ffn_b_dw, m_ffn_w_down, m_final_norm, v_norm_mix, v_norm_ffn, v_conv_w_pw1, v_conv_b_pw1, v_conv_w_dw, v_conv_b_dw, v_conv_ln_g, v_conv_ln_b, v_conv_w_pw2, v_conv_b_pw2, v_pool_w, v_pool_b, v_pool_scale, v_ffn_w_up, v_ffn_w_dw, v_ffn_b_dw, v_ffn_w_down, v_final_norm):
    bsz, seq, d = x.shape
    t = bsz * seq
    k_taps = conv_w_dw.shape[1]
    cs1 = conv_w_pw1.shape[2]
    dsh = d // N_DEV
    cg = d // N_GROUPS
    cgs = pool_w.shape[2]
    fu = ffn_w_up.shape[2]
    fd = ffn_w_down.shape[1]
    dff = fd * N_DEV
    nb = N_DEV // 2
    kf = ffn_w_dw.shape[1]
    fsh = ffn_w_dw.shape[2]
    my = _lin(_me())
    tm = _tile(t, 512)

    x2 = x.reshape(t, d)
    tgt2 = loss_target.reshape(t, d)

    small_shapes = [(k_taps, dsh), (dsh,), (dsh,), (2, kf, fsh)]
    small_mine = _pack([conv_w_dw[0], pool_b[0], pool_scale[0], ffn_w_dw], LANE)
    big = [conv_w_pw1[0], conv_w_pw2[0], ffn_w_up[0].T, ffn_w_down[0], pool_w[0], ffn_w_up[1].T, ffn_w_down[1]]
    gather = _exchange_start("gather_start", [small_mine] + [w.astype(BF16) for w in big], [GATHER2] * 8)
    h0 = _rms_fwd("l0_rms", x2, norm_mix[0:1])
    forwarded = _exchange_forward("gather_forward_w1", gather, [0, 1], h0)
    small_all, w1 = _exchange_wait("gather_wait_w1", gather, [0, 1], forwarded)
    parts = [_unpack(small_all[dev], small_shapes) for dev in range(N_DEV)]
    wdw = jnp.concatenate([p[0] for p in parts], axis=1)
    pool_b_full = jnp.concatenate([p[1] for p in parts]).reshape(1, d)
    pool_s_full = jnp.concatenate([p[2] for p in parts]).reshape(1, d)
    fwdw = jnp.concatenate([p[3] for p in parts], axis=2)
    fbdw = ffn_b_dw.reshape(2, 1, dff)

    def columns(w):
        return w.transpose(1, 0, 2).reshape(w.shape[1], N_DEV * w.shape[2])

    def column_shards(w):
        return w.reshape(w.shape[0], N_DEV, w.shape[1] // N_DEV).transpose(1, 0, 2)

    w1 = columns(w1)
    a = _mm("l0_pw1", h0, w1, grid=(2, t // tm, 1),
            a_spec=pl.BlockSpec((tm, d), lambda j, i, k: (i, 0)),
            b_spec=pl.BlockSpec((d, d), lambda j, i, k: (0, j)),
            out_spec=pl.BlockSpec((None, tm, d), lambda j, i, k: (j, i, 0)),
            out_shape=jax.ShapeDtypeStruct((2, t, d), BF16), dims=NN, acc_shape=(tm, d),
            extras=(conv_b_pw1,), extra_specs=(pl.BlockSpec((1, d), lambda j, i, k: (0, j)),),
            epilogue=lambda acc, ex, rows: ((acc + ex[0][...],), ()))
    v = _conv_fwd("l0_conv", a, wdw, conv_b_dw, seq)
    forwarded = _exchange_forward("gather_forward_wu0", gather, [2, 3], v)
    (w2,) = _exchange_wait("gather_wait_w2", gather, [2], forwarded)
    w2 = w2.reshape(d, d)
    row = pl.BlockSpec((tm, d), lambda i, j, k: (i, 0))
    vec = pl.BlockSpec((1, d), lambda i, j, k: (0, 0))
    square = pl.BlockSpec((d, d), lambda i, j, k: (0, 0))

    def ln_silu(v_blk, ex):
        s_blk = _ln_silu_tile(v_blk, ex[0][...], ex[1][...]).astype(BF16)
        return s_blk, s_blk

    def residual_and_norm(acc, ex, rows):
        r_blk = ex[3][rows, :] + (acc + ex[2][...])
        return (r_blk, _rms(r_blk, ex[4][...])), ()

    r1, h1, s = _mm("l0_pw2", v, w2, grid=(t // tm, 1, 1), a_spec=row, b_spec=square, out_spec=[row, row, row],
                    out_shape=[jax.ShapeDtypeStruct((t, d), F32), jax.ShapeDtypeStruct((t, d), BF16),
                               jax.ShapeDtypeStruct((t, d), BF16)],
                    dims=NN, acc_shape=(tm, d), extras=(conv_ln_g, conv_ln_b, conv_b_pw2, x2, norm_ffn[0:1]),
                    extra_specs=(vec, vec, vec, row, vec), prologue=ln_silu, epilogue=residual_and_norm)

    def up_getter(name, idx):
        return lambda after: _exchange_wait(name, gather, [idx], after)[0].reshape(2 * dff, d)

    def down_getter(name, idx, forward=None):
        def get(after):
            if forward is not None:
                after = _exchange_forward(forward[0], gather, forward[1], after)
            return _exchange_wait(name, gather, [idx], after)[0].reshape(dff, d)
        return get

    r2, ffn0_saved = _ffn_forward("f0", r1, h1, up_getter("gather_wait_wu0", 3),
                                  down_getter("gather_wait_wd0", 4, ("gather_forward_wu1", [4, 5, 6])),
                                  fwdw[0], fbdw[0], seq)
    forwarded = _exchange_forward("gather_forward_wd1", gather, [7], r2)
    (wp,) = _exchange_wait("gather_wait_wp", gather, [5], forwarded)
    wp = wp.transpose(1, 0, 2, 3).reshape(N_GROUPS, cg, cg)
    pooled, r3, h3 = _pool_mix_fwd("l1_mix", r2, norm_mix[1:2], wp, pool_s_full, pool_b_full, norm_ffn[1:2], seq)
    (dr4, loss_part, dfinal), ffn1_saved = _ffn_forward(
        "f1", r3, h3, up_getter("gather_wait_wu1", 6), down_getter("gather_wait_wd1", 7), fwdw[1], fbdw[1], seq,
        loss=(tgt2, final_norm.reshape(1, d)))

    dr3, dnf1, dwu1, dwd1, dfw1, dfb1, _ = _ffn_backward("f1", dr4, ffn1_saved, norm_ffn[1:2], fwdw[1], fbdw[1], seq)
    scatter_a = _exchange_start("scatter_f1_start", [dwu1.reshape(N_DEV, fu, d), dwd1.reshape(N_DEV, fd, d)],
                                [SCATTER, SCATTER])
    dr2, dwp, dpool_s, dpool_b, dnm1 = _pool_mix_bwd(
        "l1_dmix", pooled, wp, dr3, r2, norm_mix[1:2], pool_s_full + scatter_a["token"][0:1, 0:1], pool_b_full, seq)
    dr1, dnf0, dwu0, dwd0, dfw0, dfb0, db2 = _ffn_backward("f0", dr2, ffn0_saved, norm_ffn[0:1], fwdw[0], fbdw[0], seq)
    dwp_b = dwp.astype(BF16).reshape(N_GROUPS, N_DEV, cgs, cg).transpose(1, 0, 2, 3)
    tk = _tile(t, 2048)
    dw2 = _mm("l0_dw2", s, dr1, grid=(1, 1, t // tk),
              a_spec=pl.BlockSpec((tk, d), lambda i, j, k: (k, 0)),
              b_spec=pl.BlockSpec((tk, d), lambda i, j, k: (k, 0)),
              out_spec=pl.BlockSpec((d, d), lambda i, j, k: (0, 0)),
              out_shape=jax.ShapeDtypeStruct((d, d), BF16), dims=TN, acc_shape=(d, d))
    scatter_b = _exchange_start("scatter_f0_start", [dwu0.reshape(N_DEV, fu, d), dwd0.reshape(N_DEV, fd, d), dwp_b,
                                                     dw2.reshape(N_DEV, d // N_DEV, d)], [SCATTER] * 4)

    def ln_silu_backward(acc, ex, rows):
        dv_blk, dgain, dbias, colsum = _ln_silu_bwd_tile(acc, ex[0][rows, :], ex[1][...], ex[2][...])
        return (dv_blk,), (dgain, dbias, colsum)

    dv, dlg, dlb, dbdw = _mm("l0_ds", dr1, w2, grid=(t // tm, 1, 1), a_spec=row, b_spec=square,
                             out_spec=[row, vec, vec, vec],
                             out_shape=[jax.ShapeDtypeStruct((t, d), F32)] + [jax.ShapeDtypeStruct((1, d), F32)] * 3,
                             dims=NT, acc_shape=(tm, d), extras=(v, conv_ln_g, conv_ln_b), extra_specs=(row, vec, vec),
                             epilogue=ln_silu_backward, n_sums=3, token=scatter_b["token"])
    da, dwdw, db1 = _conv_bwd("l0_dconv", a, dv, wdw, seq)
    tk1 = _tile(t, 4096)
    once = dict(pipeline_mode=pl.Buffered(1)) if tk1 == t else {}
    dw1 = _mm("l0_dw1", h0, da, grid=(1, 2, t // tk1),
              a_spec=pl.BlockSpec((tk1, d), lambda i, j, k: (k, 0), **once),
              b_spec=pl.BlockSpec((None, tk1, d), lambda i, j, k: (j, k, 0)),
              out_spec=pl.BlockSpec((d, d), lambda i, j, k: (0, j)),
              out_shape=jax.ShapeDtypeStruct((d, 2 * d), BF16), dims=TN, acc_shape=(d, d))
    scatter_c = _exchange_start("scatter_l0_start", [column_shards(dw1)], [SCATTER])
    def norm_backward(acc, ex, rows):
        dx_blk, dgain, colsum = _rms_bwd_tile(acc, ex[0][rows, :], ex[1][...], ex[2][rows, :])
        return (dx_blk,), (dgain, colsum)

    dx, dnm0, _ = _mm("l0_dh", da, w1, grid=(t // tm, 1, 1),
                      a_spec=pl.BlockSpec((2, tm, d), lambda i, j, k: (0, i, 0)),
                      b_spec=pl.BlockSpec((d, 2 * d), lambda i, j, k: (0, 0), pipeline_mode=pl.Buffered(1)),
                      out_spec=[row, vec, vec],
                      out_shape=[jax.ShapeDtypeStruct((t, d), F32)] + [jax.ShapeDtypeStruct((1, d), F32)] * 2,
                      dims=NT, acc_shape=(tm, d), extras=(x2, norm_mix[0:1], dr1), extra_specs=(row, vec, row),
                      epilogue=norm_backward, n_sums=2, parts=2, token=scatter_c["token"])

    dffn_w = jnp.stack([dfw0, dfw1])
    dffn_b = jnp.stack([dfb0, dfb1]).reshape(2, dff)
    small_parts = [loss_part, jnp.concatenate([dnm0, dnm1]), jnp.concatenate([dnf0, dnf1]), db1, dwdw, dbdw, dlg, dlb,
                   db2, dpool_b, dpool_s, dffn_w, dffn_b, dfinal]
    small_part_shapes = [(1,), (2, d), (2, d), (1, 2 * d), (k_taps, d), (1, d), (1, d), (1, d), (1, d), (1, d), (1, d),
                         (2, kf, dff), (2, dff), (d,)]
    packed = _pack(small_parts, 8 * LANE)
    gather_small = _exchange_start("gather_small_start", [packed], [GATHER])

    def big_update(name, recv, w, m, v, layer=0, prev=None):
        shape = w.shape
        c = recv.shape[-1]
        rows = recv.size // (N_DEV * c)
        nl = w.size // (rows * c)
        outs = _adamw(name, recv.reshape(N_DEV, rows, c), w.reshape(nl, rows, c), m.reshape(nl, rows, c),
                      v.reshape(nl, rows, c), layer, prev)
        return outs, [o.reshape(shape) for o in outs]

    wu_t = [p.transpose(0, 2, 1) for p in (ffn_w_up, m_ffn_w_up, v_ffn_w_up)]
    g_wu1, g_wd1 = _exchange_wait("scatter_f1_wait", scatter_a, [0, 1], gather_small["token"])
    raw_wu, _ = big_update("adam_wu1", g_wu1, *wu_t, 1)
    raw_wd, _ = big_update("adam_wd1", g_wd1, ffn_w_down, m_ffn_w_down, v_ffn_w_down, 1)
    g_wu0, g_wd0, g_wp, g_w2 = _exchange_wait("scatter_f0_wait", scatter_b, [0, 1, 2, 3], raw_wd[0])
    _, u_wu = big_update("adam_wu0", g_wu0, *wu_t, 0, raw_wu)
    u_wu = [o.transpose(0, 2, 1) for o in u_wu]
    _, u_wd = big_update("adam_wd0", g_wd0, ffn_w_down, m_ffn_w_down, v_ffn_w_down, 0, raw_wd)
    _, u_wp = big_update("adam_wp", g_wp, pool_w, m_pool_w, v_pool_w)
    (g_w1,) = _exchange_wait("scatter_l0_wait", scatter_c, [0], u_wp[0])
    _, u_w1 = big_update("adam_w1", g_w1, conv_w_pw1, m_conv_w_pw1, v_conv_w_pw1)
    _, u_w2 = big_update("adam_w2", g_w2, conv_w_pw2, m_conv_w_pw2, v_conv_w_pw2)
    (all_small,) = _exchange_wait("gather_small_wait", gather_small, [0], u_w2[0])
    summed = _sum_rows("sum_small_grads", all_small)
    (loss_v, g_nm, g_nf, g_b1, g_wdw, g_bdw, g_lg, g_lb, g_b2, g_pb, g_ps, g_fw, g_fb,
     g_fin) = _unpack(summed, small_part_shapes)
    loss = loss_v[0]
    g_wdw_mine = lax.dynamic_slice_in_dim(g_wdw, my * dsh, dsh, axis=1)[None]
    g_pb_mine = lax.dynamic_slice_in_dim(g_pb, my * dsh, dsh, axis=1)
    g_ps_mine = lax.dynamic_slice_in_dim(g_ps, my * dsh, dsh, axis=1)
    g_fw_mine = lax.dynamic_slice_in_dim(g_fw, my * fsh, fsh, axis=2)

    small_g =[g_nm, g_nf, g_b1, g_wdw_mine, g_bdw, g_lg, g_lb, g_b2, g_pb_mine, g_ps_mine, g_fw_mine, g_fb, g_fin]
    small_w = [norm_mix, norm_ffn, conv_b_pw1, conv_w_dw, conv_b_dw, conv_ln_g, conv_ln_b, conv_b_pw2, pool_b,
               pool_scale, ffn_w_dw, ffn_b_dw, final_norm]
    small_m = [m_norm_mix, m_norm_ffn, m_conv_b_pw1, m_conv_w_dw, m_conv_b_dw, m_conv_ln_g, m_conv_ln_b,
               m_conv_b_pw2, m_pool_b, m_pool_scale, m_ffn_w_dw, m_ffn_b_dw, m_final_norm]
    small_v = [v_norm_mix, v_norm_ffn, v_conv_b_pw1, v_conv_w_dw, v_conv_b_dw, v_conv_ln_g, v_conv_ln_b,
               v_conv_b_pw2, v_pool_b, v_pool_scale, v_ffn_w_dw, v_ffn_b_dw, v_final_norm]
    shapes = [w.shape for w in small_w]
    outs = _adamw("adam_small", _pack(small_g, 8 * LANE)[None], _pack(small_w, 8 * LANE)[None],
                  _pack(small_m, 8 * LANE)[None], _pack(small_v, 8 * LANE)[None])
    sg, sd, sm, sv = [_unpack(o, shapes) for o in outs]

    def leaf(kind):
        (nm, nf, b1, wdw_, bdw_, lg, lb, b2, pb, ps, fw, fb, fin) = (sg, sd, sm, sv)[kind]
        return [nm, nf, u_w1[kind], b1, wdw_, bdw_, lg, lb, u_w2[kind], b2, u_wp[kind], pb, ps, u_wu[kind], fw, fb,
                u_wd[kind], fin]

    return (loss, dx.reshape(bsz, seq, d), *leaf(0), *leaf(1), *leaf(2), *leaf(3))
```

```python
import functools

import jax
import jax.numpy as jnp
from jax import lax
from jax.experimental import pallas as pl
from jax.experimental.pallas import tpu as pltpu

F32 = jnp.float32
BF16 = jnp.bfloat16
MESH = pl.DeviceIdType.MESH
HBM = pl.BlockSpec(memory_space=pltpu.HBM)

N_DEV = 8
RMS_EPS = 1e-6
LN_EPS = 1e-5
POOL_WINDOWS = (2, 4, 8, 16)
N_GROUPS = len(POOL_WINDOWS)
ADAM_LR = 0.001
ADAM_B1 = 0.9
ADAM_B2 = 0.999
ADAM_EPS = 1e-08
ADAM_WD = 0.01
ADAM_STEP = 10

LANE = 128
HALO = 32
HALO16 = 16
VMEM_LIMIT = 56 * 1024 * 1024


def _params(*sem):
    return pltpu.CompilerParams(dimension_semantics=sem if sem else None, vmem_limit_bytes=VMEM_LIMIT)


def _tile(n, pref):
    for t in range(min(pref, n), 15, -1):
        if n % t == 0 and t % 16 == 0:
            return t
    return n


def _sigmoid(z):
    return 1.0 / (1.0 + jnp.exp(-z))


def _me():
    return lax.axis_index("x"), lax.axis_index("y"), lax.axis_index("c")


def _flip(pos, m):
    x, y, c = pos
    return ((1 - x) if m & 4 else x, (1 - y) if m & 2 else y, (1 - c) if m & 1 else c)


def _lin(pos):
    return 4 * pos[0] + 2 * pos[1] + pos[2]


SEM = pl.BlockSpec(memory_space=pltpu.SEMAPHORE)
ANY = pl.BlockSpec(memory_space=pl.ANY)
EFFECT = pltpu.SideEffectType.DATAFLOW_SIDE_EFFECTING


SCATTER = "scatter"
GATHER = "gather"
GATHER2 = "gather2"
ALL_MASKS = (1, 2, 3, 4, 5, 6, 7)
SIBLING = 1
CHIPS = (2, 4, 6)


class _Copies:
    def __init__(self, a, mode, src, land, send_sems, recv_sems):
        self.a, self.mode, self.src, self.land = a, mode, src, land
        self.send_sems, self.recv_sems = send_sems, recv_sems
        self.me = _me()
        self.first = (SIBLING,) + CHIPS if mode == GATHER2 else ALL_MASKS

    def _sems(self, m, to):
        return dict(send_sem=self.send_sems.at[self.a * N_DEV + m], recv_sem=self.recv_sems.at[self.a * N_DEV + m],
                    device_id=to, device_id_type=MESH)

    def _block(self, pid):
        return self.src.at[pid] if self.mode == SCATTER else self.src

    def local(self):
        my = _lin(self.me)
        return pltpu.make_async_copy(self._block(my), self.land.at[my], self.send_sems.at[self.a * N_DEV])

    def send(self, m):
        peer = _flip(self.me, m)
        return pltpu.make_async_remote_copy(src_ref=self._block(_lin(peer)), dst_ref=self.land.at[_lin(self.me)],
                                            **self._sems(m, peer))

    def arrival(self, m):
        rows = self.land.at[_lin(_flip(self.me, m))]
        return pltpu.make_async_remote_copy(src_ref=rows, dst_ref=rows, **self._sems(m, _flip(self.me, m)))

    def forward(self, m):
        rows = self.land.at[_lin(_flip(self.me, m))]
        return pltpu.make_async_remote_copy(src_ref=rows, dst_ref=rows, **self._sems(m | 1, _flip(self.me, SIBLING)))


def _exchange_start(name, arrs, modes):
    n = len(arrs)
    blocks = [a.shape[1:] if md == SCATTER else a.shape for a, md in zip(arrs, modes)]

    def body(*refs):
        srcs, lands = refs[:n], refs[n:2 * n]
        send_sems, recv_sems = refs[2 * n], refs[2 * n + 1]
        token = refs[-1]
        for a in range(n):
            cp = _Copies(a, modes[a], srcs[a], lands[a], send_sems, recv_sems)
            cp.local().start()
            for m in cp.first:
                cp.send(m).start()
        token[...] = jnp.zeros_like(token)

    lands = [lax.empty((N_DEV,) + tuple(b), a.dtype) for a, b in zip(arrs, blocks)]
    outs = pl.pallas_call(
        body, name=name,
        out_shape=(pltpu.SemaphoreType.DMA((n * N_DEV,)), pltpu.SemaphoreType.DMA((n * N_DEV,)),
                   *[pltpu.HBM(a.shape, a.dtype) for a in arrs], *[pltpu.HBM(l.shape, l.dtype) for l in lands],
                   jax.ShapeDtypeStruct((8, LANE), F32)),
        in_specs=[HBM] * (2 * n),
        out_specs=(SEM, SEM, *[HBM] * (2 * n), pl.BlockSpec(memory_space=pltpu.VMEM)),
        input_output_aliases={i: 2 + i for i in range(2 * n)},
        compiler_params=pltpu.CompilerParams(has_side_effects=EFFECT),
    )(*[pltpu.with_memory_space_constraint(a, pltpu.HBM) for a in arrs],
      *[pltpu.with_memory_space_constraint(l, pltpu.HBM) for l in lands])
    return dict(send=outs[0], recv=outs[1], srcs=list(outs[2:2 + n]), lands=list(outs[2 + n:2 + 2 * n]),
                modes=modes, token=outs[-1])


def _exchange_forward(name, handle, which, after):
    k = len(which)

    def half(wait):
        def body(*refs):
            lands = refs[:k]
            send_sems, recv_sems = refs[k], refs[k + 1]
            token = refs[-1]
            for pos, a in enumerate(which):
                cp = _Copies(a, GATHER2, None, lands[pos], send_sems, recv_sems)
                for m in CHIPS:
                    if wait:
                        cp.arrival(m).wait_recv()
                    else:
                        cp.forward(m).start()
            token[...] = jnp.zeros_like(token)
        return body

    def call(body, call_name, lands, after):
        outs = pl.pallas_call(
            body, name=call_name,
            out_shape=(*[pltpu.HBM(x.shape, x.dtype) for x in lands], jax.ShapeDtypeStruct((8, LANE), F32)),
            in_specs=[HBM] * k + [SEM, SEM, ANY], out_specs=(*[HBM] * k, pl.BlockSpec(memory_space=pltpu.VMEM)),
            input_output_aliases={i: i for i in range(k)},
            compiler_params=pltpu.CompilerParams(has_side_effects=EFFECT),
        )(*lands, handle["send"], handle["recv"], after)
        return list(outs[:k]), outs[-1]

    lands, arrived = call(half(True), name + "_arrived", [handle["lands"][a] for a in which], after)
    lands, token = call(half(False), name, lands, arrived)
    for pos, a in enumerate(which):
        handle["lands"][a] = lands[pos]
    return token


def _exchange_wait(name, handle, which, after):
    k = len(which)
    modes = handle["modes"]

    def body(*refs):
        srcs, lands = refs[:k], refs[k:2 * k]
        send_sems, recv_sems = refs[2 * k], refs[2 * k + 1]
        for pos, a in enumerate(which):
            cp = _Copies(a, modes[a], srcs[pos], lands[pos], send_sems, recv_sems)
            cp.local().wait()
            for m in cp.first:
                cp.send(m).wait_send()
            if modes[a] == GATHER2:
                for m in CHIPS:
                    cp.forward(m).wait_send()
                arrivals = (SIBLING,) + tuple(m | 1 for m in CHIPS)
            else:
                arrivals = ALL_MASKS
            for m in arrivals:
                cp.arrival(m).wait_recv()

    srcs = [handle["srcs"][a] for a in which]
    lands = [handle["lands"][a] for a in which]
    outs = pl.pallas_call(
        body, name=name,
        out_shape=tuple(pltpu.HBM(x.shape, x.dtype) for x in srcs + lands),
        in_specs=[HBM] * (2 * k) + [SEM, SEM, ANY], out_specs=tuple([HBM] * (2 * k)),
        input_output_aliases={i: i for i in range(2 * k)},
        compiler_params=pltpu.CompilerParams(has_side_effects=EFFECT),
    )(*srcs, *lands, handle["send"], handle["recv"], after)
    for pos, a in enumerate(which):
        handle["srcs"][a], handle["lands"][a] = outs[pos], outs[k + pos]
    return list(outs[k:])


def _mm(name, a, b, *, grid, a_spec, b_spec, out_spec, out_shape, dims, acc_shape, extras=(), extra_specs=(),
        epilogue=None, token=None, prologue=None, n_sums=0, parts=1):
    nk = grid[2]
    ne = len(extras)
    deps = () if token is None else (token,)
    dep_specs = [pl.BlockSpec((8, LANE), lambda i, j, k: (0, 0))] * len(deps)
    n_out = len(out_shape) if isinstance(out_shape, (list, tuple)) else 1
    n_tiles = n_out - n_sums - (1 if prologue is not None else 0)

    def body(a_ref, b_ref, *rest):
        ex, o_refs, acc_ref = rest[:ne], rest[ne + len(deps):ne + len(deps) + n_out], rest[ne + len(deps) + n_out]
        k = pl.program_id(2)
        if parts == 1:
            a_blk, saved = a_ref[...], None
            if prologue is not None:
                a_blk, saved = prologue(a_blk, ex)
                o_refs[n_tiles][...] = saved
            part = lax.dot_general(a_blk.astype(BF16), b_ref[...].astype(BF16), (dims, ((), ())),
                                   preferred_element_type=F32)
        else:
            kb = b_ref.shape[dims[1][0]] // parts
            part = None
            for p in range(parts):
                b_blk = b_ref[p * kb:(p + 1) * kb, :] if dims[1][0] == 0 else b_ref[:, p * kb:(p + 1) * kb]
                term = lax.dot_general(a_ref[p].astype(BF16), b_blk.astype(BF16), (dims, ((), ())),
                                       preferred_element_type=F32)
                part = term if part is None else part + term
        sum_refs = o_refs[n_out - n_sums:]

        def add_sums(terms):
            @pl.when((pl.program_id(0) == 0) & (pl.program_id(1) == 0))
            def _():
                for o_ref in sum_refs:
                    o_ref[...] = jnp.zeros_like(o_ref)

            for o_ref, term in zip(sum_refs, terms):
                o_ref[...] += jnp.sum(term, axis=0, keepdims=True)

        def finish(r):
            tiles, terms = ((r,), ()) if epilogue is None else epilogue(r, ex, slice(None))
            for o_ref, val in zip(o_refs, tiles):
                o_ref[...] = val.astype(o_ref.dtype)
            if n_sums:
                add_sums(terms)

        if nk == 1:
            finish(part)
            return

        @pl.when(k == 0)
        def _():
            acc_ref[...] = part

        @pl.when((k > 0) & (k < nk - 1))
        def _():
            acc_ref[...] += part

        @pl.when(k == nk - 1)
        def _():
            finish(acc_ref[...] + part)

    return pl.pallas_call(
        body, name=name, grid=grid, in_specs=[a_spec, b_spec, *extra_specs, *dep_specs], out_specs=out_spec,
        out_shape=out_shape, scratch_shapes=[pltpu.VMEM(acc_shape if nk > 1 else (8, LANE), F32)],
        compiler_params=_params(*(("arbitrary",) * 3 if n_sums else ("parallel", "parallel", "arbitrary"))),
    )(a, b, *extras, *deps)


def _rms(x, gain):
    return x * lax.rsqrt(jnp.mean(x * x, axis=-1, keepdims=True) + RMS_EPS) * gain


def _rms_bwd_tile(dh, x, gain, dres):
    rstd = lax.rsqrt(jnp.mean(x * x, axis=-1, keepdims=True) + RMS_EPS)
    xhat = x * rstd
    dxhat = dh * gain
    dx = dres + rstd * (dxhat - xhat * jnp.mean(dxhat * xhat, axis=-1, keepdims=True))
    return dx, dh * xhat, dx


def _ln_silu_tile(v, g, b):
    mu = jnp.mean(v, axis=-1, keepdims=True)
    cen = v - mu
    z = cen * lax.rsqrt(jnp.mean(cen * cen, axis=-1, keepdims=True) + LN_EPS) * g + b
    return z * _sigmoid(z)


def _ln_silu_bwd_tile(ds, v, g, b):
    mu = jnp.mean(v, axis=-1, keepdims=True)
    cen = v - mu
    rstd = lax.rsqrt(jnp.mean(cen * cen, axis=-1, keepdims=True) + LN_EPS)
    y = cen * rstd
    z = y * g + b
    sig = _sigmoid(z)
    dz = ds * sig * (1.0 + z * (1.0 - sig))
    dy = dz * g
    dv = rstd * (dy - jnp.mean(dy, axis=-1, keepdims=True) - y * jnp.mean(dy * y, axis=-1, keepdims=True))
    return dv, dz * y, dz, dv


def _loss_tile(x, tgt, gain):
    d = x.shape[-1]
    rstd = lax.rsqrt(jnp.mean(x * x, axis=-1, keepdims=True) + RMS_EPS)
    xhat = x * rstd
    err = xhat * gain - tgt
    dy = err / d
    dxhat = dy * gain
    dx = rstd * (dxhat - xhat * jnp.mean(dxhat * xhat, axis=-1, keepdims=True))
    return dx, 0.5 * jnp.mean(err * err, axis=-1, keepdims=True), dy * xhat


NN = ((1,), (0,))
NT = ((1,), (1,))
TN = ((0,), (0,))


def _rms_fwd(name, x, gain):
    t, d = x.shape
    tr = _tile(t, 512)

    def body(x_ref, g_ref, h_ref):
        h_ref[...] = _rms(x_ref[...], g_ref[...]).astype(BF16)

    return pl.pallas_call(
        body, name=name, grid=(t // tr,),
        in_specs=[pl.BlockSpec((tr, d), lambda i: (i, 0)), pl.BlockSpec((1, d), lambda i: (0, 0))],
        out_specs=pl.BlockSpec((tr, d), lambda i: (i, 0)),
        out_shape=jax.ShapeDtypeStruct((t, d), BF16), compiler_params=_params("parallel"),
    )(x, gain)


def _conv_tiles(t, seq):
    ts = _tile(seq, 1024)
    return ts, seq // ts, _tile(ts, 64)


def _conv_fwd(name, a, w, b, seq):
    _, t, d = a.shape
    k_taps = w.shape[0]
    ts, tps, rc = _conv_tiles(t, seq)
    hb = ts // HALO

    def body(cur_ref, prev_ref, w_ref, b_ref, v_ref, upad):
        i = pl.program_id(1)
        first = (i % tps) == 0
        pv = prev_ref[0].astype(F32)
        pg = prev_ref[1].astype(F32)
        upad[0:HALO, :] = jnp.where(first, 0.0, pv * _sigmoid(pg))
        upad[HALO:HALO + ts, :] = cur_ref[0].astype(F32) * _sigmoid(cur_ref[1].astype(F32))
        wv = w_ref[...]
        bias = jnp.broadcast_to(b_ref[...], (rc, LANE))
        for r0 in range(0, ts, rc):
            acc = bias
            for k in range(k_taps):
                acc = acc + wv[k:k + 1, :] * upad[pl.ds(HALO - (k_taps - 1) + k + r0, rc), :]
            v_ref[pl.ds(r0, rc), :] = acc

    return pl.pallas_call(
        body, name=name, grid=(d // LANE, t // ts),
        in_specs=[pl.BlockSpec((2, ts, LANE), lambda c, i: (0, i, c)),
                  pl.BlockSpec((2, HALO, LANE), lambda c, i: (0, jnp.maximum(i * hb - 1, 0), c)),
                  pl.BlockSpec((k_taps, LANE), lambda c, i: (0, c)),
                  pl.BlockSpec((1, LANE), lambda c, i: (0, c))],
        out_specs=pl.BlockSpec((ts, LANE), lambda c, i: (i, c)),
        out_shape=jax.ShapeDtypeStruct((t, d), F32),
        scratch_shapes=[pltpu.VMEM((HALO + ts, LANE), F32)],
        compiler_params=_params("parallel", "parallel"),
    )(a, a, w, b)


def _conv_bwd(name, a, dv, w, seq):
    _, t, d = a.shape
    k_taps = w.shape[0]
    ts, tps, rc = _conv_tiles(t, seq)
    hb = ts // HALO
    nhb = t // HALO

    def body(cur_ref, prev_ref, dv_ref, ndv_ref, w_ref, da_ref, dw_ref, dbp_ref, upad, dvpad, dwrows):
        i = pl.program_id(1)
        first = (i % tps) == 0
        last = (i % tps) == tps - 1
        pv = prev_ref[0].astype(F32)
        pg = prev_ref[1].astype(F32)
        upad[0:HALO, :] = jnp.where(first, 0.0, pv * _sigmoid(pg))
        upad[HALO:HALO + ts, :] = cur_ref[0].astype(F32) * _sigmoid(cur_ref[1].astype(F32))
        dvpad[0:ts, :] = dv_ref[...]
        dvpad[ts:ts + HALO, :] = jnp.where(last, 0.0, ndv_ref[...])
        wv = w_ref[...]

        @pl.when(i == 0)
        def _():
            dw_ref[...] = jnp.zeros_like(dw_ref)
            dbp_ref[...] = jnp.zeros_like(dbp_ref)

        sv = jnp.zeros((1, LANE), F32)
        sg = jnp.zeros((1, LANE), F32)
        for r0 in range(0, ts, rc):
            du = jnp.zeros((rc, LANE), F32)
            for k in range(k_taps):
                du = du + wv[k:k + 1, :] * dvpad[pl.ds(r0 + (k_taps - 1) - k, rc), :]
            av = cur_ref[0, pl.ds(r0, rc), :].astype(F32)
            sig = _sigmoid(cur_ref[1, pl.ds(r0, rc), :].astype(F32))
            dval = du * sig
            dgate = du * av * sig * (1.0 - sig)
            da_ref[0, pl.ds(r0, rc), :] = dval.astype(BF16)
            da_ref[1, pl.ds(r0, rc), :] = dgate.astype(BF16)
            sv = sv + jnp.sum(dval, axis=0, keepdims=True)
            sg = sg + jnp.sum(dgate, axis=0, keepdims=True)
        dbp_ref[0] += sv
        dbp_ref[1] += sg

        for k in range(k_taps):
            acc = jnp.zeros((rc, LANE), F32)
            for r0 in range(0, ts, rc):
                acc = acc + dvpad[pl.ds(r0, rc), :] * upad[pl.ds(HALO - (k_taps - 1) + k + r0, rc), :]
            dwrows[k:k + 1, :] = jnp.sum(acc, axis=0, keepdims=True)
        dw_ref[...] += dwrows[0:k_taps, :]

    return pl.pallas_call(
        body, name=name, grid=(d // LANE, t // ts),
        in_specs=[pl.BlockSpec((2, ts, LANE), lambda c, i: (0, i, c)),
                  pl.BlockSpec((2, HALO, LANE), lambda c, i: (0, jnp.maximum(i * hb - 1, 0), c)),
                  pl.BlockSpec((ts, LANE), lambda c, i: (i, c)),
                  pl.BlockSpec((HALO, LANE), lambda c, i: (jnp.minimum((i + 1) * hb, nhb - 1), c)),
                  pl.BlockSpec((k_taps, LANE), lambda c, i: (0, c))],
        out_specs=[pl.BlockSpec((2, ts, LANE), lambda c, i: (0, i, c)),
                   pl.BlockSpec((k_taps, LANE), lambda c, i: (0, c)),
                   pl.BlockSpec((2, 1, LANE), lambda c, i: (0, 0, c))],
        out_shape=[jax.ShapeDtypeStruct((2, t, d), BF16), jax.ShapeDtypeStruct((k_taps, d), F32),
                   jax.ShapeDtypeStruct((2, 1, d), F32)],
        scratch_shapes=[pltpu.VMEM((HALO + ts, LANE), F32), pltpu.VMEM((ts + HALO, LANE), F32),
                        pltpu.VMEM((HALO, LANE), F32)],
        compiler_params=_params("parallel", "arbitrary"),
    )(a, a, dv, dv, w)


def _pool_mix_fwd(name, x, gain, wp, scale, bias, next_gain, seq):
    t, d = x.shape
    ts = _tile(seq, 256)
    tps = seq // ts
    hb = ts // HALO
    cg = d // N_GROUPS
    sw = min(cg, LANE)

    def body(cur_ref, prev_ref, g_ref, w_ref, s_ref, b_ref, ng_ref, p_ref, r_ref, h_ref, hpad):
        i = pl.program_id(0)
        first = (i % tps) == 0
        g = g_ref[...]
        h_prev = jnp.where(first, 0.0, _rms(prev_ref[...], g))
        h_cur = _rms(cur_ref[...], g)
        for si in range(d // sw):
            hpad[si, 0:HALO, :] = h_prev[:, si * sw:(si + 1) * sw]
            hpad[si, HALO:HALO + ts, :] = h_cur[:, si * sw:(si + 1) * sw]
        pos = (i % tps) * ts + lax.broadcasted_iota(jnp.int32, (ts, 1), 0)
        for gi, win in enumerate(POOL_WINDOWS):
            sl = slice(gi * cg, (gi + 1) * cg)
            cnt = jnp.minimum(pos + 1, win).astype(F32)
            for si in range(gi * cg // sw, (gi + 1) * cg // sw):
                own = hpad[si, HALO:HALO + ts, :]
                acc = own
                for j in range(1, win):
                    acc = acc + hpad[si, pl.ds(HALO - j, ts), :]
                p_ref[:, si * sw:(si + 1) * sw] = (acc / cnt - own).astype(BF16)
            mixed = jnp.dot(p_ref[:, sl], w_ref[gi], preferred_element_type=F32)
            r_ref[:, sl] = cur_ref[:, sl] + s_ref[:, sl] * (mixed + b_ref[:, sl])
        h_ref[...] = _rms(r_ref[...], ng_ref[...]).astype(BF16)

    row = pl.BlockSpec((ts, d), lambda i: (i, 0))
    vec = pl.BlockSpec((1, d), lambda i: (0, 0))
    return pl.pallas_call(
        body, name=name, grid=(t // ts,),
        in_specs=[row, pl.BlockSpec((HALO, d), lambda i: (jnp.maximum(i * hb - 1, 0), 0)), vec,
                  pl.BlockSpec((N_GROUPS, cg, cg), lambda i: (0, 0, 0)), vec, vec, vec],
        out_specs=[row, row, row],
        out_shape=[jax.ShapeDtypeStruct((t, d), BF16), jax.ShapeDtypeStruct((t, d), F32),
                   jax.ShapeDtypeStruct((t, d), BF16)],
        scratch_shapes=[pltpu.VMEM((d // sw, HALO + ts, sw), F32)],
        compiler_params=_params("parallel"),
    )(x, x, gain, wp, scale, bias, next_gain)


def _pool_mix_bwd(name, pooled, wp, dr, x, gain, scale, bias, seq):
    t, d = x.shape
    ts = _tile(seq, 256)
    tps = seq // ts
    hb = ts // HALO
    nhb = t // HALO
    cg = d // N_GROUPS
    sw = min(cg, LANE)

    def body(p_ref, w_ref, dr_ref, ndr_ref, x_ref, g_ref, s_ref, b_ref, dx_ref, dw_ref, ds_ref, db_ref, dg_ref,
             qpad, dh):
        i = pl.program_id(0)
        last = (i % tps) == tps - 1
        pos = (i % tps) * ts + lax.broadcasted_iota(jnp.int32, (ts, 1), 0)

        @pl.when(i == 0)
        def _():
            dw_ref[...] = jnp.zeros_like(dw_ref)
            ds_ref[...] = jnp.zeros_like(ds_ref)
            db_ref[...] = jnp.zeros_like(db_ref)
            dg_ref[...] = jnp.zeros_like(dg_ref)

        for gi, win in enumerate(POOL_WINDOWS):
            sl = slice(gi * cg, (gi + 1) * cg)
            wv = w_ref[gi]
            sc = s_ref[:, sl]
            drv = dr_ref[:, sl]
            dmx = drv * sc
            dmx16 = dmx.astype(BF16)
            pooled = p_ref[:, sl]
            dw_ref[gi] += lax.dot_general(pooled, dmx16, (TN, ((), ())), preferred_element_type=F32)
            mixed = jnp.dot(pooled, wv, preferred_element_type=F32)
            ds_ref[:, sl] += jnp.sum(drv * (mixed + b_ref[:, sl]), axis=0, keepdims=True)
            db_ref[:, sl] += jnp.sum(dmx, axis=0, keepdims=True)
            cur = lax.dot_general(dmx16, wv, (NT, ((), ())), preferred_element_type=F32)
            nxt = lax.dot_general((ndr_ref[:, sl] * sc).astype(BF16), wv, (NT, ((), ())),
                                  preferred_element_type=F32)
            q_cur = cur / jnp.minimum(pos + 1, win).astype(F32)
            q_nxt = jnp.where(last, 0.0, nxt / float(win))
            for k, si in enumerate(range(gi * cg // sw, (gi + 1) * cg // sw)):
                part = slice(k * sw, (k + 1) * sw)
                qpad[si, 0:ts, :] = q_cur[:, part]
                qpad[si, ts:ts + HALO, :] = q_nxt[:, part]
                acc = -cur[:, part]
                for j in range(win):
                    acc = acc + qpad[si, pl.ds(j, ts), :]
                dh[:, si * sw:(si + 1) * sw] = acc
        dx, dgain_term, _ = _rms_bwd_tile(dh[...], x_ref[...], g_ref[...], dr_ref[...])
        dx_ref[...] = dx
        dg_ref[...] += jnp.sum(dgain_term, axis=0, keepdims=True)

    row = pl.BlockSpec((ts, d), lambda i: (i, 0))
    vec = pl.BlockSpec((1, d), lambda i: (0, 0))
    return pl.pallas_call(
        body, name=name, grid=(t // ts,),
        in_specs=[row, pl.BlockSpec((N_GROUPS, cg, cg), lambda i: (0, 0, 0)), row,
                  pl.BlockSpec((HALO, d), lambda i: (jnp.minimum((i + 1) * hb, nhb - 1), 0)), row, vec, vec, vec],
        out_specs=[row, pl.BlockSpec((N_GROUPS, cg, cg), lambda i: (0, 0, 0)), vec, vec, vec],
        out_shape=[jax.ShapeDtypeStruct((t, d), F32), jax.ShapeDtypeStruct((N_GROUPS, cg, cg), F32)]
        + [jax.ShapeDtypeStruct((1, d), F32)] * 3,
        scratch_shapes=[pltpu.VMEM((d // sw, ts + HALO, sw), F32), pltpu.VMEM((ts, d), F32)],
        compiler_params=_params("arbitrary"),
    )(pooled, wp, dr, dr, x, gain, scale, bias)


def _ctile(n, pref):
    return max(c for c in range(LANE, min(pref, n) + 1, LANE) if n % c == 0)


FFN_COLS = 1408
FFN_ROWS = 32
FFN_TILE = 1024


def _ffn_fwd(name, up, w, b, seq):
    _, t, dff = up.shape
    f = _ctile(dff, FFN_COLS)
    k_taps = w.shape[0]
    ts = _tile(seq, FFN_TILE)
    tps = seq // ts
    hb = ts // HALO16
    rc = _tile(ts, FFN_ROWS)

    def body(cur_ref, prev_ref, w_ref, b_ref, g_ref, apad):
        i = pl.program_id(1)
        first = (i % tps) == 0
        for ci, c0 in enumerate(range(0, f, LANE)):
            cols = slice(c0, c0 + LANE)
            apad[ci, 0:HALO16, :] = jnp.where(first, 0.0, prev_ref[:, cols].astype(F32))
            apad[ci, HALO16:HALO16 + ts, :] = cur_ref[0, :, cols].astype(F32)
            wv = w_ref[:, cols]
            wk = [jnp.broadcast_to(wv[k:k + 1, :], (rc, LANE)) for k in range(k_taps)]
            bias = jnp.broadcast_to(b_ref[:, cols], (rc, LANE))
            for r0 in range(0, ts, rc):
                c = bias
                for k in range(k_taps):
                    c = c + wk[k] * apad[ci, pl.ds(HALO16 - (k_taps - 1) + k + r0, rc), :]
                gate = cur_ref[1, pl.ds(r0, rc), cols].astype(F32)
                g_ref[pl.ds(r0, rc), cols] = (c * _sigmoid(c) * gate).astype(BF16)

    return pl.pallas_call(
        body, name=name, grid=(dff // f, t // ts),
        in_specs=[pl.BlockSpec((2, ts, f), lambda j, i: (0, i, j)),
                  pl.BlockSpec((None, HALO16, f), lambda j, i: (0, jnp.maximum(i * hb - 1, 0), j)),
                  pl.BlockSpec((k_taps, f), lambda j, i: (0, j)),
                  pl.BlockSpec((1, f), lambda j, i: (0, j))],
        out_specs=pl.BlockSpec((ts, f), lambda j, i: (i, j)),
        out_shape=jax.ShapeDtypeStruct((t, dff), BF16),
        scratch_shapes=[pltpu.VMEM((f // LANE, HALO16 + ts, LANE), F32)],
        compiler_params=_params("parallel", "parallel"),
    )(up, up, w, b)


def _ffn_bwd(name, up, dg, w, b, seq):
    _, t, dff = up.shape
    f = _ctile(dff, FFN_COLS)
    k_taps = w.shape[0]
    ts = _tile(seq, FFN_TILE)
    tps = seq // ts
    hb = ts // HALO16
    nhb = t // HALO16
    ext = ts + HALO16
    rc = _tile(ts, FFN_ROWS)

    def body(cur_ref, prev_ref, next_ref, dg_ref, ndg_ref, w_ref, b_ref, dup_ref, dw_ref, db_ref, apad, dcpad):
        i = pl.program_id(1)
        first = (i % tps) == 0
        last = (i % tps) == tps - 1

        @pl.when(i == 0)
        def _():
            dw_ref[...] = jnp.zeros_like(dw_ref)
            db_ref[...] = jnp.zeros_like(db_ref)

        for ci, c0 in enumerate(range(0, f, LANE)):
            cols = slice(c0, c0 + LANE)
            apad[ci, 0:HALO16, :] = jnp.where(first, 0.0, prev_ref[:, cols].astype(F32))
            apad[ci, HALO16:HALO16 + ts, :] = cur_ref[0, :, cols].astype(F32)
            apad[ci, HALO16 + ts:HALO16 + ext, :] = next_ref[0, :, cols].astype(F32)
            wv = w_ref[:, cols]
            wk = [jnp.broadcast_to(wv[k:k + 1, :], (rc, LANE)) for k in range(k_taps)]
            bias = jnp.broadcast_to(b_ref[:, cols], (rc, LANE))

            def conv_grad(r0, n, gate, dgv):
                c = bias[0:n]
                for k in range(k_taps):
                    c = c + wk[k][0:n] * apad[ci, pl.ds(HALO16 - (k_taps - 1) + k + r0, n), :]
                sig = _sigmoid(c)
                silu = c * sig
                return dgv * gate * (sig + silu * (1.0 - sig)), silu

            for r0 in range(0, ts, rc):
                dgv = dg_ref[pl.ds(r0, rc), cols].astype(F32)
                dc, silu = conv_grad(r0, rc, cur_ref[1, pl.ds(r0, rc), cols].astype(F32), dgv)
                dcpad[ci, pl.ds(r0, rc), :] = dc
                dup_ref[1, pl.ds(r0, rc), cols] = (dgv * silu).astype(BF16)
            dgv = jnp.where(last, 0.0, ndg_ref[:, cols].astype(F32))
            dc, _ = conv_grad(ts, HALO16, next_ref[1, :, cols].astype(F32), dgv)
            dcpad[ci, ts:ext, :] = dc

            dw_acc = [jnp.zeros((rc, LANE), F32) for _ in range(k_taps)]
            db_acc = jnp.zeros((rc, LANE), F32)
            for r0 in range(0, ts, rc):
                dact = jnp.zeros((rc, LANE), F32)
                for k in range(k_taps):
                    dact = dact + wk[k] * dcpad[ci, pl.ds(r0 + (k_taps - 1) - k, rc), :]
                dup_ref[0, pl.ds(r0, rc), cols] = dact.astype(BF16)
                dc = dcpad[ci, pl.ds(r0, rc), :]
                for k in range(k_taps):
                    dw_acc[k] = dw_acc[k] + dc * apad[ci, pl.ds(HALO16 - (k_taps - 1) + k + r0, rc), :]
                db_acc = db_acc + dc
            for k in range(k_taps):
                dw_ref[k:k + 1, cols] += jnp.sum(dw_acc[k], axis=0, keepdims=True)
            db_ref[:, cols] += jnp.sum(db_acc, axis=0, keepdims=True)

    return pl.pallas_call(
        body, name=name, grid=(dff // f, t // ts),
        in_specs=[pl.BlockSpec((2, ts, f), lambda j, i: (0, i, j)),
                  pl.BlockSpec((None, HALO16, f), lambda j, i: (0, jnp.maximum(i * hb - 1, 0), j)),
                  pl.BlockSpec((2, HALO16, f), lambda j, i: (0, jnp.minimum((i + 1) * hb, nhb - 1), j)),
                  pl.BlockSpec((ts, f), lambda j, i: (i, j)),
                  pl.BlockSpec((HALO16, f), lambda j, i: (jnp.minimum((i + 1) * hb, nhb - 1), j)),
                  pl.BlockSpec((k_taps, f), lambda j, i: (0, j)),
                  pl.BlockSpec((1, f), lambda j, i: (0, j))],
        out_specs=[pl.BlockSpec((2, ts, f), lambda j, i: (0, i, j)),
                   pl.BlockSpec((k_taps, f), lambda j, i: (0, j)),
                   pl.BlockSpec((1, f), lambda j, i: (0, j))],
        out_shape=[jax.ShapeDtypeStruct((2, t, dff), BF16), jax.ShapeDtypeStruct((k_taps, dff), F32),
                   jax.ShapeDtypeStruct((1, dff), F32)],
        scratch_shapes=[pltpu.VMEM((f // LANE, HALO16 + ext, LANE), F32), pltpu.VMEM((f // LANE, ext, LANE), F32)],
        compiler_params=_params("parallel", "arbitrary"),
    )(up, up, up, dg, dg, w, b)


def _sum_rows(name, g):
    ns, r, c = g.shape
    tr = _tile(r, 256)

    def body(g_ref, o_ref):
        acc = g_ref[0]
        for dev in range(1, ns):
            acc = acc + g_ref[dev]
        o_ref[...] = acc

    return pl.pallas_call(
        body, name=name, grid=(r // tr,),
        in_specs=[pl.BlockSpec((ns, tr, c), lambda i: (0, i, 0))],
        out_specs=pl.BlockSpec((tr, c), lambda i: (i, 0)),
        out_shape=jax.ShapeDtypeStruct((r, c), F32), compiler_params=_params("parallel"),
    )(g)


def _adam_step(g, w, m, v):
    m_new = ADAM_B1 * m + (1.0 - ADAM_B1) * g
    v_new = ADAM_B2 * v + (1.0 - ADAM_B2) * (g * g)
    m_hat = m_new / (1.0 - ADAM_B1 ** ADAM_STEP)
    v_hat = v_new / (1.0 - ADAM_B2 ** ADAM_STEP)
    return -ADAM_LR * (m_hat / (jnp.sqrt(v_hat) + ADAM_EPS) + ADAM_WD * w), m_new, v_new


def _adamw_small(name, gs, ws, ms, vs):
    n = len(ws)

    def rows(a):
        return a.reshape(-1, a.shape[-1])

    def body(*refs):
        ins, outs = refs[:4 * n], refs[4 * n:]
        for p in range(n):
            delta, m_new, v_new = _adam_step(ins[p][...], ins[n + p][...], ins[2 * n + p][...], ins[3 * n + p][...])
            outs[p][...], outs[n + p][...], outs[2 * n + p][...] = delta, m_new, v_new

    whole = pl.BlockSpec(memory_space=pltpu.VMEM)
    operands = [rows(a) for a in (*gs, *ws, *ms, *vs)]
    outs = pl.pallas_call(
        body, name=name, in_specs=[whole] * (4 * n), out_specs=[whole] * (3 * n),
        out_shape=[jax.ShapeDtypeStruct(rows(w).shape, F32) for _ in range(3) for w in ws],
        compiler_params=_params(),
    )(*operands)
    return [[o.reshape(w.shape) for o, w in zip(outs[kind * n:(kind + 1) * n], ws)] for kind in range(3)]


def _adamw(name, gsrc, w, m, v, layer=0, prev=None):
    ns, r, c = gsrc.shape
    nl = w.shape[0]
    tr = _tile(r, 256)
    prev = () if prev is None else tuple(prev)

    def body(g_ref, w_ref, m_ref, v_ref, *rest):
        go_ref, do_ref, mo_ref, vo_ref = rest[len(prev):]
        g = g_ref[0].astype(F32)
        for dev in range(1, ns):
            g = g + g_ref[dev].astype(F32)
        go_ref[...] = g
        do_ref[...], mo_ref[...], vo_ref[...] = _adam_step(g, w_ref[...], m_ref[...], v_ref[...])

    row = pl.BlockSpec((None, tr, c), lambda i: (layer, i, 0))
    return pl.pallas_call(
        body, name=name, grid=(r // tr,),
        in_specs=[pl.BlockSpec((ns, tr, c), lambda i: (0, i, 0)), row, row, row] + [ANY] * len(prev),
        out_specs=[row] * 4, out_shape=[jax.ShapeDtypeStruct((nl, r, c), F32)] * 4,
        input_output_aliases={4 + i: i for i in range(len(prev))},
        compiler_params=_params("parallel"),
    )(gsrc, w, m, v, *prev)


def _ffn_forward(tag, r_in, h, get_wu, get_wd, wdw, bdw, seq, loss=None):
    t, d = r_in.shape
    tm = _tile(t, 512)
    wu = get_wu(h)
    dff = wu.shape[0] // 2
    tu = _tile(t, 1024)
    up = _mm(f"{tag}_up", h, wu, grid=(2, t // tu, 1),
             a_spec=pl.BlockSpec((tu, d), lambda j, i, k: (i, 0)),
             b_spec=pl.BlockSpec((dff, d), lambda j, i, k: (j, 0)),
             out_spec=pl.BlockSpec((None, tu, dff), lambda j, i, k: (j, i, 0)),
             out_shape=jax.ShapeDtypeStruct((2, t, dff), BF16), dims=NT, acc_shape=(tu, dff))
    wd = get_wd(up)
    g = _ffn_fwd(f"{tag}_act", up, wdw, bdw, seq)
    row = pl.BlockSpec((tm, d), lambda i, j, k: (i, 0))
    vec = pl.BlockSpec((1, d), lambda i, j, k: (0, 0))
    common = dict(grid=(t // tm, 1, 1), a_spec=pl.BlockSpec((tm, dff), lambda i, j, k: (i, 0)),
                  b_spec=pl.BlockSpec((dff, d), lambda i, j, k: (0, 0)), dims=NN, acc_shape=(tm, d))
    if loss is None:
        out = _mm(f"{tag}_down", g, wd, out_spec=row, out_shape=jax.ShapeDtypeStruct((t, d), F32),
                  extras=(r_in,), extra_specs=(row,), epilogue=lambda acc, ex, rows: ((ex[0][rows, :] + acc,), ()),
                  **common)
    else:
        def head(acc, ex, rows):
            dx, part, dgain = _loss_tile(ex[0][rows, :] + acc, ex[1][rows, :], ex[2][...])
            return (dx,), (part, dgain)

        out = _mm(f"{tag}_down", g, wd, out_spec=[row, pl.BlockSpec((1, 1), lambda i, j, k: (0, 0)), vec],
                  out_shape=[jax.ShapeDtypeStruct((t, d), F32), jax.ShapeDtypeStruct((1, 1), F32),
                             jax.ShapeDtypeStruct((1, d), F32)],
                  extras=(r_in, *loss), extra_specs=(row, row, vec), epilogue=head, n_sums=2, **common)
    return out, (r_in, h, up, g, wu, wd)


def _ffn_backward(tag, dr, saved, gain, wdw, bdw, seq, token=None):
    r_in, h, up, g, wu, wd = saved
    t, d = r_in.shape
    dff = wd.shape[0]
    tm = _tile(t, 512)
    tk = _tile(t, 2048)
    tku = _tile(t, 4096)
    cw = _ctile(dff, 1408)
    nc = dff // cw
    once = dict(pipeline_mode=pl.Buffered(1)) if tku == t else {}
    dg = _mm(f"{tag}_dg", dr, wd, grid=(t // tm, 1, 1),
             a_spec=pl.BlockSpec((tm, d), lambda i, j, k: (i, 0)),
             b_spec=pl.BlockSpec((dff, d), lambda i, j, k: (0, 0)),
             out_spec=pl.BlockSpec((tm, dff), lambda i, j, k: (i, 0)),
             out_shape=jax.ShapeDtypeStruct((t, dff), BF16), dims=NT, acc_shape=(tm, dff), token=token)
    dwd = _mm(f"{tag}_dwd", g, dr, grid=(dff // cw, 1, t // tk),
              a_spec=pl.BlockSpec((tk, cw), lambda i, j, k: (k, i)),
              b_spec=pl.BlockSpec((tk, d), lambda i, j, k: (k, 0)),
              out_spec=pl.BlockSpec((cw, d), lambda i, j, k: (i, 0)),
              out_shape=jax.ShapeDtypeStruct((dff, d), BF16), dims=TN, acc_shape=(cw, d))
    dup, dwdw, dbdw = _ffn_bwd(f"{tag}_dact", up, dg, wdw, bdw, seq)
    row = pl.BlockSpec((tm, d), lambda i, j, k: (i, 0))
    vec = pl.BlockSpec((1, d), lambda i, j, k: (0, 0))

    def norm_backward(acc, ex, rows):
        dx, dgain, colsum = _rms_bwd_tile(acc, ex[0][rows, :], ex[1][...], ex[2][rows, :])
        return (dx,), (dgain, colsum)

    dr_in, dgain, colsum = _mm(
        f"{tag}_dh", dup, wu, grid=(t // tm, 1, 1),
        a_spec=pl.BlockSpec((2, tm, dff), lambda i, j, k: (0, i, 0)),
        b_spec=pl.BlockSpec((2 * dff, d), lambda i, j, k: (0, 0), pipeline_mode=pl.Buffered(1)),
        out_spec=[row, vec, vec],
        out_shape=[jax.ShapeDtypeStruct((t, d), F32)] + [jax.ShapeDtypeStruct((1, d), F32)] * 2,
        dims=NN, acc_shape=(tm, d), extras=(r_in, gain, dr), extra_specs=(row, vec, row),
        epilogue=norm_backward, n_sums=2, parts=2)
    dwu = _mm(f"{tag}_dwu", dup, h, grid=(2 * nc, 1, t // tku),
              a_spec=pl.BlockSpec((None, tku, cw), lambda i, j, k: (i // nc, k, i % nc)),
              b_spec=pl.BlockSpec((tku, d), lambda i, j, k: (k, 0), **once),
              out_spec=pl.BlockSpec((cw, d), lambda i, j, k: (i, 0)),
              out_shape=jax.ShapeDtypeStruct((2 * dff, d), BF16), dims=TN, acc_shape=(cw, d))
    return dr_in, dgain, dwu, dwd, dwdw, dbdw, colsum


def _pad_to(vec, n):
    return jnp.pad(vec, (0, n - vec.shape[0]))


def _pack(parts, width):
    flat = jnp.concatenate([p.reshape(-1).astype(F32) for p in parts])
    n = -(-flat.shape[0] // (8 * width)) * (8 * width)
    return _pad_to(flat, n).reshape(n // width, width)


def _unpack(mat, shapes):
    flat = mat.reshape(-1)
    out, off = [], 0
    for s in shapes:
        n = 1
        for dim in s:
            n *= dim
        out.append(flat[off:off + n].reshape(s))
        off += n
    return out


def kernel(x, norm_mix, norm_ffn, conv_w_pw1, conv_b_pw1, conv_w_dw, conv_b_dw, conv_ln_g, conv_ln_b, conv_w_pw2, conv_b_pw2, pool_w, pool_b, pool_scale, ffn_w_up, ffn_w_dw, ffn_b_dw, ffn_w_down, final_norm, loss_target, m_norm_mix, m_norm_ffn, m_conv_w_pw1, m_conv_b_pw1, m_conv_w_dw, m_conv_b_dw, m_conv_ln_g, m_conv_ln_b, m_conv_w_pw2, m_conv_b_pw2, m_pool_w, m_pool_b, m_pool_scale, m_ffn_w_up, m_ffn_w_dw, m_ffn_b_dw, m_ffn_w_down, m_final_norm, v_norm_mix, v_norm_ffn, v_conv_w_pw1, v_conv_b_pw1, v_conv_w_dw, v_conv_b_dw, v_conv_ln_g, v_conv_ln_b, v_conv_w_pw2, v_conv_b_pw2, v_pool_w, v_pool_b, v_pool_scale, v_ffn_w_up, v_ffn_w_dw, v_ffn_b_dw, v_ffn_w_down, v_final_norm):
    bsz, seq, d = x.shape
    t = bsz * seq
    k_taps = conv_w_dw.shape[1]
    cs1 = conv_w_pw1.shape[2]
    dsh = d // N_DEV
    cg = d // N_GROUPS
    cgs = pool_w.shape[2]
    fu = ffn_w_up.shape[2]
    fd = ffn_w_down.shape[1]
    dff = fd * N_DEV
    nb = N_DEV // 2
    kf = ffn_w_dw.shape[1]
    fsh = ffn_w_dw.shape[2]
    my = _lin(_me())
    tm = _tile(t, 512)

    x2 = x.reshape(t, d)
    tgt2 = loss_target.reshape(t, d)

    small_shapes = [(k_taps, dsh), (dsh,), (dsh,), (2, kf, fsh)]
    small_mine = _pack([conv_w_dw[0], pool_b[0], pool_scale[0], ffn_w_dw], LANE)
    big = [conv_w_pw1[0], conv_w_pw2[0], ffn_w_up[0].T, ffn_w_down[0], pool_w[0], ffn_w_up[1].T, ffn_w_down[1]]
    gather = _exchange_start("gather_start", [small_mine] + [w.astype(BF16) for w in big], [GATHER2] * 8)
    h0 = _rms_fwd("l0_rms", x2, norm_mix[0:1])
    forwarded = _exchange_forward("gather_forward_w1", gather, [0, 1], h0)
    small_all, w1 = _exchange_wait("gather_wait_w1", gather, [0, 1], forwarded)
    parts = [_unpack(small_all[dev], small_shapes) for dev in range(N_DEV)]
    wdw = jnp.concatenate([p[0] for p in parts], axis=1)
    pool_b_full = jnp.concatenate([p[1] for p in parts]).reshape(1, d)
    pool_s_full = jnp.concatenate([p[2] for p in parts]).reshape(1, d)
    fwdw = jnp.concatenate([p[3] for p in parts], axis=2)
    fbdw = ffn_b_dw.reshape(2, 1, dff)

    def columns(w):
        return w.transpose(1, 0, 2).reshape(w.shape[1], N_DEV * w.shape[2])

    def column_shards(w):
        return w.reshape(w.shape[0], N_DEV, w.shape[1] // N_DEV).transpose(1, 0, 2)

    w1 = columns(w1)
    a = _mm("l0_pw1", h0, w1, grid=(2, t // tm, 1),
            a_spec=pl.BlockSpec((tm, d), lambda j, i, k: (i, 0)),
            b_spec=pl.BlockSpec((d, d), lambda j, i, k: (0, j)),
            out_spec=pl.BlockSpec((None, tm, d), lambda j, i, k: (j, i, 0)),
            out_shape=jax.ShapeDtypeStruct((2, t, d), BF16), dims=NN, acc_shape=(tm, d),
            extras=(conv_b_pw1,), extra_specs=(pl.BlockSpec((1, d), lambda j, i, k: (0, j)),),
            epilogue=lambda acc, ex, rows: ((acc + ex[0][...],), ()))
    v = _conv_fwd("l0_conv", a, wdw, conv_b_dw, seq)
    forwarded = _exchange_forward("gather_forward_wu0", gather, [2, 3], v)
    (w2,) = _exchange_wait("gather_wait_w2", gather, [2], forwarded)
    w2 = w2.reshape(d, d)
    row = pl.BlockSpec((tm, d), lambda i, j, k: (i, 0))
    vec = pl.BlockSpec((1, d), lambda i, j, k: (0, 0))
    square = pl.BlockSpec((d, d), lambda i, j, k: (0, 0))

    def ln_silu(v_blk, ex):
        s_blk = _ln_silu_tile(v_blk, ex[0][...], ex[1][...]).astype(BF16)
        return s_blk, s_blk

    def residual_and_norm(acc, ex, rows):
        r_blk = ex[3][rows, :] + (acc + ex[2][...])
        return (r_blk, _rms(r_blk, ex[4][...])), ()

    r1, h1, s = _mm("l0_pw2", v, w2, grid=(t // tm, 1, 1), a_spec=row, b_spec=square, out_spec=[row, row, row],
                    out_shape=[jax.ShapeDtypeStruct((t, d), F32), jax.ShapeDtypeStruct((t, d), BF16),
                               jax.ShapeDtypeStruct((t, d), BF16)],
                    dims=NN, acc_shape=(tm, d), extras=(conv_ln_g, conv_ln_b, conv_b_pw2, x2, norm_ffn[0:1]),
                    extra_specs=(vec, vec, vec, row, vec), prologue=ln_silu, epilogue=residual_and_norm)

    def up_getter(name, idx):
        return lambda after: _exchange_wait(name, gather, [idx], after)[0].reshape(2 * dff, d)

    def down_getter(name, idx, forward=None):
        def get(after):
            if forward is not None:
                after = _exchange_forward(forward[0], gather, forward[1], after)
            return _exchange_wait(name, gather, [idx], after)[0].reshape(dff, d)
        return get

    r2, ffn0_saved = _ffn_forward("f0", r1, h1, up_getter("gather_wait_wu0", 3),
                                  down_getter("gather_wait_wd0", 4, ("gather_forward_wu1", [4, 5, 6])),
                                  fwdw[0], fbdw[0], seq)
    forwarded = _exchange_forward("gather_forward_wd1", gather, [7], r2)
    (wp,) = _exchange_wait("gather_wait_wp", gather, [5], forwarded)
    wp = wp.transpose(1, 0, 2, 3).reshape(N_GROUPS, cg, cg)
    pooled, r3, h3 = _pool_mix_fwd("l1_mix", r2, norm_mix[1:2], wp, pool_s_full, pool_b_full, norm_ffn[1:2], seq)
    (dr4, loss_part, dfinal), ffn1_saved = _ffn_forward(
        "f1", r3, h3, up_getter("gather_wait_wu1", 6), down_getter("gather_wait_wd1", 7), fwdw[1], fbdw[1], seq,
        loss=(tgt2, final_norm.reshape(1, d)))

    dr3, dnf1, dwu1, dwd1, dfw1, dfb1, _ = _ffn_backward("f1", dr4, ffn1_saved, norm_ffn[1:2], fwdw[1], fbdw[1], seq)
    scatter_a = _exchange_start("scatter_f1_start", [dwu1.reshape(N_DEV, fu, d), dwd1.reshape(N_DEV, fd, d)],
                                [SCATTER, SCATTER])
    dr2, dwp, dpool_s, dpool_b, dnm1 = _pool_mix_bwd(
        "l1_dmix", pooled, wp, dr3, r2, norm_mix[1:2], pool_s_full + scatter_a["token"][0:1, 0:1], pool_b_full, seq)
    dr1, dnf0, dwu0, dwd0, dfw0, dfb0, db2 = _ffn_backward("f0", dr2, ffn0_saved, norm_ffn[0:1], fwdw[0], fbdw[0], seq)
    dwp_b = dwp.astype(BF16).reshape(N_GROUPS, N_DEV, cgs, cg).transpose(1, 0, 2, 3)
    tk = _tile(t, 2048)
    dw2 = _mm("l0_dw2", s, dr1, grid=(1, 1, t // tk),
              a_spec=pl.BlockSpec((tk, d), lambda i, j, k: (k, 0)),
              b_spec=pl.BlockSpec((tk, d), lambda i, j, k: (k, 0)),
              out_spec=pl.BlockSpec((d, d), lambda i, j, k: (0, 0)),
              out_shape=jax.ShapeDtypeStruct((d, d), BF16), dims=TN, acc_shape=(d, d))
    scatter_b = _exchange_start("scatter_f0_start", [dwu0.reshape(N_DEV, fu, d), dwd0.reshape(N_DEV, fd, d), dwp_b,
                                                     dw2.reshape(N_DEV, d // N_DEV, d)], [SCATTER] * 4)

    def ln_silu_backward(acc, ex, rows):
        dv_blk, dgain, dbias, colsum = _ln_silu_bwd_tile(acc, ex[0][rows, :], ex[1][...], ex[2][...])
        return (dv_blk,), (dgain, dbias, colsum)

    dv, dlg, dlb, dbdw = _mm("l0_ds", dr1, w2, grid=(t // tm, 1, 1), a_spec=row, b_spec=square,
                             out_spec=[row, vec, vec, vec],
                             out_shape=[jax.ShapeDtypeStruct((t, d), F32)] + [jax.ShapeDtypeStruct((1, d), F32)] * 3,
                             dims=NT, acc_shape=(tm, d), extras=(v, conv_ln_g, conv_ln_b), extra_specs=(row, vec, vec),
                             epilogue=ln_silu_backward, n_sums=3, token=scatter_b["token"])
    da, dwdw, db1 = _conv_bwd("l0_dconv", a, dv, wdw, seq)
    tk1 = _tile(t, 4096)
    once = dict(pipeline_mode=pl.Buffered(1)) if tk1 == t else {}
    dw1 = _mm("l0_dw1", h0, da, grid=(1, 2, t // tk1),
              a_spec=pl.BlockSpec((tk1, d), lambda i, j, k: (k, 0), **once),
              b_spec=pl.BlockSpec((None, tk1, d), lambda i, j, k: (j, k, 0)),
              out_spec=pl.BlockSpec((d, d), lambda i, j, k: (0, j)),
              out_shape=jax.ShapeDtypeStruct((d, 2 * d), BF16), dims=TN, acc_shape=(d, d))
    scatter_c = _exchange_start("scatter_l0_start", [column_shards(dw1)], [SCATTER])
    def norm_backward(acc, ex, rows):
        dx_blk, dgain, colsum = _rms_bwd_tile(acc, ex[0][rows, :], ex[1][...], ex[2][rows, :])
        return (dx_blk,), (dgain, colsum)

    dx, dnm0, _ = _mm("l0_dh", da, w1, grid=(t // tm, 1, 1),
                      a_spec=pl.BlockSpec((2, tm, d), lambda i, j, k: (0, i, 0)),
                      b_spec=pl.BlockSpec((d, 2 * d), lambda i, j, k: (0, 0), pipeline_mode=pl.Buffered(1)),
                      out_spec=[row, vec, vec],
                      out_shape=[jax.ShapeDtypeStruct((t, d), F32)] + [jax.ShapeDtypeStruct((1, d), F32)] * 2,
                      dims=NT, acc_shape=(tm, d), extras=(x2, norm_mix[0:1], dr1), extra_specs=(row, vec, row),
                      epilogue=norm_backward, n_sums=2, parts=2, token=scatter_c["token"])

    dffn_w = jnp.stack([dfw0, dfw1])
    dffn_b = jnp.stack([dfb0, dfb1]).reshape(2, dff)
    small_parts = [loss_part, jnp.concatenate([dnm0, dnm1]), jnp.concatenate([dnf0, dnf1]), db1, dwdw, dbdw, dlg, dlb,
                   db2, dpool_b, dpool_s, dffn_w, dffn_b, dfinal]
    small_part_shapes = [(1,), (2, d), (2, d), (1, 2 * d), (k_taps, d), (1, d), (1, d), (1, d), (1, d), (1, d), (1, d),
                         (2, kf, dff), (2, dff), (d,)]
    packed = _pack(small_parts, 8 * LANE)
    gather_small = _exchange_start("gather_small_start", [packed], [GATHER])

    def big_update(name, recv, w, m, v, layer=0, prev=None):
        shape = w.shape
        c = recv.shape[-1]
        rows = recv.size // (N_DEV * c)
        nl = w.size // (rows * c)
        outs = _adamw(name, recv.reshape(N_DEV, rows, c), w.reshape(nl, rows, c), m.reshape(nl, rows, c),
                      v.reshape(nl, rows, c), layer, prev)
        return outs, [o.reshape(shape) for o in outs]

    wu_t = [p.transpose(0, 2, 1) for p in (ffn_w_up, m_ffn_w_up, v_ffn_w_up)]
    g_wu1, g_wd1 = _exchange_wait("scatter_f1_wait", scatter_a, [0, 1], gather_small["token"])
    raw_wu, _ = big_update("adam_wu1", g_wu1, *wu_t, 1)
    raw_wd, _ = big_update("adam_wd1", g_wd1, ffn_w_down, m_ffn_w_down, v_ffn_w_down, 1)
    g_wu0, g_wd0, g_wp, g_w2 = _exchange_wait("scatter_f0_wait", scatter_b, [0, 1, 2, 3], raw_wd[0])
    _, u_wu = big_update("adam_wu0", g_wu0, *wu_t, 0, raw_wu)
    u_wu = [o.transpose(0, 2, 1) for o in u_wu]
    _, u_wd = big_update("adam_wd0", g_wd0, ffn_w_down, m_ffn_w_down, v_ffn_w_down, 0, raw_wd)
    _, u_wp = big_update("adam_wp", g_wp, pool_w, m_pool_w, v_pool_w)
    (g_w1,) = _exchange_wait("scatter_l0_wait", scatter_c, [0], u_wp[0])
    _, u_w1 = big_update("adam_w1", g_w1, conv_w_pw1, m_conv_w_pw1, v_conv_w_pw1)
    _, u_w2 = big_update("adam_w2", g_w2, conv_w_pw2, m_conv_w_pw2, v_conv_w_pw2)
    (all_small,) = _exchange_wait("gather_small_wait", gather_small, [0], u_w2[0])
    summed = _sum_rows("sum_small_grads", all_small)
    (loss_v, g_nm, g_nf, g_b1, g_wdw, g_bdw, g_lg, g_lb, g_b2, g_pb, g_ps, g_fw, g_fb,
     g_fin) = _unpack(summed, small_part_shapes)
    loss = loss_v[0]
    g_wdw_mine = lax.dynamic_slice_in_dim(g_wdw, my * dsh, dsh, axis=1)[None]
    g_pb_mine = lax.dynamic_slice_in_dim(g_pb, my * dsh, dsh, axis=1)
    g_ps_mine = lax.dynamic_slice_in_dim(g_ps, my * dsh, dsh, axis=1)
    g_fw_mine = lax.dynamic_slice_in_dim(g_fw, my * fsh, fsh, axis=2)

    small_g =[g_nm, g_nf, g_b1, g_wdw_mine, g_bdw, g_lg, g_lb, g_b2, g_pb_mine, g_ps_mine, g_fw_mine, g_fb, g_fin]
    small_w = [norm_mix, norm_ffn, conv_b_pw1, conv_w_dw, conv_b_dw, conv_ln_g, conv_ln_b, conv_b_pw2, pool_b,
               pool_scale, ffn_w_dw, ffn_b_dw, final_norm]
    small_m = [m_norm_mix, m_norm_ffn, m_conv_b_pw1, m_conv_w_dw, m_conv_b_dw, m_conv_ln_g, m_conv_ln_b,
               m_conv_b_pw2, m_pool_b, m_pool_scale, m_ffn_w_dw, m_ffn_b_dw, m_final_norm]
    small_v = [v_norm_mix, v_norm_ffn, v_conv_b_pw1, v_conv_w_dw, v_conv_b_dw, v_conv_ln_g, v_conv_ln_b,
               v_conv_b_pw2, v_pool_b, v_pool_scale, v_ffn_w_dw, v_ffn_b_dw, v_final_norm]
    sg = [g.reshape(w.shape) for g, w in zip(small_g, small_w)]
    sd, sm, sv = _adamw_small("adam_small", sg, small_w, small_m, small_v)

    def leaf(kind):
        (nm, nf, b1, wdw_, bdw_, lg, lb, b2, pb, ps, fw, fb, fin) = (sg, sd, sm, sv)[kind]
        return [nm, nf, u_w1[kind], b1, wdw_, bdw_, lg, lb, u_w2[kind], b2, u_wp[kind], pb, ps, u_wu[kind], fw, fb,
                u_wd[kind], fin]

    return (loss, dx.reshape(bsz, seq, d), *leaf(0), *leaf(1), *leaf(2), *leaf(3))
```

```python
import functools

import jax
import jax.numpy as jnp
from jax import lax
from jax.experimental import pallas as pl
from jax.experimental.pallas import tpu as pltpu

F32 = jnp.float32
BF16 = jnp.bfloat16
MESH = pl.DeviceIdType.MESH
HBM = pl.BlockSpec(memory_space=pltpu.HBM)

N_DEV = 8
RMS_EPS = 1e-6
LN_EPS = 1e-5
POOL_WINDOWS = (2, 4, 8, 16)
N_GROUPS = len(POOL_WINDOWS)
ADAM_LR = 0.001
ADAM_B1 = 0.9
ADAM_B2 = 0.999
ADAM_EPS = 1e-08
ADAM_WD = 0.01
ADAM_STEP = 10

LANE = 128
HALO = 32
HALO16 = 16
VMEM_LIMIT = 56 * 1024 * 1024


def _params(*sem):
    return pltpu.CompilerParams(dimension_semantics=sem if sem else None, vmem_limit_bytes=VMEM_LIMIT)


def _tile(n, pref):
    for t in range(min(pref, n), 15, -1):
        if n % t == 0 and t % 16 == 0:
            return t
    return n


def _sigmoid(z):
    return 1.0 / (1.0 + jnp.exp(-z))


def _me():
    return lax.axis_index("x"), lax.axis_index("y"), lax.axis_index("c")


def _flip(pos, m):
    x, y, c = pos
    return ((1 - x) if m & 4 else x, (1 - y) if m & 2 else y, (1 - c) if m & 1 else c)


def _lin(pos):
    return 4 * pos[0] + 2 * pos[1] + pos[2]


SEM = pl.BlockSpec(memory_space=pltpu.SEMAPHORE)
ANY = pl.BlockSpec(memory_space=pl.ANY)
EFFECT = pltpu.SideEffectType.DATAFLOW_SIDE_EFFECTING


SCATTER = "scatter"
GATHER = "gather"
GATHER2 = "gather2"
ALL_MASKS = (1, 2, 3, 4, 5, 6, 7)
SIBLING = 1
CHIPS = (2, 4, 6)


class _Copies:
    def __init__(self, a, mode, src, land, send_sems, recv_sems):
        self.a, self.mode, self.src, self.land = a, mode, src, land
        self.send_sems, self.recv_sems = send_sems, recv_sems
        self.me = _me()
        self.first = (SIBLING,) + CHIPS if mode == GATHER2 else ALL_MASKS

    def _sems(self, m, to):
        return dict(send_sem=self.send_sems.at[self.a * N_DEV + m], recv_sem=self.recv_sems.at[self.a * N_DEV + m],
                    device_id=to, device_id_type=MESH)

    def _block(self, pid):
        return self.src.at[pid] if self.mode == SCATTER else self.src

    def local(self):
        my = _lin(self.me)
        return pltpu.make_async_copy(self._block(my), self.land.at[my], self.send_sems.at[self.a * N_DEV])

    def send(self, m):
        peer = _flip(self.me, m)
        return pltpu.make_async_remote_copy(src_ref=self._block(_lin(peer)), dst_ref=self.land.at[_lin(self.me)],
                                            **self._sems(m, peer))

    def arrival(self, m):
        rows = self.land.at[_lin(_flip(self.me, m))]
        return pltpu.make_async_remote_copy(src_ref=rows, dst_ref=rows, **self._sems(m, _flip(self.me, m)))

    def forward(self, m):
        rows = self.land.at[_lin(_flip(self.me, m))]
        return pltpu.make_async_remote_copy(src_ref=rows, dst_ref=rows, **self._sems(m | 1, _flip(self.me, SIBLING)))


def _exchange_start(name, arrs, modes):
    n = len(arrs)
    blocks = [a.shape[1:] if md == SCATTER else a.shape for a, md in zip(arrs, modes)]

    def body(*refs):
        srcs, lands = refs[:n], refs[n:2 * n]
        send_sems, recv_sems = refs[2 * n], refs[2 * n + 1]
        token = refs[-1]
        for a in range(n):
            cp = _Copies(a, modes[a], srcs[a], lands[a], send_sems, recv_sems)
            cp.local().start()
            for m in cp.first:
                cp.send(m).start()
        token[...] = jnp.zeros_like(token)

    lands = [lax.empty((N_DEV,) + tuple(b), a.dtype) for a, b in zip(arrs, blocks)]
    outs = pl.pallas_call(
        body, name=name,
        out_shape=(pltpu.SemaphoreType.DMA((n * N_DEV,)), pltpu.SemaphoreType.DMA((n * N_DEV,)),
                   *[pltpu.HBM(a.shape, a.dtype) for a in arrs], *[pltpu.HBM(l.shape, l.dtype) for l in lands],
                   jax.ShapeDtypeStruct((8, LANE), F32)),
        in_specs=[HBM] * (2 * n),
        out_specs=(SEM, SEM, *[HBM] * (2 * n), pl.BlockSpec(memory_space=pltpu.VMEM)),
        input_output_aliases={i: 2 + i for i in range(2 * n)},
        compiler_params=pltpu.CompilerParams(has_side_effects=EFFECT),
    )(*[pltpu.with_memory_space_constraint(a, pltpu.HBM) for a in arrs],
      *[pltpu.with_memory_space_constraint(l, pltpu.HBM) for l in lands])
    return dict(send=outs[0], recv=outs[1], srcs=list(outs[2:2 + n]), lands=list(outs[2 + n:2 + 2 * n]),
                modes=modes, token=outs[-1])


def _exchange_forward(name, handle, which, after):
    k = len(which)

    def half(wait):
        def body(*refs):
            lands = refs[:k]
            send_sems, recv_sems = refs[k], refs[k + 1]
            token = refs[-1]
            for pos, a in enumerate(which):
                cp = _Copies(a, GATHER2, None, lands[pos], send_sems, recv_sems)
                for m in CHIPS:
                    if wait:
                        cp.arrival(m).wait_recv()
                    else:
                        cp.forward(m).start()
            token[...] = jnp.zeros_like(token)
        return body

    def call(body, call_name, lands, after):
        outs = pl.pallas_call(
            body, name=call_name,
            out_shape=(*[pltpu.HBM(x.shape, x.dtype) for x in lands], jax.ShapeDtypeStruct((8, LANE), F32)),
            in_specs=[HBM] * k + [SEM, SEM, ANY], out_specs=(*[HBM] * k, pl.BlockSpec(memory_space=pltpu.VMEM)),
            input_output_aliases={i: i for i in range(k)},
            compiler_params=pltpu.CompilerParams(has_side_effects=EFFECT),
        )(*lands, handle["send"], handle["recv"], after)
        return list(outs[:k]), outs[-1]

    lands, arrived = call(half(True), name + "_arrived", [handle["lands"][a] for a in which], after)
    lands, token = call(half(False), name, lands, arrived)
    for pos, a in enumerate(which):
        handle["lands"][a] = lands[pos]
    return token


def _exchange_wait(name, handle, which, after):
    k = len(which)
    modes = handle["modes"]

    def body(*refs):
        srcs, lands = refs[:k], refs[k:2 * k]
        send_sems, recv_sems = refs[2 * k], refs[2 * k + 1]
        for pos, a in enumerate(which):
            cp = _Copies(a, modes[a], srcs[pos], lands[pos], send_sems, recv_sems)
            cp.local().wait()
            for m in cp.first:
                cp.send(m).wait_send()
            if modes[a] == GATHER2:
                for m in CHIPS:
                    cp.forward(m).wait_send()
                arrivals = (SIBLING,) + tuple(m | 1 for m in CHIPS)
            else:
                arrivals = ALL_MASKS
            for m in arrivals:
                cp.arrival(m).wait_recv()

    srcs = [handle["srcs"][a] for a in which]
    lands = [handle["lands"][a] for a in which]
    outs = pl.pallas_call(
        body, name=name,
        out_shape=tuple(pltpu.HBM(x.shape, x.dtype) for x in srcs + lands),
        in_specs=[HBM] * (2 * k) + [SEM, SEM, ANY], out_specs=tuple([HBM] * (2 * k)),
        input_output_aliases={i: i for i in range(2 * k)},
        compiler_params=pltpu.CompilerParams(has_side_effects=EFFECT),
    )(*srcs, *lands, handle["send"], handle["recv"], after)
    for pos, a in enumerate(which):
        handle["srcs"][a], handle["lands"][a] = outs[pos], outs[k + pos]
    return list(outs[k:])


def _mm(name, a, b, *, grid, a_spec, b_spec, out_spec, out_shape, dims, acc_shape, extras=(), extra_specs=(),
        epilogue=None, token=None, prologue=None, n_sums=0, parts=1):
    nk = grid[2]
    ne = len(extras)
    deps = () if token is None else (token,)
    dep_specs = [pl.BlockSpec((8, LANE), lambda i, j, k: (0, 0))] * len(deps)
    n_out = len(out_shape) if isinstance(out_shape, (list, tuple)) else 1
    n_tiles = n_out - n_sums - (1 if prologue is not None else 0)

    def body(a_ref, b_ref, *rest):
        ex, o_refs, acc_ref = rest[:ne], rest[ne + len(deps):ne + len(deps) + n_out], rest[ne + len(deps) + n_out]
        k = pl.program_id(2)
        if parts == 1:
            a_blk, saved = a_ref[...], None
            if prologue is not None:
                a_blk, saved = prologue(a_blk, ex)
                o_refs[n_tiles][...] = saved
            part = lax.dot_general(a_blk.astype(BF16), b_ref[...].astype(BF16), (dims, ((), ())),
                                   preferred_element_type=F32)
        else:
            kb = b_ref.shape[dims[1][0]] // parts
            part = None
            for p in range(parts):
                b_blk = b_ref[p * kb:(p + 1) * kb, :] if dims[1][0] == 0 else b_ref[:, p * kb:(p + 1) * kb]
                term = lax.dot_general(a_ref[p].astype(BF16), b_blk.astype(BF16), (dims, ((), ())),
                                       preferred_element_type=F32)
                part = term if part is None else part + term
        sum_refs = o_refs[n_out - n_sums:]

        def add_sums(terms):
            @pl.when((pl.program_id(0) == 0) & (pl.program_id(1) == 0))
            def _():
                for o_ref in sum_refs:
                    o_ref[...] = jnp.zeros_like(o_ref)

            for o_ref, term in zip(sum_refs, terms):
                o_ref[...] += jnp.sum(term, axis=0, keepdims=True)

        def finish(r):
            tiles, terms = ((r,), ()) if epilogue is None else epilogue(r, ex, slice(None))
            for o_ref, val in zip(o_refs, tiles):
                o_ref[...] = val.astype(o_ref.dtype)
            if n_sums:
                add_sums(terms)

        if nk == 1:
            finish(part)
            return

        @pl.when(k == 0)
        def _():
            acc_ref[...] = part

        @pl.when((k > 0) & (k < nk - 1))
        def _():
            acc_ref[...] += part

        @pl.when(k == nk - 1)
        def _():
            finish(acc_ref[...] + part)

    return pl.pallas_call(
        body, name=name, grid=grid, in_specs=[a_spec, b_spec, *extra_specs, *dep_specs], out_specs=out_spec,
        out_shape=out_shape, scratch_shapes=[pltpu.VMEM(acc_shape if nk > 1 else (8, LANE), F32)],
        compiler_params=_params(*(("arbitrary",) * 3 if n_sums else ("parallel", "parallel", "arbitrary"))),
    )(a, b, *extras, *deps)


def _rms(x, gain):
    return x * lax.rsqrt(jnp.mean(x * x, axis=-1, keepdims=True) + RMS_EPS) * gain


def _rms_bwd_tile(dh, x, gain, dres):
    rstd = lax.rsqrt(jnp.mean(x * x, axis=-1, keepdims=True) + RMS_EPS)
    xhat = x * rstd
    dxhat = dh * gain
    dx = dres + rstd * (dxhat - xhat * jnp.mean(dxhat * xhat, axis=-1, keepdims=True))
    return dx, dh * xhat, dx


def _ln_silu_tile(v, g, b):
    mu = jnp.mean(v, axis=-1, keepdims=True)
    cen = v - mu
    z = cen * lax.rsqrt(jnp.mean(cen * cen, axis=-1, keepdims=True) + LN_EPS) * g + b
    return z * _sigmoid(z)


def _ln_silu_bwd_tile(ds, v, g, b):
    mu = jnp.mean(v, axis=-1, keepdims=True)
    cen = v - mu
    rstd = lax.rsqrt(jnp.mean(cen * cen, axis=-1, keepdims=True) + LN_EPS)
    y = cen * rstd
    z = y * g + b
    sig = _sigmoid(z)
    dz = ds * sig * (1.0 + z * (1.0 - sig))
    dy = dz * g
    dv = rstd * (dy - jnp.mean(dy, axis=-1, keepdims=True) - y * jnp.mean(dy * y, axis=-1, keepdims=True))
    return dv, dz * y, dz, dv


def _loss_tile(x, tgt, gain):
    d = x.shape[-1]
    rstd = lax.rsqrt(jnp.mean(x * x, axis=-1, keepdims=True) + RMS_EPS)
    xhat = x * rstd
    err = xhat * gain - tgt
    dy = err / d
    dxhat = dy * gain
    dx = rstd * (dxhat - xhat * jnp.mean(dxhat * xhat, axis=-1, keepdims=True))
    return dx, 0.5 * jnp.mean(err * err, axis=-1, keepdims=True), dy * xhat


NN = ((1,), (0,))
NT = ((1,), (1,))
TN = ((0,), (0,))


def _rms_fwd(name, x, gain):
    t, d = x.shape
    tr = _tile(t, 512)

    def body(x_ref, g_ref, h_ref):
        h_ref[...] = _rms(x_ref[...], g_ref[...]).astype(BF16)

    return pl.pallas_call(
        body, name=name, grid=(t // tr,),
        in_specs=[pl.BlockSpec((tr, d), lambda i: (i, 0)), pl.BlockSpec((1, d), lambda i: (0, 0))],
        out_specs=pl.BlockSpec((tr, d), lambda i: (i, 0)),
        out_shape=jax.ShapeDtypeStruct((t, d), BF16), compiler_params=_params("parallel"),
    )(x, gain)


def _conv_tiles(t, seq):
    ts = _tile(seq, 1024)
    return ts, seq // ts, _tile(ts, 64)


def _conv_fwd(name, a, w, b, seq):
    _, t, d = a.shape
    k_taps = w.shape[0]
    ts, tps, rc = _conv_tiles(t, seq)
    hb = ts // HALO

    def body(cur_ref, prev_ref, w_ref, b_ref, v_ref, upad):
        i = pl.program_id(1)
        first = (i % tps) == 0
        pv = prev_ref[0].astype(F32)
        pg = prev_ref[1].astype(F32)
        upad[0:HALO, :] = jnp.where(first, 0.0, pv * _sigmoid(pg))
        upad[HALO:HALO + ts, :] = cur_ref[0].astype(F32) * _sigmoid(cur_ref[1].astype(F32))
        wv = w_ref[...]
        bias = jnp.broadcast_to(b_ref[...], (rc, LANE))
        for r0 in range(0, ts, rc):
            acc = bias
            for k in range(k_taps):
                acc = acc + wv[k:k + 1, :] * upad[pl.ds(HALO - (k_taps - 1) + k + r0, rc), :]
            v_ref[pl.ds(r0, rc), :] = acc

    return pl.pallas_call(
        body, name=name, grid=(d // LANE, t // ts),
        in_specs=[pl.BlockSpec((2, ts, LANE), lambda c, i: (0, i, c)),
                  pl.BlockSpec((2, HALO, LANE), lambda c, i: (0, jnp.maximum(i * hb - 1, 0), c)),
                  pl.BlockSpec((k_taps, LANE), lambda c, i: (0, c)),
                  pl.BlockSpec((1, LANE), lambda c, i: (0, c))],
        out_specs=pl.BlockSpec((ts, LANE), lambda c, i: (i, c)),
        out_shape=jax.ShapeDtypeStruct((t, d), F32),
        scratch_shapes=[pltpu.VMEM((HALO + ts, LANE), F32)],
        compiler_params=_params("parallel", "parallel"),
    )(a, a, w, b)


def _conv_bwd(name, a, dv, w, seq):
    _, t, d = a.shape
    k_taps = w.shape[0]
    ts, tps, rc = _conv_tiles(t, seq)
    hb = ts // HALO
    nhb = t // HALO

    def body(cur_ref, prev_ref, dv_ref, ndv_ref, w_ref, da_ref, dw_ref, dbp_ref, upad, dvpad, dwrows):
        i = pl.program_id(1)
        first = (i % tps) == 0
        last = (i % tps) == tps - 1
        pv = prev_ref[0].astype(F32)
        pg = prev_ref[1].astype(F32)
        upad[0:HALO, :] = jnp.where(first, 0.0, pv * _sigmoid(pg))
        upad[HALO:HALO + ts, :] = cur_ref[0].astype(F32) * _sigmoid(cur_ref[1].astype(F32))
        dvpad[0:ts, :] = dv_ref[...]
        dvpad[ts:ts + HALO, :] = jnp.where(last, 0.0, ndv_ref[...])
        wv = w_ref[...]

        @pl.when(i == 0)
        def _():
            dw_ref[...] = jnp.zeros_like(dw_ref)
            dbp_ref[...] = jnp.zeros_like(dbp_ref)

        sv = jnp.zeros((1, LANE), F32)
        sg = jnp.zeros((1, LANE), F32)
        for r0 in range(0, ts, rc):
            du = jnp.zeros((rc, LANE), F32)
            for k in range(k_taps):
                du = du + wv[k:k + 1, :] * dvpad[pl.ds(r0 + (k_taps - 1) - k, rc), :]
            av = cur_ref[0, pl.ds(r0, rc), :].astype(F32)
            sig = _sigmoid(cur_ref[1, pl.ds(r0, rc), :].astype(F32))
            dval = du * sig
            dgate = du * av * sig * (1.0 - sig)
            da_ref[0, pl.ds(r0, rc), :] = dval.astype(BF16)
            da_ref[1, pl.ds(r0, rc), :] = dgate.astype(BF16)
            sv = sv + jnp.sum(dval, axis=0, keepdims=True)
            sg = sg + jnp.sum(dgate, axis=0, keepdims=True)
        dbp_ref[0] += sv
        dbp_ref[1] += sg

        for k in range(k_taps):
            acc = jnp.zeros((rc, LANE), F32)
            for r0 in range(0, ts, rc):
                acc = acc + dvpad[pl.ds(r0, rc), :] * upad[pl.ds(HALO - (k_taps - 1) + k + r0, rc), :]
            dwrows[k:k + 1, :] = jnp.sum(acc, axis=0, keepdims=True)
        dw_ref[...] += dwrows[0:k_taps, :]

    return pl.pallas_call(
        body, name=name, grid=(d // LANE, t // ts),
        in_specs=[pl.BlockSpec((2, ts, LANE), lambda c, i: (0, i, c)),
                  pl.BlockSpec((2, HALO, LANE), lambda c, i: (0, jnp.maximum(i * hb - 1, 0), c)),
                  pl.BlockSpec((ts, LANE), lambda c, i: (i, c)),
                  pl.BlockSpec((HALO, LANE), lambda c, i: (jnp.minimum((i + 1) * hb, nhb - 1), c)),
                  pl.BlockSpec((k_taps, LANE), lambda c, i: (0, c))],
        out_specs=[pl.BlockSpec((2, ts, LANE), lambda c, i: (0, i, c)),
                   pl.BlockSpec((k_taps, LANE), lambda c, i: (0, c)),
                   pl.BlockSpec((2, 1, LANE), lambda c, i: (0, 0, c))],
        out_shape=[jax.ShapeDtypeStruct((2, t, d), BF16), jax.ShapeDtypeStruct((k_taps, d), F32),
                   jax.ShapeDtypeStruct((2, 1, d), F32)],
        scratch_shapes=[pltpu.VMEM((HALO + ts, LANE), F32), pltpu.VMEM((ts + HALO, LANE), F32),
                        pltpu.VMEM((HALO, LANE), F32)],
        compiler_params=_params("parallel", "arbitrary"),
    )(a, a, dv, dv, w)


def _pool_mix_fwd(name, x, gain, wp, scale, bias, next_gain, seq):
    t, d = x.shape
    ts = _tile(seq, 256)
    tps = seq // ts
    hb = ts // HALO
    cg = d // N_GROUPS
    sw = min(cg, LANE)

    def body(cur_ref, prev_ref, g_ref, w_ref, s_ref, b_ref, ng_ref, p_ref, r_ref, h_ref, hpad):
        i = pl.program_id(0)
        first = (i % tps) == 0
        g = g_ref[...]
        h_prev = jnp.where(first, 0.0, _rms(prev_ref[...], g))
        h_cur = _rms(cur_ref[...], g)
        for si in range(d // sw):
            hpad[si, 0:HALO, :] = h_prev[:, si * sw:(si + 1) * sw]
            hpad[si, HALO:HALO + ts, :] = h_cur[:, si * sw:(si + 1) * sw]
        pos = (i % tps) * ts + lax.broadcasted_iota(jnp.int32, (ts, 1), 0)
        for gi, win in enumerate(POOL_WINDOWS):
            sl = slice(gi * cg, (gi + 1) * cg)
            cnt = jnp.minimum(pos + 1, win).astype(F32)
            for si in range(gi * cg // sw, (gi + 1) * cg // sw):
                own = hpad[si, HALO:HALO + ts, :]
                acc = own
                for j in range(1, win):
                    acc = acc + hpad[si, pl.ds(HALO - j, ts), :]
                p_ref[:, si * sw:(si + 1) * sw] = (acc / cnt - own).astype(BF16)
            mixed = jnp.dot(p_ref[:, sl], w_ref[gi], preferred_element_type=F32)
            r_ref[:, sl] = cur_ref[:, sl] + s_ref[:, sl] * (mixed + b_ref[:, sl])
        h_ref[...] = _rms(r_ref[...], ng_ref[...]).astype(BF16)

    row = pl.BlockSpec((ts, d), lambda i: (i, 0))
    vec = pl.BlockSpec((1, d), lambda i: (0, 0))
    return pl.pallas_call(
        body, name=name, grid=(t // ts,),
        in_specs=[row, pl.BlockSpec((HALO, d), lambda i: (jnp.maximum(i * hb - 1, 0), 0)), vec,
                  pl.BlockSpec((N_GROUPS, cg, cg), lambda i: (0, 0, 0)), vec, vec, vec],
        out_specs=[row, row, row],
        out_shape=[jax.ShapeDtypeStruct((t, d), BF16), jax.ShapeDtypeStruct((t, d), F32),
                   jax.ShapeDtypeStruct((t, d), BF16)],
        scratch_shapes=[pltpu.VMEM((d // sw, HALO + ts, sw), F32)],
        compiler_params=_params("parallel"),
    )(x, x, gain, wp, scale, bias, next_gain)


def _pool_mix_bwd(name, pooled, wp, dr, x, gain, scale, bias, seq):
    t, d = x.shape
    ts = _tile(seq, 256)
    tps = seq // ts
    hb = ts // HALO
    nhb = t // HALO
    cg = d // N_GROUPS
    sw = min(cg, LANE)

    def body(p_ref, w_ref, dr_ref, ndr_ref, x_ref, g_ref, s_ref, b_ref, dx_ref, dw_ref, ds_ref, db_ref, dg_ref,
             qpad, dh):
        i = pl.program_id(0)
        last = (i % tps) == tps - 1
        pos = (i % tps) * ts + lax.broadcasted_iota(jnp.int32, (ts, 1), 0)

        @pl.when(i == 0)
        def _():
            dw_ref[...] = jnp.zeros_like(dw_ref)
            ds_ref[...] = jnp.zeros_like(ds_ref)
            db_ref[...] = jnp.zeros_like(db_ref)
            dg_ref[...] = jnp.zeros_like(dg_ref)

        for gi, win in enumerate(POOL_WINDOWS):
            sl = slice(gi * cg, (gi + 1) * cg)
            wv = w_ref[gi]
            sc = s_ref[:, sl]
            drv = dr_ref[:, sl]
            dmx = drv * sc
            dmx16 = dmx.astype(BF16)
            pooled = p_ref[:, sl]
            dw_ref[gi] += lax.dot_general(pooled, dmx16, (TN, ((), ())), preferred_element_type=F32)
            mixed = jnp.dot(pooled, wv, preferred_element_type=F32)
            ds_ref[:, sl] += jnp.sum(drv * (mixed + b_ref[:, sl]), axis=0, keepdims=True)
            db_ref[:, sl] += jnp.sum(dmx, axis=0, keepdims=True)
            cur = lax.dot_general(dmx16, wv, (NT, ((), ())), preferred_element_type=F32)
            nxt = lax.dot_general((ndr_ref[:, sl] * sc).astype(BF16), wv, (NT, ((), ())),
                                  preferred_element_type=F32)
            q_cur = cur / jnp.minimum(pos + 1, win).astype(F32)
            q_nxt = jnp.where(last, 0.0, nxt / float(win))
            for k, si in enumerate(range(gi * cg // sw, (gi + 1) * cg // sw)):
                part = slice(k * sw, (k + 1) * sw)
                qpad[si, 0:ts, :] = q_cur[:, part]
                qpad[si, ts:ts + HALO, :] = q_nxt[:, part]
                acc = -cur[:, part]
                for j in range(win):
                    acc = acc + qpad[si, pl.ds(j, ts), :]
                dh[:, si * sw:(si + 1) * sw] = acc
        dx, dgain_term, _ = _rms_bwd_tile(dh[...], x_ref[...], g_ref[...], dr_ref[...])
        dx_ref[...] = dx
        dg_ref[...] += jnp.sum(dgain_term, axis=0, keepdims=True)

    row = pl.BlockSpec((ts, d), lambda i: (i, 0))
    vec = pl.BlockSpec((1, d), lambda i: (0, 0))
    return pl.pallas_call(
        body, name=name, grid=(t // ts,),
        in_specs=[row, pl.BlockSpec((N_GROUPS, cg, cg), lambda i: (0, 0, 0)), row,
                  pl.BlockSpec((HALO, d), lambda i: (jnp.minimum((i + 1) * hb, nhb - 1), 0)), row, vec, vec, vec],
        out_specs=[row, pl.BlockSpec((N_GROUPS, cg, cg), lambda i: (0, 0, 0)), vec, vec, vec],
        out_shape=[jax.ShapeDtypeStruct((t, d), F32), jax.ShapeDtypeStruct((N_GROUPS, cg, cg), F32)]
        + [jax.ShapeDtypeStruct((1, d), F32)] * 3,
        scratch_shapes=[pltpu.VMEM((d // sw, ts + HALO, sw), F32), pltpu.VMEM((ts, d), F32)],
        compiler_params=_params("arbitrary"),
    )(pooled, wp, dr, dr, x, gain, scale, bias)


def _ctile(n, pref):
    return max(c for c in range(LANE, min(pref, n) + 1, LANE) if n % c == 0)


FFN_COLS = 1408
FFN_ROWS = 32
FFN_TILE = 1024


def _ffn_fwd(name, up, w, b, seq):
    _, t, dff = up.shape
    f = _ctile(dff, FFN_COLS)
    k_taps = w.shape[0]
    ts = _tile(seq, FFN_TILE)
    tps = seq // ts
    hb = ts // HALO16
    rc = _tile(ts, FFN_ROWS)

    def body(cur_ref, prev_ref, w_ref, b_ref, g_ref, apad):
        i = pl.program_id(1)
        first = (i % tps) == 0
        for ci, c0 in enumerate(range(0, f, LANE)):
            cols = slice(c0, c0 + LANE)
            apad[ci, 0:HALO16, :] = jnp.where(first, 0.0, prev_ref[:, cols].astype(F32))
            apad[ci, HALO16:HALO16 + ts, :] = cur_ref[0, :, cols].astype(F32)
            wv = w_ref[:, cols]
            wk = [jnp.broadcast_to(wv[k:k + 1, :], (rc, LANE)) for k in range(k_taps)]
            bias = jnp.broadcast_to(b_ref[:, cols], (rc, LANE))
            for r0 in range(0, ts, rc):
                c = bias
                for k in range(k_taps):
                    c = c + wk[k] * apad[ci, pl.ds(HALO16 - (k_taps - 1) + k + r0, rc), :]
                gate = cur_ref[1, pl.ds(r0, rc), cols].astype(F32)
                g_ref[pl.ds(r0, rc), cols] = (c * _sigmoid(c) * gate).astype(BF16)

    return pl.pallas_call(
        body, name=name, grid=(dff // f, t // ts),
        in_specs=[pl.BlockSpec((2, ts, f), lambda j, i: (0, i, j)),
                  pl.BlockSpec((None, HALO16, f), lambda j, i: (0, jnp.maximum(i * hb - 1, 0), j)),
                  pl.BlockSpec((k_taps, f), lambda j, i: (0, j)),
                  pl.BlockSpec((1, f), lambda j, i: (0, j))],
        out_specs=pl.BlockSpec((ts, f), lambda j, i: (i, j)),
        out_shape=jax.ShapeDtypeStruct((t, dff), BF16),
        scratch_shapes=[pltpu.VMEM((f // LANE, HALO16 + ts, LANE), F32)],
        compiler_params=_params("parallel", "parallel"),
    )(up, up, w, b)


def _ffn_bwd(name, up, dg, w, b, seq):
    _, t, dff = up.shape
    f = _ctile(dff, FFN_COLS)
    k_taps = w.shape[0]
    ts = _tile(seq, FFN_TILE)
    tps = seq // ts
    hb = ts // HALO16
    nhb = t // HALO16
    ext = ts + HALO16
    rc = _tile(ts, FFN_ROWS)

    def body(cur_ref, prev_ref, next_ref, dg_ref, ndg_ref, w_ref, b_ref, dup_ref, dw_ref, db_ref, apad, dcpad):
        i = pl.program_id(1)
        first = (i % tps) == 0
        last = (i % tps) == tps - 1

        @pl.when(i == 0)
        def _():
            dw_ref[...] = jnp.zeros_like(dw_ref)
            db_ref[...] = jnp.zeros_like(db_ref)

        for ci, c0 in enumerate(range(0, f, LANE)):
            cols = slice(c0, c0 + LANE)
            apad[ci, 0:HALO16, :] = jnp.where(first, 0.0, prev_ref[:, cols].astype(F32))
            apad[ci, HALO16:HALO16 + ts, :] = cur_ref[0, :, cols].astype(F32)
            apad[ci, HALO16 + ts:HALO16 + ext, :] = next_ref[0, :, cols].astype(F32)
            wv = w_ref[:, cols]
            wk = [jnp.broadcast_to(wv[k:k + 1, :], (rc, LANE)) for k in range(k_taps)]
            bias = jnp.broadcast_to(b_ref[:, cols], (rc, LANE))

            def conv_grad(r0, n, gate, dgv):
                c = bias[0:n]
                for k in range(k_taps):
                    c = c + wk[k][0:n] * apad[ci, pl.ds(HALO16 - (k_taps - 1) + k + r0, n), :]
                sig = _sigmoid(c)
                silu = c * sig
                return dgv * gate * (sig + silu * (1.0 - sig)), silu

            for r0 in range(0, ts, rc):
                dgv = dg_ref[pl.ds(r0, rc), cols].astype(F32)
                dc, silu = conv_grad(r0, rc, cur_ref[1, pl.ds(r0, rc), cols].astype(F32), dgv)
                dcpad[ci, pl.ds(r0, rc), :] = dc
                dup_ref[1, pl.ds(r0, rc), cols] = (dgv * silu).astype(BF16)
            dgv = jnp.where(last, 0.0, ndg_ref[:, cols].astype(F32))
            dc, _ = conv_grad(ts, HALO16, next_ref[1, :, cols].astype(F32), dgv)
            dcpad[ci, ts:ext, :] = dc

            dw_acc = [jnp.zeros((rc, LANE), F32) for _ in range(k_taps)]
            db_acc = jnp.zeros((rc, LANE), F32)
            for r0 in range(0, ts, rc):
                dact = jnp.zeros((rc, LANE), F32)
                for k in range(k_taps):
                    dact = dact + wk[k] * dcpad[ci, pl.ds(r0 + (k_taps - 1) - k, rc), :]
                dup_ref[0, pl.ds(r0, rc), cols] = dact.astype(BF16)
                dc = dcpad[ci, pl.ds(r0, rc), :]
                for k in range(k_taps):
                    dw_acc[k] = dw_acc[k] + dc * apad[ci, pl.ds(HALO16 - (k_taps - 1) + k + r0, rc), :]
                db_acc = db_acc + dc
            for k in range(k_taps):
                dw_ref[k:k + 1, cols] += jnp.sum(dw_acc[k], axis=0, keepdims=True)
            db_ref[:, cols] += jnp.sum(db_acc, axis=0, keepdims=True)

    return pl.pallas_call(
        body, name=name, grid=(dff // f, t // ts),
        in_specs=[pl.BlockSpec((2, ts, f), lambda j, i: (0, i, j)),
                  pl.BlockSpec((None, HALO16, f), lambda j, i: (0, jnp.maximum(i * hb - 1, 0), j)),
                  pl.BlockSpec((2, HALO16, f), lambda j, i: (0, jnp.minimum((i + 1) * hb, nhb - 1), j)),
                  pl.BlockSpec((ts, f), lambda j, i: (i, j)),
                  pl.BlockSpec((HALO16, f), lambda j, i: (jnp.minimum((i + 1) * hb, nhb - 1), j)),
                  pl.BlockSpec((k_taps, f), lambda j, i: (0, j)),
                  pl.BlockSpec((1, f), lambda j, i: (0, j))],
        out_specs=[pl.BlockSpec((2, ts, f), lambda j, i: (0, i, j)),
                   pl.BlockSpec((k_taps, f), lambda j, i: (0, j)),
                   pl.BlockSpec((1, f), lambda j, i: (0, j))],
        out_shape=[jax.ShapeDtypeStruct((2, t, dff), BF16), jax.ShapeDtypeStruct((k_taps, dff), F32),
                   jax.ShapeDtypeStruct((1, dff), F32)],
        scratch_shapes=[pltpu.VMEM((f // LANE, HALO16 + ext, LANE), F32), pltpu.VMEM((f // LANE, ext, LANE), F32)],
        compiler_params=_params("parallel", "arbitrary"),
    )(up, up, up, dg, dg, w, b)


def _sum_rows(name, g):
    ns, r, c = g.shape
    tr = _tile(r, 256)

    def body(g_ref, o_ref):
        acc = g_ref[0]
        for dev in range(1, ns):
            acc = acc + g_ref[dev]
        o_ref[...] = acc

    return pl.pallas_call(
        body, name=name, grid=(r // tr,),
        in_specs=[pl.BlockSpec((ns, tr, c), lambda i: (0, i, 0))],
        out_specs=pl.BlockSpec((tr, c), lambda i: (i, 0)),
        out_shape=jax.ShapeDtypeStruct((r, c), F32), compiler_params=_params("parallel"),
    )(g)


def _adamw(name, gsrc, w, m, v, layer=0, prev=None):
    ns, r, c = gsrc.shape
    nl = w.shape[0]
    tr = _tile(r, 256)
    prev = () if prev is None else tuple(prev)

    def body(g_ref, w_ref, m_ref, v_ref, *rest):
        go_ref, do_ref, mo_ref, vo_ref = rest[len(prev):]
        g = g_ref[0].astype(F32)
        for dev in range(1, ns):
            g = g + g_ref[dev].astype(F32)
        m_new = ADAM_B1 * m_ref[...] + (1.0 - ADAM_B1) * g
        v_new = ADAM_B2 * v_ref[...] + (1.0 - ADAM_B2) * (g * g)
        m_hat = m_new / (1.0 - ADAM_B1 ** ADAM_STEP)
        v_hat = v_new / (1.0 - ADAM_B2 ** ADAM_STEP)
        go_ref[...] = g
        do_ref[...] = -ADAM_LR * (m_hat / (jnp.sqrt(v_hat) + ADAM_EPS) + ADAM_WD * w_ref[...])
        mo_ref[...] = m_new
        vo_ref[...] = v_new

    row = pl.BlockSpec((None, tr, c), lambda i: (layer, i, 0))
    return pl.pallas_call(
        body, name=name, grid=(r // tr,),
        in_specs=[pl.BlockSpec((ns, tr, c), lambda i: (0, i, 0)), row, row, row] + [ANY] * len(prev),
        out_specs=[row] * 4, out_shape=[jax.ShapeDtypeStruct((nl, r, c), F32)] * 4,
        input_output_aliases={4 + i: i for i in range(len(prev))},
        compiler_params=_params("parallel"),
    )(gsrc, w, m, v, *prev)


def _ffn_forward(tag, r_in, h, get_wu, get_wd, wdw, bdw, seq, loss=None):
    t, d = r_in.shape
    tm = _tile(t, 512 if loss is not None else 1024)
    wu = get_wu(h)
    dff = wu.shape[0] // 2
    tu = _tile(t, 1024)
    up = _mm(f"{tag}_up", h, wu, grid=(2, t // tu, 1),
             a_spec=pl.BlockSpec((tu, d), lambda j, i, k: (i, 0)),
             b_spec=pl.BlockSpec((dff, d), lambda j, i, k: (j, 0)),
             out_spec=pl.BlockSpec((None, tu, dff), lambda j, i, k: (j, i, 0)),
             out_shape=jax.ShapeDtypeStruct((2, t, dff), BF16), dims=NT, acc_shape=(tu, dff))
    wd = get_wd(up)
    g = _ffn_fwd(f"{tag}_act", up, wdw, bdw, seq)
    row = pl.BlockSpec((tm, d), lambda i, j, k: (i, 0))
    vec = pl.BlockSpec((1, d), lambda i, j, k: (0, 0))
    common = dict(grid=(t // tm, 1, 1), a_spec=pl.BlockSpec((tm, dff), lambda i, j, k: (i, 0)),
                  b_spec=pl.BlockSpec((dff, d), lambda i, j, k: (0, 0), pipeline_mode=pl.Buffered(1)),
                  dims=NN, acc_shape=(tm, d))
    if loss is None:
        out = _mm(f"{tag}_down", g, wd, out_spec=row, out_shape=jax.ShapeDtypeStruct((t, d), F32),
                  extras=(r_in,), extra_specs=(row,), epilogue=lambda acc, ex, rows: ((ex[0][rows, :] + acc,), ()),
                  **common)
    else:
        def head(acc, ex, rows):
            dx, part, dgain = _loss_tile(ex[0][rows, :] + acc, ex[1][rows, :], ex[2][...])
            return (dx,), (part, dgain)

        out = _mm(f"{tag}_down", g, wd, out_spec=[row, pl.BlockSpec((1, 1), lambda i, j, k: (0, 0)), vec],
                  out_shape=[jax.ShapeDtypeStruct((t, d), F32), jax.ShapeDtypeStruct((1, 1), F32),
                             jax.ShapeDtypeStruct((1, d), F32)],
                  extras=(r_in, *loss), extra_specs=(row, row, vec), epilogue=head, n_sums=2, **common)
    return out, (r_in, h, up, g, wu, wd)


def _ffn_backward(tag, dr, saved, gain, wdw, bdw, seq, token=None):
    r_in, h, up, g, wu, wd = saved
    t, d = r_in.shape
    dff = wd.shape[0]
    tm = _tile(t, 512)
    tk = _tile(t, 2048)
    tku = _tile(t, 4096)
    cw = _ctile(dff, 1408)
    nc = dff // cw
    once = dict(pipeline_mode=pl.Buffered(1)) if tku == t else {}
    tg = _tile(t, 1024)
    dg = _mm(f"{tag}_dg", dr, wd, grid=(t // tg, 1, 1),
             a_spec=pl.BlockSpec((tg, d), lambda i, j, k: (i, 0)),
             b_spec=pl.BlockSpec((dff, d), lambda i, j, k: (0, 0), pipeline_mode=pl.Buffered(1)),
             out_spec=pl.BlockSpec((tg, dff), lambda i, j, k: (i, 0)),
             out_shape=jax.ShapeDtypeStruct((t, dff), BF16), dims=NT, acc_shape=(tg, dff), token=token)
    dwd = _mm(f"{tag}_dwd", g, dr, grid=(dff // cw, 1, t // tk),
              a_spec=pl.BlockSpec((tk, cw), lambda i, j, k: (k, i)),
              b_spec=pl.BlockSpec((tk, d), lambda i, j, k: (k, 0)),
              out_spec=pl.BlockSpec((cw, d), lambda i, j, k: (i, 0)),
              out_shape=jax.ShapeDtypeStruct((dff, d), BF16), dims=TN, acc_shape=(cw, d))
    dup, dwdw, dbdw = _ffn_bwd(f"{tag}_dact", up, dg, wdw, bdw, seq)
    row = pl.BlockSpec((tm, d), lambda i, j, k: (i, 0))
    vec = pl.BlockSpec((1, d), lambda i, j, k: (0, 0))

    def norm_backward(acc, ex, rows):
        dx, dgain, colsum = _rms_bwd_tile(acc, ex[0][rows, :], ex[1][...], ex[2][rows, :])
        return (dx,), (dgain, colsum)

    dr_in, dgain, colsum = _mm(
        f"{tag}_dh", dup, wu, grid=(t // tm, 1, 1),
        a_spec=pl.BlockSpec((2, tm, dff), lambda i, j, k: (0, i, 0)),
        b_spec=pl.BlockSpec((2 * dff, d), lambda i, j, k: (0, 0), pipeline_mode=pl.Buffered(1)),
        out_spec=[row, vec, vec],
        out_shape=[jax.ShapeDtypeStruct((t, d), F32)] + [jax.ShapeDtypeStruct((1, d), F32)] * 2,
        dims=NN, acc_shape=(tm, d), extras=(r_in, gain, dr), extra_specs=(row, vec, row),
        epilogue=norm_backward, n_sums=2, parts=2)
    dwu = _mm(f"{tag}_dwu", dup, h, grid=(2 * nc, 1, t // tku),
              a_spec=pl.BlockSpec((None, tku, cw), lambda i, j, k: (i // nc, k, i % nc)),
              b_spec=pl.BlockSpec((tku, d), lambda i, j, k: (k, 0), **once),
              out_spec=pl.BlockSpec((cw, d), lambda i, j, k: (i, 0)),
              out_shape=jax.ShapeDtypeStruct((2 * dff, d), BF16), dims=TN, acc_shape=(cw, d))
    return dr_in, dgain, dwu, dwd, dwdw, dbdw, colsum


def _pad_to(vec, n):
    return jnp.pad(vec, (0, n - vec.shape[0]))


def _pack(parts, width):
    flat = jnp.concatenate([p.reshape(-1).astype(F32) for p in parts])
    n = -(-flat.shape[0] // (8 * width)) * (8 * width)
    return _pad_to(flat, n).reshape(n // width, width)


def _unpack(mat, shapes):
    flat = mat.reshape(-1)
    out, off = [], 0
    for s in shapes:
        n = 1
        for dim in s:
            n *= dim
        out.append(flat[off:off + n].reshape(s))
        off += n
    return out


def kernel(x, norm_mix, norm_ffn, conv_w_pw1, conv_b_pw1, conv_w_dw, conv_b_dw, conv_ln_g, conv_ln_b, conv_w_pw2, conv_b_pw2, pool_w, pool_b, pool_scale, ffn_w_up, ffn_w_dw, ffn_b_dw, ffn_w_down, final_norm, loss_target, m_norm_mix, m_norm_ffn, m_conv_w_pw1, m_conv_b_pw1, m_conv_w_dw, m_conv_b_dw, m_conv_ln_g, m_conv_ln_b, m_conv_w_pw2, m_conv_b_pw2, m_pool_w, m_pool_b, m_pool_scale, m_ffn_w_up, m_ffn_w_dw, m_ffn_b_dw, m_ffn_w_down, m_final_norm, v_norm_mix, v_norm_ffn, v_conv_w_pw1, v_conv_b_pw1, v_conv_w_dw, v_conv_b_dw, v_conv_ln_g, v_conv_ln_b, v_conv_w_pw2, v_conv_b_pw2, v_pool_w, v_pool_b, v_pool_scale, v_ffn_w_up, v_ffn_w_dw, v_ffn_b_dw, v_ffn_w_down, v_final_norm):
    bsz, seq, d = x.shape
    t = bsz * seq
    k_taps = conv_w_dw.shape[1]
    cs1 = conv_w_pw1.shape[2]
    dsh = d // N_DEV
    cg = d // N_GROUPS
    cgs = pool_w.shape[2]
    fu = ffn_w_up.shape[2]
    fd = ffn_w_down.shape[1]
    dff = fd * N_DEV
    nb = N_DEV // 2
    kf = ffn_w_dw.shape[1]
    fsh = ffn_w_dw.shape[2]
    my = _lin(_me())
    tm = _tile(t, 512)

    x2 = x.reshape(t, d)
    tgt2 = loss_target.reshape(t, d)

    small_shapes = [(k_taps, dsh), (dsh,), (dsh,), (2, kf, fsh)]
    small_mine = _pack([conv_w_dw[0], pool_b[0], pool_scale[0], ffn_w_dw], LANE)
    big = [conv_w_pw1[0], conv_w_pw2[0], ffn_w_up[0].T, ffn_w_down[0], pool_w[0], ffn_w_up[1].T, ffn_w_down[1]]
    gather = _exchange_start("gather_start", [small_mine] + [w.astype(BF16) for w in big], [GATHER2] * 8)
    h0 = _rms_fwd("l0_rms", x2, norm_mix[0:1])
    forwarded = _exchange_forward("gather_forward_w1", gather, [0, 1], h0)
    small_all, w1 = _exchange_wait("gather_wait_w1", gather, [0, 1], forwarded)
    parts = [_unpack(small_all[dev], small_shapes) for dev in range(N_DEV)]
    wdw = jnp.concatenate([p[0] for p in parts], axis=1)
    pool_b_full = jnp.concatenate([p[1] for p in parts]).reshape(1, d)
    pool_s_full = jnp.concatenate([p[2] for p in parts]).reshape(1, d)
    fwdw = jnp.concatenate([p[3] for p in parts], axis=2)
    fbdw = ffn_b_dw.reshape(2, 1, dff)

    def columns(w):
        return w.transpose(1, 0, 2).reshape(w.shape[1], N_DEV * w.shape[2])

    def column_shards(w):
        return w.reshape(w.shape[0], N_DEV, w.shape[1] // N_DEV).transpose(1, 0, 2)

    w1 = columns(w1)
    a = _mm("l0_pw1", h0, w1, grid=(2, t // tm, 1),
            a_spec=pl.BlockSpec((tm, d), lambda j, i, k: (i, 0)),
            b_spec=pl.BlockSpec((d, d), lambda j, i, k: (0, j)),
            out_spec=pl.BlockSpec((None, tm, d), lambda j, i, k: (j, i, 0)),
            out_shape=jax.ShapeDtypeStruct((2, t, d), BF16), dims=NN, acc_shape=(tm, d),
            extras=(conv_b_pw1,), extra_specs=(pl.BlockSpec((1, d), lambda j, i, k: (0, j)),),
            epilogue=lambda acc, ex, rows: ((acc + ex[0][...],), ()))
    v = _conv_fwd("l0_conv", a, wdw, conv_b_dw, seq)
    forwarded = _exchange_forward("gather_forward_wu0", gather, [2, 3], v)
    (w2,) = _exchange_wait("gather_wait_w2", gather, [2], forwarded)
    w2 = w2.reshape(d, d)
    row = pl.BlockSpec((tm, d), lambda i, j, k: (i, 0))
    vec = pl.BlockSpec((1, d), lambda i, j, k: (0, 0))
    square = pl.BlockSpec((d, d), lambda i, j, k: (0, 0))

    def ln_silu(v_blk, ex):
        s_blk = _ln_silu_tile(v_blk, ex[0][...], ex[1][...]).astype(BF16)
        return s_blk, s_blk

    def residual_and_norm(acc, ex, rows):
        r_blk = ex[3][rows, :] + (acc + ex[2][...])
        return (r_blk, _rms(r_blk, ex[4][...])), ()

    r1, h1, s = _mm("l0_pw2", v, w2, grid=(t // tm, 1, 1), a_spec=row, b_spec=square, out_spec=[row, row, row],
                    out_shape=[jax.ShapeDtypeStruct((t, d), F32), jax.ShapeDtypeStruct((t, d), BF16),
                               jax.ShapeDtypeStruct((t, d), BF16)],
                    dims=NN, acc_shape=(tm, d), extras=(conv_ln_g, conv_ln_b, conv_b_pw2, x2, norm_ffn[0:1]),
                    extra_specs=(vec, vec, vec, row, vec), prologue=ln_silu, epilogue=residual_and_norm)

    def up_getter(name, idx):
        return lambda after: _exchange_wait(name, gather, [idx], after)[0].reshape(2 * dff, d)

    def down_getter(name, idx, forward=None):
        def get(after):
            if forward is not None:
                after = _exchange_forward(forward[0], gather, forward[1], after)
            return _exchange_wait(name, gather, [idx], after)[0].reshape(dff, d)
        return get

    r2, ffn0_saved = _ffn_forward("f0", r1, h1, up_getter("gather_wait_wu0", 3),
                                  down_getter("gather_wait_wd0", 4, ("gather_forward_wu1", [4, 5, 6])),
                                  fwdw[0], fbdw[0], seq)
    forwarded = _exchange_forward("gather_forward_wd1", gather, [7], r2)
    (wp,) = _exchange_wait("gather_wait_wp", gather, [5], forwarded)
    wp = wp.transpose(1, 0, 2, 3).reshape(N_GROUPS, cg, cg)
    pooled, r3, h3 = _pool_mix_fwd("l1_mix", r2, norm_mix[1:2], wp, pool_s_full, pool_b_full, norm_ffn[1:2], seq)
    (dr4, loss_part, dfinal), ffn1_saved = _ffn_forward(
        "f1", r3, h3, up_getter("gather_wait_wu1", 6), down_getter("gather_wait_wd1", 7), fwdw[1], fbdw[1], seq,
        loss=(tgt2, final_norm.reshape(1, d)))

    dr3, dnf1, dwu1, dwd1, dfw1, dfb1, _ = _ffn_backward("f1", dr4, ffn1_saved, norm_ffn[1:2], fwdw[1], fbdw[1], seq)
    scatter_a = _exchange_start("scatter_f1_start", [dwu1.reshape(N_DEV, fu, d), dwd1.reshape(N_DEV, fd, d)],
                                [SCATTER, SCATTER])
    dr2, dwp, dpool_s, dpool_b, dnm1 = _pool_mix_bwd(
        "l1_dmix", pooled, wp, dr3, r2, norm_mix[1:2], pool_s_full + scatter_a["token"][0:1, 0:1], pool_b_full, seq)
    dr1, dnf0, dwu0, dwd0, dfw0, dfb0, db2 = _ffn_backward("f0", dr2, ffn0_saved, norm_ffn[0:1], fwdw[0], fbdw[0], seq)
    dwp_b = dwp.astype(BF16).reshape(N_GROUPS, N_DEV, cgs, cg).transpose(1, 0, 2, 3)
    tk = _tile(t, 2048)
    dw2 = _mm("l0_dw2", s, dr1, grid=(1, 1, t // tk),
              a_spec=pl.BlockSpec((tk, d), lambda i, j, k: (k, 0)),
              b_spec=pl.BlockSpec((tk, d), lambda i, j, k: (k, 0)),
              out_spec=pl.BlockSpec((d, d), lambda i, j, k: (0, 0)),
              out_shape=jax.ShapeDtypeStruct((d, d), BF16), dims=TN, acc_shape=(d, d))
    scatter_b = _exchange_start("scatter_f0_start", [dwu0.reshape(N_DEV, fu, d), dwd0.reshape(N_DEV, fd, d), dwp_b,
                                                     dw2.reshape(N_DEV, d // N_DEV, d)], [SCATTER] * 4)

    def ln_silu_backward(acc, ex, rows):
        dv_blk, dgain, dbias, colsum = _ln_silu_bwd_tile(acc, ex[0][rows, :], ex[1][...], ex[2][...])
        return (dv_blk,), (dgain, dbias, colsum)

    dv, dlg, dlb, dbdw = _mm("l0_ds", dr1, w2, grid=(t // tm, 1, 1), a_spec=row, b_spec=square,
                             out_spec=[row, vec, vec, vec],
                             out_shape=[jax.ShapeDtypeStruct((t, d), F32)] + [jax.ShapeDtypeStruct((1, d), F32)] * 3,
                             dims=NT, acc_shape=(tm, d), extras=(v, conv_ln_g, conv_ln_b), extra_specs=(row, vec, vec),
                             epilogue=ln_silu_backward, n_sums=3, token=scatter_b["token"])
    da, dwdw, db1 = _conv_bwd("l0_dconv", a, dv, wdw, seq)
    tk1 = _tile(t, 4096)
    once = dict(pipeline_mode=pl.Buffered(1)) if tk1 == t else {}
    dw1 = _mm("l0_dw1", h0, da, grid=(1, 2, t // tk1),
              a_spec=pl.BlockSpec((tk1, d), lambda i, j, k: (k, 0), **once),
              b_spec=pl.BlockSpec((None, tk1, d), lambda i, j, k: (j, k, 0)),
              out_spec=pl.BlockSpec((d, d), lambda i, j, k: (0, j)),
              out_shape=jax.ShapeDtypeStruct((d, 2 * d), BF16), dims=TN, acc_shape=(d, d))
    scatter_c = _exchange_start("scatter_l0_start", [column_shards(dw1)], [SCATTER])
    def norm_backward(acc, ex, rows):
        dx_blk, dgain, colsum = _rms_bwd_tile(acc, ex[0][rows, :], ex[1][...], ex[2][rows, :])
        return (dx_blk,), (dgain, colsum)

    dx, dnm0, _ = _mm("l0_dh", da, w1, grid=(t // tm, 1, 1),
                      a_spec=pl.BlockSpec((2, tm, d), lambda i, j, k: (0, i, 0)),
                      b_spec=pl.BlockSpec((d, 2 * d), lambda i, j, k: (0, 0), pipeline_mode=pl.Buffered(1)),
                      out_spec=[row, vec, vec],
                      out_shape=[jax.ShapeDtypeStruct((t, d), F32)] + [jax.ShapeDtypeStruct((1, d), F32)] * 2,
                      dims=NT, acc_shape=(tm, d), extras=(x2, norm_mix[0:1], dr1), extra_specs=(row, vec, row),
                      epilogue=norm_backward, n_sums=2, parts=2, token=scatter_c["token"])

    dffn_w = jnp.stack([dfw0, dfw1])
    dffn_b = jnp.stack([dfb0, dfb1]).reshape(2, dff)
    small_parts = [loss_part, jnp.concatenate([dnm0, dnm1]), jnp.concatenate([dnf0, dnf1]), db1, dwdw, dbdw, dlg, dlb,
                   db2, dpool_b, dpool_s, dffn_w, dffn_b, dfinal]
    small_part_shapes = [(1,), (2, d), (2, d), (1, 2 * d), (k_taps, d), (1, d), (1, d), (1, d), (1, d), (1, d), (1, d),
                         (2, kf, dff), (2, dff), (d,)]
    packed = _pack(small_parts, 8 * LANE)
    gather_small = _exchange_start("gather_small_start", [packed], [GATHER])

    def big_update(name, recv, w, m, v, layer=0, prev=None):
        shape = w.shape
        c = recv.shape[-1]
        rows = recv.size // (N_DEV * c)
        nl = w.size // (rows * c)
        outs = _adamw(name, recv.reshape(N_DEV, rows, c), w.reshape(nl, rows, c), m.reshape(nl, rows, c),
                      v.reshape(nl, rows, c), layer, prev)
        return outs, [o.reshape(shape) for o in outs]

    wu_t = [p.transpose(0, 2, 1) for p in (ffn_w_up, m_ffn_w_up, v_ffn_w_up)]
    g_wu1, g_wd1 = _exchange_wait("scatter_f1_wait", scatter_a, [0, 1], gather_small["token"])
    raw_wu, _ = big_update("adam_wu1", g_wu1, *wu_t, 1)
    raw_wd, _ = big_update("adam_wd1", g_wd1, ffn_w_down, m_ffn_w_down, v_ffn_w_down, 1)
    g_wu0, g_wd0, g_wp, g_w2 = _exchange_wait("scatter_f0_wait", scatter_b, [0, 1, 2, 3], raw_wd[0])
    _, u_wu = big_update("adam_wu0", g_wu0, *wu_t, 0, raw_wu)
    u_wu = [o.transpose(0, 2, 1) for o in u_wu]
    _, u_wd = big_update("adam_wd0", g_wd0, ffn_w_down, m_ffn_w_down, v_ffn_w_down, 0, raw_wd)
    _, u_wp = big_update("adam_wp", g_wp, pool_w, m_pool_w, v_pool_w)
    (g_w1,) = _exchange_wait("scatter_l0_wait", scatter_c, [0], u_wp[0])
    _, u_w1 = big_update("adam_w1", g_w1, conv_w_pw1, m_conv_w_pw1, v_conv_w_pw1)
    _, u_w2 = big_update("adam_w2", g_w2, conv_w_pw2, m_conv_w_pw2, v_conv_w_pw2)
    (all_small,) = _exchange_wait("gather_small_wait", gather_small, [0], u_w2[0])
    summed = _sum_rows("sum_small_grads", all_small)
    (loss_v, g_nm, g_nf, g_b1, g_wdw, g_bdw, g_lg, g_lb, g_b2, g_pb, g_ps, g_fw, g_fb,
     g_fin) = _unpack(summed, small_part_shapes)
    loss = loss_v[0]
    g_wdw_mine = lax.dynamic_slice_in_dim(g_wdw, my * dsh, dsh, axis=1)[None]
    g_pb_mine = lax.dynamic_slice_in_dim(g_pb, my * dsh, dsh, axis=1)
    g_ps_mine = lax.dynamic_slice_in_dim(g_ps, my * dsh, dsh, axis=1)
    g_fw_mine = lax.dynamic_slice_in_dim(g_fw, my * fsh, fsh, axis=2)

    small_g =[g_nm, g_nf, g_b1, g_wdw_mine, g_bdw, g_lg, g_lb, g_b2, g_pb_mine, g_ps_mine, g_fw_mine, g_fb, g_fin]
    small_w = [norm_mix, norm_ffn, conv_b_pw1, conv_w_dw, conv_b_dw, conv_ln_g, conv_ln_b, conv_b_pw2, pool_b,
               pool_scale, ffn_w_dw, ffn_b_dw, final_norm]
    small_m = [m_norm_mix, m_norm_ffn, m_conv_b_pw1, m_conv_w_dw, m_conv_b_dw, m_conv_ln_g, m_conv_ln_b,
               m_conv_b_pw2, m_pool_b, m_pool_scale, m_ffn_w_dw, m_ffn_b_dw, m_final_norm]
    small_v = [v_norm_mix, v_norm_ffn, v_conv_b_pw1, v_conv_w_dw, v_conv_b_dw, v_conv_ln_g, v_conv_ln_b,
               v_conv_b_pw2, v_pool_b, v_pool_scale, v_ffn_w_dw, v_ffn_b_dw, v_final_norm]
    shapes = [w.shape for w in small_w]
    outs = _adamw("adam_small", _pack(small_g, 8 * LANE)[None], _pack(small_w, 8 * LANE)[None],
                  _pack(small_m, 8 * LANE)[None], _pack(small_v, 8 * LANE)[None])
    sg, sd, sm, sv = [_unpack(o, shapes) for o in outs]

    def leaf(kind):
        (nm, nf, b1, wdw_, bdw_, lg, lb, b2, pb, ps, fw, fb, fin) = (sg, sd, sm, sv)[kind]
        return [nm, nf, u_w1[kind], b1, wdw_, bdw_, lg, lb, u_w2[kind], b2, u_wp[kind], pb, ps, u_wu[kind], fw, fb,
                u_wd[kind], fin]

    return (loss, dx.reshape(bsz, seq, d), *leaf(0), *leaf(1), *leaf(2), *leaf(3))
```

```python
import functools

import jax
import jax.numpy as jnp
from jax import lax
from jax.experimental import pallas as pl
from jax.experimental.pallas import tpu as pltpu

F32 = jnp.float32
BF16 = jnp.bfloat16
MESH = pl.DeviceIdType.MESH
HBM = pl.BlockSpec(memory_space=pltpu.HBM)

N_DEV = 8
RMS_EPS = 1e-6
LN_EPS = 1e-5
POOL_WINDOWS = (2, 4, 8, 16)
N_GROUPS = len(POOL_WINDOWS)
ADAM_LR = 0.001
ADAM_B1 = 0.9
ADAM_B2 = 0.999
ADAM_EPS = 1e-08
ADAM_WD = 0.01
ADAM_STEP = 10

LANE = 128
HALO = 32
HALO16 = 16
VMEM_LIMIT = 56 * 1024 * 1024


def _params(*sem):
    return pltpu.CompilerParams(dimension_semantics=sem if sem else None, vmem_limit_bytes=VMEM_LIMIT)


def _tile(n, pref):
    for t in range(min(pref, n), 15, -1):
        if n % t == 0 and t % 16 == 0:
            return t
    return n


def _sigmoid(z):
    return 1.0 / (1.0 + jnp.exp(-z))


def _me():
    return lax.axis_index("x"), lax.axis_index("y"), lax.axis_index("c")


def _flip(pos, m):
    x, y, c = pos
    return ((1 - x) if m & 4 else x, (1 - y) if m & 2 else y, (1 - c) if m & 1 else c)


def _lin(pos):
    return 4 * pos[0] + 2 * pos[1] + pos[2]


SEM = pl.BlockSpec(memory_space=pltpu.SEMAPHORE)
ANY = pl.BlockSpec(memory_space=pl.ANY)
EFFECT = pltpu.SideEffectType.DATAFLOW_SIDE_EFFECTING


SCATTER = "scatter"
GATHER = "gather"
GATHER2 = "gather2"
ALL_MASKS = (1, 2, 3, 4, 5, 6, 7)
SIBLING = 1
CHIPS = (2, 4, 6)


class _Copies:
    def __init__(self, a, mode, src, land, send_sems, recv_sems):
        self.a, self.mode, self.src, self.land = a, mode, src, land
        self.send_sems, self.recv_sems = send_sems, recv_sems
        self.me = _me()
        self.first = (SIBLING,) + CHIPS if mode == GATHER2 else ALL_MASKS

    def _sems(self, m, to):
        return dict(send_sem=self.send_sems.at[self.a * N_DEV + m], recv_sem=self.recv_sems.at[self.a * N_DEV + m],
                    device_id=to, device_id_type=MESH)

    def _block(self, pid):
        return self.src.at[pid] if self.mode == SCATTER else self.src

    def local(self):
        my = _lin(self.me)
        return pltpu.make_async_copy(self._block(my), self.land.at[my], self.send_sems.at[self.a * N_DEV])

    def send(self, m):
        peer = _flip(self.me, m)
        return pltpu.make_async_remote_copy(src_ref=self._block(_lin(peer)), dst_ref=self.land.at[_lin(self.me)],
                                            **self._sems(m, peer))

    def arrival(self, m):
        rows = self.land.at[_lin(_flip(self.me, m))]
        return pltpu.make_async_remote_copy(src_ref=rows, dst_ref=rows, **self._sems(m, _flip(self.me, m)))

    def forward(self, m):
        rows = self.land.at[_lin(_flip(self.me, m))]
        return pltpu.make_async_remote_copy(src_ref=rows, dst_ref=rows, **self._sems(m | 1, _flip(self.me, SIBLING)))


def _exchange_start(name, arrs, modes):
    n = len(arrs)
    blocks = [a.shape[1:] if md == SCATTER else a.shape for a, md in zip(arrs, modes)]

    def body(*refs):
        srcs, lands = refs[:n], refs[n:2 * n]
        send_sems, recv_sems = refs[2 * n], refs[2 * n + 1]
        token = refs[-1]
        for a in range(n):
            cp = _Copies(a, modes[a], srcs[a], lands[a], send_sems, recv_sems)
            cp.local().start()
            for m in cp.first:
                cp.send(m).start()
        token[...] = jnp.zeros_like(token)

    lands = [lax.empty((N_DEV,) + tuple(b), a.dtype) for a, b in zip(arrs, blocks)]
    outs = pl.pallas_call(
        body, name=name,
        out_shape=(pltpu.SemaphoreType.DMA((n * N_DEV,)), pltpu.SemaphoreType.DMA((n * N_DEV,)),
                   *[pltpu.HBM(a.shape, a.dtype) for a in arrs], *[pltpu.HBM(l.shape, l.dtype) for l in lands],
                   jax.ShapeDtypeStruct((8, LANE), F32)),
        in_specs=[HBM] * (2 * n),
        out_specs=(SEM, SEM, *[HBM] * (2 * n), pl.BlockSpec(memory_space=pltpu.VMEM)),
        input_output_aliases={i: 2 + i for i in range(2 * n)},
        compiler_params=pltpu.CompilerParams(has_side_effects=EFFECT),
    )(*[pltpu.with_memory_space_constraint(a, pltpu.HBM) for a in arrs],
      *[pltpu.with_memory_space_constraint(l, pltpu.HBM) for l in lands])
    return dict(send=outs[0], recv=outs[1], srcs=list(outs[2:2 + n]), lands=list(outs[2 + n:2 + 2 * n]),
                modes=modes, token=outs[-1])


def _exchange_forward(name, handle, which, after):
    k = len(which)

    def half(wait):
        def body(*refs):
            lands = refs[:k]
            send_sems, recv_sems = refs[k], refs[k + 1]
            token = refs[-1]
            for pos, a in enumerate(which):
                cp = _Copies(a, GATHER2, None, lands[pos], send_sems, recv_sems)
                for m in CHIPS:
                    if wait:
                        cp.arrival(m).wait_recv()
                    else:
                        cp.forward(m).start()
            token[...] = jnp.zeros_like(token)
        return body

    def call(body, call_name, lands, after):
        after = tuple(after) if isinstance(after, (tuple, list)) else (after,)
        outs = pl.pallas_call(
            body, name=call_name,
            out_shape=(*[pltpu.HBM(x.shape, x.dtype) for x in lands], jax.ShapeDtypeStruct((8, LANE), F32)),
            in_specs=[HBM] * k + [SEM, SEM] + [ANY] * len(after),
            out_specs=(*[HBM] * k, pl.BlockSpec(memory_space=pltpu.VMEM)),
            input_output_aliases={i: i for i in range(k)},
            compiler_params=pltpu.CompilerParams(has_side_effects=EFFECT),
        )(*lands, handle["send"], handle["recv"], *after)
        return list(outs[:k]), outs[-1]

    lands, arrived = call(half(True), name + "_arrived", [handle["lands"][a] for a in which], after)
    lands, token = call(half(False), name, lands, arrived)
    for pos, a in enumerate(which):
        handle["lands"][a] = lands[pos]
    return token


def _exchange_wait(name, handle, which, after):
    k = len(which)
    modes = handle["modes"]

    def body(*refs):
        srcs, lands = refs[:k], refs[k:2 * k]
        send_sems, recv_sems = refs[2 * k], refs[2 * k + 1]
        for pos, a in enumerate(which):
            cp = _Copies(a, modes[a], srcs[pos], lands[pos], send_sems, recv_sems)
            cp.local().wait()
            for m in cp.first:
                cp.send(m).wait_send()
            if modes[a] == GATHER2:
                for m in CHIPS:
                    cp.forward(m).wait_send()
                arrivals = (SIBLING,) + tuple(m | 1 for m in CHIPS)
            else:
                arrivals = ALL_MASKS
            for m in arrivals:
                cp.arrival(m).wait_recv()

    srcs = [handle["srcs"][a] for a in which]
    lands = [handle["lands"][a] for a in which]
    outs = pl.pallas_call(
        body, name=name,
        out_shape=tuple(pltpu.HBM(x.shape, x.dtype) for x in srcs + lands),
        in_specs=[HBM] * (2 * k) + [SEM, SEM, ANY], out_specs=tuple([HBM] * (2 * k)),
        input_output_aliases={i: i for i in range(2 * k)},
        compiler_params=pltpu.CompilerParams(has_side_effects=EFFECT),
    )(*srcs, *lands, handle["send"], handle["recv"], after)
    for pos, a in enumerate(which):
        handle["srcs"][a], handle["lands"][a] = outs[pos], outs[k + pos]
    return list(outs[k:])


def _mm(name, a, b, *, grid, a_spec, b_spec, out_spec, out_shape, dims, acc_shape, extras=(), extra_specs=(),
        epilogue=None, token=None, prologue=None, n_sums=0, parts=1):
    nk = grid[2]
    ne = len(extras)
    deps = () if token is None else (token,)
    dep_specs = [pl.BlockSpec((8, LANE), lambda i, j, k: (0, 0))] * len(deps)
    n_out = len(out_shape) if isinstance(out_shape, (list, tuple)) else 1
    n_tiles = n_out - n_sums - (1 if prologue is not None else 0)

    def body(a_ref, b_ref, *rest):
        ex, o_refs, acc_ref = rest[:ne], rest[ne + len(deps):ne + len(deps) + n_out], rest[ne + len(deps) + n_out]
        k = pl.program_id(2)
        if parts == 1:
            a_blk, saved = a_ref[...], None
            if prologue is not None:
                a_blk, saved = prologue(a_blk, ex)
                o_refs[n_tiles][...] = saved
            part = lax.dot_general(a_blk.astype(BF16), b_ref[...].astype(BF16), (dims, ((), ())),
                                   preferred_element_type=F32)
        else:
            kb = b_ref.shape[dims[1][0]] // parts
            part = None
            for p in range(parts):
                b_blk = b_ref[p * kb:(p + 1) * kb, :] if dims[1][0] == 0 else b_ref[:, p * kb:(p + 1) * kb]
                term = lax.dot_general(a_ref[p].astype(BF16), b_blk.astype(BF16), (dims, ((), ())),
                                       preferred_element_type=F32)
                part = term if part is None else part + term
        sum_refs = o_refs[n_out - n_sums:]

        def add_sums(terms):
            @pl.when((pl.program_id(0) == 0) & (pl.program_id(1) == 0))
            def _():
                for o_ref in sum_refs:
                    o_ref[...] = jnp.zeros_like(o_ref)

            for o_ref, term in zip(sum_refs, terms):
                o_ref[...] += jnp.sum(term, axis=0, keepdims=True)

        def finish(r):
            tiles, terms = ((r,), ()) if epilogue is None else epilogue(r, ex, slice(None))
            for o_ref, val in zip(o_refs, tiles):
                o_ref[...] = val.astype(o_ref.dtype)
            if n_sums:
                add_sums(terms)

        if nk == 1:
            finish(part)
            return

        @pl.when(k == 0)
        def _():
            acc_ref[...] = part

        @pl.when((k > 0) & (k < nk - 1))
        def _():
            acc_ref[...] += part

        @pl.when(k == nk - 1)
        def _():
            finish(acc_ref[...] + part)

    return pl.pallas_call(
        body, name=name, grid=grid, in_specs=[a_spec, b_spec, *extra_specs, *dep_specs], out_specs=out_spec,
        out_shape=out_shape, scratch_shapes=[pltpu.VMEM(acc_shape if nk > 1 else (8, LANE), F32)],
        compiler_params=_params(*(("arbitrary",) * 3 if n_sums else ("parallel", "parallel", "arbitrary"))),
    )(a, b, *extras, *deps)


def _rms(x, gain):
    return x * lax.rsqrt(jnp.mean(x * x, axis=-1, keepdims=True) + RMS_EPS) * gain


def _rms_bwd_tile(dh, x, gain, dres):
    rstd = lax.rsqrt(jnp.mean(x * x, axis=-1, keepdims=True) + RMS_EPS)
    xhat = x * rstd
    dxhat = dh * gain
    dx = dres + rstd * (dxhat - xhat * jnp.mean(dxhat * xhat, axis=-1, keepdims=True))
    return dx, dh * xhat, dx


def _ln_silu_tile(v, g, b):
    mu = jnp.mean(v, axis=-1, keepdims=True)
    cen = v - mu
    z = cen * lax.rsqrt(jnp.mean(cen * cen, axis=-1, keepdims=True) + LN_EPS) * g + b
    return z * _sigmoid(z)


def _ln_silu_bwd_tile(ds, v, g, b):
    mu = jnp.mean(v, axis=-1, keepdims=True)
    cen = v - mu
    rstd = lax.rsqrt(jnp.mean(cen * cen, axis=-1, keepdims=True) + LN_EPS)
    y = cen * rstd
    z = y * g + b
    sig = _sigmoid(z)
    dz = ds * sig * (1.0 + z * (1.0 - sig))
    dy = dz * g
    dv = rstd * (dy - jnp.mean(dy, axis=-1, keepdims=True) - y * jnp.mean(dy * y, axis=-1, keepdims=True))
    return dv, dz * y, dz, dv


def _loss_tile(x, tgt, gain):
    d = x.shape[-1]
    rstd = lax.rsqrt(jnp.mean(x * x, axis=-1, keepdims=True) + RMS_EPS)
    xhat = x * rstd
    err = xhat * gain - tgt
    dy = err / d
    dxhat = dy * gain
    dx = rstd * (dxhat - xhat * jnp.mean(dxhat * xhat, axis=-1, keepdims=True))
    return dx, 0.5 * jnp.mean(err * err, axis=-1, keepdims=True), dy * xhat


NN = ((1,), (0,))
NT = ((1,), (1,))
TN = ((0,), (0,))


def _rms_fwd(name, x, gain):
    t, d = x.shape
    tr = _tile(t, 512)

    def body(x_ref, g_ref, h_ref):
        h_ref[...] = _rms(x_ref[...], g_ref[...]).astype(BF16)

    return pl.pallas_call(
        body, name=name, grid=(t // tr,),
        in_specs=[pl.BlockSpec((tr, d), lambda i: (i, 0)), pl.BlockSpec((1, d), lambda i: (0, 0))],
        out_specs=pl.BlockSpec((tr, d), lambda i: (i, 0)),
        out_shape=jax.ShapeDtypeStruct((t, d), BF16), compiler_params=_params("parallel"),
    )(x, gain)


def _conv_tiles(t, seq):
    ts = _tile(seq, 1024)
    return ts, seq // ts, _tile(ts, 64)


def _conv_fwd(name, a, w, b, seq):
    _, t, d = a.shape
    k_taps = w.shape[0]
    ts, tps, rc = _conv_tiles(t, seq)
    hb = ts // HALO

    def body(cur_ref, prev_ref, w_ref, b_ref, v_ref, upad):
        i = pl.program_id(1)
        first = (i % tps) == 0
        pv = prev_ref[0].astype(F32)
        pg = prev_ref[1].astype(F32)
        upad[0:HALO, :] = jnp.where(first, 0.0, pv * _sigmoid(pg))
        upad[HALO:HALO + ts, :] = cur_ref[0].astype(F32) * _sigmoid(cur_ref[1].astype(F32))
        wv = w_ref[...]
        bias = jnp.broadcast_to(b_ref[...], (rc, LANE))
        for r0 in range(0, ts, rc):
            acc = bias
            for k in range(k_taps):
                acc = acc + wv[k:k + 1, :] * upad[pl.ds(HALO - (k_taps - 1) + k + r0, rc), :]
            v_ref[pl.ds(r0, rc), :] = acc

    return pl.pallas_call(
        body, name=name, grid=(d // LANE, t // ts),
        in_specs=[pl.BlockSpec((2, ts, LANE), lambda c, i: (0, i, c)),
                  pl.BlockSpec((2, HALO, LANE), lambda c, i: (0, jnp.maximum(i * hb - 1, 0), c)),
                  pl.BlockSpec((k_taps, LANE), lambda c, i: (0, c)),
                  pl.BlockSpec((1, LANE), lambda c, i: (0, c))],
        out_specs=pl.BlockSpec((ts, LANE), lambda c, i: (i, c)),
        out_shape=jax.ShapeDtypeStruct((t, d), F32),
        scratch_shapes=[pltpu.VMEM((HALO + ts, LANE), F32)],
        compiler_params=_params("parallel", "parallel"),
    )(a, a, w, b)


def _conv_bwd(name, a, dv, w, seq):
    _, t, d = a.shape
    k_taps = w.shape[0]
    ts, tps, rc = _conv_tiles(t, seq)
    hb = ts // HALO
    nhb = t // HALO

    def body(cur_ref, prev_ref, dv_ref, ndv_ref, w_ref, da_ref, dw_ref, dbp_ref, upad, dvpad, dwrows):
        i = pl.program_id(1)
        first = (i % tps) == 0
        last = (i % tps) == tps - 1
        pv = prev_ref[0].astype(F32)
        pg = prev_ref[1].astype(F32)
        upad[0:HALO, :] = jnp.where(first, 0.0, pv * _sigmoid(pg))
        upad[HALO:HALO + ts, :] = cur_ref[0].astype(F32) * _sigmoid(cur_ref[1].astype(F32))
        dvpad[0:ts, :] = dv_ref[...]
        dvpad[ts:ts + HALO, :] = jnp.where(last, 0.0, ndv_ref[...])
        wv = w_ref[...]

        @pl.when(i == 0)
        def _():
            dw_ref[...] = jnp.zeros_like(dw_ref)
            dbp_ref[...] = jnp.zeros_like(dbp_ref)

        sv = jnp.zeros((1, LANE), F32)
        sg = jnp.zeros((1, LANE), F32)
        for r0 in range(0, ts, rc):
            du = jnp.zeros((rc, LANE), F32)
            for k in range(k_taps):
                du = du + wv[k:k + 1, :] * dvpad[pl.ds(r0 + (k_taps - 1) - k, rc), :]
            av = cur_ref[0, pl.ds(r0, rc), :].astype(F32)
            sig = _sigmoid(cur_ref[1, pl.ds(r0, rc), :].astype(F32))
            dval = du * sig
            dgate = du * av * sig * (1.0 - sig)
            da_ref[0, pl.ds(r0, rc), :] = dval.astype(BF16)
            da_ref[1, pl.ds(r0, rc), :] = dgate.astype(BF16)
            sv = sv + jnp.sum(dval, axis=0, keepdims=True)
            sg = sg + jnp.sum(dgate, axis=0, keepdims=True)
        dbp_ref[0] += sv
        dbp_ref[1] += sg

        for k in range(k_taps):
            acc = jnp.zeros((rc, LANE), F32)
            for r0 in range(0, ts, rc):
                acc = acc + dvpad[pl.ds(r0, rc), :] * upad[pl.ds(HALO - (k_taps - 1) + k + r0, rc), :]
            dwrows[k:k + 1, :] = jnp.sum(acc, axis=0, keepdims=True)
        dw_ref[...] += dwrows[0:k_taps, :]

    return pl.pallas_call(
        body, name=name, grid=(d // LANE, t // ts),
        in_specs=[pl.BlockSpec((2, ts, LANE), lambda c, i: (0, i, c)),
                  pl.BlockSpec((2, HALO, LANE), lambda c, i: (0, jnp.maximum(i * hb - 1, 0), c)),
                  pl.BlockSpec((ts, LANE), lambda c, i: (i, c)),
                  pl.BlockSpec((HALO, LANE), lambda c, i: (jnp.minimum((i + 1) * hb, nhb - 1), c)),
                  pl.BlockSpec((k_taps, LANE), lambda c, i: (0, c))],
        out_specs=[pl.BlockSpec((2, ts, LANE), lambda c, i: (0, i, c)),
                   pl.BlockSpec((k_taps, LANE), lambda c, i: (0, c)),
                   pl.BlockSpec((2, 1, LANE), lambda c, i: (0, 0, c))],
        out_shape=[jax.ShapeDtypeStruct((2, t, d), BF16), jax.ShapeDtypeStruct((k_taps, d), F32),
                   jax.ShapeDtypeStruct((2, 1, d), F32)],
        scratch_shapes=[pltpu.VMEM((HALO + ts, LANE), F32), pltpu.VMEM((ts + HALO, LANE), F32),
                        pltpu.VMEM((HALO, LANE), F32)],
        compiler_params=_params("parallel", "arbitrary"),
    )(a, a, dv, dv, w)


def _pool_mix_fwd(name, x, gain, wp, scale, bias, next_gain, seq):
    t, d = x.shape
    ts = _tile(seq, 256)
    tps = seq // ts
    hb = ts // HALO
    cg = d // N_GROUPS
    sw = min(cg, LANE)

    def body(cur_ref, prev_ref, g_ref, w_ref, s_ref, b_ref, ng_ref, p_ref, r_ref, h_ref, hpad):
        i = pl.program_id(0)
        first = (i % tps) == 0
        g = g_ref[...]
        h_prev = jnp.where(first, 0.0, _rms(prev_ref[...], g))
        h_cur = _rms(cur_ref[...], g)
        for si in range(d // sw):
            hpad[si, 0:HALO, :] = h_prev[:, si * sw:(si + 1) * sw]
            hpad[si, HALO:HALO + ts, :] = h_cur[:, si * sw:(si + 1) * sw]
        pos = (i % tps) * ts + lax.broadcasted_iota(jnp.int32, (ts, 1), 0)
        for gi, win in enumerate(POOL_WINDOWS):
            sl = slice(gi * cg, (gi + 1) * cg)
            cnt = jnp.minimum(pos + 1, win).astype(F32)
            for si in range(gi * cg // sw, (gi + 1) * cg // sw):
                own = hpad[si, HALO:HALO + ts, :]
                acc = own
                for j in range(1, win):
                    acc = acc + hpad[si, pl.ds(HALO - j, ts), :]
                p_ref[:, si * sw:(si + 1) * sw] = (acc / cnt - own).astype(BF16)
            mixed = jnp.dot(p_ref[:, sl], w_ref[gi], preferred_element_type=F32)
            r_ref[:, sl] = cur_ref[:, sl] + s_ref[:, sl] * (mixed + b_ref[:, sl])
        h_ref[...] = _rms(r_ref[...], ng_ref[...]).astype(BF16)

    row = pl.BlockSpec((ts, d), lambda i: (i, 0))
    vec = pl.BlockSpec((1, d), lambda i: (0, 0))
    return pl.pallas_call(
        body, name=name, grid=(t // ts,),
        in_specs=[row, pl.BlockSpec((HALO, d), lambda i: (jnp.maximum(i * hb - 1, 0), 0)), vec,
                  pl.BlockSpec((N_GROUPS, cg, cg), lambda i: (0, 0, 0)), vec, vec, vec],
        out_specs=[row, row, row],
        out_shape=[jax.ShapeDtypeStruct((t, d), BF16), jax.ShapeDtypeStruct((t, d), F32),
                   jax.ShapeDtypeStruct((t, d), BF16)],
        scratch_shapes=[pltpu.VMEM((d // sw, HALO + ts, sw), F32)],
        compiler_params=_params("parallel"),
    )(x, x, gain, wp, scale, bias, next_gain)


def _pool_mix_bwd(name, pooled, wp, dr, x, gain, scale, bias, seq):
    t, d = x.shape
    ts = _tile(seq, 256)
    tps = seq // ts
    hb = ts // HALO
    nhb = t // HALO
    cg = d // N_GROUPS
    sw = min(cg, LANE)

    def body(p_ref, w_ref, dr_ref, ndr_ref, x_ref, g_ref, s_ref, b_ref, dx_ref, dw_ref, ds_ref, db_ref, dg_ref,
             qpad, dh):
        i = pl.program_id(0)
        last = (i % tps) == tps - 1
        pos = (i % tps) * ts + lax.broadcasted_iota(jnp.int32, (ts, 1), 0)

        @pl.when(i == 0)
        def _():
            dw_ref[...] = jnp.zeros_like(dw_ref)
            ds_ref[...] = jnp.zeros_like(ds_ref)
            db_ref[...] = jnp.zeros_like(db_ref)
            dg_ref[...] = jnp.zeros_like(dg_ref)

        for gi, win in enumerate(POOL_WINDOWS):
            sl = slice(gi * cg, (gi + 1) * cg)
            wv = w_ref[gi]
            sc = s_ref[:, sl]
            drv = dr_ref[:, sl]
            dmx = drv * sc
            dmx16 = dmx.astype(BF16)
            pooled = p_ref[:, sl]
            dw_ref[gi] += lax.dot_general(pooled, dmx16, (TN, ((), ())), preferred_element_type=F32)
            mixed = jnp.dot(pooled, wv, preferred_element_type=F32)
            ds_ref[:, sl] += jnp.sum(drv * (mixed + b_ref[:, sl]), axis=0, keepdims=True)
            db_ref[:, sl] += jnp.sum(dmx, axis=0, keepdims=True)
            cur = lax.dot_general(dmx16, wv, (NT, ((), ())), preferred_element_type=F32)
            nxt = lax.dot_general((ndr_ref[:, sl] * sc).astype(BF16), wv, (NT, ((), ())),
                                  preferred_element_type=F32)
            q_cur = cur / jnp.minimum(pos + 1, win).astype(F32)
            q_nxt = jnp.where(last, 0.0, nxt / float(win))
            for k, si in enumerate(range(gi * cg // sw, (gi + 1) * cg // sw)):
                part = slice(k * sw, (k + 1) * sw)
                qpad[si, 0:ts, :] = q_cur[:, part]
                qpad[si, ts:ts + HALO, :] = q_nxt[:, part]
                acc = -cur[:, part]
                for j in range(win):
                    acc = acc + qpad[si, pl.ds(j, ts), :]
                dh[:, si * sw:(si + 1) * sw] = acc
        dx, dgain_term, _ = _rms_bwd_tile(dh[...], x_ref[...], g_ref[...], dr_ref[...])
        dx_ref[...] = dx
        dg_ref[...] += jnp.sum(dgain_term, axis=0, keepdims=True)

    row = pl.BlockSpec((ts, d), lambda i: (i, 0))
    vec = pl.BlockSpec((1, d), lambda i: (0, 0))
    return pl.pallas_call(
        body, name=name, grid=(t // ts,),
        in_specs=[row, pl.BlockSpec((N_GROUPS, cg, cg), lambda i: (0, 0, 0)), row,
                  pl.BlockSpec((HALO, d), lambda i: (jnp.minimum((i + 1) * hb, nhb - 1), 0)), row, vec, vec, vec],
        out_specs=[row, pl.BlockSpec((N_GROUPS, cg, cg), lambda i: (0, 0, 0)), vec, vec, vec],
        out_shape=[jax.ShapeDtypeStruct((t, d), F32), jax.ShapeDtypeStruct((N_GROUPS, cg, cg), F32)]
        + [jax.ShapeDtypeStruct((1, d), F32)] * 3,
        scratch_shapes=[pltpu.VMEM((d // sw, ts + HALO, sw), F32), pltpu.VMEM((ts, d), F32)],
        compiler_params=_params("arbitrary"),
    )(pooled, wp, dr, dr, x, gain, scale, bias)


def _ctile(n, pref):
    return max(c for c in range(LANE, min(pref, n) + 1, LANE) if n % c == 0)


FFN_COLS = 1408
FFN_ROWS = 32
FFN_TILE = 1024


def _ffn_fwd(name, up, w, b, seq):
    _, t, dff = up.shape
    f = _ctile(dff, FFN_COLS)
    k_taps = w.shape[0]
    ts = _tile(seq, FFN_TILE)
    tps = seq // ts
    hb = ts // HALO16
    rc = _tile(ts, FFN_ROWS)

    def body(cur_ref, prev_ref, w_ref, b_ref, g_ref, apad):
        i = pl.program_id(1)
        first = (i % tps) == 0
        for ci, c0 in enumerate(range(0, f, LANE)):
            cols = slice(c0, c0 + LANE)
            apad[ci, 0:HALO16, :] = jnp.where(first, 0.0, prev_ref[:, cols].astype(F32))
            apad[ci, HALO16:HALO16 + ts, :] = cur_ref[0, :, cols].astype(F32)
            wv = w_ref[:, cols]
            wk = [jnp.broadcast_to(wv[k:k + 1, :], (rc, LANE)) for k in range(k_taps)]
            bias = jnp.broadcast_to(b_ref[:, cols], (rc, LANE))
            for r0 in range(0, ts, rc):
                c = bias
                for k in range(k_taps):
                    c = c + wk[k] * apad[ci, pl.ds(HALO16 - (k_taps - 1) + k + r0, rc), :]
                gate = cur_ref[1, pl.ds(r0, rc), cols].astype(F32)
                g_ref[pl.ds(r0, rc), cols] = (c * _sigmoid(c) * gate).astype(BF16)

    return pl.pallas_call(
        body, name=name, grid=(dff // f, t // ts),
        in_specs=[pl.BlockSpec((2, ts, f), lambda j, i: (0, i, j)),
                  pl.BlockSpec((None, HALO16, f), lambda j, i: (0, jnp.maximum(i * hb - 1, 0), j)),
                  pl.BlockSpec((k_taps, f), lambda j, i: (0, j)),
                  pl.BlockSpec((1, f), lambda j, i: (0, j))],
        out_specs=pl.BlockSpec((ts, f), lambda j, i: (i, j)),
        out_shape=jax.ShapeDtypeStruct((t, dff), BF16),
        scratch_shapes=[pltpu.VMEM((f // LANE, HALO16 + ts, LANE), F32)],
        compiler_params=_params("parallel", "parallel"),
    )(up, up, w, b)


def _ffn_bwd(name, up, dg, w, b, seq):
    _, t, dff = up.shape
    f = _ctile(dff, FFN_COLS)
    k_taps = w.shape[0]
    ts = _tile(seq, FFN_TILE)
    tps = seq // ts
    hb = ts // HALO16
    nhb = t // HALO16
    ext = ts + HALO16
    rc = _tile(ts, FFN_ROWS)

    def body(cur_ref, prev_ref, next_ref, dg_ref, ndg_ref, w_ref, b_ref, dup_ref, dw_ref, db_ref, apad, dcpad):
        i = pl.program_id(1)
        first = (i % tps) == 0
        last = (i % tps) == tps - 1

        @pl.when(i == 0)
        def _():
            dw_ref[...] = jnp.zeros_like(dw_ref)
            db_ref[...] = jnp.zeros_like(db_ref)

        for ci, c0 in enumerate(range(0, f, LANE)):
            cols = slice(c0, c0 + LANE)
            apad[ci, 0:HALO16, :] = jnp.where(first, 0.0, prev_ref[:, cols].astype(F32))
            apad[ci, HALO16:HALO16 + ts, :] = cur_ref[0, :, cols].astype(F32)
            apad[ci, HALO16 + ts:HALO16 + ext, :] = next_ref[0, :, cols].astype(F32)
            wv = w_ref[:, cols]
            wk = [jnp.broadcast_to(wv[k:k + 1, :], (rc, LANE)) for k in range(k_taps)]
            bias = jnp.broadcast_to(b_ref[:, cols], (rc, LANE))

            def conv_grad(r0, n, gate, dgv):
                c = bias[0:n]
                for k in range(k_taps):
                    c = c + wk[k][0:n] * apad[ci, pl.ds(HALO16 - (k_taps - 1) + k + r0, n), :]
                sig = _sigmoid(c)
                silu = c * sig
                return dgv * gate * (sig + silu * (1.0 - sig)), silu

            for r0 in range(0, ts, rc):
                dgv = dg_ref[pl.ds(r0, rc), cols].astype(F32)
                dc, silu = conv_grad(r0, rc, cur_ref[1, pl.ds(r0, rc), cols].astype(F32), dgv)
                dcpad[ci, pl.ds(r0, rc), :] = dc
                dup_ref[1, pl.ds(r0, rc), cols] = (dgv * silu).astype(BF16)
            dgv = jnp.where(last, 0.0, ndg_ref[:, cols].astype(F32))
            dc, _ = conv_grad(ts, HALO16, next_ref[1, :, cols].astype(F32), dgv)
            dcpad[ci, ts:ext, :] = dc

            dw_acc = [jnp.zeros((rc, LANE), F32) for _ in range(k_taps)]
            db_acc = jnp.zeros((rc, LANE), F32)
            for r0 in range(0, ts, rc):
                dact = jnp.zeros((rc, LANE), F32)
                for k in range(k_taps):
                    dact = dact + wk[k] * dcpad[ci, pl.ds(r0 + (k_taps - 1) - k, rc), :]
                dup_ref[0, pl.ds(r0, rc), cols] = dact.astype(BF16)
                dc = dcpad[ci, pl.ds(r0, rc), :]
                for k in range(k_taps):
                    dw_acc[k] = dw_acc[k] + dc * apad[ci, pl.ds(HALO16 - (k_taps - 1) + k + r0, rc), :]
                db_acc = db_acc + dc
            for k in range(k_taps):
                dw_ref[k:k + 1, cols] += jnp.sum(dw_acc[k], axis=0, keepdims=True)
            db_ref[:, cols] += jnp.sum(db_acc, axis=0, keepdims=True)

    return pl.pallas_call(
        body, name=name, grid=(dff // f, t // ts),
        in_specs=[pl.BlockSpec((2, ts, f), lambda j, i: (0, i, j)),
                  pl.BlockSpec((None, HALO16, f), lambda j, i: (0, jnp.maximum(i * hb - 1, 0), j)),
                  pl.BlockSpec((2, HALO16, f), lambda j, i: (0, jnp.minimum((i + 1) * hb, nhb - 1), j)),
                  pl.BlockSpec((ts, f), lambda j, i: (i, j)),
                  pl.BlockSpec((HALO16, f), lambda j, i: (jnp.minimum((i + 1) * hb, nhb - 1), j)),
                  pl.BlockSpec((k_taps, f), lambda j, i: (0, j)),
                  pl.BlockSpec((1, f), lambda j, i: (0, j))],
        out_specs=[pl.BlockSpec((2, ts, f), lambda j, i: (0, i, j)),
                   pl.BlockSpec((k_taps, f), lambda j, i: (0, j)),
                   pl.BlockSpec((1, f), lambda j, i: (0, j))],
        out_shape=[jax.ShapeDtypeStruct((2, t, dff), BF16), jax.ShapeDtypeStruct((k_taps, dff), F32),
                   jax.ShapeDtypeStruct((1, dff), F32)],
        scratch_shapes=[pltpu.VMEM((f // LANE, HALO16 + ext, LANE), F32), pltpu.VMEM((f // LANE, ext, LANE), F32)],
        compiler_params=_params("parallel", "arbitrary"),
    )(up, up, up, dg, dg, w, b)


def _sum_rows(name, g):
    ns, r, c = g.shape
    tr = _tile(r, 256)

    def body(g_ref, o_ref):
        acc = g_ref[0]
        for dev in range(1, ns):
            acc = acc + g_ref[dev]
        o_ref[...] = acc

    return pl.pallas_call(
        body, name=name, grid=(r // tr,),
        in_specs=[pl.BlockSpec((ns, tr, c), lambda i: (0, i, 0))],
        out_specs=pl.BlockSpec((tr, c), lambda i: (i, 0)),
        out_shape=jax.ShapeDtypeStruct((r, c), F32), compiler_params=_params("parallel"),
    )(g)


def _adamw(name, gsrc, w, m, v, layer=0, prev=None):
    ns, r, c = gsrc.shape
    nl = w.shape[0]
    tr = _tile(r, 256)
    prev = () if prev is None else tuple(prev)

    def body(g_ref, w_ref, m_ref, v_ref, *rest):
        go_ref, do_ref, mo_ref, vo_ref = rest[len(prev):]
        g = g_ref[0].astype(F32)
        for dev in range(1, ns):
            g = g + g_ref[dev].astype(F32)
        m_new = ADAM_B1 * m_ref[...] + (1.0 - ADAM_B1) * g
        v_new = ADAM_B2 * v_ref[...] + (1.0 - ADAM_B2) * (g * g)
        m_hat = m_new / (1.0 - ADAM_B1 ** ADAM_STEP)
        v_hat = v_new / (1.0 - ADAM_B2 ** ADAM_STEP)
        go_ref[...] = g
        do_ref[...] = -ADAM_LR * (m_hat / (jnp.sqrt(v_hat) + ADAM_EPS) + ADAM_WD * w_ref[...])
        mo_ref[...] = m_new
        vo_ref[...] = v_new

    row = pl.BlockSpec((None, tr, c), lambda i: (layer, i, 0))
    return pl.pallas_call(
        body, name=name, grid=(r // tr,),
        in_specs=[pl.BlockSpec((ns, tr, c), lambda i: (0, i, 0)), row, row, row] + [ANY] * len(prev),
        out_specs=[row] * 4, out_shape=[jax.ShapeDtypeStruct((nl, r, c), F32)] * 4,
        input_output_aliases={4 + i: i for i in range(len(prev))},
        compiler_params=_params("parallel"),
    )(gsrc, w, m, v, *prev)


def _ffn_forward(tag, r_in, h, get_wu, get_wd, wdw, bdw, seq, loss=None):
    t, d = r_in.shape
    tm = _tile(t, 512)
    wu = get_wu(h)
    dff = wu.shape[0] // 2
    tu = _tile(t, 1024)
    up = _mm(f"{tag}_up", h, wu, grid=(2, t // tu, 1),
             a_spec=pl.BlockSpec((tu, d), lambda j, i, k: (i, 0)),
             b_spec=pl.BlockSpec((dff, d), lambda j, i, k: (j, 0)),
             out_spec=pl.BlockSpec((None, tu, dff), lambda j, i, k: (j, i, 0)),
             out_shape=jax.ShapeDtypeStruct((2, t, dff), BF16), dims=NT, acc_shape=(tu, dff))
    wd = get_wd(up)
    g = _ffn_fwd(f"{tag}_act", up, wdw, bdw, seq)
    row = pl.BlockSpec((tm, d), lambda i, j, k: (i, 0))
    vec = pl.BlockSpec((1, d), lambda i, j, k: (0, 0))
    common = dict(grid=(t // tm, 1, 1), a_spec=pl.BlockSpec((tm, dff), lambda i, j, k: (i, 0)),
                  b_spec=pl.BlockSpec((dff, d), lambda i, j, k: (0, 0)), dims=NN, acc_shape=(tm, d))
    if loss is None:
        out = _mm(f"{tag}_down", g, wd, out_spec=row, out_shape=jax.ShapeDtypeStruct((t, d), F32),
                  extras=(r_in,), extra_specs=(row,), epilogue=lambda acc, ex, rows: ((ex[0][rows, :] + acc,), ()),
                  **common)
    else:
        def head(acc, ex, rows):
            dx, part, dgain = _loss_tile(ex[0][rows, :] + acc, ex[1][rows, :], ex[2][...])
            return (dx,), (part, dgain)

        out = _mm(f"{tag}_down", g, wd, out_spec=[row, pl.BlockSpec((1, 1), lambda i, j, k: (0, 0)), vec],
                  out_shape=[jax.ShapeDtypeStruct((t, d), F32), jax.ShapeDtypeStruct((1, 1), F32),
                             jax.ShapeDtypeStruct((1, d), F32)],
                  extras=(r_in, *loss), extra_specs=(row, row, vec), epilogue=head, n_sums=2, **common)
    return out, (r_in, h, up, g, wu, wd)


def _ffn_backward(tag, dr, saved, gain, wdw, bdw, seq, token=None):
    r_in, h, up, g, wu, wd = saved
    t, d = r_in.shape
    dff = wd.shape[0]
    tm = _tile(t, 512)
    tk = _tile(t, 2048)
    tku = _tile(t, 4096)
    cw = _ctile(dff, 1408)
    nc = dff // cw
    once = dict(pipeline_mode=pl.Buffered(1)) if tku == t else {}
    dg = _mm(f"{tag}_dg", dr, wd, grid=(t // tm, 1, 1),
             a_spec=pl.BlockSpec((tm, d), lambda i, j, k: (i, 0)),
             b_spec=pl.BlockSpec((dff, d), lambda i, j, k: (0, 0)),
             out_spec=pl.BlockSpec((tm, dff), lambda i, j, k: (i, 0)),
             out_shape=jax.ShapeDtypeStruct((t, dff), BF16), dims=NT, acc_shape=(tm, dff), token=token)
    dwd = _mm(f"{tag}_dwd", g, dr, grid=(dff // cw, 1, t // tk),
              a_spec=pl.BlockSpec((tk, cw), lambda i, j, k: (k, i)),
              b_spec=pl.BlockSpec((tk, d), lambda i, j, k: (k, 0)),
              out_spec=pl.BlockSpec((cw, d), lambda i, j, k: (i, 0)),
              out_shape=jax.ShapeDtypeStruct((dff, d), BF16), dims=TN, acc_shape=(cw, d))
    dup, dwdw, dbdw = _ffn_bwd(f"{tag}_dact", up, dg, wdw, bdw, seq)
    row = pl.BlockSpec((tm, d), lambda i, j, k: (i, 0))
    vec = pl.BlockSpec((1, d), lambda i, j, k: (0, 0))

    def norm_backward(acc, ex, rows):
        dx, dgain, colsum = _rms_bwd_tile(acc, ex[0][rows, :], ex[1][...], ex[2][rows, :])
        return (dx,), (dgain, colsum)

    dr_in, dgain, colsum = _mm(
        f"{tag}_dh", dup, wu, grid=(t // tm, 1, 1),
        a_spec=pl.BlockSpec((2, tm, dff), lambda i, j, k: (0, i, 0)),
        b_spec=pl.BlockSpec((2 * dff, d), lambda i, j, k: (0, 0), pipeline_mode=pl.Buffered(1)),
        out_spec=[row, vec, vec],
        out_shape=[jax.ShapeDtypeStruct((t, d), F32)] + [jax.ShapeDtypeStruct((1, d), F32)] * 2,
        dims=NN, acc_shape=(tm, d), extras=(r_in, gain, dr), extra_specs=(row, vec, row),
        epilogue=norm_backward, n_sums=2, parts=2)
    dwu = _mm(f"{tag}_dwu", dup, h, grid=(2 * nc, 1, t // tku),
              a_spec=pl.BlockSpec((None, tku, cw), lambda i, j, k: (i // nc, k, i % nc)),
              b_spec=pl.BlockSpec((tku, d), lambda i, j, k: (k, 0), **once),
              out_spec=pl.BlockSpec((cw, d), lambda i, j, k: (i, 0)),
              out_shape=jax.ShapeDtypeStruct((2 * dff, d), BF16), dims=TN, acc_shape=(cw, d))
    return dr_in, dgain, dwu, dwd, dwdw, dbdw, colsum


def _pad_to(vec, n):
    return jnp.pad(vec, (0, n - vec.shape[0]))


def _pack(parts, width):
    flat = jnp.concatenate([p.reshape(-1).astype(F32) for p in parts])
    n = -(-flat.shape[0] // (8 * width)) * (8 * width)
    return _pad_to(flat, n).reshape(n // width, width)


def _unpack(mat, shapes):
    flat = mat.reshape(-1)
    out, off = [], 0
    for s in shapes:
        n = 1
        for dim in s:
            n *= dim
        out.append(flat[off:off + n].reshape(s))
        off += n
    return out


def kernel(x, norm_mix, norm_ffn, conv_w_pw1, conv_b_pw1, conv_w_dw, conv_b_dw, conv_ln_g, conv_ln_b, conv_w_pw2, conv_b_pw2, pool_w, pool_b, pool_scale, ffn_w_up, ffn_w_dw, ffn_b_dw, ffn_w_down, final_norm, loss_target, m_norm_mix, m_norm_ffn, m_conv_w_pw1, m_conv_b_pw1, m_conv_w_dw, m_conv_b_dw, m_conv_ln_g, m_conv_ln_b, m_conv_w_pw2, m_conv_b_pw2, m_pool_w, m_pool_b, m_pool_scale, m_ffn_w_up, m_ffn_w_dw, m_ffn_b_dw, m_ffn_w_down, m_final_norm, v_norm_mix, v_norm_ffn, v_conv_w_pw1, v_conv_b_pw1, v_conv_w_dw, v_conv_b_dw, v_conv_ln_g, v_conv_ln_b, v_conv_w_pw2, v_conv_b_pw2, v_pool_w, v_pool_b, v_pool_scale, v_ffn_w_up, v_ffn_w_dw, v_ffn_b_dw, v_ffn_w_down, v_final_norm):
    bsz, seq, d = x.shape
    t = bsz * seq
    k_taps = conv_w_dw.shape[1]
    cs1 = conv_w_pw1.shape[2]
    dsh = d // N_DEV
    cg = d // N_GROUPS
    cgs = pool_w.shape[2]
    fu = ffn_w_up.shape[2]
    fd = ffn_w_down.shape[1]
    dff = fd * N_DEV
    nb = N_DEV // 2
    kf = ffn_w_dw.shape[1]
    fsh = ffn_w_dw.shape[2]
    my = _lin(_me())
    tm = _tile(t, 512)

    x2 = x.reshape(t, d)
    tgt2 = loss_target.reshape(t, d)

    small_shapes = [(k_taps, dsh), (dsh,), (dsh,), (2, kf, fsh)]
    small_mine = _pack([conv_w_dw[0], pool_b[0], pool_scale[0], ffn_w_dw], LANE)
    big = [conv_w_pw1[0], conv_w_pw2[0], ffn_w_up[0].T, ffn_w_down[0], pool_w[0], ffn_w_up[1].T, ffn_w_down[1]]
    gather = _exchange_start("gather_start", [small_mine] + [w.astype(BF16) for w in big], [GATHER2] * 8)
    h0 = _rms_fwd("l0_rms", x2, norm_mix[0:1])
    small_w = [norm_mix, norm_ffn, conv_b_pw1, conv_w_dw, conv_b_dw, conv_ln_g, conv_ln_b, conv_b_pw2, pool_b,
               pool_scale, ffn_w_dw, ffn_b_dw, final_norm]
    small_m = [m_norm_mix, m_norm_ffn, m_conv_b_pw1, m_conv_w_dw, m_conv_b_dw, m_conv_ln_g, m_conv_ln_b,
               m_conv_b_pw2, m_pool_b, m_pool_scale, m_ffn_w_dw, m_ffn_b_dw, m_final_norm]
    small_v = [v_norm_mix, v_norm_ffn, v_conv_b_pw1, v_conv_w_dw, v_conv_b_dw, v_conv_ln_g, v_conv_ln_b,
               v_conv_b_pw2, v_pool_b, v_pool_scale, v_ffn_w_dw, v_ffn_b_dw, v_final_norm]
    small_state = [_pack(group, 8 * LANE)[None] for group in (small_w, small_m, small_v)]
    forwarded = _exchange_forward("gather_forward_w1", gather, [0, 1], (h0, *small_state))
    small_all, w1 = _exchange_wait("gather_wait_w1", gather, [0, 1], forwarded)
    parts = [_unpack(small_all[dev], small_shapes) for dev in range(N_DEV)]
    wdw = jnp.concatenate([p[0] for p in parts], axis=1)
    pool_b_full = jnp.concatenate([p[1] for p in parts]).reshape(1, d)
    pool_s_full = jnp.concatenate([p[2] for p in parts]).reshape(1, d)
    fwdw = jnp.concatenate([p[3] for p in parts], axis=2)
    fbdw = ffn_b_dw.reshape(2, 1, dff)

    def columns(w):
        return w.transpose(1, 0, 2).reshape(w.shape[1], N_DEV * w.shape[2])

    def column_shards(w):
        return w.reshape(w.shape[0], N_DEV, w.shape[1] // N_DEV).transpose(1, 0, 2)

    w1 = columns(w1)
    a = _mm("l0_pw1", h0, w1, grid=(2, t // tm, 1),
            a_spec=pl.BlockSpec((tm, d), lambda j, i, k: (i, 0)),
            b_spec=pl.BlockSpec((d, d), lambda j, i, k: (0, j)),
            out_spec=pl.BlockSpec((None, tm, d), lambda j, i, k: (j, i, 0)),
            out_shape=jax.ShapeDtypeStruct((2, t, d), BF16), dims=NN, acc_shape=(tm, d),
            extras=(conv_b_pw1,), extra_specs=(pl.BlockSpec((1, d), lambda j, i, k: (0, j)),),
            epilogue=lambda acc, ex, rows: ((acc + ex[0][...],), ()))
    v = _conv_fwd("l0_conv", a, wdw, conv_b_dw, seq)
    forwarded = _exchange_forward("gather_forward_wu0", gather, [2, 3], v)
    (w2,) = _exchange_wait("gather_wait_w2", gather, [2], forwarded)
    w2 = w2.reshape(d, d)
    row = pl.BlockSpec((tm, d), lambda i, j, k: (i, 0))
    vec = pl.BlockSpec((1, d), lambda i, j, k: (0, 0))
    square = pl.BlockSpec((d, d), lambda i, j, k: (0, 0))

    def ln_silu(v_blk, ex):
        s_blk = _ln_silu_tile(v_blk, ex[0][...], ex[1][...]).astype(BF16)
        return s_blk, s_blk

    def residual_and_norm(acc, ex, rows):
        r_blk = ex[3][rows, :] + (acc + ex[2][...])
        return (r_blk, _rms(r_blk, ex[4][...])), ()

    r1, h1, s = _mm("l0_pw2", v, w2, grid=(t // tm, 1, 1), a_spec=row, b_spec=square, out_spec=[row, row, row],
                    out_shape=[jax.ShapeDtypeStruct((t, d), F32), jax.ShapeDtypeStruct((t, d), BF16),
                               jax.ShapeDtypeStruct((t, d), BF16)],
                    dims=NN, acc_shape=(tm, d), extras=(conv_ln_g, conv_ln_b, conv_b_pw2, x2, norm_ffn[0:1]),
                    extra_specs=(vec, vec, vec, row, vec), prologue=ln_silu, epilogue=residual_and_norm)

    def up_getter(name, idx):
        return lambda after: _exchange_wait(name, gather, [idx], after)[0].reshape(2 * dff, d)

    def down_getter(name, idx, forward=None):
        def get(after):
            if forward is not None:
                after = _exchange_forward(forward[0], gather, forward[1], after)
            return _exchange_wait(name, gather, [idx], after)[0].reshape(dff, d)
        return get

    r2, ffn0_saved = _ffn_forward("f0", r1, h1, up_getter("gather_wait_wu0", 3),
                                  down_getter("gather_wait_wd0", 4, ("gather_forward_wu1", [4, 5, 6])),
                                  fwdw[0], fbdw[0], seq)
    forwarded = _exchange_forward("gather_forward_wd1", gather, [7], r2)
    (wp,) = _exchange_wait("gather_wait_wp", gather, [5], forwarded)
    wp = wp.transpose(1, 0, 2, 3).reshape(N_GROUPS, cg, cg)
    pooled, r3, h3 = _pool_mix_fwd("l1_mix", r2, norm_mix[1:2], wp, pool_s_full, pool_b_full, norm_ffn[1:2], seq)
    (dr4, loss_part, dfinal), ffn1_saved = _ffn_forward(
        "f1", r3, h3, up_getter("gather_wait_wu1", 6), down_getter("gather_wait_wd1", 7), fwdw[1], fbdw[1], seq,
        loss=(tgt2, final_norm.reshape(1, d)))

    dr3, dnf1, dwu1, dwd1, dfw1, dfb1, _ = _ffn_backward("f1", dr4, ffn1_saved, norm_ffn[1:2], fwdw[1], fbdw[1], seq)
    scatter_a = _exchange_start("scatter_f1_start", [dwu1.reshape(N_DEV, fu, d), dwd1.reshape(N_DEV, fd, d)],
                                [SCATTER, SCATTER])
    dr2, dwp, dpool_s, dpool_b, dnm1 = _pool_mix_bwd(
        "l1_dmix", pooled, wp, dr3, r2, norm_mix[1:2], pool_s_full + scatter_a["token"][0:1, 0:1], pool_b_full, seq)
    dr1, dnf0, dwu0, dwd0, dfw0, dfb0, db2 = _ffn_backward("f0", dr2, ffn0_saved, norm_ffn[0:1], fwdw[0], fbdw[0], seq)
    dwp_b = dwp.astype(BF16).reshape(N_GROUPS, N_DEV, cgs, cg).transpose(1, 0, 2, 3)
    tk = _tile(t, 2048)
    dw2 = _mm("l0_dw2", s, dr1, grid=(1, 1, t // tk),
              a_spec=pl.BlockSpec((tk, d), lambda i, j, k: (k, 0)),
              b_spec=pl.BlockSpec((tk, d), lambda i, j, k: (k, 0)),
              out_spec=pl.BlockSpec((d, d), lambda i, j, k: (0, 0)),
              out_shape=jax.ShapeDtypeStruct((d, d), BF16), dims=TN, acc_shape=(d, d))
    scatter_b = _exchange_start("scatter_f0_start", [dwu0.reshape(N_DEV, fu, d), dwd0.reshape(N_DEV, fd, d), dwp_b,
                                                     dw2.reshape(N_DEV, d // N_DEV, d)], [SCATTER] * 4)

    def ln_silu_backward(acc, ex, rows):
        dv_blk, dgain, dbias, colsum = _ln_silu_bwd_tile(acc, ex[0][rows, :], ex[1][...], ex[2][...])
        return (dv_blk,), (dgain, dbias, colsum)

    dv, dlg, dlb, dbdw = _mm("l0_ds", dr1, w2, grid=(t // tm, 1, 1), a_spec=row, b_spec=square,
                             out_spec=[row, vec, vec, vec],
                             out_shape=[jax.ShapeDtypeStruct((t, d), F32)] + [jax.ShapeDtypeStruct((1, d), F32)] * 3,
                             dims=NT, acc_shape=(tm, d), extras=(v, conv_ln_g, conv_ln_b), extra_specs=(row, vec, vec),
                             epilogue=ln_silu_backward, n_sums=3, token=scatter_b["token"])
    da, dwdw, db1 = _conv_bwd("l0_dconv", a, dv, wdw, seq)
    tk1 = _tile(t, 4096)
    once = dict(pipeline_mode=pl.Buffered(1)) if tk1 == t else {}
    dw1 = _mm("l0_dw1", h0, da, grid=(1, 2, t // tk1),
              a_spec=pl.BlockSpec((tk1, d), lambda i, j, k: (k, 0), **once),
              b_spec=pl.BlockSpec((None, tk1, d), lambda i, j, k: (j, k, 0)),
              out_spec=pl.BlockSpec((d, d), lambda i, j, k: (0, j)),
              out_shape=jax.ShapeDtypeStruct((d, 2 * d), BF16), dims=TN, acc_shape=(d, d))
    scatter_c = _exchange_start("scatter_l0_start", [column_shards(dw1)], [SCATTER])
    def norm_backward(acc, ex, rows):
        dx_blk, dgain, colsum = _rms_bwd_tile(acc, ex[0][rows, :], ex[1][...], ex[2][rows, :])
        return (dx_blk,), (dgain, colsum)

    dx, dnm0, _ = _mm("l0_dh", da, w1, grid=(t // tm, 1, 1),
                      a_spec=pl.BlockSpec((2, tm, d), lambda i, j, k: (0, i, 0)),
                      b_spec=pl.BlockSpec((d, 2 * d), lambda i, j, k: (0, 0), pipeline_mode=pl.Buffered(1)),
                      out_spec=[row, vec, vec],
                      out_shape=[jax.ShapeDtypeStruct((t, d), F32)] + [jax.ShapeDtypeStruct((1, d), F32)] * 2,
                      dims=NT, acc_shape=(tm, d), extras=(x2, norm_mix[0:1], dr1), extra_specs=(row, vec, row),
                      epilogue=norm_backward, n_sums=2, parts=2, token=scatter_c["token"])

    dffn_w = jnp.stack([dfw0, dfw1])
    dffn_b = jnp.stack([dfb0, dfb1]).reshape(2, dff)
    small_parts = [loss_part, jnp.concatenate([dnm0, dnm1]), jnp.concatenate([dnf0, dnf1]), db1, dwdw, dbdw, dlg, dlb,
                   db2, dpool_b, dpool_s, dffn_w, dffn_b, dfinal]
    small_part_shapes = [(1,), (2, d), (2, d), (1, 2 * d), (k_taps, d), (1, d), (1, d), (1, d), (1, d), (1, d), (1, d),
                         (2, kf, dff), (2, dff), (d,)]
    packed = _pack(small_parts, 8 * LANE)
    gather_small = _exchange_start("gather_small_start", [packed], [GATHER])

    def big_update(name, recv, w, m, v, layer=0, prev=None):
        shape = w.shape
        c = recv.shape[-1]
        rows = recv.size // (N_DEV * c)
        nl = w.size // (rows * c)
        outs = _adamw(name, recv.reshape(N_DEV, rows, c), w.reshape(nl, rows, c), m.reshape(nl, rows, c),
                      v.reshape(nl, rows, c), layer, prev)
        return outs, [o.reshape(shape) for o in outs]

    wu_t = [p.transpose(0, 2, 1) for p in (ffn_w_up, m_ffn_w_up, v_ffn_w_up)]
    g_wu1, g_wd1 = _exchange_wait("scatter_f1_wait", scatter_a, [0, 1], gather_small["token"])
    raw_wu, _ = big_update("adam_wu1", g_wu1, *wu_t, 1)
    raw_wd, _ = big_update("adam_wd1", g_wd1, ffn_w_down, m_ffn_w_down, v_ffn_w_down, 1)
    g_wu0, g_wd0, g_wp, g_w2 = _exchange_wait("scatter_f0_wait", scatter_b, [0, 1, 2, 3], raw_wd[0])
    _, u_wu = big_update("adam_wu0", g_wu0, *wu_t, 0, raw_wu)
    u_wu = [o.transpose(0, 2, 1) for o in u_wu]
    _, u_wd = big_update("adam_wd0", g_wd0, ffn_w_down, m_ffn_w_down, v_ffn_w_down, 0, raw_wd)
    _, u_wp = big_update("adam_wp", g_wp, pool_w, m_pool_w, v_pool_w)
    (g_w1,) = _exchange_wait("scatter_l0_wait", scatter_c, [0], u_wp[0])
    _, u_w1 = big_update("adam_w1", g_w1, conv_w_pw1, m_conv_w_pw1, v_conv_w_pw1)
    _, u_w2 = big_update("adam_w2", g_w2, conv_w_pw2, m_conv_w_pw2, v_conv_w_pw2)
    (all_small,) = _exchange_wait("gather_small_wait", gather_small, [0], u_w2[0])
    summed = _sum_rows("sum_small_grads", all_small)
    (loss_v, g_nm, g_nf, g_b1, g_wdw, g_bdw, g_lg, g_lb, g_b2, g_pb, g_ps, g_fw, g_fb,
     g_fin) = _unpack(summed, small_part_shapes)
    loss = loss_v[0]
    g_wdw_mine = lax.dynamic_slice_in_dim(g_wdw, my * dsh, dsh, axis=1)[None]
    g_pb_mine = lax.dynamic_slice_in_dim(g_pb, my * dsh, dsh, axis=1)
    g_ps_mine = lax.dynamic_slice_in_dim(g_ps, my * dsh, dsh, axis=1)
    g_fw_mine = lax.dynamic_slice_in_dim(g_fw, my * fsh, fsh, axis=2)

    small_g = [g_nm, g_nf, g_b1, g_wdw_mine, g_bdw, g_lg, g_lb, g_b2, g_pb_mine, g_ps_mine, g_fw_mine, g_fb, g_fin]
    shapes = [w.shape for w in small_w]
    outs = _adamw("adam_small", _pack(small_g, 8 * LANE)[None], *small_state)
    sg, sd, sm, sv = [_unpack(o, shapes) for o in outs]

    def leaf(kind):
        (nm, nf, b1, wdw_, bdw_, lg, lb, b2, pb, ps, fw, fb, fin) = (sg, sd, sm, sv)[kind]
        return [nm, nf, u_w1[kind], b1, wdw_, bdw_, lg, lb, u_w2[kind], b2, u_wp[kind], pb, ps, u_wu[kind], fw, fb,
                u_wd[kind], fin]

    return (loss, dx.reshape(bsz, seq, d), *leaf(0), *leaf(1), *leaf(2), *leaf(3))
```

```python
import functools

import jax
import jax.numpy as jnp
from jax import lax
from jax.experimental import pallas as pl
from jax.experimental.pallas import tpu as pltpu

F32 = jnp.float32
BF16 = jnp.bfloat16
MESH = pl.DeviceIdType.MESH
HBM = pl.BlockSpec(memory_space=pltpu.HBM)

N_DEV = 8
RMS_EPS = 1e-6
LN_EPS = 1e-5
POOL_WINDOWS = (2, 4, 8, 16)
N_GROUPS = len(POOL_WINDOWS)
ADAM_LR = 0.001
ADAM_B1 = 0.9
ADAM_B2 = 0.999
ADAM_EPS = 1e-08
ADAM_WD = 0.01
ADAM_STEP = 10

LANE = 128
HALO = 32
HALO16 = 16
VMEM_LIMIT = 56 * 1024 * 1024


def _params(*sem):
    return pltpu.CompilerParams(dimension_semantics=sem if sem else None, vmem_limit_bytes=VMEM_LIMIT)


def _tile(n, pref):
    for t in range(min(pref, n), 15, -1):
        if n % t == 0 and t % 16 == 0:
            return t
    return n


def _sigmoid(z):
    return 1.0 / (1.0 + jnp.exp(-z))


def _me():
    return lax.axis_index("x"), lax.axis_index("y"), lax.axis_index("c")


def _flip(pos, m):
    x, y, c = pos
    return ((1 - x) if m & 4 else x, (1 - y) if m & 2 else y, (1 - c) if m & 1 else c)


def _lin(pos):
    return 4 * pos[0] + 2 * pos[1] + pos[2]


SEM = pl.BlockSpec(memory_space=pltpu.SEMAPHORE)
ANY = pl.BlockSpec(memory_space=pl.ANY)
EFFECT = pltpu.SideEffectType.DATAFLOW_SIDE_EFFECTING


SCATTER = "scatter"
GATHER = "gather"
GATHER2 = "gather2"
ALL_MASKS = (1, 2, 3, 4, 5, 6, 7)
SIBLING = 1
CHIPS = (2, 4, 6)


class _Copies:
    def __init__(self, a, mode, src, land, send_sems, recv_sems):
        self.a, self.mode, self.src, self.land = a, mode, src, land
        self.send_sems, self.recv_sems = send_sems, recv_sems
        self.me = _me()
        self.first = (SIBLING,) + CHIPS if mode == GATHER2 else ALL_MASKS

    def _sems(self, m, to):
        return dict(send_sem=self.send_sems.at[self.a * N_DEV + m], recv_sem=self.recv_sems.at[self.a * N_DEV + m],
                    device_id=to, device_id_type=MESH)

    def _block(self, pid):
        return self.src.at[pid] if self.mode == SCATTER else self.src

    def local(self):
        my = _lin(self.me)
        return pltpu.make_async_copy(self._block(my), self.land.at[my], self.send_sems.at[self.a * N_DEV])

    def send(self, m):
        peer = _flip(self.me, m)
        return pltpu.make_async_remote_copy(src_ref=self._block(_lin(peer)), dst_ref=self.land.at[_lin(self.me)],
                                            **self._sems(m, peer))

    def arrival(self, m):
        rows = self.land.at[_lin(_flip(self.me, m))]
        return pltpu.make_async_remote_copy(src_ref=rows, dst_ref=rows, **self._sems(m, _flip(self.me, m)))

    def forward(self, m):
        rows = self.land.at[_lin(_flip(self.me, m))]
        return pltpu.make_async_remote_copy(src_ref=rows, dst_ref=rows, **self._sems(m | 1, _flip(self.me, SIBLING)))


def _exchange_start(name, arrs, modes):
    n = len(arrs)
    blocks = [a.shape[1:] if md == SCATTER else a.shape for a, md in zip(arrs, modes)]

    def body(*refs):
        srcs, lands = refs[:n], refs[n:2 * n]
        send_sems, recv_sems = refs[2 * n], refs[2 * n + 1]
        token = refs[-1]
        for a in range(n):
            cp = _Copies(a, modes[a], srcs[a], lands[a], send_sems, recv_sems)
            cp.local().start()
            for m in cp.first:
                cp.send(m).start()
        token[...] = jnp.zeros_like(token)

    lands = [lax.empty((N_DEV,) + tuple(b), a.dtype) for a, b in zip(arrs, blocks)]
    outs = pl.pallas_call(
        body, name=name,
        out_shape=(pltpu.SemaphoreType.DMA((n * N_DEV,)), pltpu.SemaphoreType.DMA((n * N_DEV,)),
                   *[pltpu.HBM(a.shape, a.dtype) for a in arrs], *[pltpu.HBM(l.shape, l.dtype) for l in lands],
                   jax.ShapeDtypeStruct((8, LANE), F32)),
        in_specs=[HBM] * (2 * n),
        out_specs=(SEM, SEM, *[HBM] * (2 * n), pl.BlockSpec(memory_space=pltpu.VMEM)),
        input_output_aliases={i: 2 + i for i in range(2 * n)},
        compiler_params=pltpu.CompilerParams(has_side_effects=EFFECT),
    )(*[pltpu.with_memory_space_constraint(a, pltpu.HBM) for a in arrs],
      *[pltpu.with_memory_space_constraint(l, pltpu.HBM) for l in lands])
    return dict(send=outs[0], recv=outs[1], srcs=list(outs[2:2 + n]), lands=list(outs[2 + n:2 + 2 * n]),
                modes=modes, token=outs[-1])


def _exchange_forward(name, handle, which, after):
    k = len(which)

    def half(wait):
        def body(*refs):
            lands = refs[:k]
            send_sems, recv_sems = refs[k], refs[k + 1]
            token = refs[-1]
            for pos, a in enumerate(which):
                cp = _Copies(a, GATHER2, None, lands[pos], send_sems, recv_sems)
                for m in CHIPS:
                    if wait:
                        cp.arrival(m).wait_recv()
                    else:
                        cp.forward(m).start()
            token[...] = jnp.zeros_like(token)
        return body

    def call(body, call_name, lands, after):
        after = tuple(after) if isinstance(after, (tuple, list)) else (after,)
        outs = pl.pallas_call(
            body, name=call_name,
            out_shape=(*[pltpu.HBM(x.shape, x.dtype) for x in lands], jax.ShapeDtypeStruct((8, LANE), F32)),
            in_specs=[HBM] * k + [SEM, SEM] + [ANY] * len(after),
            out_specs=(*[HBM] * k, pl.BlockSpec(memory_space=pltpu.VMEM)),
            input_output_aliases={i: i for i in range(k)},
            compiler_params=pltpu.CompilerParams(has_side_effects=EFFECT),
        )(*lands, handle["send"], handle["recv"], *after)
        return list(outs[:k]), outs[-1]

    lands, arrived = call(half(True), name + "_arrived", [handle["lands"][a] for a in which], after)
    lands, token = call(half(False), name, lands, arrived)
    for pos, a in enumerate(which):
        handle["lands"][a] = lands[pos]
    return token


def _exchange_wait(name, handle, which, after):
    k = len(which)
    modes = handle["modes"]

    def body(*refs):
        srcs, lands = refs[:k], refs[k:2 * k]
        send_sems, recv_sems = refs[2 * k], refs[2 * k + 1]
        for pos, a in enumerate(which):
            cp = _Copies(a, modes[a], srcs[pos], lands[pos], send_sems, recv_sems)
            cp.local().wait()
            for m in cp.first:
                cp.send(m).wait_send()
            if modes[a] == GATHER2:
                for m in CHIPS:
                    cp.forward(m).wait_send()
                arrivals = (SIBLING,) + tuple(m | 1 for m in CHIPS)
            else:
                arrivals = ALL_MASKS
            for m in arrivals:
                cp.arrival(m).wait_recv()

    srcs = [handle["srcs"][a] for a in which]
    lands = [handle["lands"][a] for a in which]
    outs = pl.pallas_call(
        body, name=name,
        out_shape=tuple(pltpu.HBM(x.shape, x.dtype) for x in srcs + lands),
        in_specs=[HBM] * (2 * k) + [SEM, SEM, ANY], out_specs=tuple([HBM] * (2 * k)),
        input_output_aliases={i: i for i in range(2 * k)},
        compiler_params=pltpu.CompilerParams(has_side_effects=EFFECT),
    )(*srcs, *lands, handle["send"], handle["recv"], after)
    for pos, a in enumerate(which):
        handle["srcs"][a], handle["lands"][a] = outs[pos], outs[k + pos]
    return list(outs[k:])


def _mm(name, a, b, *, grid, a_spec, b_spec, out_spec, out_shape, dims, acc_shape, extras=(), extra_specs=(),
        epilogue=None, token=None, prologue=None, n_sums=0, parts=1):
    nk = grid[2]
    ne = len(extras)
    deps = () if token is None else (token,)
    dep_specs = [pl.BlockSpec((8, LANE), lambda i, j, k: (0, 0))] * len(deps)
    n_out = len(out_shape) if isinstance(out_shape, (list, tuple)) else 1
    n_tiles = n_out - n_sums - (1 if prologue is not None else 0)

    def body(a_ref, b_ref, *rest):
        ex, o_refs, acc_ref = rest[:ne], rest[ne + len(deps):ne + len(deps) + n_out], rest[ne + len(deps) + n_out]
        k = pl.program_id(2)
        if parts == 1:
            a_blk, saved = a_ref[...], None
            if prologue is not None:
                a_blk, saved = prologue(a_blk, ex)
                o_refs[n_tiles][...] = saved
            part = lax.dot_general(a_blk.astype(BF16), b_ref[...].astype(BF16), (dims, ((), ())),
                                   preferred_element_type=F32)
        else:
            kb = b_ref.shape[dims[1][0]] // parts
            part = None
            for p in range(parts):
                b_blk = b_ref[p * kb:(p + 1) * kb, :] if dims[1][0] == 0 else b_ref[:, p * kb:(p + 1) * kb]
                term = lax.dot_general(a_ref[p].astype(BF16), b_blk.astype(BF16), (dims, ((), ())),
                                       preferred_element_type=F32)
                part = term if part is None else part + term
        sum_refs = o_refs[n_out - n_sums:]

        def add_sums(terms):
            @pl.when((pl.program_id(0) == 0) & (pl.program_id(1) == 0))
            def _():
                for o_ref in sum_refs:
                    o_ref[...] = jnp.zeros_like(o_ref)

            for o_ref, term in zip(sum_refs, terms):
                o_ref[...] += jnp.sum(term, axis=0, keepdims=True)

        def finish(r):
            tiles, terms = ((r,), ()) if epilogue is None else epilogue(r, ex, slice(None))
            for o_ref, val in zip(o_refs, tiles):
                o_ref[...] = val.astype(o_ref.dtype)
            if n_sums:
                add_sums(terms)

        if nk == 1:
            finish(part)
            return

        @pl.when(k == 0)
        def _():
            acc_ref[...] = part

        @pl.when((k > 0) & (k < nk - 1))
        def _():
            acc_ref[...] += part

        @pl.when(k == nk - 1)
        def _():
            finish(acc_ref[...] + part)

    return pl.pallas_call(
        body, name=name, grid=grid, in_specs=[a_spec, b_spec, *extra_specs, *dep_specs], out_specs=out_spec,
        out_shape=out_shape, scratch_shapes=[pltpu.VMEM(acc_shape if nk > 1 else (8, LANE), F32)],
        compiler_params=_params(*(("arbitrary",) * 3 if n_sums else ("parallel", "parallel", "arbitrary"))),
    )(a, b, *extras, *deps)


def _rms(x, gain):
    return x * lax.rsqrt(jnp.mean(x * x, axis=-1, keepdims=True) + RMS_EPS) * gain


def _rms_bwd_tile(dh, x, gain, dres):
    rstd = lax.rsqrt(jnp.mean(x * x, axis=-1, keepdims=True) + RMS_EPS)
    xhat = x * rstd
    dxhat = dh * gain
    dx = dres + rstd * (dxhat - xhat * jnp.mean(dxhat * xhat, axis=-1, keepdims=True))
    return dx, dh * xhat, dx


def _ln_silu_tile(v, g, b):
    mu = jnp.mean(v, axis=-1, keepdims=True)
    cen = v - mu
    z = cen * lax.rsqrt(jnp.mean(cen * cen, axis=-1, keepdims=True) + LN_EPS) * g + b
    return z * _sigmoid(z)


def _ln_silu_bwd_tile(ds, v, g, b):
    mu = jnp.mean(v, axis=-1, keepdims=True)
    cen = v - mu
    rstd = lax.rsqrt(jnp.mean(cen * cen, axis=-1, keepdims=True) + LN_EPS)
    y = cen * rstd
    z = y * g + b
    sig = _sigmoid(z)
    dz = ds * sig * (1.0 + z * (1.0 - sig))
    dy = dz * g
    dv = rstd * (dy - jnp.mean(dy, axis=-1, keepdims=True) - y * jnp.mean(dy * y, axis=-1, keepdims=True))
    return dv, dz * y, dz, dv


def _loss_tile(x, tgt, gain):
    d = x.shape[-1]
    rstd = lax.rsqrt(jnp.mean(x * x, axis=-1, keepdims=True) + RMS_EPS)
    xhat = x * rstd
    err = xhat * gain - tgt
    dy = err / d
    dxhat = dy * gain
    dx = rstd * (dxhat - xhat * jnp.mean(dxhat * xhat, axis=-1, keepdims=True))
    return dx, 0.5 * jnp.mean(err * err, axis=-1, keepdims=True), dy * xhat


NN = ((1,), (0,))
NT = ((1,), (1,))
TN = ((0,), (0,))


def _rms_fwd(name, x, gain):
    t, d = x.shape
    tr = _tile(t, 512)

    def body(x_ref, g_ref, h_ref):
        h_ref[...] = _rms(x_ref[...], g_ref[...]).astype(BF16)

    return pl.pallas_call(
        body, name=name, grid=(t // tr,),
        in_specs=[pl.BlockSpec((tr, d), lambda i: (i, 0)), pl.BlockSpec((1, d), lambda i: (0, 0))],
        out_specs=pl.BlockSpec((tr, d), lambda i: (i, 0)),
        out_shape=jax.ShapeDtypeStruct((t, d), BF16), compiler_params=_params("parallel"),
    )(x, gain)


def _conv_tiles(t, seq):
    ts = _tile(seq, 1024)
    return ts, seq // ts, _tile(ts, 64)


def _conv_fwd(name, a, w, b, seq):
    _, t, d = a.shape
    k_taps = w.shape[0]
    ts, tps, rc = _conv_tiles(t, seq)
    hb = ts // HALO

    def body(cur_ref, prev_ref, w_ref, b_ref, v_ref, upad):
        i = pl.program_id(1)
        first = (i % tps) == 0
        pv = prev_ref[0].astype(F32)
        pg = prev_ref[1].astype(F32)
        upad[0:HALO, :] = jnp.where(first, 0.0, pv * _sigmoid(pg))
        upad[HALO:HALO + ts, :] = cur_ref[0].astype(F32) * _sigmoid(cur_ref[1].astype(F32))
        wv = w_ref[...]
        bias = jnp.broadcast_to(b_ref[...], (rc, LANE))
        for r0 in range(0, ts, rc):
            acc = bias
            for k in range(k_taps):
                acc = acc + wv[k:k + 1, :] * upad[pl.ds(HALO - (k_taps - 1) + k + r0, rc), :]
            v_ref[pl.ds(r0, rc), :] = acc

    return pl.pallas_call(
        body, name=name, grid=(d // LANE, t // ts),
        in_specs=[pl.BlockSpec((2, ts, LANE), lambda c, i: (0, i, c)),
                  pl.BlockSpec((2, HALO, LANE), lambda c, i: (0, jnp.maximum(i * hb - 1, 0), c)),
                  pl.BlockSpec((k_taps, LANE), lambda c, i: (0, c)),
                  pl.BlockSpec((1, LANE), lambda c, i: (0, c))],
        out_specs=pl.BlockSpec((ts, LANE), lambda c, i: (i, c)),
        out_shape=jax.ShapeDtypeStruct((t, d), F32),
        scratch_shapes=[pltpu.VMEM((HALO + ts, LANE), F32)],
        compiler_params=_params("parallel", "parallel"),
    )(a, a, w, b)


def _conv_bwd(name, a, dv, w, seq):
    _, t, d = a.shape
    k_taps = w.shape[0]
    ts, tps, rc = _conv_tiles(t, seq)
    hb = ts // HALO
    nhb = t // HALO

    def body(cur_ref, prev_ref, dv_ref, ndv_ref, w_ref, da_ref, dw_ref, dbp_ref, upad, dvpad, dwrows):
        i = pl.program_id(1)
        first = (i % tps) == 0
        last = (i % tps) == tps - 1
        pv = prev_ref[0].astype(F32)
        pg = prev_ref[1].astype(F32)
        upad[0:HALO, :] = jnp.where(first, 0.0, pv * _sigmoid(pg))
        upad[HALO:HALO + ts, :] = cur_ref[0].astype(F32) * _sigmoid(cur_ref[1].astype(F32))
        dvpad[0:ts, :] = dv_ref[...]
        dvpad[ts:ts + HALO, :] = jnp.where(last, 0.0, ndv_ref[...])
        wv = w_ref[...]

        @pl.when(i == 0)
        def _():
            dw_ref[...] = jnp.zeros_like(dw_ref)
            dbp_ref[...] = jnp.zeros_like(dbp_ref)

        sv = jnp.zeros((1, LANE), F32)
        sg = jnp.zeros((1, LANE), F32)
        for r0 in range(0, ts, rc):
            du = jnp.zeros((rc, LANE), F32)
            for k in range(k_taps):
                du = du + wv[k:k + 1, :] * dvpad[pl.ds(r0 + (k_taps - 1) - k, rc), :]
            av = cur_ref[0, pl.ds(r0, rc), :].astype(F32)
            sig = _sigmoid(cur_ref[1, pl.ds(r0, rc), :].astype(F32))
            dval = du * sig
            dgate = du * av * sig * (1.0 - sig)
            da_ref[0, pl.ds(r0, rc), :] = dval.astype(BF16)
            da_ref[1, pl.ds(r0, rc), :] = dgate.astype(BF16)
            sv = sv + jnp.sum(dval, axis=0, keepdims=True)
            sg = sg + jnp.sum(dgate, axis=0, keepdims=True)
        dbp_ref[0] += sv
        dbp_ref[1] += sg

        for k in range(k_taps):
            acc = jnp.zeros((rc, LANE), F32)
            for r0 in range(0, ts, rc):
                acc = acc + dvpad[pl.ds(r0, rc), :] * upad[pl.ds(HALO - (k_taps - 1) + k + r0, rc), :]
            dwrows[k:k + 1, :] = jnp.sum(acc, axis=0, keepdims=True)
        dw_ref[...] += dwrows[0:k_taps, :]

    return pl.pallas_call(
        body, name=name, grid=(d // LANE, t // ts),
        in_specs=[pl.BlockSpec((2, ts, LANE), lambda c, i: (0, i, c)),
                  pl.BlockSpec((2, HALO, LANE), lambda c, i: (0, jnp.maximum(i * hb - 1, 0), c)),
                  pl.BlockSpec((ts, LANE), lambda c, i: (i, c)),
                  pl.BlockSpec((HALO, LANE), lambda c, i: (jnp.minimum((i + 1) * hb, nhb - 1), c)),
                  pl.BlockSpec((k_taps, LANE), lambda c, i: (0, c))],
        out_specs=[pl.BlockSpec((2, ts, LANE), lambda c, i: (0, i, c)),
                   pl.BlockSpec((k_taps, LANE), lambda c, i: (0, c)),
                   pl.BlockSpec((2, 1, LANE), lambda c, i: (0, 0, c))],
        out_shape=[jax.ShapeDtypeStruct((2, t, d), BF16), jax.ShapeDtypeStruct((k_taps, d), F32),
                   jax.ShapeDtypeStruct((2, 1, d), F32)],
        scratch_shapes=[pltpu.VMEM((HALO + ts, LANE), F32), pltpu.VMEM((ts + HALO, LANE), F32),
                        pltpu.VMEM((HALO, LANE), F32)],
        compiler_params=_params("parallel", "arbitrary"),
    )(a, a, dv, dv, w)


def _pool_mix_fwd(name, x, gain, wp, scale, bias, next_gain, seq):
    t, d = x.shape
    ts = _tile(seq, 256)
    tps = seq // ts
    hb = ts // HALO
    cg = d // N_GROUPS
    sw = min(cg, LANE)

    def body(cur_ref, prev_ref, g_ref, w_ref, s_ref, b_ref, ng_ref, p_ref, r_ref, h_ref, hpad):
        i = pl.program_id(0)
        first = (i % tps) == 0
        g = g_ref[...]
        h_prev = jnp.where(first, 0.0, _rms(prev_ref[...], g))
        h_cur = _rms(cur_ref[...], g)
        for si in range(d // sw):
            hpad[si, 0:HALO, :] = h_prev[:, si * sw:(si + 1) * sw]
            hpad[si, HALO:HALO + ts, :] = h_cur[:, si * sw:(si + 1) * sw]
        pos = (i % tps) * ts + lax.broadcasted_iota(jnp.int32, (ts, 1), 0)
        for gi, win in enumerate(POOL_WINDOWS):
            sl = slice(gi * cg, (gi + 1) * cg)
            cnt = jnp.minimum(pos + 1, win).astype(F32)
            for si in range(gi * cg // sw, (gi + 1) * cg // sw):
                own = hpad[si, HALO:HALO + ts, :]
                acc = own
                for j in range(1, win):
                    acc = acc + hpad[si, pl.ds(HALO - j, ts), :]
                p_ref[:, si * sw:(si + 1) * sw] = (acc / cnt - own).astype(BF16)
            mixed = jnp.dot(p_ref[:, sl], w_ref[gi], preferred_element_type=F32)
            r_ref[:, sl] = cur_ref[:, sl] + s_ref[:, sl] * (mixed + b_ref[:, sl])
        h_ref[...] = _rms(r_ref[...], ng_ref[...]).astype(BF16)

    row = pl.BlockSpec((ts, d), lambda i: (i, 0))
    vec = pl.BlockSpec((1, d), lambda i: (0, 0))
    return pl.pallas_call(
        body, name=name, grid=(t // ts,),
        in_specs=[row, pl.BlockSpec((HALO, d), lambda i: (jnp.maximum(i * hb - 1, 0), 0)), vec,
                  pl.BlockSpec((N_GROUPS, cg, cg), lambda i: (0, 0, 0)), vec, vec, vec],
        out_specs=[row, row, row],
        out_shape=[jax.ShapeDtypeStruct((t, d), BF16), jax.ShapeDtypeStruct((t, d), F32),
                   jax.ShapeDtypeStruct((t, d), BF16)],
        scratch_shapes=[pltpu.VMEM((d // sw, HALO + ts, sw), F32)],
        compiler_params=_params("parallel"),
    )(x, x, gain, wp, scale, bias, next_gain)


def _pool_mix_bwd(name, pooled, wp, dr, x, gain, scale, bias, seq):
    t, d = x.shape
    ts = _tile(seq, 256)
    tps = seq // ts
    hb = ts // HALO
    nhb = t // HALO
    cg = d // N_GROUPS
    sw = min(cg, LANE)

    def body(p_ref, w_ref, dr_ref, ndr_ref, x_ref, g_ref, s_ref, b_ref, dx_ref, dw_ref, ds_ref, db_ref, dg_ref,
             qpad, dh):
        i = pl.program_id(0)
        last = (i % tps) == tps - 1
        pos = (i % tps) * ts + lax.broadcasted_iota(jnp.int32, (ts, 1), 0)

        @pl.when(i == 0)
        def _():
            dw_ref[...] = jnp.zeros_like(dw_ref)
            ds_ref[...] = jnp.zeros_like(ds_ref)
            db_ref[...] = jnp.zeros_like(db_ref)
            dg_ref[...] = jnp.zeros_like(dg_ref)

        for gi, win in enumerate(POOL_WINDOWS):
            sl = slice(gi * cg, (gi + 1) * cg)
            wv = w_ref[gi]
            sc = s_ref[:, sl]
            drv = dr_ref[:, sl]
            dmx = drv * sc
            dmx16 = dmx.astype(BF16)
            pooled = p_ref[:, sl]
            dw_ref[gi] += lax.dot_general(pooled, dmx16, (TN, ((), ())), preferred_element_type=F32)
            mixed = jnp.dot(pooled, wv, preferred_element_type=F32)
            ds_ref[:, sl] += jnp.sum(drv * (mixed + b_ref[:, sl]), axis=0, keepdims=True)
            db_ref[:, sl] += jnp.sum(dmx, axis=0, keepdims=True)
            cur = lax.dot_general(dmx16, wv, (NT, ((), ())), preferred_element_type=F32)
            nxt = lax.dot_general((ndr_ref[:, sl] * sc).astype(BF16), wv, (NT, ((), ())),
                                  preferred_element_type=F32)
            q_cur = cur / jnp.minimum(pos + 1, win).astype(F32)
            q_nxt = jnp.where(last, 0.0, nxt / float(win))
            for k, si in enumerate(range(gi * cg // sw, (gi + 1) * cg // sw)):
                part = slice(k * sw, (k + 1) * sw)
                qpad[si, 0:ts, :] = q_cur[:, part]
                qpad[si, ts:ts + HALO, :] = q_nxt[:, part]
                acc = -cur[:, part]
                for j in range(win):
                    acc = acc + qpad[si, pl.ds(j, ts), :]
                dh[:, si * sw:(si + 1) * sw] = acc
        dx, dgain_term, _ = _rms_bwd_tile(dh[...], x_ref[...], g_ref[...], dr_ref[...])
        dx_ref[...] = dx
        dg_ref[...] += jnp.sum(dgain_term, axis=0, keepdims=True)

    row = pl.BlockSpec((ts, d), lambda i: (i, 0))
    vec = pl.BlockSpec((1, d), lambda i: (0, 0))
    return pl.pallas_call(
        body, name=name, grid=(t // ts,),
        in_specs=[row, pl.BlockSpec((N_GROUPS, cg, cg), lambda i: (0, 0, 0)), row,
                  pl.BlockSpec((HALO, d), lambda i: (jnp.minimum((i + 1) * hb, nhb - 1), 0)), row, vec, vec, vec],
        out_specs=[row, pl.BlockSpec((N_GROUPS, cg, cg), lambda i: (0, 0, 0)), vec, vec, vec],
        out_shape=[jax.ShapeDtypeStruct((t, d), F32), jax.ShapeDtypeStruct((N_GROUPS, cg, cg), F32)]
        + [jax.ShapeDtypeStruct((1, d), F32)] * 3,
        scratch_shapes=[pltpu.VMEM((d // sw, ts + HALO, sw), F32), pltpu.VMEM((ts, d), F32)],
        compiler_params=_params("arbitrary"),
    )(pooled, wp, dr, dr, x, gain, scale, bias)


def _ctile(n, pref):
    return max(c for c in range(LANE, min(pref, n) + 1, LANE) if n % c == 0)


FFN_COLS = 1408
FFN_ROWS = 32
FFN_TILE = 1024


def _ffn_fwd(name, up, w, b, seq):
    _, t, dff = up.shape
    f = _ctile(dff, FFN_COLS)
    k_taps = w.shape[0]
    ts = _tile(seq, FFN_TILE)
    tps = seq // ts
    hb = ts // HALO16
    rc = _tile(ts, FFN_ROWS)

    def body(cur_ref, prev_ref, w_ref, b_ref, g_ref, apad):
        i = pl.program_id(1)
        first = (i % tps) == 0
        for ci, c0 in enumerate(range(0, f, LANE)):
            cols = slice(c0, c0 + LANE)
            apad[ci, 0:HALO16, :] = jnp.where(first, 0.0, prev_ref[:, cols].astype(F32))
            apad[ci, HALO16:HALO16 + ts, :] = cur_ref[0, :, cols].astype(F32)
            wv = w_ref[:, cols]
            wk = [jnp.broadcast_to(wv[k:k + 1, :], (rc, LANE)) for k in range(k_taps)]
            bias = jnp.broadcast_to(b_ref[:, cols], (rc, LANE))
            for r0 in range(0, ts, rc):
                c = bias
                for k in range(k_taps):
                    c = c + wk[k] * apad[ci, pl.ds(HALO16 - (k_taps - 1) + k + r0, rc), :]
                gate = cur_ref[1, pl.ds(r0, rc), cols].astype(F32)
                g_ref[pl.ds(r0, rc), cols] = (c * _sigmoid(c) * gate).astype(BF16)

    return pl.pallas_call(
        body, name=name, grid=(dff // f, t // ts),
        in_specs=[pl.BlockSpec((2, ts, f), lambda j, i: (0, i, j)),
                  pl.BlockSpec((None, HALO16, f), lambda j, i: (0, jnp.maximum(i * hb - 1, 0), j)),
                  pl.BlockSpec((k_taps, f), lambda j, i: (0, j)),
                  pl.BlockSpec((1, f), lambda j, i: (0, j))],
        out_specs=pl.BlockSpec((ts, f), lambda j, i: (i, j)),
        out_shape=jax.ShapeDtypeStruct((t, dff), BF16),
        scratch_shapes=[pltpu.VMEM((f // LANE, HALO16 + ts, LANE), F32)],
        compiler_params=_params("parallel", "parallel"),
    )(up, up, w, b)


def _ffn_bwd(name, up, dg, w, b, seq):
    _, t, dff = up.shape
    f = _ctile(dff, FFN_COLS)
    k_taps = w.shape[0]
    ts = _tile(seq, FFN_TILE)
    tps = seq // ts
    hb = ts // HALO16
    nhb = t // HALO16
    ext = ts + HALO16
    rc = _tile(ts, FFN_ROWS)

    def body(cur_ref, prev_ref, next_ref, dg_ref, ndg_ref, w_ref, b_ref, dup_ref, dw_ref, db_ref, apad, dcpad):
        i = pl.program_id(1)
        first = (i % tps) == 0
        last = (i % tps) == tps - 1

        @pl.when(i == 0)
        def _():
            dw_ref[...] = jnp.zeros_like(dw_ref)
            db_ref[...] = jnp.zeros_like(db_ref)

        for ci, c0 in enumerate(range(0, f, LANE)):
            cols = slice(c0, c0 + LANE)
            apad[ci, 0:HALO16, :] = jnp.where(first, 0.0, prev_ref[:, cols].astype(F32))
            apad[ci, HALO16:HALO16 + ts, :] = cur_ref[0, :, cols].astype(F32)
            apad[ci, HALO16 + ts:HALO16 + ext, :] = next_ref[0, :, cols].astype(F32)
            wv = w_ref[:, cols]
            wk = [jnp.broadcast_to(wv[k:k + 1, :], (rc, LANE)) for k in range(k_taps)]
            bias = jnp.broadcast_to(b_ref[:, cols], (rc, LANE))

            def conv_grad(r0, n, gate, dgv):
                c = bias[0:n]
                for k in range(k_taps):
                    c = c + wk[k][0:n] * apad[ci, pl.ds(HALO16 - (k_taps - 1) + k + r0, n), :]
                sig = _sigmoid(c)
                silu = c * sig
                return dgv * gate * (sig + silu * (1.0 - sig)), silu

            for r0 in range(0, ts, rc):
                dgv = dg_ref[pl.ds(r0, rc), cols].astype(F32)
                dc, silu = conv_grad(r0, rc, cur_ref[1, pl.ds(r0, rc), cols].astype(F32), dgv)
                dcpad[ci, pl.ds(r0, rc), :] = dc
                dup_ref[1, pl.ds(r0, rc), cols] = (dgv * silu).astype(BF16)
            dgv = jnp.where(last, 0.0, ndg_ref[:, cols].astype(F32))
            dc, _ = conv_grad(ts, HALO16, next_ref[1, :, cols].astype(F32), dgv)
            dcpad[ci, ts:ext, :] = dc

            dw_acc = [jnp.zeros((rc, LANE), F32) for _ in range(k_taps)]
            db_acc = jnp.zeros((rc, LANE), F32)
            for r0 in range(0, ts, rc):
                dact = jnp.zeros((rc, LANE), F32)
                for k in range(k_taps):
                    dact = dact + wk[k] * dcpad[ci, pl.ds(r0 + (k_taps - 1) - k, rc), :]
                dup_ref[0, pl.ds(r0, rc), cols] = dact.astype(BF16)
                dc = dcpad[ci, pl.ds(r0, rc), :]
                for k in range(k_taps):
                    dw_acc[k] = dw_acc[k] + dc * apad[ci, pl.ds(HALO16 - (k_taps - 1) + k + r0, rc), :]
                db_acc = db_acc + dc
            for k in range(k_taps):
                dw_ref[k:k + 1, cols] += jnp.sum(dw_acc[k], axis=0, keepdims=True)
            db_ref[:, cols] += jnp.sum(db_acc, axis=0, keepdims=True)

    return pl.pallas_call(
        body, name=name, grid=(dff // f, t // ts),
        in_specs=[pl.BlockSpec((2, ts, f), lambda j, i: (0, i, j)),
                  pl.BlockSpec((None, HALO16, f), lambda j, i: (0, jnp.maximum(i * hb - 1, 0), j)),
                  pl.BlockSpec((2, HALO16, f), lambda j, i: (0, jnp.minimum((i + 1) * hb, nhb - 1), j)),
                  pl.BlockSpec((ts, f), lambda j, i: (i, j)),
                  pl.BlockSpec((HALO16, f), lambda j, i: (jnp.minimum((i + 1) * hb, nhb - 1), j)),
                  pl.BlockSpec((k_taps, f), lambda j, i: (0, j)),
                  pl.BlockSpec((1, f), lambda j, i: (0, j))],
        out_specs=[pl.BlockSpec((2, ts, f), lambda j, i: (0, i, j)),
                   pl.BlockSpec((k_taps, f), lambda j, i: (0, j)),
                   pl.BlockSpec((1, f), lambda j, i: (0, j))],
        out_shape=[jax.ShapeDtypeStruct((2, t, dff), BF16), jax.ShapeDtypeStruct((k_taps, dff), F32),
                   jax.ShapeDtypeStruct((1, dff), F32)],
        scratch_shapes=[pltpu.VMEM((f // LANE, HALO16 + ext, LANE), F32), pltpu.VMEM((f // LANE, ext, LANE), F32)],
        compiler_params=_params("parallel", "arbitrary"),
    )(up, up, up, dg, dg, w, b)


def _sum_rows(name, g):
    ns, r, c = g.shape
    tr = _tile(r, 256)

    def body(g_ref, o_ref):
        acc = g_ref[0]
        for dev in range(1, ns):
            acc = acc + g_ref[dev]
        o_ref[...] = acc

    return pl.pallas_call(
        body, name=name, grid=(r // tr,),
        in_specs=[pl.BlockSpec((ns, tr, c), lambda i: (0, i, 0))],
        out_specs=pl.BlockSpec((tr, c), lambda i: (i, 0)),
        out_shape=jax.ShapeDtypeStruct((r, c), F32), compiler_params=_params("parallel"),
    )(g)


def _adamw(name, gsrc, w, m, v, layer=0, prev=None):
    ns, r, c = gsrc.shape
    nl = w.shape[0]
    tr = _tile(r, 256)
    prev = () if prev is None else tuple(prev)

    def body(g_ref, w_ref, m_ref, v_ref, *rest):
        go_ref, do_ref, mo_ref, vo_ref = rest[len(prev):]
        g = g_ref[0].astype(F32)
        for dev in range(1, ns):
            g = g + g_ref[dev].astype(F32)
        m_new = ADAM_B1 * m_ref[...] + (1.0 - ADAM_B1) * g
        v_new = ADAM_B2 * v_ref[...] + (1.0 - ADAM_B2) * (g * g)
        m_hat = m_new / (1.0 - ADAM_B1 ** ADAM_STEP)
        v_hat = v_new / (1.0 - ADAM_B2 ** ADAM_STEP)
        go_ref[...] = g
        do_ref[...] = -ADAM_LR * (m_hat / (jnp.sqrt(v_hat) + ADAM_EPS) + ADAM_WD * w_ref[...])
        mo_ref[...] = m_new
        vo_ref[...] = v_new

    row = pl.BlockSpec((None, tr, c), lambda i: (layer, i, 0))
    return pl.pallas_call(
        body, name=name, grid=(r // tr,),
        in_specs=[pl.BlockSpec((ns, tr, c), lambda i: (0, i, 0)), row, row, row] + [ANY] * len(prev),
        out_specs=[row] * 4, out_shape=[jax.ShapeDtypeStruct((nl, r, c), F32)] * 4,
        input_output_aliases={4 + i: i for i in range(len(prev))},
        compiler_params=_params("parallel"),
    )(gsrc, w, m, v, *prev)


def _ffn_forward(tag, r_in, h, get_wu, get_wd, wdw, bdw, seq, loss=None):
    t, d = r_in.shape
    tm = _tile(t, 512)
    wu = get_wu(h)
    dff = wu.shape[0] // 2
    tu = _tile(t, 1024)
    up = _mm(f"{tag}_up", h, wu, grid=(2, t // tu, 1),
             a_spec=pl.BlockSpec((tu, d), lambda j, i, k: (i, 0)),
             b_spec=pl.BlockSpec((dff, d), lambda j, i, k: (j, 0)),
             out_spec=pl.BlockSpec((None, tu, dff), lambda j, i, k: (j, i, 0)),
             out_shape=jax.ShapeDtypeStruct((2, t, dff), BF16), dims=NT, acc_shape=(tu, dff))
    wd = get_wd(up)
    g = _ffn_fwd(f"{tag}_act", up, wdw, bdw, seq)
    row = pl.BlockSpec((tm, d), lambda i, j, k: (i, 0))
    vec = pl.BlockSpec((1, d), lambda i, j, k: (0, 0))
    common = dict(grid=(t // tm, 1, 1), a_spec=pl.BlockSpec((tm, dff), lambda i, j, k: (i, 0)),
                  b_spec=pl.BlockSpec((dff, d), lambda i, j, k: (0, 0)), dims=NN, acc_shape=(tm, d))
    if loss is None:
        out = _mm(f"{tag}_down", g, wd, out_spec=row, out_shape=jax.ShapeDtypeStruct((t, d), F32),
                  extras=(r_in,), extra_specs=(row,), epilogue=lambda acc, ex, rows: ((ex[0][rows, :] + acc,), ()),
                  **common)
    else:
        def head(acc, ex, rows):
            dx, part, dgain = _loss_tile(ex[0][rows, :] + acc, ex[1][rows, :], ex[2][...])
            return (dx,), (part, dgain)

        out = _mm(f"{tag}_down", g, wd, out_spec=[row, pl.BlockSpec((1, 1), lambda i, j, k: (0, 0)), vec],
                  out_shape=[jax.ShapeDtypeStruct((t, d), F32), jax.ShapeDtypeStruct((1, 1), F32),
                             jax.ShapeDtypeStruct((1, d), F32)],
                  extras=(r_in, *loss), extra_specs=(row, row, vec), epilogue=head, n_sums=2, **common)
    return out, (r_in, h, up, g, wu, wd)


def _ffn_backward(tag, dr, saved, gain, wdw, bdw, seq, token=None, send_dwd=None):
    r_in, h, up, g, wu, wd = saved
    t, d = r_in.shape
    dff = wd.shape[0]
    tm = _tile(t, 512)
    tk = _tile(t, 2048)
    tku = _tile(t, 4096)
    cw = _ctile(dff, 1408)
    nc = dff // cw
    once = dict(pipeline_mode=pl.Buffered(1)) if tku == t else {}
    dg = _mm(f"{tag}_dg", dr, wd, grid=(t // tm, 1, 1),
             a_spec=pl.BlockSpec((tm, d), lambda i, j, k: (i, 0)),
             b_spec=pl.BlockSpec((dff, d), lambda i, j, k: (0, 0)),
             out_spec=pl.BlockSpec((tm, dff), lambda i, j, k: (i, 0)),
             out_shape=jax.ShapeDtypeStruct((t, dff), BF16), dims=NT, acc_shape=(tm, dff), token=token)
    dwd = _mm(f"{tag}_dwd", g, dr, grid=(dff // cw, 1, t // tk),
              a_spec=pl.BlockSpec((tk, cw), lambda i, j, k: (k, i)),
              b_spec=pl.BlockSpec((tk, d), lambda i, j, k: (k, 0)),
              out_spec=pl.BlockSpec((cw, d), lambda i, j, k: (i, 0)),
              out_shape=jax.ShapeDtypeStruct((dff, d), BF16), dims=TN, acc_shape=(cw, d))
    sent = None if send_dwd is None else send_dwd(dwd)
    dup, dwdw, dbdw = _ffn_bwd(f"{tag}_dact", up, dg, wdw, bdw, seq)
    row = pl.BlockSpec((tm, d), lambda i, j, k: (i, 0))
    vec = pl.BlockSpec((1, d), lambda i, j, k: (0, 0))

    def norm_backward(acc, ex, rows):
        dx, dgain, colsum = _rms_bwd_tile(acc, ex[0][rows, :], ex[1][...], ex[2][rows, :])
        return (dx,), (dgain, colsum)

    dr_in, dgain, colsum = _mm(
        f"{tag}_dh", dup, wu, grid=(t // tm, 1, 1),
        a_spec=pl.BlockSpec((2, tm, dff), lambda i, j, k: (0, i, 0)),
        b_spec=pl.BlockSpec((2 * dff, d), lambda i, j, k: (0, 0), pipeline_mode=pl.Buffered(1)),
        out_spec=[row, vec, vec],
        out_shape=[jax.ShapeDtypeStruct((t, d), F32)] + [jax.ShapeDtypeStruct((1, d), F32)] * 2,
        dims=NN, acc_shape=(tm, d), extras=(r_in, gain, dr), extra_specs=(row, vec, row),
        epilogue=norm_backward, n_sums=2, parts=2, token=sent)
    dwu = _mm(f"{tag}_dwu", dup, h, grid=(2 * nc, 1, t // tku),
              a_spec=pl.BlockSpec((None, tku, cw), lambda i, j, k: (i // nc, k, i % nc)),
              b_spec=pl.BlockSpec((tku, d), lambda i, j, k: (k, 0), **once),
              out_spec=pl.BlockSpec((cw, d), lambda i, j, k: (i, 0)),
              out_shape=jax.ShapeDtypeStruct((2 * dff, d), BF16), dims=TN, acc_shape=(cw, d))
    return dr_in, dgain, dwu, dwd, dwdw, dbdw, colsum


def _pad_to(vec, n):
    return jnp.pad(vec, (0, n - vec.shape[0]))


def _pack(parts, width):
    flat = jnp.concatenate([p.reshape(-1).astype(F32) for p in parts])
    n = -(-flat.shape[0] // (8 * width)) * (8 * width)
    return _pad_to(flat, n).reshape(n // width, width)


def _unpack(mat, shapes):
    flat = mat.reshape(-1)
    out, off = [], 0
    for s in shapes:
        n = 1
        for dim in s:
            n *= dim
        out.append(flat[off:off + n].reshape(s))
        off += n
    return out


def kernel(x, norm_mix, norm_ffn, conv_w_pw1, conv_b_pw1, conv_w_dw, conv_b_dw, conv_ln_g, conv_ln_b, conv_w_pw2, conv_b_pw2, pool_w, pool_b, pool_scale, ffn_w_up, ffn_w_dw, ffn_b_dw, ffn_w_down, final_norm, loss_target, m_norm_mix, m_norm_ffn, m_conv_w_pw1, m_conv_b_pw1, m_conv_w_dw, m_conv_b_dw, m_conv_ln_g, m_conv_ln_b, m_conv_w_pw2, m_conv_b_pw2, m_pool_w, m_pool_b, m_pool_scale, m_ffn_w_up, m_ffn_w_dw, m_ffn_b_dw, m_ffn_w_down, m_final_norm, v_norm_mix, v_norm_ffn, v_conv_w_pw1, v_conv_b_pw1, v_conv_w_dw, v_conv_b_dw, v_conv_ln_g, v_conv_ln_b, v_conv_w_pw2, v_conv_b_pw2, v_pool_w, v_pool_b, v_pool_scale, v_ffn_w_up, v_ffn_w_dw, v_ffn_b_dw, v_ffn_w_down, v_final_norm):
    bsz, seq, d = x.shape
    t = bsz * seq
    k_taps = conv_w_dw.shape[1]
    cs1 = conv_w_pw1.shape[2]
    dsh = d // N_DEV
    cg = d // N_GROUPS
    cgs = pool_w.shape[2]
    fu = ffn_w_up.shape[2]
    fd = ffn_w_down.shape[1]
    dff = fd * N_DEV
    nb = N_DEV // 2
    kf = ffn_w_dw.shape[1]
    fsh = ffn_w_dw.shape[2]
    my = _lin(_me())
    tm = _tile(t, 512)

    x2 = x.reshape(t, d)
    tgt2 = loss_target.reshape(t, d)

    small_shapes = [(k_taps, dsh), (dsh,), (dsh,), (2, kf, fsh)]
    small_mine = _pack([conv_w_dw[0], pool_b[0], pool_scale[0], ffn_w_dw], LANE)
    big = [conv_w_pw1[0], conv_w_pw2[0], ffn_w_up[0].T, ffn_w_down[0], pool_w[0], ffn_w_up[1].T, ffn_w_down[1]]
    gather = _exchange_start("gather_start", [small_mine] + [w.astype(BF16) for w in big], [GATHER2] * 8)
    h0 = _rms_fwd("l0_rms", x2, norm_mix[0:1])
    small_w = [norm_mix, norm_ffn, conv_b_pw1, conv_w_dw, conv_b_dw, conv_ln_g, conv_ln_b, conv_b_pw2, pool_b,
               pool_scale, ffn_w_dw, ffn_b_dw, final_norm]
    small_m = [m_norm_mix, m_norm_ffn, m_conv_b_pw1, m_conv_w_dw, m_conv_b_dw, m_conv_ln_g, m_conv_ln_b,
               m_conv_b_pw2, m_pool_b, m_pool_scale, m_ffn_w_dw, m_ffn_b_dw, m_final_norm]
    small_v = [v_norm_mix, v_norm_ffn, v_conv_b_pw1, v_conv_w_dw, v_conv_b_dw, v_conv_ln_g, v_conv_ln_b,
               v_conv_b_pw2, v_pool_b, v_pool_scale, v_ffn_w_dw, v_ffn_b_dw, v_final_norm]
    small_state = [_pack(group, 8 * LANE)[None] for group in (small_w, small_m, small_v)]
    forwarded = _exchange_forward("gather_forward_w1", gather, [0, 1], (h0, *small_state))
    small_all, w1 = _exchange_wait("gather_wait_w1", gather, [0, 1], forwarded)
    parts = [_unpack(small_all[dev], small_shapes) for dev in range(N_DEV)]
    wdw = jnp.concatenate([p[0] for p in parts], axis=1)
    pool_b_full = jnp.concatenate([p[1] for p in parts]).reshape(1, d)
    pool_s_full = jnp.concatenate([p[2] for p in parts]).reshape(1, d)
    fwdw = jnp.concatenate([p[3] for p in parts], axis=2)
    fbdw = ffn_b_dw.reshape(2, 1, dff)

    def columns(w):
        return w.transpose(1, 0, 2).reshape(w.shape[1], N_DEV * w.shape[2])

    def column_shards(w):
        return w.reshape(w.shape[0], N_DEV, w.shape[1] // N_DEV).transpose(1, 0, 2)

    w1 = columns(w1)
    a = _mm("l0_pw1", h0, w1, grid=(2, t // tm, 1),
            a_spec=pl.BlockSpec((tm, d), lambda j, i, k: (i, 0)),
            b_spec=pl.BlockSpec((d, d), lambda j, i, k: (0, j)),
            out_spec=pl.BlockSpec((None, tm, d), lambda j, i, k: (j, i, 0)),
            out_shape=jax.ShapeDtypeStruct((2, t, d), BF16), dims=NN, acc_shape=(tm, d),
            extras=(conv_b_pw1,), extra_specs=(pl.BlockSpec((1, d), lambda j, i, k: (0, j)),),
            epilogue=lambda acc, ex, rows: ((acc + ex[0][...],), ()))
    v = _conv_fwd("l0_conv", a, wdw, conv_b_dw, seq)
    forwarded = _exchange_forward("gather_forward_wu0", gather, [2, 3], v)
    (w2,) = _exchange_wait("gather_wait_w2", gather, [2], forwarded)
    w2 = w2.reshape(d, d)
    row = pl.BlockSpec((tm, d), lambda i, j, k: (i, 0))
    vec = pl.BlockSpec((1, d), lambda i, j, k: (0, 0))
    square = pl.BlockSpec((d, d), lambda i, j, k: (0, 0))

    def ln_silu(v_blk, ex):
        s_blk = _ln_silu_tile(v_blk, ex[0][...], ex[1][...]).astype(BF16)
        return s_blk, s_blk

    def residual_and_norm(acc, ex, rows):
        r_blk = ex[3][rows, :] + (acc + ex[2][...])
        return (r_blk, _rms(r_blk, ex[4][...])), ()

    r1, h1, s = _mm("l0_pw2", v, w2, grid=(t // tm, 1, 1), a_spec=row, b_spec=square, out_spec=[row, row, row],
                    out_shape=[jax.ShapeDtypeStruct((t, d), F32), jax.ShapeDtypeStruct((t, d), BF16),
                               jax.ShapeDtypeStruct((t, d), BF16)],
                    dims=NN, acc_shape=(tm, d), extras=(conv_ln_g, conv_ln_b, conv_b_pw2, x2, norm_ffn[0:1]),
                    extra_specs=(vec, vec, vec, row, vec), prologue=ln_silu, epilogue=residual_and_norm)

    def up_getter(name, idx):
        return lambda after: _exchange_wait(name, gather, [idx], after)[0].reshape(2 * dff, d)

    def down_getter(name, idx, forward=None):
        def get(after):
            if forward is not None:
                after = _exchange_forward(forward[0], gather, forward[1], after)
            return _exchange_wait(name, gather, [idx], after)[0].reshape(dff, d)
        return get

    r2, ffn0_saved = _ffn_forward("f0", r1, h1, up_getter("gather_wait_wu0", 3),
                                  down_getter("gather_wait_wd0", 4, ("gather_forward_wu1", [4, 5, 6])),
                                  fwdw[0], fbdw[0], seq)
    forwarded = _exchange_forward("gather_forward_wd1", gather, [7], r2)
    (wp,) = _exchange_wait("gather_wait_wp", gather, [5], forwarded)
    wp = wp.transpose(1, 0, 2, 3).reshape(N_GROUPS, cg, cg)
    pooled, r3, h3 = _pool_mix_fwd("l1_mix", r2, norm_mix[1:2], wp, pool_s_full, pool_b_full, norm_ffn[1:2], seq)
    (dr4, loss_part, dfinal), ffn1_saved = _ffn_forward(
        "f1", r3, h3, up_getter("gather_wait_wu1", 6), down_getter("gather_wait_wd1", 7), fwdw[1], fbdw[1], seq,
        loss=(tgt2, final_norm.reshape(1, d)))

    dr3, dnf1, dwu1, dwd1, dfw1, dfb1, _ = _ffn_backward("f1", dr4, ffn1_saved, norm_ffn[1:2], fwdw[1], fbdw[1], seq)
    scatter_a = _exchange_start("scatter_f1_start", [dwu1.reshape(N_DEV, fu, d), dwd1.reshape(N_DEV, fd, d)],
                                [SCATTER, SCATTER])
    dr2, dwp, dpool_s, dpool_b, dnm1 = _pool_mix_bwd(
        "l1_dmix", pooled, wp, dr3, r2, norm_mix[1:2], pool_s_full + scatter_a["token"][0:1, 0:1], pool_b_full, seq)
    early = {}

    def send_dwd0(dwd):
        early["down"] = _exchange_start("scatter_f0_down_start", [dwd.reshape(N_DEV, fd, d)], [SCATTER])
        return early["down"]["token"]

    dr1, dnf0, dwu0, dwd0, dfw0, dfb0, db2 = _ffn_backward("f0", dr2, ffn0_saved, norm_ffn[0:1], fwdw[0], fbdw[0], seq,
                                                           send_dwd=send_dwd0)
    dwp_b = dwp.astype(BF16).reshape(N_GROUPS, N_DEV, cgs, cg).transpose(1, 0, 2, 3)
    tk = _tile(t, 2048)
    dw2 = _mm("l0_dw2", s, dr1, grid=(1, 1, t // tk),
              a_spec=pl.BlockSpec((tk, d), lambda i, j, k: (k, 0)),
              b_spec=pl.BlockSpec((tk, d), lambda i, j, k: (k, 0)),
              out_spec=pl.BlockSpec((d, d), lambda i, j, k: (0, 0)),
              out_shape=jax.ShapeDtypeStruct((d, d), BF16), dims=TN, acc_shape=(d, d))
    scatter_b = _exchange_start("scatter_f0_start", [dwu0.reshape(N_DEV, fu, d), dwp_b,
                                                     dw2.reshape(N_DEV, d // N_DEV, d)], [SCATTER] * 3)

    def ln_silu_backward(acc, ex, rows):
        dv_blk, dgain, dbias, colsum = _ln_silu_bwd_tile(acc, ex[0][rows, :], ex[1][...], ex[2][...])
        return (dv_blk,), (dgain, dbias, colsum)

    dv, dlg, dlb, dbdw = _mm("l0_ds", dr1, w2, grid=(t // tm, 1, 1), a_spec=row, b_spec=square,
                             out_spec=[row, vec, vec, vec],
                             out_shape=[jax.ShapeDtypeStruct((t, d), F32)] + [jax.ShapeDtypeStruct((1, d), F32)] * 3,
                             dims=NT, acc_shape=(tm, d), extras=(v, conv_ln_g, conv_ln_b), extra_specs=(row, vec, vec),
                             epilogue=ln_silu_backward, n_sums=3, token=scatter_b["token"])
    da, dwdw, db1 = _conv_bwd("l0_dconv", a, dv, wdw, seq)
    tk1 = _tile(t, 4096)
    once = dict(pipeline_mode=pl.Buffered(1)) if tk1 == t else {}
    dw1 = _mm("l0_dw1", h0, da, grid=(1, 2, t // tk1),
              a_spec=pl.BlockSpec((tk1, d), lambda i, j, k: (k, 0), **once),
              b_spec=pl.BlockSpec((None, tk1, d), lambda i, j, k: (j, k, 0)),
              out_spec=pl.BlockSpec((d, d), lambda i, j, k: (0, j)),
              out_shape=jax.ShapeDtypeStruct((d, 2 * d), BF16), dims=TN, acc_shape=(d, d))
    scatter_c = _exchange_start("scatter_l0_start", [column_shards(dw1)], [SCATTER])
    def norm_backward(acc, ex, rows):
        dx_blk, dgain, colsum = _rms_bwd_tile(acc, ex[0][rows, :], ex[1][...], ex[2][rows, :])
        return (dx_blk,), (dgain, colsum)

    dx, dnm0, _ = _mm("l0_dh", da, w1, grid=(t // tm, 1, 1),
                      a_spec=pl.BlockSpec((2, tm, d), lambda i, j, k: (0, i, 0)),
                      b_spec=pl.BlockSpec((d, 2 * d), lambda i, j, k: (0, 0), pipeline_mode=pl.Buffered(1)),
                      out_spec=[row, vec, vec],
                      out_shape=[jax.ShapeDtypeStruct((t, d), F32)] + [jax.ShapeDtypeStruct((1, d), F32)] * 2,
                      dims=NT, acc_shape=(tm, d), extras=(x2, norm_mix[0:1], dr1), extra_specs=(row, vec, row),
                      epilogue=norm_backward, n_sums=2, parts=2, token=scatter_c["token"])

    dffn_w = jnp.stack([dfw0, dfw1])
    dffn_b = jnp.stack([dfb0, dfb1]).reshape(2, dff)
    small_parts = [loss_part, jnp.concatenate([dnm0, dnm1]), jnp.concatenate([dnf0, dnf1]), db1, dwdw, dbdw, dlg, dlb,
                   db2, dpool_b, dpool_s, dffn_w, dffn_b, dfinal]
    small_part_shapes = [(1,), (2, d), (2, d), (1, 2 * d), (k_taps, d), (1, d), (1, d), (1, d), (1, d), (1, d), (1, d),
                         (2, kf, dff), (2, dff), (d,)]
    packed = _pack(small_parts, 8 * LANE)
    gather_small = _exchange_start("gather_small_start", [packed], [GATHER])

    def big_update(name, recv, w, m, v, layer=0, prev=None):
        shape = w.shape
        c = recv.shape[-1]
        rows = recv.size // (N_DEV * c)
        nl = w.size // (rows * c)
        outs = _adamw(name, recv.reshape(N_DEV, rows, c), w.reshape(nl, rows, c), m.reshape(nl, rows, c),
                      v.reshape(nl, rows, c), layer, prev)
        return outs, [o.reshape(shape) for o in outs]

    wu_t = [p.transpose(0, 2, 1) for p in (ffn_w_up, m_ffn_w_up, v_ffn_w_up)]
    g_wu1, g_wd1 = _exchange_wait("scatter_f1_wait", scatter_a, [0, 1], gather_small["token"])
    raw_wu, _ = big_update("adam_wu1", g_wu1, *wu_t, 1)
    raw_wd, _ = big_update("adam_wd1", g_wd1, ffn_w_down, m_ffn_w_down, v_ffn_w_down, 1)
    (g_wd0,) = _exchange_wait("scatter_f0_down_wait", early["down"], [0], raw_wd[0])
    g_wu0, g_wp, g_w2 = _exchange_wait("scatter_f0_wait", scatter_b, [0, 1, 2], g_wd0)
    _, u_wu = big_update("adam_wu0", g_wu0, *wu_t, 0, raw_wu)
    u_wu = [o.transpose(0, 2, 1) for o in u_wu]
    _, u_wd = big_update("adam_wd0", g_wd0, ffn_w_down, m_ffn_w_down, v_ffn_w_down, 0, raw_wd)
    _, u_wp = big_update("adam_wp", g_wp, pool_w, m_pool_w, v_pool_w)
    (g_w1,) = _exchange_wait("scatter_l0_wait", scatter_c, [0], u_wp[0])
    _, u_w1 = big_update("adam_w1", g_w1, conv_w_pw1, m_conv_w_pw1, v_conv_w_pw1)
    _, u_w2 = big_update("adam_w2", g_w2, conv_w_pw2, m_conv_w_pw2, v_conv_w_pw2)
    (all_small,) = _exchange_wait("gather_small_wait", gather_small, [0], u_w2[0])
    summed = _sum_rows("sum_small_grads", all_small)
    (loss_v, g_nm, g_nf, g_b1, g_wdw, g_bdw, g_lg, g_lb, g_b2, g_pb, g_ps, g_fw, g_fb,
     g_fin) = _unpack(summed, small_part_shapes)
    loss = loss_v[0]
    g_wdw_mine = lax.dynamic_slice_in_dim(g_wdw, my * dsh, dsh, axis=1)[None]
    g_pb_mine = lax.dynamic_slice_in_dim(g_pb, my * dsh, dsh, axis=1)
    g_ps_mine = lax.dynamic_slice_in_dim(g_ps, my * dsh, dsh, axis=1)
    g_fw_mine = lax.dynamic_slice_in_dim(g_fw, my * fsh, fsh, axis=2)

    small_g = [g_nm, g_nf, g_b1, g_wdw_mine, g_bdw, g_lg, g_lb, g_b2, g_pb_mine, g_ps_mine, g_fw_mine, g_fb, g_fin]
    shapes = [w.shape for w in small_w]
    outs = _adamw("adam_small", _pack(small_g, 8 * LANE)[None], *small_state)
    sg, sd, sm, sv = [_unpack(o, shapes) for o in outs]

    def leaf(kind):
        (nm, nf, b1, wdw_, bdw_, lg, lb, b2, pb, ps, fw, fb, fin) = (sg, sd, sm, sv)[kind]
        return [nm, nf, u_w1[kind], b1, wdw_, bdw_, lg, lb, u_w2[kind], b2, u_wp[kind], pb, ps, u_wu[kind], fw, fb,
                u_wd[kind], fin]

    return (loss, dx.reshape(bsz, seq, d), *leaf(0), *leaf(1), *leaf(2), *leaf(3))
```

```python
import functools

import jax
import jax.numpy as jnp
from jax import lax
from jax.experimental import pallas as pl
from jax.experimental.pallas import tpu as pltpu

F32 = jnp.float32
BF16 = jnp.bfloat16
MESH = pl.DeviceIdType.MESH
HBM = pl.BlockSpec(memory_space=pltpu.HBM)

N_DEV = 8
RMS_EPS = 1e-6
LN_EPS = 1e-5
POOL_WINDOWS = (2, 4, 8, 16)
N_GROUPS = len(POOL_WINDOWS)
ADAM_LR = 0.001
ADAM_B1 = 0.9
ADAM_B2 = 0.999
ADAM_EPS = 1e-08
ADAM_WD = 0.01
ADAM_STEP = 10

LANE = 128
HALO = 32
HALO16 = 16
VMEM_LIMIT = 56 * 1024 * 1024


def _params(*sem):
    return pltpu.CompilerParams(dimension_semantics=sem if sem else None, vmem_limit_bytes=VMEM_LIMIT)


def _tile(n, pref):
    for t in range(min(pref, n), 15, -1):
        if n % t == 0 and t % 16 == 0:
            return t
    return n


def _sigmoid(z):
    return 1.0 / (1.0 + jnp.exp(-z))


def _me():
    return lax.axis_index("x"), lax.axis_index("y"), lax.axis_index("c")


def _flip(pos, m):
    x, y, c = pos
    return ((1 - x) if m & 4 else x, (1 - y) if m & 2 else y, (1 - c) if m & 1 else c)


def _lin(pos):
    return 4 * pos[0] + 2 * pos[1] + pos[2]


SEM = pl.BlockSpec(memory_space=pltpu.SEMAPHORE)
ANY = pl.BlockSpec(memory_space=pl.ANY)
EFFECT = pltpu.SideEffectType.DATAFLOW_SIDE_EFFECTING


SCATTER = "scatter"
GATHER = "gather"
GATHER2 = "gather2"
ALL_MASKS = (1, 2, 3, 4, 5, 6, 7)
SIBLING = 1
CHIPS = (2, 4, 6)


class _Copies:
    def __init__(self, a, mode, src, land, send_sems, recv_sems):
        self.a, self.mode, self.src, self.land = a, mode, src, land
        self.send_sems, self.recv_sems = send_sems, recv_sems
        self.me = _me()
        self.first = (SIBLING,) + CHIPS if mode == GATHER2 else ALL_MASKS

    def _sems(self, m, to):
        return dict(send_sem=self.send_sems.at[self.a * N_DEV + m], recv_sem=self.recv_sems.at[self.a * N_DEV + m],
                    device_id=to, device_id_type=MESH)

    def _block(self, pid):
        return self.src.at[pid] if self.mode == SCATTER else self.src

    def local(self):
        my = _lin(self.me)
        return pltpu.make_async_copy(self._block(my), self.land.at[my], self.send_sems.at[self.a * N_DEV])

    def send(self, m):
        peer = _flip(self.me, m)
        return pltpu.make_async_remote_copy(src_ref=self._block(_lin(peer)), dst_ref=self.land.at[_lin(self.me)],
                                            **self._sems(m, peer))

    def arrival(self, m):
        rows = self.land.at[_lin(_flip(self.me, m))]
        return pltpu.make_async_remote_copy(src_ref=rows, dst_ref=rows, **self._sems(m, _flip(self.me, m)))

    def forward(self, m):
        rows = self.land.at[_lin(_flip(self.me, m))]
        return pltpu.make_async_remote_copy(src_ref=rows, dst_ref=rows, **self._sems(m | 1, _flip(self.me, SIBLING)))


def _exchange_start(name, arrs, modes):
    n = len(arrs)
    blocks = [a.shape[1:] if md == SCATTER else a.shape for a, md in zip(arrs, modes)]

    def body(*refs):
        srcs, lands = refs[:n], refs[n:2 * n]
        send_sems, recv_sems = refs[2 * n], refs[2 * n + 1]
        token = refs[-1]
        for a in range(n):
            cp = _Copies(a, modes[a], srcs[a], lands[a], send_sems, recv_sems)
            cp.local().start()
            for m in cp.first:
                cp.send(m).start()
        token[...] = jnp.zeros_like(token)

    lands = [lax.empty((N_DEV,) + tuple(b), a.dtype) for a, b in zip(arrs, blocks)]
    outs = pl.pallas_call(
        body, name=name,
        out_shape=(pltpu.SemaphoreType.DMA((n * N_DEV,)), pltpu.SemaphoreType.DMA((n * N_DEV,)),
                   *[pltpu.HBM(a.shape, a.dtype) for a in arrs], *[pltpu.HBM(l.shape, l.dtype) for l in lands],
                   jax.ShapeDtypeStruct((8, LANE), F32)),
        in_specs=[HBM] * (2 * n),
        out_specs=(SEM, SEM, *[HBM] * (2 * n), pl.BlockSpec(memory_space=pltpu.VMEM)),
        input_output_aliases={i: 2 + i for i in range(2 * n)},
        compiler_params=pltpu.CompilerParams(has_side_effects=EFFECT),
    )(*[pltpu.with_memory_space_constraint(a, pltpu.HBM) for a in arrs],
      *[pltpu.with_memory_space_constraint(l, pltpu.HBM) for l in lands])
    return dict(send=outs[0], recv=outs[1], srcs=list(outs[2:2 + n]), lands=list(outs[2 + n:2 + 2 * n]),
                modes=modes, token=outs[-1])


def _exchange_forward(name, handle, which, after):
    k = len(which)

    def half(wait):
        def body(*refs):
            lands = refs[:k]
            send_sems, recv_sems = refs[k], refs[k + 1]
            token = refs[-1]
            for pos, a in enumerate(which):
                cp = _Copies(a, GATHER2, None, lands[pos], send_sems, recv_sems)
                for m in CHIPS:
                    if wait:
                        cp.arrival(m).wait_recv()
                    else:
                        cp.forward(m).start()
            token[...] = jnp.zeros_like(token)
        return body

    def call(body, call_name, lands, after):
        after = tuple(after) if isinstance(after, (tuple, list)) else (after,)
        outs = pl.pallas_call(
            body, name=call_name,
            out_shape=(*[pltpu.HBM(x.shape, x.dtype) for x in lands], jax.ShapeDtypeStruct((8, LANE), F32)),
            in_specs=[HBM] * k + [SEM, SEM] + [ANY] * len(after),
            out_specs=(*[HBM] * k, pl.BlockSpec(memory_space=pltpu.VMEM)),
            input_output_aliases={i: i for i in range(k)},
            compiler_params=pltpu.CompilerParams(has_side_effects=EFFECT),
        )(*lands, handle["send"], handle["recv"], *after)
        return list(outs[:k]), outs[-1]

    lands, arrived = call(half(True), name + "_arrived", [handle["lands"][a] for a in which], after)
    lands, token = call(half(False), name, lands, arrived)
    for pos, a in enumerate(which):
        handle["lands"][a] = lands[pos]
    return token


def _exchange_wait(name, handle, which, after):
    k = len(which)
    modes = handle["modes"]

    def body(*refs):
        srcs, lands = refs[:k], refs[k:2 * k]
        send_sems, recv_sems = refs[2 * k], refs[2 * k + 1]
        for pos, a in enumerate(which):
            cp = _Copies(a, modes[a], srcs[pos], lands[pos], send_sems, recv_sems)
            cp.local().wait()
            for m in cp.first:
                cp.send(m).wait_send()
            if modes[a] == GATHER2:
                for m in CHIPS:
                    cp.forward(m).wait_send()
                arrivals = (SIBLING,) + tuple(m | 1 for m in CHIPS)
            else:
                arrivals = ALL_MASKS
            for m in arrivals:
                cp.arrival(m).wait_recv()

    srcs = [handle["srcs"][a] for a in which]
    lands = [handle["lands"][a] for a in which]
    outs = pl.pallas_call(
        body, name=name,
        out_shape=tuple(pltpu.HBM(x.shape, x.dtype) for x in srcs + lands),
        in_specs=[HBM] * (2 * k) + [SEM, SEM, ANY], out_specs=tuple([HBM] * (2 * k)),
        input_output_aliases={i: i for i in range(2 * k)},
        compiler_params=pltpu.CompilerParams(has_side_effects=EFFECT),
    )(*srcs, *lands, handle["send"], handle["recv"], after)
    for pos, a in enumerate(which):
        handle["srcs"][a], handle["lands"][a] = outs[pos], outs[k + pos]
    return list(outs[k:])


def _mm(name, a, b, *, grid, a_spec, b_spec, out_spec, out_shape, dims, acc_shape, extras=(), extra_specs=(),
        epilogue=None, token=None, prologue=None, n_sums=0, parts=1):
    nk = grid[2]
    ne = len(extras)
    deps = () if token is None else (token,)
    dep_specs = [pl.BlockSpec((8, LANE), lambda i, j, k: (0, 0))] * len(deps)
    n_out = len(out_shape) if isinstance(out_shape, (list, tuple)) else 1
    n_tiles = n_out - n_sums - (1 if prologue is not None else 0)

    def body(a_ref, b_ref, *rest):
        ex, o_refs, acc_ref = rest[:ne], rest[ne + len(deps):ne + len(deps) + n_out], rest[ne + len(deps) + n_out]
        k = pl.program_id(2)
        if parts == 1:
            a_blk, saved = a_ref[...], None
            if prologue is not None:
                a_blk, saved = prologue(a_blk, ex)
                o_refs[n_tiles][...] = saved
            part = lax.dot_general(a_blk.astype(BF16), b_ref[...].astype(BF16), (dims, ((), ())),
                                   preferred_element_type=F32)
        else:
            kb = b_ref.shape[dims[1][0]] // parts
            part = None
            for p in range(parts):
                b_blk = b_ref[p * kb:(p + 1) * kb, :] if dims[1][0] == 0 else b_ref[:, p * kb:(p + 1) * kb]
                term = lax.dot_general(a_ref[p].astype(BF16), b_blk.astype(BF16), (dims, ((), ())),
                                       preferred_element_type=F32)
                part = term if part is None else part + term
        sum_refs = o_refs[n_out - n_sums:]

        def add_sums(terms):
            @pl.when((pl.program_id(0) == 0) & (pl.program_id(1) == 0))
            def _():
                for o_ref in sum_refs:
                    o_ref[...] = jnp.zeros_like(o_ref)

            for o_ref, term in zip(sum_refs, terms):
                o_ref[...] += jnp.sum(term, axis=0, keepdims=True)

        def finish(r):
            tiles, terms = ((r,), ()) if epilogue is None else epilogue(r, ex, slice(None))
            for o_ref, val in zip(o_refs, tiles):
                o_ref[...] = val.astype(o_ref.dtype)
            if n_sums:
                add_sums(terms)

        if nk == 1:
            finish(part)
            return

        @pl.when(k == 0)
        def _():
            acc_ref[...] = part

        @pl.when((k > 0) & (k < nk - 1))
        def _():
            acc_ref[...] += part

        @pl.when(k == nk - 1)
        def _():
            finish(acc_ref[...] + part)

    return pl.pallas_call(
        body, name=name, grid=grid, in_specs=[a_spec, b_spec, *extra_specs, *dep_specs], out_specs=out_spec,
        out_shape=out_shape, scratch_shapes=[pltpu.VMEM(acc_shape if nk > 1 else (8, LANE), F32)],
        compiler_params=_params(*(("arbitrary",) * 3 if n_sums else ("parallel", "parallel", "arbitrary"))),
    )(a, b, *extras, *deps)


def _rms(x, gain):
    return x * lax.rsqrt(jnp.mean(x * x, axis=-1, keepdims=True) + RMS_EPS) * gain


def _rms_bwd_tile(dh, x, gain, dres):
    rstd = lax.rsqrt(jnp.mean(x * x, axis=-1, keepdims=True) + RMS_EPS)
    xhat = x * rstd
    dxhat = dh * gain
    dx = dres + rstd * (dxhat - xhat * jnp.mean(dxhat * xhat, axis=-1, keepdims=True))
    return dx, dh * xhat, dx


def _ln_silu_tile(v, g, b):
    mu = jnp.mean(v, axis=-1, keepdims=True)
    cen = v - mu
    z = cen * lax.rsqrt(jnp.mean(cen * cen, axis=-1, keepdims=True) + LN_EPS) * g + b
    return z * _sigmoid(z)


def _ln_silu_bwd_tile(ds, v, g, b):
    mu = jnp.mean(v, axis=-1, keepdims=True)
    cen = v - mu
    rstd = lax.rsqrt(jnp.mean(cen * cen, axis=-1, keepdims=True) + LN_EPS)
    y = cen * rstd
    z = y * g + b
    sig = _sigmoid(z)
    dz = ds * sig * (1.0 + z * (1.0 - sig))
    dy = dz * g
    dv = rstd * (dy - jnp.mean(dy, axis=-1, keepdims=True) - y * jnp.mean(dy * y, axis=-1, keepdims=True))
    return dv, dz * y, dz, dv


def _loss_tile(x, tgt, gain):
    d = x.shape[-1]
    rstd = lax.rsqrt(jnp.mean(x * x, axis=-1, keepdims=True) + RMS_EPS)
    xhat = x * rstd
    err = xhat * gain - tgt
    dy = err / d
    dxhat = dy * gain
    dx = rstd * (dxhat - xhat * jnp.mean(dxhat * xhat, axis=-1, keepdims=True))
    return dx, 0.5 * jnp.mean(err * err, axis=-1, keepdims=True), dy * xhat


NN = ((1,), (0,))
NT = ((1,), (1,))
TN = ((0,), (0,))


def _rms_fwd(name, x, gain):
    t, d = x.shape
    tr = _tile(t, 512)

    def body(x_ref, g_ref, h_ref):
        h_ref[...] = _rms(x_ref[...], g_ref[...]).astype(BF16)

    return pl.pallas_call(
        body, name=name, grid=(t // tr,),
        in_specs=[pl.BlockSpec((tr, d), lambda i: (i, 0)), pl.BlockSpec((1, d), lambda i: (0, 0))],
        out_specs=pl.BlockSpec((tr, d), lambda i: (i, 0)),
        out_shape=jax.ShapeDtypeStruct((t, d), BF16), compiler_params=_params("parallel"),
    )(x, gain)


def _conv_tiles(t, seq):
    ts = _tile(seq, 1024)
    return ts, seq // ts, _tile(ts, 64)


def _conv_fwd(name, a, w, b, seq):
    _, t, d = a.shape
    k_taps = w.shape[0]
    ts, tps, rc = _conv_tiles(t, seq)
    hb = ts // HALO

    def body(cur_ref, prev_ref, w_ref, b_ref, v_ref, upad):
        i = pl.program_id(1)
        first = (i % tps) == 0
        pv = prev_ref[0].astype(F32)
        pg = prev_ref[1].astype(F32)
        upad[0:HALO, :] = jnp.where(first, 0.0, pv * _sigmoid(pg))
        upad[HALO:HALO + ts, :] = cur_ref[0].astype(F32) * _sigmoid(cur_ref[1].astype(F32))
        wv = w_ref[...]
        bias = jnp.broadcast_to(b_ref[...], (rc, LANE))
        for r0 in range(0, ts, rc):
            acc = bias
            for k in range(k_taps):
                acc = acc + wv[k:k + 1, :] * upad[pl.ds(HALO - (k_taps - 1) + k + r0, rc), :]
            v_ref[pl.ds(r0, rc), :] = acc

    return pl.pallas_call(
        body, name=name, grid=(d // LANE, t // ts),
        in_specs=[pl.BlockSpec((2, ts, LANE), lambda c, i: (0, i, c)),
                  pl.BlockSpec((2, HALO, LANE), lambda c, i: (0, jnp.maximum(i * hb - 1, 0), c)),
                  pl.BlockSpec((k_taps, LANE), lambda c, i: (0, c)),
                  pl.BlockSpec((1, LANE), lambda c, i: (0, c))],
        out_specs=pl.BlockSpec((ts, LANE), lambda c, i: (i, c)),
        out_shape=jax.ShapeDtypeStruct((t, d), F32),
        scratch_shapes=[pltpu.VMEM((HALO + ts, LANE), F32)],
        compiler_params=_params("parallel", "parallel"),
    )(a, a, w, b)


def _conv_bwd(name, a, dv, w, seq):
    _, t, d = a.shape
    k_taps = w.shape[0]
    ts, tps, rc = _conv_tiles(t, seq)
    hb = ts // HALO
    nhb = t // HALO

    def body(cur_ref, prev_ref, dv_ref, ndv_ref, w_ref, da_ref, dw_ref, dbp_ref, upad, dvpad, dwrows):
        i = pl.program_id(1)
        first = (i % tps) == 0
        last = (i % tps) == tps - 1
        pv = prev_ref[0].astype(F32)
        pg = prev_ref[1].astype(F32)
        upad[0:HALO, :] = jnp.where(first, 0.0, pv * _sigmoid(pg))
        upad[HALO:HALO + ts, :] = cur_ref[0].astype(F32) * _sigmoid(cur_ref[1].astype(F32))
        dvpad[0:ts, :] = dv_ref[...]
        dvpad[ts:ts + HALO, :] = jnp.where(last, 0.0, ndv_ref[...])
        wv = w_ref[...]

        @pl.when(i == 0)
        def _():
            dw_ref[...] = jnp.zeros_like(dw_ref)
            dbp_ref[...] = jnp.zeros_like(dbp_ref)

        sv = jnp.zeros((1, LANE), F32)
        sg = jnp.zeros((1, LANE), F32)
        for r0 in range(0, ts, rc):
            du = jnp.zeros((rc, LANE), F32)
            for k in range(k_taps):
                du = du + wv[k:k + 1, :] * dvpad[pl.ds(r0 + (k_taps - 1) - k, rc), :]
            av = cur_ref[0, pl.ds(r0, rc), :].astype(F32)
            sig = _sigmoid(cur_ref[1, pl.ds(r0, rc), :].astype(F32))
            dval = du * sig
            dgate = du * av * sig * (1.0 - sig)
            da_ref[0, pl.ds(r0, rc), :] = dval.astype(BF16)
            da_ref[1, pl.ds(r0, rc), :] = dgate.astype(BF16)
            sv = sv + jnp.sum(dval, axis=0, keepdims=True)
            sg = sg + jnp.sum(dgate, axis=0, keepdims=True)
        dbp_ref[0] += sv
        dbp_ref[1] += sg

        for k in range(k_taps):
            acc = jnp.zeros((rc, LANE), F32)
            for r0 in range(0, ts, rc):
                acc = acc + dvpad[pl.ds(r0, rc), :] * upad[pl.ds(HALO - (k_taps - 1) + k + r0, rc), :]
            dwrows[k:k + 1, :] = jnp.sum(acc, axis=0, keepdims=True)
        dw_ref[...] += dwrows[0:k_taps, :]

    return pl.pallas_call(
        body, name=name, grid=(d // LANE, t // ts),
        in_specs=[pl.BlockSpec((2, ts, LANE), lambda c, i: (0, i, c)),
                  pl.BlockSpec((2, HALO, LANE), lambda c, i: (0, jnp.maximum(i * hb - 1, 0), c)),
                  pl.BlockSpec((ts, LANE), lambda c, i: (i, c)),
                  pl.BlockSpec((HALO, LANE), lambda c, i: (jnp.minimum((i + 1) * hb, nhb - 1), c)),
                  pl.BlockSpec((k_taps, LANE), lambda c, i: (0, c))],
        out_specs=[pl.BlockSpec((2, ts, LANE), lambda c, i: (0, i, c)),
                   pl.BlockSpec((k_taps, LANE), lambda c, i: (0, c)),
                   pl.BlockSpec((2, 1, LANE), lambda c, i: (0, 0, c))],
        out_shape=[jax.ShapeDtypeStruct((2, t, d), BF16), jax.ShapeDtypeStruct((k_taps, d), F32),
                   jax.ShapeDtypeStruct((2, 1, d), F32)],
        scratch_shapes=[pltpu.VMEM((HALO + ts, LANE), F32), pltpu.VMEM((ts + HALO, LANE), F32),
                        pltpu.VMEM((HALO, LANE), F32)],
        compiler_params=_params("parallel", "arbitrary"),
    )(a, a, dv, dv, w)


def _pool_mix_fwd(name, x, gain, wp, scale, bias, next_gain, seq):
    t, d = x.shape
    ts = _tile(seq, 256)
    tps = seq // ts
    hb = ts // HALO
    cg = d // N_GROUPS
    sw = min(cg, LANE)

    def body(cur_ref, prev_ref, g_ref, w_ref, s_ref, b_ref, ng_ref, p_ref, r_ref, h_ref, hpad):
        i = pl.program_id(0)
        first = (i % tps) == 0
        g = g_ref[...]
        h_prev = jnp.where(first, 0.0, _rms(prev_ref[...], g))
        h_cur = _rms(cur_ref[...], g)
        for si in range(d // sw):
            hpad[si, 0:HALO, :] = h_prev[:, si * sw:(si + 1) * sw]
            hpad[si, HALO:HALO + ts, :] = h_cur[:, si * sw:(si + 1) * sw]
        pos = (i % tps) * ts + lax.broadcasted_iota(jnp.int32, (ts, 1), 0)
        for gi, win in enumerate(POOL_WINDOWS):
            sl = slice(gi * cg, (gi + 1) * cg)
            cnt = jnp.minimum(pos + 1, win).astype(F32)
            for si in range(gi * cg // sw, (gi + 1) * cg // sw):
                own = hpad[si, HALO:HALO + ts, :]
                acc = own
                for j in range(1, win):
                    acc = acc + hpad[si, pl.ds(HALO - j, ts), :]
                p_ref[:, si * sw:(si + 1) * sw] = (acc / cnt - own).astype(BF16)
            mixed = jnp.dot(p_ref[:, sl], w_ref[gi], preferred_element_type=F32)
            r_ref[:, sl] = cur_ref[:, sl] + s_ref[:, sl] * (mixed + b_ref[:, sl])
        h_ref[...] = _rms(r_ref[...], ng_ref[...]).astype(BF16)

    row = pl.BlockSpec((ts, d), lambda i: (i, 0))
    vec = pl.BlockSpec((1, d), lambda i: (0, 0))
    return pl.pallas_call(
        body, name=name, grid=(t // ts,),
        in_specs=[row, pl.BlockSpec((HALO, d), lambda i: (jnp.maximum(i * hb - 1, 0), 0)), vec,
                  pl.BlockSpec((N_GROUPS, cg, cg), lambda i: (0, 0, 0)), vec, vec, vec],
        out_specs=[row, row, row],
        out_shape=[jax.ShapeDtypeStruct((t, d), BF16), jax.ShapeDtypeStruct((t, d), F32),
                   jax.ShapeDtypeStruct((t, d), BF16)],
        scratch_shapes=[pltpu.VMEM((d // sw, HALO + ts, sw), F32)],
        compiler_params=_params("parallel"),
    )(x, x, gain, wp, scale, bias, next_gain)


def _pool_mix_bwd(name, pooled, wp, dr, x, gain, scale, bias, seq):
    t, d = x.shape
    ts = _tile(seq, 256)
    tps = seq // ts
    hb = ts // HALO
    nhb = t // HALO
    cg = d // N_GROUPS
    sw = min(cg, LANE)

    def body(p_ref, w_ref, dr_ref, ndr_ref, x_ref, g_ref, s_ref, b_ref, dx_ref, dw_ref, ds_ref, db_ref, dg_ref,
             qpad, dh):
        i = pl.program_id(0)
        last = (i % tps) == tps - 1
        pos = (i % tps) * ts + lax.broadcasted_iota(jnp.int32, (ts, 1), 0)

        @pl.when(i == 0)
        def _():
            dw_ref[...] = jnp.zeros_like(dw_ref)
            ds_ref[...] = jnp.zeros_like(ds_ref)
            db_ref[...] = jnp.zeros_like(db_ref)
            dg_ref[...] = jnp.zeros_like(dg_ref)

        for gi, win in enumerate(POOL_WINDOWS):
            sl = slice(gi * cg, (gi + 1) * cg)
            wv = w_ref[gi]
            sc = s_ref[:, sl]
            drv = dr_ref[:, sl]
            dmx = drv * sc
            dmx16 = dmx.astype(BF16)
            pooled = p_ref[:, sl]
            dw_ref[gi] += lax.dot_general(pooled, dmx16, (TN, ((), ())), preferred_element_type=F32)
            mixed = jnp.dot(pooled, wv, preferred_element_type=F32)
            ds_ref[:, sl] += jnp.sum(drv * (mixed + b_ref[:, sl]), axis=0, keepdims=True)
            db_ref[:, sl] += jnp.sum(dmx, axis=0, keepdims=True)
            cur = lax.dot_general(dmx16, wv, (NT, ((), ())), preferred_element_type=F32)
            nxt = lax.dot_general((ndr_ref[:, sl] * sc).astype(BF16), wv, (NT, ((), ())),
                                  preferred_element_type=F32)
            q_cur = cur / jnp.minimum(pos + 1, win).astype(F32)
            q_nxt = jnp.where(last, 0.0, nxt / float(win))
            for k, si in enumerate(range(gi * cg // sw, (gi + 1) * cg // sw)):
                part = slice(k * sw, (k + 1) * sw)
                qpad[si, 0:ts, :] = q_cur[:, part]
                qpad[si, ts:ts + HALO, :] = q_nxt[:, part]
                acc = -cur[:, part]
                for j in range(win):
                    acc = acc + qpad[si, pl.ds(j, ts), :]
                dh[:, si * sw:(si + 1) * sw] = acc
        dx, dgain_term, _ = _rms_bwd_tile(dh[...], x_ref[...], g_ref[...], dr_ref[...])
        dx_ref[...] = dx
        dg_ref[...] += jnp.sum(dgain_term, axis=0, keepdims=True)

    row = pl.BlockSpec((ts, d), lambda i: (i, 0))
    vec = pl.BlockSpec((1, d), lambda i: (0, 0))
    return pl.pallas_call(
        body, name=name, grid=(t // ts,),
        in_specs=[row, pl.BlockSpec((N_GROUPS, cg, cg), lambda i: (0, 0, 0)), row,
                  pl.BlockSpec((HALO, d), lambda i: (jnp.minimum((i + 1) * hb, nhb - 1), 0)), row, vec, vec, vec],
        out_specs=[row, pl.BlockSpec((N_GROUPS, cg, cg), lambda i: (0, 0, 0)), vec, vec, vec],
        out_shape=[jax.ShapeDtypeStruct((t, d), F32), jax.ShapeDtypeStruct((N_GROUPS, cg, cg), F32)]
        + [jax.ShapeDtypeStruct((1, d), F32)] * 3,
        scratch_shapes=[pltpu.VMEM((d // sw, ts + HALO, sw), F32), pltpu.VMEM((ts, d), F32)],
        compiler_params=_params("arbitrary"),
    )(pooled, wp, dr, dr, x, gain, scale, bias)


def _ctile(n, pref):
    return max(c for c in range(LANE, min(pref, n) + 1, LANE) if n % c == 0)


FFN_COLS = 1408
FFN_ROWS = 32
FFN_TILE = 1024


def _ffn_fwd(name, up, w, b, seq):
    _, t, dff = up.shape
    f = _ctile(dff, FFN_COLS)
    k_taps = w.shape[0]
    ts = _tile(seq, FFN_TILE)
    tps = seq // ts
    hb = ts // HALO16
    rc = _tile(ts, FFN_ROWS)

    def body(cur_ref, prev_ref, w_ref, b_ref, g_ref, apad):
        i = pl.program_id(1)
        first = (i % tps) == 0
        for ci, c0 in enumerate(range(0, f, LANE)):
            cols = slice(c0, c0 + LANE)
            apad[ci, 0:HALO16, :] = jnp.where(first, 0.0, prev_ref[:, cols].astype(F32))
            apad[ci, HALO16:HALO16 + ts, :] = cur_ref[0, :, cols].astype(F32)
            wv = w_ref[:, cols]
            wk = [jnp.broadcast_to(wv[k:k + 1, :], (rc, LANE)) for k in range(k_taps)]
            bias = jnp.broadcast_to(b_ref[:, cols], (rc, LANE))
            for r0 in range(0, ts, rc):
                c = bias
                for k in range(k_taps):
                    c = c + wk[k] * apad[ci, pl.ds(HALO16 - (k_taps - 1) + k + r0, rc), :]
                gate = cur_ref[1, pl.ds(r0, rc), cols].astype(F32)
                g_ref[pl.ds(r0, rc), cols] = (c * _sigmoid(c) * gate).astype(BF16)

    return pl.pallas_call(
        body, name=name, grid=(dff // f, t // ts),
        in_specs=[pl.BlockSpec((2, ts, f), lambda j, i: (0, i, j)),
                  pl.BlockSpec((None, HALO16, f), lambda j, i: (0, jnp.maximum(i * hb - 1, 0), j)),
                  pl.BlockSpec((k_taps, f), lambda j, i: (0, j)),
                  pl.BlockSpec((1, f), lambda j, i: (0, j))],
        out_specs=pl.BlockSpec((ts, f), lambda j, i: (i, j)),
        out_shape=jax.ShapeDtypeStruct((t, dff), BF16),
        scratch_shapes=[pltpu.VMEM((f // LANE, HALO16 + ts, LANE), F32)],
        compiler_params=_params("parallel", "parallel"),
    )(up, up, w, b)


def _ffn_bwd(name, up, dg, w, b, seq):
    _, t, dff = up.shape
    f = _ctile(dff, FFN_COLS)
    k_taps = w.shape[0]
    ts = _tile(seq, FFN_TILE)
    tps = seq // ts
    hb = ts // HALO16
    nhb = t // HALO16
    ext = ts + HALO16
    rc = _tile(ts, FFN_ROWS)

    def body(cur_ref, prev_ref, next_ref, dg_ref, ndg_ref, w_ref, b_ref, dup_ref, dw_ref, db_ref, apad, dcpad):
        i = pl.program_id(1)
        first = (i % tps) == 0
        last = (i % tps) == tps - 1

        @pl.when(i == 0)
        def _():
            dw_ref[...] = jnp.zeros_like(dw_ref)
            db_ref[...] = jnp.zeros_like(db_ref)

        for ci, c0 in enumerate(range(0, f, LANE)):
            cols = slice(c0, c0 + LANE)
            apad[ci, 0:HALO16, :] = jnp.where(first, 0.0, prev_ref[:, cols].astype(F32))
            apad[ci, HALO16:HALO16 + ts, :] = cur_ref[0, :, cols].astype(F32)
            apad[ci, HALO16 + ts:HALO16 + ext, :] = next_ref[0, :, cols].astype(F32)
            wv = w_ref[:, cols]
            wk = [jnp.broadcast_to(wv[k:k + 1, :], (rc, LANE)) for k in range(k_taps)]
            bias = jnp.broadcast_to(b_ref[:, cols], (rc, LANE))

            def conv_grad(r0, n, gate, dgv):
                c = bias[0:n]
                for k in range(k_taps):
                    c = c + wk[k][0:n] * apad[ci, pl.ds(HALO16 - (k_taps - 1) + k + r0, n), :]
                sig = _sigmoid(c)
                silu = c * sig
                return dgv * gate * (sig + silu * (1.0 - sig)), silu

            for r0 in range(0, ts, rc):
                dgv = dg_ref[pl.ds(r0, rc), cols].astype(F32)
                dc, silu = conv_grad(r0, rc, cur_ref[1, pl.ds(r0, rc), cols].astype(F32), dgv)
                dcpad[ci, pl.ds(r0, rc), :] = dc
                dup_ref[1, pl.ds(r0, rc), cols] = (dgv * silu).astype(BF16)
            dgv = jnp.where(last, 0.0, ndg_ref[:, cols].astype(F32))
            dc, _ = conv_grad(ts, HALO16, next_ref[1, :, cols].astype(F32), dgv)
            dcpad[ci, ts:ext, :] = dc

            dw_acc = [jnp.zeros((rc, LANE), F32) for _ in range(k_taps)]
            db_acc = jnp.zeros((rc, LANE), F32)
            for r0 in range(0, ts, rc):
                dact = jnp.zeros((rc, LANE), F32)
                for k in range(k_taps):
                    dact = dact + wk[k] * dcpad[ci, pl.ds(r0 + (k_taps - 1) - k, rc), :]
                dup_ref[0, pl.ds(r0, rc), cols] = dact.astype(BF16)
                dc = dcpad[ci, pl.ds(r0, rc), :]
                for k in range(k_taps):
                    dw_acc[k] = dw_acc[k] + dc * apad[ci, pl.ds(HALO16 - (k_taps - 1) + k + r0, rc), :]
                db_acc = db_acc + dc
            for k in range(k_taps):
                dw_ref[k:k + 1, cols] += jnp.sum(dw_acc[k], axis=0, keepdims=True)
            db_ref[:, cols] += jnp.sum(db_acc, axis=0, keepdims=True)

    return pl.pallas_call(
        body, name=name, grid=(dff // f, t // ts),
        in_specs=[pl.BlockSpec((2, ts, f), lambda j, i: (0, i, j)),
                  pl.BlockSpec((None, HALO16, f), lambda j, i: (0, jnp.maximum(i * hb - 1, 0), j)),
                  pl.BlockSpec((2, HALO16, f), lambda j, i: (0, jnp.minimum((i + 1) * hb, nhb - 1), j)),
                  pl.BlockSpec((ts, f), lambda j, i: (i, j)),
                  pl.BlockSpec((HALO16, f), lambda j, i: (jnp.minimum((i + 1) * hb, nhb - 1), j)),
                  pl.BlockSpec((k_taps, f), lambda j, i: (0, j)),
                  pl.BlockSpec((1, f), lambda j, i: (0, j))],
        out_specs=[pl.BlockSpec((2, ts, f), lambda j, i: (0, i, j)),
                   pl.BlockSpec((k_taps, f), lambda j, i: (0, j)),
                   pl.BlockSpec((1, f), lambda j, i: (0, j))],
        out_shape=[jax.ShapeDtypeStruct((2, t, dff), BF16), jax.ShapeDtypeStruct((k_taps, dff), F32),
                   jax.ShapeDtypeStruct((1, dff), F32)],
        scratch_shapes=[pltpu.VMEM((f // LANE, HALO16 + ext, LANE), F32), pltpu.VMEM((f // LANE, ext, LANE), F32)],
        compiler_params=_params("parallel", "arbitrary"),
    )(up, up, up, dg, dg, w, b)


def _sum_rows(name, g):
    ns, r, c = g.shape
    tr = _tile(r, 256)

    def body(g_ref, o_ref):
        acc = g_ref[0]
        for dev in range(1, ns):
            acc = acc + g_ref[dev]
        o_ref[...] = acc

    return pl.pallas_call(
        body, name=name, grid=(r // tr,),
        in_specs=[pl.BlockSpec((ns, tr, c), lambda i: (0, i, 0))],
        out_specs=pl.BlockSpec((tr, c), lambda i: (i, 0)),
        out_shape=jax.ShapeDtypeStruct((r, c), F32), compiler_params=_params("parallel"),
    )(g)


def _adam_step(g, w, m, v):
    m_new = ADAM_B1 * m + (1.0 - ADAM_B1) * g
    v_new = ADAM_B2 * v + (1.0 - ADAM_B2) * (g * g)
    m_hat = m_new / (1.0 - ADAM_B1 ** ADAM_STEP)
    v_hat = v_new / (1.0 - ADAM_B2 ** ADAM_STEP)
    return -ADAM_LR * (m_hat / (jnp.sqrt(v_hat) + ADAM_EPS) + ADAM_WD * w), m_new, v_new


def _adamw_small(name, gs, ws, ms, vs):
    n = len(ws)

    def rows(a):
        return a.reshape(-1, a.shape[-1])

    def body(*refs):
        ins, outs = refs[:4 * n], refs[4 * n:]
        for p in range(n):
            delta, m_new, v_new = _adam_step(ins[p][...], ins[n + p][...], ins[2 * n + p][...], ins[3 * n + p][...])
            outs[p][...], outs[n + p][...], outs[2 * n + p][...] = delta, m_new, v_new

    whole = pl.BlockSpec(memory_space=pltpu.VMEM)
    outs = pl.pallas_call(
        body, name=name, in_specs=[whole] * (4 * n), out_specs=[whole] * (3 * n),
        out_shape=[jax.ShapeDtypeStruct(rows(w).shape, F32) for _ in range(3) for w in ws],
        compiler_params=_params(),
    )(*[rows(a) for a in (*gs, *ws, *ms, *vs)])
    return [[o.reshape(w.shape) for o, w in zip(outs[kind * n:(kind + 1) * n], ws)] for kind in range(3)]


def _adamw(name, gsrc, w, m, v, layer=0, prev=None):
    ns, r, c = gsrc.shape
    nl = w.shape[0]
    tr = _tile(r, 256)
    prev = () if prev is None else tuple(prev)

    def body(g_ref, w_ref, m_ref, v_ref, *rest):
        go_ref, do_ref, mo_ref, vo_ref = rest[len(prev):]
        g = g_ref[0].astype(F32)
        for dev in range(1, ns):
            g = g + g_ref[dev].astype(F32)
        go_ref[...] = g
        do_ref[...], mo_ref[...], vo_ref[...] = _adam_step(g, w_ref[...], m_ref[...], v_ref[...])

    row = pl.BlockSpec((None, tr, c), lambda i: (layer, i, 0))
    return pl.pallas_call(
        body, name=name, grid=(r // tr,),
        in_specs=[pl.BlockSpec((ns, tr, c), lambda i: (0, i, 0)), row, row, row] + [ANY] * len(prev),
        out_specs=[row] * 4, out_shape=[jax.ShapeDtypeStruct((nl, r, c), F32)] * 4,
        input_output_aliases={4 + i: i for i in range(len(prev))},
        compiler_params=_params("parallel"),
    )(gsrc, w, m, v, *prev)


def _ffn_forward(tag, r_in, h, get_wu, get_wd, wdw, bdw, seq, loss=None):
    t, d = r_in.shape
    tm = _tile(t, 512)
    wu = get_wu(h)
    dff = wu.shape[0] // 2
    tu = _tile(t, 1024)
    up = _mm(f"{tag}_up", h, wu, grid=(2, t // tu, 1),
             a_spec=pl.BlockSpec((tu, d), lambda j, i, k: (i, 0)),
             b_spec=pl.BlockSpec((dff, d), lambda j, i, k: (j, 0)),
             out_spec=pl.BlockSpec((None, tu, dff), lambda j, i, k: (j, i, 0)),
             out_shape=jax.ShapeDtypeStruct((2, t, dff), BF16), dims=NT, acc_shape=(tu, dff))
    wd = get_wd(up)
    g = _ffn_fwd(f"{tag}_act", up, wdw, bdw, seq)
    row = pl.BlockSpec((tm, d), lambda i, j, k: (i, 0))
    vec = pl.BlockSpec((1, d), lambda i, j, k: (0, 0))
    common = dict(grid=(t // tm, 1, 1), a_spec=pl.BlockSpec((tm, dff), lambda i, j, k: (i, 0)),
                  b_spec=pl.BlockSpec((dff, d), lambda i, j, k: (0, 0)), dims=NN, acc_shape=(tm, d))
    if loss is None:
        out = _mm(f"{tag}_down", g, wd, out_spec=row, out_shape=jax.ShapeDtypeStruct((t, d), F32),
                  extras=(r_in,), extra_specs=(row,), epilogue=lambda acc, ex, rows: ((ex[0][rows, :] + acc,), ()),
                  **common)
    else:
        def head(acc, ex, rows):
            dx, part, dgain = _loss_tile(ex[0][rows, :] + acc, ex[1][rows, :], ex[2][...])
            return (dx,), (part, dgain)

        out = _mm(f"{tag}_down", g, wd, out_spec=[row, pl.BlockSpec((1, 1), lambda i, j, k: (0, 0)), vec],
                  out_shape=[jax.ShapeDtypeStruct((t, d), F32), jax.ShapeDtypeStruct((1, 1), F32),
                             jax.ShapeDtypeStruct((1, d), F32)],
                  extras=(r_in, *loss), extra_specs=(row, row, vec), epilogue=head, n_sums=2, **common)
    return out, (r_in, h, up, g, wu, wd)


def _ffn_backward(tag, dr, saved, gain, wdw, bdw, seq, token=None, send_dwd=None):
    r_in, h, up, g, wu, wd = saved
    t, d = r_in.shape
    dff = wd.shape[0]
    tm = _tile(t, 512)
    tk = _tile(t, 2048)
    tku = _tile(t, 4096)
    cw = _ctile(dff, 1408)
    nc = dff // cw
    once = dict(pipeline_mode=pl.Buffered(1)) if tku == t else {}
    dg = _mm(f"{tag}_dg", dr, wd, grid=(t // tm, 1, 1),
             a_spec=pl.BlockSpec((tm, d), lambda i, j, k: (i, 0)),
             b_spec=pl.BlockSpec((dff, d), lambda i, j, k: (0, 0)),
             out_spec=pl.BlockSpec((tm, dff), lambda i, j, k: (i, 0)),
             out_shape=jax.ShapeDtypeStruct((t, dff), BF16), dims=NT, acc_shape=(tm, dff), token=token)
    dwd = _mm(f"{tag}_dwd", g, dr, grid=(dff // cw, 1, t // tk),
              a_spec=pl.BlockSpec((tk, cw), lambda i, j, k: (k, i)),
              b_spec=pl.BlockSpec((tk, d), lambda i, j, k: (k, 0)),
              out_spec=pl.BlockSpec((cw, d), lambda i, j, k: (i, 0)),
              out_shape=jax.ShapeDtypeStruct((dff, d), BF16), dims=TN, acc_shape=(cw, d))
    sent = None if send_dwd is None else send_dwd(dwd)
    dup, dwdw, dbdw = _ffn_bwd(f"{tag}_dact", up, dg, wdw, bdw, seq)
    row = pl.BlockSpec((tm, d), lambda i, j, k: (i, 0))
    vec = pl.BlockSpec((1, d), lambda i, j, k: (0, 0))

    def norm_backward(acc, ex, rows):
        dx, dgain, colsum = _rms_bwd_tile(acc, ex[0][rows, :], ex[1][...], ex[2][rows, :])
        return (dx,), (dgain, colsum)

    dr_in, dgain, colsum = _mm(
        f"{tag}_dh", dup, wu, grid=(t // tm, 1, 1),
        a_spec=pl.BlockSpec((2, tm, dff), lambda i, j, k: (0, i, 0)),
        b_spec=pl.BlockSpec((2 * dff, d), lambda i, j, k: (0, 0), pipeline_mode=pl.Buffered(1)),
        out_spec=[row, vec, vec],
        out_shape=[jax.ShapeDtypeStruct((t, d), F32)] + [jax.ShapeDtypeStruct((1, d), F32)] * 2,
        dims=NN, acc_shape=(tm, d), extras=(r_in, gain, dr), extra_specs=(row, vec, row),
        epilogue=norm_backward, n_sums=2, parts=2, token=sent)
    dwu = _mm(f"{tag}_dwu", dup, h, grid=(2 * nc, 1, t // tku),
              a_spec=pl.BlockSpec((None, tku, cw), lambda i, j, k: (i // nc, k, i % nc)),
              b_spec=pl.BlockSpec((tku, d), lambda i, j, k: (k, 0), **once),
              out_spec=pl.BlockSpec((cw, d), lambda i, j, k: (i, 0)),
              out_shape=jax.ShapeDtypeStruct((2 * dff, d), BF16), dims=TN, acc_shape=(cw, d))
    return dr_in, dgain, dwu, dwd, dwdw, dbdw, colsum


def _pad_to(vec, n):
    return jnp.pad(vec, (0, n - vec.shape[0]))


def _pack(parts, width):
    flat = jnp.concatenate([p.reshape(-1).astype(F32) for p in parts])
    n = -(-flat.shape[0] // (8 * width)) * (8 * width)
    return _pad_to(flat, n).reshape(n // width, width)


def _unpack(mat, shapes):
    flat = mat.reshape(-1)
    out, off = [], 0
    for s in shapes:
        n = 1
        for dim in s:
            n *= dim
        out.append(flat[off:off + n].reshape(s))
        off += n
    return out


def kernel(x, norm_mix, norm_ffn, conv_w_pw1, conv_b_pw1, conv_w_dw, conv_b_dw, conv_ln_g, conv_ln_b, conv_w_pw2, conv_b_pw2, pool_w, pool_b, pool_scale, ffn_w_up, ffn_w_dw, ffn_b_dw, ffn_w_down, final_norm, loss_target, m_norm_mix, m_norm_ffn, m_conv_w_pw1, m_conv_b_pw1, m_conv_w_dw, m_conv_b_dw, m_conv_ln_g, m_conv_ln_b, m_conv_w_pw2, m_conv_b_pw2, m_pool_w, m_pool_b, m_pool_scale, m_ffn_w_up, m_ffn_w_dw, m_ffn_b_dw, m_ffn_w_down, m_final_norm, v_norm_mix, v_norm_ffn, v_conv_w_pw1, v_conv_b_pw1, v_conv_w_dw, v_conv_b_dw, v_conv_ln_g, v_conv_ln_b, v_conv_w_pw2, v_conv_b_pw2, v_pool_w, v_pool_b, v_pool_scale, v_ffn_w_up, v_ffn_w_dw, v_ffn_b_dw, v_ffn_w_down, v_final_norm):
    bsz, seq, d = x.shape
    t = bsz * seq
    k_taps = conv_w_dw.shape[1]
    cs1 = conv_w_pw1.shape[2]
    dsh = d // N_DEV
    cg = d // N_GROUPS
    cgs = pool_w.shape[2]
    fu = ffn_w_up.shape[2]
    fd = ffn_w_down.shape[1]
    dff = fd * N_DEV
    nb = N_DEV // 2
    kf = ffn_w_dw.shape[1]
    fsh = ffn_w_dw.shape[2]
    my = _lin(_me())
    tm = _tile(t, 512)

    x2 = x.reshape(t, d)
    tgt2 = loss_target.reshape(t, d)

    small_shapes = [(k_taps, dsh), (dsh,), (dsh,), (2, kf, fsh)]
    small_mine = _pack([conv_w_dw[0], pool_b[0], pool_scale[0], ffn_w_dw], LANE)
    big = [conv_w_pw1[0], conv_w_pw2[0], ffn_w_up[0].T, ffn_w_down[0], pool_w[0], ffn_w_up[1].T, ffn_w_down[1]]
    gather = _exchange_start("gather_start", [small_mine] + [w.astype(BF16) for w in big], [GATHER2] * 8)
    h0 = _rms_fwd("l0_rms", x2, norm_mix[0:1])
    small_w = [norm_mix, norm_ffn, conv_b_pw1, conv_w_dw, conv_b_dw, conv_ln_g, conv_ln_b, conv_b_pw2, pool_b,
               pool_scale, ffn_w_dw, ffn_b_dw, final_norm]
    small_m = [m_norm_mix, m_norm_ffn, m_conv_b_pw1, m_conv_w_dw, m_conv_b_dw, m_conv_ln_g, m_conv_ln_b,
               m_conv_b_pw2, m_pool_b, m_pool_scale, m_ffn_w_dw, m_ffn_b_dw, m_final_norm]
    small_v = [v_norm_mix, v_norm_ffn, v_conv_b_pw1, v_conv_w_dw, v_conv_b_dw, v_conv_ln_g, v_conv_ln_b,
               v_conv_b_pw2, v_pool_b, v_pool_scale, v_ffn_w_dw, v_ffn_b_dw, v_final_norm]
    forwarded = _exchange_forward("gather_forward_w1", gather, [0, 1], h0)
    small_all, w1 = _exchange_wait("gather_wait_w1", gather, [0, 1], forwarded)
    parts = [_unpack(small_all[dev], small_shapes) for dev in range(N_DEV)]
    wdw = jnp.concatenate([p[0] for p in parts], axis=1)
    pool_b_full = jnp.concatenate([p[1] for p in parts]).reshape(1, d)
    pool_s_full = jnp.concatenate([p[2] for p in parts]).reshape(1, d)
    fwdw = jnp.concatenate([p[3] for p in parts], axis=2)
    fbdw = ffn_b_dw.reshape(2, 1, dff)

    def columns(w):
        return w.transpose(1, 0, 2).reshape(w.shape[1], N_DEV * w.shape[2])

    def column_shards(w):
        return w.reshape(w.shape[0], N_DEV, w.shape[1] // N_DEV).transpose(1, 0, 2)

    w1 = columns(w1)
    a = _mm("l0_pw1", h0, w1, grid=(2, t // tm, 1),
            a_spec=pl.BlockSpec((tm, d), lambda j, i, k: (i, 0)),
            b_spec=pl.BlockSpec((d, d), lambda j, i, k: (0, j)),
            out_spec=pl.BlockSpec((None, tm, d), lambda j, i, k: (j, i, 0)),
            out_shape=jax.ShapeDtypeStruct((2, t, d), BF16), dims=NN, acc_shape=(tm, d),
            extras=(conv_b_pw1,), extra_specs=(pl.BlockSpec((1, d), lambda j, i, k: (0, j)),),
            epilogue=lambda acc, ex, rows: ((acc + ex[0][...],), ()))
    v = _conv_fwd("l0_conv", a, wdw, conv_b_dw, seq)
    forwarded = _exchange_forward("gather_forward_wu0", gather, [2, 3], v)
    (w2,) = _exchange_wait("gather_wait_w2", gather, [2], forwarded)
    w2 = w2.reshape(d, d)
    row = pl.BlockSpec((tm, d), lambda i, j, k: (i, 0))
    vec = pl.BlockSpec((1, d), lambda i, j, k: (0, 0))
    square = pl.BlockSpec((d, d), lambda i, j, k: (0, 0))

    def ln_silu(v_blk, ex):
        s_blk = _ln_silu_tile(v_blk, ex[0][...], ex[1][...]).astype(BF16)
        return s_blk, s_blk

    def residual_and_norm(acc, ex, rows):
        r_blk = ex[3][rows, :] + (acc + ex[2][...])
        return (r_blk, _rms(r_blk, ex[4][...])), ()

    r1, h1, s = _mm("l0_pw2", v, w2, grid=(t // tm, 1, 1), a_spec=row, b_spec=square, out_spec=[row, row, row],
                    out_shape=[jax.ShapeDtypeStruct((t, d), F32), jax.ShapeDtypeStruct((t, d), BF16),
                               jax.ShapeDtypeStruct((t, d), BF16)],
                    dims=NN, acc_shape=(tm, d), extras=(conv_ln_g, conv_ln_b, conv_b_pw2, x2, norm_ffn[0:1]),
                    extra_specs=(vec, vec, vec, row, vec), prologue=ln_silu, epilogue=residual_and_norm)

    def up_getter(name, idx):
        return lambda after: _exchange_wait(name, gather, [idx], after)[0].reshape(2 * dff, d)

    def down_getter(name, idx, forward=None):
        def get(after):
            if forward is not None:
                after = _exchange_forward(forward[0], gather, forward[1], after)
            return _exchange_wait(name, gather, [idx], after)[0].reshape(dff, d)
        return get

    r2, ffn0_saved = _ffn_forward("f0", r1, h1, up_getter("gather_wait_wu0", 3),
                                  down_getter("gather_wait_wd0", 4, ("gather_forward_wu1", [4, 5, 6])),
                                  fwdw[0], fbdw[0], seq)
    forwarded = _exchange_forward("gather_forward_wd1", gather, [7], r2)
    (wp,) = _exchange_wait("gather_wait_wp", gather, [5], forwarded)
    wp = wp.transpose(1, 0, 2, 3).reshape(N_GROUPS, cg, cg)
    pooled, r3, h3 = _pool_mix_fwd("l1_mix", r2, norm_mix[1:2], wp, pool_s_full, pool_b_full, norm_ffn[1:2], seq)
    (dr4, loss_part, dfinal), ffn1_saved = _ffn_forward(
        "f1", r3, h3, up_getter("gather_wait_wu1", 6), down_getter("gather_wait_wd1", 7), fwdw[1], fbdw[1], seq,
        loss=(tgt2, final_norm.reshape(1, d)))

    dr3, dnf1, dwu1, dwd1, dfw1, dfb1, _ = _ffn_backward("f1", dr4, ffn1_saved, norm_ffn[1:2], fwdw[1], fbdw[1], seq)
    scatter_a = _exchange_start("scatter_f1_start", [dwu1.reshape(N_DEV, fu, d), dwd1.reshape(N_DEV, fd, d)],
                                [SCATTER, SCATTER])
    dr2, dwp, dpool_s, dpool_b, dnm1 = _pool_mix_bwd(
        "l1_dmix", pooled, wp, dr3, r2, norm_mix[1:2], pool_s_full + scatter_a["token"][0:1, 0:1], pool_b_full, seq)
    early = {}

    def send_dwd0(dwd):
        early["down"] = _exchange_start("scatter_f0_down_start", [dwd.reshape(N_DEV, fd, d)], [SCATTER])
        return early["down"]["token"]

    dr1, dnf0, dwu0, dwd0, dfw0, dfb0, db2 = _ffn_backward("f0", dr2, ffn0_saved, norm_ffn[0:1], fwdw[0], fbdw[0], seq,
                                                           send_dwd=send_dwd0)
    dwp_b = dwp.astype(BF16).reshape(N_GROUPS, N_DEV, cgs, cg).transpose(1, 0, 2, 3)
    tk = _tile(t, 2048)
    dw2 = _mm("l0_dw2", s, dr1, grid=(1, 1, t // tk),
              a_spec=pl.BlockSpec((tk, d), lambda i, j, k: (k, 0)),
              b_spec=pl.BlockSpec((tk, d), lambda i, j, k: (k, 0)),
              out_spec=pl.BlockSpec((d, d), lambda i, j, k: (0, 0)),
              out_shape=jax.ShapeDtypeStruct((d, d), BF16), dims=TN, acc_shape=(d, d))
    scatter_b = _exchange_start("scatter_f0_start", [dwu0.reshape(N_DEV, fu, d), dwp_b,
                                                     dw2.reshape(N_DEV, d // N_DEV, d)], [SCATTER] * 3)

    def ln_silu_backward(acc, ex, rows):
        dv_blk, dgain, dbias, colsum = _ln_silu_bwd_tile(acc, ex[0][rows, :], ex[1][...], ex[2][...])
        return (dv_blk,), (dgain, dbias, colsum)

    dv, dlg, dlb, dbdw = _mm("l0_ds", dr1, w2, grid=(t // tm, 1, 1), a_spec=row, b_spec=square,
                             out_spec=[row, vec, vec, vec],
                             out_shape=[jax.ShapeDtypeStruct((t, d), F32)] + [jax.ShapeDtypeStruct((1, d), F32)] * 3,
                             dims=NT, acc_shape=(tm, d), extras=(v, conv_ln_g, conv_ln_b), extra_specs=(row, vec, vec),
                             epilogue=ln_silu_backward, n_sums=3, token=scatter_b["token"])
    da, dwdw, db1 = _conv_bwd("l0_dconv", a, dv, wdw, seq)
    tk1 = _tile(t, 4096)
    once = dict(pipeline_mode=pl.Buffered(1)) if tk1 == t else {}
    dw1 = _mm("l0_dw1", h0, da, grid=(1, 2, t // tk1),
              a_spec=pl.BlockSpec((tk1, d), lambda i, j, k: (k, 0), **once),
              b_spec=pl.BlockSpec((None, tk1, d), lambda i, j, k: (j, k, 0)),
              out_spec=pl.BlockSpec((d, d), lambda i, j, k: (0, j)),
              out_shape=jax.ShapeDtypeStruct((d, 2 * d), BF16), dims=TN, acc_shape=(d, d))
    scatter_c = _exchange_start("scatter_l0_start", [column_shards(dw1)], [SCATTER])
    def norm_backward(acc, ex, rows):
        dx_blk, dgain, colsum = _rms_bwd_tile(acc, ex[0][rows, :], ex[1][...], ex[2][rows, :])
        return (dx_blk,), (dgain, colsum)

    dx, dnm0, _ = _mm("l0_dh", da, w1, grid=(t // tm, 1, 1),
                      a_spec=pl.BlockSpec((2, tm, d), lambda i, j, k: (0, i, 0)),
                      b_spec=pl.BlockSpec((d, 2 * d), lambda i, j, k: (0, 0), pipeline_mode=pl.Buffered(1)),
                      out_spec=[row, vec, vec],
                      out_shape=[jax.ShapeDtypeStruct((t, d), F32)] + [jax.ShapeDtypeStruct((1, d), F32)] * 2,
                      dims=NT, acc_shape=(tm, d), extras=(x2, norm_mix[0:1], dr1), extra_specs=(row, vec, row),
                      epilogue=norm_backward, n_sums=2, parts=2, token=scatter_c["token"])

    dffn_w = jnp.stack([dfw0, dfw1])
    dffn_b = jnp.stack([dfb0, dfb1]).reshape(2, dff)
    small_parts = [loss_part, jnp.concatenate([dnm0, dnm1]), jnp.concatenate([dnf0, dnf1]), db1, dwdw, dbdw, dlg, dlb,
                   db2, dpool_b, dpool_s, dffn_w, dffn_b, dfinal]
    small_part_shapes = [(1,), (2, d), (2, d), (1, 2 * d), (k_taps, d), (1, d), (1, d), (1, d), (1, d), (1, d), (1, d),
                         (2, kf, dff), (2, dff), (d,)]
    packed = _pack(small_parts, 8 * LANE)
    gather_small = _exchange_start("gather_small_start", [packed], [GATHER])

    def big_update(name, recv, w, m, v, layer=0, prev=None):
        shape = w.shape
        c = recv.shape[-1]
        rows = recv.size // (N_DEV * c)
        nl = w.size // (rows * c)
        outs = _adamw(name, recv.reshape(N_DEV, rows, c), w.reshape(nl, rows, c), m.reshape(nl, rows, c),
                      v.reshape(nl, rows, c), layer, prev)
        return outs, [o.reshape(shape) for o in outs]

    wu_t = [p.transpose(0, 2, 1) for p in (ffn_w_up, m_ffn_w_up, v_ffn_w_up)]
    g_wu1, g_wd1 = _exchange_wait("scatter_f1_wait", scatter_a, [0, 1], gather_small["token"])
    raw_wu, _ = big_update("adam_wu1", g_wu1, *wu_t, 1)
    raw_wd, _ = big_update("adam_wd1", g_wd1, ffn_w_down, m_ffn_w_down, v_ffn_w_down, 1)
    (g_wd0,) = _exchange_wait("scatter_f0_down_wait", early["down"], [0], raw_wd[0])
    g_wu0, g_wp, g_w2 = _exchange_wait("scatter_f0_wait", scatter_b, [0, 1, 2], g_wd0)
    _, u_wu = big_update("adam_wu0", g_wu0, *wu_t, 0, raw_wu)
    u_wu = [o.transpose(0, 2, 1) for o in u_wu]
    _, u_wd = big_update("adam_wd0", g_wd0, ffn_w_down, m_ffn_w_down, v_ffn_w_down, 0, raw_wd)
    _, u_wp = big_update("adam_wp", g_wp, pool_w, m_pool_w, v_pool_w)
    (g_w1,) = _exchange_wait("scatter_l0_wait", scatter_c, [0], u_wp[0])
    _, u_w1 = big_update("adam_w1", g_w1, conv_w_pw1, m_conv_w_pw1, v_conv_w_pw1)
    _, u_w2 = big_update("adam_w2", g_w2, conv_w_pw2, m_conv_w_pw2, v_conv_w_pw2)
    (all_small,) = _exchange_wait("gather_small_wait", gather_small, [0], u_w2[0])
    summed = _sum_rows("sum_small_grads", all_small)
    (loss_v, g_nm, g_nf, g_b1, g_wdw, g_bdw, g_lg, g_lb, g_b2, g_pb, g_ps, g_fw, g_fb,
     g_fin) = _unpack(summed, small_part_shapes)
    loss = loss_v[0]
    g_wdw_mine = lax.dynamic_slice_in_dim(g_wdw, my * dsh, dsh, axis=1)[None]
    g_pb_mine = lax.dynamic_slice_in_dim(g_pb, my * dsh, dsh, axis=1)
    g_ps_mine = lax.dynamic_slice_in_dim(g_ps, my * dsh, dsh, axis=1)
    g_fw_mine = lax.dynamic_slice_in_dim(g_fw, my * fsh, fsh, axis=2)

    small_g = [g_nm, g_nf, g_b1, g_wdw_mine, g_bdw, g_lg, g_lb, g_b2, g_pb_mine, g_ps_mine, g_fw_mine, g_fb, g_fin]
    sg = [g.reshape(w.shape) for g, w in zip(small_g, small_w)]
    sd, sm, sv = _adamw_small("adam_small", sg, small_w, small_m, small_v)

    def leaf(kind):
        (nm, nf, b1, wdw_, bdw_, lg, lb, b2, pb, ps, fw, fb, fin) = (sg, sd, sm, sv)[kind]
        return [nm, nf, u_w1[kind], b1, wdw_, bdw_, lg, lb, u_w2[kind], b2, u_wp[kind], pb, ps, u_wu[kind], fw, fb,
                u_wd[kind], fin]

    return (loss, dx.reshape(bsz, seq, d), *leaf(0), *leaf(1), *leaf(2), *leaf(3))
```

```python
import functools

import jax
import jax.numpy as jnp
from jax import lax
from jax.experimental import pallas as pl
from jax.experimental.pallas import tpu as pltpu

F32 = jnp.float32
BF16 = jnp.bfloat16
MESH = pl.DeviceIdType.MESH
HBM = pl.BlockSpec(memory_space=pltpu.HBM)

N_DEV = 8
RMS_EPS = 1e-6
LN_EPS = 1e-5
POOL_WINDOWS = (2, 4, 8, 16)
N_GROUPS = len(POOL_WINDOWS)
ADAM_LR = 0.001
ADAM_B1 = 0.9
ADAM_B2 = 0.999
ADAM_EPS = 1e-08
ADAM_WD = 0.01
ADAM_STEP = 10

LANE = 128
HALO = 32
HALO16 = 16
VMEM_LIMIT = 56 * 1024 * 1024


def _params(*sem):
    return pltpu.CompilerParams(dimension_semantics=sem if sem else None, vmem_limit_bytes=VMEM_LIMIT)


def _tile(n, pref):
    for t in range(min(pref, n), 15, -1):
        if n % t == 0 and t % 16 == 0:
            return t
    return n


def _sigmoid(z):
    return 1.0 / (1.0 + jnp.exp(-z))


def _me():
    return lax.axis_index("x"), lax.axis_index("y"), lax.axis_index("c")


def _flip(pos, m):
    x, y, c = pos
    return ((1 - x) if m & 4 else x, (1 - y) if m & 2 else y, (1 - c) if m & 1 else c)


def _lin(pos):
    return 4 * pos[0] + 2 * pos[1] + pos[2]


SEM = pl.BlockSpec(memory_space=pltpu.SEMAPHORE)
ANY = pl.BlockSpec(memory_space=pl.ANY)
EFFECT = pltpu.SideEffectType.DATAFLOW_SIDE_EFFECTING


SCATTER = "scatter"
GATHER = "gather"
GATHER2 = "gather2"
ALL_MASKS = (1, 2, 3, 4, 5, 6, 7)
SIBLING = 1
CHIPS = (2, 4, 6)


class _Copies:
    def __init__(self, a, mode, src, land, send_sems, recv_sems):
        self.a, self.mode, self.src, self.land = a, mode, src, land
        self.send_sems, self.recv_sems = send_sems, recv_sems
        self.me = _me()
        self.first = (SIBLING,) + CHIPS if mode == GATHER2 else ALL_MASKS

    def _sems(self, m, to):
        return dict(send_sem=self.send_sems.at[self.a * N_DEV + m], recv_sem=self.recv_sems.at[self.a * N_DEV + m],
                    device_id=to, device_id_type=MESH)

    def _block(self, pid):
        return self.src.at[pid] if self.mode == SCATTER else self.src

    def local(self):
        my = _lin(self.me)
        return pltpu.make_async_copy(self._block(my), self.land.at[my], self.send_sems.at[self.a * N_DEV])

    def send(self, m):
        peer = _flip(self.me, m)
        return pltpu.make_async_remote_copy(src_ref=self._block(_lin(peer)), dst_ref=self.land.at[_lin(self.me)],
                                            **self._sems(m, peer))

    def arrival(self, m):
        rows = self.land.at[_lin(_flip(self.me, m))]
        return pltpu.make_async_remote_copy(src_ref=rows, dst_ref=rows, **self._sems(m, _flip(self.me, m)))

    def forward(self, m):
        rows = self.land.at[_lin(_flip(self.me, m))]
        return pltpu.make_async_remote_copy(src_ref=rows, dst_ref=rows, **self._sems(m | 1, _flip(self.me, SIBLING)))


def _exchange_start(name, arrs, modes):
    n = len(arrs)
    blocks = [a.shape[1:] if md == SCATTER else a.shape for a, md in zip(arrs, modes)]

    def body(*refs):
        srcs, lands = refs[:n], refs[n:2 * n]
        send_sems, recv_sems = refs[2 * n], refs[2 * n + 1]
        token = refs[-1]
        for a in range(n):
            cp = _Copies(a, modes[a], srcs[a], lands[a], send_sems, recv_sems)
            cp.local().start()
            for m in cp.first:
                cp.send(m).start()
        token[...] = jnp.zeros_like(token)

    lands = [lax.empty((N_DEV,) + tuple(b), a.dtype) for a, b in zip(arrs, blocks)]
    outs = pl.pallas_call(
        body, name=name,
        out_shape=(pltpu.SemaphoreType.DMA((n * N_DEV,)), pltpu.SemaphoreType.DMA((n * N_DEV,)),
                   *[pltpu.HBM(a.shape, a.dtype) for a in arrs], *[pltpu.HBM(l.shape, l.dtype) for l in lands],
                   jax.ShapeDtypeStruct((8, LANE), F32)),
        in_specs=[HBM] * (2 * n),
        out_specs=(SEM, SEM, *[HBM] * (2 * n), pl.BlockSpec(memory_space=pltpu.VMEM)),
        input_output_aliases={i: 2 + i for i in range(2 * n)},
        compiler_params=pltpu.CompilerParams(has_side_effects=EFFECT),
    )(*[pltpu.with_memory_space_constraint(a, pltpu.HBM) for a in arrs],
      *[pltpu.with_memory_space_constraint(l, pltpu.HBM) for l in lands])
    return dict(send=outs[0], recv=outs[1], srcs=list(outs[2:2 + n]), lands=list(outs[2 + n:2 + 2 * n]),
                modes=modes, token=outs[-1])


def _exchange_forward(name, handle, which, after):
    k = len(which)

    def half(wait):
        def body(*refs):
            lands = refs[:k]
            send_sems, recv_sems = refs[k], refs[k + 1]
            token = refs[-1]
            for pos, a in enumerate(which):
                cp = _Copies(a, GATHER2, None, lands[pos], send_sems, recv_sems)
                for m in CHIPS:
                    if wait:
                        cp.arrival(m).wait_recv()
                    else:
                        cp.forward(m).start()
            token[...] = jnp.zeros_like(token)
        return body

    def call(body, call_name, lands, after):
        after = tuple(after) if isinstance(after, (tuple, list)) else (after,)
        outs = pl.pallas_call(
            body, name=call_name,
            out_shape=(*[pltpu.HBM(x.shape, x.dtype) for x in lands], jax.ShapeDtypeStruct((8, LANE), F32)),
            in_specs=[HBM] * k + [SEM, SEM] + [ANY] * len(after),
            out_specs=(*[HBM] * k, pl.BlockSpec(memory_space=pltpu.VMEM)),
            input_output_aliases={i: i for i in range(k)},
            compiler_params=pltpu.CompilerParams(has_side_effects=EFFECT),
        )(*lands, handle["send"], handle["recv"], *after)
        return list(outs[:k]), outs[-1]

    lands, arrived = call(half(True), name + "_arrived", [handle["lands"][a] for a in which], after)
    lands, token = call(half(False), name, lands, arrived)
    for pos, a in enumerate(which):
        handle["lands"][a] = lands[pos]
    return token


def _exchange_wait(name, handle, which, after):
    k = len(which)
    modes = handle["modes"]

    def body(*refs):
        srcs, lands = refs[:k], refs[k:2 * k]
        send_sems, recv_sems = refs[2 * k], refs[2 * k + 1]
        for pos, a in enumerate(which):
            cp = _Copies(a, modes[a], srcs[pos], lands[pos], send_sems, recv_sems)
            cp.local().wait()
            for m in cp.first:
                cp.send(m).wait_send()
            if modes[a] == GATHER2:
                for m in CHIPS:
                    cp.forward(m).wait_send()
                arrivals = (SIBLING,) + tuple(m | 1 for m in CHIPS)
            else:
                arrivals = ALL_MASKS
            for m in arrivals:
                cp.arrival(m).wait_recv()

    srcs = [handle["srcs"][a] for a in which]
    lands = [handle["lands"][a] for a in which]
    outs = pl.pallas_call(
        body, name=name,
        out_shape=tuple(pltpu.HBM(x.shape, x.dtype) for x in srcs + lands),
        in_specs=[HBM] * (2 * k) + [SEM, SEM, ANY], out_specs=tuple([HBM] * (2 * k)),
        input_output_aliases={i: i for i in range(2 * k)},
        compiler_params=pltpu.CompilerParams(has_side_effects=EFFECT),
    )(*srcs, *lands, handle["send"], handle["recv"], after)
    for pos, a in enumerate(which):
        handle["srcs"][a], handle["lands"][a] = outs[pos], outs[k + pos]
    return list(outs[k:])


def _mm(name, a, b, *, grid, a_spec, b_spec, out_spec, out_shape, dims, acc_shape, extras=(), extra_specs=(),
        epilogue=None, token=None, prologue=None, n_sums=0, parts=1):
    nk = grid[2]
    ne = len(extras)
    deps = () if token is None else (token,)
    dep_specs = [pl.BlockSpec((8, LANE), lambda i, j, k: (0, 0))] * len(deps)
    n_out = len(out_shape) if isinstance(out_shape, (list, tuple)) else 1
    n_tiles = n_out - n_sums - (1 if prologue is not None else 0)

    def body(a_ref, b_ref, *rest):
        ex, o_refs, acc_ref = rest[:ne], rest[ne + len(deps):ne + len(deps) + n_out], rest[ne + len(deps) + n_out]
        k = pl.program_id(2)
        if parts == 1:
            a_blk, saved = a_ref[...], None
            if prologue is not None:
                a_blk, saved = prologue(a_blk, ex)
                o_refs[n_tiles][...] = saved
            part = lax.dot_general(a_blk.astype(BF16), b_ref[...].astype(BF16), (dims, ((), ())),
                                   preferred_element_type=F32)
        else:
            kb = b_ref.shape[dims[1][0]] // parts
            part = None
            for p in range(parts):
                b_blk = b_ref[p * kb:(p + 1) * kb, :] if dims[1][0] == 0 else b_ref[:, p * kb:(p + 1) * kb]
                term = lax.dot_general(a_ref[p].astype(BF16), b_blk.astype(BF16), (dims, ((), ())),
                                       preferred_element_type=F32)
                part = term if part is None else part + term
        sum_refs = o_refs[n_out - n_sums:]

        def add_sums(terms):
            @pl.when((pl.program_id(0) == 0) & (pl.program_id(1) == 0))
            def _():
                for o_ref in sum_refs:
                    o_ref[...] = jnp.zeros_like(o_ref)

            for o_ref, term in zip(sum_refs, terms):
                o_ref[...] += jnp.sum(term, axis=0, keepdims=True)

        def finish(r):
            tiles, terms = ((r,), ()) if epilogue is None else epilogue(r, ex, slice(None))
            for o_ref, val in zip(o_refs, tiles):
                o_ref[...] = val.astype(o_ref.dtype)
            if n_sums:
                add_sums(terms)

        if nk == 1:
            finish(part)
            return

        @pl.when(k == 0)
        def _():
            acc_ref[...] = part

        @pl.when((k > 0) & (k < nk - 1))
        def _():
            acc_ref[...] += part

        @pl.when(k == nk - 1)
        def _():
            finish(acc_ref[...] + part)

    return pl.pallas_call(
        body, name=name, grid=grid, in_specs=[a_spec, b_spec, *extra_specs, *dep_specs], out_specs=out_spec,
        out_shape=out_shape, scratch_shapes=[pltpu.VMEM(acc_shape if nk > 1 else (8, LANE), F32)],
        compiler_params=_params(*(("arbitrary",) * 3 if n_sums else ("parallel", "parallel", "arbitrary"))),
    )(a, b, *extras, *deps)


def _rms(x, gain):
    return x * lax.rsqrt(jnp.mean(x * x, axis=-1, keepdims=True) + RMS_EPS) * gain


def _rms_bwd_tile(dh, x, gain, dres):
    rstd = lax.rsqrt(jnp.mean(x * x, axis=-1, keepdims=True) + RMS_EPS)
    xhat = x * rstd
    dxhat = dh * gain
    dx = dres + rstd * (dxhat - xhat * jnp.mean(dxhat * xhat, axis=-1, keepdims=True))
    return dx, dh * xhat, dx


def _ln_silu_tile(v, g, b):
    mu = jnp.mean(v, axis=-1, keepdims=True)
    cen = v - mu
    z = cen * lax.rsqrt(jnp.mean(cen * cen, axis=-1, keepdims=True) + LN_EPS) * g + b
    return z * _sigmoid(z)


def _ln_silu_bwd_tile(ds, v, g, b):
    mu = jnp.mean(v, axis=-1, keepdims=True)
    cen = v - mu
    rstd = lax.rsqrt(jnp.mean(cen * cen, axis=-1, keepdims=True) + LN_EPS)
    y = cen * rstd
    z = y * g + b
    sig = _sigmoid(z)
    dz = ds * sig * (1.0 + z * (1.0 - sig))
    dy = dz * g
    dv = rstd * (dy - jnp.mean(dy, axis=-1, keepdims=True) - y * jnp.mean(dy * y, axis=-1, keepdims=True))
    return dv, dz * y, dz, dv


def _loss_tile(x, tgt, gain):
    d = x.shape[-1]
    rstd = lax.rsqrt(jnp.mean(x * x, axis=-1, keepdims=True) + RMS_EPS)
    xhat = x * rstd
    err = xhat * gain - tgt
    dy = err / d
    dxhat = dy * gain
    dx = rstd * (dxhat - xhat * jnp.mean(dxhat * xhat, axis=-1, keepdims=True))
    return dx, 0.5 * jnp.mean(err * err, axis=-1, keepdims=True), dy * xhat


NN = ((1,), (0,))
NT = ((1,), (1,))
TN = ((0,), (0,))


def _rms_fwd(name, x, gain):
    t, d = x.shape
    tr = _tile(t, 512)

    def body(x_ref, g_ref, h_ref):
        h_ref[...] = _rms(x_ref[...], g_ref[...]).astype(BF16)

    return pl.pallas_call(
        body, name=name, grid=(t // tr,),
        in_specs=[pl.BlockSpec((tr, d), lambda i: (i, 0)), pl.BlockSpec((1, d), lambda i: (0, 0))],
        out_specs=pl.BlockSpec((tr, d), lambda i: (i, 0)),
        out_shape=jax.ShapeDtypeStruct((t, d), BF16), compiler_params=_params("parallel"),
    )(x, gain)


def _conv_tiles(t, seq):
    ts = _tile(seq, 1024)
    return ts, seq // ts, _tile(ts, 64)


def _conv_fwd(name, a, w, b, seq):
    _, t, d = a.shape
    k_taps = w.shape[0]
    ts, tps, rc = _conv_tiles(t, seq)
    hb = ts // HALO

    def body(cur_ref, prev_ref, w_ref, b_ref, v_ref, upad):
        i = pl.program_id(1)
        first = (i % tps) == 0
        pv = prev_ref[0].astype(F32)
        pg = prev_ref[1].astype(F32)
        upad[0:HALO, :] = jnp.where(first, 0.0, pv * _sigmoid(pg))
        upad[HALO:HALO + ts, :] = cur_ref[0].astype(F32) * _sigmoid(cur_ref[1].astype(F32))
        wv = w_ref[...]
        bias = jnp.broadcast_to(b_ref[...], (rc, LANE))
        for r0 in range(0, ts, rc):
            acc = bias
            for k in range(k_taps):
                acc = acc + wv[k:k + 1, :] * upad[pl.ds(HALO - (k_taps - 1) + k + r0, rc), :]
            v_ref[pl.ds(r0, rc), :] = acc

    return pl.pallas_call(
        body, name=name, grid=(d // LANE, t // ts),
        in_specs=[pl.BlockSpec((2, ts, LANE), lambda c, i: (0, i, c)),
                  pl.BlockSpec((2, HALO, LANE), lambda c, i: (0, jnp.maximum(i * hb - 1, 0), c)),
                  pl.BlockSpec((k_taps, LANE), lambda c, i: (0, c)),
                  pl.BlockSpec((1, LANE), lambda c, i: (0, c))],
        out_specs=pl.BlockSpec((ts, LANE), lambda c, i: (i, c)),
        out_shape=jax.ShapeDtypeStruct((t, d), F32),
        scratch_shapes=[pltpu.VMEM((HALO + ts, LANE), F32)],
        compiler_params=_params("parallel", "parallel"),
    )(a, a, w, b)


def _conv_bwd(name, a, dv, w, seq):
    _, t, d = a.shape
    k_taps = w.shape[0]
    ts, tps, rc = _conv_tiles(t, seq)
    hb = ts // HALO
    nhb = t // HALO

    def body(cur_ref, prev_ref, dv_ref, ndv_ref, w_ref, da_ref, dw_ref, dbp_ref, upad, dvpad, dwrows):
        i = pl.program_id(1)
        first = (i % tps) == 0
        last = (i % tps) == tps - 1
        pv = prev_ref[0].astype(F32)
        pg = prev_ref[1].astype(F32)
        upad[0:HALO, :] = jnp.where(first, 0.0, pv * _sigmoid(pg))
        upad[HALO:HALO + ts, :] = cur_ref[0].astype(F32) * _sigmoid(cur_ref[1].astype(F32))
        dvpad[0:ts, :] = dv_ref[...]
        dvpad[ts:ts + HALO, :] = jnp.where(last, 0.0, ndv_ref[...])
        wv = w_ref[...]

        @pl.when(i == 0)
        def _():
            dw_ref[...] = jnp.zeros_like(dw_ref)
            dbp_ref[...] = jnp.zeros_like(dbp_ref)

        sv = jnp.zeros((1, LANE), F32)
        sg = jnp.zeros((1, LANE), F32)
        for r0 in range(0, ts, rc):
            du = jnp.zeros((rc, LANE), F32)
            for k in range(k_taps):
                du = du + wv[k:k + 1, :] * dvpad[pl.ds(r0 + (k_taps - 1) - k, rc), :]
            av = cur_ref[0, pl.ds(r0, rc), :].astype(F32)
            sig = _sigmoid(cur_ref[1, pl.ds(r0, rc), :].astype(F32))
            dval = du * sig
            dgate = du * av * sig * (1.0 - sig)
            da_ref[0, pl.ds(r0, rc), :] = dval.astype(BF16)
            da_ref[1, pl.ds(r0, rc), :] = dgate.astype(BF16)
            sv = sv + jnp.sum(dval, axis=0, keepdims=True)
            sg = sg + jnp.sum(dgate, axis=0, keepdims=True)
        dbp_ref[0] += sv
        dbp_ref[1] += sg

        for k in range(k_taps):
            acc = jnp.zeros((rc, LANE), F32)
            for r0 in range(0, ts, rc):
                acc = acc + dvpad[pl.ds(r0, rc), :] * upad[pl.ds(HALO - (k_taps - 1) + k + r0, rc), :]
            dwrows[k:k + 1, :] = jnp.sum(acc, axis=0, keepdims=True)
        dw_ref[...] += dwrows[0:k_taps, :]

    return pl.pallas_call(
        body, name=name, grid=(d // LANE, t // ts),
        in_specs=[pl.BlockSpec((2, ts, LANE), lambda c, i: (0, i, c)),
                  pl.BlockSpec((2, HALO, LANE), lambda c, i: (0, jnp.maximum(i * hb - 1, 0), c)),
                  pl.BlockSpec((ts, LANE), lambda c, i: (i, c)),
                  pl.BlockSpec((HALO, LANE), lambda c, i: (jnp.minimum((i + 1) * hb, nhb - 1), c)),
                  pl.BlockSpec((k_taps, LANE), lambda c, i: (0, c))],
        out_specs=[pl.BlockSpec((2, ts, LANE), lambda c, i: (0, i, c)),
                   pl.BlockSpec((k_taps, LANE), lambda c, i: (0, c)),
                   pl.BlockSpec((2, 1, LANE), lambda c, i: (0, 0, c))],
        out_shape=[jax.ShapeDtypeStruct((2, t, d), BF16), jax.ShapeDtypeStruct((k_taps, d), F32),
                   jax.ShapeDtypeStruct((2, 1, d), F32)],
        scratch_shapes=[pltpu.VMEM((HALO + ts, LANE), F32), pltpu.VMEM((ts + HALO, LANE), F32),
                        pltpu.VMEM((HALO, LANE), F32)],
        compiler_params=_params("parallel", "arbitrary"),
    )(a, a, dv, dv, w)


def _pool_mix_fwd(name, x, gain, wp, scale, bias, next_gain, seq):
    t, d = x.shape
    ts = _tile(seq, 512)
    tps = seq // ts
    hb = ts // HALO
    cg = d // N_GROUPS
    sw = min(cg, LANE)

    def body(cur_ref, prev_ref, g_ref, w_ref, s_ref, b_ref, ng_ref, p_ref, r_ref, h_ref, hpad):
        i = pl.program_id(0)
        first = (i % tps) == 0
        g = g_ref[...]
        h_prev = jnp.where(first, 0.0, _rms(prev_ref[...], g))
        h_cur = _rms(cur_ref[...], g)
        for si in range(d // sw):
            hpad[si, 0:HALO, :] = h_prev[:, si * sw:(si + 1) * sw]
            hpad[si, HALO:HALO + ts, :] = h_cur[:, si * sw:(si + 1) * sw]
        pos = (i % tps) * ts + lax.broadcasted_iota(jnp.int32, (ts, 1), 0)
        for gi, win in enumerate(POOL_WINDOWS):
            sl = slice(gi * cg, (gi + 1) * cg)
            cnt = jnp.minimum(pos + 1, win).astype(F32)
            for si in range(gi * cg // sw, (gi + 1) * cg // sw):
                own = hpad[si, HALO:HALO + ts, :]
                acc = own
                for j in range(1, win):
                    acc = acc + hpad[si, pl.ds(HALO - j, ts), :]
                p_ref[:, si * sw:(si + 1) * sw] = (acc / cnt - own).astype(BF16)
            mixed = jnp.dot(p_ref[:, sl], w_ref[gi], preferred_element_type=F32)
            r_ref[:, sl] = cur_ref[:, sl] + s_ref[:, sl] * (mixed + b_ref[:, sl])
        h_ref[...] = _rms(r_ref[...], ng_ref[...]).astype(BF16)

    row = pl.BlockSpec((ts, d), lambda i: (i, 0))
    vec = pl.BlockSpec((1, d), lambda i: (0, 0))
    return pl.pallas_call(
        body, name=name, grid=(t // ts,),
        in_specs=[row, pl.BlockSpec((HALO, d), lambda i: (jnp.maximum(i * hb - 1, 0), 0)), vec,
                  pl.BlockSpec((N_GROUPS, cg, cg), lambda i: (0, 0, 0)), vec, vec, vec],
        out_specs=[row, row, row],
        out_shape=[jax.ShapeDtypeStruct((t, d), BF16), jax.ShapeDtypeStruct((t, d), F32),
                   jax.ShapeDtypeStruct((t, d), BF16)],
        scratch_shapes=[pltpu.VMEM((d // sw, HALO + ts, sw), F32)],
        compiler_params=_params("parallel"),
    )(x, x, gain, wp, scale, bias, next_gain)


def _pool_mix_bwd(name, pooled, wp, dr, x, gain, scale, bias, seq):
    t, d = x.shape
    ts = _tile(seq, 512)
    tps = seq // ts
    hb = ts // HALO
    nhb = t // HALO
    cg = d // N_GROUPS
    sw = min(cg, LANE)

    def body(p_ref, w_ref, dr_ref, ndr_ref, x_ref, g_ref, s_ref, b_ref, dx_ref, dw_ref, ds_ref, db_ref, dg_ref,
             qpad, dh):
        i = pl.program_id(0)
        last = (i % tps) == tps - 1
        pos = (i % tps) * ts + lax.broadcasted_iota(jnp.int32, (ts, 1), 0)

        @pl.when(i == 0)
        def _():
            dw_ref[...] = jnp.zeros_like(dw_ref)
            ds_ref[...] = jnp.zeros_like(ds_ref)
            db_ref[...] = jnp.zeros_like(db_ref)
            dg_ref[...] = jnp.zeros_like(dg_ref)

        for gi, win in enumerate(POOL_WINDOWS):
            sl = slice(gi * cg, (gi + 1) * cg)
            wv = w_ref[gi]
            sc = s_ref[:, sl]
            drv = dr_ref[:, sl]
            dmx = drv * sc
            dmx16 = dmx.astype(BF16)
            pooled = p_ref[:, sl]
            dw_ref[gi] += lax.dot_general(pooled, dmx16, (TN, ((), ())), preferred_element_type=F32)
            mixed = jnp.dot(pooled, wv, preferred_element_type=F32)
            ds_ref[:, sl] += jnp.sum(drv * (mixed + b_ref[:, sl]), axis=0, keepdims=True)
            db_ref[:, sl] += jnp.sum(dmx, axis=0, keepdims=True)
            cur = lax.dot_general(dmx16, wv, (NT, ((), ())), preferred_element_type=F32)
            nxt = lax.dot_general((ndr_ref[:, sl] * sc).astype(BF16), wv, (NT, ((), ())),
                                  preferred_element_type=F32)
            q_cur = cur / jnp.minimum(pos + 1, win).astype(F32)
            q_nxt = jnp.where(last, 0.0, nxt / float(win))
            for k, si in enumerate(range(gi * cg // sw, (gi + 1) * cg // sw)):
                part = slice(k * sw, (k + 1) * sw)
                qpad[si, 0:ts, :] = q_cur[:, part]
                qpad[si, ts:ts + HALO, :] = q_nxt[:, part]
                acc = -cur[:, part]
                for j in range(win):
                    acc = acc + qpad[si, pl.ds(j, ts), :]
                dh[:, si * sw:(si + 1) * sw] = acc
        dx, dgain_term, _ = _rms_bwd_tile(dh[...], x_ref[...], g_ref[...], dr_ref[...])
        dx_ref[...] = dx
        dg_ref[...] += jnp.sum(dgain_term, axis=0, keepdims=True)

    row = pl.BlockSpec((ts, d), lambda i: (i, 0))
    vec = pl.BlockSpec((1, d), lambda i: (0, 0))
    return pl.pallas_call(
        body, name=name, grid=(t // ts,),
        in_specs=[row, pl.BlockSpec((N_GROUPS, cg, cg), lambda i: (0, 0, 0)), row,
                  pl.BlockSpec((HALO, d), lambda i: (jnp.minimum((i + 1) * hb, nhb - 1), 0)), row, vec, vec, vec],
        out_specs=[row, pl.BlockSpec((N_GROUPS, cg, cg), lambda i: (0, 0, 0)), vec, vec, vec],
        out_shape=[jax.ShapeDtypeStruct((t, d), F32), jax.ShapeDtypeStruct((N_GROUPS, cg, cg), F32)]
        + [jax.ShapeDtypeStruct((1, d), F32)] * 3,
        scratch_shapes=[pltpu.VMEM((d // sw, ts + HALO, sw), F32), pltpu.VMEM((ts, d), F32)],
        compiler_params=_params("arbitrary"),
    )(pooled, wp, dr, dr, x, gain, scale, bias)


def _ctile(n, pref):
    return max(c for c in range(LANE, min(pref, n) + 1, LANE) if n % c == 0)


FFN_COLS = 1408
FFN_ROWS = 32
FFN_TILE = 1024


def _ffn_fwd(name, up, w, b, seq):
    _, t, dff = up.shape
    f = _ctile(dff, FFN_COLS)
    k_taps = w.shape[0]
    ts = _tile(seq, FFN_TILE)
    tps = seq // ts
    hb = ts // HALO16
    rc = _tile(ts, FFN_ROWS)

    def body(cur_ref, prev_ref, w_ref, b_ref, g_ref, apad):
        i = pl.program_id(1)
        first = (i % tps) == 0
        for ci, c0 in enumerate(range(0, f, LANE)):
            cols = slice(c0, c0 + LANE)
            apad[ci, 0:HALO16, :] = jnp.where(first, 0.0, prev_ref[:, cols].astype(F32))
            apad[ci, HALO16:HALO16 + ts, :] = cur_ref[0, :, cols].astype(F32)
            wv = w_ref[:, cols]
            wk = [jnp.broadcast_to(wv[k:k + 1, :], (rc, LANE)) for k in range(k_taps)]
            bias = jnp.broadcast_to(b_ref[:, cols], (rc, LANE))
            for r0 in range(0, ts, rc):
                c = bias
                for k in range(k_taps):
                    c = c + wk[k] * apad[ci, pl.ds(HALO16 - (k_taps - 1) + k + r0, rc), :]
                gate = cur_ref[1, pl.ds(r0, rc), cols].astype(F32)
                g_ref[pl.ds(r0, rc), cols] = (c * _sigmoid(c) * gate).astype(BF16)

    return pl.pallas_call(
        body, name=name, grid=(dff // f, t // ts),
        in_specs=[pl.BlockSpec((2, ts, f), lambda j, i: (0, i, j)),
                  pl.BlockSpec((None, HALO16, f), lambda j, i: (0, jnp.maximum(i * hb - 1, 0), j)),
                  pl.BlockSpec((k_taps, f), lambda j, i: (0, j)),
                  pl.BlockSpec((1, f), lambda j, i: (0, j))],
        out_specs=pl.BlockSpec((ts, f), lambda j, i: (i, j)),
        out_shape=jax.ShapeDtypeStruct((t, dff), BF16),
        scratch_shapes=[pltpu.VMEM((f // LANE, HALO16 + ts, LANE), F32)],
        compiler_params=_params("parallel", "parallel"),
    )(up, up, w, b)


def _ffn_bwd(name, up, dg, w, b, seq):
    _, t, dff = up.shape
    f = _ctile(dff, FFN_COLS)
    k_taps = w.shape[0]
    ts = _tile(seq, FFN_TILE)
    tps = seq // ts
    hb = ts // HALO16
    nhb = t // HALO16
    ext = ts + HALO16
    rc = _tile(ts, FFN_ROWS)

    def body(cur_ref, prev_ref, next_ref, dg_ref, ndg_ref, w_ref, b_ref, dup_ref, dw_ref, db_ref, apad, dcpad):
        i = pl.program_id(1)
        first = (i % tps) == 0
        last = (i % tps) == tps - 1

        @pl.when(i == 0)
        def _():
            dw_ref[...] = jnp.zeros_like(dw_ref)
            db_ref[...] = jnp.zeros_like(db_ref)

        for ci, c0 in enumerate(range(0, f, LANE)):
            cols = slice(c0, c0 + LANE)
            apad[ci, 0:HALO16, :] = jnp.where(first, 0.0, prev_ref[:, cols].astype(F32))
            apad[ci, HALO16:HALO16 + ts, :] = cur_ref[0, :, cols].astype(F32)
            apad[ci, HALO16 + ts:HALO16 + ext, :] = next_ref[0, :, cols].astype(F32)
            wv = w_ref[:, cols]
            wk = [jnp.broadcast_to(wv[k:k + 1, :], (rc, LANE)) for k in range(k_taps)]
            bias = jnp.broadcast_to(b_ref[:, cols], (rc, LANE))

            def conv_grad(r0, n, gate, dgv):
                c = bias[0:n]
                for k in range(k_taps):
                    c = c + wk[k][0:n] * apad[ci, pl.ds(HALO16 - (k_taps - 1) + k + r0, n), :]
                sig = _sigmoid(c)
                silu = c * sig
                return dgv * gate * (sig + silu * (1.0 - sig)), silu

            for r0 in range(0, ts, rc):
                dgv = dg_ref[pl.ds(r0, rc), cols].astype(F32)
                dc, silu = conv_grad(r0, rc, cur_ref[1, pl.ds(r0, rc), cols].astype(F32), dgv)
                dcpad[ci, pl.ds(r0, rc), :] = dc
                dup_ref[1, pl.ds(r0, rc), cols] = (dgv * silu).astype(BF16)
            dgv = jnp.where(last, 0.0, ndg_ref[:, cols].astype(F32))
            dc, _ = conv_grad(ts, HALO16, next_ref[1, :, cols].astype(F32), dgv)
            dcpad[ci, ts:ext, :] = dc

            dw_acc = [jnp.zeros((rc, LANE), F32) for _ in range(k_taps)]
            db_acc = jnp.zeros((rc, LANE), F32)
            for r0 in range(0, ts, rc):
                dact = jnp.zeros((rc, LANE), F32)
                for k in range(k_taps):
                    dact = dact + wk[k] * dcpad[ci, pl.ds(r0 + (k_taps - 1) - k, rc), :]
                dup_ref[0, pl.ds(r0, rc), cols] = dact.astype(BF16)
                dc = dcpad[ci, pl.ds(r0, rc), :]
                for k in range(k_taps):
                    dw_acc[k] = dw_acc[k] + dc * apad[ci, pl.ds(HALO16 - (k_taps - 1) + k + r0, rc), :]
                db_acc = db_acc + dc
            for k in range(k_taps):
                dw_ref[k:k + 1, cols] += jnp.sum(dw_acc[k], axis=0, keepdims=True)
            db_ref[:, cols] += jnp.sum(db_acc, axis=0, keepdims=True)

    return pl.pallas_call(
        body, name=name, grid=(dff // f, t // ts),
        in_specs=[pl.BlockSpec((2, ts, f), lambda j, i: (0, i, j)),
                  pl.BlockSpec((None, HALO16, f), lambda j, i: (0, jnp.maximum(i * hb - 1, 0), j)),
                  pl.BlockSpec((2, HALO16, f), lambda j, i: (0, jnp.minimum((i + 1) * hb, nhb - 1), j)),
                  pl.BlockSpec((ts, f), lambda j, i: (i, j)),
                  pl.BlockSpec((HALO16, f), lambda j, i: (jnp.minimum((i + 1) * hb, nhb - 1), j)),
                  pl.BlockSpec((k_taps, f), lambda j, i: (0, j)),
                  pl.BlockSpec((1, f), lambda j, i: (0, j))],
        out_specs=[pl.BlockSpec((2, ts, f), lambda j, i: (0, i, j)),
                   pl.BlockSpec((k_taps, f), lambda j, i: (0, j)),
                   pl.BlockSpec((1, f), lambda j, i: (0, j))],
        out_shape=[jax.ShapeDtypeStruct((2, t, dff), BF16), jax.ShapeDtypeStruct((k_taps, dff), F32),
                   jax.ShapeDtypeStruct((1, dff), F32)],
        scratch_shapes=[pltpu.VMEM((f // LANE, HALO16 + ext, LANE), F32), pltpu.VMEM((f // LANE, ext, LANE), F32)],
        compiler_params=_params("parallel", "arbitrary"),
    )(up, up, up, dg, dg, w, b)


def _sum_rows(name, g):
    ns, r, c = g.shape
    tr = _tile(r, 256)

    def body(g_ref, o_ref):
        acc = g_ref[0]
        for dev in range(1, ns):
            acc = acc + g_ref[dev]
        o_ref[...] = acc

    return pl.pallas_call(
        body, name=name, grid=(r // tr,),
        in_specs=[pl.BlockSpec((ns, tr, c), lambda i: (0, i, 0))],
        out_specs=pl.BlockSpec((tr, c), lambda i: (i, 0)),
        out_shape=jax.ShapeDtypeStruct((r, c), F32), compiler_params=_params("parallel"),
    )(g)


def _adamw(name, gsrc, w, m, v, layer=0, prev=None):
    ns, r, c = gsrc.shape
    nl = w.shape[0]
    tr = _tile(r, 256)
    prev = () if prev is None else tuple(prev)

    def body(g_ref, w_ref, m_ref, v_ref, *rest):
        go_ref, do_ref, mo_ref, vo_ref = rest[len(prev):]
        g = g_ref[0].astype(F32)
        for dev in range(1, ns):
            g = g + g_ref[dev].astype(F32)
        m_new = ADAM_B1 * m_ref[...] + (1.0 - ADAM_B1) * g
        v_new = ADAM_B2 * v_ref[...] + (1.0 - ADAM_B2) * (g * g)
        m_hat = m_new / (1.0 - ADAM_B1 ** ADAM_STEP)
        v_hat = v_new / (1.0 - ADAM_B2 ** ADAM_STEP)
        go_ref[...] = g
        do_ref[...] = -ADAM_LR * (m_hat / (jnp.sqrt(v_hat) + ADAM_EPS) + ADAM_WD * w_ref[...])
        mo_ref[...] = m_new
        vo_ref[...] = v_new

    row = pl.BlockSpec((None, tr, c), lambda i: (layer, i, 0))
    return pl.pallas_call(
        body, name=name, grid=(r // tr,),
        in_specs=[pl.BlockSpec((ns, tr, c), lambda i: (0, i, 0)), row, row, row] + [ANY] * len(prev),
        out_specs=[row] * 4, out_shape=[jax.ShapeDtypeStruct((nl, r, c), F32)] * 4,
        input_output_aliases={4 + i: i for i in range(len(prev))},
        compiler_params=_params("parallel"),
    )(gsrc, w, m, v, *prev)


def _ffn_forward(tag, r_in, h, get_wu, get_wd, wdw, bdw, seq, loss=None):
    t, d = r_in.shape
    tm = _tile(t, 512)
    wu = get_wu(h)
    dff = wu.shape[0] // 2
    tu = _tile(t, 1024)
    up = _mm(f"{tag}_up", h, wu, grid=(2, t // tu, 1),
             a_spec=pl.BlockSpec((tu, d), lambda j, i, k: (i, 0)),
             b_spec=pl.BlockSpec((dff, d), lambda j, i, k: (j, 0)),
             out_spec=pl.BlockSpec((None, tu, dff), lambda j, i, k: (j, i, 0)),
             out_shape=jax.ShapeDtypeStruct((2, t, dff), BF16), dims=NT, acc_shape=(tu, dff))
    wd = get_wd(up)
    g = _ffn_fwd(f"{tag}_act", up, wdw, bdw, seq)
    row = pl.BlockSpec((tm, d), lambda i, j, k: (i, 0))
    vec = pl.BlockSpec((1, d), lambda i, j, k: (0, 0))
    common = dict(grid=(t // tm, 1, 1), a_spec=pl.BlockSpec((tm, dff), lambda i, j, k: (i, 0)),
                  b_spec=pl.BlockSpec((dff, d), lambda i, j, k: (0, 0)), dims=NN, acc_shape=(tm, d))
    if loss is None:
        out = _mm(f"{tag}_down", g, wd, out_spec=row, out_shape=jax.ShapeDtypeStruct((t, d), F32),
                  extras=(r_in,), extra_specs=(row,), epilogue=lambda acc, ex, rows: ((ex[0][rows, :] + acc,), ()),
                  **common)
    else:
        def head(acc, ex, rows):
            dx, part, dgain = _loss_tile(ex[0][rows, :] + acc, ex[1][rows, :], ex[2][...])
            return (dx,), (part, dgain)

        out = _mm(f"{tag}_down", g, wd, out_spec=[row, pl.BlockSpec((1, 1), lambda i, j, k: (0, 0)), vec],
                  out_shape=[jax.ShapeDtypeStruct((t, d), F32), jax.ShapeDtypeStruct((1, 1), F32),
                             jax.ShapeDtypeStruct((1, d), F32)],
                  extras=(r_in, *loss), extra_specs=(row, row, vec), epilogue=head, n_sums=2, **common)
    return out, (r_in, h, up, g, wu, wd)


def _ffn_backward(tag, dr, saved, gain, wdw, bdw, seq, token=None, send_dwd=None):
    r_in, h, up, g, wu, wd = saved
    t, d = r_in.shape
    dff = wd.shape[0]
    tm = _tile(t, 512)
    tk = _tile(t, 2048)
    tku = _tile(t, 4096)
    cw = _ctile(dff, 1408)
    nc = dff // cw
    once = dict(pipeline_mode=pl.Buffered(1)) if tku == t else {}
    dg = _mm(f"{tag}_dg", dr, wd, grid=(t // tm, 1, 1),
             a_spec=pl.BlockSpec((tm, d), lambda i, j, k: (i, 0)),
             b_spec=pl.BlockSpec((dff, d), lambda i, j, k: (0, 0)),
             out_spec=pl.BlockSpec((tm, dff), lambda i, j, k: (i, 0)),
             out_shape=jax.ShapeDtypeStruct((t, dff), BF16), dims=NT, acc_shape=(tm, dff), token=token)
    dwd = _mm(f"{tag}_dwd", g, dr, grid=(dff // cw, 1, t // tk),
              a_spec=pl.BlockSpec((tk, cw), lambda i, j, k: (k, i)),
              b_spec=pl.BlockSpec((tk, d), lambda i, j, k: (k, 0)),
              out_spec=pl.BlockSpec((cw, d), lambda i, j, k: (i, 0)),
              out_shape=jax.ShapeDtypeStruct((dff, d), BF16), dims=TN, acc_shape=(cw, d))
    sent = None if send_dwd is None else send_dwd(dwd)
    dup, dwdw, dbdw = _ffn_bwd(f"{tag}_dact", up, dg, wdw, bdw, seq)
    row = pl.BlockSpec((tm, d), lambda i, j, k: (i, 0))
    vec = pl.BlockSpec((1, d), lambda i, j, k: (0, 0))

    def norm_backward(acc, ex, rows):
        dx, dgain, colsum = _rms_bwd_tile(acc, ex[0][rows, :], ex[1][...], ex[2][rows, :])
        return (dx,), (dgain, colsum)

    dr_in, dgain, colsum = _mm(
        f"{tag}_dh", dup, wu, grid=(t // tm, 1, 1),
        a_spec=pl.BlockSpec((2, tm, dff), lambda i, j, k: (0, i, 0)),
        b_spec=pl.BlockSpec((2 * dff, d), lambda i, j, k: (0, 0), pipeline_mode=pl.Buffered(1)),
        out_spec=[row, vec, vec],
        out_shape=[jax.ShapeDtypeStruct((t, d), F32)] + [jax.ShapeDtypeStruct((1, d), F32)] * 2,
        dims=NN, acc_shape=(tm, d), extras=(r_in, gain, dr), extra_specs=(row, vec, row),
        epilogue=norm_backward, n_sums=2, parts=2, token=sent)
    dwu = _mm(f"{tag}_dwu", dup, h, grid=(2 * nc, 1, t // tku),
              a_spec=pl.BlockSpec((None, tku, cw), lambda i, j, k: (i // nc, k, i % nc)),
              b_spec=pl.BlockSpec((tku, d), lambda i, j, k: (k, 0), **once),
              out_spec=pl.BlockSpec((cw, d), lambda i, j, k: (i, 0)),
              out_shape=jax.ShapeDtypeStruct((2 * dff, d), BF16), dims=TN, acc_shape=(cw, d))
    return dr_in, dgain, dwu, dwd, dwdw, dbdw, colsum


def _pad_to(vec, n):
    return jnp.pad(vec, (0, n - vec.shape[0]))


def _pack(parts, width):
    flat = jnp.concatenate([p.reshape(-1).astype(F32) for p in parts])
    n = -(-flat.shape[0] // (8 * width)) * (8 * width)
    return _pad_to(flat, n).reshape(n // width, width)


def _unpack(mat, shapes):
    flat = mat.reshape(-1)
    out, off = [], 0
    for s in shapes:
        n = 1
        for dim in s:
            n *= dim
        out.append(flat[off:off + n].reshape(s))
        off += n
    return out


def kernel(x, norm_mix, norm_ffn, conv_w_pw1, conv_b_pw1, conv_w_dw, conv_b_dw, conv_ln_g, conv_ln_b, conv_w_pw2, conv_b_pw2, pool_w, pool_b, pool_scale, ffn_w_up, ffn_w_dw, ffn_b_dw, ffn_w_down, final_norm, loss_target, m_norm_mix, m_norm_ffn, m_conv_w_pw1, m_conv_b_pw1, m_conv_w_dw, m_conv_b_dw, m_conv_ln_g, m_conv_ln_b, m_conv_w_pw2, m_conv_b_pw2, m_pool_w, m_pool_b, m_pool_scale, m_ffn_w_up, m_ffn_w_dw, m_ffn_b_dw, m_ffn_w_down, m_final_norm, v_norm_mix, v_norm_ffn, v_conv_w_pw1, v_conv_b_pw1, v_conv_w_dw, v_conv_b_dw, v_conv_ln_g, v_conv_ln_b, v_conv_w_pw2, v_conv_b_pw2, v_pool_w, v_pool_b, v_pool_scale, v_ffn_w_up, v_ffn_w_dw, v_ffn_b_dw, v_ffn_w_down, v_final_norm):
    bsz, seq, d = x.shape
    t = bsz * seq
    k_taps = conv_w_dw.shape[1]
    cs1 = conv_w_pw1.shape[2]
    dsh = d // N_DEV
    cg = d // N_GROUPS
    cgs = pool_w.shape[2]
    fu = ffn_w_up.shape[2]
    fd = ffn_w_down.shape[1]
    dff = fd * N_DEV
    nb = N_DEV // 2
    kf = ffn_w_dw.shape[1]
    fsh = ffn_w_dw.shape[2]
    my = _lin(_me())
    tm = _tile(t, 512)

    x2 = x.reshape(t, d)
    tgt2 = loss_target.reshape(t, d)

    small_shapes = [(k_taps, dsh), (dsh,), (dsh,), (2, kf, fsh)]
    small_mine = _pack([conv_w_dw[0], pool_b[0], pool_scale[0], ffn_w_dw], LANE)
    big = [conv_w_pw1[0], conv_w_pw2[0], ffn_w_up[0].T, ffn_w_down[0], pool_w[0], ffn_w_up[1].T, ffn_w_down[1]]
    gather = _exchange_start("gather_start", [small_mine] + [w.astype(BF16) for w in big], [GATHER2] * 8)
    h0 = _rms_fwd("l0_rms", x2, norm_mix[0:1])
    small_w = [norm_mix, norm_ffn, conv_b_pw1, conv_w_dw, conv_b_dw, conv_ln_g, conv_ln_b, conv_b_pw2, pool_b,
               pool_scale, ffn_w_dw, ffn_b_dw, final_norm]
    small_m = [m_norm_mix, m_norm_ffn, m_conv_b_pw1, m_conv_w_dw, m_conv_b_dw, m_conv_ln_g, m_conv_ln_b,
               m_conv_b_pw2, m_pool_b, m_pool_scale, m_ffn_w_dw, m_ffn_b_dw, m_final_norm]
    small_v = [v_norm_mix, v_norm_ffn, v_conv_b_pw1, v_conv_w_dw, v_conv_b_dw, v_conv_ln_g, v_conv_ln_b,
               v_conv_b_pw2, v_pool_b, v_pool_scale, v_ffn_w_dw, v_ffn_b_dw, v_final_norm]
    small_state = [_pack(group, 8 * LANE)[None] for group in (small_w, small_m, small_v)]
    forwarded = _exchange_forward("gather_forward_w1", gather, [0, 1], (h0, *small_state))
    small_all, w1 = _exchange_wait("gather_wait_w1", gather, [0, 1], forwarded)
    parts = [_unpack(small_all[dev], small_shapes) for dev in range(N_DEV)]
    wdw = jnp.concatenate([p[0] for p in parts], axis=1)
    pool_b_full = jnp.concatenate([p[1] for p in parts]).reshape(1, d)
    pool_s_full = jnp.concatenate([p[2] for p in parts]).reshape(1, d)
    fwdw = jnp.concatenate([p[3] for p in parts], axis=2)
    fbdw = ffn_b_dw.reshape(2, 1, dff)

    def columns(w):
        return w.transpose(1, 0, 2).reshape(w.shape[1], N_DEV * w.shape[2])

    def column_shards(w):
        return w.reshape(w.shape[0], N_DEV, w.shape[1] // N_DEV).transpose(1, 0, 2)

    w1 = columns(w1)
    a = _mm("l0_pw1", h0, w1, grid=(2, t // tm, 1),
            a_spec=pl.BlockSpec((tm, d), lambda j, i, k: (i, 0)),
            b_spec=pl.BlockSpec((d, d), lambda j, i, k: (0, j)),
            out_spec=pl.BlockSpec((None, tm, d), lambda j, i, k: (j, i, 0)),
            out_shape=jax.ShapeDtypeStruct((2, t, d), BF16), dims=NN, acc_shape=(tm, d),
            extras=(conv_b_pw1,), extra_specs=(pl.BlockSpec((1, d), lambda j, i, k: (0, j)),),
            epilogue=lambda acc, ex, rows: ((acc + ex[0][...],), ()))
    v = _conv_fwd("l0_conv", a, wdw, conv_b_dw, seq)
    forwarded = _exchange_forward("gather_forward_wu0", gather, [2, 3], v)
    (w2,) = _exchange_wait("gather_wait_w2", gather, [2], forwarded)
    w2 = w2.reshape(d, d)
    row = pl.BlockSpec((tm, d), lambda i, j, k: (i, 0))
    vec = pl.BlockSpec((1, d), lambda i, j, k: (0, 0))
    square = pl.BlockSpec((d, d), lambda i, j, k: (0, 0))

    def ln_silu(v_blk, ex):
        s_blk = _ln_silu_tile(v_blk, ex[0][...], ex[1][...]).astype(BF16)
        return s_blk, s_blk

    def residual_and_norm(acc, ex, rows):
        r_blk = ex[3][rows, :] + (acc + ex[2][...])
        return (r_blk, _rms(r_blk, ex[4][...])), ()

    r1, h1, s = _mm("l0_pw2", v, w2, grid=(t // tm, 1, 1), a_spec=row, b_spec=square, out_spec=[row, row, row],
                    out_shape=[jax.ShapeDtypeStruct((t, d), F32), jax.ShapeDtypeStruct((t, d), BF16),
                               jax.ShapeDtypeStruct((t, d), BF16)],
                    dims=NN, acc_shape=(tm, d), extras=(conv_ln_g, conv_ln_b, conv_b_pw2, x2, norm_ffn[0:1]),
                    extra_specs=(vec, vec, vec, row, vec), prologue=ln_silu, epilogue=residual_and_norm)

    def up_getter(name, idx):
        return lambda after: _exchange_wait(name, gather, [idx], after)[0].reshape(2 * dff, d)

    def down_getter(name, idx, forward=None):
        def get(after):
            if forward is not None:
                after = _exchange_forward(forward[0], gather, forward[1], after)
            return _exchange_wait(name, gather, [idx], after)[0].reshape(dff, d)
        return get

    r2, ffn0_saved = _ffn_forward("f0", r1, h1, up_getter("gather_wait_wu0", 3),
                                  down_getter("gather_wait_wd0", 4, ("gather_forward_wu1", [4, 5, 6])),
                                  fwdw[0], fbdw[0], seq)
    forwarded = _exchange_forward("gather_forward_wd1", gather, [7], r2)
    (wp,) = _exchange_wait("gather_wait_wp", gather, [5], forwarded)
    wp = wp.transpose(1, 0, 2, 3).reshape(N_GROUPS, cg, cg)
    pooled, r3, h3 = _pool_mix_fwd("l1_mix", r2, norm_mix[1:2], wp, pool_s_full, pool_b_full, norm_ffn[1:2], seq)
    (dr4, loss_part, dfinal), ffn1_saved = _ffn_forward(
        "f1", r3, h3, up_getter("gather_wait_wu1", 6), down_getter("gather_wait_wd1", 7), fwdw[1], fbdw[1], seq,
        loss=(tgt2, final_norm.reshape(1, d)))

    dr3, dnf1, dwu1, dwd1, dfw1, dfb1, _ = _ffn_backward("f1", dr4, ffn1_saved, norm_ffn[1:2], fwdw[1], fbdw[1], seq)
    scatter_a = _exchange_start("scatter_f1_start", [dwu1.reshape(N_DEV, fu, d), dwd1.reshape(N_DEV, fd, d)],
                                [SCATTER, SCATTER])
    dr2, dwp, dpool_s, dpool_b, dnm1 = _pool_mix_bwd(
        "l1_dmix", pooled, wp, dr3, r2, norm_mix[1:2], pool_s_full + scatter_a["token"][0:1, 0:1], pool_b_full, seq)
    early = {}

    def send_dwd0(dwd):
        early["down"] = _exchange_start("scatter_f0_down_start", [dwd.reshape(N_DEV, fd, d)], [SCATTER])
        return early["down"]["token"]

    dr1, dnf0, dwu0, dwd0, dfw0, dfb0, db2 = _ffn_backward("f0", dr2, ffn0_saved, norm_ffn[0:1], fwdw[0], fbdw[0], seq,
                                                           send_dwd=send_dwd0)
    dwp_b = dwp.astype(BF16).reshape(N_GROUPS, N_DEV, cgs, cg).transpose(1, 0, 2, 3)
    tk = _tile(t, 2048)
    dw2 = _mm("l0_dw2", s, dr1, grid=(1, 1, t // tk),
              a_spec=pl.BlockSpec((tk, d), lambda i, j, k: (k, 0)),
              b_spec=pl.BlockSpec((tk, d), lambda i, j, k: (k, 0)),
              out_spec=pl.BlockSpec((d, d), lambda i, j, k: (0, 0)),
              out_shape=jax.ShapeDtypeStruct((d, d), BF16), dims=TN, acc_shape=(d, d))
    scatter_b = _exchange_start("scatter_f0_start", [dwu0.reshape(N_DEV, fu, d), dwp_b,
                                                     dw2.reshape(N_DEV, d // N_DEV, d)], [SCATTER] * 3)

    def ln_silu_backward(acc, ex, rows):
        dv_blk, dgain, dbias, colsum = _ln_silu_bwd_tile(acc, ex[0][rows, :], ex[1][...], ex[2][...])
        return (dv_blk,), (dgain, dbias, colsum)

    dv, dlg, dlb, dbdw = _mm("l0_ds", dr1, w2, grid=(t // tm, 1, 1), a_spec=row, b_spec=square,
                             out_spec=[row, vec, vec, vec],
                             out_shape=[jax.ShapeDtypeStruct((t, d), F32)] + [jax.ShapeDtypeStruct((1, d), F32)] * 3,
                             dims=NT, acc_shape=(tm, d), extras=(v, conv_ln_g, conv_ln_b), extra_specs=(row, vec, vec),
                             epilogue=ln_silu_backward, n_sums=3, token=scatter_b["token"])
    da, dwdw, db1 = _conv_bwd("l0_dconv", a, dv, wdw, seq)
    tk1 = _tile(t, 4096)
    once = dict(pipeline_mode=pl.Buffered(1)) if tk1 == t else {}
    dw1 = _mm("l0_dw1", h0, da, grid=(1, 2, t // tk1),
              a_spec=pl.BlockSpec((tk1, d), lambda i, j, k: (k, 0), **once),
              b_spec=pl.BlockSpec((None, tk1, d), lambda i, j, k: (j, k, 0)),
              out_spec=pl.BlockSpec((d, d), lambda i, j, k: (0, j)),
              out_shape=jax.ShapeDtypeStruct((d, 2 * d), BF16), dims=TN, acc_shape=(d, d))
    scatter_c = _exchange_start("scatter_l0_start", [column_shards(dw1)], [SCATTER])
    def norm_backward(acc, ex, rows):
        dx_blk, dgain, colsum = _rms_bwd_tile(acc, ex[0][rows, :], ex[1][...], ex[2][rows, :])
        return (dx_blk,), (dgain, colsum)

    dx, dnm0, _ = _mm("l0_dh", da, w1, grid=(t // tm, 1, 1),
                      a_spec=pl.BlockSpec((2, tm, d), lambda i, j, k: (0, i, 0)),
                      b_spec=pl.BlockSpec((d, 2 * d), lambda i, j, k: (0, 0), pipeline_mode=pl.Buffered(1)),
                      out_spec=[row, vec, vec],
                      out_shape=[jax.ShapeDtypeStruct((t, d), F32)] + [jax.ShapeDtypeStruct((1, d), F32)] * 2,
                      dims=NT, acc_shape=(tm, d), extras=(x2, norm_mix[0:1], dr1), extra_specs=(row, vec, row),
                      epilogue=norm_backward, n_sums=2, parts=2, token=scatter_c["token"])

    dffn_w = jnp.stack([dfw0, dfw1])
    dffn_b = jnp.stack([dfb0, dfb1]).reshape(2, dff)
    small_parts = [loss_part, jnp.concatenate([dnm0, dnm1]), jnp.concatenate([dnf0, dnf1]), db1, dwdw, dbdw, dlg, dlb,
                   db2, dpool_b, dpool_s, dffn_w, dffn_b, dfinal]
    small_part_shapes = [(1,), (2, d), (2, d), (1, 2 * d), (k_taps, d), (1, d), (1, d), (1, d), (1, d), (1, d), (1, d),
                         (2, kf, dff), (2, dff), (d,)]
    packed = _pack(small_parts, 8 * LANE)
    gather_small = _exchange_start("gather_small_start", [packed], [GATHER])

    def big_update(name, recv, w, m, v, layer=0, prev=None):
        shape = w.shape
        c = recv.shape[-1]
        rows = recv.size // (N_DEV * c)
        nl = w.size // (rows * c)
        outs = _adamw(name, recv.reshape(N_DEV, rows, c), w.reshape(nl, rows, c), m.reshape(nl, rows, c),
                      v.reshape(nl, rows, c), layer, prev)
        return outs, [o.reshape(shape) for o in outs]

    wu_t = [p.transpose(0, 2, 1) for p in (ffn_w_up, m_ffn_w_up, v_ffn_w_up)]
    g_wu1, g_wd1 = _exchange_wait("scatter_f1_wait", scatter_a, [0, 1], gather_small["token"])
    raw_wu, _ = big_update("adam_wu1", g_wu1, *wu_t, 1)
    raw_wd, _ = big_update("adam_wd1", g_wd1, ffn_w_down, m_ffn_w_down, v_ffn_w_down, 1)
    (g_wd0,) = _exchange_wait("scatter_f0_down_wait", early["down"], [0], raw_wd[0])
    g_wu0, g_wp, g_w2 = _exchange_wait("scatter_f0_wait", scatter_b, [0, 1, 2], g_wd0)
    _, u_wu = big_update("adam_wu0", g_wu0, *wu_t, 0, raw_wu)
    u_wu = [o.transpose(0, 2, 1) for o in u_wu]
    _, u_wd = big_update("adam_wd0", g_wd0, ffn_w_down, m_ffn_w_down, v_ffn_w_down, 0, raw_wd)
    _, u_wp = big_update("adam_wp", g_wp, pool_w, m_pool_w, v_pool_w)
    (g_w1,) = _exchange_wait("scatter_l0_wait", scatter_c, [0], u_wp[0])
    _, u_w1 = big_update("adam_w1", g_w1, conv_w_pw1, m_conv_w_pw1, v_conv_w_pw1)
    _, u_w2 = big_update("adam_w2", g_w2, conv_w_pw2, m_conv_w_pw2, v_conv_w_pw2)
    (all_small,) = _exchange_wait("gather_small_wait", gather_small, [0], u_w2[0])
    summed = _sum_rows("sum_small_grads", all_small)
    (loss_v, g_nm, g_nf, g_b1, g_wdw, g_bdw, g_lg, g_lb, g_b2, g_pb, g_ps, g_fw, g_fb,
     g_fin) = _unpack(summed, small_part_shapes)
    loss = loss_v[0]
    g_wdw_mine = lax.dynamic_slice_in_dim(g_wdw, my * dsh, dsh, axis=1)[None]
    g_pb_mine = lax.dynamic_slice_in_dim(g_pb, my * dsh, dsh, axis=1)
    g_ps_mine = lax.dynamic_slice_in_dim(g_ps, my * dsh, dsh, axis=1)
    g_fw_mine = lax.dynamic_slice_in_dim(g_fw, my * fsh, fsh, axis=2)

    small_g = [g_nm, g_nf, g_b1, g_wdw_mine, g_bdw, g_lg, g_lb, g_b2, g_pb_mine, g_ps_mine, g_fw_mine, g_fb, g_fin]
    shapes = [w.shape for w in small_w]
    outs = _adamw("adam_small", _pack(small_g, 8 * LANE)[None], *small_state)
    sg, sd, sm, sv = [_unpack(o, shapes) for o in outs]

    def leaf(kind):
        (nm, nf, b1, wdw_, bdw_, lg, lb, b2, pb, ps, fw, fb, fin) = (sg, sd, sm, sv)[kind]
        return [nm, nf, u_w1[kind], b1, wdw_, bdw_, lg, lb, u_w2[kind], b2, u_wp[kind], pb, ps, u_wu[kind], fw, fb,
                u_wd[kind], fin]

    return (loss, dx.reshape(bsz, seq, d), *leaf(0), *leaf(1), *leaf(2), *leaf(3))
```

```python
import functools

import jax
import jax.numpy as jnp
from jax import lax
from jax.experimental import pallas as pl
from jax.experimental.pallas import tpu as pltpu

F32 = jnp.float32
BF16 = jnp.bfloat16
MESH = pl.DeviceIdType.MESH
HBM = pl.BlockSpec(memory_space=pltpu.HBM)

N_DEV = 8
RMS_EPS = 1e-6
LN_EPS = 1e-5
POOL_WINDOWS = (2, 4, 8, 16)
N_GROUPS = len(POOL_WINDOWS)
ADAM_LR = 0.001
ADAM_B1 = 0.9
ADAM_B2 = 0.999
ADAM_EPS = 1e-08
ADAM_WD = 0.01
ADAM_STEP = 10

LANE = 128
HALO = 32
HALO16 = 16
VMEM_LIMIT = 56 * 1024 * 1024


def _params(*sem):
    return pltpu.CompilerParams(dimension_semantics=sem if sem else None, vmem_limit_bytes=VMEM_LIMIT)


def _tile(n, pref):
    for t in range(min(pref, n), 15, -1):
        if n % t == 0 and t % 16 == 0:
            return t
    return n


def _sigmoid(z):
    return 1.0 / (1.0 + jnp.exp(-z))


def _me():
    return lax.axis_index("x"), lax.axis_index("y"), lax.axis_index("c")


def _flip(pos, m):
    x, y, c = pos
    return ((1 - x) if m & 4 else x, (1 - y) if m & 2 else y, (1 - c) if m & 1 else c)


def _lin(pos):
    return 4 * pos[0] + 2 * pos[1] + pos[2]


SEM = pl.BlockSpec(memory_space=pltpu.SEMAPHORE)
ANY = pl.BlockSpec(memory_space=pl.ANY)
EFFECT = pltpu.SideEffectType.DATAFLOW_SIDE_EFFECTING


SCATTER = "scatter"
GATHER = "gather"
GATHER2 = "gather2"
ALL_MASKS = (1, 2, 3, 4, 5, 6, 7)
SIBLING = 1
CHIPS = (2, 4, 6)


class _Copies:
    def __init__(self, a, mode, src, land, send_sems, recv_sems):
        self.a, self.mode, self.src, self.land = a, mode, src, land
        self.send_sems, self.recv_sems = send_sems, recv_sems
        self.me = _me()
        self.first = (SIBLING,) + CHIPS if mode == GATHER2 else ALL_MASKS

    def _sems(self, m, to):
        return dict(send_sem=self.send_sems.at[self.a * N_DEV + m], recv_sem=self.recv_sems.at[self.a * N_DEV + m],
                    device_id=to, device_id_type=MESH)

    def _block(self, pid):
        return self.src.at[pid] if self.mode == SCATTER else self.src

    def local(self):
        my = _lin(self.me)
        return pltpu.make_async_copy(self._block(my), self.land.at[my], self.send_sems.at[self.a * N_DEV])

    def send(self, m):
        peer = _flip(self.me, m)
        return pltpu.make_async_remote_copy(src_ref=self._block(_lin(peer)), dst_ref=self.land.at[_lin(self.me)],
                                            **self._sems(m, peer))

    def arrival(self, m):
        rows = self.land.at[_lin(_flip(self.me, m))]
        return pltpu.make_async_remote_copy(src_ref=rows, dst_ref=rows, **self._sems(m, _flip(self.me, m)))

    def forward(self, m):
        rows = self.land.at[_lin(_flip(self.me, m))]
        return pltpu.make_async_remote_copy(src_ref=rows, dst_ref=rows, **self._sems(m | 1, _flip(self.me, SIBLING)))


def _exchange_start(name, arrs, modes):
    n = len(arrs)
    blocks = [a.shape[1:] if md == SCATTER else a.shape for a, md in zip(arrs, modes)]

    def body(*refs):
        srcs, lands = refs[:n], refs[n:2 * n]
        send_sems, recv_sems = refs[2 * n], refs[2 * n + 1]
        token = refs[-1]
        for a in range(n):
            cp = _Copies(a, modes[a], srcs[a], lands[a], send_sems, recv_sems)
            cp.local().start()
            for m in cp.first:
                cp.send(m).start()
        token[...] = jnp.zeros_like(token)

    lands = [lax.empty((N_DEV,) + tuple(b), a.dtype) for a, b in zip(arrs, blocks)]
    outs = pl.pallas_call(
        body, name=name,
        out_shape=(pltpu.SemaphoreType.DMA((n * N_DEV,)), pltpu.SemaphoreType.DMA((n * N_DEV,)),
                   *[pltpu.HBM(a.shape, a.dtype) for a in arrs], *[pltpu.HBM(l.shape, l.dtype) for l in lands],
                   jax.ShapeDtypeStruct((8, LANE), F32)),
        in_specs=[HBM] * (2 * n),
        out_specs=(SEM, SEM, *[HBM] * (2 * n), pl.BlockSpec(memory_space=pltpu.VMEM)),
        input_output_aliases={i: 2 + i for i in range(2 * n)},
        compiler_params=pltpu.CompilerParams(has_side_effects=EFFECT),
    )(*[pltpu.with_memory_space_constraint(a, pltpu.HBM) for a in arrs],
      *[pltpu.with_memory_space_constraint(l, pltpu.HBM) for l in lands])
    return dict(send=outs[0], recv=outs[1], srcs=list(outs[2:2 + n]), lands=list(outs[2 + n:2 + 2 * n]),
                modes=modes, token=outs[-1])


def _exchange_forward(name, handle, which, after):
    k = len(which)

    def half(wait):
        def body(*refs):
            lands = refs[:k]
            send_sems, recv_sems = refs[k], refs[k + 1]
            token = refs[-1]
            for pos, a in enumerate(which):
                cp = _Copies(a, GATHER2, None, lands[pos], send_sems, recv_sems)
                for m in CHIPS:
                    if wait:
                        cp.arrival(m).wait_recv()
                    else:
                        cp.forward(m).start()
            token[...] = jnp.zeros_like(token)
        return body

    def call(body, call_name, lands, after):
        after = tuple(after) if isinstance(after, (tuple, list)) else (after,)
        outs = pl.pallas_call(
            body, name=call_name,
            out_shape=(*[pltpu.HBM(x.shape, x.dtype) for x in lands], jax.ShapeDtypeStruct((8, LANE), F32)),
            in_specs=[HBM] * k + [SEM, SEM] + [ANY] * len(after),
            out_specs=(*[HBM] * k, pl.BlockSpec(memory_space=pltpu.VMEM)),
            input_output_aliases={i: i for i in range(k)},
            compiler_params=pltpu.CompilerParams(has_side_effects=EFFECT),
        )(*lands, handle["send"], handle["recv"], *after)
        return list(outs[:k]), outs[-1]

    lands, arrived = call(half(True), name + "_arrived", [handle["lands"][a] for a in which], after)
    lands, token = call(half(False), name, lands, arrived)
    for pos, a in enumerate(which):
        handle["lands"][a] = lands[pos]
    return token


def _exchange_wait(name, handle, which, after):
    k = len(which)
    modes = handle["modes"]

    def body(*refs):
        srcs, lands = refs[:k], refs[k:2 * k]
        send_sems, recv_sems = refs[2 * k], refs[2 * k + 1]
        for pos, a in enumerate(which):
            cp = _Copies(a, modes[a], srcs[pos], lands[pos], send_sems, recv_sems)
            cp.local().wait()
            for m in cp.first:
                cp.send(m).wait_send()
            if modes[a] == GATHER2:
                for m in CHIPS:
                    cp.forward(m).wait_send()
                arrivals = (SIBLING,) + tuple(m | 1 for m in CHIPS)
            else:
                arrivals = ALL_MASKS
            for m in arrivals:
                cp.arrival(m).wait_recv()

    srcs = [handle["srcs"][a] for a in which]
    lands = [handle["lands"][a] for a in which]
    outs = pl.pallas_call(
        body, name=name,
        out_shape=tuple(pltpu.HBM(x.shape, x.dtype) for x in srcs + lands),
        in_specs=[HBM] * (2 * k) + [SEM, SEM, ANY], out_specs=tuple([HBM] * (2 * k)),
        input_output_aliases={i: i for i in range(2 * k)},
        compiler_params=pltpu.CompilerParams(has_side_effects=EFFECT),
    )(*srcs, *lands, handle["send"], handle["recv"], after)
    for pos, a in enumerate(which):
        handle["srcs"][a], handle["lands"][a] = outs[pos], outs[k + pos]
    return list(outs[k:])


def _mm(name, a, b, *, grid, a_spec, b_spec, out_spec, out_shape, dims, acc_shape, extras=(), extra_specs=(),
        epilogue=None, token=None, prologue=None, n_sums=0, parts=1):
    nk = grid[2]
    ne = len(extras)
    deps = () if token is None else (token,)
    dep_specs = [pl.BlockSpec((8, LANE), lambda i, j, k: (0, 0))] * len(deps)
    n_out = len(out_shape) if isinstance(out_shape, (list, tuple)) else 1
    n_tiles = n_out - n_sums - (1 if prologue is not None else 0)

    def body(a_ref, b_ref, *rest):
        ex, o_refs, acc_ref = rest[:ne], rest[ne + len(deps):ne + len(deps) + n_out], rest[ne + len(deps) + n_out]
        k = pl.program_id(2)
        if parts == 1:
            a_blk, saved = a_ref[...], None
            if prologue is not None:
                a_blk, saved = prologue(a_blk, ex)
                o_refs[n_tiles][...] = saved
            part = lax.dot_general(a_blk.astype(BF16), b_ref[...].astype(BF16), (dims, ((), ())),
                                   preferred_element_type=F32)
        else:
            kb = b_ref.shape[dims[1][0]] // parts
            part = None
            for p in range(parts):
                b_blk = b_ref[p * kb:(p + 1) * kb, :] if dims[1][0] == 0 else b_ref[:, p * kb:(p + 1) * kb]
                term = lax.dot_general(a_ref[p].astype(BF16), b_blk.astype(BF16), (dims, ((), ())),
                                       preferred_element_type=F32)
                part = term if part is None else part + term
        sum_refs = o_refs[n_out - n_sums:]

        def add_sums(terms):
            @pl.when((pl.program_id(0) == 0) & (pl.program_id(1) == 0))
            def _():
                for o_ref in sum_refs:
                    o_ref[...] = jnp.zeros_like(o_ref)

            for o_ref, term in zip(sum_refs, terms):
                o_ref[...] += jnp.sum(term, axis=0, keepdims=True)

        def finish(r):
            tiles, terms = ((r,), ()) if epilogue is None else epilogue(r, ex, slice(None))
            for o_ref, val in zip(o_refs, tiles):
                o_ref[...] = val.astype(o_ref.dtype)
            if n_sums:
                add_sums(terms)

        if nk == 1:
            finish(part)
            return

        @pl.when(k == 0)
        def _():
            acc_ref[...] = part

        @pl.when((k > 0) & (k < nk - 1))
        def _():
            acc_ref[...] += part

        @pl.when(k == nk - 1)
        def _():
            finish(acc_ref[...] + part)

    return pl.pallas_call(
        body, name=name, grid=grid, in_specs=[a_spec, b_spec, *extra_specs, *dep_specs], out_specs=out_spec,
        out_shape=out_shape, scratch_shapes=[pltpu.VMEM(acc_shape if nk > 1 else (8, LANE), F32)],
        compiler_params=_params(*(("arbitrary",) * 3 if n_sums else ("parallel", "parallel", "arbitrary"))),
    )(a, b, *extras, *deps)


def _rms(x, gain):
    return x * lax.rsqrt(jnp.mean(x * x, axis=-1, keepdims=True) + RMS_EPS) * gain


def _rms_bwd_tile(dh, x, gain, dres):
    rstd = lax.rsqrt(jnp.mean(x * x, axis=-1, keepdims=True) + RMS_EPS)
    xhat = x * rstd
    dxhat = dh * gain
    dx = dres + rstd * (dxhat - xhat * jnp.mean(dxhat * xhat, axis=-1, keepdims=True))
    return dx, dh * xhat, dx


def _ln_silu_tile(v, g, b):
    mu = jnp.mean(v, axis=-1, keepdims=True)
    cen = v - mu
    z = cen * lax.rsqrt(jnp.mean(cen * cen, axis=-1, keepdims=True) + LN_EPS) * g + b
    return z * _sigmoid(z)


def _ln_silu_bwd_tile(ds, v, g, b):
    mu = jnp.mean(v, axis=-1, keepdims=True)
    cen = v - mu
    rstd = lax.rsqrt(jnp.mean(cen * cen, axis=-1, keepdims=True) + LN_EPS)
    y = cen * rstd
    z = y * g + b
    sig = _sigmoid(z)
    dz = ds * sig * (1.0 + z * (1.0 - sig))
    dy = dz * g
    dv = rstd * (dy - jnp.mean(dy, axis=-1, keepdims=True) - y * jnp.mean(dy * y, axis=-1, keepdims=True))
    return dv, dz * y, dz, dv


def _loss_tile(x, tgt, gain):
    d = x.shape[-1]
    rstd = lax.rsqrt(jnp.mean(x * x, axis=-1, keepdims=True) + RMS_EPS)
    xhat = x * rstd
    err = xhat * gain - tgt
    dy = err / d
    dxhat = dy * gain
    dx = rstd * (dxhat - xhat * jnp.mean(dxhat * xhat, axis=-1, keepdims=True))
    return dx, 0.5 * jnp.mean(err * err, axis=-1, keepdims=True), dy * xhat


NN = ((1,), (0,))
NT = ((1,), (1,))
TN = ((0,), (0,))


def _rms_fwd(name, x, gain):
    t, d = x.shape
    tr = _tile(t, 512)

    def body(x_ref, g_ref, h_ref):
        h_ref[...] = _rms(x_ref[...], g_ref[...]).astype(BF16)

    return pl.pallas_call(
        body, name=name, grid=(t // tr,),
        in_specs=[pl.BlockSpec((tr, d), lambda i: (i, 0)), pl.BlockSpec((1, d), lambda i: (0, 0))],
        out_specs=pl.BlockSpec((tr, d), lambda i: (i, 0)),
        out_shape=jax.ShapeDtypeStruct((t, d), BF16), compiler_params=_params("parallel"),
    )(x, gain)


def _conv_tiles(t, seq):
    ts = _tile(seq, 2048)
    return ts, seq // ts, _tile(ts, 64)


def _conv_fwd(name, a, w, b, seq):
    _, t, d = a.shape
    k_taps = w.shape[0]
    ts, tps, rc = _conv_tiles(t, seq)
    hb = ts // HALO

    def body(cur_ref, prev_ref, w_ref, b_ref, v_ref, upad):
        i = pl.program_id(1)
        first = (i % tps) == 0
        pv = prev_ref[0].astype(F32)
        pg = prev_ref[1].astype(F32)
        upad[0:HALO, :] = jnp.where(first, 0.0, pv * _sigmoid(pg))
        upad[HALO:HALO + ts, :] = cur_ref[0].astype(F32) * _sigmoid(cur_ref[1].astype(F32))
        wv = w_ref[...]
        bias = jnp.broadcast_to(b_ref[...], (rc, LANE))
        for r0 in range(0, ts, rc):
            acc = bias
            for k in range(k_taps):
                acc = acc + wv[k:k + 1, :] * upad[pl.ds(HALO - (k_taps - 1) + k + r0, rc), :]
            v_ref[pl.ds(r0, rc), :] = acc

    return pl.pallas_call(
        body, name=name, grid=(d // LANE, t // ts),
        in_specs=[pl.BlockSpec((2, ts, LANE), lambda c, i: (0, i, c)),
                  pl.BlockSpec((2, HALO, LANE), lambda c, i: (0, jnp.maximum(i * hb - 1, 0), c)),
                  pl.BlockSpec((k_taps, LANE), lambda c, i: (0, c)),
                  pl.BlockSpec((1, LANE), lambda c, i: (0, c))],
        out_specs=pl.BlockSpec((ts, LANE), lambda c, i: (i, c)),
        out_shape=jax.ShapeDtypeStruct((t, d), F32),
        scratch_shapes=[pltpu.VMEM((HALO + ts, LANE), F32)],
        compiler_params=_params("parallel", "parallel"),
    )(a, a, w, b)


def _conv_bwd(name, a, dv, w, seq):
    _, t, d = a.shape
    k_taps = w.shape[0]
    ts, tps, rc = _conv_tiles(t, seq)
    hb = ts // HALO
    nhb = t // HALO

    def body(cur_ref, prev_ref, dv_ref, ndv_ref, w_ref, da_ref, dw_ref, dbp_ref, upad, dvpad, dwrows):
        i = pl.program_id(1)
        first = (i % tps) == 0
        last = (i % tps) == tps - 1
        pv = prev_ref[0].astype(F32)
        pg = prev_ref[1].astype(F32)
        upad[0:HALO, :] = jnp.where(first, 0.0, pv * _sigmoid(pg))
        upad[HALO:HALO + ts, :] = cur_ref[0].astype(F32) * _sigmoid(cur_ref[1].astype(F32))
        dvpad[0:ts, :] = dv_ref[...]
        dvpad[ts:ts + HALO, :] = jnp.where(last, 0.0, ndv_ref[...])
        wv = w_ref[...]

        @pl.when(i == 0)
        def _():
            dw_ref[...] = jnp.zeros_like(dw_ref)
            dbp_ref[...] = jnp.zeros_like(dbp_ref)

        sv = jnp.zeros((1, LANE), F32)
        sg = jnp.zeros((1, LANE), F32)
        for r0 in range(0, ts, rc):
            du = jnp.zeros((rc, LANE), F32)
            for k in range(k_taps):
                du = du + wv[k:k + 1, :] * dvpad[pl.ds(r0 + (k_taps - 1) - k, rc), :]
            av = cur_ref[0, pl.ds(r0, rc), :].astype(F32)
            sig = _sigmoid(cur_ref[1, pl.ds(r0, rc), :].astype(F32))
            dval = du * sig
            dgate = du * av * sig * (1.0 - sig)
            da_ref[0, pl.ds(r0, rc), :] = dval.astype(BF16)
            da_ref[1, pl.ds(r0, rc), :] = dgate.astype(BF16)
            sv = sv + jnp.sum(dval, axis=0, keepdims=True)
            sg = sg + jnp.sum(dgate, axis=0, keepdims=True)
        dbp_ref[0] += sv
        dbp_ref[1] += sg

        for k in range(k_taps):
            acc = jnp.zeros((rc, LANE), F32)
            for r0 in range(0, ts, rc):
                acc = acc + dvpad[pl.ds(r0, rc), :] * upad[pl.ds(HALO - (k_taps - 1) + k + r0, rc), :]
            dwrows[k:k + 1, :] = jnp.sum(acc, axis=0, keepdims=True)
        dw_ref[...] += dwrows[0:k_taps, :]

    return pl.pallas_call(
        body, name=name, grid=(d // LANE, t // ts),
        in_specs=[pl.BlockSpec((2, ts, LANE), lambda c, i: (0, i, c)),
                  pl.BlockSpec((2, HALO, LANE), lambda c, i: (0, jnp.maximum(i * hb - 1, 0), c)),
                  pl.BlockSpec((ts, LANE), lambda c, i: (i, c)),
                  pl.BlockSpec((HALO, LANE), lambda c, i: (jnp.minimum((i + 1) * hb, nhb - 1), c)),
                  pl.BlockSpec((k_taps, LANE), lambda c, i: (0, c))],
        out_specs=[pl.BlockSpec((2, ts, LANE), lambda c, i: (0, i, c)),
                   pl.BlockSpec((k_taps, LANE), lambda c, i: (0, c)),
                   pl.BlockSpec((2, 1, LANE), lambda c, i: (0, 0, c))],
        out_shape=[jax.ShapeDtypeStruct((2, t, d), BF16), jax.ShapeDtypeStruct((k_taps, d), F32),
                   jax.ShapeDtypeStruct((2, 1, d), F32)],
        scratch_shapes=[pltpu.VMEM((HALO + ts, LANE), F32), pltpu.VMEM((ts + HALO, LANE), F32),
                        pltpu.VMEM((HALO, LANE), F32)],
        compiler_params=_params("parallel", "arbitrary"),
    )(a, a, dv, dv, w)


def _pool_mix_fwd(name, x, gain, wp, scale, bias, next_gain, seq):
    t, d = x.shape
    ts = _tile(seq, 512)
    tps = seq // ts
    hb = ts // HALO
    cg = d // N_GROUPS
    sw = min(cg, LANE)

    def body(cur_ref, prev_ref, g_ref, w_ref, s_ref, b_ref, ng_ref, p_ref, r_ref, h_ref, hpad):
        i = pl.program_id(0)
        first = (i % tps) == 0
        g = g_ref[...]
        h_prev = jnp.where(first, 0.0, _rms(prev_ref[...], g))
        h_cur = _rms(cur_ref[...], g)
        for si in range(d // sw):
            hpad[si, 0:HALO, :] = h_prev[:, si * sw:(si + 1) * sw]
            hpad[si, HALO:HALO + ts, :] = h_cur[:, si * sw:(si + 1) * sw]
        pos = (i % tps) * ts + lax.broadcasted_iota(jnp.int32, (ts, 1), 0)
        for gi, win in enumerate(POOL_WINDOWS):
            sl = slice(gi * cg, (gi + 1) * cg)
            cnt = jnp.minimum(pos + 1, win).astype(F32)
            for si in range(gi * cg // sw, (gi + 1) * cg // sw):
                own = hpad[si, HALO:HALO + ts, :]
                acc = own
                for j in range(1, win):
                    acc = acc + hpad[si, pl.ds(HALO - j, ts), :]
                p_ref[:, si * sw:(si + 1) * sw] = (acc / cnt - own).astype(BF16)
            mixed = jnp.dot(p_ref[:, sl], w_ref[gi], preferred_element_type=F32)
            r_ref[:, sl] = cur_ref[:, sl] + s_ref[:, sl] * (mixed + b_ref[:, sl])
        h_ref[...] = _rms(r_ref[...], ng_ref[...]).astype(BF16)

    row = pl.BlockSpec((ts, d), lambda i: (i, 0))
    vec = pl.BlockSpec((1, d), lambda i: (0, 0))
    return pl.pallas_call(
        body, name=name, grid=(t // ts,),
        in_specs=[row, pl.BlockSpec((HALO, d), lambda i: (jnp.maximum(i * hb - 1, 0), 0)), vec,
                  pl.BlockSpec((N_GROUPS, cg, cg), lambda i: (0, 0, 0)), vec, vec, vec],
        out_specs=[row, row, row],
        out_shape=[jax.ShapeDtypeStruct((t, d), BF16), jax.ShapeDtypeStruct((t, d), F32),
                   jax.ShapeDtypeStruct((t, d), BF16)],
        scratch_shapes=[pltpu.VMEM((d // sw, HALO + ts, sw), F32)],
        compiler_params=_params("parallel"),
    )(x, x, gain, wp, scale, bias, next_gain)


def _pool_mix_bwd(name, pooled, wp, dr, x, gain, scale, bias, seq):
    t, d = x.shape
    ts = _tile(seq, 512)
    tps = seq // ts
    hb = ts // HALO
    nhb = t // HALO
    cg = d // N_GROUPS
    sw = min(cg, LANE)

    def body(p_ref, w_ref, dr_ref, ndr_ref, x_ref, g_ref, s_ref, b_ref, dx_ref, dw_ref, ds_ref, db_ref, dg_ref,
             qpad, dh):
        i = pl.program_id(0)
        last = (i % tps) == tps - 1
        pos = (i % tps) * ts + lax.broadcasted_iota(jnp.int32, (ts, 1), 0)

        @pl.when(i == 0)
        def _():
            dw_ref[...] = jnp.zeros_like(dw_ref)
            ds_ref[...] = jnp.zeros_like(ds_ref)
            db_ref[...] = jnp.zeros_like(db_ref)
            dg_ref[...] = jnp.zeros_like(dg_ref)

        for gi, win in enumerate(POOL_WINDOWS):
            sl = slice(gi * cg, (gi + 1) * cg)
            wv = w_ref[gi]
            sc = s_ref[:, sl]
            drv = dr_ref[:, sl]
            dmx = drv * sc
            dmx16 = dmx.astype(BF16)
            pooled = p_ref[:, sl]
            dw_ref[gi] += lax.dot_general(pooled, dmx16, (TN, ((), ())), preferred_element_type=F32)
            mixed = jnp.dot(pooled, wv, preferred_element_type=F32)
            ds_ref[:, sl] += jnp.sum(drv * (mixed + b_ref[:, sl]), axis=0, keepdims=True)
            db_ref[:, sl] += jnp.sum(dmx, axis=0, keepdims=True)
            cur = lax.dot_general(dmx16, wv, (NT, ((), ())), preferred_element_type=F32)
            nxt = lax.dot_general((ndr_ref[:, sl] * sc).astype(BF16), wv, (NT, ((), ())),
                                  preferred_element_type=F32)
            q_cur = cur / jnp.minimum(pos + 1, win).astype(F32)
            q_nxt = jnp.where(last, 0.0, nxt / float(win))
            for k, si in enumerate(range(gi * cg // sw, (gi + 1) * cg // sw)):
                part = slice(k * sw, (k + 1) * sw)
                qpad[si, 0:ts, :] = q_cur[:, part]
                qpad[si, ts:ts + HALO, :] = q_nxt[:, part]
                acc = -cur[:, part]
                for j in range(win):
                    acc = acc + qpad[si, pl.ds(j, ts), :]
                dh[:, si * sw:(si + 1) * sw] = acc
        dx, dgain_term, _ = _rms_bwd_tile(dh[...], x_ref[...], g_ref[...], dr_ref[...])
        dx_ref[...] = dx
        dg_ref[...] += jnp.sum(dgain_term, axis=0, keepdims=True)

    row = pl.BlockSpec((ts, d), lambda i: (i, 0))
    vec = pl.BlockSpec((1, d), lambda i: (0, 0))
    return pl.pallas_call(
        body, name=name, grid=(t // ts,),
        in_specs=[row, pl.BlockSpec((N_GROUPS, cg, cg), lambda i: (0, 0, 0)), row,
                  pl.BlockSpec((HALO, d), lambda i: (jnp.minimum((i + 1) * hb, nhb - 1), 0)), row, vec, vec, vec],
        out_specs=[row, pl.BlockSpec((N_GROUPS, cg, cg), lambda i: (0, 0, 0)), vec, vec, vec],
        out_shape=[jax.ShapeDtypeStruct((t, d), F32), jax.ShapeDtypeStruct((N_GROUPS, cg, cg), F32)]
        + [jax.ShapeDtypeStruct((1, d), F32)] * 3,
        scratch_shapes=[pltpu.VMEM((d // sw, ts + HALO, sw), F32), pltpu.VMEM((ts, d), F32)],
        compiler_params=_params("arbitrary"),
    )(pooled, wp, dr, dr, x, gain, scale, bias)


def _ctile(n, pref):
    return max(c for c in range(LANE, min(pref, n) + 1, LANE) if n % c == 0)


FFN_COLS = 1408
FFN_ROWS = 32
FFN_TILE = 1024


def _ffn_fwd(name, up, w, b, seq):
    _, t, dff = up.shape
    f = _ctile(dff, FFN_COLS)
    k_taps = w.shape[0]
    ts = _tile(seq, FFN_TILE)
    tps = seq // ts
    hb = ts // HALO16
    rc = _tile(ts, FFN_ROWS)

    def body(cur_ref, prev_ref, w_ref, b_ref, g_ref, apad):
        i = pl.program_id(1)
        first = (i % tps) == 0
        for ci, c0 in enumerate(range(0, f, LANE)):
            cols = slice(c0, c0 + LANE)
            apad[ci, 0:HALO16, :] = jnp.where(first, 0.0, prev_ref[:, cols].astype(F32))
            apad[ci, HALO16:HALO16 + ts, :] = cur_ref[0, :, cols].astype(F32)
            wv = w_ref[:, cols]
            wk = [jnp.broadcast_to(wv[k:k + 1, :], (rc, LANE)) for k in range(k_taps)]
            bias = jnp.broadcast_to(b_ref[:, cols], (rc, LANE))
            for r0 in range(0, ts, rc):
                c = bias
                for k in range(k_taps):
                    c = c + wk[k] * apad[ci, pl.ds(HALO16 - (k_taps - 1) + k + r0, rc), :]
                gate = cur_ref[1, pl.ds(r0, rc), cols].astype(F32)
                g_ref[pl.ds(r0, rc), cols] = (c * _sigmoid(c) * gate).astype(BF16)

    return pl.pallas_call(
        body, name=name, grid=(dff // f, t // ts),
        in_specs=[pl.BlockSpec((2, ts, f), lambda j, i: (0, i, j)),
                  pl.BlockSpec((None, HALO16, f), lambda j, i: (0, jnp.maximum(i * hb - 1, 0), j)),
                  pl.BlockSpec((k_taps, f), lambda j, i: (0, j)),
                  pl.BlockSpec((1, f), lambda j, i: (0, j))],
        out_specs=pl.BlockSpec((ts, f), lambda j, i: (i, j)),
        out_shape=jax.ShapeDtypeStruct((t, dff), BF16),
        scratch_shapes=[pltpu.VMEM((f // LANE, HALO16 + ts, LANE), F32)],
        compiler_params=_params("parallel", "parallel"),
    )(up, up, w, b)


def _ffn_bwd(name, up, dg, w, b, seq):
    _, t, dff = up.shape
    f = _ctile(dff, FFN_COLS)
    k_taps = w.shape[0]
    ts = _tile(seq, FFN_TILE)
    tps = seq // ts
    hb = ts // HALO16
    nhb = t // HALO16
    ext = ts + HALO16
    rc = _tile(ts, FFN_ROWS)

    def body(cur_ref, prev_ref, next_ref, dg_ref, ndg_ref, w_ref, b_ref, dup_ref, dw_ref, db_ref, apad, dcpad):
        i = pl.program_id(1)
        first = (i % tps) == 0
        last = (i % tps) == tps - 1

        @pl.when(i == 0)
        def _():
            dw_ref[...] = jnp.zeros_like(dw_ref)
            db_ref[...] = jnp.zeros_like(db_ref)

        for ci, c0 in enumerate(range(0, f, LANE)):
            cols = slice(c0, c0 + LANE)
            apad[ci, 0:HALO16, :] = jnp.where(first, 0.0, prev_ref[:, cols].astype(F32))
            apad[ci, HALO16:HALO16 + ts, :] = cur_ref[0, :, cols].astype(F32)
            apad[ci, HALO16 + ts:HALO16 + ext, :] = next_ref[0, :, cols].astype(F32)
            wv = w_ref[:, cols]
            wk = [jnp.broadcast_to(wv[k:k + 1, :], (rc, LANE)) for k in range(k_taps)]
            bias = jnp.broadcast_to(b_ref[:, cols], (rc, LANE))

            def conv_grad(r0, n, gate, dgv):
                c = bias[0:n]
                for k in range(k_taps):
                    c = c + wk[k][0:n] * apad[ci, pl.ds(HALO16 - (k_taps - 1) + k + r0, n), :]
                sig = _sigmoid(c)
                silu = c * sig
                return dgv * gate * (sig + silu * (1.0 - sig)), silu

            for r0 in range(0, ts, rc):
                dgv = dg_ref[pl.ds(r0, rc), cols].astype(F32)
                dc, silu = conv_grad(r0, rc, cur_ref[1, pl.ds(r0, rc), cols].astype(F32), dgv)
                dcpad[ci, pl.ds(r0, rc), :] = dc
                dup_ref[1, pl.ds(r0, rc), cols] = (dgv * silu).astype(BF16)
            dgv = jnp.where(last, 0.0, ndg_ref[:, cols].astype(F32))
            dc, _ = conv_grad(ts, HALO16, next_ref[1, :, cols].astype(F32), dgv)
            dcpad[ci, ts:ext, :] = dc

            dw_acc = [jnp.zeros((rc, LANE), F32) for _ in range(k_taps)]
            db_acc = jnp.zeros((rc, LANE), F32)
            for r0 in range(0, ts, rc):
                dact = jnp.zeros((rc, LANE), F32)
                for k in range(k_taps):
                    dact = dact + wk[k] * dcpad[ci, pl.ds(r0 + (k_taps - 1) - k, rc), :]
                dup_ref[0, pl.ds(r0, rc), cols] = dact.astype(BF16)
                dc = dcpad[ci, pl.ds(r0, rc), :]
                for k in range(k_taps):
                    dw_acc[k] = dw_acc[k] + dc * apad[ci, pl.ds(HALO16 - (k_taps - 1) + k + r0, rc), :]
                db_acc = db_acc + dc
            for k in range(k_taps):
                dw_ref[k:k + 1, cols] += jnp.sum(dw_acc[k], axis=0, keepdims=True)
            db_ref[:, cols] += jnp.sum(db_acc, axis=0, keepdims=True)

    return pl.pallas_call(
        body, name=name, grid=(dff // f, t // ts),
        in_specs=[pl.BlockSpec((2, ts, f), lambda j, i: (0, i, j)),
                  pl.BlockSpec((None, HALO16, f), lambda j, i: (0, jnp.maximum(i * hb - 1, 0), j)),
                  pl.BlockSpec((2, HALO16, f), lambda j, i: (0, jnp.minimum((i + 1) * hb, nhb - 1), j)),
                  pl.BlockSpec((ts, f), lambda j, i: (i, j)),
                  pl.BlockSpec((HALO16, f), lambda j, i: (jnp.minimum((i + 1) * hb, nhb - 1), j)),
                  pl.BlockSpec((k_taps, f), lambda j, i: (0, j)),
                  pl.BlockSpec((1, f), lambda j, i: (0, j))],
        out_specs=[pl.BlockSpec((2, ts, f), lambda j, i: (0, i, j)),
                   pl.BlockSpec((k_taps, f), lambda j, i: (0, j)),
                   pl.BlockSpec((1, f), lambda j, i: (0, j))],
        out_shape=[jax.ShapeDtypeStruct((2, t, dff), BF16), jax.ShapeDtypeStruct((k_taps, dff), F32),
                   jax.ShapeDtypeStruct((1, dff), F32)],
        scratch_shapes=[pltpu.VMEM((f // LANE, HALO16 + ext, LANE), F32), pltpu.VMEM((f // LANE, ext, LANE), F32)],
        compiler_params=_params("parallel", "arbitrary"),
    )(up, up, up, dg, dg, w, b)


def _sum_rows(name, g):
    ns, r, c = g.shape
    tr = _tile(r, 256)

    def body(g_ref, o_ref):
        acc = g_ref[0]
        for dev in range(1, ns):
            acc = acc + g_ref[dev]
        o_ref[...] = acc

    return pl.pallas_call(
        body, name=name, grid=(r // tr,),
        in_specs=[pl.BlockSpec((ns, tr, c), lambda i: (0, i, 0))],
        out_specs=pl.BlockSpec((tr, c), lambda i: (i, 0)),
        out_shape=jax.ShapeDtypeStruct((r, c), F32), compiler_params=_params("parallel"),
    )(g)


def _adamw(name, gsrc, w, m, v, layer=0, prev=None):
    ns, r, c = gsrc.shape
    nl = w.shape[0]
    tr = _tile(r, 256)
    prev = () if prev is None else tuple(prev)

    def body(g_ref, w_ref, m_ref, v_ref, *rest):
        go_ref, do_ref, mo_ref, vo_ref = rest[len(prev):]
        g = g_ref[0].astype(F32)
        for dev in range(1, ns):
            g = g + g_ref[dev].astype(F32)
        m_new = ADAM_B1 * m_ref[...] + (1.0 - ADAM_B1) * g
        v_new = ADAM_B2 * v_ref[...] + (1.0 - ADAM_B2) * (g * g)
        m_hat = m_new / (1.0 - ADAM_B1 ** ADAM_STEP)
        v_hat = v_new / (1.0 - ADAM_B2 ** ADAM_STEP)
        go_ref[...] = g
        do_ref[...] = -ADAM_LR * (m_hat / (jnp.sqrt(v_hat) + ADAM_EPS) + ADAM_WD * w_ref[...])
        mo_ref[...] = m_new
        vo_ref[...] = v_new

    row = pl.BlockSpec((None, tr, c), lambda i: (layer, i, 0))
    return pl.pallas_call(
        body, name=name, grid=(r // tr,),
        in_specs=[pl.BlockSpec((ns, tr, c), lambda i: (0, i, 0)), row, row, row] + [ANY] * len(prev),
        out_specs=[row] * 4, out_shape=[jax.ShapeDtypeStruct((nl, r, c), F32)] * 4,
        input_output_aliases={4 + i: i for i in range(len(prev))},
        compiler_params=_params("parallel"),
    )(gsrc, w, m, v, *prev)


def _ffn_forward(tag, r_in, h, get_wu, get_wd, wdw, bdw, seq, loss=None):
    t, d = r_in.shape
    tm = _tile(t, 512)
    wu = get_wu(h)
    dff = wu.shape[0] // 2
    tu = _tile(t, 1024)
    up = _mm(f"{tag}_up", h, wu, grid=(2, t // tu, 1),
             a_spec=pl.BlockSpec((tu, d), lambda j, i, k: (i, 0)),
             b_spec=pl.BlockSpec((dff, d), lambda j, i, k: (j, 0)),
             out_spec=pl.BlockSpec((None, tu, dff), lambda j, i, k: (j, i, 0)),
             out_shape=jax.ShapeDtypeStruct((2, t, dff), BF16), dims=NT, acc_shape=(tu, dff))
    wd = get_wd(up)
    g = _ffn_fwd(f"{tag}_act", up, wdw, bdw, seq)
    row = pl.BlockSpec((tm, d), lambda i, j, k: (i, 0))
    vec = pl.BlockSpec((1, d), lambda i, j, k: (0, 0))
    common = dict(grid=(t // tm, 1, 1), a_spec=pl.BlockSpec((tm, dff), lambda i, j, k: (i, 0)),
                  b_spec=pl.BlockSpec((dff, d), lambda i, j, k: (0, 0)), dims=NN, acc_shape=(tm, d))
    if loss is None:
        out = _mm(f"{tag}_down", g, wd, out_spec=row, out_shape=jax.ShapeDtypeStruct((t, d), F32),
                  extras=(r_in,), extra_specs=(row,), epilogue=lambda acc, ex, rows: ((ex[0][rows, :] + acc,), ()),
                  **common)
    else:
        def head(acc, ex, rows):
            dx, part, dgain = _loss_tile(ex[0][rows, :] + acc, ex[1][rows, :], ex[2][...])
            return (dx,), (part, dgain)

        out = _mm(f"{tag}_down", g, wd, out_spec=[row, pl.BlockSpec((1, 1), lambda i, j, k: (0, 0)), vec],
                  out_shape=[jax.ShapeDtypeStruct((t, d), F32), jax.ShapeDtypeStruct((1, 1), F32),
                             jax.ShapeDtypeStruct((1, d), F32)],
                  extras=(r_in, *loss), extra_specs=(row, row, vec), epilogue=head, n_sums=2, **common)
    return out, (r_in, h, up, g, wu, wd)


def _ffn_backward(tag, dr, saved, gain, wdw, bdw, seq, token=None, send_dwd=None):
    r_in, h, up, g, wu, wd = saved
    t, d = r_in.shape
    dff = wd.shape[0]
    tm = _tile(t, 512)
    tk = _tile(t, 2048)
    tku = _tile(t, 4096)
    cw = _ctile(dff, 1408)
    nc = dff // cw
    once = dict(pipeline_mode=pl.Buffered(1)) if tku == t else {}
    dg = _mm(f"{tag}_dg", dr, wd, grid=(t // tm, 1, 1),
             a_spec=pl.BlockSpec((tm, d), lambda i, j, k: (i, 0)),
             b_spec=pl.BlockSpec((dff, d), lambda i, j, k: (0, 0)),
             out_spec=pl.BlockSpec((tm, dff), lambda i, j, k: (i, 0)),
             out_shape=jax.ShapeDtypeStruct((t, dff), BF16), dims=NT, acc_shape=(tm, dff), token=token)
    dwd = _mm(f"{tag}_dwd", g, dr, grid=(dff // cw, 1, t // tk),
              a_spec=pl.BlockSpec((tk, cw), lambda i, j, k: (k, i)),
              b_spec=pl.BlockSpec((tk, d), lambda i, j, k: (k, 0)),
              out_spec=pl.BlockSpec((cw, d), lambda i, j, k: (i, 0)),
              out_shape=jax.ShapeDtypeStruct((dff, d), BF16), dims=TN, acc_shape=(cw, d))
    sent = None if send_dwd is None else send_dwd(dwd)
    dup, dwdw, dbdw = _ffn_bwd(f"{tag}_dact", up, dg, wdw, bdw, seq)
    row = pl.BlockSpec((tm, d), lambda i, j, k: (i, 0))
    vec = pl.BlockSpec((1, d), lambda i, j, k: (0, 0))

    def norm_backward(acc, ex, rows):
        dx, dgain, colsum = _rms_bwd_tile(acc, ex[0][rows, :], ex[1][...], ex[2][rows, :])
        return (dx,), (dgain, colsum)

    dr_in, dgain, colsum = _mm(
        f"{tag}_dh", dup, wu, grid=(t // tm, 1, 1),
        a_spec=pl.BlockSpec((2, tm, dff), lambda i, j, k: (0, i, 0)),
        b_spec=pl.BlockSpec((2 * dff, d), lambda i, j, k: (0, 0), pipeline_mode=pl.Buffered(1)),
        out_spec=[row, vec, vec],
        out_shape=[jax.ShapeDtypeStruct((t, d), F32)] + [jax.ShapeDtypeStruct((1, d), F32)] * 2,
        dims=NN, acc_shape=(tm, d), extras=(r_in, gain, dr), extra_specs=(row, vec, row),
        epilogue=norm_backward, n_sums=2, parts=2, token=sent)
    dwu = _mm(f"{tag}_dwu", dup, h, grid=(2 * nc, 1, t // tku),
              a_spec=pl.BlockSpec((None, tku, cw), lambda i, j, k: (i // nc, k, i % nc)),
              b_spec=pl.BlockSpec((tku, d), lambda i, j, k: (k, 0), **once),
              out_spec=pl.BlockSpec((cw, d), lambda i, j, k: (i, 0)),
              out_shape=jax.ShapeDtypeStruct((2 * dff, d), BF16), dims=TN, acc_shape=(cw, d))
    return dr_in, dgain, dwu, dwd, dwdw, dbdw, colsum


def _pad_to(vec, n):
    return jnp.pad(vec, (0, n - vec.shape[0]))


def _pack(parts, width):
    flat = jnp.concatenate([p.reshape(-1).astype(F32) for p in parts])
    n = -(-flat.shape[0] // (8 * width)) * (8 * width)
    return _pad_to(flat, n).reshape(n // width, width)


def _unpack(mat, shapes):
    flat = mat.reshape(-1)
    out, off = [], 0
    for s in shapes:
        n = 1
        for dim in s:
            n *= dim
        out.append(flat[off:off + n].reshape(s))
        off += n
    return out


def kernel(x, norm_mix, norm_ffn, conv_w_pw1, conv_b_pw1, conv_w_dw, conv_b_dw, conv_ln_g, conv_ln_b, conv_w_pw2, conv_b_pw2, pool_w, pool_b, pool_scale, ffn_w_up, ffn_w_dw, ffn_b_dw, ffn_w_down, final_norm, loss_target, m_norm_mix, m_norm_ffn, m_conv_w_pw1, m_conv_b_pw1, m_conv_w_dw, m_conv_b_dw, m_conv_ln_g, m_conv_ln_b, m_conv_w_pw2, m_conv_b_pw2, m_pool_w, m_pool_b, m_pool_scale, m_ffn_w_up, m_ffn_w_dw, m_ffn_b_dw, m_ffn_w_down, m_final_norm, v_norm_mix, v_norm_ffn, v_conv_w_pw1, v_conv_b_pw1, v_conv_w_dw, v_conv_b_dw, v_conv_ln_g, v_conv_ln_b, v_conv_w_pw2, v_conv_b_pw2, v_pool_w, v_pool_b, v_pool_scale, v_ffn_w_up, v_ffn_w_dw, v_ffn_b_dw, v_ffn_w_down, v_final_norm):
    bsz, seq, d = x.shape
    t = bsz * seq
    k_taps = conv_w_dw.shape[1]
    cs1 = conv_w_pw1.shape[2]
    dsh = d // N_DEV
    cg = d // N_GROUPS
    cgs = pool_w.shape[2]
    fu = ffn_w_up.shape[2]
    fd = ffn_w_down.shape[1]
    dff = fd * N_DEV
    nb = N_DEV // 2
    kf = ffn_w_dw.shape[1]
    fsh = ffn_w_dw.shape[2]
    my = _lin(_me())
    tm = _tile(t, 512)

    x2 = x.reshape(t, d)
    tgt2 = loss_target.reshape(t, d)

    small_shapes = [(k_taps, dsh), (dsh,), (dsh,), (2, kf, fsh)]
    small_mine = _pack([conv_w_dw[0], pool_b[0], pool_scale[0], ffn_w_dw], LANE)
    big = [conv_w_pw1[0], conv_w_pw2[0], ffn_w_up[0].T, ffn_w_down[0], pool_w[0], ffn_w_up[1].T, ffn_w_down[1]]
    gather = _exchange_start("gather_start", [small_mine] + [w.astype(BF16) for w in big], [GATHER2] * 8)
    h0 = _rms_fwd("l0_rms", x2, norm_mix[0:1])
    small_w = [norm_mix, norm_ffn, conv_b_pw1, conv_w_dw, conv_b_dw, conv_ln_g, conv_ln_b, conv_b_pw2, pool_b,
               pool_scale, ffn_w_dw, ffn_b_dw, final_norm]
    small_m = [m_norm_mix, m_norm_ffn, m_conv_b_pw1, m_conv_w_dw, m_conv_b_dw, m_conv_ln_g, m_conv_ln_b,
               m_conv_b_pw2, m_pool_b, m_pool_scale, m_ffn_w_dw, m_ffn_b_dw, m_final_norm]
    small_v = [v_norm_mix, v_norm_ffn, v_conv_b_pw1, v_conv_w_dw, v_conv_b_dw, v_conv_ln_g, v_conv_ln_b,
               v_conv_b_pw2, v_pool_b, v_pool_scale, v_ffn_w_dw, v_ffn_b_dw, v_final_norm]
    small_state = [_pack(group, 8 * LANE)[None] for group in (small_w, small_m, small_v)]
    forwarded = _exchange_forward("gather_forward_w1", gather, [0, 1], (h0, *small_state))
    small_all, w1 = _exchange_wait("gather_wait_w1", gather, [0, 1], forwarded)
    parts = [_unpack(small_all[dev], small_shapes) for dev in range(N_DEV)]
    wdw = jnp.concatenate([p[0] for p in parts], axis=1)
    pool_b_full = jnp.concatenate([p[1] for p in parts]).reshape(1, d)
    pool_s_full = jnp.concatenate([p[2] for p in parts]).reshape(1, d)
    fwdw = jnp.concatenate([p[3] for p in parts], axis=2)
    fbdw = ffn_b_dw.reshape(2, 1, dff)

    def columns(w):
        return w.transpose(1, 0, 2).reshape(w.shape[1], N_DEV * w.shape[2])

    def column_shards(w):
        return w.reshape(w.shape[0], N_DEV, w.shape[1] // N_DEV).transpose(1, 0, 2)

    w1 = columns(w1)
    a = _mm("l0_pw1", h0, w1, grid=(2, t // tm, 1),
            a_spec=pl.BlockSpec((tm, d), lambda j, i, k: (i, 0)),
            b_spec=pl.BlockSpec((d, d), lambda j, i, k: (0, j)),
            out_spec=pl.BlockSpec((None, tm, d), lambda j, i, k: (j, i, 0)),
            out_shape=jax.ShapeDtypeStruct((2, t, d), BF16), dims=NN, acc_shape=(tm, d),
            extras=(conv_b_pw1,), extra_specs=(pl.BlockSpec((1, d), lambda j, i, k: (0, j)),),
            epilogue=lambda acc, ex, rows: ((acc + ex[0][...],), ()))
    v = _conv_fwd("l0_conv", a, wdw, conv_b_dw, seq)
    forwarded = _exchange_forward("gather_forward_wu0", gather, [2, 3], v)
    (w2,) = _exchange_wait("gather_wait_w2", gather, [2], forwarded)
    w2 = w2.reshape(d, d)
    row = pl.BlockSpec((tm, d), lambda i, j, k: (i, 0))
    vec = pl.BlockSpec((1, d), lambda i, j, k: (0, 0))
    square = pl.BlockSpec((d, d), lambda i, j, k: (0, 0))

    def ln_silu(v_blk, ex):
        s_blk = _ln_silu_tile(v_blk, ex[0][...], ex[1][...]).astype(BF16)
        return s_blk, s_blk

    def residual_and_norm(acc, ex, rows):
        r_blk = ex[3][rows, :] + (acc + ex[2][...])
        return (r_blk, _rms(r_blk, ex[4][...])), ()

    r1, h1, s = _mm("l0_pw2", v, w2, grid=(t // tm, 1, 1), a_spec=row, b_spec=square, out_spec=[row, row, row],
                    out_shape=[jax.ShapeDtypeStruct((t, d), F32), jax.ShapeDtypeStruct((t, d), BF16),
                               jax.ShapeDtypeStruct((t, d), BF16)],
                    dims=NN, acc_shape=(tm, d), extras=(conv_ln_g, conv_ln_b, conv_b_pw2, x2, norm_ffn[0:1]),
                    extra_specs=(vec, vec, vec, row, vec), prologue=ln_silu, epilogue=residual_and_norm)

    def up_getter(name, idx):
        return lambda after: _exchange_wait(name, gather, [idx], after)[0].reshape(2 * dff, d)

    def down_getter(name, idx, forward=None):
        def get(after):
            if forward is not None:
                after = _exchange_forward(forward[0], gather, forward[1], after)
            return _exchange_wait(name, gather, [idx], after)[0].reshape(dff, d)
        return get

    r2, ffn0_saved = _ffn_forward("f0", r1, h1, up_getter("gather_wait_wu0", 3),
                                  down_getter("gather_wait_wd0", 4, ("gather_forward_wu1", [4, 5, 6])),
                                  fwdw[0], fbdw[0], seq)
    forwarded = _exchange_forward("gather_forward_wd1", gather, [7], r2)
    (wp,) = _exchange_wait("gather_wait_wp", gather, [5], forwarded)
    wp = wp.transpose(1, 0, 2, 3).reshape(N_GROUPS, cg, cg)
    pooled, r3, h3 = _pool_mix_fwd("l1_mix", r2, norm_mix[1:2], wp, pool_s_full, pool_b_full, norm_ffn[1:2], seq)
    (dr4, loss_part, dfinal), ffn1_saved = _ffn_forward(
        "f1", r3, h3, up_getter("gather_wait_wu1", 6), down_getter("gather_wait_wd1", 7), fwdw[1], fbdw[1], seq,
        loss=(tgt2, final_norm.reshape(1, d)))

    dr3, dnf1, dwu1, dwd1, dfw1, dfb1, _ = _ffn_backward("f1", dr4, ffn1_saved, norm_ffn[1:2], fwdw[1], fbdw[1], seq)
    scatter_a = _exchange_start("scatter_f1_start", [dwu1.reshape(N_DEV, fu, d), dwd1.reshape(N_DEV, fd, d)],
                                [SCATTER, SCATTER])
    dr2, dwp, dpool_s, dpool_b, dnm1 = _pool_mix_bwd(
        "l1_dmix", pooled, wp, dr3, r2, norm_mix[1:2], pool_s_full + scatter_a["token"][0:1, 0:1], pool_b_full, seq)
    early = {}

    def send_dwd0(dwd):
        early["down"] = _exchange_start("scatter_f0_down_start", [dwd.reshape(N_DEV, fd, d)], [SCATTER])
        return early["down"]["token"]

    dr1, dnf0, dwu0, dwd0, dfw0, dfb0, db2 = _ffn_backward("f0", dr2, ffn0_saved, norm_ffn[0:1], fwdw[0], fbdw[0], seq,
                                                           send_dwd=send_dwd0)
    dwp_b = dwp.astype(BF16).reshape(N_GROUPS, N_DEV, cgs, cg).transpose(1, 0, 2, 3)
    tk = _tile(t, 2048)
    dw2 = _mm("l0_dw2", s, dr1, grid=(1, 1, t // tk),
              a_spec=pl.BlockSpec((tk, d), lambda i, j, k: (k, 0)),
              b_spec=pl.BlockSpec((tk, d), lambda i, j, k: (k, 0)),
              out_spec=pl.BlockSpec((d, d), lambda i, j, k: (0, 0)),
              out_shape=jax.ShapeDtypeStruct((d, d), BF16), dims=TN, acc_shape=(d, d))
    scatter_b = _exchange_start("scatter_f0_start", [dwu0.reshape(N_DEV, fu, d), dwp_b,
                                                     dw2.reshape(N_DEV, d // N_DEV, d)], [SCATTER] * 3)

    def ln_silu_backward(acc, ex, rows):
        dv_blk, dgain, dbias, colsum = _ln_silu_bwd_tile(acc, ex[0][rows, :], ex[1][...], ex[2][...])
        return (dv_blk,), (dgain, dbias, colsum)

    dv, dlg, dlb, dbdw = _mm("l0_ds", dr1, w2, grid=(t // tm, 1, 1), a_spec=row, b_spec=square,
                             out_spec=[row, vec, vec, vec],
                             out_shape=[jax.ShapeDtypeStruct((t, d), F32)] + [jax.ShapeDtypeStruct((1, d), F32)] * 3,
                             dims=NT, acc_shape=(tm, d), extras=(v, conv_ln_g, conv_ln_b), extra_specs=(row, vec, vec),
                             epilogue=ln_silu_backward, n_sums=3, token=scatter_b["token"])
    da, dwdw, db1 = _conv_bwd("l0_dconv", a, dv, wdw, seq)
    tk1 = _tile(t, 4096)
    once = dict(pipeline_mode=pl.Buffered(1)) if tk1 == t else {}
    dw1 = _mm("l0_dw1", h0, da, grid=(1, 2, t // tk1),
              a_spec=pl.BlockSpec((tk1, d), lambda i, j, k: (k, 0), **once),
              b_spec=pl.BlockSpec((None, tk1, d), lambda i, j, k: (j, k, 0)),
              out_spec=pl.BlockSpec((d, d), lambda i, j, k: (0, j)),
              out_shape=jax.ShapeDtypeStruct((d, 2 * d), BF16), dims=TN, acc_shape=(d, d))
    scatter_c = _exchange_start("scatter_l0_start", [column_shards(dw1)], [SCATTER])
    def norm_backward(acc, ex, rows):
        dx_blk, dgain, colsum = _rms_bwd_tile(acc, ex[0][rows, :], ex[1][...], ex[2][rows, :])
        return (dx_blk,), (dgain, colsum)

    dx, dnm0, _ = _mm("l0_dh", da, w1, grid=(t // tm, 1, 1),
                      a_spec=pl.BlockSpec((2, tm, d), lambda i, j, k: (0, i, 0)),
                      b_spec=pl.BlockSpec((d, 2 * d), lambda i, j, k: (0, 0), pipeline_mode=pl.Buffered(1)),
                      out_spec=[row, vec, vec],
                      out_shape=[jax.ShapeDtypeStruct((t, d), F32)] + [jax.ShapeDtypeStruct((1, d), F32)] * 2,
                      dims=NT, acc_shape=(tm, d), extras=(x2, norm_mix[0:1], dr1), extra_specs=(row, vec, row),
                      epilogue=norm_backward, n_sums=2, parts=2, token=scatter_c["token"])

    dffn_w = jnp.stack([dfw0, dfw1])
    dffn_b = jnp.stack([dfb0, dfb1]).reshape(2, dff)
    small_parts = [loss_part, jnp.concatenate([dnm0, dnm1]), jnp.concatenate([dnf0, dnf1]), db1, dwdw, dbdw, dlg, dlb,
                   db2, dpool_b, dpool_s, dffn_w, dffn_b, dfinal]
    small_part_shapes = [(1,), (2, d), (2, d), (1, 2 * d), (k_taps, d), (1, d), (1, d), (1, d), (1, d), (1, d), (1, d),
                         (2, kf, dff), (2, dff), (d,)]
    packed = _pack(small_parts, 8 * LANE)
    gather_small = _exchange_start("gather_small_start", [packed], [GATHER])

    def big_update(name, recv, w, m, v, layer=0, prev=None):
        shape = w.shape
        c = recv.shape[-1]
        rows = recv.size // (N_DEV * c)
        nl = w.size // (rows * c)
        outs = _adamw(name, recv.reshape(N_DEV, rows, c), w.reshape(nl, rows, c), m.reshape(nl, rows, c),
                      v.reshape(nl, rows, c), layer, prev)
        return outs, [o.reshape(shape) for o in outs]

    wu_t = [p.transpose(0, 2, 1) for p in (ffn_w_up, m_ffn_w_up, v_ffn_w_up)]
    g_wu1, g_wd1 = _exchange_wait("scatter_f1_wait", scatter_a, [0, 1], gather_small["token"])
    raw_wu, _ = big_update("adam_wu1", g_wu1, *wu_t, 1)
    raw_wd, _ = big_update("adam_wd1", g_wd1, ffn_w_down, m_ffn_w_down, v_ffn_w_down, 1)
    (g_wd0,) = _exchange_wait("scatter_f0_down_wait", early["down"], [0], raw_wd[0])
    g_wu0, g_wp, g_w2 = _exchange_wait("scatter_f0_wait", scatter_b, [0, 1, 2], g_wd0)
    _, u_wu = big_update("adam_wu0", g_wu0, *wu_t, 0, raw_wu)
    u_wu = [o.transpose(0, 2, 1) for o in u_wu]
    _, u_wd = big_update("adam_wd0", g_wd0, ffn_w_down, m_ffn_w_down, v_ffn_w_down, 0, raw_wd)
    _, u_wp = big_update("adam_wp", g_wp, pool_w, m_pool_w, v_pool_w)
    (g_w1,) = _exchange_wait("scatter_l0_wait", scatter_c, [0], u_wp[0])
    _, u_w1 = big_update("adam_w1", g_w1, conv_w_pw1, m_conv_w_pw1, v_conv_w_pw1)
    _, u_w2 = big_update("adam_w2", g_w2, conv_w_pw2, m_conv_w_pw2, v_conv_w_pw2)
    (all_small,) = _exchange_wait("gather_small_wait", gather_small, [0], u_w2[0])
    summed = _sum_rows("sum_small_grads", all_small)
    (loss_v, g_nm, g_nf, g_b1, g_wdw, g_bdw, g_lg, g_lb, g_b2, g_pb, g_ps, g_fw, g_fb,
     g_fin) = _unpack(summed, small_part_shapes)
    loss = loss_v[0]
    g_wdw_mine = lax.dynamic_slice_in_dim(g_wdw, my * dsh, dsh, axis=1)[None]
    g_pb_mine = lax.dynamic_slice_in_dim(g_pb, my * dsh, dsh, axis=1)
    g_ps_mine = lax.dynamic_slice_in_dim(g_ps, my * dsh, dsh, axis=1)
    g_fw_mine = lax.dynamic_slice_in_dim(g_fw, my * fsh, fsh, axis=2)

    small_g = [g_nm, g_nf, g_b1, g_wdw_mine, g_bdw, g_lg, g_lb, g_b2, g_pb_mine, g_ps_mine, g_fw_mine, g_fb, g_fin]
    shapes = [w.shape for w in small_w]
    outs = _adamw("adam_small", _pack(small_g, 8 * LANE)[None], *small_state)
    sg, sd, sm, sv = [_unpack(o, shapes) for o in outs]

    def leaf(kind):
        (nm, nf, b1, wdw_, bdw_, lg, lb, b2, pb, ps, fw, fb, fin) = (sg, sd, sm, sv)[kind]
        return [nm, nf, u_w1[kind], b1, wdw_, bdw_, lg, lb, u_w2[kind], b2, u_wp[kind], pb, ps, u_wu[kind], fw, fb,
                u_wd[kind], fin]

    return (loss, dx.reshape(bsz, seq, d), *leaf(0), *leaf(1), *leaf(2), *leaf(3))
```

```python
import functools

import jax
import jax.numpy as jnp
from jax import lax
from jax.experimental import pallas as pl
from jax.experimental.pallas import tpu as pltpu

F32 = jnp.float32
BF16 = jnp.bfloat16
MESH = pl.DeviceIdType.MESH
HBM = pl.BlockSpec(memory_space=pltpu.HBM)

N_DEV = 8
RMS_EPS = 1e-6
LN_EPS = 1e-5
POOL_WINDOWS = (2, 4, 8, 16)
N_GROUPS = len(POOL_WINDOWS)
ADAM_LR = 0.001
ADAM_B1 = 0.9
ADAM_B2 = 0.999
ADAM_EPS = 1e-08
ADAM_WD = 0.01
ADAM_STEP = 10

LANE = 128
HALO = 32
HALO16 = 16
VMEM_LIMIT = 56 * 1024 * 1024


def _params(*sem):
    return pltpu.CompilerParams(dimension_semantics=sem if sem else None, vmem_limit_bytes=VMEM_LIMIT)


def _tile(n, pref):
    for t in range(min(pref, n), 15, -1):
        if n % t == 0 and t % 16 == 0:
            return t
    return n


def _sigmoid(z):
    return 1.0 / (1.0 + jnp.exp(-z))


def _me():
    return lax.axis_index("x"), lax.axis_index("y"), lax.axis_index("c")


def _flip(pos, m):
    x, y, c = pos
    return ((1 - x) if m & 4 else x, (1 - y) if m & 2 else y, (1 - c) if m & 1 else c)


def _lin(pos):
    return 4 * pos[0] + 2 * pos[1] + pos[2]


SEM = pl.BlockSpec(memory_space=pltpu.SEMAPHORE)
ANY = pl.BlockSpec(memory_space=pl.ANY)
EFFECT = pltpu.SideEffectType.DATAFLOW_SIDE_EFFECTING


SCATTER = "scatter"
GATHER = "gather"
GATHER2 = "gather2"
ALL_MASKS = (1, 2, 3, 4, 5, 6, 7)
SIBLING = 1
CHIPS = (2, 4, 6)


class _Copies:
    def __init__(self, a, mode, src, land, send_sems, recv_sems):
        self.a, self.mode, self.src, self.land = a, mode, src, land
        self.send_sems, self.recv_sems = send_sems, recv_sems
        self.me = _me()
        self.first = (SIBLING,) + CHIPS if mode == GATHER2 else ALL_MASKS

    def _sems(self, m, to):
        return dict(send_sem=self.send_sems.at[self.a * N_DEV + m], recv_sem=self.recv_sems.at[self.a * N_DEV + m],
                    device_id=to, device_id_type=MESH)

    def _block(self, pid):
        return self.src.at[pid] if self.mode == SCATTER else self.src

    def local(self):
        my = _lin(self.me)
        return pltpu.make_async_copy(self._block(my), self.land.at[my], self.send_sems.at[self.a * N_DEV])

    def send(self, m):
        peer = _flip(self.me, m)
        return pltpu.make_async_remote_copy(src_ref=self._block(_lin(peer)), dst_ref=self.land.at[_lin(self.me)],
                                            **self._sems(m, peer))

    def arrival(self, m):
        rows = self.land.at[_lin(_flip(self.me, m))]
        return pltpu.make_async_remote_copy(src_ref=rows, dst_ref=rows, **self._sems(m, _flip(self.me, m)))

    def forward(self, m):
        rows = self.land.at[_lin(_flip(self.me, m))]
        return pltpu.make_async_remote_copy(src_ref=rows, dst_ref=rows, **self._sems(m | 1, _flip(self.me, SIBLING)))


def _exchange_start(name, arrs, modes):
    n = len(arrs)
    blocks = [a.shape[1:] if md == SCATTER else a.shape for a, md in zip(arrs, modes)]

    def body(*refs):
        srcs, lands = refs[:n], refs[n:2 * n]
        send_sems, recv_sems = refs[2 * n], refs[2 * n + 1]
        token = refs[-1]
        for a in range(n):
            cp = _Copies(a, modes[a], srcs[a], lands[a], send_sems, recv_sems)
            cp.local().start()
            for m in cp.first:
                cp.send(m).start()
        token[...] = jnp.zeros_like(token)

    lands = [lax.empty((N_DEV,) + tuple(b), a.dtype) for a, b in zip(arrs, blocks)]
    outs = pl.pallas_call(
        body, name=name,
        out_shape=(pltpu.SemaphoreType.DMA((n * N_DEV,)), pltpu.SemaphoreType.DMA((n * N_DEV,)),
                   *[pltpu.HBM(a.shape, a.dtype) for a in arrs], *[pltpu.HBM(l.shape, l.dtype) for l in lands],
                   jax.ShapeDtypeStruct((8, LANE), F32)),
        in_specs=[HBM] * (2 * n),
        out_specs=(SEM, SEM, *[HBM] * (2 * n), pl.BlockSpec(memory_space=pltpu.VMEM)),
        input_output_aliases={i: 2 + i for i in range(2 * n)},
        compiler_params=pltpu.CompilerParams(has_side_effects=EFFECT),
    )(*[pltpu.with_memory_space_constraint(a, pltpu.HBM) for a in arrs],
      *[pltpu.with_memory_space_constraint(l, pltpu.HBM) for l in lands])
    return dict(send=outs[0], recv=outs[1], srcs=list(outs[2:2 + n]), lands=list(outs[2 + n:2 + 2 * n]),
                modes=modes, token=outs[-1])


def _exchange_forward(name, handle, which, after):
    k = len(which)

    def half(wait):
        def body(*refs):
            lands = refs[:k]
            send_sems, recv_sems = refs[k], refs[k + 1]
            token = refs[-1]
            for pos, a in enumerate(which):
                cp = _Copies(a, GATHER2, None, lands[pos], send_sems, recv_sems)
                for m in CHIPS:
                    if wait:
                        cp.arrival(m).wait_recv()
                    else:
                        cp.forward(m).start()
            token[...] = jnp.zeros_like(token)
        return body

    def call(body, call_name, lands, after):
        after = tuple(after) if isinstance(after, (tuple, list)) else (after,)
        outs = pl.pallas_call(
            body, name=call_name,
            out_shape=(*[pltpu.HBM(x.shape, x.dtype) for x in lands], jax.ShapeDtypeStruct((8, LANE), F32)),
            in_specs=[HBM] * k + [SEM, SEM] + [ANY] * len(after),
            out_specs=(*[HBM] * k, pl.BlockSpec(memory_space=pltpu.VMEM)),
            input_output_aliases={i: i for i in range(k)},
            compiler_params=pltpu.CompilerParams(has_side_effects=EFFECT),
        )(*lands, handle["send"], handle["recv"], *after)
        return list(outs[:k]), outs[-1]

    lands, arrived = call(half(True), name + "_arrived", [handle["lands"][a] for a in which], after)
    lands, token = call(half(False), name, lands, arrived)
    for pos, a in enumerate(which):
        handle["lands"][a] = lands[pos]
    return token


def _exchange_wait(name, handle, which, after):
    k = len(which)
    modes = handle["modes"]

    def body(*refs):
        srcs, lands = refs[:k], refs[k:2 * k]
        send_sems, recv_sems = refs[2 * k], refs[2 * k + 1]
        for pos, a in enumerate(which):
            cp = _Copies(a, modes[a], srcs[pos], lands[pos], send_sems, recv_sems)
            cp.local().wait()
            for m in cp.first:
                cp.send(m).wait_send()
            if modes[a] == GATHER2:
                for m in CHIPS:
                    cp.forward(m).wait_send()
                arrivals = (SIBLING,) + tuple(m | 1 for m in CHIPS)
            else:
                arrivals = ALL_MASKS
            for m in arrivals:
                cp.arrival(m).wait_recv()

    srcs = [handle["srcs"][a] for a in which]
    lands = [handle["lands"][a] for a in which]
    outs = pl.pallas_call(
        body, name=name,
        out_shape=tuple(pltpu.HBM(x.shape, x.dtype) for x in srcs + lands),
        in_specs=[HBM] * (2 * k) + [SEM, SEM, ANY], out_specs=tuple([HBM] * (2 * k)),
        input_output_aliases={i: i for i in range(2 * k)},
        compiler_params=pltpu.CompilerParams(has_side_effects=EFFECT),
    )(*srcs, *lands, handle["send"], handle["recv"], after)
    for pos, a in enumerate(which):
        handle["srcs"][a], handle["lands"][a] = outs[pos], outs[k + pos]
    return list(outs[k:])


def _mm(name, a, b, *, grid, a_spec, b_spec, out_spec, out_shape, dims, acc_shape, extras=(), extra_specs=(),
        epilogue=None, token=None, prologue=None, n_sums=0, parts=1):
    nk = grid[2]
    ne = len(extras)
    deps = () if token is None else (token,)
    dep_specs = [pl.BlockSpec((8, LANE), lambda i, j, k: (0, 0))] * len(deps)
    n_out = len(out_shape) if isinstance(out_shape, (list, tuple)) else 1
    n_tiles = n_out - n_sums - (1 if prologue is not None else 0)

    def body(a_ref, b_ref, *rest):
        ex, o_refs, acc_ref = rest[:ne], rest[ne + len(deps):ne + len(deps) + n_out], rest[ne + len(deps) + n_out]
        k = pl.program_id(2)
        if parts == 1:
            a_blk, saved = a_ref[...], None
            if prologue is not None:
                a_blk, saved = prologue(a_blk, ex)
                o_refs[n_tiles][...] = saved
            part = lax.dot_general(a_blk.astype(BF16), b_ref[...].astype(BF16), (dims, ((), ())),
                                   preferred_element_type=F32)
        else:
            kb = b_ref.shape[dims[1][0]] // parts
            part = None
            for p in range(parts):
                b_blk = b_ref[p * kb:(p + 1) * kb, :] if dims[1][0] == 0 else b_ref[:, p * kb:(p + 1) * kb]
                term = lax.dot_general(a_ref[p].astype(BF16), b_blk.astype(BF16), (dims, ((), ())),
                                       preferred_element_type=F32)
                part = term if part is None else part + term
        sum_refs = o_refs[n_out - n_sums:]

        def add_sums(terms):
            @pl.when((pl.program_id(0) == 0) & (pl.program_id(1) == 0))
            def _():
                for o_ref in sum_refs:
                    o_ref[...] = jnp.zeros_like(o_ref)

            for o_ref, term in zip(sum_refs, terms):
                o_ref[...] += jnp.sum(term, axis=0, keepdims=True)

        def finish(r):
            tiles, terms = ((r,), ()) if epilogue is None else epilogue(r, ex, slice(None))
            for o_ref, val in zip(o_refs, tiles):
                o_ref[...] = val.astype(o_ref.dtype)
            if n_sums:
                add_sums(terms)

        if nk == 1:
            finish(part)
            return

        @pl.when(k == 0)
        def _():
            acc_ref[...] = part

        @pl.when((k > 0) & (k < nk - 1))
        def _():
            acc_ref[...] += part

        @pl.when(k == nk - 1)
        def _():
            finish(acc_ref[...] + part)

    return pl.pallas_call(
        body, name=name, grid=grid, in_specs=[a_spec, b_spec, *extra_specs, *dep_specs], out_specs=out_spec,
        out_shape=out_shape, scratch_shapes=[pltpu.VMEM(acc_shape if nk > 1 else (8, LANE), F32)],
        compiler_params=_params(*(("arbitrary",) * 3 if n_sums else ("parallel", "parallel", "arbitrary"))),
    )(a, b, *extras, *deps)


def _rms(x, gain):
    return x * lax.rsqrt(jnp.mean(x * x, axis=-1, keepdims=True) + RMS_EPS) * gain


def _rms_bwd_tile(dh, x, gain, dres):
    rstd = lax.rsqrt(jnp.mean(x * x, axis=-1, keepdims=True) + RMS_EPS)
    xhat = x * rstd
    dxhat = dh * gain
    dx = dres + rstd * (dxhat - xhat * jnp.mean(dxhat * xhat, axis=-1, keepdims=True))
    return dx, dh * xhat, dx


def _ln_silu_tile(v, g, b):
    mu = jnp.mean(v, axis=-1, keepdims=True)
    cen = v - mu
    z = cen * lax.rsqrt(jnp.mean(cen * cen, axis=-1, keepdims=True) + LN_EPS) * g + b
    return z * _sigmoid(z)


def _ln_silu_bwd_tile(ds, v, g, b):
    mu = jnp.mean(v, axis=-1, keepdims=True)
    cen = v - mu
    rstd = lax.rsqrt(jnp.mean(cen * cen, axis=-1, keepdims=True) + LN_EPS)
    y = cen * rstd
    z = y * g + b
    sig = _sigmoid(z)
    dz = ds * sig * (1.0 + z * (1.0 - sig))
    dy = dz * g
    dv = rstd * (dy - jnp.mean(dy, axis=-1, keepdims=True) - y * jnp.mean(dy * y, axis=-1, keepdims=True))
    return dv, dz * y, dz, dv


def _loss_tile(x, tgt, gain):
    d = x.shape[-1]
    rstd = lax.rsqrt(jnp.mean(x * x, axis=-1, keepdims=True) + RMS_EPS)
    xhat = x * rstd
    err = xhat * gain - tgt
    dy = err / d
    dxhat = dy * gain
    dx = rstd * (dxhat - xhat * jnp.mean(dxhat * xhat, axis=-1, keepdims=True))
    return dx, 0.5 * jnp.mean(err * err, axis=-1, keepdims=True), dy * xhat


NN = ((1,), (0,))
NT = ((1,), (1,))
TN = ((0,), (0,))


def _rms_fwd(name, x, gain):
    t, d = x.shape
    tr = _tile(t, 512)

    def body(x_ref, g_ref, h_ref):
        h_ref[...] = _rms(x_ref[...], g_ref[...]).astype(BF16)

    return pl.pallas_call(
        body, name=name, grid=(t // tr,),
        in_specs=[pl.BlockSpec((tr, d), lambda i: (i, 0)), pl.BlockSpec((1, d), lambda i: (0, 0))],
        out_specs=pl.BlockSpec((tr, d), lambda i: (i, 0)),
        out_shape=jax.ShapeDtypeStruct((t, d), BF16), compiler_params=_params("parallel"),
    )(x, gain)


def _conv_tiles(t, seq):
    ts = _tile(seq, 2048)
    return ts, seq // ts, _tile(ts, 64)


def _conv_fwd(name, a, w, b, seq):
    _, t, d = a.shape
    k_taps = w.shape[0]
    ts, tps, rc = _conv_tiles(t, seq)
    hb = ts // HALO

    def body(cur_ref, prev_ref, w_ref, b_ref, v_ref, upad):
        i = pl.program_id(1)
        first = (i % tps) == 0
        pv = prev_ref[0].astype(F32)
        pg = prev_ref[1].astype(F32)
        upad[0:HALO, :] = jnp.where(first, 0.0, pv * _sigmoid(pg))
        upad[HALO:HALO + ts, :] = cur_ref[0].astype(F32) * _sigmoid(cur_ref[1].astype(F32))
        wv = w_ref[...]
        bias = jnp.broadcast_to(b_ref[...], (rc, LANE))
        for r0 in range(0, ts, rc):
            acc = bias
            for k in range(k_taps):
                acc = acc + wv[k:k + 1, :] * upad[pl.ds(HALO - (k_taps - 1) + k + r0, rc), :]
            v_ref[pl.ds(r0, rc), :] = acc

    return pl.pallas_call(
        body, name=name, grid=(d // LANE, t // ts),
        in_specs=[pl.BlockSpec((2, ts, LANE), lambda c, i: (0, i, c)),
                  pl.BlockSpec((2, HALO, LANE), lambda c, i: (0, jnp.maximum(i * hb - 1, 0), c)),
                  pl.BlockSpec((k_taps, LANE), lambda c, i: (0, c)),
                  pl.BlockSpec((1, LANE), lambda c, i: (0, c))],
        out_specs=pl.BlockSpec((ts, LANE), lambda c, i: (i, c)),
        out_shape=jax.ShapeDtypeStruct((t, d), F32),
        scratch_shapes=[pltpu.VMEM((HALO + ts, LANE), F32)],
        compiler_params=_params("parallel", "parallel"),
    )(a, a, w, b)


def _conv_bwd(name, a, dv, w, seq):
    _, t, d = a.shape
    k_taps = w.shape[0]
    ts, tps, rc = _conv_tiles(t, seq)
    hb = ts // HALO
    nhb = t // HALO

    def body(cur_ref, prev_ref, dv_ref, ndv_ref, w_ref, da_ref, dw_ref, dbp_ref, upad, dvpad, dwrows):
        i = pl.program_id(1)
        first = (i % tps) == 0
        last = (i % tps) == tps - 1
        pv = prev_ref[0].astype(F32)
        pg = prev_ref[1].astype(F32)
        upad[0:HALO, :] = jnp.where(first, 0.0, pv * _sigmoid(pg))
        upad[HALO:HALO + ts, :] = cur_ref[0].astype(F32) * _sigmoid(cur_ref[1].astype(F32))
        dvpad[0:ts, :] = dv_ref[...]
        dvpad[ts:ts + HALO, :] = jnp.where(last, 0.0, ndv_ref[...])
        wv = w_ref[...]

        @pl.when(i == 0)
        def _():
            dw_ref[...] = jnp.zeros_like(dw_ref)
            dbp_ref[...] = jnp.zeros_like(dbp_ref)

        sv = jnp.zeros((1, LANE), F32)
        sg = jnp.zeros((1, LANE), F32)
        for r0 in range(0, ts, rc):
            du = jnp.zeros((rc, LANE), F32)
            for k in range(k_taps):
                du = du + wv[k:k + 1, :] * dvpad[pl.ds(r0 + (k_taps - 1) - k, rc), :]
            av = cur_ref[0, pl.ds(r0, rc), :].astype(F32)
            sig = _sigmoid(cur_ref[1, pl.ds(r0, rc), :].astype(F32))
            dval = du * sig
            dgate = du * av * sig * (1.0 - sig)
            da_ref[0, pl.ds(r0, rc), :] = dval.astype(BF16)
            da_ref[1, pl.ds(r0, rc), :] = dgate.astype(BF16)
            sv = sv + jnp.sum(dval, axis=0, keepdims=True)
            sg = sg + jnp.sum(dgate, axis=0, keepdims=True)
        dbp_ref[0] += sv
        dbp_ref[1] += sg

        for k in range(k_taps):
            acc = jnp.zeros((rc, LANE), F32)
            for r0 in range(0, ts, rc):
                acc = acc + dvpad[pl.ds(r0, rc), :] * upad[pl.ds(HALO - (k_taps - 1) + k + r0, rc), :]
            dwrows[k:k + 1, :] = jnp.sum(acc, axis=0, keepdims=True)
        dw_ref[...] += dwrows[0:k_taps, :]

    return pl.pallas_call(
        body, name=name, grid=(d // LANE, t // ts),
        in_specs=[pl.BlockSpec((2, ts, LANE), lambda c, i: (0, i, c)),
                  pl.BlockSpec((2, HALO, LANE), lambda c, i: (0, jnp.maximum(i * hb - 1, 0), c)),
                  pl.BlockSpec((ts, LANE), lambda c, i: (i, c)),
                  pl.BlockSpec((HALO, LANE), lambda c, i: (jnp.minimum((i + 1) * hb, nhb - 1), c)),
                  pl.BlockSpec((k_taps, LANE), lambda c, i: (0, c))],
        out_specs=[pl.BlockSpec((2, ts, LANE), lambda c, i: (0, i, c)),
                   pl.BlockSpec((k_taps, LANE), lambda c, i: (0, c)),
                   pl.BlockSpec((2, 1, LANE), lambda c, i: (0, 0, c))],
        out_shape=[jax.ShapeDtypeStruct((2, t, d), BF16), jax.ShapeDtypeStruct((k_taps, d), F32),
                   jax.ShapeDtypeStruct((2, 1, d), F32)],
        scratch_shapes=[pltpu.VMEM((HALO + ts, LANE), F32), pltpu.VMEM((ts + HALO, LANE), F32),
                        pltpu.VMEM((HALO, LANE), F32)],
        compiler_params=_params("parallel", "arbitrary"),
    )(a, a, dv, dv, w)


def _pool_mix_fwd(name, x, gain, wp, scale, bias, next_gain, seq):
    t, d = x.shape
    ts = _tile(seq, 512)
    tps = seq // ts
    hb = ts // HALO
    cg = d // N_GROUPS
    sw = min(cg, LANE)

    def body(cur_ref, prev_ref, g_ref, w_ref, s_ref, b_ref, ng_ref, p_ref, r_ref, h_ref, hpad):
        i = pl.program_id(0)
        first = (i % tps) == 0
        g = g_ref[...]
        h_prev = jnp.where(first, 0.0, _rms(prev_ref[...], g))
        h_cur = _rms(cur_ref[...], g)
        for si in range(d // sw):
            hpad[si, 0:HALO, :] = h_prev[:, si * sw:(si + 1) * sw]
            hpad[si, HALO:HALO + ts, :] = h_cur[:, si * sw:(si + 1) * sw]
        pos = (i % tps) * ts + lax.broadcasted_iota(jnp.int32, (ts, 1), 0)
        for gi, win in enumerate(POOL_WINDOWS):
            sl = slice(gi * cg, (gi + 1) * cg)
            cnt = jnp.minimum(pos + 1, win).astype(F32)
            for si in range(gi * cg // sw, (gi + 1) * cg // sw):
                own = hpad[si, HALO:HALO + ts, :]
                acc = own
                for j in range(1, win):
                    acc = acc + hpad[si, pl.ds(HALO - j, ts), :]
                p_ref[:, si * sw:(si + 1) * sw] = (acc / cnt - own).astype(BF16)
            mixed = jnp.dot(p_ref[:, sl], w_ref[gi], preferred_element_type=F32)
            r_ref[:, sl] = cur_ref[:, sl] + s_ref[:, sl] * (mixed + b_ref[:, sl])
        h_ref[...] = _rms(r_ref[...], ng_ref[...]).astype(BF16)

    row = pl.BlockSpec((ts, d), lambda i: (i, 0))
    vec = pl.BlockSpec((1, d), lambda i: (0, 0))
    return pl.pallas_call(
        body, name=name, grid=(t // ts,),
        in_specs=[row, pl.BlockSpec((HALO, d), lambda i: (jnp.maximum(i * hb - 1, 0), 0)), vec,
                  pl.BlockSpec((N_GROUPS, cg, cg), lambda i: (0, 0, 0)), vec, vec, vec],
        out_specs=[row, row, row],
        out_shape=[jax.ShapeDtypeStruct((t, d), BF16), jax.ShapeDtypeStruct((t, d), F32),
                   jax.ShapeDtypeStruct((t, d), BF16)],
        scratch_shapes=[pltpu.VMEM((d // sw, HALO + ts, sw), F32)],
        compiler_params=_params("parallel"),
    )(x, x, gain, wp, scale, bias, next_gain)


def _pool_mix_bwd(name, pooled, wp, dr, x, gain, scale, bias, seq):
    t, d = x.shape
    ts = _tile(seq, 512)
    tps = seq // ts
    hb = ts // HALO
    nhb = t // HALO
    cg = d // N_GROUPS
    sw = min(cg, LANE)

    def body(p_ref, w_ref, dr_ref, ndr_ref, x_ref, g_ref, s_ref, b_ref, dx_ref, dx16_ref, dw_ref, ds_ref, db_ref,
             dg_ref, qpad, dh):
        i = pl.program_id(0)
        last = (i % tps) == tps - 1
        pos = (i % tps) * ts + lax.broadcasted_iota(jnp.int32, (ts, 1), 0)

        @pl.when(i == 0)
        def _():
            dw_ref[...] = jnp.zeros_like(dw_ref)
            ds_ref[...] = jnp.zeros_like(ds_ref)
            db_ref[...] = jnp.zeros_like(db_ref)
            dg_ref[...] = jnp.zeros_like(dg_ref)

        for gi, win in enumerate(POOL_WINDOWS):
            sl = slice(gi * cg, (gi + 1) * cg)
            wv = w_ref[gi]
            sc = s_ref[:, sl]
            drv = dr_ref[:, sl]
            dmx = drv * sc
            dmx16 = dmx.astype(BF16)
            pooled = p_ref[:, sl]
            dw_ref[gi] += lax.dot_general(pooled, dmx16, (TN, ((), ())), preferred_element_type=F32)
            mixed = jnp.dot(pooled, wv, preferred_element_type=F32)
            ds_ref[:, sl] += jnp.sum(drv * (mixed + b_ref[:, sl]), axis=0, keepdims=True)
            db_ref[:, sl] += jnp.sum(dmx, axis=0, keepdims=True)
            cur = lax.dot_general(dmx16, wv, (NT, ((), ())), preferred_element_type=F32)
            nxt = lax.dot_general((ndr_ref[:, sl] * sc).astype(BF16), wv, (NT, ((), ())),
                                  preferred_element_type=F32)
            q_cur = cur / jnp.minimum(pos + 1, win).astype(F32)
            q_nxt = jnp.where(last, 0.0, nxt / float(win))
            for k, si in enumerate(range(gi * cg // sw, (gi + 1) * cg // sw)):
                part = slice(k * sw, (k + 1) * sw)
                qpad[si, 0:ts, :] = q_cur[:, part]
                qpad[si, ts:ts + HALO, :] = q_nxt[:, part]
                acc = -cur[:, part]
                for j in range(win):
                    acc = acc + qpad[si, pl.ds(j, ts), :]
                dh[:, si * sw:(si + 1) * sw] = acc
        dx, dgain_term, _ = _rms_bwd_tile(dh[...], x_ref[...], g_ref[...], dr_ref[...])
        dx_ref[...] = dx
        dx16_ref[...] = dx.astype(BF16)
        dg_ref[...] += jnp.sum(dgain_term, axis=0, keepdims=True)

    row = pl.BlockSpec((ts, d), lambda i: (i, 0))
    vec = pl.BlockSpec((1, d), lambda i: (0, 0))
    return pl.pallas_call(
        body, name=name, grid=(t // ts,),
        in_specs=[row, pl.BlockSpec((N_GROUPS, cg, cg), lambda i: (0, 0, 0)), row,
                  pl.BlockSpec((HALO, d), lambda i: (jnp.minimum((i + 1) * hb, nhb - 1), 0)), row, vec, vec, vec],
        out_specs=[row, row, pl.BlockSpec((N_GROUPS, cg, cg), lambda i: (0, 0, 0)), vec, vec, vec],
        out_shape=[jax.ShapeDtypeStruct((t, d), F32), jax.ShapeDtypeStruct((t, d), BF16),
                   jax.ShapeDtypeStruct((N_GROUPS, cg, cg), F32)]
        + [jax.ShapeDtypeStruct((1, d), F32)] * 3,
        scratch_shapes=[pltpu.VMEM((d // sw, ts + HALO, sw), F32), pltpu.VMEM((ts, d), F32)],
        compiler_params=_params("arbitrary"),
    )(pooled, wp, dr, dr, x, gain, scale, bias)


def _ctile(n, pref):
    return max(c for c in range(LANE, min(pref, n) + 1, LANE) if n % c == 0)


FFN_COLS = 1408
FFN_ROWS = 32
FFN_TILE = 1024


def _ffn_fwd(name, up, w, b, seq):
    _, t, dff = up.shape
    f = _ctile(dff, FFN_COLS)
    k_taps = w.shape[0]
    ts = _tile(seq, FFN_TILE)
    tps = seq // ts
    hb = ts // HALO16
    rc = _tile(ts, FFN_ROWS)

    def body(cur_ref, prev_ref, w_ref, b_ref, g_ref, apad):
        i = pl.program_id(1)
        first = (i % tps) == 0
        for ci, c0 in enumerate(range(0, f, LANE)):
            cols = slice(c0, c0 + LANE)
            apad[ci, 0:HALO16, :] = jnp.where(first, 0.0, prev_ref[:, cols].astype(F32))
            apad[ci, HALO16:HALO16 + ts, :] = cur_ref[0, :, cols].astype(F32)
            wv = w_ref[:, cols]
            wk = [jnp.broadcast_to(wv[k:k + 1, :], (rc, LANE)) for k in range(k_taps)]
            bias = jnp.broadcast_to(b_ref[:, cols], (rc, LANE))
            for r0 in range(0, ts, rc):
                c = bias
                for k in range(k_taps):
                    c = c + wk[k] * apad[ci, pl.ds(HALO16 - (k_taps - 1) + k + r0, rc), :]
                gate = cur_ref[1, pl.ds(r0, rc), cols].astype(F32)
                g_ref[pl.ds(r0, rc), cols] = (c * _sigmoid(c) * gate).astype(BF16)

    return pl.pallas_call(
        body, name=name, grid=(dff // f, t // ts),
        in_specs=[pl.BlockSpec((2, ts, f), lambda j, i: (0, i, j)),
                  pl.BlockSpec((None, HALO16, f), lambda j, i: (0, jnp.maximum(i * hb - 1, 0), j)),
                  pl.BlockSpec((k_taps, f), lambda j, i: (0, j)),
                  pl.BlockSpec((1, f), lambda j, i: (0, j))],
        out_specs=pl.BlockSpec((ts, f), lambda j, i: (i, j)),
        out_shape=jax.ShapeDtypeStruct((t, dff), BF16),
        scratch_shapes=[pltpu.VMEM((f // LANE, HALO16 + ts, LANE), F32)],
        compiler_params=_params("parallel", "parallel"),
    )(up, up, w, b)


def _ffn_bwd(name, up, dg, w, b, seq):
    _, t, dff = up.shape
    f = _ctile(dff, FFN_COLS)
    k_taps = w.shape[0]
    ts = _tile(seq, FFN_TILE)
    tps = seq // ts
    hb = ts // HALO16
    nhb = t // HALO16
    ext = ts + HALO16
    rc = _tile(ts, FFN_ROWS)

    def body(cur_ref, prev_ref, next_ref, dg_ref, ndg_ref, w_ref, b_ref, dup_ref, dw_ref, db_ref, apad, dcpad):
        i = pl.program_id(1)
        first = (i % tps) == 0
        last = (i % tps) == tps - 1

        @pl.when(i == 0)
        def _():
            dw_ref[...] = jnp.zeros_like(dw_ref)
            db_ref[...] = jnp.zeros_like(db_ref)

        for ci, c0 in enumerate(range(0, f, LANE)):
            cols = slice(c0, c0 + LANE)
            apad[ci, 0:HALO16, :] = jnp.where(first, 0.0, prev_ref[:, cols].astype(F32))
            apad[ci, HALO16:HALO16 + ts, :] = cur_ref[0, :, cols].astype(F32)
            apad[ci, HALO16 + ts:HALO16 + ext, :] = next_ref[0, :, cols].astype(F32)
            wv = w_ref[:, cols]
            wk = [jnp.broadcast_to(wv[k:k + 1, :], (rc, LANE)) for k in range(k_taps)]
            bias = jnp.broadcast_to(b_ref[:, cols], (rc, LANE))

            def conv_grad(r0, n, gate, dgv):
                c = bias[0:n]
                for k in range(k_taps):
                    c = c + wk[k][0:n] * apad[ci, pl.ds(HALO16 - (k_taps - 1) + k + r0, n), :]
                sig = _sigmoid(c)
                silu = c * sig
                return dgv * gate * (sig + silu * (1.0 - sig)), silu

            for r0 in range(0, ts, rc):
                dgv = dg_ref[pl.ds(r0, rc), cols].astype(F32)
                dc, silu = conv_grad(r0, rc, cur_ref[1, pl.ds(r0, rc), cols].astype(F32), dgv)
                dcpad[ci, pl.ds(r0, rc), :] = dc
                dup_ref[1, pl.ds(r0, rc), cols] = (dgv * silu).astype(BF16)
            dgv = jnp.where(last, 0.0, ndg_ref[:, cols].astype(F32))
            dc, _ = conv_grad(ts, HALO16, next_ref[1, :, cols].astype(F32), dgv)
            dcpad[ci, ts:ext, :] = dc

            dw_acc = [jnp.zeros((rc, LANE), F32) for _ in range(k_taps)]
            db_acc = jnp.zeros((rc, LANE), F32)
            for r0 in range(0, ts, rc):
                dact = jnp.zeros((rc, LANE), F32)
                for k in range(k_taps):
                    dact = dact + wk[k] * dcpad[ci, pl.ds(r0 + (k_taps - 1) - k, rc), :]
                dup_ref[0, pl.ds(r0, rc), cols] = dact.astype(BF16)
                dc = dcpad[ci, pl.ds(r0, rc), :]
                for k in range(k_taps):
                    dw_acc[k] = dw_acc[k] + dc * apad[ci, pl.ds(HALO16 - (k_taps - 1) + k + r0, rc), :]
                db_acc = db_acc + dc
            for k in range(k_taps):
                dw_ref[k:k + 1, cols] += jnp.sum(dw_acc[k], axis=0, keepdims=True)
            db_ref[:, cols] += jnp.sum(db_acc, axis=0, keepdims=True)

    return pl.pallas_call(
        body, name=name, grid=(dff // f, t // ts),
        in_specs=[pl.BlockSpec((2, ts, f), lambda j, i: (0, i, j)),
                  pl.BlockSpec((None, HALO16, f), lambda j, i: (0, jnp.maximum(i * hb - 1, 0), j)),
                  pl.BlockSpec((2, HALO16, f), lambda j, i: (0, jnp.minimum((i + 1) * hb, nhb - 1), j)),
                  pl.BlockSpec((ts, f), lambda j, i: (i, j)),
                  pl.BlockSpec((HALO16, f), lambda j, i: (jnp.minimum((i + 1) * hb, nhb - 1), j)),
                  pl.BlockSpec((k_taps, f), lambda j, i: (0, j)),
                  pl.BlockSpec((1, f), lambda j, i: (0, j))],
        out_specs=[pl.BlockSpec((2, ts, f), lambda j, i: (0, i, j)),
                   pl.BlockSpec((k_taps, f), lambda j, i: (0, j)),
                   pl.BlockSpec((1, f), lambda j, i: (0, j))],
        out_shape=[jax.ShapeDtypeStruct((2, t, dff), BF16), jax.ShapeDtypeStruct((k_taps, dff), F32),
                   jax.ShapeDtypeStruct((1, dff), F32)],
        scratch_shapes=[pltpu.VMEM((f // LANE, HALO16 + ext, LANE), F32), pltpu.VMEM((f // LANE, ext, LANE), F32)],
        compiler_params=_params("parallel", "arbitrary"),
    )(up, up, up, dg, dg, w, b)


def _sum_rows(name, g):
    ns, r, c = g.shape
    tr = _tile(r, 256)

    def body(g_ref, o_ref):
        acc = g_ref[0]
        for dev in range(1, ns):
            acc = acc + g_ref[dev]
        o_ref[...] = acc

    return pl.pallas_call(
        body, name=name, grid=(r // tr,),
        in_specs=[pl.BlockSpec((ns, tr, c), lambda i: (0, i, 0))],
        out_specs=pl.BlockSpec((tr, c), lambda i: (i, 0)),
        out_shape=jax.ShapeDtypeStruct((r, c), F32), compiler_params=_params("parallel"),
    )(g)


def _adamw(name, gsrc, w, m, v, layer=0, prev=None):
    ns, r, c = gsrc.shape
    nl = w.shape[0]
    tr = _tile(r, 256)
    prev = () if prev is None else tuple(prev)

    def body(g_ref, w_ref, m_ref, v_ref, *rest):
        go_ref, do_ref, mo_ref, vo_ref = rest[len(prev):]
        g = g_ref[0].astype(F32)
        for dev in range(1, ns):
            g = g + g_ref[dev].astype(F32)
        m_new = ADAM_B1 * m_ref[...] + (1.0 - ADAM_B1) * g
        v_new = ADAM_B2 * v_ref[...] + (1.0 - ADAM_B2) * (g * g)
        m_hat = m_new / (1.0 - ADAM_B1 ** ADAM_STEP)
        v_hat = v_new / (1.0 - ADAM_B2 ** ADAM_STEP)
        go_ref[...] = g
        do_ref[...] = -ADAM_LR * (m_hat / (jnp.sqrt(v_hat) + ADAM_EPS) + ADAM_WD * w_ref[...])
        mo_ref[...] = m_new
        vo_ref[...] = v_new

    row = pl.BlockSpec((None, tr, c), lambda i: (layer, i, 0))
    return pl.pallas_call(
        body, name=name, grid=(r // tr,),
        in_specs=[pl.BlockSpec((ns, tr, c), lambda i: (0, i, 0)), row, row, row] + [ANY] * len(prev),
        out_specs=[row] * 4, out_shape=[jax.ShapeDtypeStruct((nl, r, c), F32)] * 4,
        input_output_aliases={4 + i: i for i in range(len(prev))},
        compiler_params=_params("parallel"),
    )(gsrc, w, m, v, *prev)


def _ffn_forward(tag, r_in, h, get_wu, get_wd, wdw, bdw, seq, loss=None):
    t, d = r_in.shape
    tm = _tile(t, 512)
    wu = get_wu(h)
    dff = wu.shape[0] // 2
    tu = _tile(t, 1024)
    up = _mm(f"{tag}_up", h, wu, grid=(2, t // tu, 1),
             a_spec=pl.BlockSpec((tu, d), lambda j, i, k: (i, 0)),
             b_spec=pl.BlockSpec((dff, d), lambda j, i, k: (j, 0)),
             out_spec=pl.BlockSpec((None, tu, dff), lambda j, i, k: (j, i, 0)),
             out_shape=jax.ShapeDtypeStruct((2, t, dff), BF16), dims=NT, acc_shape=(tu, dff))
    wd = get_wd(up)
    g = _ffn_fwd(f"{tag}_act", up, wdw, bdw, seq)
    row = pl.BlockSpec((tm, d), lambda i, j, k: (i, 0))
    vec = pl.BlockSpec((1, d), lambda i, j, k: (0, 0))
    common = dict(grid=(t // tm, 1, 1), a_spec=pl.BlockSpec((tm, dff), lambda i, j, k: (i, 0)),
                  b_spec=pl.BlockSpec((dff, d), lambda i, j, k: (0, 0)), dims=NN, acc_shape=(tm, d))
    if loss is None:
        out = _mm(f"{tag}_down", g, wd, out_spec=row, out_shape=jax.ShapeDtypeStruct((t, d), F32),
                  extras=(r_in,), extra_specs=(row,), epilogue=lambda acc, ex, rows: ((ex[0][rows, :] + acc,), ()),
                  **common)
    else:
        def head(acc, ex, rows):
            dx, part, dgain = _loss_tile(ex[0][rows, :] + acc, ex[1][rows, :], ex[2][...])
            return (dx, dx), (part, dgain)

        out = _mm(f"{tag}_down", g, wd, out_spec=[row, row, pl.BlockSpec((1, 1), lambda i, j, k: (0, 0)), vec],
                  out_shape=[jax.ShapeDtypeStruct((t, d), F32), jax.ShapeDtypeStruct((t, d), BF16),
                             jax.ShapeDtypeStruct((1, 1), F32), jax.ShapeDtypeStruct((1, d), F32)],
                  extras=(r_in, *loss), extra_specs=(row, row, vec), epilogue=head, n_sums=2, **common)
    return out, (r_in, h, up, g, wu, wd)


def _ffn_backward(tag, dr, dr16, saved, gain, wdw, bdw, seq, token=None, send_dwd=None):
    r_in, h, up, g, wu, wd = saved
    t, d = r_in.shape
    dff = wd.shape[0]
    tm = _tile(t, 512)
    tk = _tile(t, 2048)
    tku = _tile(t, 4096)
    cw = _ctile(dff, 1408)
    nc = dff // cw
    once = dict(pipeline_mode=pl.Buffered(1)) if tku == t else {}
    dg = _mm(f"{tag}_dg", dr16, wd, grid=(t // tm, 1, 1),
             a_spec=pl.BlockSpec((tm, d), lambda i, j, k: (i, 0)),
             b_spec=pl.BlockSpec((dff, d), lambda i, j, k: (0, 0)),
             out_spec=pl.BlockSpec((tm, dff), lambda i, j, k: (i, 0)),
             out_shape=jax.ShapeDtypeStruct((t, dff), BF16), dims=NT, acc_shape=(tm, dff), token=token)
    dwd = _mm(f"{tag}_dwd", g, dr16, grid=(dff // cw, 1, t // tk),
              a_spec=pl.BlockSpec((tk, cw), lambda i, j, k: (k, i)),
              b_spec=pl.BlockSpec((tk, d), lambda i, j, k: (k, 0)),
              out_spec=pl.BlockSpec((cw, d), lambda i, j, k: (i, 0)),
              out_shape=jax.ShapeDtypeStruct((dff, d), BF16), dims=TN, acc_shape=(cw, d))
    sent = None if send_dwd is None else send_dwd(dwd)
    dup, dwdw, dbdw = _ffn_bwd(f"{tag}_dact", up, dg, wdw, bdw, seq)
    row = pl.BlockSpec((tm, d), lambda i, j, k: (i, 0))
    vec = pl.BlockSpec((1, d), lambda i, j, k: (0, 0))

    def norm_backward(acc, ex, rows):
        dx, dgain, colsum = _rms_bwd_tile(acc, ex[0][rows, :], ex[1][...], ex[2][rows, :])
        return (dx, dx), (dgain, colsum)

    dr_in, dr_in16, dgain, colsum = _mm(
        f"{tag}_dh", dup, wu, grid=(t // tm, 1, 1),
        a_spec=pl.BlockSpec((2, tm, dff), lambda i, j, k: (0, i, 0)),
        b_spec=pl.BlockSpec((2 * dff, d), lambda i, j, k: (0, 0), pipeline_mode=pl.Buffered(1)),
        out_spec=[row, row, vec, vec],
        out_shape=[jax.ShapeDtypeStruct((t, d), F32), jax.ShapeDtypeStruct((t, d), BF16)]
        + [jax.ShapeDtypeStruct((1, d), F32)] * 2,
        dims=NN, acc_shape=(tm, d), extras=(r_in, gain, dr), extra_specs=(row, vec, row),
        epilogue=norm_backward, n_sums=2, parts=2, token=sent)
    dwu = _mm(f"{tag}_dwu", dup, h, grid=(2 * nc, 1, t // tku),
              a_spec=pl.BlockSpec((None, tku, cw), lambda i, j, k: (i // nc, k, i % nc)),
              b_spec=pl.BlockSpec((tku, d), lambda i, j, k: (k, 0), **once),
              out_spec=pl.BlockSpec((cw, d), lambda i, j, k: (i, 0)),
              out_shape=jax.ShapeDtypeStruct((2 * dff, d), BF16), dims=TN, acc_shape=(cw, d))
    return (dr_in, dr_in16), dgain, dwu, dwd, dwdw, dbdw, colsum


def _pad_to(vec, n):
    return jnp.pad(vec, (0, n - vec.shape[0]))


def _pack(parts, width):
    flat = jnp.concatenate([p.reshape(-1).astype(F32) for p in parts])
    n = -(-flat.shape[0] // (8 * width)) * (8 * width)
    return _pad_to(flat, n).reshape(n // width, width)


def _unpack(mat, shapes):
    flat = mat.reshape(-1)
    out, off = [], 0
    for s in shapes:
        n = 1
        for dim in s:
            n *= dim
        out.append(flat[off:off + n].reshape(s))
        off += n
    return out


def kernel(x, norm_mix, norm_ffn, conv_w_pw1, conv_b_pw1, conv_w_dw, conv_b_dw, conv_ln_g, conv_ln_b, conv_w_pw2, conv_b_pw2, pool_w, pool_b, pool_scale, ffn_w_up, ffn_w_dw, ffn_b_dw, ffn_w_down, final_norm, loss_target, m_norm_mix, m_norm_ffn, m_conv_w_pw1, m_conv_b_pw1, m_conv_w_dw, m_conv_b_dw, m_conv_ln_g, m_conv_ln_b, m_conv_w_pw2, m_conv_b_pw2, m_pool_w, m_pool_b, m_pool_scale, m_ffn_w_up, m_ffn_w_dw, m_ffn_b_dw, m_ffn_w_down, m_final_norm, v_norm_mix, v_norm_ffn, v_conv_w_pw1, v_conv_b_pw1, v_conv_w_dw, v_conv_b_dw, v_conv_ln_g, v_conv_ln_b, v_conv_w_pw2, v_conv_b_pw2, v_pool_w, v_pool_b, v_pool_scale, v_ffn_w_up, v_ffn_w_dw, v_ffn_b_dw, v_ffn_w_down, v_final_norm):
    bsz, seq, d = x.shape
    t = bsz * seq
    k_taps = conv_w_dw.shape[1]
    cs1 = conv_w_pw1.shape[2]
    dsh = d // N_DEV
    cg = d // N_GROUPS
    cgs = pool_w.shape[2]
    fu = ffn_w_up.shape[2]
    fd = ffn_w_down.shape[1]
    dff = fd * N_DEV
    nb = N_DEV // 2
    kf = ffn_w_dw.shape[1]
    fsh = ffn_w_dw.shape[2]
    my = _lin(_me())
    tm = _tile(t, 512)

    x2 = x.reshape(t, d)
    tgt2 = loss_target.reshape(t, d)

    small_shapes = [(k_taps, dsh), (dsh,), (dsh,), (2, kf, fsh)]
    small_mine = _pack([conv_w_dw[0], pool_b[0], pool_scale[0], ffn_w_dw], LANE)
    big = [conv_w_pw1[0], conv_w_pw2[0], ffn_w_up[0].T, ffn_w_down[0], pool_w[0], ffn_w_up[1].T, ffn_w_down[1]]
    gather = _exchange_start("gather_start", [small_mine] + [w.astype(BF16) for w in big], [GATHER2] * 8)
    h0 = _rms_fwd("l0_rms", x2, norm_mix[0:1])
    small_w = [norm_mix, norm_ffn, conv_b_pw1, conv_w_dw, conv_b_dw, conv_ln_g, conv_ln_b, conv_b_pw2, pool_b,
               pool_scale, ffn_w_dw, ffn_b_dw, final_norm]
    small_m = [m_norm_mix, m_norm_ffn, m_conv_b_pw1, m_conv_w_dw, m_conv_b_dw, m_conv_ln_g, m_conv_ln_b,
               m_conv_b_pw2, m_pool_b, m_pool_scale, m_ffn_w_dw, m_ffn_b_dw, m_final_norm]
    small_v = [v_norm_mix, v_norm_ffn, v_conv_b_pw1, v_conv_w_dw, v_conv_b_dw, v_conv_ln_g, v_conv_ln_b,
               v_conv_b_pw2, v_pool_b, v_pool_scale, v_ffn_w_dw, v_ffn_b_dw, v_final_norm]
    small_state = [_pack(group, 8 * LANE)[None] for group in (small_w, small_m, small_v)]
    forwarded = _exchange_forward("gather_forward_w1", gather, [0, 1], (h0, *small_state))
    small_all, w1 = _exchange_wait("gather_wait_w1", gather, [0, 1], forwarded)
    parts = [_unpack(small_all[dev], small_shapes) for dev in range(N_DEV)]
    wdw = jnp.concatenate([p[0] for p in parts], axis=1)
    pool_b_full = jnp.concatenate([p[1] for p in parts]).reshape(1, d)
    pool_s_full = jnp.concatenate([p[2] for p in parts]).reshape(1, d)
    fwdw = jnp.concatenate([p[3] for p in parts], axis=2)
    fbdw = ffn_b_dw.reshape(2, 1, dff)

    def columns(w):
        return w.transpose(1, 0, 2).reshape(w.shape[1], N_DEV * w.shape[2])

    def column_shards(w):
        return w.reshape(w.shape[0], N_DEV, w.shape[1] // N_DEV).transpose(1, 0, 2)

    w1 = columns(w1)
    a = _mm("l0_pw1", h0, w1, grid=(2, t // tm, 1),
            a_spec=pl.BlockSpec((tm, d), lambda j, i, k: (i, 0)),
            b_spec=pl.BlockSpec((d, d), lambda j, i, k: (0, j)),
            out_spec=pl.BlockSpec((None, tm, d), lambda j, i, k: (j, i, 0)),
            out_shape=jax.ShapeDtypeStruct((2, t, d), BF16), dims=NN, acc_shape=(tm, d),
            extras=(conv_b_pw1,), extra_specs=(pl.BlockSpec((1, d), lambda j, i, k: (0, j)),),
            epilogue=lambda acc, ex, rows: ((acc + ex[0][...],), ()))
    v = _conv_fwd("l0_conv", a, wdw, conv_b_dw, seq)
    forwarded = _exchange_forward("gather_forward_wu0", gather, [2, 3], v)
    (w2,) = _exchange_wait("gather_wait_w2", gather, [2], forwarded)
    w2 = w2.reshape(d, d)
    row = pl.BlockSpec((tm, d), lambda i, j, k: (i, 0))
    vec = pl.BlockSpec((1, d), lambda i, j, k: (0, 0))
    square = pl.BlockSpec((d, d), lambda i, j, k: (0, 0))

    def ln_silu(v_blk, ex):
        s_blk = _ln_silu_tile(v_blk, ex[0][...], ex[1][...]).astype(BF16)
        return s_blk, s_blk

    def residual_and_norm(acc, ex, rows):
        r_blk = ex[3][rows, :] + (acc + ex[2][...])
        return (r_blk, _rms(r_blk, ex[4][...])), ()

    r1, h1, s = _mm("l0_pw2", v, w2, grid=(t // tm, 1, 1), a_spec=row, b_spec=square, out_spec=[row, row, row],
                    out_shape=[jax.ShapeDtypeStruct((t, d), F32), jax.ShapeDtypeStruct((t, d), BF16),
                               jax.ShapeDtypeStruct((t, d), BF16)],
                    dims=NN, acc_shape=(tm, d), extras=(conv_ln_g, conv_ln_b, conv_b_pw2, x2, norm_ffn[0:1]),
                    extra_specs=(vec, vec, vec, row, vec), prologue=ln_silu, epilogue=residual_and_norm)

    def up_getter(name, idx):
        return lambda after: _exchange_wait(name, gather, [idx], after)[0].reshape(2 * dff, d)

    def down_getter(name, idx, forward=None):
        def get(after):
            if forward is not None:
                after = _exchange_forward(forward[0], gather, forward[1], after)
            return _exchange_wait(name, gather, [idx], after)[0].reshape(dff, d)
        return get

    r2, ffn0_saved = _ffn_forward("f0", r1, h1, up_getter("gather_wait_wu0", 3),
                                  down_getter("gather_wait_wd0", 4, ("gather_forward_wu1", [4, 5, 6])),
                                  fwdw[0], fbdw[0], seq)
    forwarded = _exchange_forward("gather_forward_wd1", gather, [7], r2)
    (wp,) = _exchange_wait("gather_wait_wp", gather, [5], forwarded)
    wp = wp.transpose(1, 0, 2, 3).reshape(N_GROUPS, cg, cg)
    pooled, r3, h3 = _pool_mix_fwd("l1_mix", r2, norm_mix[1:2], wp, pool_s_full, pool_b_full, norm_ffn[1:2], seq)
    (dr4, dr4_16, loss_part, dfinal), ffn1_saved = _ffn_forward(
        "f1", r3, h3, up_getter("gather_wait_wu1", 6), down_getter("gather_wait_wd1", 7), fwdw[1], fbdw[1], seq,
        loss=(tgt2, final_norm.reshape(1, d)))

    (dr3, _), dnf1, dwu1, dwd1, dfw1, dfb1, _ = _ffn_backward("f1", dr4, dr4_16, ffn1_saved, norm_ffn[1:2], fwdw[1],
                                                              fbdw[1], seq)
    scatter_a = _exchange_start("scatter_f1_start", [dwu1.reshape(N_DEV, fu, d), dwd1.reshape(N_DEV, fd, d)],
                                [SCATTER, SCATTER])
    dr2, dr2_16, dwp, dpool_s, dpool_b, dnm1 = _pool_mix_bwd(
        "l1_dmix", pooled, wp, dr3, r2, norm_mix[1:2], pool_s_full + scatter_a["token"][0:1, 0:1], pool_b_full, seq)
    early = {}

    def send_dwd0(dwd):
        early["down"] = _exchange_start("scatter_f0_down_start", [dwd.reshape(N_DEV, fd, d)], [SCATTER])
        return early["down"]["token"]

    (dr1, dr1_16), dnf0, dwu0, dwd0, dfw0, dfb0, db2 = _ffn_backward(
        "f0", dr2, dr2_16, ffn0_saved, norm_ffn[0:1], fwdw[0], fbdw[0], seq, send_dwd=send_dwd0)
    dwp_b = dwp.astype(BF16).reshape(N_GROUPS, N_DEV, cgs, cg).transpose(1, 0, 2, 3)
    tk = _tile(t, 2048)
    dw2 = _mm("l0_dw2", s, dr1_16, grid=(1, 1, t // tk),
              a_spec=pl.BlockSpec((tk, d), lambda i, j, k: (k, 0)),
              b_spec=pl.BlockSpec((tk, d), lambda i, j, k: (k, 0)),
              out_spec=pl.BlockSpec((d, d), lambda i, j, k: (0, 0)),
              out_shape=jax.ShapeDtypeStruct((d, d), BF16), dims=TN, acc_shape=(d, d))
    scatter_b = _exchange_start("scatter_f0_start", [dwu0.reshape(N_DEV, fu, d), dwp_b,
                                                     dw2.reshape(N_DEV, d // N_DEV, d)], [SCATTER] * 3)

    def ln_silu_backward(acc, ex, rows):
        dv_blk, dgain, dbias, colsum = _ln_silu_bwd_tile(acc, ex[0][rows, :], ex[1][...], ex[2][...])
        return (dv_blk,), (dgain, dbias, colsum)

    dv, dlg, dlb, dbdw = _mm("l0_ds", dr1_16, w2, grid=(t // tm, 1, 1), a_spec=row, b_spec=square,
                             out_spec=[row, vec, vec, vec],
                             out_shape=[jax.ShapeDtypeStruct((t, d), F32)] + [jax.ShapeDtypeStruct((1, d), F32)] * 3,
                             dims=NT, acc_shape=(tm, d), extras=(v, conv_ln_g, conv_ln_b), extra_specs=(row, vec, vec),
                             epilogue=ln_silu_backward, n_sums=3, token=scatter_b["token"])
    da, dwdw, db1 = _conv_bwd("l0_dconv", a, dv, wdw, seq)
    tk1 = _tile(t, 4096)
    once = dict(pipeline_mode=pl.Buffered(1)) if tk1 == t else {}
    dw1 = _mm("l0_dw1", h0, da, grid=(1, 2, t // tk1),
              a_spec=pl.BlockSpec((tk1, d), lambda i, j, k: (k, 0), **once),
              b_spec=pl.BlockSpec((None, tk1, d), lambda i, j, k: (j, k, 0)),
              out_spec=pl.BlockSpec((d, d), lambda i, j, k: (0, j)),
              out_shape=jax.ShapeDtypeStruct((d, 2 * d), BF16), dims=TN, acc_shape=(d, d))
    scatter_c = _exchange_start("scatter_l0_start", [column_shards(dw1)], [SCATTER])
    def norm_backward(acc, ex, rows):
        dx_blk, dgain, colsum = _rms_bwd_tile(acc, ex[0][rows, :], ex[1][...], ex[2][rows, :])
        return (dx_blk,), (dgain, colsum)

    dx, dnm0, _ = _mm("l0_dh", da, w1, grid=(t // tm, 1, 1),
                      a_spec=pl.BlockSpec((2, tm, d), lambda i, j, k: (0, i, 0)),
                      b_spec=pl.BlockSpec((d, 2 * d), lambda i, j, k: (0, 0), pipeline_mode=pl.Buffered(1)),
                      out_spec=[row, vec, vec],
                      out_shape=[jax.ShapeDtypeStruct((t, d), F32)] + [jax.ShapeDtypeStruct((1, d), F32)] * 2,
                      dims=NT, acc_shape=(tm, d), extras=(x2, norm_mix[0:1], dr1), extra_specs=(row, vec, row),
                      epilogue=norm_backward, n_sums=2, parts=2, token=scatter_c["token"])

    dffn_w = jnp.stack([dfw0, dfw1])
    dffn_b = jnp.stack([dfb0, dfb1]).reshape(2, dff)
    small_parts = [loss_part, jnp.concatenate([dnm0, dnm1]), jnp.concatenate([dnf0, dnf1]), db1, dwdw, dbdw, dlg, dlb,
                   db2, dpool_b, dpool_s, dffn_w, dffn_b, dfinal]
    small_part_shapes = [(1,), (2, d), (2, d), (1, 2 * d), (k_taps, d), (1, d), (1, d), (1, d), (1, d), (1, d), (1, d),
                         (2, kf, dff), (2, dff), (d,)]
    packed = _pack(small_parts, 8 * LANE)
    gather_small = _exchange_start("gather_small_start", [packed], [GATHER])

    def big_update(name, recv, w, m, v, layer=0, prev=None):
        shape = w.shape
        c = recv.shape[-1]
        rows = recv.size // (N_DEV * c)
        nl = w.size // (rows * c)
        outs = _adamw(name, recv.reshape(N_DEV, rows, c), w.reshape(nl, rows, c), m.reshape(nl, rows, c),
                      v.reshape(nl, rows, c), layer, prev)
        return outs, [o.reshape(shape) for o in outs]

    wu_t = [p.transpose(0, 2, 1) for p in (ffn_w_up, m_ffn_w_up, v_ffn_w_up)]
    g_wu1, g_wd1 = _exchange_wait("scatter_f1_wait", scatter_a, [0, 1], gather_small["token"])
    raw_wu, _ = big_update("adam_wu1", g_wu1, *wu_t, 1)
    raw_wd, _ = big_update("adam_wd1", g_wd1, ffn_w_down, m_ffn_w_down, v_ffn_w_down, 1)
    (g_wd0,) = _exchange_wait("scatter_f0_down_wait", early["down"], [0], raw_wd[0])
    g_wu0, g_wp, g_w2 = _exchange_wait("scatter_f0_wait", scatter_b, [0, 1, 2], g_wd0)
    _, u_wu = big_update("adam_wu0", g_wu0, *wu_t, 0, raw_wu)
    u_wu = [o.transpose(0, 2, 1) for o in u_wu]
    _, u_wd = big_update("adam_wd0", g_wd0, ffn_w_down, m_ffn_w_down, v_ffn_w_down, 0, raw_wd)
    _, u_wp = big_update("adam_wp", g_wp, pool_w, m_pool_w, v_pool_w)
    (g_w1,) = _exchange_wait("scatter_l0_wait", scatter_c, [0], u_wp[0])
    _, u_w1 = big_update("adam_w1", g_w1, conv_w_pw1, m_conv_w_pw1, v_conv_w_pw1)
    _, u_w2 = big_update("adam_w2", g_w2, conv_w_pw2, m_conv_w_pw2, v_conv_w_pw2)
    (all_small,) = _exchange_wait("gather_small_wait", gather_small, [0], u_w2[0])
    summed = _sum_rows("sum_small_grads", all_small)
    (loss_v, g_nm, g_nf, g_b1, g_wdw, g_bdw, g_lg, g_lb, g_b2, g_pb, g_ps, g_fw, g_fb,
     g_fin) = _unpack(summed, small_part_shapes)
    loss = loss_v[0]
    g_wdw_mine = lax.dynamic_slice_in_dim(g_wdw, my * dsh, dsh, axis=1)[None]
    g_pb_mine = lax.dynamic_slice_in_dim(g_pb, my * dsh, dsh, axis=1)
    g_ps_mine = lax.dynamic_slice_in_dim(g_ps, my * dsh, dsh, axis=1)
    g_fw_mine = lax.dynamic_slice_in_dim(g_fw, my * fsh, fsh, axis=2)

    small_g = [g_nm, g_nf, g_b1, g_wdw_mine, g_bdw, g_lg, g_lb, g_b2, g_pb_mine, g_ps_mine, g_fw_mine, g_fb, g_fin]
    shapes = [w.shape for w in small_w]
    outs = _adamw("adam_small", _pack(small_g, 8 * LANE)[None], *small_state)
    sg, sd, sm, sv = [_unpack(o, shapes) for o in outs]

    def leaf(kind):
        (nm, nf, b1, wdw_, bdw_, lg, lb, b2, pb, ps, fw, fb, fin) = (sg, sd, sm, sv)[kind]
        return [nm, nf, u_w1[kind], b1, wdw_, bdw_, lg, lb, u_w2[kind], b2, u_wp[kind], pb, ps, u_wu[kind], fw, fb,
                u_wd[kind], fin]

    return (loss, dx.reshape(bsz, seq, d), *leaf(0), *leaf(1), *leaf(2), *leaf(3))
```
